```python
import jax, jax.numpy as jnp
from jax import lax
import numpy as np

D_MODEL = 2048
BATCH = 8
SEQ = 8192
DEPTH = 2

GRID_W = 64
CTX_LEN = 256

HEAD_DIM = 128
N_Q_HEADS = 8
N_KV_HEADS = 2
Q_PER_KV = N_Q_HEADS // N_KV_HEADS
ATTN_W = N_Q_HEADS * HEAD_DIM
KV_W = N_KV_HEADS * HEAD_DIM
Q_BLOCK = 128
ATTN_SCALE = HEAD_DIM ** -0.5
ROPE_THETA = 10000.0
AXIS_DIM = HEAD_DIM // 2
N_FREQ = AXIS_DIM // 2

CONV_CH = 1024
CONV_WIDTH = 31

CHUNK = 128
SGU_W = D_MODEL
SGU_GROUPS = 8
SGU_GW = SGU_W // SGU_GROUPS

EV_SPLITS = [KV_W, 2 * KV_W, 2 * KV_W + ATTN_W, 2 * KV_W + 2 * ATTN_W,
             2 * KV_W + 2 * ATTN_W + 2 * CONV_CH]
EV_IN = 2 * KV_W + 2 * ATTN_W + 3 * CONV_CH
EV_MIX = ATTN_W + CONV_CH
OD_IN = 3 * SGU_W

N_EVEN = (DEPTH + 1) // 2
N_ODD = DEPTH // 2
EPS = 1e-6

kernel_name = "hybrid_gqa_conformer_sgu_prefix_block"


def rmsnorm(x, g):
    xf = x.astype(jnp.float32)
    y = xf * lax.rsqrt(jnp.mean(xf * xf, axis=-1, keepdims=True) + EPS)
    return (y * g.astype(jnp.float32)).astype(x.dtype)


def layernorm(x, g, b):
    xf = x.astype(jnp.float32)
    mu = jnp.mean(xf, axis=-1, keepdims=True)
    var = jnp.mean(jnp.square(xf - mu), axis=-1, keepdims=True)
    y = (xf - mu) * lax.rsqrt(var + EPS)
    return (y * g.astype(jnp.float32) + b.astype(jnp.float32)).astype(x.dtype)


def modulate(h, shift, scale):
    return h * (1.0 + scale) + shift


def axial_rope_tables(n):
    rows = n // GRID_W
    row = jnp.repeat(jnp.arange(rows, dtype=jnp.float32), GRID_W)
    col = jnp.tile(jnp.arange(GRID_W, dtype=jnp.float32), rows)
    inv = jnp.power(ROPE_THETA, jnp.arange(N_FREQ, dtype=jnp.float32) * (-2.0 / AXIS_DIM))
    ang = jnp.concatenate([row[:, None] * inv, col[:, None] * inv], axis=-1)
    return jnp.cos(ang), jnp.sin(ang)


def apply_rope(x, cos, sin):
    shp = x.shape
    xf = x.astype(jnp.float32).reshape(shp[:-1] + (HEAD_DIM // 2, 2))
    x1, x2 = xf[..., 0], xf[..., 1]
    cs, sn = cos[None, :, None, :], sin[None, :, None, :]
    out = jnp.stack([x1 * cs - x2 * sn, x1 * sn + x2 * cs], axis=-1)
    return out.reshape(shp).astype(x.dtype)


def latent_attention(q, k_lat, v_lat, k_ctx, v_ctx):
    bsz, n = q.shape[:2]
    k_all = jnp.concatenate([k_lat, k_ctx], axis=1)
    v_all = jnp.concatenate([v_lat, v_ctx], axis=1)
    nb = n // Q_BLOCK
    qb = q.reshape(bsz, nb, Q_BLOCK, N_KV_HEADS, Q_PER_KV, HEAD_DIM).transpose(1, 0, 2, 3, 4, 5)

    def block(qi):
        s = jnp.einsum('bqhgd,bkhd->bhgqk', qi, k_all).astype(jnp.float32) * ATTN_SCALE
        p = jax.nn.softmax(s, axis=-1)
        return jnp.einsum('bhgqk,bkhd->bqhgd', p.astype(v_all.dtype), v_all)

    out = lax.map(block, qb)
    return out.transpose(1, 0, 2, 3, 4, 5).reshape(bsz, n, ATTN_W)


def context_attention(q, k, v):
    bsz, l = q.shape[:2]
    qg = q.reshape(bsz, l, N_KV_HEADS, Q_PER_KV, HEAD_DIM)
    s = jnp.einsum('bqhgd,bkhd->bhgqk', qg, k).astype(jnp.float32) * ATTN_SCALE
    p = jax.nn.softmax(s, axis=-1)
    o = jnp.einsum('bhgqk,bkhd->bqhgd', p.astype(v.dtype), v)
    return o.reshape(bsz, l, ATTN_W)


def conformer_conv(pair, dw_w, dw_b, ln_g, ln_b):
    a, b = jnp.split(pair, 2, axis=-1)
    y = a * jax.nn.sigmoid(b)
    y = lax.conv_general_dilated(
        y, dw_w[:, None, :], window_strides=(1,),
        padding=[(CONV_WIDTH // 2, CONV_WIDTH // 2)],
        dimension_numbers=('NWC', 'WIO', 'NWC'),
        feature_group_count=CONV_CH) + dw_b
    return jax.nn.silu(layernorm(y, ln_g, ln_b))


def spatial_gating(uv, ln_g, ln_b, ws, bs):
    bsz, n = uv.shape[:2]
    u, v = jnp.split(jax.nn.gelu(uv), 2, axis=-1)
    v = layernorm(v, ln_g, ln_b).reshape(bsz, n // CHUNK, CHUNK, SGU_GROUPS, SGU_GW)
    mixed = jnp.einsum('gij,bcjgd->bcigd', ws, v) + bs.T[None, None, :, :, None]
    return u * mixed.reshape(bsz, n, SGU_W)


def _fwd_setup_inputs(seed: int = 0) -> dict:
    key = jax.random.key(seed)
    ks = iter(jax.random.split(key, 32))

    def nrm(shape, scale):
        return jax.random.normal(next(ks), shape, jnp.float32) * scale

    d = D_MODEL
    return {
        "x": nrm((BATCH, SEQ, d), 1.0),
        "c": nrm((BATCH, d), 1.0),
        "ctx": nrm((BATCH, CTX_LEN, d), 1.0),
        "c_ctx": nrm((d,), 1.0),
        "ada_w": nrm((DEPTH, d, 3 * d), 0.5 * d ** -0.5),
        "ada_b": nrm((DEPTH, 3 * d), 0.02),
        "norm_g": 1.0 + nrm((DEPTH, d), 0.02),
        "ev_w_in": nrm((N_EVEN, d, EV_IN), d ** -0.5),
        "ev_q_norm": 1.0 + nrm((N_EVEN, HEAD_DIM), 0.02),
        "ev_k_norm": 1.0 + nrm((N_EVEN, HEAD_DIM), 0.02),
        "ev_dw_w": nrm((N_EVEN, CONV_WIDTH, CONV_CH), CONV_WIDTH ** -0.5),
        "ev_dw_b": nrm((N_EVEN, CONV_CH), 0.02),
        "ev_ln_g": 1.0 + nrm((N_EVEN, CONV_CH), 0.02),
        "ev_ln_b": nrm((N_EVEN, CONV_CH), 0.02),
        "ev_w_out": nrm((N_EVEN, EV_MIX, d), EV_MIX ** -0.5),
        "od_w_in": nrm((N_ODD, d, OD_IN), d ** -0.5),
        "od_ln_g": 1.0 + nrm((N_ODD, SGU_W), 0.02),
        "od_ln_b": nrm((N_ODD, SGU_W), 0.02),
        "od_ws": nrm((N_ODD, SGU_GROUPS, CHUNK, CHUNK), CHUNK ** -0.5),
        "od_bs": 1.0 + nrm((N_ODD, SGU_GROUPS, CHUNK), 0.02),
        "od_w_out": nrm((N_ODD, SGU_W, d), SGU_W ** -0.5),
        "final_g": 1.0 + nrm((d,), 0.02),
    }


def _fwd_reference(x, c, ctx, c_ctx, ada_w, ada_b, norm_g, ev_w_in, ev_q_norm, ev_k_norm,
              ev_dw_w, ev_dw_b, ev_ln_g, ev_ln_b, ev_w_out, od_w_in, od_ln_g, od_ln_b,
              od_ws, od_bs, od_w_out, final_g):
    bsz, n, _ = x.shape
    cos, sin = axial_rope_tables(n)
    sc = jax.nn.silu(c)
    scc = jax.nn.silu(c_ctx)
    xc = ctx
    lc = ctx.shape[1]
    for layer in range(DEPTH):
        ctx_needed = any(j % 2 == 0 for j in range(layer + 1, DEPTH))
        is_even = layer % 2 == 0
        mod = sc @ ada_w[layer] + ada_b[layer]
        shift, scale, gate = jnp.split(mod[:, None, :], 3, axis=-1)
        h = modulate(rmsnorm(x, norm_g[layer]), shift, scale)
        if is_even or ctx_needed:
            n_mod = 3 * D_MODEL if ctx_needed else 2 * D_MODEL
            mod_c = (scc @ ada_w[layer][:, :n_mod] + ada_b[layer][:n_mod])[None, None, :]
            hc = modulate(rmsnorm(xc, norm_g[layer]),
                          mod_c[..., :D_MODEL], mod_c[..., D_MODEL:2 * D_MODEL])
        if is_even:
            e = layer // 2
            w_in = ev_w_in[e]
            k, v, q, za, glu, zb = jnp.split(h @ w_in, EV_SPLITS, axis=-1)
            q = apply_rope(rmsnorm(q.reshape(bsz, n, N_Q_HEADS, HEAD_DIM), ev_q_norm[e]), cos, sin)
            k = apply_rope(rmsnorm(k.reshape(bsz, n, N_KV_HEADS, HEAD_DIM), ev_k_norm[e]), cos, sin)
            v = v.reshape(bsz, n, N_KV_HEADS, HEAD_DIM)
            if ctx_needed:
                kc, vc, qc, zac, gluc, zbc = jnp.split(hc @ w_in, EV_SPLITS, axis=-1)
            else:
                kc, vc = jnp.split(hc @ w_in[:, :2 * KV_W], 2, axis=-1)
            kc = rmsnorm(kc.reshape(bsz, lc, N_KV_HEADS, HEAD_DIM), ev_k_norm[e])
            vc = vc.reshape(bsz, lc, N_KV_HEADS, HEAD_DIM)
            attn = latent_attention(q, k, v, kc, vc)
            conv = conformer_conv(glu, ev_dw_w[e], ev_dw_b[e], ev_ln_g[e], ev_ln_b[e])
            mix = jnp.concatenate([attn * jax.nn.silu(za), conv * jax.nn.silu(zb)], axis=-1)
            x_new = x + gate * (mix @ ev_w_out[e])
            if ctx_needed:
                qc = rmsnorm(qc.reshape(bsz, lc, N_Q_HEADS, HEAD_DIM), ev_q_norm[e])
                attn_c = context_attention(qc, kc, vc)
                conv_c = conformer_conv(gluc, ev_dw_w[e], ev_dw_b[e], ev_ln_g[e], ev_ln_b[e])
                mix_c = jnp.concatenate([attn_c * jax.nn.silu(zac), conv_c * jax.nn.silu(zbc)], axis=-1)
                xc = xc + mod_c[..., 2 * D_MODEL:] * (mix_c @ ev_w_out[e])
            x = x_new
        else:
            o = layer // 2
            p = h @ od_w_in[o]
            mixed = spatial_gating(p[..., :2 * SGU_W], od_ln_g[o], od_ln_b[o], od_ws[o], od_bs[o])
            mixed = mixed * jax.nn.silu(p[..., 2 * SGU_W:])
            x_new = x + gate * (mixed @ od_w_out[o])
            if ctx_needed:
                pc = hc @ od_w_in[o]
                mixed_c = spatial_gating(pc[..., :2 * SGU_W], od_ln_g[o], od_ln_b[o], od_ws[o], od_bs[o])
                mixed_c = mixed_c * jax.nn.silu(pc[..., 2 * SGU_W:])
                xc = xc + mod_c[..., 2 * D_MODEL:] * (mixed_c @ od_w_out[o])
            x = x_new
    return rmsnorm(x, final_g)


import jax as _jax
import jax.numpy as _jnp

TWIN_FORMAT = 'train_step'
FWD_PARAMS = ['x', 'c', 'ctx', 'c_ctx', 'ada_w', 'ada_b', 'norm_g', 'ev_w_in', 'ev_q_norm', 'ev_k_norm', 'ev_dw_w', 'ev_dw_b', 'ev_ln_g', 'ev_ln_b', 'ev_w_out', 'od_w_in', 'od_ln_g', 'od_ln_b', 'od_ws', 'od_bs', 'od_w_out', 'final_g']
TWIN_WEIGHTS = ['c_ctx', 'ada_w', 'ada_b', 'norm_g', 'ev_w_in', 'ev_q_norm', 'ev_k_norm', 'ev_dw_w', 'ev_dw_b', 'ev_ln_g', 'ev_ln_b', 'ev_w_out', 'od_w_in', 'od_ln_g', 'od_ln_b', 'od_ws', 'od_bs', 'od_w_out', 'final_g']
TWIN_DIFF_INPUT = 'x'
TWIN_INPUTS = ['x', 'c', 'ctx', 'c_ctx', 'ada_w', 'ada_b', 'norm_g', 'ev_w_in', 'ev_q_norm', 'ev_k_norm', 'ev_dw_w', 'ev_dw_b', 'ev_ln_g', 'ev_ln_b', 'ev_w_out', 'od_w_in', 'od_ln_g', 'od_ln_b', 'od_ws', 'od_bs', 'od_w_out', 'final_g', 'loss_target', 'm_c_ctx', 'm_ada_w', 'm_ada_b', 'm_norm_g', 'm_ev_w_in', 'm_ev_q_norm', 'm_ev_k_norm', 'm_ev_dw_w', 'm_ev_dw_b', 'm_ev_ln_g', 'm_ev_ln_b', 'm_ev_w_out', 'm_od_w_in', 'm_od_ln_g', 'm_od_ln_b', 'm_od_ws', 'm_od_bs', 'm_od_w_out', 'm_final_g', 'v_c_ctx', 'v_ada_w', 'v_ada_b', 'v_norm_g', 'v_ev_w_in', 'v_ev_q_norm', 'v_ev_k_norm', 'v_ev_dw_w', 'v_ev_dw_b', 'v_ev_ln_g', 'v_ev_ln_b', 'v_ev_w_out', 'v_od_w_in', 'v_od_ln_g', 'v_od_ln_b', 'v_od_ws', 'v_od_bs', 'v_od_w_out', 'v_final_g']
TWIN_OUTPUTS = ['loss', 'grad_x', 'grad_c_ctx', 'grad_ada_w', 'grad_ada_b', 'grad_norm_g', 'grad_ev_w_in', 'grad_ev_q_norm', 'grad_ev_k_norm', 'grad_ev_dw_w', 'grad_ev_dw_b', 'grad_ev_ln_g', 'grad_ev_ln_b', 'grad_ev_w_out', 'grad_od_w_in', 'grad_od_ln_g', 'grad_od_ln_b', 'grad_od_ws', 'grad_od_bs', 'grad_od_w_out', 'grad_final_g', 'delta_c_ctx', 'delta_ada_w', 'delta_ada_b', 'delta_norm_g', 'delta_ev_w_in', 'delta_ev_q_norm', 'delta_ev_k_norm', 'delta_ev_dw_w', 'delta_ev_dw_b', 'delta_ev_ln_g', 'delta_ev_ln_b', 'delta_ev_w_out', 'delta_od_w_in', 'delta_od_ln_g', 'delta_od_ln_b', 'delta_od_ws', 'delta_od_bs', 'delta_od_w_out', 'delta_final_g', 'new_m_c_ctx', 'new_m_ada_w', 'new_m_ada_b', 'new_m_norm_g', 'new_m_ev_w_in', 'new_m_ev_q_norm', 'new_m_ev_k_norm', 'new_m_ev_dw_w', 'new_m_ev_dw_b', 'new_m_ev_ln_g', 'new_m_ev_ln_b', 'new_m_ev_w_out', 'new_m_od_w_in', 'new_m_od_ln_g', 'new_m_od_ln_b', 'new_m_od_ws', 'new_m_od_bs', 'new_m_od_w_out', 'new_m_final_g', 'new_v_c_ctx', 'new_v_ada_w', 'new_v_ada_b', 'new_v_norm_g', 'new_v_ev_w_in', 'new_v_ev_q_norm', 'new_v_ev_k_norm', 'new_v_ev_dw_w', 'new_v_ev_dw_b', 'new_v_ev_ln_g', 'new_v_ev_ln_b', 'new_v_ev_w_out', 'new_v_od_w_in', 'new_v_od_ln_g', 'new_v_od_ln_b', 'new_v_od_ws', 'new_v_od_bs', 'new_v_od_w_out', 'new_v_final_g']
TWIN_LEAF_KINDS = {'loss': 'loss', 'grad_x': 'grad_x', 'grad_c_ctx': 'grad_w', 'grad_ada_w': 'grad_w', 'grad_ada_b': 'grad_w', 'grad_norm_g': 'grad_w', 'grad_ev_w_in': 'grad_w', 'grad_ev_q_norm': 'grad_w', 'grad_ev_k_norm': 'grad_w', 'grad_ev_dw_w': 'grad_w', 'grad_ev_dw_b': 'grad_w', 'grad_ev_ln_g': 'grad_w', 'grad_ev_ln_b': 'grad_w', 'grad_ev_w_out': 'grad_w', 'grad_od_w_in': 'grad_w', 'grad_od_ln_g': 'grad_w', 'grad_od_ln_b': 'grad_w', 'grad_od_ws': 'grad_w', 'grad_od_bs': 'grad_w', 'grad_od_w_out': 'grad_w', 'grad_final_g': 'grad_w', 'delta_c_ctx': 'delta_w', 'delta_ada_w': 'delta_w', 'delta_ada_b': 'delta_w', 'delta_norm_g': 'delta_w', 'delta_ev_w_in': 'delta_w', 'delta_ev_q_norm': 'delta_w', 'delta_ev_k_norm': 'delta_w', 'delta_ev_dw_w': 'delta_w', 'delta_ev_dw_b': 'delta_w', 'delta_ev_ln_g': 'delta_w', 'delta_ev_ln_b': 'delta_w', 'delta_ev_w_out': 'delta_w', 'delta_od_w_in': 'delta_w', 'delta_od_ln_g': 'delta_w', 'delta_od_ln_b': 'delta_w', 'delta_od_ws': 'delta_w', 'delta_od_bs': 'delta_w', 'delta_od_w_out': 'delta_w', 'delta_final_g': 'delta_w', 'new_m_c_ctx': 'new_m', 'new_m_ada_w': 'new_m', 'new_m_ada_b': 'new_m', 'new_m_norm_g': 'new_m', 'new_m_ev_w_in': 'new_m', 'new_m_ev_q_norm': 'new_m', 'new_m_ev_k_norm': 'new_m', 'new_m_ev_dw_w': 'new_m', 'new_m_ev_dw_b': 'new_m', 'new_m_ev_ln_g': 'new_m', 'new_m_ev_ln_b': 'new_m', 'new_m_ev_w_out': 'new_m', 'new_m_od_w_in': 'new_m', 'new_m_od_ln_g': 'new_m', 'new_m_od_ln_b': 'new_m', 'new_m_od_ws': 'new_m', 'new_m_od_bs': 'new_m', 'new_m_od_w_out': 'new_m', 'new_m_final_g': 'new_m', 'new_v_c_ctx': 'new_v', 'new_v_ada_w': 'new_v', 'new_v_ada_b': 'new_v', 'new_v_norm_g': 'new_v', 'new_v_ev_w_in': 'new_v', 'new_v_ev_q_norm': 'new_v', 'new_v_ev_k_norm': 'new_v', 'new_v_ev_dw_w': 'new_v', 'new_v_ev_dw_b': 'new_v', 'new_v_ev_ln_g': 'new_v', 'new_v_ev_ln_b': 'new_v', 'new_v_ev_w_out': 'new_v', 'new_v_od_w_in': 'new_v', 'new_v_od_ln_g': 'new_v', 'new_v_od_ln_b': 'new_v', 'new_v_od_ws': 'new_v', 'new_v_od_bs': 'new_v', 'new_v_od_w_out': 'new_v', 'new_v_final_g': 'new_v'}


def _forward(args):
    return _fwd_reference(*[args[k] for k in FWD_PARAMS])


def _output_shape():
    def fwd():
        inp = _fwd_setup_inputs(0)
        return _fwd_reference(*[inp[k] for k in FWD_PARAMS])
    out = _jax.eval_shape(fwd)
    return out.shape, out.dtype

N_MICROBATCH = 1
ADAM_LR = 0.001
ADAM_B1 = 0.9
ADAM_B2 = 0.999
ADAM_EPS = 1e-08
ADAM_WD = 0.01
ADAM_STEP = 10
PER_EXAMPLE_BATCH_AXIS = {'x': 0, 'c': 0, 'ctx': 0, 'loss_target': 0}
SHARED_INPUTS = []
_WEIGHT_DTYPES = {'c_ctx': _jnp.float32, 'ada_w': _jnp.float32, 'ada_b': _jnp.float32, 'norm_g': _jnp.float32, 'ev_w_in': _jnp.float32, 'ev_q_norm': _jnp.float32, 'ev_k_norm': _jnp.float32, 'ev_dw_w': _jnp.float32, 'ev_dw_b': _jnp.float32, 'ev_ln_g': _jnp.float32, 'ev_ln_b': _jnp.float32, 'ev_w_out': _jnp.float32, 'od_w_in': _jnp.float32, 'od_ln_g': _jnp.float32, 'od_ln_b': _jnp.float32, 'od_ws': _jnp.float32, 'od_bs': _jnp.float32, 'od_w_out': _jnp.float32, 'final_g': _jnp.float32}
MOMENT_SCALE = {'c_ctx': 2.891037e-03, 'ada_w': 2.797834e-02, 'ada_b': 4.735873e-02, 'norm_g': 3.207324e-02, 'ev_w_in': 1.102274e-02, 'ev_q_norm': 6.117945e-03, 'ev_k_norm': 6.009313e-03, 'ev_dw_w': 1.586577e-02, 'ev_dw_b': 2.868988e-02, 'ev_ln_g': 1.887055e-02, 'ev_ln_b': 1.664912e-02, 'ev_w_out': 1.210737e-02, 'od_w_in': 2.342885e-02, 'od_ln_g': 1.717574e-02, 'od_ln_b': 1.744522e-02, 'od_ws': 2.419382e-02, 'od_bs': 2.510256e-02, 'od_w_out': 2.418768e-02, 'final_g': 3.198811e+01}


def _to_microbatches(a, axis):
    t = _jnp.moveaxis(a, axis, 0)
    t = t.reshape((N_MICROBATCH, t.shape[0] // N_MICROBATCH) + t.shape[1:])
    return _jnp.moveaxis(t, 1, axis + 1)


def setup_inputs(seed: int = 0) -> dict:
    inp = _fwd_setup_inputs(seed)
    key = _jax.random.fold_in(_jax.random.key(seed), 7919)
    shape, _ = _output_shape()
    out = dict(inp)
    out["loss_target"] = _jax.random.normal(_jax.random.fold_in(key, 0), shape, _jnp.float32)
    for i, name in enumerate(TWIN_WEIGHTS):
        w = inp[name].astype(_jnp.float32)
        if MOMENT_SCALE is None:
            s = _jnp.sqrt(_jnp.mean(_jnp.square(w)) + 1e-30)
        else:
            s = MOMENT_SCALE[name]
        km, kv = _jax.random.split(_jax.random.fold_in(key, i + 1))
        out[name] = w
        out["m_" + name] = s * _jax.random.normal(km, w.shape, _jnp.float32)
        out["v_" + name] = (s * s) * _jax.random.uniform(kv, w.shape, _jnp.float32, 0.5, 1.5)
    if N_MICROBATCH > 1:
        for name, axis in PER_EXAMPLE_BATCH_AXIS.items():
            out[name] = _to_microbatches(out[name], axis)
    return {'x': out['x'], 'c': out['c'], 'ctx': out['ctx'], 'c_ctx': out['c_ctx'], 'ada_w': out['ada_w'], 'ada_b': out['ada_b'], 'norm_g': out['norm_g'], 'ev_w_in': out['ev_w_in'], 'ev_q_norm': out['ev_q_norm'], 'ev_k_norm': out['ev_k_norm'], 'ev_dw_w': out['ev_dw_w'], 'ev_dw_b': out['ev_dw_b'], 'ev_ln_g': out['ev_ln_g'], 'ev_ln_b': out['ev_ln_b'], 'ev_w_out': out['ev_w_out'], 'od_w_in': out['od_w_in'], 'od_ln_g': out['od_ln_g'], 'od_ln_b': out['od_ln_b'], 'od_ws': out['od_ws'], 'od_bs': out['od_bs'], 'od_w_out': out['od_w_out'], 'final_g': out['final_g'], 'loss_target': out['loss_target'], 'm_c_ctx': out['m_c_ctx'], 'm_ada_w': out['m_ada_w'], 'm_ada_b': out['m_ada_b'], 'm_norm_g': out['m_norm_g'], 'm_ev_w_in': out['m_ev_w_in'], 'm_ev_q_norm': out['m_ev_q_norm'], 'm_ev_k_norm': out['m_ev_k_norm'], 'm_ev_dw_w': out['m_ev_dw_w'], 'm_ev_dw_b': out['m_ev_dw_b'], 'm_ev_ln_g': out['m_ev_ln_g'], 'm_ev_ln_b': out['m_ev_ln_b'], 'm_ev_w_out': out['m_ev_w_out'], 'm_od_w_in': out['m_od_w_in'], 'm_od_ln_g': out['m_od_ln_g'], 'm_od_ln_b': out['m_od_ln_b'], 'm_od_ws': out['m_od_ws'], 'm_od_bs': out['m_od_bs'], 'm_od_w_out': out['m_od_w_out'], 'm_final_g': out['m_final_g'], 'v_c_ctx': out['v_c_ctx'], 'v_ada_w': out['v_ada_w'], 'v_ada_b': out['v_ada_b'], 'v_norm_g': out['v_norm_g'], 'v_ev_w_in': out['v_ev_w_in'], 'v_ev_q_norm': out['v_ev_q_norm'], 'v_ev_k_norm': out['v_ev_k_norm'], 'v_ev_dw_w': out['v_ev_dw_w'], 'v_ev_dw_b': out['v_ev_dw_b'], 'v_ev_ln_g': out['v_ev_ln_g'], 'v_ev_ln_b': out['v_ev_ln_b'], 'v_ev_w_out': out['v_ev_w_out'], 'v_od_w_in': out['v_od_w_in'], 'v_od_ln_g': out['v_od_ln_g'], 'v_od_ln_b': out['v_od_ln_b'], 'v_od_ws': out['v_od_ws'], 'v_od_bs': out['v_od_bs'], 'v_od_w_out': out['v_od_w_out'], 'v_final_g': out['v_final_g']}


def _loss(weights, diff, rest, loss_target):
    with _jax.named_scope("forward"):
        args = {**rest, TWIN_DIFF_INPUT: diff, **{k: w.astype(_WEIGHT_DTYPES[k]) for k, w in weights.items()}}
        y = _forward(args)
    with _jax.named_scope("loss_head"):
        err = _jnp.square(y.astype(_jnp.float32) - loss_target)
        return 0.5 * _jnp.sum(_jnp.mean(err, axis=-1)) if err.ndim else 0.5 * err


def _adamw(w, g, m, v):
    m = ADAM_B1 * m + (1.0 - ADAM_B1) * g
    v = ADAM_B2 * v + (1.0 - ADAM_B2) * _jnp.square(g)
    m_hat = m / (1.0 - ADAM_B1 ** ADAM_STEP)
    v_hat = v / (1.0 - ADAM_B2 ** ADAM_STEP)
    delta = -ADAM_LR * (m_hat / (_jnp.sqrt(v_hat) + ADAM_EPS) + ADAM_WD * w)
    return delta, m, v


def reference(x, c, ctx, c_ctx, ada_w, ada_b, norm_g, ev_w_in, ev_q_norm, ev_k_norm, ev_dw_w, ev_dw_b, ev_ln_g, ev_ln_b, ev_w_out, od_w_in, od_ln_g, od_ln_b, od_ws, od_bs, od_w_out, final_g, loss_target, m_c_ctx, m_ada_w, m_ada_b, m_norm_g, m_ev_w_in, m_ev_q_norm, m_ev_k_norm, m_ev_dw_w, m_ev_dw_b, m_ev_ln_g, m_ev_ln_b, m_ev_w_out, m_od_w_in, m_od_ln_g, m_od_ln_b, m_od_ws, m_od_bs, m_od_w_out, m_final_g, v_c_ctx, v_ada_w, v_ada_b, v_norm_g, v_ev_w_in, v_ev_q_norm, v_ev_k_norm, v_ev_dw_w, v_ev_dw_b, v_ev_ln_g, v_ev_ln_b, v_ev_w_out, v_od_w_in, v_od_ln_g, v_od_ln_b, v_od_ws, v_od_bs, v_od_w_out, v_final_g):
    given = dict(x=x, c=c, ctx=ctx, c_ctx=c_ctx, ada_w=ada_w, ada_b=ada_b, norm_g=norm_g, ev_w_in=ev_w_in, ev_q_norm=ev_q_norm, ev_k_norm=ev_k_norm, ev_dw_w=ev_dw_w, ev_dw_b=ev_dw_b, ev_ln_g=ev_ln_g, ev_ln_b=ev_ln_b, ev_w_out=ev_w_out, od_w_in=od_w_in, od_ln_g=od_ln_g, od_ln_b=od_ln_b, od_ws=od_ws, od_bs=od_bs, od_w_out=od_w_out, final_g=final_g, loss_target=loss_target, m_c_ctx=m_c_ctx, m_ada_w=m_ada_w, m_ada_b=m_ada_b, m_norm_g=m_norm_g, m_ev_w_in=m_ev_w_in, m_ev_q_norm=m_ev_q_norm, m_ev_k_norm=m_ev_k_norm, m_ev_dw_w=m_ev_dw_w, m_ev_dw_b=m_ev_dw_b, m_ev_ln_g=m_ev_ln_g, m_ev_ln_b=m_ev_ln_b, m_ev_w_out=m_ev_w_out, m_od_w_in=m_od_w_in, m_od_ln_g=m_od_ln_g, m_od_ln_b=m_od_ln_b, m_od_ws=m_od_ws, m_od_bs=m_od_bs, m_od_w_out=m_od_w_out, m_final_g=m_final_g, v_c_ctx=v_c_ctx, v_ada_w=v_ada_w, v_ada_b=v_ada_b, v_norm_g=v_norm_g, v_ev_w_in=v_ev_w_in, v_ev_q_norm=v_ev_q_norm, v_ev_k_norm=v_ev_k_norm, v_ev_dw_w=v_ev_dw_w, v_ev_dw_b=v_ev_dw_b, v_ev_ln_g=v_ev_ln_g, v_ev_ln_b=v_ev_ln_b, v_ev_w_out=v_ev_w_out, v_od_w_in=v_od_w_in, v_od_ln_g=v_od_ln_g, v_od_ln_b=v_od_ln_b, v_od_ws=v_od_ws, v_od_bs=v_od_bs, v_od_w_out=v_od_w_out, v_final_g=v_final_g)
    weights = {n: given[n] for n in TWIN_WEIGHTS}
    shared = {n: given[n] for n in SHARED_INPUTS}
    per_example = {n: given[n] for n in ['x', 'c', 'ctx']}
    grad_fn = _jax.value_and_grad(_loss, argnums=(0, 1))

    def one_microbatch(ex, loss_target):
        ex = dict(ex)
        diff = ex.pop(TWIN_DIFF_INPUT)
        return grad_fn(weights, diff, {**shared, **ex}, loss_target)

    if N_MICROBATCH == 1:
        loss, (grad_w, grad_x) = one_microbatch(per_example, given["loss_target"])
    else:
        def body(carry, xs):
            loss_sum, grad_sum = carry
            l_k, (gw_k, gx_k) = one_microbatch(xs[0], xs[1])
            with _jax.named_scope("update"):
                return (loss_sum + l_k, _jax.tree.map(_jnp.add, grad_sum, gw_k)), gx_k

        init = (_jnp.zeros((), _jnp.float32), _jax.tree.map(_jnp.zeros_like, weights))
        (loss, grad_w), grad_x = _jax.lax.scan(body, init, (per_example, given["loss_target"]))
    with _jax.named_scope("update"):
        delta_w, new_m, new_v = {}, {}, {}
        for n in TWIN_WEIGHTS:
            delta_w[n], new_m[n], new_v[n] = _adamw(weights[n], grad_w[n], given["m_" + n], given["v_" + n])
    return (loss, grad_x, *[grad_w[n] for n in TWIN_WEIGHTS], *[delta_w[n] for n in TWIN_WEIGHTS],
            *[new_m[n] for n in TWIN_WEIGHTS], *[new_v[n] for n in TWIN_WEIGHTS])
```

```python
import functools
import math

import jax
import jax.numpy as jnp
from jax import lax
from jax.experimental import pallas as pl
from jax.experimental.pallas import tpu as pltpu

F32 = jnp.float32
BF16 = jnp.bfloat16
MESH = pl.DeviceIdType.MESH

EPS = 1e-6
HEAD_DIM = 128
N_Q_HEADS = 8
N_KV_HEADS = 2
Q_PER_KV = N_Q_HEADS // N_KV_HEADS
ATTN_W = N_Q_HEADS * HEAD_DIM
KV_W = N_KV_HEADS * HEAD_DIM
ATTN_SCALE = HEAD_DIM ** -0.5
ROPE_THETA = 10000.0
GRID_W = 64
CONV_WIDTH = 31
CONV_HALF = CONV_WIDTH // 2
HALO = 16
CHUNK = 128
SGU_GROUPS = 8
N_DEV = 8

ADAM_LR = 0.001
ADAM_B1 = 0.9
ADAM_B2 = 0.999
ADAM_EPS = 1e-08
ADAM_WD = 0.01
ADAM_STEP = 10

VMEM_LIMIT = 56 * 1024 * 1024
ANY = pl.BlockSpec(memory_space=pl.ANY)
VMEM_SPEC = pl.BlockSpec(memory_space=pltpu.VMEM)


def _params(*sem):
    return pltpu.CompilerParams(dimension_semantics=sem, vmem_limit_bytes=VMEM_LIMIT)


def _sigmoid(x):
    return 1.0 / (1.0 + jnp.exp(-x))


def _silu(x):
    return x * _sigmoid(x)


def _dsilu(x):
    s = _sigmoid(x)
    return s * (1.0 + x * (1.0 - s))


_GELU_C = math.sqrt(2.0 / math.pi)


def _gelu(x):
    t = jnp.tanh(_GELU_C * (x + 0.044715 * (x * x * x)))
    return 0.5 * x * (1.0 + t)


def _dgelu(x):
    t = jnp.tanh(_GELU_C * (x + 0.044715 * (x * x * x)))
    return 0.5 * (1.0 + t) + 0.5 * x * (1.0 - t * t) * (_GELU_C * (1.0 + 3.0 * 0.044715 * (x * x)))


def _row(v):
    return v.reshape(1, -1).astype(F32)


def _flat_id(p):
    return 4 * p[0] + 2 * p[1] + p[2]


def _gather_body(n_arr, sum_out):
    def body(*refs):
        x_refs = refs[:n_arr]
        out_refs = refs[n_arr:2 * n_arr]
        pos = 2 * n_arr
        sum_refs = refs[pos:pos + n_arr] if sum_out else ()
        pos += n_arr if sum_out else 0
        send_sems, recv_sems, local_sems = refs[pos:pos + 3]
        x, y, c = lax.axis_index("x"), lax.axis_index("y"), lax.axis_index("c")
        me, sibling = (x, y, c), (x, y, 1 - c)
        chips = [(1 - x, y), (x, 1 - y), (1 - x, 1 - y)]

        def copy(a, k, block, to, src=None):
            rows = out_refs[a].at[_flat_id(block)]
            return pltpu.make_async_remote_copy(
                src_ref=rows if src is None else src, dst_ref=rows,
                send_sem=send_sems.at[a, k], recv_sem=recv_sems.at[a, k],
                device_id=to, device_id_type=MESH)

        sends = []
        mine = []
        for a in range(n_arr):
            cp = pltpu.make_async_copy(x_refs[a], out_refs[a].at[_flat_id(me)], local_sems.at[a])
            cp.start()
            mine.append(cp)
            first = [copy(a, 0, me, sibling, src=x_refs[a])]
            first += [copy(a, 1 + j, me, (*chip, c), src=x_refs[a]) for j, chip in enumerate(chips)]
            for cp in first:
                cp.start()
            sends += first
        for a in range(n_arr):
            for j, chip in enumerate(chips):
                copy(a, 1 + j, (*chip, c), me).wait_recv()
                fwd = copy(a, 4 + j, (*chip, c), sibling)
                fwd.start()
                sends.append(fwd)
        for a in range(n_arr):
            copy(a, 0, sibling, me).wait_recv()
            for j, chip in enumerate(chips):
                copy(a, 4 + j, (*chip, 1 - c), me).wait_recv()
        for cp in sends:
            cp.wait_send()
        for cp in mine:
            cp.wait()
        if sum_out:
            for a in range(n_arr):
                acc = out_refs[a][0]
                for k in range(1, N_DEV):
                    acc = acc + out_refs[a][k]
                sum_refs[a][...] = acc

    return body


def all_gather(arrs, *, name, hbm, sum_out=False):
    n = len(arrs)
    spec = ANY if hbm else VMEM_SPEC
    out_shape = [jax.ShapeDtypeStruct((N_DEV,) + a.shape, a.dtype) for a in arrs]
    out_specs = [spec] * n
    if sum_out:
        out_shape += [jax.ShapeDtypeStruct(a.shape, a.dtype) for a in arrs]
        out_specs += [VMEM_SPEC] * n
    res = pl.pallas_call(
        _gather_body(n, sum_out), name=name,
        out_shape=out_shape, in_specs=[spec] * n, out_specs=out_specs,
        scratch_shapes=[pltpu.SemaphoreType.DMA((n, 7)), pltpu.SemaphoreType.DMA((n, 7)),
                        pltpu.SemaphoreType.DMA((n,))],
        compiler_params=pltpu.CompilerParams(vmem_limit_bytes=VMEM_LIMIT),
    )(*arrs)
    return res


_RELATIONS = [(rx, ry, rc) for rx in (0, 1) for ry in (0, 1) for rc in (0, 1)][1:]


def all_to_all(arrs, *, name):
    n = len(arrs)

    def body(*refs):
        x_refs = refs[:n]
        out_refs = refs[n:2 * n]
        send_sems, recv_sems, local_sems = refs[2 * n:2 * n + 3]
        x, y, c = lax.axis_index("x"), lax.axis_index("y"), lax.axis_index("c")
        me = _flat_id((x, y, c))
        local, remote = [], []
        for a in range(n):
            cp = pltpu.make_async_copy(x_refs[a].at[me], out_refs[a].at[me], local_sems.at[a])
            cp.start()
            local.append(cp)
            for k, (rx, ry, rc) in enumerate(_RELATIONS):
                peer = (1 - x if rx else x, 1 - y if ry else y, 1 - c if rc else c)
                cp = pltpu.make_async_remote_copy(
                    src_ref=x_refs[a].at[_flat_id(peer)], dst_ref=out_refs[a].at[me],
                    send_sem=send_sems.at[a, k], recv_sem=recv_sems.at[a, k],
                    device_id=peer, device_id_type=MESH)
                cp.start()
                remote.append(cp)
        for a in range(n):
            for k, (rx, ry, rc) in enumerate(_RELATIONS):
                peer = (1 - x if rx else x, 1 - y if ry else y, 1 - c if rc else c)
                pltpu.make_async_remote_copy(
                    src_ref=x_refs[a].at[me], dst_ref=out_refs[a].at[_flat_id(peer)],
                    send_sem=send_sems.at[a, k], recv_sem=recv_sems.at[a, k],
                    device_id=peer, device_id_type=MESH).wait_recv()
        for cp in remote:
            cp.wait_send()
        for cp in local:
            cp.wait()

    return pl.pallas_call(
        body, name=name,
        out_shape=[jax.ShapeDtypeStruct(a.shape, a.dtype) for a in arrs],
        in_specs=[ANY] * n, out_specs=[ANY] * n,
        scratch_shapes=[pltpu.SemaphoreType.DMA((n, 7)), pltpu.SemaphoreType.DMA((n, 7)),
                        pltpu.SemaphoreType.DMA((n,))],
    )(*arrs)


_DIMS = {"nn": (((1,), (0,)), ((), ())), "nt": (((1,), (1,)), ((), ())), "tn": (((0,), (0,)), ((), ()))}


def matmul(a, b, *, mode, tm, tn, tk, out_dtype, name, n_out=None, res=None, gate=None, add=None):
    if mode == "tn":
        kdim, m = a.shape
    else:
        m, kdim = a.shape
    nfull = b.shape[0] if mode == "nt" else b.shape[1]
    n = nfull if n_out is None else n_out
    tm, tn, tk = min(tm, m), min(tn, n), min(tk, kdim)
    assert m % tm == 0 and n % tn == 0 and kdim % tk == 0, (name, m, n, kdim, tm, tn, tk)
    nk = kdim // tk
    dims = _DIMS[mode]
    a_spec = pl.BlockSpec((tk, tm), lambda j, i, k: (k, i)) if mode == "tn" else pl.BlockSpec((tm, tk), lambda j, i, k: (i, k))
    b_spec = pl.BlockSpec((tn, tk), lambda j, i, k: (j, k)) if mode == "nt" else pl.BlockSpec((tk, tn), lambda j, i, k: (k, j))
    o_spec = pl.BlockSpec((tm, tn), lambda j, i, k: (i, j))
    in_specs = [a_spec, b_spec]
    operands = [a, b]
    aliases = {}
    if res is not None:
        in_specs += [o_spec, pl.BlockSpec((1, tn), lambda j, i, k: (0, j))]
        operands += [res, gate]
    if add is not None:
        in_specs += [o_spec]
        aliases = {len(operands): 0}
        operands += [add]
    out_cols = n if add is None else add.shape[1]
    out_shape = [jax.ShapeDtypeStruct((m, out_cols), out_dtype)]
    out_specs = [o_spec]
    if res is not None:
        out_shape.append(jax.ShapeDtypeStruct((m, n), BF16))
        out_specs.append(o_spec)

    def body(*refs):
        a_ref, b_ref = refs[:2]
        pos = 2
        if res is not None:
            res_ref, gate_ref = refs[pos:pos + 2]
            pos += 2
        if add is not None:
            add_ref = refs[pos]
            pos += 1
        o_ref = refs[pos]
        pos += 1
        if res is not None:
            r_ref = refs[pos]
            pos += 1
        acc_ref = refs[pos] if nk > 1 else None
        prod = lax.dot_general(a_ref[...].astype(BF16), b_ref[...].astype(BF16), dims,
                               preferred_element_type=F32)

        def finish(acc):
            if res is not None:
                o_ref[...] = (res_ref[...] + gate_ref[...] * acc).astype(out_dtype)
                r_ref[...] = acc.astype(BF16)
            elif add is not None:
                o_ref[...] = (add_ref[...] + acc).astype(out_dtype)
            else:
                o_ref[...] = acc.astype(out_dtype)

        if nk == 1:
            finish(prod)
        else:
            k = pl.program_id(2)

            @pl.when(k == 0)
            def _():
                acc_ref[...] = prod

            @pl.when(k > 0)
            def _():
                acc_ref[...] += prod

            @pl.when(k == nk - 1)
            def _():
                finish(acc_ref[...])

    outs = pl.pallas_call(
        body, name=name, grid=(n // tn, m // tm, nk),
        in_specs=in_specs, out_specs=out_specs, out_shape=out_shape,
        scratch_shapes=[pltpu.VMEM((tm, tn), F32)] if nk > 1 else [],
        input_output_aliases=aliases,
        compiler_params=_params("parallel", "parallel", "arbitrary"),
    )(*operands)
    return outs if res is not None else outs[0]


def _rows_tile(n, want):
    return min(n, want)


def norm_mod_fwd(x, g, shift, scale, *, name):
    n, d = x.shape
    tt = _rows_tile(n, 256)

    def body(x_ref, g_ref, sh_ref, sc_ref, h_ref):
        xv = x_ref[...]
        rstd = lax.rsqrt(jnp.mean(xv * xv, axis=-1, keepdims=True) + EPS)
        y = xv * rstd * g_ref[...]
        h_ref[...] = (y * (1.0 + sc_ref[...]) + sh_ref[...]).astype(BF16)

    vec = pl.BlockSpec((1, d), lambda i: (0, 0))
    return pl.pallas_call(
        body, name=name, grid=(n // tt,),
        in_specs=[pl.BlockSpec((tt, d), lambda i: (i, 0)), vec, vec, vec],
        out_specs=pl.BlockSpec((tt, d), lambda i: (i, 0)),
        out_shape=jax.ShapeDtypeStruct((n, d), BF16),
        compiler_params=_params("parallel"),
    )(x, _row(g), _row(shift), _row(scale))


def norm_mod_bwd(x, dh, dres, g, scale, *, name):
    n, d = x.shape
    tt = _rows_tile(n, 256)
    has_res = dres is not None
    last = n // tt - 1

    def body(*refs):
        x_ref, dh_ref = refs[:2]
        pos = 2
        if has_res:
            dres_ref = refs[pos]
            pos += 1
        g_ref, sc_ref, dx_ref, acc_ref, s_ref = refs[pos:pos + 5]
        i = pl.program_id(0)
        xv = x_ref[...]
        dhv = dh_ref[...]
        rstd = lax.rsqrt(jnp.mean(xv * xv, axis=-1, keepdims=True) + EPS)
        xhat = xv * rstd
        dxhat = dhv * (g_ref[...] * (1.0 + sc_ref[...]))
        dx = rstd * (dxhat - xhat * jnp.mean(dxhat * xhat, axis=-1, keepdims=True))
        if has_res:
            dx = dx + dres_ref[...]
        dx_ref[...] = dx

        @pl.when(i == 0)
        def _():
            s_ref[...] = jnp.zeros_like(s_ref)

        s_ref[0:1, :] += jnp.sum(dhv, axis=0, keepdims=True)
        s_ref[1:2, :] += jnp.sum(dhv * xhat, axis=0, keepdims=True)

        @pl.when(i == last)
        def _():
            s1 = s_ref[0:1, :]
            s2 = s_ref[1:2, :]
            acc_ref[...] = jnp.zeros_like(acc_ref)
            acc_ref[0:1, :] = s1
            acc_ref[1:2, :] = s2 * g_ref[...]
            acc_ref[2:3, :] = s2 * (1.0 + sc_ref[...])

    vec = pl.BlockSpec((1, d), lambda i: (0, 0))
    big = pl.BlockSpec((tt, d), lambda i: (i, 0))
    ops = [x, dh] + ([dres] if has_res else []) + [_row(g), _row(scale)]
    return pl.pallas_call(
        body, name=name, grid=(n // tt,),
        in_specs=[big, big] + ([big] if has_res else []) + [vec, vec],
        out_specs=[big, pl.BlockSpec((8, d), lambda i: (0, 0))],
        out_shape=[jax.ShapeDtypeStruct((n, d), F32), jax.ShapeDtypeStruct((8, d), F32)],
        scratch_shapes=[pltpu.VMEM((8, d), F32)],
        compiler_params=_params("arbitrary"),
    )(*ops)


def gate_bwd(dx, r, gate, *, name):
    n, d = dx.shape
    tt = _rows_tile(n, 256)

    def body(dx_ref, r_ref, gate_ref, dr_ref, acc_ref):
        i = pl.program_id(0)
        dxv = dx_ref[...]
        dr_ref[...] = (dxv * gate_ref[...]).astype(BF16)

        @pl.when(i == 0)
        def _():
            acc_ref[...] = jnp.zeros_like(acc_ref)

        acc_ref[0:1, :] += jnp.sum(dxv * r_ref[...].astype(F32), axis=0, keepdims=True)

    big = pl.BlockSpec((tt, d), lambda i: (i, 0))
    return pl.pallas_call(
        body, name=name, grid=(n // tt,),
        in_specs=[big, big, pl.BlockSpec((1, d), lambda i: (0, 0))],
        out_specs=[big, pl.BlockSpec((8, d), lambda i: (0, 0))],
        out_shape=[jax.ShapeDtypeStruct((n, d), BF16), jax.ShapeDtypeStruct((8, d), F32)],
        compiler_params=_params("arbitrary"),
    )(dx, r, _row(gate))


def final_loss(x, target, g, *, name):
    n, d = x.shape
    tt = _rows_tile(n, 256)

    def body(x_ref, t_ref, g_ref, dx_ref, acc_ref, loss_ref):
        i = pl.program_id(0)
        xv = x_ref[...]
        rstd = lax.rsqrt(jnp.mean(xv * xv, axis=-1, keepdims=True) + EPS)
        xhat = xv * rstd
        e = xhat * g_ref[...] - t_ref[...]
        dy = e * (1.0 / d)
        dxhat = dy * g_ref[...]
        dx_ref[...] = rstd * (dxhat - xhat * jnp.mean(dxhat * xhat, axis=-1, keepdims=True))

        @pl.when(i == 0)
        def _():
            acc_ref[...] = jnp.zeros_like(acc_ref)
            loss_ref[...] = jnp.zeros_like(loss_ref)

        acc_ref[0:1, :] += jnp.sum(dy * xhat, axis=0, keepdims=True)
        part = 0.5 * jnp.sum(jnp.mean(e * e, axis=-1, keepdims=True), axis=0, keepdims=True)
        loss_ref[...] += jnp.broadcast_to(part, loss_ref.shape)

    big = pl.BlockSpec((tt, d), lambda i: (i, 0))
    return pl.pallas_call(
        body, name=name, grid=(n // tt,),
        in_specs=[big, big, pl.BlockSpec((1, d), lambda i: (0, 0))],
        out_specs=[big, pl.BlockSpec((8, d), lambda i: (0, 0)), pl.BlockSpec((8, 128), lambda i: (0, 0))],
        out_shape=[jax.ShapeDtypeStruct((n, d), F32), jax.ShapeDtypeStruct((8, d), F32),
                   jax.ShapeDtypeStruct((8, 128), F32)],
        compiler_params=_params("arbitrary"),
    )(x, target, _row(g))


def _swap_pairs(x):
    lane = lax.broadcasted_iota(jnp.int32, x.shape, 1)
    return jnp.where(lane % 2 == 0, pltpu.roll(x, HEAD_DIM - 1, 1), pltpu.roll(x, 1, 1))


def rope_tables(n):
    rows = n // GRID_W
    row = jnp.repeat(jnp.arange(rows, dtype=F32), GRID_W)
    col = jnp.tile(jnp.arange(GRID_W, dtype=F32), rows)
    n_freq = HEAD_DIM // 4
    inv = jnp.power(ROPE_THETA, jnp.arange(n_freq, dtype=F32) * (-2.0 / (HEAD_DIM // 2)))
    ang = jnp.concatenate([row[:, None] * inv, col[:, None] * inv], axis=-1)
    cos, sin = jnp.cos(ang), jnp.sin(ang)
    cexp = jnp.repeat(cos, 2, axis=-1)
    sexp = jnp.stack([-sin, sin], axis=-1).reshape(n, HEAD_DIM)
    return cexp, sexp


def qkv_prep_fwd(p, wq, wk, cexp, sexp, *, latent, name):
    n = p.shape[0]
    tt = _rows_tile(n, 256)
    width = 2 * KV_W + (ATTN_W if latent else 0)

    def body(*refs):
        if latent:
            p_ref, wq_ref, wk_ref, c_ref, s_ref, q_ref, k_ref, v_ref = refs
        else:
            p_ref, wk_ref, k_ref, v_ref = refs

        def head(xv, w):
            rstd = lax.rsqrt(jnp.mean(xv * xv, axis=-1, keepdims=True) + EPS)
            yv = xv * rstd * w
            if latent:
                yv = yv * c_ref[...] + _swap_pairs(yv) * s_ref[...]
            return yv

        for h in range(N_KV_HEADS):
            sl = slice(h * HEAD_DIM, (h + 1) * HEAD_DIM)
            k_ref[:, sl] = head(p_ref[:, sl], wk_ref[...]).astype(BF16)
        v_ref[...] = p_ref[:, KV_W:2 * KV_W].astype(BF16)
        if latent:
            for h in range(N_Q_HEADS):
                sl = slice(2 * KV_W + h * HEAD_DIM, 2 * KV_W + (h + 1) * HEAD_DIM)
                q_ref[:, h * HEAD_DIM:(h + 1) * HEAD_DIM] = (head(p_ref[:, sl], wq_ref[...]) * ATTN_SCALE).astype(BF16)

    vec = pl.BlockSpec((1, HEAD_DIM), lambda i: (0, 0))
    tab = pl.BlockSpec((tt, HEAD_DIM), lambda i: (i, 0))
    kv_spec = pl.BlockSpec((tt, KV_W), lambda i: (i, 0))
    kv_shape = jax.ShapeDtypeStruct((n, KV_W), BF16)
    if latent:
        in_specs = [pl.BlockSpec((tt, width), lambda i: (i, 0)), vec, vec, tab, tab]
        ops = [p, _row(wq), _row(wk), cexp, sexp]
        out_specs = [pl.BlockSpec((tt, ATTN_W), lambda i: (i, 0)), kv_spec, kv_spec]
        out_shape = [jax.ShapeDtypeStruct((n, ATTN_W), BF16), kv_shape, kv_shape]
    else:
        in_specs = [pl.BlockSpec((tt, width), lambda i: (i, 0)), vec]
        ops = [p, _row(wk)]
        out_specs = [kv_spec, kv_spec]
        out_shape = [kv_shape, kv_shape]
    return pl.pallas_call(
        body, name=name, grid=(n // tt,), in_specs=in_specs, out_specs=out_specs, out_shape=out_shape,
        compiler_params=_params("parallel"),
    )(*ops)


def qkv_prep_bwd(p, dq, dk, dv, wq, wk, cexp, sexp, *, latent, name):
    n = p.shape[0]
    tt = _rows_tile(n, 256)
    width = 2 * KV_W + (ATTN_W if latent else 0)

    def body(*refs):
        if latent:
            p_ref, dq_ref, dk_ref, dv_ref, wq_ref, wk_ref, c_ref, s_ref, dp_ref, acc_ref = refs
        else:
            p_ref, dk_ref, dv_ref, wk_ref, dp_ref, acc_ref = refs
        i = pl.program_id(0)

        @pl.when(i == 0)
        def _():
            acc_ref[...] = jnp.zeros_like(acc_ref)

        def head(xv, dy, w, row):
            if latent:
                dy = dy * c_ref[...] + _swap_pairs(dy * s_ref[...])
            rstd = lax.rsqrt(jnp.mean(xv * xv, axis=-1, keepdims=True) + EPS)
            xhat = xv * rstd
            acc_ref[row:row + 1, :] += jnp.sum(dy * xhat, axis=0, keepdims=True)
            dxhat = dy * w
            return rstd * (dxhat - xhat * jnp.mean(dxhat * xhat, axis=-1, keepdims=True))

        for h in range(N_KV_HEADS):
            sl = slice(h * HEAD_DIM, (h + 1) * HEAD_DIM)
            dp_ref[:, sl] = head(p_ref[:, sl], dk_ref[:, sl], wk_ref[...], 1).astype(BF16)
        dp_ref[:, KV_W:2 * KV_W] = dv_ref[...].astype(BF16)
        if latent:
            for h in range(N_Q_HEADS):
                sl = slice(2 * KV_W + h * HEAD_DIM, 2 * KV_W + (h + 1) * HEAD_DIM)
                dyq = dq_ref[:, h * HEAD_DIM:(h + 1) * HEAD_DIM] * ATTN_SCALE
                dp_ref[:, sl] = head(p_ref[:, sl], dyq, wq_ref[...], 0).astype(BF16)

    vec = pl.BlockSpec((1, HEAD_DIM), lambda i: (0, 0))
    tab = pl.BlockSpec((tt, HEAD_DIM), lambda i: (i, 0))
    kv_spec = pl.BlockSpec((tt, KV_W), lambda i: (i, 0))
    p_spec = pl.BlockSpec((tt, width), lambda i: (i, 0))
    if latent:
        in_specs = [p_spec, pl.BlockSpec((tt, ATTN_W), lambda i: (i, 0)), kv_spec, kv_spec, vec, vec, tab, tab]
        ops = [p, dq, dk, dv, _row(wq), _row(wk), cexp, sexp]
    else:
        in_specs = [p_spec, kv_spec, kv_spec, vec]
        ops = [p, dk, dv, _row(wk)]
    return pl.pallas_call(
        body, name=name, grid=(n // tt,), in_specs=in_specs,
        out_specs=[p_spec, pl.BlockSpec((8, HEAD_DIM), lambda i: (0, 0))],
        out_shape=[jax.ShapeDtypeStruct((n, width), BF16), jax.ShapeDtypeStruct((8, HEAD_DIM), F32)],
        compiler_params=_params("arbitrary"),
    )(*ops)


def _kv_chunks(n, s_all):
    step = 1024 if n % 1024 == 0 else 256
    chunks = [(s, step) for s in range(0, n, step)]
    if s_all > n:
        chunks.append((n, s_all - n))
    return chunks


def flash_fwd(q, k_all, v_all, p, *, za_block, name):
    n = q.shape[0]
    s_all = k_all.shape[0]
    tq = _rows_tile(n, 256)
    chunks = _kv_chunks(n, s_all)

    def body(q_ref, k_ref, v_ref, za_ref, o_ref, mix_ref, lse_ref, m_ref, l_ref, acc_ref):
        qv = q_ref[...]
        m_ref[...] = jnp.full_like(m_ref, -jnp.inf)
        l_ref[...] = jnp.zeros_like(l_ref)
        acc_ref[...] = jnp.zeros_like(acc_ref)
        for start, size in chunks:
            kc = k_ref[pl.ds(start, size), :]
            vc = v_ref[pl.ds(start, size), :]
            s = lax.dot_general(qv, kc, _DIMS["nt"], preferred_element_type=F32)
            m_old = m_ref[...]
            m_new = jnp.maximum(m_old, jnp.max(s, axis=-1, keepdims=True))
            pr = jnp.exp(s - m_new)
            alpha = jnp.exp(m_old - m_new)
            l_ref[...] = alpha * l_ref[...] + jnp.sum(pr, axis=-1, keepdims=True)
            acc_ref[...] = alpha * acc_ref[...] + jnp.dot(pr.astype(BF16), vc, preferred_element_type=F32)
            m_ref[...] = m_new
        o = acc_ref[...] / l_ref[...]
        o_ref[...] = o.astype(BF16)
        mix_ref[...] = (o * _silu(za_ref[...])).astype(BF16)
        lse_ref[0] = m_ref[...] + jnp.log(l_ref[...])

    qspec = pl.BlockSpec((tq, HEAD_DIM), lambda h, i: (i, h))
    kvspec = pl.BlockSpec((s_all, HEAD_DIM), lambda h, i: (0, h // Q_PER_KV))
    return pl.pallas_call(
        body, name=name, grid=(N_Q_HEADS, n // tq),
        in_specs=[qspec, kvspec, kvspec, pl.BlockSpec((tq, HEAD_DIM), lambda h, i: (i, za_block + h))],
        out_specs=[qspec, qspec, pl.BlockSpec((1, tq, 1), lambda h, i: (h, i, 0))],
        out_shape=[jax.ShapeDtypeStruct((n, ATTN_W), BF16), jax.ShapeDtypeStruct((n, ATTN_W), BF16),
                   jax.ShapeDtypeStruct((N_Q_HEADS, n, 1), F32)],
        scratch_shapes=[pltpu.VMEM((tq, 1), F32), pltpu.VMEM((tq, 1), F32), pltpu.VMEM((tq, HEAD_DIM), F32)],
        compiler_params=_params("parallel", "parallel"),
    )(q, k_all, v_all, p)


def attn_gate_bwd(dmix, o, p, *, za_block, name):
    n = o.shape[0]
    tt = _rows_tile(n, 512)

    def body(dm_ref, o_ref, za_ref, do_ref, dza_ref, delta_ref):
        dm = dm_ref[...]
        ov = o_ref[...].astype(F32)
        za = za_ref[...]
        do = dm * _silu(za)
        do_ref[...] = do.astype(BF16)
        dza_ref[...] = (dm * ov * _dsilu(za)).astype(BF16)
        delta_ref[0] = jnp.sum(do * ov, axis=-1, keepdims=True)

    spec = pl.BlockSpec((tt, HEAD_DIM), lambda i, h: (i, h))
    return pl.pallas_call(
        body, name=name, grid=(n // tt, N_Q_HEADS),
        in_specs=[spec, spec, pl.BlockSpec((tt, HEAD_DIM), lambda i, h: (i, za_block + h))],
        out_specs=[spec, spec, pl.BlockSpec((1, tt, 1), lambda i, h: (h, i, 0))],
        out_shape=[jax.ShapeDtypeStruct((n, ATTN_W), BF16), jax.ShapeDtypeStruct((n, ATTN_W), BF16),
                   jax.ShapeDtypeStruct((N_Q_HEADS, n, 1), F32)],
        compiler_params=_params("parallel", "parallel"),
    )(dmix, o, p)


def flash_bwd(q, do, lse_row, delta_row, k_all, v_all, k_t, *, name):
    n = q.shape[0]
    s_all = k_all.shape[0]
    tq = _rows_tile(n, 256)
    chunks = _kv_chunks(n, s_all)

    def body(q_ref, do_ref, lse_ref, dl_ref, k_ref, v_ref, kt_ref, dqt_ref, dk_ref, dv_ref):
        g = pl.program_id(1)
        i = pl.program_id(2)

        @pl.when((g == 0) & (i == 0))
        def _():
            dk_ref[...] = jnp.zeros_like(dk_ref)
            dv_ref[...] = jnp.zeros_like(dv_ref)

        qv = q_ref[...]
        dov = do_ref[...]
        lse = lse_ref[0]
        dl = dl_ref[0]
        dqt = jnp.zeros((HEAD_DIM, tq), F32)
        for start, size in chunks:
            kc = k_ref[pl.ds(start, size), :]
            vc = v_ref[pl.ds(start, size), :]
            st = lax.dot_general(kc, qv, _DIMS["nt"], preferred_element_type=F32)
            pt = jnp.exp(st - lse)
            dpt = lax.dot_general(vc, dov, _DIMS["nt"], preferred_element_type=F32)
            dst = (pt * (dpt - dl)).astype(BF16)
            dv_ref[pl.ds(start, size), :] += jnp.dot(pt.astype(BF16), dov, preferred_element_type=F32)
            dk_ref[pl.ds(start, size), :] += jnp.dot(dst, qv, preferred_element_type=F32)
            dqt = dqt + jnp.dot(kt_ref[:, pl.ds(start, size)], dst, preferred_element_type=F32)
        dqt_ref[0] = dqt

    qspec = pl.BlockSpec((tq, HEAD_DIM), lambda kh, g, i: (i, kh * Q_PER_KV + g))
    rowspec = pl.BlockSpec((1, 1, tq), lambda kh, g, i: (kh * Q_PER_KV + g, 0, i))
    kvspec = pl.BlockSpec((s_all, HEAD_DIM), lambda kh, g, i: (0, kh))
    return pl.pallas_call(
        body, name=name, grid=(N_KV_HEADS, Q_PER_KV, n // tq),
        in_specs=[qspec, qspec, rowspec, rowspec, kvspec, kvspec,
                  pl.BlockSpec((HEAD_DIM, s_all), lambda kh, g, i: (kh, 0))],
        out_specs=[pl.BlockSpec((1, HEAD_DIM, tq), lambda kh, g, i: (kh * Q_PER_KV + g, 0, i)), kvspec, kvspec],
        out_shape=[jax.ShapeDtypeStruct((N_Q_HEADS, HEAD_DIM, n), F32),
                   jax.ShapeDtypeStruct((s_all, KV_W), F32), jax.ShapeDtypeStruct((s_all, KV_W), F32)],
        compiler_params=_params("arbitrary", "arbitrary", "arbitrary"),
    )(q, do, lse_row, delta_row, k_all, v_all, k_t)


HALF = 512


def _halo_specs(tt, n, cb):
    per = tt // HALO
    last = n // HALO - 1
    return [pl.BlockSpec((HALO, HALF), lambda i: (jnp.maximum(i * per - 1, 0), cb)),
            pl.BlockSpec((tt, HALF), lambda i: (i, cb)),
            pl.BlockSpec((HALO, HALF), lambda i: (jnp.minimum((i + 1) * per, last), cb))]


def conv_fwd(p, dw_w, dw_b, ln_g, ln_b, *, glu_block, name):
    n = p.shape[0]
    ch = dw_w.shape[1]
    nh = ch // HALF
    tt = _rows_tile(n, 256)
    last = n // tt - 1

    def body(*refs):
        a_refs = [refs[3 * h:3 * h + 3] for h in range(nh)]
        b_refs = [refs[3 * (nh + h):3 * (nh + h) + 3] for h in range(nh)]
        pos = 6 * nh
        zb_refs = refs[pos:pos + nh]
        pos += nh
        w_ref, bias_ref, g_ref, be_ref, ycv_ref, mix_ref, ext_ref = refs[pos:pos + 7]
        i = pl.program_id(0)
        for h in range(nh):
            cs = slice(h * HALF, (h + 1) * HALF)
            ap, am, an = a_refs[h]
            bp, bm, bn = b_refs[h]
            ext_ref[0:HALO, :] = jnp.where(i > 0, ap[...] * _sigmoid(bp[...]), 0.0)
            ext_ref[HALO:HALO + tt, :] = am[...] * _sigmoid(bm[...])
            ext_ref[HALO + tt:2 * HALO + tt, :] = jnp.where(i < last, an[...] * _sigmoid(bn[...]), 0.0)
            acc = jnp.broadcast_to(bias_ref[:, cs], (tt, HALF))
            for k in range(CONV_WIDTH):
                acc = acc + w_ref[k:k + 1, cs] * ext_ref[pl.ds(1 + k, tt), :]
            ycv_ref[:, cs] = acc
        yc = ycv_ref[...]
        mu = jnp.mean(yc, axis=-1, keepdims=True)
        var = jnp.mean(jnp.square(yc - mu), axis=-1, keepdims=True)
        ln = (yc - mu) * lax.rsqrt(var + EPS) * g_ref[...] + be_ref[...]
        out = _silu(ln)
        for h in range(nh):
            cs = slice(h * HALF, (h + 1) * HALF)
            mix_ref[:, cs] = (out[:, cs] * _silu(zb_refs[h][...])).astype(BF16)

    in_specs = []
    for h in range(2 * nh):
        in_specs += _halo_specs(tt, n, glu_block + h)
    in_specs += [pl.BlockSpec((tt, HALF), functools.partial(lambda i, cb: (i, cb), cb=glu_block + 2 * nh + h))
                 for h in range(nh)]
    vec = pl.BlockSpec((1, ch), lambda i: (0, 0))
    in_specs += [pl.BlockSpec((CONV_WIDTH, ch), lambda i: (0, 0)), vec, vec, vec]
    ops = [p] * (6 * nh + nh) + [dw_w, _row(dw_b), _row(ln_g), _row(ln_b)]
    big = pl.BlockSpec((tt, ch), lambda i: (i, 0))
    return pl.pallas_call(
        body, name=name, grid=(n // tt,), in_specs=in_specs, out_specs=[big, big],
        out_shape=[jax.ShapeDtypeStruct((n, ch), F32), jax.ShapeDtypeStruct((n, ch), BF16)],
        scratch_shapes=[pltpu.VMEM((tt + 2 * HALO, HALF), F32)],
        compiler_params=_params("parallel"),
    )(*ops)


def conv_bwd_rows(dmix, ycv, p, ln_g, ln_b, *, mix_block, zb_block, name):
    n, ch = ycv.shape
    nh = ch // HALF
    tt = _rows_tile(n, 256)

    def body(*refs):
        dm_ref, ycv_ref = refs[:2]
        zb_refs = refs[2:2 + nh]
        g_ref, be_ref, dy_ref, dzb_ref, acc_ref = refs[2 + nh:]
        i = pl.program_id(0)

        @pl.when(i == 0)
        def _():
            acc_ref[...] = jnp.zeros_like(acc_ref)

        yc = ycv_ref[...]
        mu = jnp.mean(yc, axis=-1, keepdims=True)
        var = jnp.mean(jnp.square(yc - mu), axis=-1, keepdims=True)
        rstd = lax.rsqrt(var + EPS)
        xhat = (yc - mu) * rstd
        ln = xhat * g_ref[...] + be_ref[...]
        out = _silu(ln)
        dm = dm_ref[...]
        zb = jnp.concatenate([r[...] for r in zb_refs], axis=-1)
        dzb_ref[...] = (dm * out * _dsilu(zb)).astype(BF16)
        dln = dm * _silu(zb) * _dsilu(ln)
        acc_ref[0:1, :] += jnp.sum(dln * xhat, axis=0, keepdims=True)
        acc_ref[1:2, :] += jnp.sum(dln, axis=0, keepdims=True)
        dxhat = dln * g_ref[...]
        dy_ref[...] = rstd * (dxhat - jnp.mean(dxhat, axis=-1, keepdims=True)
                              - xhat * jnp.mean(dxhat * xhat, axis=-1, keepdims=True))

    big = pl.BlockSpec((tt, ch), lambda i: (i, 0))
    vec = pl.BlockSpec((1, ch), lambda i: (0, 0))
    in_specs = [pl.BlockSpec((tt, ch), lambda i: (i, mix_block)), big]
    in_specs += [pl.BlockSpec((tt, HALF), functools.partial(lambda i, cb: (i, cb), cb=zb_block + h)) for h in range(nh)]
    in_specs += [vec, vec]
    return pl.pallas_call(
        body, name=name, grid=(n // tt,), in_specs=in_specs,
        out_specs=[big, big, pl.BlockSpec((8, ch), lambda i: (0, 0))],
        out_shape=[jax.ShapeDtypeStruct((n, ch), F32), jax.ShapeDtypeStruct((n, ch), BF16),
                   jax.ShapeDtypeStruct((8, ch), F32)],
        compiler_params=_params("arbitrary"),
    )(dmix, ycv, *([p] * nh), _row(ln_g), _row(ln_b))


def conv_bwd_taps(dycv, p, dw_w, *, glu_block, name):
    n, ch = dycv.shape
    nh = ch // HALF
    tt = _rows_tile(n, 256)
    last = n // tt - 1

    def body(*refs):
        d_refs = [refs[3 * h:3 * h + 3] for h in range(nh)]
        a_refs = [refs[3 * (nh + h):3 * (nh + h) + 3] for h in range(nh)]
        b_refs = [refs[3 * (2 * nh + h):3 * (2 * nh + h) + 3] for h in range(nh)]
        w_ref, dglu_ref, dw_ref, db_ref, yext_ref, dext_ref = refs[9 * nh:]
        i = pl.program_id(0)

        @pl.when(i == 0)
        def _():
            dw_ref[...] = jnp.zeros_like(dw_ref)
            db_ref[...] = jnp.zeros_like(db_ref)

        for h in range(nh):
            cs = slice(h * HALF, (h + 1) * HALF)
            ap, am, an = a_refs[h]
            bp, bm, bn = b_refs[h]
            dp, dm, dn = d_refs[h]
            av = am[...]
            sb = _sigmoid(bm[...])
            yext_ref[0:HALO, :] = jnp.where(i > 0, ap[...] * _sigmoid(bp[...]), 0.0)
            yext_ref[HALO:HALO + tt, :] = av * sb
            yext_ref[HALO + tt:2 * HALO + tt, :] = jnp.where(i < last, an[...] * _sigmoid(bn[...]), 0.0)
            dmain = dm[...]
            dext_ref[0:HALO, :] = jnp.where(i > 0, dp[...], 0.0)
            dext_ref[HALO:HALO + tt, :] = dmain
            dext_ref[HALO + tt:2 * HALO + tt, :] = jnp.where(i < last, dn[...], 0.0)
            dy = jnp.zeros((tt, HALF), F32)
            for k in range(CONV_WIDTH):
                dy = dy + w_ref[k:k + 1, cs] * dext_ref[pl.ds(CONV_WIDTH - k, tt), :]
                dw_ref[k:k + 1, cs] += jnp.sum(dmain * yext_ref[pl.ds(1 + k, tt), :], axis=0, keepdims=True)
            db_ref[0:1, cs] += jnp.sum(dmain, axis=0, keepdims=True)
            dglu_ref[:, cs] = (dy * sb).astype(BF16)
            dglu_ref[:, ch + h * HALF:ch + (h + 1) * HALF] = (dy * av * sb * (1.0 - sb)).astype(BF16)

    in_specs = []
    for h in range(nh):
        in_specs += _halo_specs(tt, n, h)
    for h in range(2 * nh):
        in_specs += _halo_specs(tt, n, glu_block + h)
    in_specs += [pl.BlockSpec((CONV_WIDTH, ch), lambda i: (0, 0))]
    ops = [dycv] * (3 * nh) + [p] * (6 * nh) + [dw_w]
    return pl.pallas_call(
        body, name=name, grid=(n // tt,), in_specs=in_specs,
        out_specs=[pl.BlockSpec((tt, 2 * ch), lambda i: (i, 0)), pl.BlockSpec((32, ch), lambda i: (0, 0)),
                   pl.BlockSpec((8, ch), lambda i: (0, 0))],
        out_shape=[jax.ShapeDtypeStruct((n, 2 * ch), BF16), jax.ShapeDtypeStruct((32, ch), F32),
                   jax.ShapeDtypeStruct((8, ch), F32)],
        scratch_shapes=[pltpu.VMEM((tt + 2 * HALO, HALF), F32), pltpu.VMEM((tt + 2 * HALO, HALF), F32)],
        compiler_params=_params("arbitrary"),
    )(*ops)


def _sgu_common(p_ref, g_ref, be_ref, ws_ref, bs_ref, w):
    gw = w // SGU_GROUPS
    u_pre = p_ref[:, 0:w]
    v_pre = p_ref[:, w:2 * w]
    zc = p_ref[:, 2 * w:3 * w]
    u = _gelu(u_pre)
    v = _gelu(v_pre)
    mu = jnp.mean(v, axis=-1, keepdims=True)
    var = jnp.mean(jnp.square(v - mu), axis=-1, keepdims=True)
    rstd = lax.rsqrt(var + EPS)
    vhat = (v - mu) * rstd
    vn = (vhat * g_ref[...] + be_ref[...]).astype(BF16)
    mixed = jnp.concatenate(
        [jnp.dot(ws_ref[gi].astype(BF16), vn[:, gi * gw:(gi + 1) * gw], preferred_element_type=F32)
         + bs_ref[:, gi:gi + 1] for gi in range(SGU_GROUPS)], axis=-1)
    return u_pre, v_pre, zc, u, rstd, vhat, vn, mixed


def sgu_fwd(p, ln_g, ln_b, ws, bs_t, *, name):
    n, w3 = p.shape
    w = w3 // 3

    def body(p_ref, g_ref, be_ref, ws_ref, bs_ref, m_ref):
        _, _, zc, u, _, _, _, mixed = _sgu_common(p_ref, g_ref, be_ref, ws_ref, bs_ref, w)
        m_ref[...] = (u * mixed * _silu(zc)).astype(BF16)

    vec = pl.BlockSpec((1, w), lambda i: (0, 0))
    return pl.pallas_call(
        body, name=name, grid=(n // CHUNK,),
        in_specs=[pl.BlockSpec((CHUNK, w3), lambda i: (i, 0)), vec, vec,
                  pl.BlockSpec((SGU_GROUPS, CHUNK, CHUNK), lambda i: (0, 0, 0)),
                  pl.BlockSpec((CHUNK, SGU_GROUPS), lambda i: (0, 0))],
        out_specs=pl.BlockSpec((CHUNK, w), lambda i: (i, 0)),
        out_shape=jax.ShapeDtypeStruct((n, w), BF16),
        compiler_params=_params("parallel"),
    )(p, _row(ln_g), _row(ln_b), ws, bs_t)


def sgu_bwd(p, dm, ln_g, ln_b, ws, ws_t, bs_t, *, name):
    n, w3 = p.shape
    w = w3 // 3
    gw = w // SGU_GROUPS

    def body(p_ref, dm_ref, g_ref, be_ref, ws_ref, wst_ref, bs_ref, dp_ref, dws_ref, dbs_ref, acc_ref):
        i = pl.program_id(0)

        @pl.when(i == 0)
        def _():
            dws_ref[...] = jnp.zeros_like(dws_ref)
            dbs_ref[...] = jnp.zeros_like(dbs_ref)
            acc_ref[...] = jnp.zeros_like(acc_ref)

        u_pre, v_pre, zc, u, rstd, vhat, vn, mixed = _sgu_common(p_ref, g_ref, be_ref, ws_ref, bs_ref, w)
        dmv = dm_ref[...]
        um = u * mixed
        dp_ref[:, 2 * w:3 * w] = (dmv * um * _dsilu(zc)).astype(BF16)
        dum = dmv * _silu(zc)
        dp_ref[:, 0:w] = (dum * mixed * _dgelu(u_pre)).astype(BF16)
        dmixed = dum * u
        dmixed_b = dmixed.astype(BF16)
        dvn_parts = []
        for gi in range(SGU_GROUPS):
            cs = slice(gi * gw, (gi + 1) * gw)
            dws_ref[gi] += lax.dot_general(dmixed_b[:, cs], vn[:, cs], _DIMS["nt"], preferred_element_type=F32)
            dbs_ref[:, gi:gi + 1] += jnp.sum(dmixed[:, cs], axis=-1, keepdims=True)
            dvn_parts.append(jnp.dot(wst_ref[gi].astype(BF16), dmixed_b[:, cs], preferred_element_type=F32))
        dvn = jnp.concatenate(dvn_parts, axis=-1)
        acc_ref[0:1, :] += jnp.sum(dvn * vhat, axis=0, keepdims=True)
        acc_ref[1:2, :] += jnp.sum(dvn, axis=0, keepdims=True)
        dvhat = dvn * g_ref[...]
        dv = rstd * (dvhat - jnp.mean(dvhat, axis=-1, keepdims=True)
                     - vhat * jnp.mean(dvhat * vhat, axis=-1, keepdims=True))
        dp_ref[:, w:2 * w] = (dv * _dgelu(v_pre)).astype(BF16)

    vec = pl.BlockSpec((1, w), lambda i: (0, 0))
    wspec = pl.BlockSpec((SGU_GROUPS, CHUNK, CHUNK), lambda i: (0, 0, 0))
    bspec = pl.BlockSpec((CHUNK, SGU_GROUPS), lambda i: (0, 0))
    return pl.pallas_call(
        body, name=name, grid=(n // CHUNK,),
        in_specs=[pl.BlockSpec((CHUNK, w3), lambda i: (i, 0)), pl.BlockSpec((CHUNK, w), lambda i: (i, 0)),
                  vec, vec, wspec, wspec, bspec],
        out_specs=[pl.BlockSpec((CHUNK, w3), lambda i: (i, 0)), wspec, bspec,
                   pl.BlockSpec((8, w), lambda i: (0, 0))],
        out_shape=[jax.ShapeDtypeStruct((n, w3), BF16), jax.ShapeDtypeStruct((SGU_GROUPS, CHUNK, CHUNK), F32),
                   jax.ShapeDtypeStruct((CHUNK, SGU_GROUPS), F32), jax.ShapeDtypeStruct((8, w), F32)],
        compiler_params=_params("arbitrary"),
    )(p, dm, _row(ln_g), _row(ln_b), ws, ws_t, bs_t)


def _adam_math(w, g, m, v):
    m_new = ADAM_B1 * m + (1.0 - ADAM_B1) * g
    v_new = ADAM_B2 * v + (1.0 - ADAM_B2) * (g * g)
    m_hat = m_new / (1.0 - ADAM_B1 ** ADAM_STEP)
    v_hat = v_new / (1.0 - ADAM_B2 ** ADAM_STEP)
    delta = -ADAM_LR * (m_hat / (jnp.sqrt(v_hat) + ADAM_EPS) + ADAM_WD * w)
    return delta, m_new, v_new


def adamw(w, g, m, v, *, name, slots=False, rows=512):
    r, c = w.shape
    tr = min(r, rows)
    assert r % tr == 0, (name, r, tr)

    def body(w_ref, g_ref, m_ref, v_ref, go_ref, d_ref, mo_ref, vo_ref):
        if slots:
            g = g_ref[0].astype(F32)
            for k in range(1, N_DEV):
                g = g + g_ref[k].astype(F32)
        else:
            g = g_ref[...].astype(F32)
        delta, m_new, v_new = _adam_math(w_ref[...], g, m_ref[...], v_ref[...])
        go_ref[...] = g
        d_ref[...] = delta
        mo_ref[...] = m_new
        vo_ref[...] = v_new

    spec = pl.BlockSpec((tr, c), lambda i: (i, 0))
    gspec = pl.BlockSpec((N_DEV, tr, c), lambda i: (0, i, 0)) if slots else spec
    shape = jax.ShapeDtypeStruct((r, c), F32)
    return pl.pallas_call(
        body, name=name, grid=(r // tr,),
        in_specs=[spec, gspec, spec, spec], out_specs=[spec] * 4, out_shape=[shape] * 4,
        compiler_params=_params("parallel"),
    )(w, g, m, v)


def _pad_rows(a, rows):
    return jnp.pad(a, ((0, rows - a.shape[0]), (0, 0)))


def kernel(x, c, ctx, c_ctx, ada_w, ada_b, norm_g, ev_w_in, ev_q_norm, ev_k_norm, ev_dw_w, ev_dw_b, ev_ln_g, ev_ln_b, ev_w_out, od_w_in, od_ln_g, od_ln_b, od_ws, od_bs, od_w_out, final_g, loss_target, m_c_ctx, m_ada_w, m_ada_b, m_norm_g, m_ev_w_in, m_ev_q_norm, m_ev_k_norm, m_ev_dw_w, m_ev_dw_b, m_ev_ln_g, m_ev_ln_b, m_ev_w_out, m_od_w_in, m_od_ln_g, m_od_ln_b, m_od_ws, m_od_bs, m_od_w_out, m_final_g, v_c_ctx, v_ada_w, v_ada_b, v_norm_g, v_ev_w_in, v_ev_q_norm, v_ev_k_norm, v_ev_dw_w, v_ev_dw_b, v_ev_ln_g, v_ev_ln_b, v_ev_w_out, v_od_w_in, v_od_ln_g, v_od_ln_b, v_od_ws, v_od_bs, v_od_w_out, v_final_g):
    n, d = x.shape[1], x.shape[2]
    lc = ctx.shape[1]
    ev_in = ev_w_in.shape[2] * N_DEV
    od_in = od_w_in.shape[2] * N_DEV
    conv_ch = ev_dw_w.shape[2] * N_DEV
    ada_cols = ada_w.shape[2]
    me = 4 * lax.axis_index("x") + 2 * lax.axis_index("y") + lax.axis_index("c")
    xs, tgt, ctxs = x[0], loss_target[0], ctx[0]
    za_block = (2 * KV_W + ATTN_W) // HEAD_DIM
    glu_block = (2 * KV_W + 2 * ATTN_W) // HALF
    zb_block = glu_block + 2 * conv_ch // HALF

    small = jnp.concatenate([
        jax.nn.silu(c).reshape(1, d),
        od_ln_g.reshape(1, -1), od_ln_b.reshape(1, -1)], axis=1)
    small = _pad_rows(small, 8)
    dw_rows = _pad_rows(ev_dw_w[0], 32)
    small_g, dw_g = all_gather([small, dw_rows], name="gather_small", hbm=False)
    sc_all = small_g[:, 0, :d]
    shard = d // N_DEV
    od_ln_g_full = small_g[:, 0, d:d + shard].reshape(d)
    od_ln_b_full = small_g[:, 0, d + shard:d + 2 * shard].reshape(d)
    dw_w_full = jnp.moveaxis(dw_g, 0, 1).reshape(32, conv_ch)[:CONV_WIDTH]
    scc = jax.nn.silu(c_ctx)
    sc16 = _pad_rows(jnp.concatenate([sc_all, scc.reshape(1, d)], axis=0), 16)

    ada_bf = ada_w.astype(BF16)
    mod_loc = [matmul(sc16, ada_bf[l], mode="nn", tm=16, tn=ada_cols, tk=d, out_dtype=F32, name=f"ada_mod{l}")
               for l in range(2)]
    (mod_g,) = all_gather([jnp.stack(mod_loc)], name="gather_mod", hbm=False)
    mod_all = jnp.moveaxis(mod_g, 0, 2).reshape(2, 16, N_DEV * ada_cols) + ada_b[:, None, :]
    mod_me = lax.dynamic_index_in_dim(mod_all, me, axis=1, keepdims=False)
    shift = [mod_me[l, :d] for l in range(2)]
    scale = [mod_me[l, d:2 * d] for l in range(2)]
    gate = [mod_me[l, 2 * d:] for l in range(2)]
    shift_c, scale_c = mod_all[0, 8, :d], mod_all[0, 8, d:2 * d]

    w_parts = [ev_w_in[0].astype(BF16), od_w_in[0].astype(BF16), ev_w_out[0].astype(BF16), od_w_out[0].astype(BF16)]
    wi0_g, wi1_g, wo0_g, wo1_g = all_gather(w_parts, name="gather_weights", hbm=True)
    wi0 = jnp.moveaxis(wi0_g, 0, 1).reshape(d, ev_in)
    wi1 = jnp.moveaxis(wi1_g, 0, 1).reshape(d, od_in)
    wo0 = wo0_g.reshape(-1, d)
    wo1 = wo1_g.reshape(-1, d)

    cexp, sexp = rope_tables(n)

    h0 = norm_mod_fwd(xs, norm_g[0], shift[0], scale[0], name="norm_mod_fwd0")
    hc = norm_mod_fwd(ctxs, norm_g[0], shift_c, scale_c, name="norm_mod_fwd_ctx")
    p0 = matmul(h0, wi0, mode="nn", tm=512, tn=512, tk=d, out_dtype=F32, name="proj_in0")
    pc = matmul(hc, wi0, mode="nn", tm=lc, tn=2 * KV_W, tk=d, out_dtype=F32, n_out=2 * KV_W, name="proj_in_ctx")
    q_r, k_lat, v_lat = qkv_prep_fwd(p0, ev_q_norm[0], ev_k_norm[0], cexp, sexp, latent=True, name="qkv_prep")
    k_ctx, v_ctx = qkv_prep_fwd(pc, None, ev_k_norm[0], None, None, latent=False, name="kv_prep_ctx")
    k_all = jnp.concatenate([k_lat, k_ctx], axis=0)
    v_all = jnp.concatenate([v_lat, v_ctx], axis=0)
    o_attn, mix_a, lse = flash_fwd(q_r, k_all, v_all, p0, za_block=za_block, name="flash_fwd")
    ycv, mix_b = conv_fwd(p0, dw_w_full, ev_dw_b[0], ev_ln_g[0], ev_ln_b[0], glu_block=glu_block, name="conv_fwd")
    mix0 = jnp.concatenate([mix_a, mix_b], axis=1)
    x1, r0 = matmul(mix0, wo0, mode="nn", tm=512, tn=512, tk=mix0.shape[1], out_dtype=F32, name="proj_out0",
                    res=xs, gate=_row(gate[0]))

    h1 = norm_mod_fwd(x1, norm_g[1], shift[1], scale[1], name="norm_mod_fwd1")
    p1 = matmul(h1, wi1, mode="nn", tm=512, tn=512, tk=d, out_dtype=F32, name="proj_in1")
    ws_bf = od_ws[0]
    bs_t = od_bs[0].T
    m1 = sgu_fwd(p1, od_ln_g_full, od_ln_b_full, ws_bf, bs_t, name="sgu_fwd")
    x2, r1 = matmul(m1, wo1, mode="nn", tm=512, tn=512, tk=m1.shape[1], out_dtype=F32, name="proj_out1",
                    res=x1, gate=_row(gate[1]))

    dx2, acc_final, loss_tile = final_loss(x2, tgt, final_g, name="final_loss")

    dr1, acc_gate1 = gate_bwd(dx2, r1, gate[1], name="gate_bwd1")
    dm1 = matmul(dr1, wo1, mode="nt", tm=512, tn=512, tk=d, out_dtype=F32, name="d_mix1")
    dwo1 = matmul(m1, dr1, mode="tn", tm=512, tn=512, tk=512, out_dtype=BF16, name="d_wout1")
    dp1, dws, dbs_t, acc_sgu = sgu_bwd(p1, dm1, od_ln_g_full, od_ln_b_full, ws_bf, jnp.swapaxes(ws_bf, 1, 2), bs_t,
                                       name="sgu_bwd")
    dh1 = matmul(dp1, wi1, mode="nt", tm=512, tn=512, tk=1024, out_dtype=F32, name="d_h1")
    dwi1 = matmul(h1, dp1, mode="tn", tm=512, tn=od_in // N_DEV, tk=512, out_dtype=BF16, name="d_win1")
    dx1, acc_norm1 = norm_mod_bwd(x1, dh1, dx2, norm_g[1], scale[1], name="norm_mod_bwd1")

    dr0, acc_gate0 = gate_bwd(dx1, r0, gate[0], name="gate_bwd0")
    dmix0 = matmul(dr0, wo0, mode="nt", tm=512, tn=512, tk=d, out_dtype=F32, name="d_mix0")
    dwo0 = matmul(mix0, dr0, mode="tn", tm=512, tn=512, tk=512, out_dtype=BF16, name="d_wout0")
    do_attn, dza, delta = attn_gate_bwd(dmix0, o_attn, p0, za_block=za_block, name="attn_gate_bwd")
    dycv, dzb, acc_ln = conv_bwd_rows(dmix0, ycv, p0, ev_ln_g[0], ev_ln_b[0], mix_block=ATTN_W // conv_ch,
                                      zb_block=zb_block, name="conv_bwd_rows")
    dglu, ddw_w, acc_dwb = conv_bwd_taps(dycv, p0, dw_w_full, glu_block=glu_block, name="conv_bwd_taps")
    lse_row = lse.reshape(N_Q_HEADS, 1, n)
    delta_row = delta.reshape(N_Q_HEADS, 1, n)
    dq_t, dk_all, dv_all = flash_bwd(q_r, do_attn, lse_row, delta_row, k_all, v_all, k_all.T, name="flash_bwd")
    dq_r = jnp.moveaxis(dq_t, 2, 0).reshape(n, ATTN_W)
    dkvq, acc_qk = qkv_prep_bwd(p0, dq_r, dk_all[:n], dv_all[:n], ev_q_norm[0], ev_k_norm[0], cexp, sexp,
                                latent=True, name="qkv_prep_bwd")
    dpc, acc_kc = qkv_prep_bwd(pc, None, dk_all[n:], dv_all[n:], None, ev_k_norm[0], None, None,
                               latent=False, name="kv_prep_ctx_bwd")
    dp0 = jnp.concatenate([dkvq, dza, dglu, dzb], axis=1)
    dh0 = matmul(dp0, wi0, mode="nt", tm=512, tn=512, tk=512, out_dtype=F32, name="d_h0")
    dwi0 = matmul(h0, dp0, mode="tn", tm=512, tn=512, tk=512, out_dtype=F32, name="d_win0")
    dwi0 = matmul(hc, dpc, mode="tn", tm=512, tn=2 * KV_W, tk=lc, out_dtype=F32, name="d_win0_ctx", add=dwi0)
    dhc = matmul(dpc, wi0, mode="nt", tm=lc, tn=512, tk=2 * KV_W, out_dtype=F32, name="d_h_ctx")
    grad_x, acc_norm0 = norm_mod_bwd(xs, dh0, dx1, norm_g[0], scale[0], name="norm_mod_bwd0")
    _, acc_normc = norm_mod_bwd(ctxs, dhc, None, norm_g[0], scale_c, name="norm_mod_bwd_ctx")

    zeros_d = jnp.zeros((d,), F32)
    dmod0 = jnp.stack([acc_norm0[0], acc_norm0[1], acc_gate0[0]])
    dmod1 = jnp.stack([acc_norm1[0], acc_norm1[1], acc_gate1[0]])
    dmodc = jnp.stack([acc_normc[0], acc_normc[1]])
    half_pad = jnp.zeros((d - 2 * conv_ch,), F32) if d > 2 * conv_ch else jnp.zeros((0,), F32)
    row_a = jnp.concatenate([acc_dwb[0], acc_ln[0], half_pad])
    row_b = jnp.concatenate([acc_ln[1], acc_qk[0], acc_qk[1] + acc_kc[1],
                             jnp.zeros((d - conv_ch - 2 * HEAD_DIM,), F32)])
    row_c = jnp.concatenate([dbs_t.T.reshape(-1), jnp.zeros((d - SGU_GROUPS * CHUNK,), F32)])
    row_loss = jnp.concatenate([loss_tile[0, :1], jnp.zeros((d - 1,), F32)])
    pack = jnp.concatenate([
        dmod0, dmod1, dmodc,
        (acc_norm0[2] + acc_normc[2])[None], acc_norm1[2][None],
        acc_final[0][None],
        acc_sgu[0][None], acc_sgu[1][None],
        row_a[None], row_b[None], row_c[None], row_loss[None],
        ddw_w.reshape(-1, d),
        dws.reshape(-1, d),
    ], axis=0)
    n_rows = pack.shape[0]
    pack = _pad_rows(pack, -(-n_rows // 8) * 8)
    pack_g, pack_sum = all_gather([pack], name="gather_small_grads", hbm=False, sum_out=True)
    gsum = pack_sum
    loss = gsum[16, 0]
    dw_rows_n = 32 * conv_ch // d
    g_dw_w = gsum[17:17 + dw_rows_n].reshape(32, conv_ch)[:CONV_WIDTH]
    g_od_ws = gsum[17 + dw_rows_n:17 + dw_rows_n + SGU_GROUPS * CHUNK * CHUNK // d].reshape(od_ws.shape)

    dmodc_sum = jnp.concatenate([gsum[6], gsum[7], zeros_d])
    col0 = me * ada_cols
    dm_cols = []
    for l in range(2):
        rows = pack_g[:, 3 * l:3 * l + 3, :].reshape(N_DEV, 3 * d)
        extra = dmodc_sum[None] if l == 0 else jnp.zeros((1, 3 * d), F32)
        full = _pad_rows(jnp.concatenate([rows, extra], axis=0), 16)
        dm_cols.append(lax.dynamic_slice_in_dim(full, col0, ada_cols, axis=1))
    g_ada_w = jnp.stack([matmul(sc16, dm_cols[l], mode="tn", tm=512, tn=ada_cols, tk=16, out_dtype=F32,
                                name=f"d_ada_w{l}") for l in range(2)])
    dsc = matmul(dm_cols[0], ada_bf[0], mode="nt", tm=16, tn=512, tk=ada_cols, out_dtype=F32, name="d_scc")
    (_, dscc_sum) = all_gather([dsc[8:16]], name="gather_dscc", hbm=False, sum_out=True)
    sg = jax.nn.sigmoid(c_ctx)
    g_c_ctx = dscc_sum[0] * (sg * (1.0 + c_ctx * (1.0 - sg)))
    g_ada_b = jnp.stack([gsum[0:3].reshape(-1) + dmodc_sum, gsum[3:6].reshape(-1)])

    dwi0_s = jnp.moveaxis(dwi0.astype(BF16).reshape(d, N_DEV, ev_in // N_DEV), 1, 0)
    dwi1_s = jnp.moveaxis(dwi1.reshape(d, N_DEV, od_in // N_DEV), 1, 0)
    dwo0_s = dwo0.reshape(N_DEV, -1, d)
    dwo1_s = dwo1.reshape(N_DEV, -1, d)
    gi0, gi1, go0, go1 = all_to_all([dwi0_s, dwi1_s, dwo0_s, dwo1_s], name="exchange_weight_grads")

    out = {}

    def upd(key, w, g, m, v, slots=False, rows=512):
        shp = w.shape
        w2 = w.reshape(-1, shp[-1])
        g2 = g.reshape((N_DEV, -1, shp[-1])) if slots else g.reshape(-1, shp[-1])
        res = adamw(w2, g2, m.reshape(w2.shape), v.reshape(w2.shape), name="adamw_" + key, slots=slots, rows=rows)
        out[key] = tuple(r.reshape(shp) for r in res)

    upd("ev_w_in", ev_w_in, gi0, m_ev_w_in, v_ev_w_in, slots=True, rows=256)
    upd("od_w_in", od_w_in, gi1, m_od_w_in, v_od_w_in, slots=True, rows=256)
    upd("ev_w_out", ev_w_out, go0, m_ev_w_out, v_ev_w_out, slots=True, rows=256)
    upd("od_w_out", od_w_out, go1, m_od_w_out, v_od_w_out, slots=True, rows=256)
    upd("ada_w", ada_w, g_ada_w, m_ada_w, v_ada_w)

    def my_shard(full, size):
        return lax.dynamic_slice_in_dim(full, me * size, size, axis=full.ndim - 1)

    small_items = [
        ("c_ctx", c_ctx, g_c_ctx, m_c_ctx, v_c_ctx),
        ("ada_b", ada_b, g_ada_b, m_ada_b, v_ada_b),
        ("norm_g", norm_g, gsum[8:10], m_norm_g, v_norm_g),
        ("ev_q_norm", ev_q_norm, gsum[14, conv_ch:conv_ch + HEAD_DIM], m_ev_q_norm, v_ev_q_norm),
        ("ev_k_norm", ev_k_norm, gsum[14, conv_ch + HEAD_DIM:conv_ch + 2 * HEAD_DIM], m_ev_k_norm, v_ev_k_norm),
        ("ev_dw_w", ev_dw_w, my_shard(g_dw_w, conv_ch // N_DEV), m_ev_dw_w, v_ev_dw_w),
        ("ev_dw_b", ev_dw_b, gsum[13, :conv_ch], m_ev_dw_b, v_ev_dw_b),
        ("ev_ln_g", ev_ln_g, gsum[13, conv_ch:2 * conv_ch], m_ev_ln_g, v_ev_ln_g),
        ("ev_ln_b", ev_ln_b, gsum[14, :conv_ch], m_ev_ln_b, v_ev_ln_b),
        ("od_ln_g", od_ln_g, my_shard(gsum[11], shard), m_od_ln_g, v_od_ln_g),
        ("od_ln_b", od_ln_b, my_shard(gsum[12], shard), m_od_ln_b, v_od_ln_b),
        ("od_ws", od_ws, g_od_ws, m_od_ws, v_od_ws),
        ("od_bs", od_bs, gsum[15, :SGU_GROUPS * CHUNK], m_od_bs, v_od_bs),
        ("final_g", final_g, gsum[10], m_final_g, v_final_g),
    ]
    sizes = [it[1].size for it in small_items]
    total = sum(sizes)
    lanes = 1024
    prow = -(-total // lanes)
    prow = -(-prow // 8) * 8

    def pack_small(idx):
        flat = jnp.concatenate([it[idx].reshape(-1).astype(F32) for it in small_items])
        return jnp.pad(flat, (0, prow * lanes - total)).reshape(prow, lanes)

    sres = adamw(pack_small(1), pack_small(2), pack_small(3), pack_small(4), name="adamw_small", rows=prow)
    off = 0
    for it, size in zip(small_items, sizes):
        out[it[0]] = tuple(r.reshape(-1)[off:off + size].reshape(it[1].shape) for r in sres)
        off += size

    names = ['c_ctx', 'ada_w', 'ada_b', 'norm_g', 'ev_w_in', 'ev_q_norm', 'ev_k_norm', 'ev_dw_w', 'ev_dw_b',
             'ev_ln_g', 'ev_ln_b', 'ev_w_out', 'od_w_in', 'od_ln_g', 'od_ln_b', 'od_ws', 'od_bs', 'od_w_out',
             'final_g']
    return (loss, grad_x[None], *[out[k][0] for k in names], *[out[k][1] for k in names],
            *[out[k][2] for k in names], *[out[k][3] for k in names])
```

```python
import functools
import math

import jax
import jax.numpy as jnp
from jax import lax
from jax.experimental import pallas as pl
from jax.experimental.pallas import tpu as pltpu

F32 = jnp.float32
BF16 = jnp.bfloat16
MESH = pl.DeviceIdType.MESH

EPS = 1e-6
HEAD_DIM = 128
N_Q_HEADS = 8
N_KV_HEADS = 2
Q_PER_KV = N_Q_HEADS // N_KV_HEADS
ATTN_W = N_Q_HEADS * HEAD_DIM
KV_W = N_KV_HEADS * HEAD_DIM
ATTN_SCALE = HEAD_DIM ** -0.5
LN2 = math.log(2.0)
Q_SCALE = ATTN_SCALE / LN2
ROPE_THETA = 10000.0
GRID_W = 64
CONV_WIDTH = 31
CONV_HALF = CONV_WIDTH // 2
HALO = 16
CHUNK = 128
SGU_GROUPS = 8
N_DEV = 8

ADAM_LR = 0.001
ADAM_B1 = 0.9
ADAM_B2 = 0.999
ADAM_EPS = 1e-08
ADAM_WD = 0.01
ADAM_STEP = 10

VMEM_LIMIT = 56 * 1024 * 1024
ANY = pl.BlockSpec(memory_space=pl.ANY)
VMEM_SPEC = pl.BlockSpec(memory_space=pltpu.VMEM)


def _params(*sem):
    return pltpu.CompilerParams(dimension_semantics=sem, vmem_limit_bytes=VMEM_LIMIT)


def _sigmoid(x):
    return 1.0 / (1.0 + jnp.exp(-x))


def _silu(x):
    return x * _sigmoid(x)


def _dsilu(x):
    s = _sigmoid(x)
    return s * (1.0 + x * (1.0 - s))


_GELU_C = math.sqrt(2.0 / math.pi)


def _gelu(x):
    t = jnp.tanh(_GELU_C * (x + 0.044715 * (x * x * x)))
    return 0.5 * x * (1.0 + t)


def _dgelu(x):
    t = jnp.tanh(_GELU_C * (x + 0.044715 * (x * x * x)))
    return 0.5 * (1.0 + t) + 0.5 * x * (1.0 - t * t) * (_GELU_C * (1.0 + 3.0 * 0.044715 * (x * x)))


def _row(v):
    return v.reshape(1, -1).astype(F32)


def _flat_id(p):
    return 4 * p[0] + 2 * p[1] + p[2]


def _gather_body(n_arr, sum_out):
    def body(*refs):
        x_refs = refs[:n_arr]
        out_refs = refs[n_arr:2 * n_arr]
        pos = 2 * n_arr
        sum_refs = refs[pos:pos + n_arr] if sum_out else ()
        pos += n_arr if sum_out else 0
        send_sems, recv_sems, local_sems = refs[pos:pos + 3]
        x, y, c = lax.axis_index("x"), lax.axis_index("y"), lax.axis_index("c")
        me, sibling = (x, y, c), (x, y, 1 - c)
        chips = [(1 - x, y), (x, 1 - y), (1 - x, 1 - y)]

        def copy(a, k, block, to, src=None):
            rows = out_refs[a].at[_flat_id(block)]
            return pltpu.make_async_remote_copy(
                src_ref=rows if src is None else src, dst_ref=rows,
                send_sem=send_sems.at[a, k], recv_sem=recv_sems.at[a, k],
                device_id=to, device_id_type=MESH)

        sends = []
        mine = []
        for a in range(n_arr):
            cp = pltpu.make_async_copy(x_refs[a], out_refs[a].at[_flat_id(me)], local_sems.at[a])
            cp.start()
            mine.append(cp)
            first = [copy(a, 0, me, sibling, src=x_refs[a])]
            first += [copy(a, 1 + j, me, (*chip, c), src=x_refs[a]) for j, chip in enumerate(chips)]
            for cp in first:
                cp.start()
            sends += first
        for a in range(n_arr):
            for j, chip in enumerate(chips):
                copy(a, 1 + j, (*chip, c), me).wait_recv()
                fwd = copy(a, 4 + j, (*chip, c), sibling)
                fwd.start()
                sends.append(fwd)
        for a in range(n_arr):
            copy(a, 0, sibling, me).wait_recv()
            for j, chip in enumerate(chips):
                copy(a, 4 + j, (*chip, 1 - c), me).wait_recv()
        for cp in sends:
            cp.wait_send()
        for cp in mine:
            cp.wait()
        if sum_out:
            for a in range(n_arr):
                acc = out_refs[a][0]
                for k in range(1, N_DEV):
                    acc = acc + out_refs[a][k]
                sum_refs[a][...] = acc

    return body


def all_gather(arrs, *, name, hbm, sum_out=False):
    n = len(arrs)
    spec = ANY if hbm else VMEM_SPEC
    out_shape = [jax.ShapeDtypeStruct((N_DEV,) + a.shape, a.dtype) for a in arrs]
    out_specs = [spec] * n
    if sum_out:
        out_shape += [jax.ShapeDtypeStruct(a.shape, a.dtype) for a in arrs]
        out_specs += [VMEM_SPEC] * n
    res = pl.pallas_call(
        _gather_body(n, sum_out), name=name,
        out_shape=out_shape, in_specs=[spec] * n, out_specs=out_specs,
        scratch_shapes=[pltpu.SemaphoreType.DMA((n, 7)), pltpu.SemaphoreType.DMA((n, 7)),
                        pltpu.SemaphoreType.DMA((n,))],
        compiler_params=pltpu.CompilerParams(vmem_limit_bytes=VMEM_LIMIT),
    )(*arrs)
    return res


_RELATIONS = [(rx, ry, rc) for rx in (0, 1) for ry in (0, 1) for rc in (0, 1)][1:]


def all_to_all(arrs, *, name):
    n = len(arrs)

    def body(*refs):
        x_refs = refs[:n]
        out_refs = refs[n:2 * n]
        send_sems, recv_sems, local_sems = refs[2 * n:2 * n + 3]
        x, y, c = lax.axis_index("x"), lax.axis_index("y"), lax.axis_index("c")
        me = _flat_id((x, y, c))
        local, remote = [], []
        for a in range(n):
            cp = pltpu.make_async_copy(x_refs[a].at[me], out_refs[a].at[me], local_sems.at[a])
            cp.start()
            local.append(cp)
            for k, (rx, ry, rc) in enumerate(_RELATIONS):
                peer = (1 - x if rx else x, 1 - y if ry else y, 1 - c if rc else c)
                cp = pltpu.make_async_remote_copy(
                    src_ref=x_refs[a].at[_flat_id(peer)], dst_ref=out_refs[a].at[me],
                    send_sem=send_sems.at[a, k], recv_sem=recv_sems.at[a, k],
                    device_id=peer, device_id_type=MESH)
                cp.start()
                remote.append(cp)
        for a in range(n):
            for k, (rx, ry, rc) in enumerate(_RELATIONS):
                peer = (1 - x if rx else x, 1 - y if ry else y, 1 - c if rc else c)
                pltpu.make_async_remote_copy(
                    src_ref=x_refs[a].at[me], dst_ref=out_refs[a].at[_flat_id(peer)],
                    send_sem=send_sems.at[a, k], recv_sem=recv_sems.at[a, k],
                    device_id=peer, device_id_type=MESH).wait_recv()
        for cp in remote:
            cp.wait_send()
        for cp in local:
            cp.wait()

    return pl.pallas_call(
        body, name=name,
        out_shape=[jax.ShapeDtypeStruct(a.shape, a.dtype) for a in arrs],
        in_specs=[ANY] * n, out_specs=[ANY] * n,
        scratch_shapes=[pltpu.SemaphoreType.DMA((n, 7)), pltpu.SemaphoreType.DMA((n, 7)),
                        pltpu.SemaphoreType.DMA((n,))],
    )(*arrs)


_DIMS = {"nn": (((1,), (0,)), ((), ())), "nt": (((1,), (1,)), ((), ())), "tn": (((0,), (0,)), ((), ()))}


def matmul(a, b, *, mode, tm, tn, tk, out_dtype, name, n_out=None, res=None, gate=None, add=None):
    if mode == "tn":
        kdim, m = a.shape
    else:
        m, kdim = a.shape
    nfull = b.shape[0] if mode == "nt" else b.shape[1]
    n = nfull if n_out is None else n_out
    tm, tn, tk = min(tm, m), min(tn, n), min(tk, kdim)
    assert m % tm == 0 and n % tn == 0 and kdim % tk == 0, (name, m, n, kdim, tm, tn, tk)
    nk = kdim // tk
    dims = _DIMS[mode]
    a_spec = pl.BlockSpec((tk, tm), lambda j, i, k: (k, i)) if mode == "tn" else pl.BlockSpec((tm, tk), lambda j, i, k: (i, k))
    b_spec = pl.BlockSpec((tn, tk), lambda j, i, k: (j, k)) if mode == "nt" else pl.BlockSpec((tk, tn), lambda j, i, k: (k, j))
    o_spec = pl.BlockSpec((tm, tn), lambda j, i, k: (i, j))
    in_specs = [a_spec, b_spec]
    operands = [a, b]
    aliases = {}
    if res is not None:
        in_specs += [o_spec, pl.BlockSpec((1, tn), lambda j, i, k: (0, j))]
        operands += [res, gate]
    if add is not None:
        in_specs += [o_spec]
        aliases = {len(operands): 0}
        operands += [add]
    out_cols = n if add is None else add.shape[1]
    out_shape = [jax.ShapeDtypeStruct((m, out_cols), out_dtype)]
    out_specs = [o_spec]
    if res is not None:
        out_shape.append(jax.ShapeDtypeStruct((m, n), BF16))
        out_specs.append(o_spec)

    def body(*refs):
        a_ref, b_ref = refs[:2]
        pos = 2
        if res is not None:
            res_ref, gate_ref = refs[pos:pos + 2]
            pos += 2
        if add is not None:
            add_ref = refs[pos]
            pos += 1
        o_ref = refs[pos]
        pos += 1
        if res is not None:
            r_ref = refs[pos]
            pos += 1
        acc_ref = refs[pos] if nk > 1 else None
        prod = lax.dot_general(a_ref[...].astype(BF16), b_ref[...].astype(BF16), dims,
                               preferred_element_type=F32)

        def finish(acc):
            if res is not None:
                o_ref[...] = (res_ref[...] + gate_ref[...] * acc).astype(out_dtype)
                r_ref[...] = acc.astype(BF16)
            elif add is not None:
                o_ref[...] = (add_ref[...] + acc).astype(out_dtype)
            else:
                o_ref[...] = acc.astype(out_dtype)

        if nk == 1:
            finish(prod)
        else:
            k = pl.program_id(2)

            @pl.when(k == 0)
            def _():
                acc_ref[...] = prod

            @pl.when(k > 0)
            def _():
                acc_ref[...] += prod

            @pl.when(k == nk - 1)
            def _():
                finish(acc_ref[...])

    outs = pl.pallas_call(
        body, name=name, grid=(n // tn, m // tm, nk),
        in_specs=in_specs, out_specs=out_specs, out_shape=out_shape,
        scratch_shapes=[pltpu.VMEM((tm, tn), F32)] if nk > 1 else [],
        input_output_aliases=aliases,
        compiler_params=_params("parallel", "parallel", "arbitrary"),
    )(*operands)
    return outs if res is not None else outs[0]


def _rows_tile(n, want):
    return min(n, want)


def norm_mod_fwd(x, g, shift, scale, *, name):
    n, d = x.shape
    tt = _rows_tile(n, 256)

    def body(x_ref, g_ref, sh_ref, sc_ref, h_ref):
        xv = x_ref[...]
        rstd = lax.rsqrt(jnp.mean(xv * xv, axis=-1, keepdims=True) + EPS)
        y = xv * rstd * g_ref[...]
        h_ref[...] = (y * (1.0 + sc_ref[...]) + sh_ref[...]).astype(BF16)

    vec = pl.BlockSpec((1, d), lambda i: (0, 0))
    return pl.pallas_call(
        body, name=name, grid=(n // tt,),
        in_specs=[pl.BlockSpec((tt, d), lambda i: (i, 0)), vec, vec, vec],
        out_specs=pl.BlockSpec((tt, d), lambda i: (i, 0)),
        out_shape=jax.ShapeDtypeStruct((n, d), BF16),
        compiler_params=_params("parallel"),
    )(x, _row(g), _row(shift), _row(scale))


def norm_mod_bwd(x, dh, dres, g, scale, *, name):
    n, d = x.shape
    tt = _rows_tile(n, 256)
    has_res = dres is not None
    last = n // tt - 1

    def body(*refs):
        x_ref, dh_ref = refs[:2]
        pos = 2
        if has_res:
            dres_ref = refs[pos]
            pos += 1
        g_ref, sc_ref, dx_ref, acc_ref, s_ref = refs[pos:pos + 5]
        i = pl.program_id(0)
        xv = x_ref[...]
        dhv = dh_ref[...]
        rstd = lax.rsqrt(jnp.mean(xv * xv, axis=-1, keepdims=True) + EPS)
        xhat = xv * rstd
        dxhat = dhv * (g_ref[...] * (1.0 + sc_ref[...]))
        dx = rstd * (dxhat - xhat * jnp.mean(dxhat * xhat, axis=-1, keepdims=True))
        if has_res:
            dx = dx + dres_ref[...]
        dx_ref[...] = dx

        @pl.when(i == 0)
        def _():
            s_ref[...] = jnp.zeros_like(s_ref)

        s_ref[0:1, :] += jnp.sum(dhv, axis=0, keepdims=True)
        s_ref[1:2, :] += jnp.sum(dhv * xhat, axis=0, keepdims=True)

        @pl.when(i == last)
        def _():
            s1 = s_ref[0:1, :]
            s2 = s_ref[1:2, :]
            acc_ref[...] = jnp.zeros_like(acc_ref)
            acc_ref[0:1, :] = s1
            acc_ref[1:2, :] = s2 * g_ref[...]
            acc_ref[2:3, :] = s2 * (1.0 + sc_ref[...])

    vec = pl.BlockSpec((1, d), lambda i: (0, 0))
    big = pl.BlockSpec((tt, d), lambda i: (i, 0))
    ops = [x, dh] + ([dres] if has_res else []) + [_row(g), _row(scale)]
    return pl.pallas_call(
        body, name=name, grid=(n // tt,),
        in_specs=[big, big] + ([big] if has_res else []) + [vec, vec],
        out_specs=[big, pl.BlockSpec((8, d), lambda i: (0, 0))],
        out_shape=[jax.ShapeDtypeStruct((n, d), F32), jax.ShapeDtypeStruct((8, d), F32)],
        scratch_shapes=[pltpu.VMEM((8, d), F32)],
        compiler_params=_params("arbitrary"),
    )(*ops)


def gate_bwd(dx, r, gate, *, name):
    n, d = dx.shape
    tt = _rows_tile(n, 256)

    def body(dx_ref, r_ref, gate_ref, dr_ref, acc_ref):
        i = pl.program_id(0)
        dxv = dx_ref[...]
        dr_ref[...] = (dxv * gate_ref[...]).astype(BF16)

        @pl.when(i == 0)
        def _():
            acc_ref[...] = jnp.zeros_like(acc_ref)

        acc_ref[0:1, :] += jnp.sum(dxv * r_ref[...].astype(F32), axis=0, keepdims=True)

    big = pl.BlockSpec((tt, d), lambda i: (i, 0))
    return pl.pallas_call(
        body, name=name, grid=(n // tt,),
        in_specs=[big, big, pl.BlockSpec((1, d), lambda i: (0, 0))],
        out_specs=[big, pl.BlockSpec((8, d), lambda i: (0, 0))],
        out_shape=[jax.ShapeDtypeStruct((n, d), BF16), jax.ShapeDtypeStruct((8, d), F32)],
        compiler_params=_params("arbitrary"),
    )(dx, r, _row(gate))


def final_loss(x, target, g, *, name):
    n, d = x.shape
    tt = _rows_tile(n, 256)

    def body(x_ref, t_ref, g_ref, dx_ref, acc_ref, loss_ref):
        i = pl.program_id(0)
        xv = x_ref[...]
        rstd = lax.rsqrt(jnp.mean(xv * xv, axis=-1, keepdims=True) + EPS)
        xhat = xv * rstd
        e = xhat * g_ref[...] - t_ref[...]
        dy = e * (1.0 / d)
        dxhat = dy * g_ref[...]
        dx_ref[...] = rstd * (dxhat - xhat * jnp.mean(dxhat * xhat, axis=-1, keepdims=True))

        @pl.when(i == 0)
        def _():
            acc_ref[...] = jnp.zeros_like(acc_ref)
            loss_ref[...] = jnp.zeros_like(loss_ref)

        acc_ref[0:1, :] += jnp.sum(dy * xhat, axis=0, keepdims=True)
        part = 0.5 * jnp.sum(jnp.mean(e * e, axis=-1, keepdims=True), axis=0, keepdims=True)
        loss_ref[...] += jnp.broadcast_to(part, loss_ref.shape)

    big = pl.BlockSpec((tt, d), lambda i: (i, 0))
    return pl.pallas_call(
        body, name=name, grid=(n // tt,),
        in_specs=[big, big, pl.BlockSpec((1, d), lambda i: (0, 0))],
        out_specs=[big, pl.BlockSpec((8, d), lambda i: (0, 0)), pl.BlockSpec((8, 128), lambda i: (0, 0))],
        out_shape=[jax.ShapeDtypeStruct((n, d), F32), jax.ShapeDtypeStruct((8, d), F32),
                   jax.ShapeDtypeStruct((8, 128), F32)],
        compiler_params=_params("arbitrary"),
    )(x, target, _row(g))


def _swap_pairs(x):
    lane = lax.broadcasted_iota(jnp.int32, x.shape, 1)
    return jnp.where(lane % 2 == 0, pltpu.roll(x, HEAD_DIM - 1, 1), pltpu.roll(x, 1, 1))


def rope_tables(n):
    rows = n // GRID_W
    row = jnp.repeat(jnp.arange(rows, dtype=F32), GRID_W)
    col = jnp.tile(jnp.arange(GRID_W, dtype=F32), rows)
    n_freq = HEAD_DIM // 4
    inv = jnp.power(ROPE_THETA, jnp.arange(n_freq, dtype=F32) * (-2.0 / (HEAD_DIM // 2)))
    ang = jnp.concatenate([row[:, None] * inv, col[:, None] * inv], axis=-1)
    cos, sin = jnp.cos(ang), jnp.sin(ang)
    cexp = jnp.repeat(cos, 2, axis=-1)
    sexp = jnp.stack([-sin, sin], axis=-1).reshape(n, HEAD_DIM)
    return cexp, sexp


def qkv_prep_fwd(p, wq, wk, cexp, sexp, *, latent, name):
    n = p.shape[0]
    tt = _rows_tile(n, 256)
    width = 2 * KV_W + (ATTN_W if latent else 0)

    def body(*refs):
        if latent:
            p_ref, wq_ref, wk_ref, c_ref, s_ref, q_ref, k_ref, v_ref = refs
        else:
            p_ref, wk_ref, k_ref, v_ref = refs

        def head(xv, w):
            rstd = lax.rsqrt(jnp.mean(xv * xv, axis=-1, keepdims=True) + EPS)
            yv = xv * rstd * w
            if latent:
                yv = yv * c_ref[...] + _swap_pairs(yv) * s_ref[...]
            return yv

        for h in range(N_KV_HEADS):
            sl = slice(h * HEAD_DIM, (h + 1) * HEAD_DIM)
            k_ref[:, sl] = head(p_ref[:, sl], wk_ref[...]).astype(BF16)
        v_ref[...] = p_ref[:, KV_W:2 * KV_W].astype(BF16)
        if latent:
            for h in range(N_Q_HEADS):
                sl = slice(2 * KV_W + h * HEAD_DIM, 2 * KV_W + (h + 1) * HEAD_DIM)
                q_ref[:, h * HEAD_DIM:(h + 1) * HEAD_DIM] = (head(p_ref[:, sl], wq_ref[...]) * Q_SCALE).astype(BF16)

    vec = pl.BlockSpec((1, HEAD_DIM), lambda i: (0, 0))
    tab = pl.BlockSpec((tt, HEAD_DIM), lambda i: (i, 0))
    kv_spec = pl.BlockSpec((tt, KV_W), lambda i: (i, 0))
    kv_shape = jax.ShapeDtypeStruct((n, KV_W), BF16)
    if latent:
        in_specs = [pl.BlockSpec((tt, width), lambda i: (i, 0)), vec, vec, tab, tab]
        ops = [p, _row(wq), _row(wk), cexp, sexp]
        out_specs = [pl.BlockSpec((tt, ATTN_W), lambda i: (i, 0)), kv_spec, kv_spec]
        out_shape = [jax.ShapeDtypeStruct((n, ATTN_W), BF16), kv_shape, kv_shape]
    else:
        in_specs = [pl.BlockSpec((tt, width), lambda i: (i, 0)), vec]
        ops = [p, _row(wk)]
        out_specs = [kv_spec, kv_spec]
        out_shape = [kv_shape, kv_shape]
    return pl.pallas_call(
        body, name=name, grid=(n // tt,), in_specs=in_specs, out_specs=out_specs, out_shape=out_shape,
        compiler_params=_params("parallel"),
    )(*ops)


def qkv_prep_bwd(p, dq, dk, dv, wq, wk, cexp, sexp, *, latent, name):
    n = p.shape[0]
    tt = _rows_tile(n, 256)
    width = 2 * KV_W + (ATTN_W if latent else 0)

    def body(*refs):
        if latent:
            p_ref, dq_ref, dk_ref, dv_ref, wq_ref, wk_ref, c_ref, s_ref, dp_ref, acc_ref = refs
        else:
            p_ref, dk_ref, dv_ref, wk_ref, dp_ref, acc_ref = refs
        i = pl.program_id(0)

        @pl.when(i == 0)
        def _():
            acc_ref[...] = jnp.zeros_like(acc_ref)

        def head(xv, dy, w, row):
            if latent:
                dy = dy * c_ref[...] + _swap_pairs(dy * s_ref[...])
            rstd = lax.rsqrt(jnp.mean(xv * xv, axis=-1, keepdims=True) + EPS)
            xhat = xv * rstd
            acc_ref[row:row + 1, :] += jnp.sum(dy * xhat, axis=0, keepdims=True)
            dxhat = dy * w
            return rstd * (dxhat - xhat * jnp.mean(dxhat * xhat, axis=-1, keepdims=True))

        for h in range(N_KV_HEADS):
            sl = slice(h * HEAD_DIM, (h + 1) * HEAD_DIM)
            dp_ref[:, sl] = head(p_ref[:, sl], dk_ref[:, sl], wk_ref[...], 1).astype(BF16)
        dp_ref[:, KV_W:2 * KV_W] = dv_ref[...].astype(BF16)
        if latent:
            for h in range(N_Q_HEADS):
                sl = slice(2 * KV_W + h * HEAD_DIM, 2 * KV_W + (h + 1) * HEAD_DIM)
                dyq = dq_ref[:, h * HEAD_DIM:(h + 1) * HEAD_DIM] * Q_SCALE
                dp_ref[:, sl] = head(p_ref[:, sl], dyq, wq_ref[...], 0).astype(BF16)

    vec = pl.BlockSpec((1, HEAD_DIM), lambda i: (0, 0))
    tab = pl.BlockSpec((tt, HEAD_DIM), lambda i: (i, 0))
    kv_spec = pl.BlockSpec((tt, KV_W), lambda i: (i, 0))
    p_spec = pl.BlockSpec((tt, width), lambda i: (i, 0))
    if latent:
        in_specs = [p_spec, pl.BlockSpec((tt, ATTN_W), lambda i: (i, 0)), kv_spec, kv_spec, vec, vec, tab, tab]
        ops = [p, dq, dk, dv, _row(wq), _row(wk), cexp, sexp]
    else:
        in_specs = [p_spec, kv_spec, kv_spec, vec]
        ops = [p, dk, dv, _row(wk)]
    return pl.pallas_call(
        body, name=name, grid=(n // tt,), in_specs=in_specs,
        out_specs=[p_spec, pl.BlockSpec((8, HEAD_DIM), lambda i: (0, 0))],
        out_shape=[jax.ShapeDtypeStruct((n, width), BF16), jax.ShapeDtypeStruct((8, HEAD_DIM), F32)],
        compiler_params=_params("arbitrary"),
    )(*ops)


def _kv_chunks(n, s_all):
    step = 1024 if n % 1024 == 0 else 256
    chunks = [(s, step) for s in range(0, n, step)]
    if s_all > n:
        chunks.append((n, s_all - n))
    return chunks


def flash_fwd(q, k_all, v_ext, p, *, za_block, name):
    n = q.shape[0]
    s_all = k_all.shape[0]
    tq = _rows_tile(n, 512)
    chunks = _kv_chunks(n, s_all)
    wide = 2 * HEAD_DIM

    def body(q_ref, k_ref, v_ref, za_ref, o_ref, mix_ref, lse_ref, m_ref, acc_ref):
        qv = q_ref[...]
        m_ref[...] = jnp.full_like(m_ref, -jnp.inf)
        acc_ref[...] = jnp.zeros_like(acc_ref)
        for start, size in chunks:
            kc = k_ref[pl.ds(start, size), :]
            vc = v_ref[pl.ds(start, size), :]
            s = lax.dot_general(qv, kc, _DIMS["nt"], preferred_element_type=F32)
            m_old = m_ref[...]
            m_new = jnp.maximum(m_old, jnp.max(s, axis=-1, keepdims=True))
            pr = jnp.exp2(s - m_new)
            alpha = jnp.exp2(m_old - m_new)
            acc_ref[...] = alpha * acc_ref[...] + jnp.dot(pr.astype(BF16), vc, preferred_element_type=F32)
            m_ref[...] = m_new
        acc = acc_ref[...]
        denom = acc[:, HEAD_DIM:HEAD_DIM + 1]
        o = acc[:, :HEAD_DIM] / denom
        o_ref[...] = o.astype(BF16)
        mix_ref[...] = (o * _silu(za_ref[...])).astype(BF16)
        lse_ref[0] = m_ref[...] + jnp.log2(denom)

    qspec = pl.BlockSpec((tq, HEAD_DIM), lambda h, i: (i, h))
    return pl.pallas_call(
        body, name=name, grid=(N_Q_HEADS, n // tq),
        in_specs=[qspec, pl.BlockSpec((s_all, HEAD_DIM), lambda h, i: (0, h // Q_PER_KV)),
                  pl.BlockSpec((s_all, wide), lambda h, i: (0, h // Q_PER_KV)),
                  pl.BlockSpec((tq, HEAD_DIM), lambda h, i: (i, za_block + h))],
        out_specs=[qspec, qspec, pl.BlockSpec((1, tq, 1), lambda h, i: (h, i, 0))],
        out_shape=[jax.ShapeDtypeStruct((n, ATTN_W), BF16), jax.ShapeDtypeStruct((n, ATTN_W), BF16),
                   jax.ShapeDtypeStruct((N_Q_HEADS, n, 1), F32)],
        scratch_shapes=[pltpu.VMEM((tq, 1), F32), pltpu.VMEM((tq, wide), F32)],
        compiler_params=_params("parallel", "parallel"),
    )(q, k_all, v_ext, p)


def attn_gate_bwd(dmix, o, p, *, za_block, name):
    n = o.shape[0]
    tt = _rows_tile(n, 512)

    def body(dm_ref, o_ref, za_ref, do_ref, dos_ref, dza_ref, delta_ref):
        dm = dm_ref[...]
        ov = o_ref[...].astype(F32)
        za = za_ref[...]
        do = dm * _silu(za)
        do_ref[...] = do.astype(BF16)
        dos_ref[...] = (do * LN2).astype(BF16)
        dza_ref[...] = (dm * ov * _dsilu(za)).astype(BF16)
        delta_ref[0] = jnp.sum(do * ov, axis=-1, keepdims=True) * LN2

    spec = pl.BlockSpec((tt, HEAD_DIM), lambda i, h: (i, h))
    shape = jax.ShapeDtypeStruct((n, ATTN_W), BF16)
    return pl.pallas_call(
        body, name=name, grid=(n // tt, N_Q_HEADS),
        in_specs=[spec, spec, pl.BlockSpec((tt, HEAD_DIM), lambda i, h: (i, za_block + h))],
        out_specs=[spec, spec, spec, pl.BlockSpec((1, tt, 1), lambda i, h: (h, i, 0))],
        out_shape=[shape, shape, shape, jax.ShapeDtypeStruct((N_Q_HEADS, n, 1), F32)],
        compiler_params=_params("parallel", "parallel"),
    )(dmix, o, p)


def flash_bwd(q, do, do_s, lse_row, delta_row, k_all, v_all, *, name):
    n = q.shape[0]
    s_all = k_all.shape[0]
    tq = _rows_tile(n, 512)
    chunks = _kv_chunks(n, s_all)

    def body(q_ref, do_ref, dos_ref, lse_ref, dl_ref, k_ref, v_ref, dq_ref, dk_ref, dv_ref):
        g = pl.program_id(1)
        i = pl.program_id(2)

        @pl.when((g == 0) & (i == 0))
        def _():
            dk_ref[...] = jnp.zeros_like(dk_ref)
            dv_ref[...] = jnp.zeros_like(dv_ref)

        qv = q_ref[...]
        dov = do_ref[...]
        dosv = dos_ref[...]
        lse = lse_ref[0]
        dl = dl_ref[0]
        dq = jnp.zeros((tq, HEAD_DIM), F32)
        for start, size in chunks:
            kc = k_ref[pl.ds(start, size), :]
            vc = v_ref[pl.ds(start, size), :]
            st = lax.dot_general(kc, qv, _DIMS["nt"], preferred_element_type=F32)
            pt = jnp.exp2(st - lse)
            dpt = lax.dot_general(vc, dosv, _DIMS["nt"], preferred_element_type=F32)
            dst = (pt * (dpt - dl)).astype(BF16)
            dv_ref[pl.ds(start, size), :] += jnp.dot(pt.astype(BF16), dov, preferred_element_type=F32)
            dk_ref[pl.ds(start, size), :] += jnp.dot(dst, qv, preferred_element_type=F32)
            dq = dq + lax.dot_general(dst, kc, _DIMS["tn"], preferred_element_type=F32)
        dq_ref[...] = dq

    qspec = pl.BlockSpec((tq, HEAD_DIM), lambda kh, g, i: (i, kh * Q_PER_KV + g))
    rowspec = pl.BlockSpec((1, 1, tq), lambda kh, g, i: (kh * Q_PER_KV + g, 0, i))
    kvspec = pl.BlockSpec((s_all, HEAD_DIM), lambda kh, g, i: (0, kh))
    return pl.pallas_call(
        body, name=name, grid=(N_KV_HEADS, Q_PER_KV, n // tq),
        in_specs=[qspec, qspec, qspec, rowspec, rowspec, kvspec, kvspec],
        out_specs=[qspec, kvspec, kvspec],
        out_shape=[jax.ShapeDtypeStruct((n, ATTN_W), F32),
                   jax.ShapeDtypeStruct((s_all, KV_W), F32), jax.ShapeDtypeStruct((s_all, KV_W), F32)],
        compiler_params=_params("arbitrary", "arbitrary", "arbitrary"),
    )(q, do, do_s, lse_row, delta_row, k_all, v_all)


HALF = 512


def _halo_specs(tt, n, cb):
    per = tt // HALO
    last = n // HALO - 1
    return [pl.BlockSpec((HALO, HALF), lambda i: (jnp.maximum(i * per - 1, 0), cb)),
            pl.BlockSpec((tt, HALF), lambda i: (i, cb)),
            pl.BlockSpec((HALO, HALF), lambda i: (jnp.minimum((i + 1) * per, last), cb))]


def conv_fwd(p, dw_w, dw_b, ln_g, ln_b, *, glu_block, name):
    n = p.shape[0]
    ch = dw_w.shape[1]
    nh = ch // HALF
    tt = _rows_tile(n, 256)
    last = n // tt - 1

    def body(*refs):
        a_refs = [refs[3 * h:3 * h + 3] for h in range(nh)]
        b_refs = [refs[3 * (nh + h):3 * (nh + h) + 3] for h in range(nh)]
        pos = 6 * nh
        zb_refs = refs[pos:pos + nh]
        pos += nh
        w_ref, bias_ref, g_ref, be_ref, ycv_ref, mix_ref, ext_ref = refs[pos:pos + 7]
        i = pl.program_id(0)
        for h in range(nh):
            cs = slice(h * HALF, (h + 1) * HALF)
            ap, am, an = a_refs[h]
            bp, bm, bn = b_refs[h]
            ext_ref[0:HALO, :] = jnp.where(i > 0, ap[...] * _sigmoid(bp[...]), 0.0)
            ext_ref[HALO:HALO + tt, :] = am[...] * _sigmoid(bm[...])
            ext_ref[HALO + tt:2 * HALO + tt, :] = jnp.where(i < last, an[...] * _sigmoid(bn[...]), 0.0)
            acc = jnp.broadcast_to(bias_ref[:, cs], (tt, HALF))
            for k in range(CONV_WIDTH):
                acc = acc + w_ref[k:k + 1, cs] * ext_ref[pl.ds(1 + k, tt), :]
            ycv_ref[:, cs] = acc
        yc = ycv_ref[...]
        mu = jnp.mean(yc, axis=-1, keepdims=True)
        var = jnp.mean(jnp.square(yc - mu), axis=-1, keepdims=True)
        ln = (yc - mu) * lax.rsqrt(var + EPS) * g_ref[...] + be_ref[...]
        out = _silu(ln)
        for h in range(nh):
            cs = slice(h * HALF, (h + 1) * HALF)
            mix_ref[:, cs] = (out[:, cs] * _silu(zb_refs[h][...])).astype(BF16)

    in_specs = []
    for h in range(2 * nh):
        in_specs += _halo_specs(tt, n, glu_block + h)
    in_specs += [pl.BlockSpec((tt, HALF), functools.partial(lambda i, cb: (i, cb), cb=glu_block + 2 * nh + h))
                 for h in range(nh)]
    vec = pl.BlockSpec((1, ch), lambda i: (0, 0))
    in_specs += [pl.BlockSpec((CONV_WIDTH, ch), lambda i: (0, 0)), vec, vec, vec]
    ops = [p] * (6 * nh + nh) + [dw_w, _row(dw_b), _row(ln_g), _row(ln_b)]
    big = pl.BlockSpec((tt, ch), lambda i: (i, 0))
    return pl.pallas_call(
        body, name=name, grid=(n // tt,), in_specs=in_specs, out_specs=[big, big],
        out_shape=[jax.ShapeDtypeStruct((n, ch), F32), jax.ShapeDtypeStruct((n, ch), BF16)],
        scratch_shapes=[pltpu.VMEM((tt + 2 * HALO, HALF), F32)],
        compiler_params=_params("parallel"),
    )(*ops)


def conv_bwd_rows(dmix, ycv, p, ln_g, ln_b, *, mix_block, zb_block, name):
    n, ch = ycv.shape
    nh = ch // HALF
    tt = _rows_tile(n, 256)

    def body(*refs):
        dm_ref, ycv_ref = refs[:2]
        zb_refs = refs[2:2 + nh]
        g_ref, be_ref, dy_ref, dzb_ref, acc_ref = refs[2 + nh:]
        i = pl.program_id(0)

        @pl.when(i == 0)
        def _():
            acc_ref[...] = jnp.zeros_like(acc_ref)

        yc = ycv_ref[...]
        mu = jnp.mean(yc, axis=-1, keepdims=True)
        var = jnp.mean(jnp.square(yc - mu), axis=-1, keepdims=True)
        rstd = lax.rsqrt(var + EPS)
        xhat = (yc - mu) * rstd
        ln = xhat * g_ref[...] + be_ref[...]
        out = _silu(ln)
        dm = dm_ref[...]
        zb = jnp.concatenate([r[...] for r in zb_refs], axis=-1)
        dzb_ref[...] = (dm * out * _dsilu(zb)).astype(BF16)
        dln = dm * _silu(zb) * _dsilu(ln)
        acc_ref[0:1, :] += jnp.sum(dln * xhat, axis=0, keepdims=True)
        acc_ref[1:2, :] += jnp.sum(dln, axis=0, keepdims=True)
        dxhat = dln * g_ref[...]
        dy_ref[...] = rstd * (dxhat - jnp.mean(dxhat, axis=-1, keepdims=True)
                              - xhat * jnp.mean(dxhat * xhat, axis=-1, keepdims=True))

    big = pl.BlockSpec((tt, ch), lambda i: (i, 0))
    vec = pl.BlockSpec((1, ch), lambda i: (0, 0))
    in_specs = [pl.BlockSpec((tt, ch), lambda i: (i, mix_block)), big]
    in_specs += [pl.BlockSpec((tt, HALF), functools.partial(lambda i, cb: (i, cb), cb=zb_block + h)) for h in range(nh)]
    in_specs += [vec, vec]
    return pl.pallas_call(
        body, name=name, grid=(n // tt,), in_specs=in_specs,
        out_specs=[big, big, pl.BlockSpec((8, ch), lambda i: (0, 0))],
        out_shape=[jax.ShapeDtypeStruct((n, ch), F32), jax.ShapeDtypeStruct((n, ch), BF16),
                   jax.ShapeDtypeStruct((8, ch), F32)],
        compiler_params=_params("arbitrary"),
    )(dmix, ycv, *([p] * nh), _row(ln_g), _row(ln_b))


def conv_bwd_taps(dycv, p, dw_w, *, glu_block, name):
    n, ch = dycv.shape
    nh = ch // HALF
    tt = _rows_tile(n, 256)
    last = n // tt - 1

    def body(*refs):
        d_refs = [refs[3 * h:3 * h + 3] for h in range(nh)]
        a_refs = [refs[3 * (nh + h):3 * (nh + h) + 3] for h in range(nh)]
        b_refs = [refs[3 * (2 * nh + h):3 * (2 * nh + h) + 3] for h in range(nh)]
        w_ref, dglu_ref, dw_ref, db_ref, yext_ref, dext_ref = refs[9 * nh:]
        i = pl.program_id(0)

        @pl.when(i == 0)
        def _():
            dw_ref[...] = jnp.zeros_like(dw_ref)
            db_ref[...] = jnp.zeros_like(db_ref)

        for h in range(nh):
            cs = slice(h * HALF, (h + 1) * HALF)
            ap, am, an = a_refs[h]
            bp, bm, bn = b_refs[h]
            dp, dm, dn = d_refs[h]
            av = am[...]
            sb = _sigmoid(bm[...])
            yext_ref[0:HALO, :] = jnp.where(i > 0, ap[...] * _sigmoid(bp[...]), 0.0)
            yext_ref[HALO:HALO + tt, :] = av * sb
            yext_ref[HALO + tt:2 * HALO + tt, :] = jnp.where(i < last, an[...] * _sigmoid(bn[...]), 0.0)
            dmain = dm[...]
            dext_ref[0:HALO, :] = jnp.where(i > 0, dp[...], 0.0)
            dext_ref[HALO:HALO + tt, :] = dmain
            dext_ref[HALO + tt:2 * HALO + tt, :] = jnp.where(i < last, dn[...], 0.0)
            dy = jnp.zeros((tt, HALF), F32)
            for k in range(CONV_WIDTH):
                dy = dy + w_ref[k:k + 1, cs] * dext_ref[pl.ds(CONV_WIDTH - k, tt), :]
                dw_ref[k:k + 1, cs] += jnp.sum(dmain * yext_ref[pl.ds(1 + k, tt), :], axis=0, keepdims=True)
            db_ref[0:1, cs] += jnp.sum(dmain, axis=0, keepdims=True)
            dglu_ref[:, cs] = (dy * sb).astype(BF16)
            dglu_ref[:, ch + h * HALF:ch + (h + 1) * HALF] = (dy * av * sb * (1.0 - sb)).astype(BF16)

    in_specs = []
    for h in range(nh):
        in_specs += _halo_specs(tt, n, h)
    for h in range(2 * nh):
        in_specs += _halo_specs(tt, n, glu_block + h)
    in_specs += [pl.BlockSpec((CONV_WIDTH, ch), lambda i: (0, 0))]
    ops = [dycv] * (3 * nh) + [p] * (6 * nh) + [dw_w]
    return pl.pallas_call(
        body, name=name, grid=(n // tt,), in_specs=in_specs,
        out_specs=[pl.BlockSpec((tt, 2 * ch), lambda i: (i, 0)), pl.BlockSpec((32, ch), lambda i: (0, 0)),
                   pl.BlockSpec((8, ch), lambda i: (0, 0))],
        out_shape=[jax.ShapeDtypeStruct((n, 2 * ch), BF16), jax.ShapeDtypeStruct((32, ch), F32),
                   jax.ShapeDtypeStruct((8, ch), F32)],
        scratch_shapes=[pltpu.VMEM((tt + 2 * HALO, HALF), F32), pltpu.VMEM((tt + 2 * HALO, HALF), F32)],
        compiler_params=_params("arbitrary"),
    )(*ops)


def _sgu_common(p_ref, g_ref, be_ref, ws_ref, bs_ref, w):
    gw = w // SGU_GROUPS
    u_pre = p_ref[:, 0:w]
    v_pre = p_ref[:, w:2 * w]
    zc = p_ref[:, 2 * w:3 * w]
    u = _gelu(u_pre)
    v = _gelu(v_pre)
    mu = jnp.mean(v, axis=-1, keepdims=True)
    var = jnp.mean(jnp.square(v - mu), axis=-1, keepdims=True)
    rstd = lax.rsqrt(var + EPS)
    vhat = (v - mu) * rstd
    vn = (vhat * g_ref[...] + be_ref[...]).astype(BF16)
    mixed = jnp.concatenate(
        [jnp.dot(ws_ref[gi].astype(BF16), vn[:, gi * gw:(gi + 1) * gw], preferred_element_type=F32)
         + bs_ref[:, gi:gi + 1] for gi in range(SGU_GROUPS)], axis=-1)
    return u_pre, v_pre, zc, u, rstd, vhat, vn, mixed


def sgu_fwd(p, ln_g, ln_b, ws, bs_t, *, name):
    n, w3 = p.shape
    w = w3 // 3

    def body(p_ref, g_ref, be_ref, ws_ref, bs_ref, m_ref):
        _, _, zc, u, _, _, _, mixed = _sgu_common(p_ref, g_ref, be_ref, ws_ref, bs_ref, w)
        m_ref[...] = (u * mixed * _silu(zc)).astype(BF16)

    vec = pl.BlockSpec((1, w), lambda i: (0, 0))
    return pl.pallas_call(
        body, name=name, grid=(n // CHUNK,),
        in_specs=[pl.BlockSpec((CHUNK, w3), lambda i: (i, 0)), vec, vec,
                  pl.BlockSpec((SGU_GROUPS, CHUNK, CHUNK), lambda i: (0, 0, 0)),
                  pl.BlockSpec((CHUNK, SGU_GROUPS), lambda i: (0, 0))],
        out_specs=pl.BlockSpec((CHUNK, w), lambda i: (i, 0)),
        out_shape=jax.ShapeDtypeStruct((n, w), BF16),
        compiler_params=_params("parallel"),
    )(p, _row(ln_g), _row(ln_b), ws, bs_t)


def sgu_bwd(p, dm, ln_g, ln_b, ws, ws_t, bs_t, *, name):
    n, w3 = p.shape
    w = w3 // 3
    gw = w // SGU_GROUPS

    def body(p_ref, dm_ref, g_ref, be_ref, ws_ref, wst_ref, bs_ref, dp_ref, dws_ref, dbs_ref, acc_ref):
        i = pl.program_id(0)

        @pl.when(i == 0)
        def _():
            dws_ref[...] = jnp.zeros_like(dws_ref)
            dbs_ref[...] = jnp.zeros_like(dbs_ref)
            acc_ref[...] = jnp.zeros_like(acc_ref)

        u_pre, v_pre, zc, u, rstd, vhat, vn, mixed = _sgu_common(p_ref, g_ref, be_ref, ws_ref, bs_ref, w)
        dmv = dm_ref[...]
        um = u * mixed
        dp_ref[:, 2 * w:3 * w] = (dmv * um * _dsilu(zc)).astype(BF16)
        dum = dmv * _silu(zc)
        dp_ref[:, 0:w] = (dum * mixed * _dgelu(u_pre)).astype(BF16)
        dmixed = dum * u
        dmixed_b = dmixed.astype(BF16)
        dvn_parts = []
        for gi in range(SGU_GROUPS):
            cs = slice(gi * gw, (gi + 1) * gw)
            dws_ref[gi] += lax.dot_general(dmixed_b[:, cs], vn[:, cs], _DIMS["nt"], preferred_element_type=F32)
            dbs_ref[:, gi:gi + 1] += jnp.sum(dmixed[:, cs], axis=-1, keepdims=True)
            dvn_parts.append(jnp.dot(wst_ref[gi].astype(BF16), dmixed_b[:, cs], preferred_element_type=F32))
        dvn = jnp.concatenate(dvn_parts, axis=-1)
        acc_ref[0:1, :] += jnp.sum(dvn * vhat, axis=0, keepdims=True)
        acc_ref[1:2, :] += jnp.sum(dvn, axis=0, keepdims=True)
        dvhat = dvn * g_ref[...]
        dv = rstd * (dvhat - jnp.mean(dvhat, axis=-1, keepdims=True)
                     - vhat * jnp.mean(dvhat * vhat, axis=-1, keepdims=True))
        dp_ref[:, w:2 * w] = (dv * _dgelu(v_pre)).astype(BF16)

    vec = pl.BlockSpec((1, w), lambda i: (0, 0))
    wspec = pl.BlockSpec((SGU_GROUPS, CHUNK, CHUNK), lambda i: (0, 0, 0))
    bspec = pl.BlockSpec((CHUNK, SGU_GROUPS), lambda i: (0, 0))
    return pl.pallas_call(
        body, name=name, grid=(n // CHUNK,),
        in_specs=[pl.BlockSpec((CHUNK, w3), lambda i: (i, 0)), pl.BlockSpec((CHUNK, w), lambda i: (i, 0)),
                  vec, vec, wspec, wspec, bspec],
        out_specs=[pl.BlockSpec((CHUNK, w3), lambda i: (i, 0)), wspec, bspec,
                   pl.BlockSpec((8, w), lambda i: (0, 0))],
        out_shape=[jax.ShapeDtypeStruct((n, w3), BF16), jax.ShapeDtypeStruct((SGU_GROUPS, CHUNK, CHUNK), F32),
                   jax.ShapeDtypeStruct((CHUNK, SGU_GROUPS), F32), jax.ShapeDtypeStruct((8, w), F32)],
        compiler_params=_params("arbitrary"),
    )(p, dm, _row(ln_g), _row(ln_b), ws, ws_t, bs_t)


def _adam_math(w, g, m, v):
    m_new = ADAM_B1 * m + (1.0 - ADAM_B1) * g
    v_new = ADAM_B2 * v + (1.0 - ADAM_B2) * (g * g)
    m_hat = m_new / (1.0 - ADAM_B1 ** ADAM_STEP)
    v_hat = v_new / (1.0 - ADAM_B2 ** ADAM_STEP)
    delta = -ADAM_LR * (m_hat / (jnp.sqrt(v_hat) + ADAM_EPS) + ADAM_WD * w)
    return delta, m_new, v_new


def adamw(w, g, m, v, *, name, slots=False, rows=512):
    r, c = w.shape
    tr = min(r, rows)
    assert r % tr == 0, (name, r, tr)

    def body(w_ref, g_ref, m_ref, v_ref, go_ref, d_ref, mo_ref, vo_ref):
        if slots:
            g = g_ref[0].astype(F32)
            for k in range(1, N_DEV):
                g = g + g_ref[k].astype(F32)
        else:
            g = g_ref[...].astype(F32)
        delta, m_new, v_new = _adam_math(w_ref[...], g, m_ref[...], v_ref[...])
        go_ref[...] = g
        d_ref[...] = delta
        mo_ref[...] = m_new
        vo_ref[...] = v_new

    spec = pl.BlockSpec((tr, c), lambda i: (i, 0))
    gspec = pl.BlockSpec((N_DEV, tr, c), lambda i: (0, i, 0)) if slots else spec
    shape = jax.ShapeDtypeStruct((r, c), F32)
    return pl.pallas_call(
        body, name=name, grid=(r // tr,),
        in_specs=[spec, gspec, spec, spec], out_specs=[spec] * 4, out_shape=[shape] * 4,
        compiler_params=_params("parallel"),
    )(w, g, m, v)


def _pad_rows(a, rows):
    return jnp.pad(a, ((0, rows - a.shape[0]), (0, 0)))


def kernel(x, c, ctx, c_ctx, ada_w, ada_b, norm_g, ev_w_in, ev_q_norm, ev_k_norm, ev_dw_w, ev_dw_b, ev_ln_g, ev_ln_b, ev_w_out, od_w_in, od_ln_g, od_ln_b, od_ws, od_bs, od_w_out, final_g, loss_target, m_c_ctx, m_ada_w, m_ada_b, m_norm_g, m_ev_w_in, m_ev_q_norm, m_ev_k_norm, m_ev_dw_w, m_ev_dw_b, m_ev_ln_g, m_ev_ln_b, m_ev_w_out, m_od_w_in, m_od_ln_g, m_od_ln_b, m_od_ws, m_od_bs, m_od_w_out, m_final_g, v_c_ctx, v_ada_w, v_ada_b, v_norm_g, v_ev_w_in, v_ev_q_norm, v_ev_k_norm, v_ev_dw_w, v_ev_dw_b, v_ev_ln_g, v_ev_ln_b, v_ev_w_out, v_od_w_in, v_od_ln_g, v_od_ln_b, v_od_ws, v_od_bs, v_od_w_out, v_final_g):
    n, d = x.shape[1], x.shape[2]
    lc = ctx.shape[1]
    ev_in = ev_w_in.shape[2] * N_DEV
    od_in = od_w_in.shape[2] * N_DEV
    conv_ch = ev_dw_w.shape[2] * N_DEV
    ada_cols = ada_w.shape[2]
    me = 4 * lax.axis_index("x") + 2 * lax.axis_index("y") + lax.axis_index("c")
    xs, tgt, ctxs = x[0], loss_target[0], ctx[0]
    za_block = (2 * KV_W + ATTN_W) // HEAD_DIM
    glu_block = (2 * KV_W + 2 * ATTN_W) // HALF
    zb_block = glu_block + 2 * conv_ch // HALF

    small = jnp.concatenate([
        jax.nn.silu(c).reshape(1, d),
        od_ln_g.reshape(1, -1), od_ln_b.reshape(1, -1)], axis=1)
    small = _pad_rows(small, 8)
    dw_rows = _pad_rows(ev_dw_w[0], 32)
    small_g, dw_g = all_gather([small, dw_rows], name="gather_small", hbm=False)
    sc_all = small_g[:, 0, :d]
    shard = d // N_DEV
    od_ln_g_full = small_g[:, 0, d:d + shard].reshape(d)
    od_ln_b_full = small_g[:, 0, d + shard:d + 2 * shard].reshape(d)
    dw_w_full = jnp.moveaxis(dw_g, 0, 1).reshape(32, conv_ch)[:CONV_WIDTH]
    scc = jax.nn.silu(c_ctx)
    sc16 = _pad_rows(jnp.concatenate([sc_all, scc.reshape(1, d)], axis=0), 16)

    ada_bf = ada_w.astype(BF16)
    mod_loc = [matmul(sc16, ada_bf[l], mode="nn", tm=16, tn=ada_cols, tk=d, out_dtype=F32, name=f"ada_mod{l}")
               for l in range(2)]
    (mod_g,) = all_gather([jnp.stack(mod_loc)], name="gather_mod", hbm=False)
    mod_all = jnp.moveaxis(mod_g, 0, 2).reshape(2, 16, N_DEV * ada_cols) + ada_b[:, None, :]
    mod_me = lax.dynamic_index_in_dim(mod_all, me, axis=1, keepdims=False)
    shift = [mod_me[l, :d] for l in range(2)]
    scale = [mod_me[l, d:2 * d] for l in range(2)]
    gate = [mod_me[l, 2 * d:] for l in range(2)]
    shift_c, scale_c = mod_all[0, 8, :d], mod_all[0, 8, d:2 * d]

    w_parts = [ev_w_in[0].astype(BF16), od_w_in[0].astype(BF16), ev_w_out[0].astype(BF16), od_w_out[0].astype(BF16)]
    wi0_g, wi1_g, wo0_g, wo1_g = all_gather(w_parts, name="gather_weights", hbm=True)
    wi0 = jnp.moveaxis(wi0_g, 0, 1).reshape(d, ev_in)
    wi1 = jnp.moveaxis(wi1_g, 0, 1).reshape(d, od_in)
    wo0 = wo0_g.reshape(-1, d)
    wo1 = wo1_g.reshape(-1, d)

    cexp, sexp = rope_tables(n)

    h0 = norm_mod_fwd(xs, norm_g[0], shift[0], scale[0], name="norm_mod_fwd0")
    hc = norm_mod_fwd(ctxs, norm_g[0], shift_c, scale_c, name="norm_mod_fwd_ctx")
    p0 = matmul(h0, wi0, mode="nn", tm=1024, tn=ev_in // 4, tk=d, out_dtype=F32, name="proj_in0")
    pc = matmul(hc, wi0, mode="nn", tm=lc, tn=2 * KV_W, tk=d, out_dtype=F32, n_out=2 * KV_W, name="proj_in_ctx")
    q_r, k_lat, v_lat = qkv_prep_fwd(p0, ev_q_norm[0], ev_k_norm[0], cexp, sexp, latent=True, name="qkv_prep")
    k_ctx, v_ctx = qkv_prep_fwd(pc, None, ev_k_norm[0], None, None, latent=False, name="kv_prep_ctx")
    k_all = jnp.concatenate([k_lat, k_ctx], axis=0)
    v_all = jnp.concatenate([v_lat, v_ctx], axis=0)
    s_all = v_all.shape[0]
    ones_col = jnp.concatenate([jnp.ones((s_all, 1), BF16), jnp.zeros((s_all, HEAD_DIM - 1), BF16)], axis=1)
    v_ext = jnp.concatenate([blk for h in range(N_KV_HEADS)
                             for blk in (v_all[:, h * HEAD_DIM:(h + 1) * HEAD_DIM], ones_col)], axis=1)
    o_attn, mix_a, lse = flash_fwd(q_r, k_all, v_ext, p0, za_block=za_block, name="flash_fwd")
    ycv, mix_b = conv_fwd(p0, dw_w_full, ev_dw_b[0], ev_ln_g[0], ev_ln_b[0], glu_block=glu_block, name="conv_fwd")
    mix0 = jnp.concatenate([mix_a, mix_b], axis=1)
    x1, r0 = matmul(mix0, wo0, mode="nn", tm=1024, tn=1024, tk=mix0.shape[1], out_dtype=F32, name="proj_out0",
                    res=xs, gate=_row(gate[0]))

    h1 = norm_mod_fwd(x1, norm_g[1], shift[1], scale[1], name="norm_mod_fwd1")
    p1 = matmul(h1, wi1, mode="nn", tm=1024, tn=od_in // 4, tk=d, out_dtype=F32, name="proj_in1")
    ws_bf = od_ws[0]
    bs_t = od_bs[0].T
    m1 = sgu_fwd(p1, od_ln_g_full, od_ln_b_full, ws_bf, bs_t, name="sgu_fwd")
    x2, r1 = matmul(m1, wo1, mode="nn", tm=1024, tn=1024, tk=m1.shape[1], out_dtype=F32, name="proj_out1",
                    res=x1, gate=_row(gate[1]))

    dx2, acc_final, loss_tile = final_loss(x2, tgt, final_g, name="final_loss")

    dr1, acc_gate1 = gate_bwd(dx2, r1, gate[1], name="gate_bwd1")
    dm1 = matmul(dr1, wo1, mode="nt", tm=1024, tn=1024, tk=d, out_dtype=F32, name="d_mix1")
    dwo1 = matmul(m1, dr1, mode="tn", tm=512, tn=512, tk=4096,out_dtype=BF16, name="d_wout1")
    dp1, dws, dbs_t, acc_sgu = sgu_bwd(p1, dm1, od_ln_g_full, od_ln_b_full, ws_bf, jnp.swapaxes(ws_bf, 1, 2), bs_t,
                                       name="sgu_bwd")
    dh1 = matmul(dp1, wi1, mode="nt", tm=1024, tn=512, tk=od_in, out_dtype=F32, name="d_h1")
    dwi1 = matmul(h1, dp1, mode="tn", tm=512, tn=od_in // N_DEV, tk=4096,out_dtype=BF16, name="d_win1")
    dx1, acc_norm1 = norm_mod_bwd(x1, dh1, dx2, norm_g[1], scale[1], name="norm_mod_bwd1")

    dr0, acc_gate0 = gate_bwd(dx1, r0, gate[0], name="gate_bwd0")
    dmix0 = matmul(dr0, wo0, mode="nt", tm=1024, tn=1024, tk=d, out_dtype=F32, name="d_mix0")
    dwo0 = matmul(mix0, dr0, mode="tn", tm=512, tn=512, tk=4096,out_dtype=BF16, name="d_wout0")
    do_attn, do_s, dza, delta = attn_gate_bwd(dmix0, o_attn, p0, za_block=za_block, name="attn_gate_bwd")
    dycv, dzb, acc_ln = conv_bwd_rows(dmix0, ycv, p0, ev_ln_g[0], ev_ln_b[0], mix_block=ATTN_W // conv_ch,
                                      zb_block=zb_block, name="conv_bwd_rows")
    dglu, ddw_w, acc_dwb = conv_bwd_taps(dycv, p0, dw_w_full, glu_block=glu_block, name="conv_bwd_taps")
    lse_row = lse.reshape(N_Q_HEADS, 1, n)
    delta_row = delta.reshape(N_Q_HEADS, 1, n)
    dq_r, dk_all, dv_all = flash_bwd(q_r, do_attn, do_s, lse_row, delta_row, k_all, v_all, name="flash_bwd")
    dkvq, acc_qk = qkv_prep_bwd(p0, dq_r, dk_all[:n], dv_all[:n], ev_q_norm[0], ev_k_norm[0], cexp, sexp,
                                latent=True, name="qkv_prep_bwd")
    dpc, acc_kc = qkv_prep_bwd(pc, None, dk_all[n:], dv_all[n:], None, ev_k_norm[0], None, None,
                               latent=False, name="kv_prep_ctx_bwd")
    dp0 = jnp.concatenate([dkvq, dza, dglu, dzb], axis=1)
    dh0 = matmul(dp0, wi0, mode="nt", tm=1024, tn=512, tk=ev_in, out_dtype=F32, name="d_h0")
    dwi0 = matmul(h0, dp0, mode="tn", tm=512, tn=512, tk=4096,out_dtype=F32, name="d_win0")
    dwi0 = matmul(hc, dpc, mode="tn", tm=512, tn=2 * KV_W, tk=lc, out_dtype=F32, name="d_win0_ctx", add=dwi0)
    dhc = matmul(dpc, wi0, mode="nt", tm=lc, tn=512, tk=2 * KV_W, out_dtype=F32, name="d_h_ctx")
    grad_x, acc_norm0 = norm_mod_bwd(xs, dh0, dx1, norm_g[0], scale[0], name="norm_mod_bwd0")
    _, acc_normc = norm_mod_bwd(ctxs, dhc, None, norm_g[0], scale_c, name="norm_mod_bwd_ctx")

    zeros_d = jnp.zeros((d,), F32)
    dmod0 = jnp.stack([acc_norm0[0], acc_norm0[1], acc_gate0[0]])
    dmod1 = jnp.stack([acc_norm1[0], acc_norm1[1], acc_gate1[0]])
    dmodc = jnp.stack([acc_normc[0], acc_normc[1]])
    half_pad = jnp.zeros((d - 2 * conv_ch,), F32) if d > 2 * conv_ch else jnp.zeros((0,), F32)
    row_a = jnp.concatenate([acc_dwb[0], acc_ln[0], half_pad])
    row_b = jnp.concatenate([acc_ln[1], acc_qk[0], acc_qk[1] + acc_kc[1],
                             jnp.zeros((d - conv_ch - 2 * HEAD_DIM,), F32)])
    row_c = jnp.concatenate([dbs_t.T.reshape(-1), jnp.zeros((d - SGU_GROUPS * CHUNK,), F32)])
    row_loss = jnp.concatenate([loss_tile[0, :1], jnp.zeros((d - 1,), F32)])
    pack = jnp.concatenate([
        dmod0, dmod1, dmodc,
        (acc_norm0[2] + acc_normc[2])[None], acc_norm1[2][None],
        acc_final[0][None],
        acc_sgu[0][None], acc_sgu[1][None],
        row_a[None], row_b[None], row_c[None], row_loss[None],
        ddw_w.reshape(-1, d),
        dws.reshape(-1, d),
    ], axis=0)
    n_rows = pack.shape[0]
    pack = _pad_rows(pack, -(-n_rows // 8) * 8)
    pack_g, pack_sum = all_gather([pack], name="gather_small_grads", hbm=False, sum_out=True)
    gsum = pack_sum
    loss = gsum[16, 0]
    dw_rows_n = 32 * conv_ch // d
    g_dw_w = gsum[17:17 + dw_rows_n].reshape(32, conv_ch)[:CONV_WIDTH]
    g_od_ws = gsum[17 + dw_rows_n:17 + dw_rows_n + SGU_GROUPS * CHUNK * CHUNK // d].reshape(od_ws.shape)

    dmodc_sum = jnp.concatenate([gsum[6], gsum[7], zeros_d])
    col0 = me * ada_cols
    dm_cols = []
    for l in range(2):
        rows = pack_g[:, 3 * l:3 * l + 3, :].reshape(N_DEV, 3 * d)
        extra = dmodc_sum[None] if l == 0 else jnp.zeros((1, 3 * d), F32)
        full = _pad_rows(jnp.concatenate([rows, extra], axis=0), 16)
        dm_cols.append(lax.dynamic_slice_in_dim(full, col0, ada_cols, axis=1))
    g_ada_w = jnp.stack([matmul(sc16, dm_cols[l], mode="tn", tm=512, tn=ada_cols, tk=16, out_dtype=F32,
                                name=f"d_ada_w{l}") for l in range(2)])
    dsc = matmul(dm_cols[0], ada_bf[0], mode="nt", tm=16, tn=512, tk=ada_cols, out_dtype=F32, name="d_scc")
    (_, dscc_sum) = all_gather([dsc[8:16]], name="gather_dscc", hbm=False, sum_out=True)
    sg = jax.nn.sigmoid(c_ctx)
    g_c_ctx = dscc_sum[0] * (sg * (1.0 + c_ctx * (1.0 - sg)))
    g_ada_b = jnp.stack([gsum[0:3].reshape(-1) + dmodc_sum, gsum[3:6].reshape(-1)])

    dwi0_s = jnp.moveaxis(dwi0.astype(BF16).reshape(d, N_DEV, ev_in // N_DEV), 1, 0)
    dwi1_s = jnp.moveaxis(dwi1.reshape(d, N_DEV, od_in // N_DEV), 1, 0)
    dwo0_s = dwo0.reshape(N_DEV, -1, d)
    dwo1_s = dwo1.reshape(N_DEV, -1, d)
    gi0, gi1, go0, go1 = all_to_all([dwi0_s, dwi1_s, dwo0_s, dwo1_s], name="exchange_weight_grads")

    out = {}

    def upd(key, w, g, m, v, slots=False, rows=512):
        shp = w.shape
        w2 = w.reshape(-1, shp[-1])
        g2 = g.reshape((N_DEV, -1, shp[-1])) if slots else g.reshape(-1, shp[-1])
        res = adamw(w2, g2, m.reshape(w2.shape), v.reshape(w2.shape), name="adamw_" + key, slots=slots, rows=rows)
        out[key] = tuple(r.reshape(shp) for r in res)

    upd("ev_w_in", ev_w_in, gi0, m_ev_w_in, v_ev_w_in, slots=True, rows=256)
    upd("od_w_in", od_w_in, gi1, m_od_w_in, v_od_w_in, slots=True, rows=256)
    upd("ev_w_out", ev_w_out, go0, m_ev_w_out, v_ev_w_out, slots=True, rows=256)
    upd("od_w_out", od_w_out, go1, m_od_w_out, v_od_w_out, slots=True, rows=256)
    upd("ada_w", ada_w, g_ada_w, m_ada_w, v_ada_w)

    def my_shard(full, size):
        return lax.dynamic_slice_in_dim(full, me * size, size, axis=full.ndim - 1)

    small_items = [
        ("c_ctx", c_ctx, g_c_ctx, m_c_ctx, v_c_ctx),
        ("ada_b", ada_b, g_ada_b, m_ada_b, v_ada_b),
        ("norm_g", norm_g, gsum[8:10], m_norm_g, v_norm_g),
        ("ev_q_norm", ev_q_norm, gsum[14, conv_ch:conv_ch + HEAD_DIM], m_ev_q_norm, v_ev_q_norm),
        ("ev_k_norm", ev_k_norm, gsum[14, conv_ch + HEAD_DIM:conv_ch + 2 * HEAD_DIM], m_ev_k_norm, v_ev_k_norm),
        ("ev_dw_w", ev_dw_w, my_shard(g_dw_w, conv_ch // N_DEV), m_ev_dw_w, v_ev_dw_w),
        ("ev_dw_b", ev_dw_b, gsum[13, :conv_ch], m_ev_dw_b, v_ev_dw_b),
        ("ev_ln_g", ev_ln_g, gsum[13, conv_ch:2 * conv_ch], m_ev_ln_g, v_ev_ln_g),
        ("ev_ln_b", ev_ln_b, gsum[14, :conv_ch], m_ev_ln_b, v_ev_ln_b),
        ("od_ln_g", od_ln_g, my_shard(gsum[11], shard), m_od_ln_g, v_od_ln_g),
        ("od_ln_b", od_ln_b, my_shard(gsum[12], shard), m_od_ln_b, v_od_ln_b),
        ("od_ws", od_ws, g_od_ws, m_od_ws, v_od_ws),
        ("od_bs", od_bs, gsum[15, :SGU_GROUPS * CHUNK], m_od_bs, v_od_bs),
        ("final_g", final_g, gsum[10], m_final_g, v_final_g),
    ]
    sizes = [it[1].size for it in small_items]
    total = sum(sizes)
    lanes = 1024
    prow = -(-total // lanes)
    prow = -(-prow // 8) * 8

    def pack_small(idx):
        flat = jnp.concatenate([it[idx].reshape(-1).astype(F32) for it in small_items])
        return jnp.pad(flat, (0, prow * lanes - total)).reshape(prow, lanes)

    sres = adamw(pack_small(1), pack_small(2), pack_small(3), pack_small(4), name="adamw_small", rows=prow)
    off = 0
    for it, size in zip(small_items, sizes):
        out[it[0]] = tuple(r.reshape(-1)[off:off + size].reshape(it[1].shape) for r in sres)
        off += size

    names = ['c_ctx', 'ada_w', 'ada_b', 'norm_g', 'ev_w_in', 'ev_q_norm', 'ev_k_norm', 'ev_dw_w', 'ev_dw_b',
             'ev_ln_g', 'ev_ln_b', 'ev_w_out', 'od_w_in', 'od_ln_g', 'od_ln_b', 'od_ws', 'od_bs', 'od_w_out',
             'final_g']
    return (loss, grad_x[None], *[out[k][0] for k in names], *[out[k][1] for k in names],
            *[out[k][2] for k in names], *[out[k][3] for k in names])
```

```python
import functools
import math

import jax
import jax.numpy as jnp
from jax import lax
from jax.experimental import pallas as pl
from jax.experimental.pallas import tpu as pltpu

F32 = jnp.float32
BF16 = jnp.bfloat16
MESH = pl.DeviceIdType.MESH

EPS = 1e-6
HEAD_DIM = 128
N_Q_HEADS = 8
N_KV_HEADS = 2
Q_PER_KV = N_Q_HEADS // N_KV_HEADS
ATTN_W = N_Q_HEADS * HEAD_DIM
KV_W = N_KV_HEADS * HEAD_DIM
ATTN_SCALE = HEAD_DIM ** -0.5
LN2 = math.log(2.0)
Q_SCALE = ATTN_SCALE / LN2
ROPE_THETA = 10000.0
GRID_W = 64
CONV_WIDTH = 31
CONV_HALF = CONV_WIDTH // 2
HALO = 16
CHUNK = 128
SGU_GROUPS = 8
N_DEV = 8

ADAM_LR = 0.001
ADAM_B1 = 0.9
ADAM_B2 = 0.999
ADAM_EPS = 1e-08
ADAM_WD = 0.01
ADAM_STEP = 10

VMEM_LIMIT = 56 * 1024 * 1024
ANY = pl.BlockSpec(memory_space=pl.ANY)
VMEM_SPEC = pl.BlockSpec(memory_space=pltpu.VMEM)


def _params(*sem):
    return pltpu.CompilerParams(dimension_semantics=sem, vmem_limit_bytes=VMEM_LIMIT)


def _sigmoid(x):
    return 1.0 / (1.0 + jnp.exp(-x))


def _silu(x):
    return x * _sigmoid(x)


def _dsilu(x):
    s = _sigmoid(x)
    return s * (1.0 + x * (1.0 - s))


_GELU_C = math.sqrt(2.0 / math.pi)


def _gelu(x):
    t = jnp.tanh(_GELU_C * (x + 0.044715 * (x * x * x)))
    return 0.5 * x * (1.0 + t)


def _dgelu(x):
    t = jnp.tanh(_GELU_C * (x + 0.044715 * (x * x * x)))
    return 0.5 * (1.0 + t) + 0.5 * x * (1.0 - t * t) * (_GELU_C * (1.0 + 3.0 * 0.044715 * (x * x)))


def _row(v):
    return v.reshape(1, -1).astype(F32)


def _flat_id(p):
    return 4 * p[0] + 2 * p[1] + p[2]


def _gather_body(n_arr, sum_out):
    def body(*refs):
        x_refs = refs[:n_arr]
        out_refs = refs[n_arr:2 * n_arr]
        pos = 2 * n_arr
        sum_refs = refs[pos:pos + n_arr] if sum_out else ()
        pos += n_arr if sum_out else 0
        send_sems, recv_sems, local_sems = refs[pos:pos + 3]
        x, y, c = lax.axis_index("x"), lax.axis_index("y"), lax.axis_index("c")
        me, sibling = (x, y, c), (x, y, 1 - c)
        chips = [(1 - x, y), (x, 1 - y), (1 - x, 1 - y)]

        def copy(a, k, block, to, src=None):
            rows = out_refs[a].at[_flat_id(block)]
            return pltpu.make_async_remote_copy(
                src_ref=rows if src is None else src, dst_ref=rows,
                send_sem=send_sems.at[a, k], recv_sem=recv_sems.at[a, k],
                device_id=to, device_id_type=MESH)

        sends = []
        mine = []
        for a in range(n_arr):
            cp = pltpu.make_async_copy(x_refs[a], out_refs[a].at[_flat_id(me)], local_sems.at[a])
            cp.start()
            mine.append(cp)
            first = [copy(a, 0, me, sibling, src=x_refs[a])]
            first += [copy(a, 1 + j, me, (*chip, c), src=x_refs[a]) for j, chip in enumerate(chips)]
            for cp in first:
                cp.start()
            sends += first
        for a in range(n_arr):
            for j, chip in enumerate(chips):
                copy(a, 1 + j, (*chip, c), me).wait_recv()
                fwd = copy(a, 4 + j, (*chip, c), sibling)
                fwd.start()
                sends.append(fwd)
        for a in range(n_arr):
            copy(a, 0, sibling, me).wait_recv()
            for j, chip in enumerate(chips):
                copy(a, 4 + j, (*chip, 1 - c), me).wait_recv()
        for cp in sends:
            cp.wait_send()
        for cp in mine:
            cp.wait()
        if sum_out:
            for a in range(n_arr):
                acc = out_refs[a][0]
                for k in range(1, N_DEV):
                    acc = acc + out_refs[a][k]
                sum_refs[a][...] = acc

    return body


def all_gather(arrs, *, name, hbm, sum_out=False):
    n = len(arrs)
    spec = ANY if hbm else VMEM_SPEC
    out_shape = [jax.ShapeDtypeStruct((N_DEV,) + a.shape, a.dtype) for a in arrs]
    out_specs = [spec] * n
    if sum_out:
        out_shape += [jax.ShapeDtypeStruct(a.shape, a.dtype) for a in arrs]
        out_specs += [VMEM_SPEC] * n
    res = pl.pallas_call(
        _gather_body(n, sum_out), name=name,
        out_shape=out_shape, in_specs=[spec] * n, out_specs=out_specs,
        scratch_shapes=[pltpu.SemaphoreType.DMA((n, 7)), pltpu.SemaphoreType.DMA((n, 7)),
                        pltpu.SemaphoreType.DMA((n,))],
        compiler_params=pltpu.CompilerParams(vmem_limit_bytes=VMEM_LIMIT),
    )(*arrs)
    return res


_RELATIONS = [(rx, ry, rc) for rx in (0, 1) for ry in (0, 1) for rc in (0, 1)][1:]


HBM_SPEC = pl.BlockSpec(memory_space=pltpu.HBM)
SEM_SPEC = pl.BlockSpec(memory_space=pltpu.SEMAPHORE)
_DATAFLOW = pltpu.SideEffectType.DATAFLOW_SIDE_EFFECTING
N_PEERS = N_DEV - 1


def _peer_copy(src_ref, land_ref, send_sem, recv_sem, k, rel, scatter, sending):
    x, y, c = lax.axis_index("x"), lax.axis_index("y"), lax.axis_index("c")
    rx, ry, rc = rel
    peer = (1 - x if rx else x, 1 - y if ry else y, 1 - c if rc else c)
    src = src_ref.at[_flat_id(peer)] if scatter else src_ref
    dst = land_ref.at[_flat_id((x, y, c)) if sending else _flat_id(peer)]
    return pltpu.make_async_remote_copy(src_ref=src, dst_ref=dst, send_sem=send_sem.at[k], recv_sem=recv_sem.at[k],
                                        device_id=peer, device_id_type=MESH)


def exchange_start(arrs, *, scatter, name):
    n = len(arrs)
    lands = [lax.empty((N_DEV,) + (a.shape[1:] if scatter else a.shape), a.dtype) for a in arrs]

    def body(*refs):
        srcs, lnds, sems = refs[:n], refs[n:2 * n], refs[2 * n:4 * n]
        token = refs[6 * n]
        for a in range(n):
            for k, rel in enumerate(_RELATIONS):
                _peer_copy(srcs[a], lnds[a], sems[2 * a], sems[2 * a + 1], k, rel, scatter, True).start()
        token[...] = jnp.zeros_like(token)

    outs = pl.pallas_call(
        body, name=name,
        out_shape=[pltpu.SemaphoreType.DMA((N_PEERS,))] * (2 * n)
        + [pltpu.HBM(a.shape, a.dtype) for a in arrs] + [pltpu.HBM(l.shape, l.dtype) for l in lands]
        + [jax.ShapeDtypeStruct((8, 128), F32)],
        in_specs=[HBM_SPEC] * (2 * n),
        out_specs=[SEM_SPEC] * (2 * n) + [HBM_SPEC] * (2 * n) + [VMEM_SPEC],
        input_output_aliases={i: 2 * n + i for i in range(2 * n)},
        compiler_params=pltpu.CompilerParams(has_side_effects=_DATAFLOW),
    )(*[pltpu.with_memory_space_constraint(a, pltpu.HBM) for a in arrs],
      *[pltpu.with_memory_space_constraint(l, pltpu.HBM) for l in lands])
    handles = [(outs[2 * a], outs[2 * a + 1], outs[2 * n + a], outs[3 * n + a]) for a in range(n)]
    return handles, outs[4 * n]


def exchange_wait(handle, after, *, scatter, name):
    send_sem, recv_sem, src, land = handle

    def body(src_ref, land_ref, send_ref, recv_ref, after_ref, src_out, land_out):
        for k, rel in enumerate(_RELATIONS):
            cp = _peer_copy(src_ref, land_ref, send_ref, recv_ref, k, rel, scatter, False)
            cp.wait_send()
            cp.wait_recv()

    outs = pl.pallas_call(
        body, name=name,
        out_shape=[pltpu.HBM(src.shape, src.dtype), pltpu.HBM(land.shape, land.dtype)],
        in_specs=[HBM_SPEC, HBM_SPEC, SEM_SPEC, SEM_SPEC, ANY],
        out_specs=[HBM_SPEC, HBM_SPEC], input_output_aliases={0: 0, 1: 1},
        compiler_params=pltpu.CompilerParams(has_side_effects=_DATAFLOW),
    )(src, land, send_sem, recv_sem, after)
    return outs[0], outs[1]


_DIMS = {"nn": (((1,), (0,)), ((), ())), "nt": (((1,), (1,)), ((), ())), "tn": (((0,), (0,)), ((), ()))}


def matmul(a, b, *, mode, tm, tn, tk, out_dtype, name, n_out=None, res=None, gate=None, add=None):
    if mode == "tn":
        kdim, m = a.shape
    else:
        m, kdim = a.shape
    nfull = b.shape[0] if mode == "nt" else b.shape[1]
    n = nfull if n_out is None else n_out
    tm, tn, tk = min(tm, m), min(tn, n), min(tk, kdim)
    assert m % tm == 0 and n % tn == 0 and kdim % tk == 0, (name, m, n, kdim, tm, tn, tk)
    nk = kdim // tk
    dims = _DIMS[mode]
    a_spec = pl.BlockSpec((tk, tm), lambda j, i, k: (k, i)) if mode == "tn" else pl.BlockSpec((tm, tk), lambda j, i, k: (i, k))
    b_spec = pl.BlockSpec((tn, tk), lambda j, i, k: (j, k)) if mode == "nt" else pl.BlockSpec((tk, tn), lambda j, i, k: (k, j))
    o_spec = pl.BlockSpec((tm, tn), lambda j, i, k: (i, j))
    in_specs = [a_spec, b_spec]
    operands = [a, b]
    aliases = {}
    if res is not None:
        in_specs += [o_spec, pl.BlockSpec((1, tn), lambda j, i, k: (0, j))]
        operands += [res, gate]
    if add is not None:
        in_specs += [o_spec]
        aliases = {len(operands): 0}
        operands += [add]
    out_cols = n if add is None else add.shape[1]
    out_shape = [jax.ShapeDtypeStruct((m, out_cols), out_dtype)]
    out_specs = [o_spec]
    if res is not None:
        out_shape.append(jax.ShapeDtypeStruct((m, n), BF16))
        out_specs.append(o_spec)

    def body(*refs):
        a_ref, b_ref = refs[:2]
        pos = 2
        if res is not None:
            res_ref, gate_ref = refs[pos:pos + 2]
            pos += 2
        if add is not None:
            add_ref = refs[pos]
            pos += 1
        o_ref = refs[pos]
        pos += 1
        if res is not None:
            r_ref = refs[pos]
            pos += 1
        acc_ref = refs[pos] if nk > 1 else None
        prod = lax.dot_general(a_ref[...].astype(BF16), b_ref[...].astype(BF16), dims,
                               preferred_element_type=F32)

        def finish(acc):
            if res is not None:
                o_ref[...] = (res_ref[...] + gate_ref[...] * acc).astype(out_dtype)
                r_ref[...] = acc.astype(BF16)
            elif add is not None:
                o_ref[...] = (add_ref[...] + acc).astype(out_dtype)
            else:
                o_ref[...] = acc.astype(out_dtype)

        if nk == 1:
            finish(prod)
        else:
            k = pl.program_id(2)

            @pl.when(k == 0)
            def _():
                acc_ref[...] = prod

            @pl.when(k > 0)
            def _():
                acc_ref[...] += prod

            @pl.when(k == nk - 1)
            def _():
                finish(acc_ref[...])

    outs = pl.pallas_call(
        body, name=name, grid=(n // tn, m // tm, nk),
        in_specs=in_specs, out_specs=out_specs, out_shape=out_shape,
        scratch_shapes=[pltpu.VMEM((tm, tn), F32)] if nk > 1 else [],
        input_output_aliases=aliases,
        compiler_params=_params("parallel", "parallel", "arbitrary"),
    )(*operands)
    return outs if res is not None else outs[0]


def _rows_tile(n, want):
    return min(n, want)


def norm_mod_fwd(x, g, shift, scale, *, name):
    n, d = x.shape
    tt = _rows_tile(n, 256)

    def body(x_ref, g_ref, sh_ref, sc_ref, h_ref):
        xv = x_ref[...]
        rstd = lax.rsqrt(jnp.mean(xv * xv, axis=-1, keepdims=True) + EPS)
        y = xv * rstd * g_ref[...]
        h_ref[...] = (y * (1.0 + sc_ref[...]) + sh_ref[...]).astype(BF16)

    vec = pl.BlockSpec((1, d), lambda i: (0, 0))
    return pl.pallas_call(
        body, name=name, grid=(n // tt,),
        in_specs=[pl.BlockSpec((tt, d), lambda i: (i, 0)), vec, vec, vec],
        out_specs=pl.BlockSpec((tt, d), lambda i: (i, 0)),
        out_shape=jax.ShapeDtypeStruct((n, d), BF16),
        compiler_params=_params("parallel"),
    )(x, _row(g), _row(shift), _row(scale))


def norm_mod_bwd(x, dh, dres, g, scale, *, name):
    n, d = x.shape
    tt = _rows_tile(n, 256)
    has_res = dres is not None
    last = n // tt - 1

    def body(*refs):
        x_ref, dh_ref = refs[:2]
        pos = 2
        if has_res:
            dres_ref = refs[pos]
            pos += 1
        g_ref, sc_ref, dx_ref, acc_ref, s_ref = refs[pos:pos + 5]
        i = pl.program_id(0)
        xv = x_ref[...]
        dhv = dh_ref[...]
        rstd = lax.rsqrt(jnp.mean(xv * xv, axis=-1, keepdims=True) + EPS)
        xhat = xv * rstd
        dxhat = dhv * (g_ref[...] * (1.0 + sc_ref[...]))
        dx = rstd * (dxhat - xhat * jnp.mean(dxhat * xhat, axis=-1, keepdims=True))
        if has_res:
            dx = dx + dres_ref[...]
        dx_ref[...] = dx

        @pl.when(i == 0)
        def _():
            s_ref[...] = jnp.zeros_like(s_ref)

        s_ref[0:1, :] += jnp.sum(dhv, axis=0, keepdims=True)
        s_ref[1:2, :] += jnp.sum(dhv * xhat, axis=0, keepdims=True)

        @pl.when(i == last)
        def _():
            s1 = s_ref[0:1, :]
            s2 = s_ref[1:2, :]
            acc_ref[...] = jnp.zeros_like(acc_ref)
            acc_ref[0:1, :] = s1
            acc_ref[1:2, :] = s2 * g_ref[...]
            acc_ref[2:3, :] = s2 * (1.0 + sc_ref[...])

    vec = pl.BlockSpec((1, d), lambda i: (0, 0))
    big = pl.BlockSpec((tt, d), lambda i: (i, 0))
    ops = [x, dh] + ([dres] if has_res else []) + [_row(g), _row(scale)]
    return pl.pallas_call(
        body, name=name, grid=(n // tt,),
        in_specs=[big, big] + ([big] if has_res else []) + [vec, vec],
        out_specs=[big, pl.BlockSpec((8, d), lambda i: (0, 0))],
        out_shape=[jax.ShapeDtypeStruct((n, d), F32), jax.ShapeDtypeStruct((8, d), F32)],
        scratch_shapes=[pltpu.VMEM((8, d), F32)],
        compiler_params=_params("arbitrary"),
    )(*ops)


def gate_bwd(dx, r, gate, *, name):
    n, d = dx.shape
    tt = _rows_tile(n, 256)

    def body(dx_ref, r_ref, gate_ref, dr_ref, acc_ref):
        i = pl.program_id(0)
        dxv = dx_ref[...]
        dr_ref[...] = (dxv * gate_ref[...]).astype(BF16)

        @pl.when(i == 0)
        def _():
            acc_ref[...] = jnp.zeros_like(acc_ref)

        acc_ref[0:1, :] += jnp.sum(dxv * r_ref[...].astype(F32), axis=0, keepdims=True)

    big = pl.BlockSpec((tt, d), lambda i: (i, 0))
    return pl.pallas_call(
        body, name=name, grid=(n // tt,),
        in_specs=[big, big, pl.BlockSpec((1, d), lambda i: (0, 0))],
        out_specs=[big, pl.BlockSpec((8, d), lambda i: (0, 0))],
        out_shape=[jax.ShapeDtypeStruct((n, d), BF16), jax.ShapeDtypeStruct((8, d), F32)],
        compiler_params=_params("arbitrary"),
    )(dx, r, _row(gate))


def final_loss(x, target, g, *, name):
    n, d = x.shape
    tt = _rows_tile(n, 256)

    def body(x_ref, t_ref, g_ref, dx_ref, acc_ref, loss_ref):
        i = pl.program_id(0)
        xv = x_ref[...]
        rstd = lax.rsqrt(jnp.mean(xv * xv, axis=-1, keepdims=True) + EPS)
        xhat = xv * rstd
        e = xhat * g_ref[...] - t_ref[...]
        dy = e * (1.0 / d)
        dxhat = dy * g_ref[...]
        dx_ref[...] = rstd * (dxhat - xhat * jnp.mean(dxhat * xhat, axis=-1, keepdims=True))

        @pl.when(i == 0)
        def _():
            acc_ref[...] = jnp.zeros_like(acc_ref)
            loss_ref[...] = jnp.zeros_like(loss_ref)

        acc_ref[0:1, :] += jnp.sum(dy * xhat, axis=0, keepdims=True)
        part = 0.5 * jnp.sum(jnp.mean(e * e, axis=-1, keepdims=True), axis=0, keepdims=True)
        loss_ref[...] += jnp.broadcast_to(part, loss_ref.shape)

    big = pl.BlockSpec((tt, d), lambda i: (i, 0))
    return pl.pallas_call(
        body, name=name, grid=(n // tt,),
        in_specs=[big, big, pl.BlockSpec((1, d), lambda i: (0, 0))],
        out_specs=[big, pl.BlockSpec((8, d), lambda i: (0, 0)), pl.BlockSpec((8, 128), lambda i: (0, 0))],
        out_shape=[jax.ShapeDtypeStruct((n, d), F32), jax.ShapeDtypeStruct((8, d), F32),
                   jax.ShapeDtypeStruct((8, 128), F32)],
        compiler_params=_params("arbitrary"),
    )(x, target, _row(g))


def _swap_pairs(x):
    lane = lax.broadcasted_iota(jnp.int32, x.shape, 1)
    return jnp.where(lane % 2 == 0, pltpu.roll(x, HEAD_DIM - 1, 1), pltpu.roll(x, 1, 1))


def rope_tables(n):
    rows = n // GRID_W
    row = jnp.repeat(jnp.arange(rows, dtype=F32), GRID_W)
    col = jnp.tile(jnp.arange(GRID_W, dtype=F32), rows)
    n_freq = HEAD_DIM // 4
    inv = jnp.power(ROPE_THETA, jnp.arange(n_freq, dtype=F32) * (-2.0 / (HEAD_DIM // 2)))
    ang = jnp.concatenate([row[:, None] * inv, col[:, None] * inv], axis=-1)
    cos, sin = jnp.cos(ang), jnp.sin(ang)
    cexp = jnp.repeat(cos, 2, axis=-1)
    sexp = jnp.stack([-sin, sin], axis=-1).reshape(n, HEAD_DIM)
    return cexp, sexp


def qkv_prep_fwd(p, wq, wk, cexp, sexp, *, latent, name):
    n = p.shape[0]
    tt = _rows_tile(n, 256)
    width = 2 * KV_W + (ATTN_W if latent else 0)

    def body(*refs):
        if latent:
            p_ref, wq_ref, wk_ref, c_ref, s_ref, q_ref, k_ref, v_ref = refs
        else:
            p_ref, wk_ref, k_ref, v_ref = refs

        def head(xv, w):
            rstd = lax.rsqrt(jnp.mean(xv * xv, axis=-1, keepdims=True) + EPS)
            yv = xv * rstd * w
            if latent:
                yv = yv * c_ref[...] + _swap_pairs(yv) * s_ref[...]
            return yv

        for h in range(N_KV_HEADS):
            sl = slice(h * HEAD_DIM, (h + 1) * HEAD_DIM)
            k_ref[:, sl] = head(p_ref[:, sl], wk_ref[...]).astype(BF16)
        v_ref[...] = p_ref[:, KV_W:2 * KV_W].astype(BF16)
        if latent:
            for h in range(N_Q_HEADS):
                sl = slice(2 * KV_W + h * HEAD_DIM, 2 * KV_W + (h + 1) * HEAD_DIM)
                q_ref[:, h * HEAD_DIM:(h + 1) * HEAD_DIM] = (head(p_ref[:, sl], wq_ref[...]) * Q_SCALE).astype(BF16)

    vec = pl.BlockSpec((1, HEAD_DIM), lambda i: (0, 0))
    tab = pl.BlockSpec((tt, HEAD_DIM), lambda i: (i, 0))
    kv_spec = pl.BlockSpec((tt, KV_W), lambda i: (i, 0))
    kv_shape = jax.ShapeDtypeStruct((n, KV_W), BF16)
    if latent:
        in_specs = [pl.BlockSpec((tt, width), lambda i: (i, 0)), vec, vec, tab, tab]
        ops = [p, _row(wq), _row(wk), cexp, sexp]
        out_specs = [pl.BlockSpec((tt, ATTN_W), lambda i: (i, 0)), kv_spec, kv_spec]
        out_shape = [jax.ShapeDtypeStruct((n, ATTN_W), BF16), kv_shape, kv_shape]
    else:
        in_specs = [pl.BlockSpec((tt, width), lambda i: (i, 0)), vec]
        ops = [p, _row(wk)]
        out_specs = [kv_spec, kv_spec]
        out_shape = [kv_shape, kv_shape]
    return pl.pallas_call(
        body, name=name, grid=(n // tt,), in_specs=in_specs, out_specs=out_specs, out_shape=out_shape,
        compiler_params=_params("parallel"),
    )(*ops)


def qkv_prep_bwd(p, dq, dk, dv, wq, wk, cexp, sexp, *, latent, name):
    n = p.shape[0]
    tt = _rows_tile(n, 256)
    width = 2 * KV_W + (ATTN_W if latent else 0)

    def body(*refs):
        if latent:
            p_ref, dq_ref, dk_ref, dv_ref, wq_ref, wk_ref, c_ref, s_ref, dp_ref, acc_ref = refs
        else:
            p_ref, dk_ref, dv_ref, wk_ref, dp_ref, acc_ref = refs
        i = pl.program_id(0)

        @pl.when(i == 0)
        def _():
            acc_ref[...] = jnp.zeros_like(acc_ref)

        def head(xv, dy, w, row):
            if latent:
                dy = dy * c_ref[...] + _swap_pairs(dy * s_ref[...])
            rstd = lax.rsqrt(jnp.mean(xv * xv, axis=-1, keepdims=True) + EPS)
            xhat = xv * rstd
            acc_ref[row:row + 1, :] += jnp.sum(dy * xhat, axis=0, keepdims=True)
            dxhat = dy * w
            return rstd * (dxhat - xhat * jnp.mean(dxhat * xhat, axis=-1, keepdims=True))

        for h in range(N_KV_HEADS):
            sl = slice(h * HEAD_DIM, (h + 1) * HEAD_DIM)
            dp_ref[:, sl] = head(p_ref[:, sl], dk_ref[:, sl], wk_ref[...], 1).astype(BF16)
        dp_ref[:, KV_W:2 * KV_W] = dv_ref[...].astype(BF16)
        if latent:
            for h in range(N_Q_HEADS):
                sl = slice(2 * KV_W + h * HEAD_DIM, 2 * KV_W + (h + 1) * HEAD_DIM)
                dyq = dq_ref[:, h * HEAD_DIM:(h + 1) * HEAD_DIM] * Q_SCALE
                dp_ref[:, sl] = head(p_ref[:, sl], dyq, wq_ref[...], 0).astype(BF16)

    vec = pl.BlockSpec((1, HEAD_DIM), lambda i: (0, 0))
    tab = pl.BlockSpec((tt, HEAD_DIM), lambda i: (i, 0))
    kv_spec = pl.BlockSpec((tt, KV_W), lambda i: (i, 0))
    p_spec = pl.BlockSpec((tt, width), lambda i: (i, 0))
    if latent:
        in_specs = [p_spec, pl.BlockSpec((tt, ATTN_W), lambda i: (i, 0)), kv_spec, kv_spec, vec, vec, tab, tab]
        ops = [p, dq, dk, dv, _row(wq), _row(wk), cexp, sexp]
    else:
        in_specs = [p_spec, kv_spec, kv_spec, vec]
        ops = [p, dk, dv, _row(wk)]
    return pl.pallas_call(
        body, name=name, grid=(n // tt,), in_specs=in_specs,
        out_specs=[p_spec, pl.BlockSpec((8, HEAD_DIM), lambda i: (0, 0))],
        out_shape=[jax.ShapeDtypeStruct((n, width), BF16), jax.ShapeDtypeStruct((8, HEAD_DIM), F32)],
        compiler_params=_params("arbitrary"),
    )(*ops)


def _kv_chunks(n, s_all):
    step = 1024 if n % 1024 == 0 else 256
    chunks = [(s, step) for s in range(0, n, step)]
    if s_all > n:
        chunks.append((n, s_all - n))
    return chunks


def flash_fwd(q, k_all, v_ext, p, *, za_block, name):
    n = q.shape[0]
    s_all = k_all.shape[0]
    tq = _rows_tile(n, 512)
    chunks = _kv_chunks(n, s_all)
    wide = 2 * HEAD_DIM

    def body(q_ref, k_ref, v_ref, za_ref, o_ref, mix_ref, lse_ref, m_ref, acc_ref):
        qv = q_ref[...]
        m_ref[...] = jnp.full_like(m_ref, -jnp.inf)
        acc_ref[...] = jnp.zeros_like(acc_ref)
        for start, size in chunks:
            kc = k_ref[pl.ds(start, size), :]
            vc = v_ref[pl.ds(start, size), :]
            s = lax.dot_general(qv, kc, _DIMS["nt"], preferred_element_type=F32)
            m_old = m_ref[...]
            m_new = jnp.maximum(m_old, jnp.max(s, axis=-1, keepdims=True))
            pr = jnp.exp2(s - m_new)
            alpha = jnp.exp2(m_old - m_new)
            acc_ref[...] = alpha * acc_ref[...] + jnp.dot(pr.astype(BF16), vc, preferred_element_type=F32)
            m_ref[...] = m_new
        acc = acc_ref[...]
        denom = acc[:, HEAD_DIM:HEAD_DIM + 1]
        o = acc[:, :HEAD_DIM] / denom
        o_ref[...] = o.astype(BF16)
        mix_ref[...] = (o * _silu(za_ref[...])).astype(BF16)
        lse_ref[0] = m_ref[...] + jnp.log2(denom)

    qspec = pl.BlockSpec((tq, HEAD_DIM), lambda h, i: (i, h))
    return pl.pallas_call(
        body, name=name, grid=(N_Q_HEADS, n // tq),
        in_specs=[qspec, pl.BlockSpec((s_all, HEAD_DIM), lambda h, i: (0, h // Q_PER_KV)),
                  pl.BlockSpec((s_all, wide), lambda h, i: (0, h // Q_PER_KV)),
                  pl.BlockSpec((tq, HEAD_DIM), lambda h, i: (i, za_block + h))],
        out_specs=[qspec, qspec, pl.BlockSpec((1, tq, 1), lambda h, i: (h, i, 0))],
        out_shape=[jax.ShapeDtypeStruct((n, ATTN_W), BF16), jax.ShapeDtypeStruct((n, ATTN_W), BF16),
                   jax.ShapeDtypeStruct((N_Q_HEADS, n, 1), F32)],
        scratch_shapes=[pltpu.VMEM((tq, 1), F32), pltpu.VMEM((tq, wide), F32)],
        compiler_params=_params("parallel", "parallel"),
    )(q, k_all, v_ext, p)


def attn_gate_bwd(dmix, o, p, *, za_block, name):
    n = o.shape[0]
    tt = _rows_tile(n, 512)

    def body(dm_ref, o_ref, za_ref, do_ref, dos_ref, dza_ref, delta_ref):
        dm = dm_ref[...]
        ov = o_ref[...].astype(F32)
        za = za_ref[...]
        do = dm * _silu(za)
        do_ref[...] = do.astype(BF16)
        dos_ref[...] = (do * LN2).astype(BF16)
        dza_ref[...] = (dm * ov * _dsilu(za)).astype(BF16)
        delta_ref[0] = jnp.sum(do * ov, axis=-1, keepdims=True) * LN2

    spec = pl.BlockSpec((tt, HEAD_DIM), lambda i, h: (i, h))
    shape = jax.ShapeDtypeStruct((n, ATTN_W), BF16)
    return pl.pallas_call(
        body, name=name, grid=(n // tt, N_Q_HEADS),
        in_specs=[spec, spec, pl.BlockSpec((tt, HEAD_DIM), lambda i, h: (i, za_block + h))],
        out_specs=[spec, spec, spec, pl.BlockSpec((1, tt, 1), lambda i, h: (h, i, 0))],
        out_shape=[shape, shape, shape, jax.ShapeDtypeStruct((N_Q_HEADS, n, 1), F32)],
        compiler_params=_params("parallel", "parallel"),
    )(dmix, o, p)


def flash_bwd(q, do, do_s, lse_row, delta_row, k_all, v_all, *, name):
    n = q.shape[0]
    s_all = k_all.shape[0]
    tq = _rows_tile(n, 512)
    chunks = _kv_chunks(n, s_all)

    def body(q_ref, do_ref, dos_ref, lse_ref, dl_ref, k_ref, v_ref, dq_ref, dk_ref, dv_ref):
        g = pl.program_id(1)
        i = pl.program_id(2)

        @pl.when((g == 0) & (i == 0))
        def _():
            dk_ref[...] = jnp.zeros_like(dk_ref)
            dv_ref[...] = jnp.zeros_like(dv_ref)

        qv = q_ref[...]
        dov = do_ref[...]
        dosv = dos_ref[...]
        lse = lse_ref[0]
        dl = dl_ref[0]
        dq = jnp.zeros((tq, HEAD_DIM), F32)
        for start, size in chunks:
            kc = k_ref[pl.ds(start, size), :]
            vc = v_ref[pl.ds(start, size), :]
            st = lax.dot_general(kc, qv, _DIMS["nt"], preferred_element_type=F32)
            pt = jnp.exp2(st - lse)
            dpt = lax.dot_general(vc, dosv, _DIMS["nt"], preferred_element_type=F32)
            dst = (pt * (dpt - dl)).astype(BF16)
            dv_ref[pl.ds(start, size), :] += jnp.dot(pt.astype(BF16), dov, preferred_element_type=F32)
            dk_ref[pl.ds(start, size), :] += jnp.dot(dst, qv, preferred_element_type=F32)
            dq = dq + lax.dot_general(dst, kc, _DIMS["tn"], preferred_element_type=F32)
        dq_ref[...] = dq

    qspec = pl.BlockSpec((tq, HEAD_DIM), lambda kh, g, i: (i, kh * Q_PER_KV + g))
    rowspec = pl.BlockSpec((1, 1, tq), lambda kh, g, i: (kh * Q_PER_KV + g, 0, i))
    kvspec = pl.BlockSpec((s_all, HEAD_DIM), lambda kh, g, i: (0, kh))
    return pl.pallas_call(
        body, name=name, grid=(N_KV_HEADS, Q_PER_KV, n // tq),
        in_specs=[qspec, qspec, qspec, rowspec, rowspec, kvspec, kvspec],
        out_specs=[qspec, kvspec, kvspec],
        out_shape=[jax.ShapeDtypeStruct((n, ATTN_W), F32),
                   jax.ShapeDtypeStruct((s_all, KV_W), F32), jax.ShapeDtypeStruct((s_all, KV_W), F32)],
        compiler_params=_params("arbitrary", "arbitrary", "arbitrary"),
    )(q, do, do_s, lse_row, delta_row, k_all, v_all)


HALF = 512


def _halo_specs(tt, n, cb):
    per = tt // HALO
    last = n // HALO - 1
    return [pl.BlockSpec((HALO, HALF), lambda i: (jnp.maximum(i * per - 1, 0), cb)),
            pl.BlockSpec((tt, HALF), lambda i: (i, cb)),
            pl.BlockSpec((HALO, HALF), lambda i: (jnp.minimum((i + 1) * per, last), cb))]


def conv_fwd(p, dw_w, dw_b, ln_g, ln_b, *, glu_block, name):
    n = p.shape[0]
    ch = dw_w.shape[1]
    nh = ch // HALF
    tt = _rows_tile(n, 256)
    last = n // tt - 1

    def body(*refs):
        a_refs = [refs[3 * h:3 * h + 3] for h in range(nh)]
        b_refs = [refs[3 * (nh + h):3 * (nh + h) + 3] for h in range(nh)]
        pos = 6 * nh
        zb_refs = refs[pos:pos + nh]
        pos += nh
        w_ref, bias_ref, g_ref, be_ref, ycv_ref, mix_ref, ext_ref = refs[pos:pos + 7]
        i = pl.program_id(0)
        for h in range(nh):
            cs = slice(h * HALF, (h + 1) * HALF)
            ap, am, an = a_refs[h]
            bp, bm, bn = b_refs[h]
            ext_ref[0:HALO, :] = jnp.where(i > 0, ap[...] * _sigmoid(bp[...]), 0.0)
            ext_ref[HALO:HALO + tt, :] = am[...] * _sigmoid(bm[...])
            ext_ref[HALO + tt:2 * HALO + tt, :] = jnp.where(i < last, an[...] * _sigmoid(bn[...]), 0.0)
            acc = jnp.broadcast_to(bias_ref[:, cs], (tt, HALF))
            for k in range(CONV_WIDTH):
                acc = acc + w_ref[k:k + 1, cs] * ext_ref[pl.ds(1 + k, tt), :]
            ycv_ref[:, cs] = acc
        yc = ycv_ref[...]
        mu = jnp.mean(yc, axis=-1, keepdims=True)
        var = jnp.mean(jnp.square(yc - mu), axis=-1, keepdims=True)
        ln = (yc - mu) * lax.rsqrt(var + EPS) * g_ref[...] + be_ref[...]
        out = _silu(ln)
        for h in range(nh):
            cs = slice(h * HALF, (h + 1) * HALF)
            mix_ref[:, cs] = (out[:, cs] * _silu(zb_refs[h][...])).astype(BF16)

    in_specs = []
    for h in range(2 * nh):
        in_specs += _halo_specs(tt, n, glu_block + h)
    in_specs += [pl.BlockSpec((tt, HALF), functools.partial(lambda i, cb: (i, cb), cb=glu_block + 2 * nh + h))
                 for h in range(nh)]
    vec = pl.BlockSpec((1, ch), lambda i: (0, 0))
    in_specs += [pl.BlockSpec((CONV_WIDTH, ch), lambda i: (0, 0)), vec, vec, vec]
    ops = [p] * (6 * nh + nh) + [dw_w, _row(dw_b), _row(ln_g), _row(ln_b)]
    big = pl.BlockSpec((tt, ch), lambda i: (i, 0))
    return pl.pallas_call(
        body, name=name, grid=(n // tt,), in_specs=in_specs, out_specs=[big, big],
        out_shape=[jax.ShapeDtypeStruct((n, ch), F32), jax.ShapeDtypeStruct((n, ch), BF16)],
        scratch_shapes=[pltpu.VMEM((tt + 2 * HALO, HALF), F32)],
        compiler_params=_params("parallel"),
    )(*ops)


def conv_bwd_rows(dmix, ycv, p, ln_g, ln_b, *, mix_block, zb_block, name):
    n, ch = ycv.shape
    nh = ch // HALF
    tt = _rows_tile(n, 256)

    def body(*refs):
        dm_ref, ycv_ref = refs[:2]
        zb_refs = refs[2:2 + nh]
        g_ref, be_ref, dy_ref, dzb_ref, acc_ref = refs[2 + nh:]
        i = pl.program_id(0)

        @pl.when(i == 0)
        def _():
            acc_ref[...] = jnp.zeros_like(acc_ref)

        yc = ycv_ref[...]
        mu = jnp.mean(yc, axis=-1, keepdims=True)
        var = jnp.mean(jnp.square(yc - mu), axis=-1, keepdims=True)
        rstd = lax.rsqrt(var + EPS)
        xhat = (yc - mu) * rstd
        ln = xhat * g_ref[...] + be_ref[...]
        out = _silu(ln)
        dm = dm_ref[...]
        zb = jnp.concatenate([r[...] for r in zb_refs], axis=-1)
        dzb_ref[...] = (dm * out * _dsilu(zb)).astype(BF16)
        dln = dm * _silu(zb) * _dsilu(ln)
        acc_ref[0:1, :] += jnp.sum(dln * xhat, axis=0, keepdims=True)
        acc_ref[1:2, :] += jnp.sum(dln, axis=0, keepdims=True)
        dxhat = dln * g_ref[...]
        dy_ref[...] = rstd * (dxhat - jnp.mean(dxhat, axis=-1, keepdims=True)
                              - xhat * jnp.mean(dxhat * xhat, axis=-1, keepdims=True))

    big = pl.BlockSpec((tt, ch), lambda i: (i, 0))
    vec = pl.BlockSpec((1, ch), lambda i: (0, 0))
    in_specs = [pl.BlockSpec((tt, ch), lambda i: (i, mix_block)), big]
    in_specs += [pl.BlockSpec((tt, HALF), functools.partial(lambda i, cb: (i, cb), cb=zb_block + h)) for h in range(nh)]
    in_specs += [vec, vec]
    return pl.pallas_call(
        body, name=name, grid=(n // tt,), in_specs=in_specs,
        out_specs=[big, big, pl.BlockSpec((8, ch), lambda i: (0, 0))],
        out_shape=[jax.ShapeDtypeStruct((n, ch), F32), jax.ShapeDtypeStruct((n, ch), BF16),
                   jax.ShapeDtypeStruct((8, ch), F32)],
        compiler_params=_params("arbitrary"),
    )(dmix, ycv, *([p] * nh), _row(ln_g), _row(ln_b))


def conv_bwd_taps(dycv, p, dw_w, *, glu_block, name):
    n, ch = dycv.shape
    nh = ch // HALF
    tt = _rows_tile(n, 256)
    last = n // tt - 1

    def body(*refs):
        d_refs = [refs[3 * h:3 * h + 3] for h in range(nh)]
        a_refs = [refs[3 * (nh + h):3 * (nh + h) + 3] for h in range(nh)]
        b_refs = [refs[3 * (2 * nh + h):3 * (2 * nh + h) + 3] for h in range(nh)]
        w_ref, dglu_ref, dw_ref, db_ref, yext_ref, dext_ref = refs[9 * nh:]
        i = pl.program_id(0)

        @pl.when(i == 0)
        def _():
            dw_ref[...] = jnp.zeros_like(dw_ref)
            db_ref[...] = jnp.zeros_like(db_ref)

        for h in range(nh):
            cs = slice(h * HALF, (h + 1) * HALF)
            ap, am, an = a_refs[h]
            bp, bm, bn = b_refs[h]
            dp, dm, dn = d_refs[h]
            av = am[...]
            sb = _sigmoid(bm[...])
            yext_ref[0:HALO, :] = jnp.where(i > 0, ap[...] * _sigmoid(bp[...]), 0.0)
            yext_ref[HALO:HALO + tt, :] = av * sb
            yext_ref[HALO + tt:2 * HALO + tt, :] = jnp.where(i < last, an[...] * _sigmoid(bn[...]), 0.0)
            dmain = dm[...]
            dext_ref[0:HALO, :] = jnp.where(i > 0, dp[...], 0.0)
            dext_ref[HALO:HALO + tt, :] = dmain
            dext_ref[HALO + tt:2 * HALO + tt, :] = jnp.where(i < last, dn[...], 0.0)
            dy = jnp.zeros((tt, HALF), F32)
            for k in range(CONV_WIDTH):
                dy = dy + w_ref[k:k + 1, cs] * dext_ref[pl.ds(CONV_WIDTH - k, tt), :]
                dw_ref[k:k + 1, cs] += jnp.sum(dmain * yext_ref[pl.ds(1 + k, tt), :], axis=0, keepdims=True)
            db_ref[0:1, cs] += jnp.sum(dmain, axis=0, keepdims=True)
            dglu_ref[:, cs] = (dy * sb).astype(BF16)
            dglu_ref[:, ch + h * HALF:ch + (h + 1) * HALF] = (dy * av * sb * (1.0 - sb)).astype(BF16)

    in_specs = []
    for h in range(nh):
        in_specs += _halo_specs(tt, n, h)
    for h in range(2 * nh):
        in_specs += _halo_specs(tt, n, glu_block + h)
    in_specs += [pl.BlockSpec((CONV_WIDTH, ch), lambda i: (0, 0))]
    ops = [dycv] * (3 * nh) + [p] * (6 * nh) + [dw_w]
    return pl.pallas_call(
        body, name=name, grid=(n // tt,), in_specs=in_specs,
        out_specs=[pl.BlockSpec((tt, 2 * ch), lambda i: (i, 0)), pl.BlockSpec((32, ch), lambda i: (0, 0)),
                   pl.BlockSpec((8, ch), lambda i: (0, 0))],
        out_shape=[jax.ShapeDtypeStruct((n, 2 * ch), BF16), jax.ShapeDtypeStruct((32, ch), F32),
                   jax.ShapeDtypeStruct((8, ch), F32)],
        scratch_shapes=[pltpu.VMEM((tt + 2 * HALO, HALF), F32), pltpu.VMEM((tt + 2 * HALO, HALF), F32)],
        compiler_params=_params("arbitrary"),
    )(*ops)


def _sgu_common(p_ref, g_ref, be_ref, ws_ref, bs_ref, w):
    gw = w // SGU_GROUPS
    u_pre = p_ref[:, 0:w]
    v_pre = p_ref[:, w:2 * w]
    zc = p_ref[:, 2 * w:3 * w]
    u = _gelu(u_pre)
    v = _gelu(v_pre)
    mu = jnp.mean(v, axis=-1, keepdims=True)
    var = jnp.mean(jnp.square(v - mu), axis=-1, keepdims=True)
    rstd = lax.rsqrt(var + EPS)
    vhat = (v - mu) * rstd
    vn = (vhat * g_ref[...] + be_ref[...]).astype(BF16)
    mixed = jnp.concatenate(
        [jnp.dot(ws_ref[gi].astype(BF16), vn[:, gi * gw:(gi + 1) * gw], preferred_element_type=F32)
         + bs_ref[:, gi:gi + 1] for gi in range(SGU_GROUPS)], axis=-1)
    return u_pre, v_pre, zc, u, rstd, vhat, vn, mixed


def sgu_fwd(p, ln_g, ln_b, ws, bs_t, *, name):
    n, w3 = p.shape
    w = w3 // 3

    def body(p_ref, g_ref, be_ref, ws_ref, bs_ref, m_ref):
        _, _, zc, u, _, _, _, mixed = _sgu_common(p_ref, g_ref, be_ref, ws_ref, bs_ref, w)
        m_ref[...] = (u * mixed * _silu(zc)).astype(BF16)

    vec = pl.BlockSpec((1, w), lambda i: (0, 0))
    return pl.pallas_call(
        body, name=name, grid=(n // CHUNK,),
        in_specs=[pl.BlockSpec((CHUNK, w3), lambda i: (i, 0)), vec, vec,
                  pl.BlockSpec((SGU_GROUPS, CHUNK, CHUNK), lambda i: (0, 0, 0)),
                  pl.BlockSpec((CHUNK, SGU_GROUPS), lambda i: (0, 0))],
        out_specs=pl.BlockSpec((CHUNK, w), lambda i: (i, 0)),
        out_shape=jax.ShapeDtypeStruct((n, w), BF16),
        compiler_params=_params("parallel"),
    )(p, _row(ln_g), _row(ln_b), ws, bs_t)


def sgu_bwd(p, dm, ln_g, ln_b, ws, ws_t, bs_t, *, name):
    n, w3 = p.shape
    w = w3 // 3
    gw = w // SGU_GROUPS

    def body(p_ref, dm_ref, g_ref, be_ref, ws_ref, wst_ref, bs_ref, dp_ref, dws_ref, dbs_ref, acc_ref):
        i = pl.program_id(0)

        @pl.when(i == 0)
        def _():
            dws_ref[...] = jnp.zeros_like(dws_ref)
            dbs_ref[...] = jnp.zeros_like(dbs_ref)
            acc_ref[...] = jnp.zeros_like(acc_ref)

        u_pre, v_pre, zc, u, rstd, vhat, vn, mixed = _sgu_common(p_ref, g_ref, be_ref, ws_ref, bs_ref, w)
        dmv = dm_ref[...]
        um = u * mixed
        dp_ref[:, 2 * w:3 * w] = (dmv * um * _dsilu(zc)).astype(BF16)
        dum = dmv * _silu(zc)
        dp_ref[:, 0:w] = (dum * mixed * _dgelu(u_pre)).astype(BF16)
        dmixed = dum * u
        dmixed_b = dmixed.astype(BF16)
        dvn_parts = []
        for gi in range(SGU_GROUPS):
            cs = slice(gi * gw, (gi + 1) * gw)
            dws_ref[gi] += lax.dot_general(dmixed_b[:, cs], vn[:, cs], _DIMS["nt"], preferred_element_type=F32)
            dbs_ref[:, gi:gi + 1] += jnp.sum(dmixed[:, cs], axis=-1, keepdims=True)
            dvn_parts.append(jnp.dot(wst_ref[gi].astype(BF16), dmixed_b[:, cs], preferred_element_type=F32))
        dvn = jnp.concatenate(dvn_parts, axis=-1)
        acc_ref[0:1, :] += jnp.sum(dvn * vhat, axis=0, keepdims=True)
        acc_ref[1:2, :] += jnp.sum(dvn, axis=0, keepdims=True)
        dvhat = dvn * g_ref[...]
        dv = rstd * (dvhat - jnp.mean(dvhat, axis=-1, keepdims=True)
                     - vhat * jnp.mean(dvhat * vhat, axis=-1, keepdims=True))
        dp_ref[:, w:2 * w] = (dv * _dgelu(v_pre)).astype(BF16)

    vec = pl.BlockSpec((1, w), lambda i: (0, 0))
    wspec = pl.BlockSpec((SGU_GROUPS, CHUNK, CHUNK), lambda i: (0, 0, 0))
    bspec = pl.BlockSpec((CHUNK, SGU_GROUPS), lambda i: (0, 0))
    return pl.pallas_call(
        body, name=name, grid=(n // CHUNK,),
        in_specs=[pl.BlockSpec((CHUNK, w3), lambda i: (i, 0)), pl.BlockSpec((CHUNK, w), lambda i: (i, 0)),
                  vec, vec, wspec, wspec, bspec],
        out_specs=[pl.BlockSpec((CHUNK, w3), lambda i: (i, 0)), wspec, bspec,
                   pl.BlockSpec((8, w), lambda i: (0, 0))],
        out_shape=[jax.ShapeDtypeStruct((n, w3), BF16), jax.ShapeDtypeStruct((SGU_GROUPS, CHUNK, CHUNK), F32),
                   jax.ShapeDtypeStruct((CHUNK, SGU_GROUPS), F32), jax.ShapeDtypeStruct((8, w), F32)],
        compiler_params=_params("arbitrary"),
    )(p, dm, _row(ln_g), _row(ln_b), ws, ws_t, bs_t)


def _adam_math(w, g, m, v):
    m_new = ADAM_B1 * m + (1.0 - ADAM_B1) * g
    v_new = ADAM_B2 * v + (1.0 - ADAM_B2) * (g * g)
    m_hat = m_new / (1.0 - ADAM_B1 ** ADAM_STEP)
    v_hat = v_new / (1.0 - ADAM_B2 ** ADAM_STEP)
    delta = -ADAM_LR * (m_hat / (jnp.sqrt(v_hat) + ADAM_EPS) + ADAM_WD * w)
    return delta, m_new, v_new


def adamw(w, g, m, v, *, name, slots=False, rows=512):
    r, c = w.shape
    tr = min(r, rows)
    assert r % tr == 0, (name, r, tr)

    def body(w_ref, g_ref, m_ref, v_ref, go_ref, d_ref, mo_ref, vo_ref):
        if slots:
            g = g_ref[0].astype(F32)
            for k in range(1, N_DEV):
                g = g + g_ref[k].astype(F32)
        else:
            g = g_ref[...].astype(F32)
        delta, m_new, v_new = _adam_math(w_ref[...], g, m_ref[...], v_ref[...])
        go_ref[...] = g
        d_ref[...] = delta
        mo_ref[...] = m_new
        vo_ref[...] = v_new

    spec = pl.BlockSpec((tr, c), lambda i: (i, 0))
    gspec = pl.BlockSpec((N_DEV, tr, c), lambda i: (0, i, 0)) if slots else spec
    shape = jax.ShapeDtypeStruct((r, c), F32)
    return pl.pallas_call(
        body, name=name, grid=(r // tr,),
        in_specs=[spec, gspec, spec, spec], out_specs=[spec] * 4, out_shape=[shape] * 4,
        compiler_params=_params("parallel"),
    )(w, g, m, v)


def _after(dep, vals):
    return lax.optimization_barrier((dep, vals))[1]


def _pad_rows(a, rows):
    return jnp.pad(a, ((0, rows - a.shape[0]), (0, 0)))


def kernel(x, c, ctx, c_ctx, ada_w, ada_b, norm_g, ev_w_in, ev_q_norm, ev_k_norm, ev_dw_w, ev_dw_b, ev_ln_g, ev_ln_b, ev_w_out, od_w_in, od_ln_g, od_ln_b, od_ws, od_bs, od_w_out, final_g, loss_target, m_c_ctx, m_ada_w, m_ada_b, m_norm_g, m_ev_w_in, m_ev_q_norm, m_ev_k_norm, m_ev_dw_w, m_ev_dw_b, m_ev_ln_g, m_ev_ln_b, m_ev_w_out, m_od_w_in, m_od_ln_g, m_od_ln_b, m_od_ws, m_od_bs, m_od_w_out, m_final_g, v_c_ctx, v_ada_w, v_ada_b, v_norm_g, v_ev_w_in, v_ev_q_norm, v_ev_k_norm, v_ev_dw_w, v_ev_dw_b, v_ev_ln_g, v_ev_ln_b, v_ev_w_out, v_od_w_in, v_od_ln_g, v_od_ln_b, v_od_ws, v_od_bs, v_od_w_out, v_final_g):
    n, d = x.shape[1], x.shape[2]
    lc = ctx.shape[1]
    ev_in = ev_w_in.shape[2] * N_DEV
    od_in = od_w_in.shape[2] * N_DEV
    conv_ch = ev_dw_w.shape[2] * N_DEV
    ada_cols = ada_w.shape[2]
    me = 4 * lax.axis_index("x") + 2 * lax.axis_index("y") + lax.axis_index("c")
    xs, tgt, ctxs = x[0], loss_target[0], ctx[0]
    za_block = (2 * KV_W + ATTN_W) // HEAD_DIM
    glu_block = (2 * KV_W + 2 * ATTN_W) // HALF
    zb_block = glu_block + 2 * conv_ch // HALF

    small = jnp.concatenate([
        jax.nn.silu(c).reshape(1, d),
        od_ln_g.reshape(1, -1), od_ln_b.reshape(1, -1)], axis=1)
    small = _pad_rows(small, 8)
    dw_rows = _pad_rows(ev_dw_w[0], 32)
    small_g, dw_g = all_gather([small, dw_rows], name="gather_small", hbm=False)
    sc_all = small_g[:, 0, :d]
    shard = d // N_DEV
    od_ln_g_full = small_g[:, 0, d:d + shard].reshape(d)
    od_ln_b_full = small_g[:, 0, d + shard:d + 2 * shard].reshape(d)
    dw_w_full = jnp.moveaxis(dw_g, 0, 1).reshape(32, conv_ch)[:CONV_WIDTH]
    scc = jax.nn.silu(c_ctx)
    sc16 = _pad_rows(jnp.concatenate([sc_all, scc.reshape(1, d)], axis=0), 16)

    ada_bf = ada_w.astype(BF16)
    mod_loc = [matmul(sc16, ada_bf[l], mode="nn", tm=16, tn=ada_cols, tk=d, out_dtype=F32, name=f"ada_mod{l}")
               for l in range(2)]
    (mod_g,) = all_gather([jnp.stack(mod_loc)], name="gather_mod", hbm=False)
    mod_all = jnp.moveaxis(mod_g, 0, 2).reshape(2, 16, N_DEV * ada_cols) + ada_b[:, None, :]
    mod_me = lax.dynamic_index_in_dim(mod_all, me, axis=1, keepdims=False)
    shift = [mod_me[l, :d] for l in range(2)]
    scale = [mod_me[l, d:2 * d] for l in range(2)]
    gate = [mod_me[l, 2 * d:] for l in range(2)]
    shift_c, scale_c = mod_all[0, 8, :d], mod_all[0, 8, d:2 * d]

    w_parts = [ev_w_in[0].astype(BF16), od_w_in[0].astype(BF16), ev_w_out[0].astype(BF16), od_w_out[0].astype(BF16)]
    (wi0_g,) = all_gather([w_parts[0]], name="gather_w_in0", hbm=True)
    wi0 = jnp.moveaxis(wi0_g, 0, 1).reshape(d, ev_in)
    later = _after(wi0_g, (w_parts[2], w_parts[1], w_parts[3]))
    (h_wo0, h_wi1, h_wo1), w_token = exchange_start(list(later), scatter=False, name="gather_rest_start")

    def landed(handle, after, name):
        own, land = exchange_wait(handle, after, scatter=False, name=name)
        return lax.dynamic_update_slice_in_dim(land, own[None], me, axis=0)

    cexp, sexp = rope_tables(n)

    shift0 = _after(w_token, shift[0])
    h0 = norm_mod_fwd(xs, norm_g[0], shift0, scale[0], name="norm_mod_fwd0")
    hc = norm_mod_fwd(ctxs, norm_g[0], shift_c, scale_c, name="norm_mod_fwd_ctx")
    p0 = matmul(h0, wi0, mode="nn", tm=1024, tn=ev_in // 4, tk=d, out_dtype=F32, name="proj_in0")
    pc = matmul(hc, wi0, mode="nn", tm=lc, tn=2 * KV_W, tk=d, out_dtype=F32, n_out=2 * KV_W, name="proj_in_ctx")
    q_r, k_lat, v_lat = qkv_prep_fwd(p0, ev_q_norm[0], ev_k_norm[0], cexp, sexp, latent=True, name="qkv_prep")
    k_ctx, v_ctx = qkv_prep_fwd(pc, None, ev_k_norm[0], None, None, latent=False, name="kv_prep_ctx")
    k_all = jnp.concatenate([k_lat, k_ctx], axis=0)
    v_all = jnp.concatenate([v_lat, v_ctx], axis=0)
    s_all = v_all.shape[0]
    ones_col = jnp.concatenate([jnp.ones((s_all, 1), BF16), jnp.zeros((s_all, HEAD_DIM - 1), BF16)], axis=1)
    v_ext = jnp.concatenate([blk for h in range(N_KV_HEADS)
                             for blk in (v_all[:, h * HEAD_DIM:(h + 1) * HEAD_DIM], ones_col)], axis=1)
    o_attn, mix_a, lse = flash_fwd(q_r, k_all, v_ext, p0, za_block=za_block, name="flash_fwd")
    ycv, mix_b = conv_fwd(p0, dw_w_full, ev_dw_b[0], ev_ln_g[0], ev_ln_b[0], glu_block=glu_block, name="conv_fwd")
    mix0 = jnp.concatenate([mix_a, mix_b], axis=1)
    wo0 = landed(h_wo0, mix_b, "gather_w_out0_wait").reshape(-1, d)
    x1, r0 = matmul(mix0, wo0, mode="nn", tm=1024, tn=1024, tk=mix0.shape[1], out_dtype=F32, name="proj_out0",
                    res=xs, gate=_row(gate[0]))

    h1 = norm_mod_fwd(x1, norm_g[1], shift[1], scale[1], name="norm_mod_fwd1")
    wi1 = jnp.moveaxis(landed(h_wi1, h1, "gather_w_in1_wait"), 0, 1).reshape(d, od_in)
    p1 = matmul(h1, wi1, mode="nn", tm=1024, tn=od_in // 4, tk=d, out_dtype=F32, name="proj_in1")
    ws_bf = od_ws[0]
    bs_t = od_bs[0].T
    m1 = sgu_fwd(p1, od_ln_g_full, od_ln_b_full, ws_bf, bs_t, name="sgu_fwd")
    wo1 = landed(h_wo1, m1, "gather_w_out1_wait").reshape(-1, d)
    x2, r1 = matmul(m1, wo1, mode="nn", tm=1024, tn=1024, tk=m1.shape[1], out_dtype=F32, name="proj_out1",
                    res=x1, gate=_row(gate[1]))

    dx2, acc_final, loss_tile = final_loss(x2, tgt, final_g, name="final_loss")

    dr1, acc_gate1 = gate_bwd(dx2, r1, gate[1], name="gate_bwd1")
    dm1 = matmul(dr1, wo1, mode="nt", tm=1024, tn=1024, tk=d, out_dtype=F32, name="d_mix1")
    dwo1 = matmul(m1, dr1, mode="tn", tm=512, tn=512, tk=4096,out_dtype=BF16, name="d_wout1")
    dp1, dws, dbs_t, acc_sgu = sgu_bwd(p1, dm1, od_ln_g_full, od_ln_b_full, ws_bf, jnp.swapaxes(ws_bf, 1, 2), bs_t,
                                       name="sgu_bwd")
    dwi1 = matmul(h1, dp1, mode="tn", tm=512, tn=od_in // N_DEV, tk=4096,out_dtype=BF16, name="d_win1")
    dwi1_s = jnp.moveaxis(dwi1.reshape(d, N_DEV, od_in // N_DEV), 1, 0)
    dwo1_s = dwo1.reshape(N_DEV, -1, d)
    (h_gi1, h_go1), g1_token = exchange_start([dwi1_s, dwo1_s], scatter=True, name="grads1_start")
    dh1 = matmul(dp1, wi1, mode="nt", tm=1024, tn=512, tk=od_in, out_dtype=F32, name="d_h1")
    dx1, acc_norm1 = norm_mod_bwd(x1, dh1, dx2, norm_g[1], _after(g1_token, scale[1]), name="norm_mod_bwd1")

    dr0, acc_gate0 = gate_bwd(dx1, r0, gate[0], name="gate_bwd0")
    dmix0 = matmul(dr0, wo0, mode="nt", tm=1024, tn=1024, tk=d, out_dtype=F32, name="d_mix0")
    dwo0 = matmul(mix0, dr0, mode="tn", tm=512, tn=512, tk=4096,out_dtype=BF16, name="d_wout0")
    do_attn, do_s, dza, delta = attn_gate_bwd(dmix0, o_attn, p0, za_block=za_block, name="attn_gate_bwd")
    dycv, dzb, acc_ln = conv_bwd_rows(dmix0, ycv, p0, ev_ln_g[0], ev_ln_b[0], mix_block=ATTN_W // conv_ch,
                                      zb_block=zb_block, name="conv_bwd_rows")
    dglu, ddw_w, acc_dwb = conv_bwd_taps(dycv, p0, dw_w_full, glu_block=glu_block, name="conv_bwd_taps")
    lse_row = lse.reshape(N_Q_HEADS, 1, n)
    delta_row = delta.reshape(N_Q_HEADS, 1, n)
    dq_r, dk_all, dv_all = flash_bwd(q_r, do_attn, do_s, lse_row, delta_row, k_all, v_all, name="flash_bwd")
    dkvq, acc_qk = qkv_prep_bwd(p0, dq_r, dk_all[:n], dv_all[:n], ev_q_norm[0], ev_k_norm[0], cexp, sexp,
                                latent=True, name="qkv_prep_bwd")
    dpc, acc_kc = qkv_prep_bwd(pc, None, dk_all[n:], dv_all[n:], None, ev_k_norm[0], None, None,
                               latent=False, name="kv_prep_ctx_bwd")
    dp0 = jnp.concatenate([dkvq, dza, dglu, dzb], axis=1)
    dwi0 = matmul(h0, dp0, mode="tn", tm=512, tn=512, tk=4096,out_dtype=F32, name="d_win0")
    dwi0 = matmul(hc, dpc, mode="tn", tm=512, tn=2 * KV_W, tk=lc, out_dtype=F32, name="d_win0_ctx", add=dwi0)
    dwi0_s = jnp.moveaxis(dwi0.astype(BF16).reshape(d, N_DEV, ev_in // N_DEV), 1, 0)
    dwo0_s = dwo0.reshape(N_DEV, -1, d)
    (h_gi0, h_go0), g0_token = exchange_start([dwi0_s, dwo0_s], scatter=True, name="grads0_start")
    dh0 = matmul(dp0, wi0, mode="nt", tm=1024, tn=512, tk=ev_in, out_dtype=F32, name="d_h0")
    dhc = matmul(dpc, wi0, mode="nt", tm=lc, tn=512, tk=2 * KV_W, out_dtype=F32, name="d_h_ctx")
    grad_x, acc_norm0 = norm_mod_bwd(xs, dh0, dx1, norm_g[0], _after(g0_token, scale[0]), name="norm_mod_bwd0")
    _, acc_normc = norm_mod_bwd(ctxs, dhc, None, norm_g[0], scale_c, name="norm_mod_bwd_ctx")

    zeros_d = jnp.zeros((d,), F32)
    dmod0 = jnp.stack([acc_norm0[0], acc_norm0[1], acc_gate0[0]])
    dmod1 = jnp.stack([acc_norm1[0], acc_norm1[1], acc_gate1[0]])
    dmodc = jnp.stack([acc_normc[0], acc_normc[1]])
    half_pad = jnp.zeros((d - 2 * conv_ch,), F32) if d > 2 * conv_ch else jnp.zeros((0,), F32)
    row_a = jnp.concatenate([acc_dwb[0], acc_ln[0], half_pad])
    row_b = jnp.concatenate([acc_ln[1], acc_qk[0], acc_qk[1] + acc_kc[1],
                             jnp.zeros((d - conv_ch - 2 * HEAD_DIM,), F32)])
    row_c = jnp.concatenate([dbs_t.T.reshape(-1), jnp.zeros((d - SGU_GROUPS * CHUNK,), F32)])
    row_loss = jnp.concatenate([loss_tile[0, :1], jnp.zeros((d - 1,), F32)])
    pack = jnp.concatenate([
        dmod0, dmod1, dmodc,
        (acc_norm0[2] + acc_normc[2])[None], acc_norm1[2][None],
        acc_final[0][None],
        acc_sgu[0][None], acc_sgu[1][None],
        row_a[None], row_b[None], row_c[None], row_loss[None],
        ddw_w.reshape(-1, d),
        dws.reshape(-1, d),
    ], axis=0)
    n_rows = pack.shape[0]
    pack = _pad_rows(pack, -(-n_rows // 8) * 8)
    pack_g, pack_sum = all_gather([pack], name="gather_small_grads", hbm=False, sum_out=True)
    gsum = pack_sum
    loss = gsum[16, 0]
    dw_rows_n = 32 * conv_ch // d
    g_dw_w = gsum[17:17 + dw_rows_n].reshape(32, conv_ch)[:CONV_WIDTH]
    g_od_ws = gsum[17 + dw_rows_n:17 + dw_rows_n + SGU_GROUPS * CHUNK * CHUNK // d].reshape(od_ws.shape)

    dmodc_sum = jnp.concatenate([gsum[6], gsum[7], zeros_d])
    col0 = me * ada_cols
    dm_cols = []
    for l in range(2):
        rows = pack_g[:, 3 * l:3 * l + 3, :].reshape(N_DEV, 3 * d)
        extra = dmodc_sum[None] if l == 0 else jnp.zeros((1, 3 * d), F32)
        full = _pad_rows(jnp.concatenate([rows, extra], axis=0), 16)
        dm_cols.append(lax.dynamic_slice_in_dim(full, col0, ada_cols, axis=1))
    g_ada_w = jnp.stack([matmul(sc16, dm_cols[l], mode="tn", tm=512, tn=ada_cols, tk=16, out_dtype=F32,
                                name=f"d_ada_w{l}") for l in range(2)])
    dsc = matmul(dm_cols[0], ada_bf[0], mode="nt", tm=16, tn=512, tk=ada_cols, out_dtype=F32, name="d_scc")
    (_, dscc_sum) = all_gather([dsc[8:16]], name="gather_dscc", hbm=False, sum_out=True)
    sg = jax.nn.sigmoid(c_ctx)
    g_c_ctx = dscc_sum[0] * (sg * (1.0 + c_ctx * (1.0 - sg)))
    g_ada_b = jnp.stack([gsum[0:3].reshape(-1) + dmodc_sum, gsum[3:6].reshape(-1)])

    def summands(handle, after, name):
        mine, land = exchange_wait(handle, after, scatter=True, name=name)
        own = lax.dynamic_index_in_dim(mine, me, axis=0, keepdims=True)
        return lax.dynamic_update_slice_in_dim(land, own, me, axis=0)

    out = {}

    def upd(key, w, g, m, v, slots=False, rows=512):
        shp = w.shape
        w2 = w.reshape(-1, shp[-1])
        g2 = g.reshape((N_DEV, -1, shp[-1])) if slots else g.reshape(-1, shp[-1])
        res = adamw(w2, g2, m.reshape(w2.shape), v.reshape(w2.shape), name="adamw_" + key, slots=slots, rows=rows)
        out[key] = tuple(r.reshape(shp) for r in res)

    upd("ada_w", ada_w, g_ada_w, m_ada_w, v_ada_w)
    gi1 = summands(h_gi1,out["ada_w"][1], "grads_w_in1_wait")
    upd("od_w_in", od_w_in, gi1, m_od_w_in, v_od_w_in, slots=True, rows=256)
    go1 = summands(h_go1,out["od_w_in"][1], "grads_w_out1_wait")
    upd("od_w_out", od_w_out, go1, m_od_w_out, v_od_w_out, slots=True, rows=256)
    go0 = summands(h_go0,out["od_w_out"][1], "grads_w_out0_wait")
    upd("ev_w_out", ev_w_out, go0, m_ev_w_out, v_ev_w_out, slots=True, rows=256)
    gi0 = summands(h_gi0,out["ev_w_out"][1], "grads_w_in0_wait")
    upd("ev_w_in", ev_w_in, gi0, m_ev_w_in, v_ev_w_in, slots=True, rows=256)

    def my_shard(full, size):
        return lax.dynamic_slice_in_dim(full, me * size, size, axis=full.ndim - 1)

    small_items = [
        ("c_ctx", c_ctx, g_c_ctx, m_c_ctx, v_c_ctx),
        ("ada_b", ada_b, g_ada_b, m_ada_b, v_ada_b),
        ("norm_g", norm_g, gsum[8:10], m_norm_g, v_norm_g),
        ("ev_q_norm", ev_q_norm, gsum[14, conv_ch:conv_ch + HEAD_DIM], m_ev_q_norm, v_ev_q_norm),
        ("ev_k_norm", ev_k_norm, gsum[14, conv_ch + HEAD_DIM:conv_ch + 2 * HEAD_DIM], m_ev_k_norm, v_ev_k_norm),
        ("ev_dw_w", ev_dw_w, my_shard(g_dw_w, conv_ch // N_DEV), m_ev_dw_w, v_ev_dw_w),
        ("ev_dw_b", ev_dw_b, gsum[13, :conv_ch], m_ev_dw_b, v_ev_dw_b),
        ("ev_ln_g", ev_ln_g, gsum[13, conv_ch:2 * conv_ch], m_ev_ln_g, v_ev_ln_g),
        ("ev_ln_b", ev_ln_b, gsum[14, :conv_ch], m_ev_ln_b, v_ev_ln_b),
        ("od_ln_g", od_ln_g, my_shard(gsum[11], shard), m_od_ln_g, v_od_ln_g),
        ("od_ln_b", od_ln_b, my_shard(gsum[12], shard), m_od_ln_b, v_od_ln_b),
        ("od_ws", od_ws, g_od_ws, m_od_ws, v_od_ws),
        ("od_bs", od_bs, gsum[15, :SGU_GROUPS * CHUNK], m_od_bs, v_od_bs),
        ("final_g", final_g, gsum[10], m_final_g, v_final_g),
    ]
    sizes = [it[1].size for it in small_items]
    total = sum(sizes)
    lanes = 1024
    prow = -(-total // lanes)
    prow = -(-prow // 8) * 8

    def pack_small(idx):
        flat = jnp.concatenate([it[idx].reshape(-1).astype(F32) for it in small_items])
        return jnp.pad(flat, (0, prow * lanes - total)).reshape(prow, lanes)

    sres = adamw(pack_small(1), pack_small(2), pack_small(3), pack_small(4), name="adamw_small", rows=prow)
    off = 0
    for it, size in zip(small_items, sizes):
        out[it[0]] = tuple(r.reshape(-1)[off:off + size].reshape(it[1].shape) for r in sres)
        off += size

    names = ['c_ctx', 'ada_w', 'ada_b', 'norm_g', 'ev_w_in', 'ev_q_norm', 'ev_k_norm', 'ev_dw_w', 'ev_dw_b',
             'ev_ln_g', 'ev_ln_b', 'ev_w_out', 'od_w_in', 'od_ln_g', 'od_ln_b', 'od_ws', 'od_bs', 'od_w_out',
             'final_g']
    return (loss, grad_x[None], *[out[k][0] for k in names], *[out[k][1] for k in names],
            *[out[k][2] for k in names], *[out[k][3] for k in names])
```

```python
import functools
import math

import jax
import jax.numpy as jnp
from jax import lax
from jax.experimental import pallas as pl
from jax.experimental.pallas import tpu as pltpu

F32 = jnp.float32
BF16 = jnp.bfloat16
MESH = pl.DeviceIdType.MESH

EPS = 1e-6
HEAD_DIM = 128
N_Q_HEADS = 8
N_KV_HEADS = 2
Q_PER_KV = N_Q_HEADS // N_KV_HEADS
ATTN_W = N_Q_HEADS * HEAD_DIM
KV_W = N_KV_HEADS * HEAD_DIM
ATTN_SCALE = HEAD_DIM ** -0.5
LN2 = math.log(2.0)
Q_SCALE = ATTN_SCALE / LN2
ROPE_THETA = 10000.0
GRID_W = 64
CONV_WIDTH = 31
CONV_HALF = CONV_WIDTH // 2
HALO = 16
CHUNK = 128
SGU_GROUPS = 8
N_DEV = 8

ADAM_LR = 0.001
ADAM_B1 = 0.9
ADAM_B2 = 0.999
ADAM_EPS = 1e-08
ADAM_WD = 0.01
ADAM_STEP = 10

VMEM_LIMIT = 56 * 1024 * 1024
ANY = pl.BlockSpec(memory_space=pl.ANY)
VMEM_SPEC = pl.BlockSpec(memory_space=pltpu.VMEM)


def _params(*sem):
    return pltpu.CompilerParams(dimension_semantics=sem, vmem_limit_bytes=VMEM_LIMIT)


def _sigmoid(x):
    return 1.0 / (1.0 + jnp.exp(-x))


def _silu(x):
    return x * _sigmoid(x)


def _dsilu(x):
    s = _sigmoid(x)
    return s * (1.0 + x * (1.0 - s))


_GELU_C = math.sqrt(2.0 / math.pi)


def _gelu(x):
    t = jnp.tanh(_GELU_C * (x + 0.044715 * (x * x * x)))
    return 0.5 * x * (1.0 + t)


def _dgelu(x):
    t = jnp.tanh(_GELU_C * (x + 0.044715 * (x * x * x)))
    return 0.5 * (1.0 + t) + 0.5 * x * (1.0 - t * t) * (_GELU_C * (1.0 + 3.0 * 0.044715 * (x * x)))


def _row(v):
    return v.reshape(1, -1).astype(F32)


def _flat_id(p):
    return 4 * p[0] + 2 * p[1] + p[2]


def _gather_body(n_arr, sum_out):
    def body(*refs):
        x_refs = refs[:n_arr]
        out_refs = refs[n_arr:2 * n_arr]
        pos = 2 * n_arr
        sum_refs = refs[pos:pos + n_arr] if sum_out else ()
        pos += n_arr if sum_out else 0
        send_sems, recv_sems, local_sems = refs[pos:pos + 3]
        x, y, c = lax.axis_index("x"), lax.axis_index("y"), lax.axis_index("c")
        me, sibling = (x, y, c), (x, y, 1 - c)
        chips = [(1 - x, y), (x, 1 - y), (1 - x, 1 - y)]

        def copy(a, k, block, to, src=None):
            rows = out_refs[a].at[_flat_id(block)]
            return pltpu.make_async_remote_copy(
                src_ref=rows if src is None else src, dst_ref=rows,
                send_sem=send_sems.at[a, k], recv_sem=recv_sems.at[a, k],
                device_id=to, device_id_type=MESH)

        sends = []
        mine = []
        for a in range(n_arr):
            cp = pltpu.make_async_copy(x_refs[a], out_refs[a].at[_flat_id(me)], local_sems.at[a])
            cp.start()
            mine.append(cp)
            first = [copy(a, 0, me, sibling, src=x_refs[a])]
            first += [copy(a, 1 + j, me, (*chip, c), src=x_refs[a]) for j, chip in enumerate(chips)]
            for cp in first:
                cp.start()
            sends += first
        for a in range(n_arr):
            for j, chip in enumerate(chips):
                copy(a, 1 + j, (*chip, c), me).wait_recv()
                fwd = copy(a, 4 + j, (*chip, c), sibling)
                fwd.start()
                sends.append(fwd)
        for a in range(n_arr):
            copy(a, 0, sibling, me).wait_recv()
            for j, chip in enumerate(chips):
                copy(a, 4 + j, (*chip, 1 - c), me).wait_recv()
        for cp in sends:
            cp.wait_send()
        for cp in mine:
            cp.wait()
        if sum_out:
            for a in range(n_arr):
                acc = out_refs[a][0]
                for k in range(1, N_DEV):
                    acc = acc + out_refs[a][k]
                sum_refs[a][...] = acc

    return body


def all_gather(arrs, *, name, hbm, sum_out=False):
    n = len(arrs)
    spec = ANY if hbm else VMEM_SPEC
    out_shape = [jax.ShapeDtypeStruct((N_DEV,) + a.shape, a.dtype) for a in arrs]
    out_specs = [spec] * n
    if sum_out:
        out_shape += [jax.ShapeDtypeStruct(a.shape, a.dtype) for a in arrs]
        out_specs += [VMEM_SPEC] * n
    res = pl.pallas_call(
        _gather_body(n, sum_out), name=name,
        out_shape=out_shape, in_specs=[spec] * n, out_specs=out_specs,
        scratch_shapes=[pltpu.SemaphoreType.DMA((n, 7)), pltpu.SemaphoreType.DMA((n, 7)),
                        pltpu.SemaphoreType.DMA((n,))],
        compiler_params=pltpu.CompilerParams(vmem_limit_bytes=VMEM_LIMIT),
    )(*arrs)
    return res


_RELATIONS = [(rx, ry, rc) for rx in (0, 1) for ry in (0, 1) for rc in (0, 1)][1:]


HBM_SPEC = pl.BlockSpec(memory_space=pltpu.HBM)
SEM_SPEC = pl.BlockSpec(memory_space=pltpu.SEMAPHORE)
_DATAFLOW = pltpu.SideEffectType.DATAFLOW_SIDE_EFFECTING
N_PEERS = N_DEV - 1


def _peer_copy(src_ref, land_ref, send_sem, recv_sem, k, rel, scatter, sending):
    x, y, c = lax.axis_index("x"), lax.axis_index("y"), lax.axis_index("c")
    rx, ry, rc = rel
    peer = (1 - x if rx else x, 1 - y if ry else y, 1 - c if rc else c)
    src = src_ref.at[_flat_id(peer)] if scatter else src_ref
    dst = land_ref.at[_flat_id((x, y, c)) if sending else _flat_id(peer)]
    return pltpu.make_async_remote_copy(src_ref=src, dst_ref=dst, send_sem=send_sem.at[k], recv_sem=recv_sem.at[k],
                                        device_id=peer, device_id_type=MESH)


def exchange_start(arrs, *, scatter, name):
    n = len(arrs)
    lands = [lax.empty((N_DEV,) + (a.shape[1:] if scatter else a.shape), a.dtype) for a in arrs]

    def body(*refs):
        srcs, lnds, sems = refs[:n], refs[n:2 * n], refs[2 * n:4 * n]
        token = refs[6 * n]
        for a in range(n):
            for k, rel in enumerate(_RELATIONS):
                _peer_copy(srcs[a], lnds[a], sems[2 * a], sems[2 * a + 1], k, rel, scatter, True).start()
        token[...] = jnp.zeros_like(token)

    outs = pl.pallas_call(
        body, name=name,
        out_shape=[pltpu.SemaphoreType.DMA((N_PEERS,))] * (2 * n)
        + [pltpu.HBM(a.shape, a.dtype) for a in arrs] + [pltpu.HBM(l.shape, l.dtype) for l in lands]
        + [jax.ShapeDtypeStruct((8, 128), F32)],
        in_specs=[HBM_SPEC] * (2 * n),
        out_specs=[SEM_SPEC] * (2 * n) + [HBM_SPEC] * (2 * n) + [VMEM_SPEC],
        input_output_aliases={i: 2 * n + i for i in range(2 * n)},
        compiler_params=pltpu.CompilerParams(has_side_effects=_DATAFLOW),
    )(*[pltpu.with_memory_space_constraint(a, pltpu.HBM) for a in arrs],
      *[pltpu.with_memory_space_constraint(l, pltpu.HBM) for l in lands])
    handles = [(outs[2 * a], outs[2 * a + 1], outs[2 * n + a], outs[3 * n + a]) for a in range(n)]
    return handles, outs[4 * n]


def exchange_wait(handle, after, *, scatter, name):
    send_sem, recv_sem, src, land = handle

    def body(src_ref, land_ref, send_ref, recv_ref, after_ref, src_out, land_out):
        for k, rel in enumerate(_RELATIONS):
            cp = _peer_copy(src_ref, land_ref, send_ref, recv_ref, k, rel, scatter, False)
            cp.wait_send()
            cp.wait_recv()

    outs = pl.pallas_call(
        body, name=name,
        out_shape=[pltpu.HBM(src.shape, src.dtype), pltpu.HBM(land.shape, land.dtype)],
        in_specs=[HBM_SPEC, HBM_SPEC, SEM_SPEC, SEM_SPEC, ANY],
        out_specs=[HBM_SPEC, HBM_SPEC], input_output_aliases={0: 0, 1: 1},
        compiler_params=pltpu.CompilerParams(has_side_effects=_DATAFLOW),
    )(src, land, send_sem, recv_sem, after)
    return outs[0], outs[1]


_DIMS = {"nn": (((1,), (0,)), ((), ())), "nt": (((1,), (1,)), ((), ())), "tn": (((0,), (0,)), ((), ()))}


def matmul(a, b, *, mode, tm, tn, tk, out_dtype, name, n_out=None, res=None, gate=None, add=None, after=None):
    if mode == "tn":
        kdim, m = a.shape
    else:
        m, kdim = a.shape
    nfull = b.shape[0] if mode == "nt" else b.shape[1]
    n = nfull if n_out is None else n_out
    tm, tn, tk = min(tm, m), min(tn, n), min(tk, kdim)
    assert m % tm == 0 and n % tn == 0 and kdim % tk == 0, (name, m, n, kdim, tm, tn, tk)
    nk = kdim // tk
    dims = _DIMS[mode]
    a_spec = pl.BlockSpec((tk, tm), lambda j, i, k: (k, i)) if mode == "tn" else pl.BlockSpec((tm, tk), lambda j, i, k: (i, k))
    b_spec = pl.BlockSpec((tn, tk), lambda j, i, k: (j, k)) if mode == "nt" else pl.BlockSpec((tk, tn), lambda j, i, k: (k, j))
    o_spec = pl.BlockSpec((tm, tn), lambda j, i, k: (i, j))
    in_specs = [a_spec, b_spec]
    operands = [a, b]
    aliases = {}
    if res is not None:
        in_specs += [o_spec, pl.BlockSpec((1, tn), lambda j, i, k: (0, j))]
        operands += [res, gate]
    if add is not None:
        in_specs += [o_spec]
        aliases = {len(operands): 0}
        operands += [add]
    if after is not None:
        in_specs += [ANY]
        operands += [after]
    out_cols = n if add is None else add.shape[1]
    out_shape = [jax.ShapeDtypeStruct((m, out_cols), out_dtype)]
    out_specs = [o_spec]
    if res is not None:
        out_shape.append(jax.ShapeDtypeStruct((m, n), BF16))
        out_specs.append(o_spec)

    def body(*refs):
        a_ref, b_ref = refs[:2]
        pos = 2
        if res is not None:
            res_ref, gate_ref = refs[pos:pos + 2]
            pos += 2
        if add is not None:
            add_ref = refs[pos]
            pos += 1
        if after is not None:
            pos += 1
        o_ref = refs[pos]
        pos += 1
        if res is not None:
            r_ref = refs[pos]
            pos += 1
        acc_ref = refs[pos] if nk > 1 else None
        prod = lax.dot_general(a_ref[...].astype(BF16), b_ref[...].astype(BF16), dims,
                               preferred_element_type=F32)

        def finish(acc):
            if res is not None:
                o_ref[...] = (res_ref[...] + gate_ref[...] * acc).astype(out_dtype)
                r_ref[...] = acc.astype(BF16)
            elif add is not None:
                o_ref[...] = (add_ref[...] + acc).astype(out_dtype)
            else:
                o_ref[...] = acc.astype(out_dtype)

        if nk == 1:
            finish(prod)
        else:
            k = pl.program_id(2)

            @pl.when(k == 0)
            def _():
                acc_ref[...] = prod

            @pl.when(k > 0)
            def _():
                acc_ref[...] += prod

            @pl.when(k == nk - 1)
            def _():
                finish(acc_ref[...])

    outs = pl.pallas_call(
        body, name=name, grid=(n // tn, m // tm, nk),
        in_specs=in_specs, out_specs=out_specs, out_shape=out_shape,
        scratch_shapes=[pltpu.VMEM((tm, tn), F32)] if nk > 1 else [],
        input_output_aliases=aliases,
        compiler_params=_params("parallel", "parallel", "arbitrary"),
    )(*operands)
    return outs if res is not None else outs[0]


def _rows_tile(n, want):
    return min(n, want)


def norm_mod_fwd(x, g, shift, scale, *, name):
    n, d = x.shape
    tt = _rows_tile(n, 256)

    def body(x_ref, g_ref, sh_ref, sc_ref, h_ref):
        xv = x_ref[...]
        rstd = lax.rsqrt(jnp.mean(xv * xv, axis=-1, keepdims=True) + EPS)
        y = xv * rstd * g_ref[...]
        h_ref[...] = (y * (1.0 + sc_ref[...]) + sh_ref[...]).astype(BF16)

    vec = pl.BlockSpec((1, d), lambda i: (0, 0))
    return pl.pallas_call(
        body, name=name, grid=(n // tt,),
        in_specs=[pl.BlockSpec((tt, d), lambda i: (i, 0)), vec, vec, vec],
        out_specs=pl.BlockSpec((tt, d), lambda i: (i, 0)),
        out_shape=jax.ShapeDtypeStruct((n, d), BF16),
        compiler_params=_params("parallel"),
    )(x, _row(g), _row(shift), _row(scale))


def norm_mod_bwd(x, dh, dres, g, scale, *, name):
    n, d = x.shape
    tt = _rows_tile(n, 256)
    has_res = dres is not None
    last = n // tt - 1

    def body(*refs):
        x_ref, dh_ref = refs[:2]
        pos = 2
        if has_res:
            dres_ref = refs[pos]
            pos += 1
        g_ref, sc_ref, dx_ref, acc_ref, s_ref = refs[pos:pos + 5]
        i = pl.program_id(0)
        xv = x_ref[...]
        dhv = dh_ref[...]
        rstd = lax.rsqrt(jnp.mean(xv * xv, axis=-1, keepdims=True) + EPS)
        xhat = xv * rstd
        dxhat = dhv * (g_ref[...] * (1.0 + sc_ref[...]))
        dx = rstd * (dxhat - xhat * jnp.mean(dxhat * xhat, axis=-1, keepdims=True))
        if has_res:
            dx = dx + dres_ref[...]
        dx_ref[...] = dx

        @pl.when(i == 0)
        def _():
            s_ref[...] = jnp.zeros_like(s_ref)

        s_ref[0:1, :] += jnp.sum(dhv, axis=0, keepdims=True)
        s_ref[1:2, :] += jnp.sum(dhv * xhat, axis=0, keepdims=True)

        @pl.when(i == last)
        def _():
            s1 = s_ref[0:1, :]
            s2 = s_ref[1:2, :]
            acc_ref[...] = jnp.zeros_like(acc_ref)
            acc_ref[0:1, :] = s1
            acc_ref[1:2, :] = s2 * g_ref[...]
            acc_ref[2:3, :] = s2 * (1.0 + sc_ref[...])

    vec = pl.BlockSpec((1, d), lambda i: (0, 0))
    big = pl.BlockSpec((tt, d), lambda i: (i, 0))
    ops = [x, dh] + ([dres] if has_res else []) + [_row(g), _row(scale)]
    return pl.pallas_call(
        body, name=name, grid=(n // tt,),
        in_specs=[big, big] + ([big] if has_res else []) + [vec, vec],
        out_specs=[big, pl.BlockSpec((8, d), lambda i: (0, 0))],
        out_shape=[jax.ShapeDtypeStruct((n, d), F32), jax.ShapeDtypeStruct((8, d), F32)],
        scratch_shapes=[pltpu.VMEM((8, d), F32)],
        compiler_params=_params("arbitrary"),
    )(*ops)


def gate_bwd(dx, r, gate, *, name):
    n, d = dx.shape
    tt = _rows_tile(n, 256)

    def body(dx_ref, r_ref, gate_ref, dr_ref, acc_ref):
        i = pl.program_id(0)
        dxv = dx_ref[...]
        dr_ref[...] = (dxv * gate_ref[...]).astype(BF16)

        @pl.when(i == 0)
        def _():
            acc_ref[...] = jnp.zeros_like(acc_ref)

        acc_ref[0:1, :] += jnp.sum(dxv * r_ref[...].astype(F32), axis=0, keepdims=True)

    big = pl.BlockSpec((tt, d), lambda i: (i, 0))
    return pl.pallas_call(
        body, name=name, grid=(n // tt,),
        in_specs=[big, big, pl.BlockSpec((1, d), lambda i: (0, 0))],
        out_specs=[big, pl.BlockSpec((8, d), lambda i: (0, 0))],
        out_shape=[jax.ShapeDtypeStruct((n, d), BF16), jax.ShapeDtypeStruct((8, d), F32)],
        compiler_params=_params("arbitrary"),
    )(dx, r, _row(gate))


def final_loss(x, target, g, *, name):
    n, d = x.shape
    tt = _rows_tile(n, 256)

    def body(x_ref, t_ref, g_ref, dx_ref, acc_ref, loss_ref):
        i = pl.program_id(0)
        xv = x_ref[...]
        rstd = lax.rsqrt(jnp.mean(xv * xv, axis=-1, keepdims=True) + EPS)
        xhat = xv * rstd
        e = xhat * g_ref[...] - t_ref[...]
        dy = e * (1.0 / d)
        dxhat = dy * g_ref[...]
        dx_ref[...] = rstd * (dxhat - xhat * jnp.mean(dxhat * xhat, axis=-1, keepdims=True))

        @pl.when(i == 0)
        def _():
            acc_ref[...] = jnp.zeros_like(acc_ref)
            loss_ref[...] = jnp.zeros_like(loss_ref)

        acc_ref[0:1, :] += jnp.sum(dy * xhat, axis=0, keepdims=True)
        part = 0.5 * jnp.sum(jnp.mean(e * e, axis=-1, keepdims=True), axis=0, keepdims=True)
        loss_ref[...] += jnp.broadcast_to(part, loss_ref.shape)

    big = pl.BlockSpec((tt, d), lambda i: (i, 0))
    return pl.pallas_call(
        body, name=name, grid=(n // tt,),
        in_specs=[big, big, pl.BlockSpec((1, d), lambda i: (0, 0))],
        out_specs=[big, pl.BlockSpec((8, d), lambda i: (0, 0)), pl.BlockSpec((8, 128), lambda i: (0, 0))],
        out_shape=[jax.ShapeDtypeStruct((n, d), F32), jax.ShapeDtypeStruct((8, d), F32),
                   jax.ShapeDtypeStruct((8, 128), F32)],
        compiler_params=_params("arbitrary"),
    )(x, target, _row(g))


def _swap_pairs(x):
    lane = lax.broadcasted_iota(jnp.int32, x.shape, 1)
    return jnp.where(lane % 2 == 0, pltpu.roll(x, HEAD_DIM - 1, 1), pltpu.roll(x, 1, 1))


def rope_tables(n):
    rows = n // GRID_W
    row = jnp.repeat(jnp.arange(rows, dtype=F32), GRID_W)
    col = jnp.tile(jnp.arange(GRID_W, dtype=F32), rows)
    n_freq = HEAD_DIM // 4
    inv = jnp.power(ROPE_THETA, jnp.arange(n_freq, dtype=F32) * (-2.0 / (HEAD_DIM // 2)))
    ang = jnp.concatenate([row[:, None] * inv, col[:, None] * inv], axis=-1)
    cos, sin = jnp.cos(ang), jnp.sin(ang)
    cexp = jnp.repeat(cos, 2, axis=-1)
    sexp = jnp.stack([-sin, sin], axis=-1).reshape(n, HEAD_DIM)
    return cexp, sexp


def qkv_prep_fwd(p, wq, wk, cexp, sexp, *, latent, name):
    n = p.shape[0]
    tt = _rows_tile(n, 256)
    width = 2 * KV_W + (ATTN_W if latent else 0)

    def body(*refs):
        if latent:
            p_ref, wq_ref, wk_ref, c_ref, s_ref, q_ref, k_ref, v_ref = refs
        else:
            p_ref, wk_ref, k_ref, v_ref = refs

        def head(xv, w):
            rstd = lax.rsqrt(jnp.mean(xv * xv, axis=-1, keepdims=True) + EPS)
            yv = xv * rstd * w
            if latent:
                yv = yv * c_ref[...] + _swap_pairs(yv) * s_ref[...]
            return yv

        for h in range(N_KV_HEADS):
            sl = slice(h * HEAD_DIM, (h + 1) * HEAD_DIM)
            k_ref[:, sl] = head(p_ref[:, sl], wk_ref[...]).astype(BF16)
        v_ref[...] = p_ref[:, KV_W:2 * KV_W].astype(BF16)
        if latent:
            for h in range(N_Q_HEADS):
                sl = slice(2 * KV_W + h * HEAD_DIM, 2 * KV_W + (h + 1) * HEAD_DIM)
                q_ref[:, h * HEAD_DIM:(h + 1) * HEAD_DIM] = (head(p_ref[:, sl], wq_ref[...]) * Q_SCALE).astype(BF16)

    vec = pl.BlockSpec((1, HEAD_DIM), lambda i: (0, 0))
    tab = pl.BlockSpec((tt, HEAD_DIM), lambda i: (i, 0))
    kv_spec = pl.BlockSpec((tt, KV_W), lambda i: (i, 0))
    kv_shape = jax.ShapeDtypeStruct((n, KV_W), BF16)
    if latent:
        in_specs = [pl.BlockSpec((tt, width), lambda i: (i, 0)), vec, vec, tab, tab]
        ops = [p, _row(wq), _row(wk), cexp, sexp]
        out_specs = [pl.BlockSpec((tt, ATTN_W), lambda i: (i, 0)), kv_spec, kv_spec]
        out_shape = [jax.ShapeDtypeStruct((n, ATTN_W), BF16), kv_shape, kv_shape]
    else:
        in_specs = [pl.BlockSpec((tt, width), lambda i: (i, 0)), vec]
        ops = [p, _row(wk)]
        out_specs = [kv_spec, kv_spec]
        out_shape = [kv_shape, kv_shape]
    return pl.pallas_call(
        body, name=name, grid=(n // tt,), in_specs=in_specs, out_specs=out_specs, out_shape=out_shape,
        compiler_params=_params("parallel"),
    )(*ops)


def qkv_prep_bwd(p, dq, dk, dv, wq, wk, cexp, sexp, *, latent, name):
    n = p.shape[0]
    tt = _rows_tile(n, 256)
    width = 2 * KV_W + (ATTN_W if latent else 0)

    def body(*refs):
        if latent:
            p_ref, dq_ref, dk_ref, dv_ref, wq_ref, wk_ref, c_ref, s_ref, dp_ref, acc_ref = refs
        else:
            p_ref, dk_ref, dv_ref, wk_ref, dp_ref, acc_ref = refs
        i = pl.program_id(0)

        @pl.when(i == 0)
        def _():
            acc_ref[...] = jnp.zeros_like(acc_ref)

        def head(xv, dy, w, row):
            if latent:
                dy = dy * c_ref[...] + _swap_pairs(dy * s_ref[...])
            rstd = lax.rsqrt(jnp.mean(xv * xv, axis=-1, keepdims=True) + EPS)
            xhat = xv * rstd
            acc_ref[row:row + 1, :] += jnp.sum(dy * xhat, axis=0, keepdims=True)
            dxhat = dy * w
            return rstd * (dxhat - xhat * jnp.mean(dxhat * xhat, axis=-1, keepdims=True))

        for h in range(N_KV_HEADS):
            sl = slice(h * HEAD_DIM, (h + 1) * HEAD_DIM)
            dp_ref[:, sl] = head(p_ref[:, sl], dk_ref[:, sl], wk_ref[...], 1).astype(BF16)
        dp_ref[:, KV_W:2 * KV_W] = dv_ref[...].astype(BF16)
        if latent:
            for h in range(N_Q_HEADS):
                sl = slice(2 * KV_W + h * HEAD_DIM, 2 * KV_W + (h + 1) * HEAD_DIM)
                dyq = dq_ref[:, h * HEAD_DIM:(h + 1) * HEAD_DIM] * Q_SCALE
                dp_ref[:, sl] = head(p_ref[:, sl], dyq, wq_ref[...], 0).astype(BF16)

    vec = pl.BlockSpec((1, HEAD_DIM), lambda i: (0, 0))
    tab = pl.BlockSpec((tt, HEAD_DIM), lambda i: (i, 0))
    kv_spec = pl.BlockSpec((tt, KV_W), lambda i: (i, 0))
    p_spec = pl.BlockSpec((tt, width), lambda i: (i, 0))
    if latent:
        in_specs = [p_spec, pl.BlockSpec((tt, ATTN_W), lambda i: (i, 0)), kv_spec, kv_spec, vec, vec, tab, tab]
        ops = [p, dq, dk, dv, _row(wq), _row(wk), cexp, sexp]
    else:
        in_specs = [p_spec, kv_spec, kv_spec, vec]
        ops = [p, dk, dv, _row(wk)]
    return pl.pallas_call(
        body, name=name, grid=(n // tt,), in_specs=in_specs,
        out_specs=[p_spec, pl.BlockSpec((8, HEAD_DIM), lambda i: (0, 0))],
        out_shape=[jax.ShapeDtypeStruct((n, width), BF16), jax.ShapeDtypeStruct((8, HEAD_DIM), F32)],
        compiler_params=_params("arbitrary"),
    )(*ops)


def _kv_chunks(n, s_all):
    step = 1024 if n % 1024 == 0 else 256
    chunks = [(s, step) for s in range(0, n, step)]
    if s_all > n:
        chunks.append((n, s_all - n))
    return chunks


def flash_fwd(q, k_all, v_ext, p, *, za_block, name):
    n = q.shape[0]
    s_all = k_all.shape[0]
    tq = _rows_tile(n, 512)
    chunks = _kv_chunks(n, s_all)
    wide = 2 * HEAD_DIM

    def body(q_ref, k_ref, v_ref, za_ref, o_ref, mix_ref, lse_ref, m_ref, acc_ref):
        qv = q_ref[...]
        m_ref[...] = jnp.full_like(m_ref, -jnp.inf)
        acc_ref[...] = jnp.zeros_like(acc_ref)
        for start, size in chunks:
            kc = k_ref[pl.ds(start, size), :]
            vc = v_ref[pl.ds(start, size), :]
            s = lax.dot_general(qv, kc, _DIMS["nt"], preferred_element_type=F32)
            m_old = m_ref[...]
            m_new = jnp.maximum(m_old, jnp.max(s, axis=-1, keepdims=True))
            pr = jnp.exp2(s - m_new)
            alpha = jnp.exp2(m_old - m_new)
            acc_ref[...] = alpha * acc_ref[...] + jnp.dot(pr.astype(BF16), vc, preferred_element_type=F32)
            m_ref[...] = m_new
        acc = acc_ref[...]
        denom = acc[:, HEAD_DIM:HEAD_DIM + 1]
        o = acc[:, :HEAD_DIM] / denom
        o_ref[...] = o.astype(BF16)
        mix_ref[...] = (o * _silu(za_ref[...])).astype(BF16)
        lse_ref[0] = m_ref[...] + jnp.log2(denom)

    qspec = pl.BlockSpec((tq, HEAD_DIM), lambda h, i: (i, h))
    return pl.pallas_call(
        body, name=name, grid=(N_Q_HEADS, n // tq),
        in_specs=[qspec, pl.BlockSpec((s_all, HEAD_DIM), lambda h, i: (0, h // Q_PER_KV)),
                  pl.BlockSpec((s_all, wide), lambda h, i: (0, h // Q_PER_KV)),
                  pl.BlockSpec((tq, HEAD_DIM), lambda h, i: (i, za_block + h))],
        out_specs=[qspec, qspec, pl.BlockSpec((1, tq, 1), lambda h, i: (h, i, 0))],
        out_shape=[jax.ShapeDtypeStruct((n, ATTN_W), BF16), jax.ShapeDtypeStruct((n, ATTN_W), BF16),
                   jax.ShapeDtypeStruct((N_Q_HEADS, n, 1), F32)],
        scratch_shapes=[pltpu.VMEM((tq, 1), F32), pltpu.VMEM((tq, wide), F32)],
        compiler_params=_params("parallel", "parallel"),
    )(q, k_all, v_ext, p)


def attn_gate_bwd(dmix, o, p, *, za_block, name):
    n = o.shape[0]
    tt = _rows_tile(n, 512)

    def body(dm_ref, o_ref, za_ref, do_ref, dos_ref, dza_ref, delta_ref):
        dm = dm_ref[...]
        ov = o_ref[...].astype(F32)
        za = za_ref[...]
        do = dm * _silu(za)
        do_ref[...] = do.astype(BF16)
        dos_ref[...] = (do * LN2).astype(BF16)
        dza_ref[...] = (dm * ov * _dsilu(za)).astype(BF16)
        delta_ref[0] = jnp.sum(do * ov, axis=-1, keepdims=True) * LN2

    spec = pl.BlockSpec((tt, HEAD_DIM), lambda i, h: (i, h))
    shape = jax.ShapeDtypeStruct((n, ATTN_W), BF16)
    return pl.pallas_call(
        body, name=name, grid=(n // tt, N_Q_HEADS),
        in_specs=[spec, spec, pl.BlockSpec((tt, HEAD_DIM), lambda i, h: (i, za_block + h))],
        out_specs=[spec, spec, spec, pl.BlockSpec((1, tt, 1), lambda i, h: (h, i, 0))],
        out_shape=[shape, shape, shape, jax.ShapeDtypeStruct((N_Q_HEADS, n, 1), F32)],
        compiler_params=_params("parallel", "parallel"),
    )(dmix, o, p)


def flash_bwd(q, do, do_s, lse_row, delta_row, k_all, v_all, *, name):
    n = q.shape[0]
    s_all = k_all.shape[0]
    tq = _rows_tile(n, 512)
    chunks = _kv_chunks(n, s_all)

    def body(q_ref, do_ref, dos_ref, lse_ref, dl_ref, k_ref, v_ref, dq_ref, dk_ref, dv_ref):
        g = pl.program_id(1)
        i = pl.program_id(2)

        @pl.when((g == 0) & (i == 0))
        def _():
            dk_ref[...] = jnp.zeros_like(dk_ref)
            dv_ref[...] = jnp.zeros_like(dv_ref)

        qv = q_ref[...]
        dov = do_ref[...]
        dosv = dos_ref[...]
        lse = lse_ref[0]
        dl = dl_ref[0]
        dq = jnp.zeros((tq, HEAD_DIM), F32)
        for start, size in chunks:
            kc = k_ref[pl.ds(start, size), :]
            vc = v_ref[pl.ds(start, size), :]
            st = lax.dot_general(kc, qv, _DIMS["nt"], preferred_element_type=F32)
            pt = jnp.exp2(st - lse)
            dpt = lax.dot_general(vc, dosv, _DIMS["nt"], preferred_element_type=F32)
            dst = (pt * (dpt - dl)).astype(BF16)
            dv_ref[pl.ds(start, size), :] += jnp.dot(pt.astype(BF16), dov, preferred_element_type=F32)
            dk_ref[pl.ds(start, size), :] += jnp.dot(dst, qv, preferred_element_type=F32)
            dq = dq + lax.dot_general(dst, kc, _DIMS["tn"], preferred_element_type=F32)
        dq_ref[...] = dq

    qspec = pl.BlockSpec((tq, HEAD_DIM), lambda kh, g, i: (i, kh * Q_PER_KV + g))
    rowspec = pl.BlockSpec((1, 1, tq), lambda kh, g, i: (kh * Q_PER_KV + g, 0, i))
    kvspec = pl.BlockSpec((s_all, HEAD_DIM), lambda kh, g, i: (0, kh))
    return pl.pallas_call(
        body, name=name, grid=(N_KV_HEADS, Q_PER_KV, n // tq),
        in_specs=[qspec, qspec, qspec, rowspec, rowspec, kvspec, kvspec],
        out_specs=[qspec, kvspec, kvspec],
        out_shape=[jax.ShapeDtypeStruct((n, ATTN_W), F32),
                   jax.ShapeDtypeStruct((s_all, KV_W), F32), jax.ShapeDtypeStruct((s_all, KV_W), F32)],
        compiler_params=_params("arbitrary", "arbitrary", "arbitrary"),
    )(q, do, do_s, lse_row, delta_row, k_all, v_all)


HALF = 512


def _halo_specs(tt, n, cb):
    per = tt // HALO
    last = n // HALO - 1
    return [pl.BlockSpec((HALO, HALF), lambda i: (jnp.maximum(i * per - 1, 0), cb)),
            pl.BlockSpec((tt, HALF), lambda i: (i, cb)),
            pl.BlockSpec((HALO, HALF), lambda i: (jnp.minimum((i + 1) * per, last), cb))]


def conv_fwd(p, dw_w, dw_b, ln_g, ln_b, *, glu_block, name):
    n = p.shape[0]
    ch = dw_w.shape[1]
    nh = ch // HALF
    tt = _rows_tile(n, 256)
    last = n // tt - 1

    def body(*refs):
        a_refs = [refs[3 * h:3 * h + 3] for h in range(nh)]
        b_refs = [refs[3 * (nh + h):3 * (nh + h) + 3] for h in range(nh)]
        pos = 6 * nh
        zb_refs = refs[pos:pos + nh]
        pos += nh
        w_ref, bias_ref, g_ref, be_ref, ycv_ref, mix_ref, ext_ref = refs[pos:pos + 7]
        i = pl.program_id(0)
        for h in range(nh):
            cs = slice(h * HALF, (h + 1) * HALF)
            ap, am, an = a_refs[h]
            bp, bm, bn = b_refs[h]
            ext_ref[0:HALO, :] = jnp.where(i > 0, ap[...] * _sigmoid(bp[...]), 0.0)
            ext_ref[HALO:HALO + tt, :] = am[...] * _sigmoid(bm[...])
            ext_ref[HALO + tt:2 * HALO + tt, :] = jnp.where(i < last, an[...] * _sigmoid(bn[...]), 0.0)
            acc = jnp.broadcast_to(bias_ref[:, cs], (tt, HALF))
            for k in range(CONV_WIDTH):
                acc = acc + w_ref[k:k + 1, cs] * ext_ref[pl.ds(1 + k, tt), :]
            ycv_ref[:, cs] = acc
        yc = ycv_ref[...]
        mu = jnp.mean(yc, axis=-1, keepdims=True)
        var = jnp.mean(jnp.square(yc - mu), axis=-1, keepdims=True)
        ln = (yc - mu) * lax.rsqrt(var + EPS) * g_ref[...] + be_ref[...]
        out = _silu(ln)
        for h in range(nh):
            cs = slice(h * HALF, (h + 1) * HALF)
            mix_ref[:, cs] = (out[:, cs] * _silu(zb_refs[h][...])).astype(BF16)

    in_specs = []
    for h in range(2 * nh):
        in_specs += _halo_specs(tt, n, glu_block + h)
    in_specs += [pl.BlockSpec((tt, HALF), functools.partial(lambda i, cb: (i, cb), cb=glu_block + 2 * nh + h))
                 for h in range(nh)]
    vec = pl.BlockSpec((1, ch), lambda i: (0, 0))
    in_specs += [pl.BlockSpec((CONV_WIDTH, ch), lambda i: (0, 0)), vec, vec, vec]
    ops = [p] * (6 * nh + nh) + [dw_w, _row(dw_b), _row(ln_g), _row(ln_b)]
    big = pl.BlockSpec((tt, ch), lambda i: (i, 0))
    return pl.pallas_call(
        body, name=name, grid=(n // tt,), in_specs=in_specs, out_specs=[big, big],
        out_shape=[jax.ShapeDtypeStruct((n, ch), F32), jax.ShapeDtypeStruct((n, ch), BF16)],
        scratch_shapes=[pltpu.VMEM((tt + 2 * HALO, HALF), F32)],
        compiler_params=_params("parallel"),
    )(*ops)


def conv_bwd_rows(dmix, ycv, p, ln_g, ln_b, *, mix_block, zb_block, name):
    n, ch = ycv.shape
    nh = ch // HALF
    tt = _rows_tile(n, 256)

    def body(*refs):
        dm_ref, ycv_ref = refs[:2]
        zb_refs = refs[2:2 + nh]
        g_ref, be_ref, dy_ref, dzb_ref, acc_ref = refs[2 + nh:]
        i = pl.program_id(0)

        @pl.when(i == 0)
        def _():
            acc_ref[...] = jnp.zeros_like(acc_ref)

        yc = ycv_ref[...]
        mu = jnp.mean(yc, axis=-1, keepdims=True)
        var = jnp.mean(jnp.square(yc - mu), axis=-1, keepdims=True)
        rstd = lax.rsqrt(var + EPS)
        xhat = (yc - mu) * rstd
        ln = xhat * g_ref[...] + be_ref[...]
        out = _silu(ln)
        dm = dm_ref[...]
        zb = jnp.concatenate([r[...] for r in zb_refs], axis=-1)
        dzb_ref[...] = (dm * out * _dsilu(zb)).astype(BF16)
        dln = dm * _silu(zb) * _dsilu(ln)
        acc_ref[0:1, :] += jnp.sum(dln * xhat, axis=0, keepdims=True)
        acc_ref[1:2, :] += jnp.sum(dln, axis=0, keepdims=True)
        dxhat = dln * g_ref[...]
        dy_ref[...] = rstd * (dxhat - jnp.mean(dxhat, axis=-1, keepdims=True)
                              - xhat * jnp.mean(dxhat * xhat, axis=-1, keepdims=True))

    big = pl.BlockSpec((tt, ch), lambda i: (i, 0))
    vec = pl.BlockSpec((1, ch), lambda i: (0, 0))
    in_specs = [pl.BlockSpec((tt, ch), lambda i: (i, mix_block)), big]
    in_specs += [pl.BlockSpec((tt, HALF), functools.partial(lambda i, cb: (i, cb), cb=zb_block + h)) for h in range(nh)]
    in_specs += [vec, vec]
    return pl.pallas_call(
        body, name=name, grid=(n // tt,), in_specs=in_specs,
        out_specs=[big, big, pl.BlockSpec((8, ch), lambda i: (0, 0))],
        out_shape=[jax.ShapeDtypeStruct((n, ch), F32), jax.ShapeDtypeStruct((n, ch), BF16),
                   jax.ShapeDtypeStruct((8, ch), F32)],
        compiler_params=_params("arbitrary"),
    )(dmix, ycv, *([p] * nh), _row(ln_g), _row(ln_b))


def conv_bwd_taps(dycv, p, dw_w, *, glu_block, name):
    n, ch = dycv.shape
    nh = ch // HALF
    tt = _rows_tile(n, 256)
    last = n // tt - 1

    def body(*refs):
        d_refs = [refs[3 * h:3 * h + 3] for h in range(nh)]
        a_refs = [refs[3 * (nh + h):3 * (nh + h) + 3] for h in range(nh)]
        b_refs = [refs[3 * (2 * nh + h):3 * (2 * nh + h) + 3] for h in range(nh)]
        w_ref, dglu_ref, dw_ref, db_ref, yext_ref, dext_ref = refs[9 * nh:]
        i = pl.program_id(0)

        @pl.when(i == 0)
        def _():
            dw_ref[...] = jnp.zeros_like(dw_ref)
            db_ref[...] = jnp.zeros_like(db_ref)

        for h in range(nh):
            cs = slice(h * HALF, (h + 1) * HALF)
            ap, am, an = a_refs[h]
            bp, bm, bn = b_refs[h]
            dp, dm, dn = d_refs[h]
            av = am[...]
            sb = _sigmoid(bm[...])
            yext_ref[0:HALO, :] = jnp.where(i > 0, ap[...] * _sigmoid(bp[...]), 0.0)
            yext_ref[HALO:HALO + tt, :] = av * sb
            yext_ref[HALO + tt:2 * HALO + tt, :] = jnp.where(i < last, an[...] * _sigmoid(bn[...]), 0.0)
            dmain = dm[...]
            dext_ref[0:HALO, :] = jnp.where(i > 0, dp[...], 0.0)
            dext_ref[HALO:HALO + tt, :] = dmain
            dext_ref[HALO + tt:2 * HALO + tt, :] = jnp.where(i < last, dn[...], 0.0)
            dy = jnp.zeros((tt, HALF), F32)
            for k in range(CONV_WIDTH):
                dy = dy + w_ref[k:k + 1, cs] * dext_ref[pl.ds(CONV_WIDTH - k, tt), :]
                dw_ref[k:k + 1, cs] += jnp.sum(dmain * yext_ref[pl.ds(1 + k, tt), :], axis=0, keepdims=True)
            db_ref[0:1, cs] += jnp.sum(dmain, axis=0, keepdims=True)
            dglu_ref[:, cs] = (dy * sb).astype(BF16)
            dglu_ref[:, ch + h * HALF:ch + (h + 1) * HALF] = (dy * av * sb * (1.0 - sb)).astype(BF16)

    in_specs = []
    for h in range(nh):
        in_specs += _halo_specs(tt, n, h)
    for h in range(2 * nh):
        in_specs += _halo_specs(tt, n, glu_block + h)
    in_specs += [pl.BlockSpec((CONV_WIDTH, ch), lambda i: (0, 0))]
    ops = [dycv] * (3 * nh) + [p] * (6 * nh) + [dw_w]
    return pl.pallas_call(
        body, name=name, grid=(n // tt,), in_specs=in_specs,
        out_specs=[pl.BlockSpec((tt, 2 * ch), lambda i: (i, 0)), pl.BlockSpec((32, ch), lambda i: (0, 0)),
                   pl.BlockSpec((8, ch), lambda i: (0, 0))],
        out_shape=[jax.ShapeDtypeStruct((n, 2 * ch), BF16), jax.ShapeDtypeStruct((32, ch), F32),
                   jax.ShapeDtypeStruct((8, ch), F32)],
        scratch_shapes=[pltpu.VMEM((tt + 2 * HALO, HALF), F32), pltpu.VMEM((tt + 2 * HALO, HALF), F32)],
        compiler_params=_params("arbitrary"),
    )(*ops)


def _sgu_common(p_ref, g_ref, be_ref, ws_ref, bs_ref, w):
    gw = w // SGU_GROUPS
    u_pre = p_ref[:, 0:w]
    v_pre = p_ref[:, w:2 * w]
    zc = p_ref[:, 2 * w:3 * w]
    u = _gelu(u_pre)
    v = _gelu(v_pre)
    mu = jnp.mean(v, axis=-1, keepdims=True)
    var = jnp.mean(jnp.square(v - mu), axis=-1, keepdims=True)
    rstd = lax.rsqrt(var + EPS)
    vhat = (v - mu) * rstd
    vn = (vhat * g_ref[...] + be_ref[...]).astype(BF16)
    mixed = jnp.concatenate(
        [jnp.dot(ws_ref[gi].astype(BF16), vn[:, gi * gw:(gi + 1) * gw], preferred_element_type=F32)
         + bs_ref[:, gi:gi + 1] for gi in range(SGU_GROUPS)], axis=-1)
    return u_pre, v_pre, zc, u, rstd, vhat, vn, mixed


def sgu_fwd(p, ln_g, ln_b, ws, bs_t, *, name):
    n, w3 = p.shape
    w = w3 // 3

    def body(p_ref, g_ref, be_ref, ws_ref, bs_ref, m_ref):
        _, _, zc, u, _, _, _, mixed = _sgu_common(p_ref, g_ref, be_ref, ws_ref, bs_ref, w)
        m_ref[...] = (u * mixed * _silu(zc)).astype(BF16)

    vec = pl.BlockSpec((1, w), lambda i: (0, 0))
    return pl.pallas_call(
        body, name=name, grid=(n // CHUNK,),
        in_specs=[pl.BlockSpec((CHUNK, w3), lambda i: (i, 0)), vec, vec,
                  pl.BlockSpec((SGU_GROUPS, CHUNK, CHUNK), lambda i: (0, 0, 0)),
                  pl.BlockSpec((CHUNK, SGU_GROUPS), lambda i: (0, 0))],
        out_specs=pl.BlockSpec((CHUNK, w), lambda i: (i, 0)),
        out_shape=jax.ShapeDtypeStruct((n, w), BF16),
        compiler_params=_params("parallel"),
    )(p, _row(ln_g), _row(ln_b), ws, bs_t)


def sgu_bwd(p, dm, ln_g, ln_b, ws, ws_t, bs_t, *, name):
    n, w3 = p.shape
    w = w3 // 3
    gw = w // SGU_GROUPS

    def body(p_ref, dm_ref, g_ref, be_ref, ws_ref, wst_ref, bs_ref, dp_ref, dws_ref, dbs_ref, acc_ref):
        i = pl.program_id(0)

        @pl.when(i == 0)
        def _():
            dws_ref[...] = jnp.zeros_like(dws_ref)
            dbs_ref[...] = jnp.zeros_like(dbs_ref)
            acc_ref[...] = jnp.zeros_like(acc_ref)

        u_pre, v_pre, zc, u, rstd, vhat, vn, mixed = _sgu_common(p_ref, g_ref, be_ref, ws_ref, bs_ref, w)
        dmv = dm_ref[...]
        um = u * mixed
        dp_ref[:, 2 * w:3 * w] = (dmv * um * _dsilu(zc)).astype(BF16)
        dum = dmv * _silu(zc)
        dp_ref[:, 0:w] = (dum * mixed * _dgelu(u_pre)).astype(BF16)
        dmixed = dum * u
        dmixed_b = dmixed.astype(BF16)
        dvn_parts = []
        for gi in range(SGU_GROUPS):
            cs = slice(gi * gw, (gi + 1) * gw)
            dws_ref[gi] += lax.dot_general(dmixed_b[:, cs], vn[:, cs], _DIMS["nt"], preferred_element_type=F32)
            dbs_ref[:, gi:gi + 1] += jnp.sum(dmixed[:, cs], axis=-1, keepdims=True)
            dvn_parts.append(jnp.dot(wst_ref[gi].astype(BF16), dmixed_b[:, cs], preferred_element_type=F32))
        dvn = jnp.concatenate(dvn_parts, axis=-1)
        acc_ref[0:1, :] += jnp.sum(dvn * vhat, axis=0, keepdims=True)
        acc_ref[1:2, :] += jnp.sum(dvn, axis=0, keepdims=True)
        dvhat = dvn * g_ref[...]
        dv = rstd * (dvhat - jnp.mean(dvhat, axis=-1, keepdims=True)
                     - vhat * jnp.mean(dvhat * vhat, axis=-1, keepdims=True))
        dp_ref[:, w:2 * w] = (dv * _dgelu(v_pre)).astype(BF16)

    vec = pl.BlockSpec((1, w), lambda i: (0, 0))
    wspec = pl.BlockSpec((SGU_GROUPS, CHUNK, CHUNK), lambda i: (0, 0, 0))
    bspec = pl.BlockSpec((CHUNK, SGU_GROUPS), lambda i: (0, 0))
    return pl.pallas_call(
        body, name=name, grid=(n // CHUNK,),
        in_specs=[pl.BlockSpec((CHUNK, w3), lambda i: (i, 0)), pl.BlockSpec((CHUNK, w), lambda i: (i, 0)),
                  vec, vec, wspec, wspec, bspec],
        out_specs=[pl.BlockSpec((CHUNK, w3), lambda i: (i, 0)), wspec, bspec,
                   pl.BlockSpec((8, w), lambda i: (0, 0))],
        out_shape=[jax.ShapeDtypeStruct((n, w3), BF16), jax.ShapeDtypeStruct((SGU_GROUPS, CHUNK, CHUNK), F32),
                   jax.ShapeDtypeStruct((CHUNK, SGU_GROUPS), F32), jax.ShapeDtypeStruct((8, w), F32)],
        compiler_params=_params("arbitrary"),
    )(p, dm, _row(ln_g), _row(ln_b), ws, ws_t, bs_t)


def _adam_math(w, g, m, v):
    m_new = ADAM_B1 * m + (1.0 - ADAM_B1) * g
    v_new = ADAM_B2 * v + (1.0 - ADAM_B2) * (g * g)
    m_hat = m_new / (1.0 - ADAM_B1 ** ADAM_STEP)
    v_hat = v_new / (1.0 - ADAM_B2 ** ADAM_STEP)
    delta = -ADAM_LR * (m_hat / (jnp.sqrt(v_hat) + ADAM_EPS) + ADAM_WD * w)
    return delta, m_new, v_new


def adamw(w, g, m, v, *, name, slots=False, rows=512):
    r, c = w.shape
    tr = min(r, rows)
    assert r % tr == 0, (name, r, tr)

    def body(w_ref, g_ref, m_ref, v_ref, go_ref, d_ref, mo_ref, vo_ref):
        if slots:
            g = g_ref[0].astype(F32)
            for k in range(1, N_DEV):
                g = g + g_ref[k].astype(F32)
        else:
            g = g_ref[...].astype(F32)
        delta, m_new, v_new = _adam_math(w_ref[...], g, m_ref[...], v_ref[...])
        go_ref[...] = g
        d_ref[...] = delta
        mo_ref[...] = m_new
        vo_ref[...] = v_new

    spec = pl.BlockSpec((tr, c), lambda i: (i, 0))
    gspec = pl.BlockSpec((N_DEV, tr, c), lambda i: (0, i, 0)) if slots else spec
    shape = jax.ShapeDtypeStruct((r, c), F32)
    return pl.pallas_call(
        body, name=name, grid=(r // tr,),
        in_specs=[spec, gspec, spec, spec], out_specs=[spec] * 4, out_shape=[shape] * 4,
        compiler_params=_params("parallel"),
    )(w, g, m, v)


def _after(token, val):
    return val + token[0, 0]


def _pad_rows(a, rows):
    return jnp.pad(a, ((0, rows - a.shape[0]), (0, 0)))


def kernel(x, c, ctx, c_ctx, ada_w, ada_b, norm_g, ev_w_in, ev_q_norm, ev_k_norm, ev_dw_w, ev_dw_b, ev_ln_g, ev_ln_b, ev_w_out, od_w_in, od_ln_g, od_ln_b, od_ws, od_bs, od_w_out, final_g, loss_target, m_c_ctx, m_ada_w, m_ada_b, m_norm_g, m_ev_w_in, m_ev_q_norm, m_ev_k_norm, m_ev_dw_w, m_ev_dw_b, m_ev_ln_g, m_ev_ln_b, m_ev_w_out, m_od_w_in, m_od_ln_g, m_od_ln_b, m_od_ws, m_od_bs, m_od_w_out, m_final_g, v_c_ctx, v_ada_w, v_ada_b, v_norm_g, v_ev_w_in, v_ev_q_norm, v_ev_k_norm, v_ev_dw_w, v_ev_dw_b, v_ev_ln_g, v_ev_ln_b, v_ev_w_out, v_od_w_in, v_od_ln_g, v_od_ln_b, v_od_ws, v_od_bs, v_od_w_out, v_final_g):
    n, d = x.shape[1], x.shape[2]
    lc = ctx.shape[1]
    ev_in = ev_w_in.shape[2] * N_DEV
    od_in = od_w_in.shape[2] * N_DEV
    conv_ch = ev_dw_w.shape[2] * N_DEV
    ada_cols = ada_w.shape[2]
    me = 4 * lax.axis_index("x") + 2 * lax.axis_index("y") + lax.axis_index("c")
    xs, tgt, ctxs = x[0], loss_target[0], ctx[0]
    za_block = (2 * KV_W + ATTN_W) // HEAD_DIM
    glu_block = (2 * KV_W + 2 * ATTN_W) // HALF
    zb_block = glu_block + 2 * conv_ch // HALF

    small = jnp.concatenate([
        jax.nn.silu(c).reshape(1, d),
        od_ln_g.reshape(1, -1), od_ln_b.reshape(1, -1)], axis=1)
    small = _pad_rows(small, 8)
    dw_rows = _pad_rows(ev_dw_w[0], 32)
    small_g, dw_g = all_gather([small, dw_rows], name="gather_small", hbm=False)
    sc_all = small_g[:, 0, :d]
    shard = d // N_DEV
    od_ln_g_full = small_g[:, 0, d:d + shard].reshape(d)
    od_ln_b_full = small_g[:, 0, d + shard:d + 2 * shard].reshape(d)
    dw_w_full = jnp.moveaxis(dw_g, 0, 1).reshape(32, conv_ch)[:CONV_WIDTH]
    scc = jax.nn.silu(c_ctx)
    sc16 = _pad_rows(jnp.concatenate([sc_all, scc.reshape(1, d)], axis=0), 16)

    ada_bf = ada_w.astype(BF16)
    mod_loc = [matmul(sc16, ada_bf[l], mode="nn", tm=16, tn=ada_cols, tk=d, out_dtype=F32, name=f"ada_mod{l}")
               for l in range(2)]
    (mod_g,) = all_gather([jnp.stack(mod_loc)], name="gather_mod", hbm=False)
    mod_all = jnp.moveaxis(mod_g, 0, 2).reshape(2, 16, N_DEV * ada_cols) + ada_b[:, None, :]
    mod_me = lax.dynamic_index_in_dim(mod_all, me, axis=1, keepdims=False)
    shift = [mod_me[l, :d] for l in range(2)]
    scale = [mod_me[l, d:2 * d] for l in range(2)]
    gate = [mod_me[l, 2 * d:] for l in range(2)]
    shift_c, scale_c = mod_all[0, 8, :d], mod_all[0, 8, d:2 * d]

    (wi0_g,) = all_gather([ev_w_in[0].astype(BF16)], name="gather_w_in0", hbm=True)
    wi0 = jnp.moveaxis(wi0_g, 0, 1).reshape(d, ev_in)
    bits = lax.bitcast_convert_type(wi0_g[0, 0, 0], jnp.uint16)
    landed_zero = jnp.where((bits | 1) == 0, 1.0, 0.0).astype(F32)
    later = [(w[0] + landed_zero).astype(BF16) for w in (ev_w_out, od_w_in, od_w_out)]
    (h_wo0, h_wi1, h_wo1), w_token = exchange_start(later, scatter=False, name="gather_rest_start")

    def landed(handle, after, name):
        own, land = exchange_wait(handle, after, scatter=False, name=name)
        return lax.dynamic_update_slice_in_dim(land, own[None], me, axis=0)

    cexp, sexp = rope_tables(n)

    shift0 = _after(w_token, shift[0])
    h0 = norm_mod_fwd(xs, norm_g[0], shift0, scale[0], name="norm_mod_fwd0")
    hc = norm_mod_fwd(ctxs, norm_g[0], shift_c, scale_c, name="norm_mod_fwd_ctx")
    p0 = matmul(h0, wi0, mode="nn", tm=1024, tn=ev_in // 4, tk=d, out_dtype=F32, name="proj_in0")
    pc = matmul(hc, wi0, mode="nn", tm=lc, tn=2 * KV_W, tk=d, out_dtype=F32, n_out=2 * KV_W, name="proj_in_ctx")
    q_r, k_lat, v_lat = qkv_prep_fwd(p0, ev_q_norm[0], ev_k_norm[0], cexp, sexp, latent=True, name="qkv_prep")
    k_ctx, v_ctx = qkv_prep_fwd(pc, None, ev_k_norm[0], None, None, latent=False, name="kv_prep_ctx")
    k_all = jnp.concatenate([k_lat, k_ctx], axis=0)
    v_all = jnp.concatenate([v_lat, v_ctx], axis=0)
    s_all = v_all.shape[0]
    ones_col = jnp.concatenate([jnp.ones((s_all, 1), BF16), jnp.zeros((s_all, HEAD_DIM - 1), BF16)], axis=1)
    v_ext = jnp.concatenate([blk for h in range(N_KV_HEADS)
                             for blk in (v_all[:, h * HEAD_DIM:(h + 1) * HEAD_DIM], ones_col)], axis=1)
    o_attn, mix_a, lse = flash_fwd(q_r, k_all, v_ext, p0, za_block=za_block, name="flash_fwd")
    ycv, mix_b = conv_fwd(p0, dw_w_full, ev_dw_b[0], ev_ln_g[0], ev_ln_b[0], glu_block=glu_block, name="conv_fwd")
    mix0 = jnp.concatenate([mix_a, mix_b], axis=1)
    wo0 = landed(h_wo0, mix_b, "gather_w_out0_wait").reshape(-1, d)
    x1, r0 = matmul(mix0, wo0, mode="nn", tm=1024, tn=1024, tk=mix0.shape[1], out_dtype=F32, name="proj_out0",
                    res=xs, gate=_row(gate[0]))

    h1 = norm_mod_fwd(x1, norm_g[1], shift[1], scale[1], name="norm_mod_fwd1")
    wi1 = jnp.moveaxis(landed(h_wi1, h1, "gather_w_in1_wait"), 0, 1).reshape(d, od_in)
    p1 = matmul(h1, wi1, mode="nn", tm=1024, tn=od_in // 4, tk=d, out_dtype=F32, name="proj_in1")
    ws_bf = od_ws[0]
    bs_t = od_bs[0].T
    m1 = sgu_fwd(p1, od_ln_g_full, od_ln_b_full, ws_bf, bs_t, name="sgu_fwd")
    wo1 = landed(h_wo1, m1, "gather_w_out1_wait").reshape(-1, d)
    x2, r1 = matmul(m1, wo1, mode="nn", tm=1024, tn=1024, tk=m1.shape[1], out_dtype=F32, name="proj_out1",
                    res=x1, gate=_row(gate[1]))

    dx2, acc_final, loss_tile = final_loss(x2, tgt, final_g, name="final_loss")

    dr1, acc_gate1 = gate_bwd(dx2, r1, gate[1], name="gate_bwd1")
    dm1 = matmul(dr1, wo1, mode="nt", tm=1024, tn=1024, tk=d, out_dtype=F32, name="d_mix1")
    dwo1 = matmul(m1, dr1, mode="tn", tm=512, tn=512, tk=4096,out_dtype=BF16, name="d_wout1")
    dp1, dws, dbs_t, acc_sgu = sgu_bwd(p1, dm1, od_ln_g_full, od_ln_b_full, ws_bf, jnp.swapaxes(ws_bf, 1, 2), bs_t,
                                       name="sgu_bwd")
    dwi1 = matmul(h1, dp1, mode="tn", tm=512, tn=od_in // N_DEV, tk=4096,out_dtype=BF16, name="d_win1")
    dwi1_s = jnp.moveaxis(dwi1.reshape(d, N_DEV, od_in // N_DEV), 1, 0)
    dwo1_s = dwo1.reshape(N_DEV, -1, d)
    (h_gi1, h_go1), g1_token = exchange_start([dwi1_s, dwo1_s], scatter=True, name="grads1_start")
    dh1 = matmul(dp1, wi1, mode="nt", tm=1024, tn=512, tk=od_in, out_dtype=F32, name="d_h1")
    dx1, acc_norm1 = norm_mod_bwd(x1, dh1, dx2, norm_g[1], _after(g1_token, scale[1]), name="norm_mod_bwd1")

    dr0, acc_gate0 = gate_bwd(dx1, r0, gate[0], name="gate_bwd0")
    dmix0 = matmul(dr0, wo0, mode="nt", tm=1024, tn=1024, tk=d, out_dtype=F32, name="d_mix0")
    dwo0 = matmul(mix0, dr0, mode="tn", tm=512, tn=512, tk=4096,out_dtype=BF16, name="d_wout0")
    do_attn, do_s, dza, delta = attn_gate_bwd(dmix0, o_attn, p0, za_block=za_block, name="attn_gate_bwd")
    dycv, dzb, acc_ln = conv_bwd_rows(dmix0, ycv, p0, ev_ln_g[0], ev_ln_b[0], mix_block=ATTN_W // conv_ch,
                                      zb_block=zb_block, name="conv_bwd_rows")
    dglu, ddw_w, acc_dwb = conv_bwd_taps(dycv, p0, dw_w_full, glu_block=glu_block, name="conv_bwd_taps")
    lse_row = lse.reshape(N_Q_HEADS, 1, n)
    delta_row = delta.reshape(N_Q_HEADS, 1, n)
    dq_r, dk_all, dv_all = flash_bwd(q_r, do_attn, do_s, lse_row, delta_row, k_all, v_all, name="flash_bwd")
    dkvq, acc_qk = qkv_prep_bwd(p0, dq_r, dk_all[:n], dv_all[:n], ev_q_norm[0], ev_k_norm[0], cexp, sexp,
                                latent=True, name="qkv_prep_bwd")
    dpc, acc_kc = qkv_prep_bwd(pc, None, dk_all[n:], dv_all[n:], None, ev_k_norm[0], None, None,
                               latent=False, name="kv_prep_ctx_bwd")
    dp0 = jnp.concatenate([dkvq, dza, dglu, dzb], axis=1)
    dwi0 = matmul(h0, dp0, mode="tn", tm=512, tn=512, tk=4096,out_dtype=F32, name="d_win0")
    dwi0 = matmul(hc, dpc, mode="tn", tm=512, tn=2 * KV_W, tk=lc, out_dtype=F32, name="d_win0_ctx", add=dwi0)
    dwi0_s = jnp.moveaxis(dwi0.astype(BF16).reshape(d, N_DEV, ev_in // N_DEV), 1, 0)
    dwo0_s = dwo0.reshape(N_DEV, -1, d)
    (h_gi0, h_go0), g0_token = exchange_start([dwi0_s, dwo0_s], scatter=True, name="grads0_start")
    dh0 = matmul(dp0, wi0, mode="nt", tm=1024, tn=512, tk=ev_in, out_dtype=F32, name="d_h0", after=g0_token)
    dhc = matmul(dpc, wi0, mode="nt", tm=lc, tn=512, tk=2 * KV_W, out_dtype=F32, name="d_h_ctx")
    grad_x, acc_norm0 = norm_mod_bwd(xs, dh0, dx1, norm_g[0], _after(g0_token, scale[0]), name="norm_mod_bwd0")
    _, acc_normc = norm_mod_bwd(ctxs, dhc, None, norm_g[0], scale_c, name="norm_mod_bwd_ctx")

    zeros_d = jnp.zeros((d,), F32)
    dmod0 = jnp.stack([acc_norm0[0], acc_norm0[1], acc_gate0[0]])
    dmod1 = jnp.stack([acc_norm1[0], acc_norm1[1], acc_gate1[0]])
    dmodc = jnp.stack([acc_normc[0], acc_normc[1]])
    half_pad = jnp.zeros((d - 2 * conv_ch,), F32) if d > 2 * conv_ch else jnp.zeros((0,), F32)
    row_a = jnp.concatenate([acc_dwb[0], acc_ln[0], half_pad])
    row_b = jnp.concatenate([acc_ln[1], acc_qk[0], acc_qk[1] + acc_kc[1],
                             jnp.zeros((d - conv_ch - 2 * HEAD_DIM,), F32)])
    row_c = jnp.concatenate([dbs_t.T.reshape(-1), jnp.zeros((d - SGU_GROUPS * CHUNK,), F32)])
    row_loss = jnp.concatenate([loss_tile[0, :1], jnp.zeros((d - 1,), F32)])
    pack = jnp.concatenate([
        dmod0, dmod1, dmodc,
        (acc_norm0[2] + acc_normc[2])[None], acc_norm1[2][None],
        acc_final[0][None],
        acc_sgu[0][None], acc_sgu[1][None],
        row_a[None], row_b[None], row_c[None], row_loss[None],
        ddw_w.reshape(-1, d),
        dws.reshape(-1, d),
    ], axis=0)
    n_rows = pack.shape[0]
    pack = _pad_rows(pack, -(-n_rows // 8) * 8)
    pack_g, pack_sum = all_gather([pack], name="gather_small_grads", hbm=False, sum_out=True)
    gsum = pack_sum
    loss = gsum[16, 0]
    dw_rows_n = 32 * conv_ch // d
    g_dw_w = gsum[17:17 + dw_rows_n].reshape(32, conv_ch)[:CONV_WIDTH]
    g_od_ws = gsum[17 + dw_rows_n:17 + dw_rows_n + SGU_GROUPS * CHUNK * CHUNK // d].reshape(od_ws.shape)

    dmodc_sum = jnp.concatenate([gsum[6], gsum[7], zeros_d])
    col0 = me * ada_cols
    dm_cols = []
    for l in range(2):
        rows = pack_g[:, 3 * l:3 * l + 3, :].reshape(N_DEV, 3 * d)
        extra = dmodc_sum[None] if l == 0 else jnp.zeros((1, 3 * d), F32)
        full = _pad_rows(jnp.concatenate([rows, extra], axis=0), 16)
        dm_cols.append(lax.dynamic_slice_in_dim(full, col0, ada_cols, axis=1))
    g_ada_w = jnp.stack([matmul(sc16, dm_cols[l], mode="tn", tm=512, tn=ada_cols, tk=16, out_dtype=F32,
                                name=f"d_ada_w{l}") for l in range(2)])
    dsc = matmul(dm_cols[0], ada_bf[0], mode="nt", tm=16, tn=512, tk=ada_cols, out_dtype=F32, name="d_scc")
    (_, dscc_sum) = all_gather([dsc[8:16]], name="gather_dscc", hbm=False, sum_out=True)
    sg = jax.nn.sigmoid(c_ctx)
    g_c_ctx = dscc_sum[0] * (sg * (1.0 + c_ctx * (1.0 - sg)))
    g_ada_b = jnp.stack([gsum[0:3].reshape(-1) + dmodc_sum, gsum[3:6].reshape(-1)])

    def summands(handle, after, name):
        mine, land = exchange_wait(handle, after, scatter=True, name=name)
        own = lax.dynamic_index_in_dim(mine, me, axis=0, keepdims=True)
        return lax.dynamic_update_slice_in_dim(land, own, me, axis=0)

    out = {}

    def upd(key, w, g, m, v, slots=False, rows=512):
        shp = w.shape
        w2 = w.reshape(-1, shp[-1])
        g2 = g.reshape((N_DEV, -1, shp[-1])) if slots else g.reshape(-1, shp[-1])
        res = adamw(w2, g2, m.reshape(w2.shape), v.reshape(w2.shape), name="adamw_" + key, slots=slots, rows=rows)
        out[key] = tuple(r.reshape(shp) for r in res)

    upd("ada_w", ada_w, g_ada_w, m_ada_w, v_ada_w)
    gi1 = summands(h_gi1,out["ada_w"][1], "grads_w_in1_wait")
    upd("od_w_in", od_w_in, gi1, m_od_w_in, v_od_w_in, slots=True, rows=256)
    go1 = summands(h_go1,out["od_w_in"][1], "grads_w_out1_wait")
    upd("od_w_out", od_w_out, go1, m_od_w_out, v_od_w_out, slots=True, rows=256)
    go0 = summands(h_go0,out["od_w_out"][1], "grads_w_out0_wait")
    upd("ev_w_out", ev_w_out, go0, m_ev_w_out, v_ev_w_out, slots=True, rows=256)
    gi0 = summands(h_gi0,out["ev_w_out"][1], "grads_w_in0_wait")
    upd("ev_w_in", ev_w_in, gi0, m_ev_w_in, v_ev_w_in, slots=True, rows=256)

    def my_shard(full, size):
        return lax.dynamic_slice_in_dim(full, me * size, size, axis=full.ndim - 1)

    small_items = [
        ("c_ctx", c_ctx, g_c_ctx, m_c_ctx, v_c_ctx),
        ("ada_b", ada_b, g_ada_b, m_ada_b, v_ada_b),
        ("norm_g", norm_g, gsum[8:10], m_norm_g, v_norm_g),
        ("ev_q_norm", ev_q_norm, gsum[14, conv_ch:conv_ch + HEAD_DIM], m_ev_q_norm, v_ev_q_norm),
        ("ev_k_norm", ev_k_norm, gsum[14, conv_ch + HEAD_DIM:conv_ch + 2 * HEAD_DIM], m_ev_k_norm, v_ev_k_norm),
        ("ev_dw_w", ev_dw_w, my_shard(g_dw_w, conv_ch // N_DEV), m_ev_dw_w, v_ev_dw_w),
        ("ev_dw_b", ev_dw_b, gsum[13, :conv_ch], m_ev_dw_b, v_ev_dw_b),
        ("ev_ln_g", ev_ln_g, gsum[13, conv_ch:2 * conv_ch], m_ev_ln_g, v_ev_ln_g),
        ("ev_ln_b", ev_ln_b, gsum[14, :conv_ch], m_ev_ln_b, v_ev_ln_b),
        ("od_ln_g", od_ln_g, my_shard(gsum[11], shard), m_od_ln_g, v_od_ln_g),
        ("od_ln_b", od_ln_b, my_shard(gsum[12], shard), m_od_ln_b, v_od_ln_b),
        ("od_ws", od_ws, g_od_ws, m_od_ws, v_od_ws),
        ("od_bs", od_bs, gsum[15, :SGU_GROUPS * CHUNK], m_od_bs, v_od_bs),
        ("final_g", final_g, gsum[10], m_final_g, v_final_g),
    ]
    sizes = [it[1].size for it in small_items]
    total = sum(sizes)
    lanes = 1024
    prow = -(-total // lanes)
    prow = -(-prow // 8) * 8

    def pack_small(idx):
        flat = jnp.concatenate([it[idx].reshape(-1).astype(F32) for it in small_items])
        return jnp.pad(flat, (0, prow * lanes - total)).reshape(prow, lanes)

    sres = adamw(pack_small(1), pack_small(2), pack_small(3), pack_small(4), name="adamw_small", rows=prow)
    off = 0
    for it, size in zip(small_items, sizes):
        out[it[0]] = tuple(r.reshape(-1)[off:off + size].reshape(it[1].shape) for r in sres)
        off += size

    names = ['c_ctx', 'ada_w', 'ada_b', 'norm_g', 'ev_w_in', 'ev_q_norm', 'ev_k_norm', 'ev_dw_w', 'ev_dw_b',
             'ev_ln_g', 'ev_ln_b', 'ev_w_out', 'od_w_in', 'od_ln_g', 'od_ln_b', 'od_ws', 'od_bs', 'od_w_out',
             'final_g']
    return (loss, grad_x[None], *[out[k][0] for k in names], *[out[k][1] for k in names],
            *[out[k][2] for k in names], *[out[k][3] for k in names])
```

```python
import functools
import math

import jax
import jax.numpy as jnp
from jax import lax
from jax.experimental import pallas as pl
from jax.experimental.pallas import tpu as pltpu

F32 = jnp.float32
BF16 = jnp.bfloat16
MESH = pl.DeviceIdType.MESH

EPS = 1e-6
HEAD_DIM = 128
N_Q_HEADS = 8
N_KV_HEADS = 2
Q_PER_KV = N_Q_HEADS // N_KV_HEADS
ATTN_W = N_Q_HEADS * HEAD_DIM
KV_W = N_KV_HEADS * HEAD_DIM
ATTN_SCALE = HEAD_DIM ** -0.5
LN2 = math.log(2.0)
Q_SCALE = ATTN_SCALE / LN2
ROPE_THETA = 10000.0
GRID_W = 64
CONV_WIDTH = 31
CONV_HALF = CONV_WIDTH // 2
HALO = 16
CHUNK = 128
SGU_GROUPS = 8
N_DEV = 8

ADAM_LR = 0.001
ADAM_B1 = 0.9
ADAM_B2 = 0.999
ADAM_EPS = 1e-08
ADAM_WD = 0.01
ADAM_STEP = 10

VMEM_LIMIT = 56 * 1024 * 1024
ANY = pl.BlockSpec(memory_space=pl.ANY)
VMEM_SPEC = pl.BlockSpec(memory_space=pltpu.VMEM)


def _params(*sem):
    return pltpu.CompilerParams(dimension_semantics=sem, vmem_limit_bytes=VMEM_LIMIT)


def _sigmoid(x):
    return 1.0 / (1.0 + jnp.exp(-x))


def _silu(x):
    return x * _sigmoid(x)


def _dsilu(x):
    s = _sigmoid(x)
    return s * (1.0 + x * (1.0 - s))


_GELU_C = math.sqrt(2.0 / math.pi)


def _gelu(x):
    t = jnp.tanh(_GELU_C * (x + 0.044715 * (x * x * x)))
    return 0.5 * x * (1.0 + t)


def _dgelu(x):
    t = jnp.tanh(_GELU_C * (x + 0.044715 * (x * x * x)))
    return 0.5 * (1.0 + t) + 0.5 * x * (1.0 - t * t) * (_GELU_C * (1.0 + 3.0 * 0.044715 * (x * x)))


def _row(v):
    return v.reshape(1, -1).astype(F32)


def _flat_id(p):
    return 4 * p[0] + 2 * p[1] + p[2]


def _gather_body(n_arr, sum_out):
    def body(*refs):
        x_refs = refs[:n_arr]
        out_refs = refs[n_arr:2 * n_arr]
        pos = 2 * n_arr
        sum_refs = refs[pos:pos + n_arr] if sum_out else ()
        pos += n_arr if sum_out else 0
        send_sems, recv_sems, local_sems = refs[pos:pos + 3]
        x, y, c = lax.axis_index("x"), lax.axis_index("y"), lax.axis_index("c")
        me, sibling = (x, y, c), (x, y, 1 - c)
        chips = [(1 - x, y), (x, 1 - y), (1 - x, 1 - y)]

        def copy(a, k, block, to, src=None):
            rows = out_refs[a].at[_flat_id(block)]
            return pltpu.make_async_remote_copy(
                src_ref=rows if src is None else src, dst_ref=rows,
                send_sem=send_sems.at[a, k], recv_sem=recv_sems.at[a, k],
                device_id=to, device_id_type=MESH)

        sends = []
        mine = []
        for a in range(n_arr):
            cp = pltpu.make_async_copy(x_refs[a], out_refs[a].at[_flat_id(me)], local_sems.at[a])
            cp.start()
            mine.append(cp)
            first = [copy(a, 0, me, sibling, src=x_refs[a])]
            first += [copy(a, 1 + j, me, (*chip, c), src=x_refs[a]) for j, chip in enumerate(chips)]
            for cp in first:
                cp.start()
            sends += first
        for a in range(n_arr):
            for j, chip in enumerate(chips):
                copy(a, 1 + j, (*chip, c), me).wait_recv()
                fwd = copy(a, 4 + j, (*chip, c), sibling)
                fwd.start()
                sends.append(fwd)
        for a in range(n_arr):
            copy(a, 0, sibling, me).wait_recv()
            for j, chip in enumerate(chips):
                copy(a, 4 + j, (*chip, 1 - c), me).wait_recv()
        for cp in sends:
            cp.wait_send()
        for cp in mine:
            cp.wait()
        if sum_out:
            for a in range(n_arr):
                acc = out_refs[a][0]
                for k in range(1, N_DEV):
                    acc = acc + out_refs[a][k]
                sum_refs[a][...] = acc

    return body


def all_gather(arrs, *, name, hbm, sum_out=False):
    n = len(arrs)
    spec = ANY if hbm else VMEM_SPEC
    out_shape = [jax.ShapeDtypeStruct((N_DEV,) + a.shape, a.dtype) for a in arrs]
    out_specs = [spec] * n
    if sum_out:
        out_shape += [jax.ShapeDtypeStruct(a.shape, a.dtype) for a in arrs]
        out_specs += [VMEM_SPEC] * n
    res = pl.pallas_call(
        _gather_body(n, sum_out), name=name,
        out_shape=out_shape, in_specs=[spec] * n, out_specs=out_specs,
        scratch_shapes=[pltpu.SemaphoreType.DMA((n, 7)), pltpu.SemaphoreType.DMA((n, 7)),
                        pltpu.SemaphoreType.DMA((n,))],
        compiler_params=pltpu.CompilerParams(vmem_limit_bytes=VMEM_LIMIT),
    )(*arrs)
    return res


_RELATIONS = [(rx, ry, rc) for rx in (0, 1) for ry in (0, 1) for rc in (0, 1)][1:]


HBM_SPEC = pl.BlockSpec(memory_space=pltpu.HBM)
SEM_SPEC = pl.BlockSpec(memory_space=pltpu.SEMAPHORE)
_DATAFLOW = pltpu.SideEffectType.DATAFLOW_SIDE_EFFECTING
N_PEERS = N_DEV - 1


def _peer_copy(src_ref, land_ref, send_sem, recv_sem, k, rel, scatter, sending):
    x, y, c = lax.axis_index("x"), lax.axis_index("y"), lax.axis_index("c")
    rx, ry, rc = rel
    peer = (1 - x if rx else x, 1 - y if ry else y, 1 - c if rc else c)
    src = src_ref.at[_flat_id(peer)] if scatter else src_ref
    dst = land_ref.at[_flat_id((x, y, c)) if sending else _flat_id(peer)]
    return pltpu.make_async_remote_copy(src_ref=src, dst_ref=dst, send_sem=send_sem.at[k], recv_sem=recv_sem.at[k],
                                        device_id=peer, device_id_type=MESH)


def exchange_start(arrs, *, scatter, name):
    n = len(arrs)
    lands = [lax.empty((N_DEV,) + (a.shape[1:] if scatter else a.shape), a.dtype) for a in arrs]

    def body(*refs):
        srcs, lnds, sems = refs[:n], refs[n:2 * n], refs[2 * n:4 * n]
        token = refs[6 * n]
        for a in range(n):
            for k, rel in enumerate(_RELATIONS):
                _peer_copy(srcs[a], lnds[a], sems[2 * a], sems[2 * a + 1], k, rel, scatter, True).start()
        token[...] = jnp.zeros_like(token)

    outs = pl.pallas_call(
        body, name=name,
        out_shape=[pltpu.SemaphoreType.DMA((N_PEERS,))] * (2 * n)
        + [pltpu.HBM(a.shape, a.dtype) for a in arrs] + [pltpu.HBM(l.shape, l.dtype) for l in lands]
        + [jax.ShapeDtypeStruct((8, 128), F32)],
        in_specs=[HBM_SPEC] * (2 * n),
        out_specs=[SEM_SPEC] * (2 * n) + [HBM_SPEC] * (2 * n) + [VMEM_SPEC],
        input_output_aliases={i: 2 * n + i for i in range(2 * n)},
        compiler_params=pltpu.CompilerParams(has_side_effects=_DATAFLOW),
    )(*[pltpu.with_memory_space_constraint(a, pltpu.HBM) for a in arrs],
      *[pltpu.with_memory_space_constraint(l, pltpu.HBM) for l in lands])
    handles = [(outs[2 * a], outs[2 * a + 1], outs[2 * n + a], outs[3 * n + a]) for a in range(n)]
    return handles, outs[4 * n]


def exchange_wait(handle, after, *, scatter, name):
    send_sem, recv_sem, src, land = handle

    def body(src_ref, land_ref, send_ref, recv_ref, after_ref, src_out, land_out):
        for k, rel in enumerate(_RELATIONS):
            cp = _peer_copy(src_ref, land_ref, send_ref, recv_ref, k, rel, scatter, False)
            cp.wait_send()
            cp.wait_recv()

    outs = pl.pallas_call(
        body, name=name,
        out_shape=[pltpu.HBM(src.shape, src.dtype), pltpu.HBM(land.shape, land.dtype)],
        in_specs=[HBM_SPEC, HBM_SPEC, SEM_SPEC, SEM_SPEC, ANY],
        out_specs=[HBM_SPEC, HBM_SPEC], input_output_aliases={0: 0, 1: 1},
        compiler_params=pltpu.CompilerParams(has_side_effects=_DATAFLOW),
    )(src, land, send_sem, recv_sem, after)
    return outs[0], outs[1]


_DIMS = {"nn": (((1,), (0,)), ((), ())), "nt": (((1,), (1,)), ((), ())), "tn": (((0,), (0,)), ((), ()))}


def matmul(a, b, *, mode, tm, tn, tk, out_dtype, name, n_out=None, res=None, gate=None, add=None, after=None,
           split_out=False):
    if mode == "tn":
        kdim, m = a.shape
    else:
        m, kdim = a.shape
    nfull = b.shape[0] if mode == "nt" else b.shape[1]
    n = nfull if n_out is None else n_out
    tm, tn, tk = min(tm, m), min(tn, n), min(tk, kdim)
    assert m % tm == 0 and n % tn == 0 and kdim % tk == 0, (name, m, n, kdim, tm, tn, tk)
    nk = kdim // tk
    dims = _DIMS[mode]
    a_spec = pl.BlockSpec((tk, tm), lambda j, i, k: (k, i)) if mode == "tn" else pl.BlockSpec((tm, tk), lambda j, i, k: (i, k))
    b_spec = pl.BlockSpec((tn, tk), lambda j, i, k: (j, k)) if mode == "nt" else pl.BlockSpec((tk, tn), lambda j, i, k: (k, j))
    o_spec = pl.BlockSpec((tm, tn), lambda j, i, k: (i, j))
    in_specs = [a_spec, b_spec]
    operands = [a, b]
    aliases = {}
    if res is not None:
        in_specs += [o_spec, pl.BlockSpec((1, tn), lambda j, i, k: (0, j))]
        operands += [res, gate]
    if add is not None:
        in_specs += [o_spec]
        aliases = {len(operands): 0}
        operands += [add]
    if after is not None:
        in_specs += [ANY]
        operands += [after]
    out_cols = n if add is None else add.shape[1]
    out_shape = [jax.ShapeDtypeStruct((m, out_cols), out_dtype)]
    out_specs = [o_spec]
    if split_out:
        out_shape = [jax.ShapeDtypeStruct((n // tn, m, tn), out_dtype)]
        out_specs = [pl.BlockSpec((None, tm, tn), lambda j, i, k: (j, i, 0))]
    if res is not None:
        out_shape.append(jax.ShapeDtypeStruct((m, n), BF16))
        out_specs.append(o_spec)

    def body(*refs):
        a_ref, b_ref = refs[:2]
        pos = 2
        if res is not None:
            res_ref, gate_ref = refs[pos:pos + 2]
            pos += 2
        if add is not None:
            add_ref = refs[pos]
            pos += 1
        if after is not None:
            pos += 1
        o_ref = refs[pos]
        pos += 1
        if res is not None:
            r_ref = refs[pos]
            pos += 1
        acc_ref = refs[pos] if nk > 1 else None
        prod = lax.dot_general(a_ref[...].astype(BF16), b_ref[...].astype(BF16), dims,
                               preferred_element_type=F32)

        def finish(acc):
            if res is not None:
                o_ref[...] = (res_ref[...] + gate_ref[...] * acc).astype(out_dtype)
                r_ref[...] = acc.astype(BF16)
            elif add is not None:
                o_ref[...] = (add_ref[...] + acc).astype(out_dtype)
            else:
                o_ref[...] = acc.astype(out_dtype)

        if nk == 1:
            finish(prod)
        else:
            k = pl.program_id(2)

            @pl.when(k == 0)
            def _():
                acc_ref[...] = prod

            @pl.when(k > 0)
            def _():
                acc_ref[...] += prod

            @pl.when(k == nk - 1)
            def _():
                finish(acc_ref[...])

    outs = pl.pallas_call(
        body, name=name, grid=(n // tn, m // tm, nk),
        in_specs=in_specs, out_specs=out_specs, out_shape=out_shape,
        scratch_shapes=[pltpu.VMEM((tm, tn), F32)] if nk > 1 else [],
        input_output_aliases=aliases,
        compiler_params=_params("parallel", "parallel", "arbitrary"),
    )(*operands)
    return outs if res is not None else outs[0]


def _rows_tile(n, want):
    return min(n, want)


def norm_mod_fwd(x, g, shift, scale, *, name):
    n, d = x.shape
    tt = _rows_tile(n, 256)

    def body(x_ref, g_ref, sh_ref, sc_ref, h_ref):
        xv = x_ref[...]
        rstd = lax.rsqrt(jnp.mean(xv * xv, axis=-1, keepdims=True) + EPS)
        y = xv * rstd * g_ref[...]
        h_ref[...] = (y * (1.0 + sc_ref[...]) + sh_ref[...]).astype(BF16)

    vec = pl.BlockSpec((1, d), lambda i: (0, 0))
    return pl.pallas_call(
        body, name=name, grid=(n // tt,),
        in_specs=[pl.BlockSpec((tt, d), lambda i: (i, 0)), vec, vec, vec],
        out_specs=pl.BlockSpec((tt, d), lambda i: (i, 0)),
        out_shape=jax.ShapeDtypeStruct((n, d), BF16),
        compiler_params=_params("parallel"),
    )(x, _row(g), _row(shift), _row(scale))


GATE_ROW = 3


def norm_mod_bwd(x, dh, dres, g, scale, *, name, branch=None):
    n, d = x.shape
    tt = _rows_tile(n, 256)
    has_res = dres is not None
    has_branch = branch is not None
    last = n // tt - 1

    def body(*refs):
        x_ref, dh_ref = refs[:2]
        pos = 2
        if has_res:
            dres_ref = refs[pos]
            pos += 1
        if has_branch:
            r_ref, gate_ref = refs[pos:pos + 2]
            pos += 2
        g_ref, sc_ref, dx_ref = refs[pos:pos + 3]
        pos += 3
        if has_branch:
            dr_ref = refs[pos]
            pos += 1
        acc_ref, s_ref = refs[pos:pos + 2]
        i = pl.program_id(0)
        xv = x_ref[...]
        dhv = dh_ref[...]
        rstd = lax.rsqrt(jnp.mean(xv * xv, axis=-1, keepdims=True) + EPS)
        xhat = xv * rstd
        dxhat = dhv * (g_ref[...] * (1.0 + sc_ref[...]))
        dx = rstd * (dxhat - xhat * jnp.mean(dxhat * xhat, axis=-1, keepdims=True))
        if has_res:
            dx = dx + dres_ref[...]
        dx_ref[...] = dx

        @pl.when(i == 0)
        def _():
            s_ref[...] = jnp.zeros_like(s_ref)

        s_ref[0:1, :] += jnp.sum(dhv, axis=0, keepdims=True)
        s_ref[1:2, :] += jnp.sum(dhv * xhat, axis=0, keepdims=True)
        if has_branch:
            dr_ref[...] = (dx * gate_ref[...]).astype(BF16)
            s_ref[2:3, :] += jnp.sum(dx * r_ref[...].astype(F32), axis=0, keepdims=True)

        @pl.when(i == last)
        def _():
            s1 = s_ref[0:1, :]
            s2 = s_ref[1:2, :]
            acc_ref[...] = jnp.zeros_like(acc_ref)
            acc_ref[0:1, :] = s1
            acc_ref[1:2, :] = s2 * g_ref[...]
            acc_ref[2:3, :] = s2 * (1.0 + sc_ref[...])
            acc_ref[GATE_ROW:GATE_ROW + 1, :] = s_ref[2:3, :]

    vec = pl.BlockSpec((1, d), lambda i: (0, 0))
    big = pl.BlockSpec((tt, d), lambda i: (i, 0))
    ops = [x, dh] + ([dres] if has_res else []) + ([branch[0], _row(branch[1])] if has_branch else [])
    ops += [_row(g), _row(scale)]
    return pl.pallas_call(
        body, name=name, grid=(n // tt,),
        in_specs=[big, big] + ([big] if has_res else []) + ([big, vec] if has_branch else []) + [vec, vec],
        out_specs=[big] + ([big] if has_branch else []) + [pl.BlockSpec((8, d), lambda i: (0, 0))],
        out_shape=[jax.ShapeDtypeStruct((n, d), F32)] + ([jax.ShapeDtypeStruct((n, d), BF16)] if has_branch else [])
        + [jax.ShapeDtypeStruct((8, d), F32)],
        scratch_shapes=[pltpu.VMEM((8, d), F32)],
        compiler_params=_params("arbitrary"),
    )(*ops)


def final_loss(x, target, g, r, gate, *, name):
    n, d = x.shape
    tt = _rows_tile(n, 256)

    def body(x_ref, t_ref, g_ref, r_ref, gate_ref, dx_ref, dr_ref, acc_ref, loss_ref):
        i = pl.program_id(0)
        xv = x_ref[...]
        rstd = lax.rsqrt(jnp.mean(xv * xv, axis=-1, keepdims=True) + EPS)
        xhat = xv * rstd
        e = xhat * g_ref[...] - t_ref[...]
        dy = e * (1.0 / d)
        dxhat = dy * g_ref[...]
        dx = rstd * (dxhat - xhat * jnp.mean(dxhat * xhat, axis=-1, keepdims=True))
        dx_ref[...] = dx
        dr_ref[...] = (dx * gate_ref[...]).astype(BF16)

        @pl.when(i == 0)
        def _():
            acc_ref[...] = jnp.zeros_like(acc_ref)
            loss_ref[...] = jnp.zeros_like(loss_ref)

        acc_ref[0:1, :] += jnp.sum(dy * xhat, axis=0, keepdims=True)
        acc_ref[GATE_ROW:GATE_ROW + 1, :] += jnp.sum(dx * r_ref[...].astype(F32), axis=0, keepdims=True)
        part = 0.5 * jnp.sum(jnp.mean(e * e, axis=-1, keepdims=True), axis=0, keepdims=True)
        loss_ref[...] += jnp.broadcast_to(part, loss_ref.shape)

    big = pl.BlockSpec((tt, d), lambda i: (i, 0))
    vec = pl.BlockSpec((1, d), lambda i: (0, 0))
    return pl.pallas_call(
        body, name=name, grid=(n // tt,),
        in_specs=[big, big, vec, big, vec],
        out_specs=[big, big, pl.BlockSpec((8, d), lambda i: (0, 0)), pl.BlockSpec((8, 128), lambda i: (0, 0))],
        out_shape=[jax.ShapeDtypeStruct((n, d), F32), jax.ShapeDtypeStruct((n, d), BF16),
                   jax.ShapeDtypeStruct((8, d), F32), jax.ShapeDtypeStruct((8, 128), F32)],
        compiler_params=_params("arbitrary"),
    )(x, target, _row(g), r, _row(gate))


def _swap_pairs(x):
    lane = lax.broadcasted_iota(jnp.int32, x.shape, 1)
    return jnp.where(lane % 2 == 0, pltpu.roll(x, HEAD_DIM - 1, 1), pltpu.roll(x, 1, 1))


def rope_tables(n):
    rows = n // GRID_W
    row = jnp.repeat(jnp.arange(rows, dtype=F32), GRID_W)
    col = jnp.tile(jnp.arange(GRID_W, dtype=F32), rows)
    n_freq = HEAD_DIM // 4
    inv = jnp.power(ROPE_THETA, jnp.arange(n_freq, dtype=F32) * (-2.0 / (HEAD_DIM // 2)))
    ang = jnp.concatenate([row[:, None] * inv, col[:, None] * inv], axis=-1)
    cos, sin = jnp.cos(ang), jnp.sin(ang)
    cexp = jnp.repeat(cos, 2, axis=-1)
    sexp = jnp.stack([-sin, sin], axis=-1).reshape(n, HEAD_DIM)
    return cexp, sexp


V_EXT_W = 2 * HEAD_DIM


def qkv_prep_fwd(p, wq, wk, cexp, sexp, *, latent, name, rows_all=None, k_all=None, v_ext=None):
    n = p.shape[0]
    tt = _rows_tile(n, 256)
    width = 2 * KV_W + (ATTN_W if latent else 0)

    def body(*refs):
        if latent:
            p_ref, wq_ref, wk_ref, c_ref, s_ref, q_ref, k_ref, v_ref = refs
        else:
            p_ref, wk_ref, _, _, k_ref, v_ref = refs

        def head(xv, w):
            rstd = lax.rsqrt(jnp.mean(xv * xv, axis=-1, keepdims=True) + EPS)
            yv = xv * rstd * w
            if latent:
                yv = yv * c_ref[...] + _swap_pairs(yv) * s_ref[...]
            return yv

        for h in range(N_KV_HEADS):
            sl = slice(h * HEAD_DIM, (h + 1) * HEAD_DIM)
            k_ref[:, sl] = head(p_ref[:, sl], wk_ref[...]).astype(BF16)
            v_ref[:, h * V_EXT_W:h * V_EXT_W + HEAD_DIM] = p_ref[:, KV_W + h * HEAD_DIM:KV_W + (h + 1) * HEAD_DIM].astype(BF16)
            lane = lax.broadcasted_iota(jnp.int32, (tt, HEAD_DIM), 1)
            v_ref[:, h * V_EXT_W + HEAD_DIM:(h + 1) * V_EXT_W] = jnp.where(lane == 0, 1.0, 0.0).astype(BF16)
        if latent:
            for h in range(N_Q_HEADS):
                sl = slice(2 * KV_W + h * HEAD_DIM, 2 * KV_W + (h + 1) * HEAD_DIM)
                q_ref[:, h * HEAD_DIM:(h + 1) * HEAD_DIM] = (head(p_ref[:, sl], wq_ref[...]) * Q_SCALE).astype(BF16)

    vec = pl.BlockSpec((1, HEAD_DIM), lambda i: (0, 0))
    tab = pl.BlockSpec((tt, HEAD_DIM), lambda i: (i, 0))
    vw = N_KV_HEADS * V_EXT_W
    k_shape = jax.ShapeDtypeStruct((rows_all, KV_W), BF16)
    v_shape = jax.ShapeDtypeStruct((rows_all, vw), BF16)
    aliases = {}
    if latent:
        in_specs = [pl.BlockSpec((tt, width), lambda i: (i, 0)), vec, vec, tab, tab]
        ops = [p, _row(wq), _row(wk), cexp, sexp]
        out_specs = [pl.BlockSpec((tt, ATTN_W), lambda i: (i, 0)), pl.BlockSpec((tt, KV_W), lambda i: (i, 0)),
                     pl.BlockSpec((tt, vw), lambda i: (i, 0))]
        out_shape = [jax.ShapeDtypeStruct((n, ATTN_W), BF16), k_shape, v_shape]
    else:
        assert n == tt and (rows_all - n) % n == 0, (n, tt, rows_all)
        first = (rows_all - n) // n
        in_specs = [pl.BlockSpec((tt, width), lambda i: (i, 0)), vec, ANY, ANY]
        ops = [p, _row(wk), k_all, v_ext]
        out_specs = [pl.BlockSpec((tt, KV_W), lambda i: (first, 0)), pl.BlockSpec((tt, vw), lambda i: (first, 0))]
        out_shape = [k_shape, v_shape]
        aliases = {2: 0, 3: 1}
    return pl.pallas_call(
        body, name=name, grid=(n // tt,), in_specs=in_specs, out_specs=out_specs, out_shape=out_shape,
        input_output_aliases=aliases, compiler_params=_params("parallel"),
    )(*ops)


def qkv_prep_bwd(p, dq, dk, dv, wq, wk, cexp, sexp, *, latent, name):
    n = p.shape[0]
    tt = _rows_tile(n, 256)
    width = 2 * KV_W + (ATTN_W if latent else 0)

    def body(*refs):
        if latent:
            p_ref, dq_ref, dk_ref, dv_ref, wq_ref, wk_ref, c_ref, s_ref, dp_ref, acc_ref = refs
        else:
            p_ref, dk_ref, dv_ref, wk_ref, dp_ref, acc_ref = refs
        i = pl.program_id(0)

        @pl.when(i == 0)
        def _():
            acc_ref[...] = jnp.zeros_like(acc_ref)

        def head(xv, dy, w, row):
            if latent:
                dy = dy * c_ref[...] + _swap_pairs(dy * s_ref[...])
            rstd = lax.rsqrt(jnp.mean(xv * xv, axis=-1, keepdims=True) + EPS)
            xhat = xv * rstd
            acc_ref[row:row + 1, :] += jnp.sum(dy * xhat, axis=0, keepdims=True)
            dxhat = dy * w
            return rstd * (dxhat - xhat * jnp.mean(dxhat * xhat, axis=-1, keepdims=True))

        for h in range(N_KV_HEADS):
            sl = slice(h * HEAD_DIM, (h + 1) * HEAD_DIM)
            dp_ref[:, sl] = head(p_ref[:, sl], dk_ref[:, sl], wk_ref[...], 1).astype(BF16)
        dp_ref[:, KV_W:2 * KV_W] = dv_ref[...].astype(BF16)
        if latent:
            for h in range(N_Q_HEADS):
                sl = slice(2 * KV_W + h * HEAD_DIM, 2 * KV_W + (h + 1) * HEAD_DIM)
                dyq = dq_ref[:, h * HEAD_DIM:(h + 1) * HEAD_DIM] * Q_SCALE
                dp_ref[:, sl] = head(p_ref[:, sl], dyq, wq_ref[...], 0).astype(BF16)

    vec = pl.BlockSpec((1, HEAD_DIM), lambda i: (0, 0))
    tab = pl.BlockSpec((tt, HEAD_DIM), lambda i: (i, 0))
    kv_spec = pl.BlockSpec((tt, KV_W), lambda i: (i, 0))
    p_spec = pl.BlockSpec((tt, width), lambda i: (i, 0))
    if latent:
        in_specs = [p_spec, pl.BlockSpec((tt, ATTN_W), lambda i: (i, 0)), kv_spec, kv_spec, vec, vec, tab, tab]
        ops = [p, dq, dk, dv, _row(wq), _row(wk), cexp, sexp]
    else:
        in_specs = [p_spec, kv_spec, kv_spec, vec]
        ops = [p, dk, dv, _row(wk)]
    return pl.pallas_call(
        body, name=name, grid=(n // tt,), in_specs=in_specs,
        out_specs=[p_spec, pl.BlockSpec((8, HEAD_DIM), lambda i: (0, 0))],
        out_shape=[jax.ShapeDtypeStruct((n, width), BF16), jax.ShapeDtypeStruct((8, HEAD_DIM), F32)],
        compiler_params=_params("arbitrary"),
    )(*ops)


def _kv_chunks(n, s_all):
    step = 1024 if n % 1024 == 0 else 256
    chunks = [(s, step) for s in range(0, n, step)]
    if s_all > n:
        chunks.append((n, s_all - n))
    return chunks


def flash_fwd(q, k_all, v_ext, p, *, za_block, mix_width, name):
    n = q.shape[0]
    s_all = k_all.shape[0]
    tq = _rows_tile(n, 512)
    chunks = _kv_chunks(n, s_all)
    wide = 2 * HEAD_DIM

    def body(q_ref, k_ref, v_ref, za_ref, o_ref, mix_ref, lse_ref, m_ref, acc_ref):
        qv = q_ref[...]
        m_ref[...] = jnp.full_like(m_ref, -jnp.inf)
        acc_ref[...] = jnp.zeros_like(acc_ref)
        for start, size in chunks:
            kc = k_ref[pl.ds(start, size), :]
            vc = v_ref[pl.ds(start, size), :]
            s = lax.dot_general(qv, kc, _DIMS["nt"], preferred_element_type=F32)
            m_old = m_ref[...]
            m_new = jnp.maximum(m_old, jnp.max(s, axis=-1, keepdims=True))
            pr = jnp.exp2(s - m_new)
            alpha = jnp.exp2(m_old - m_new)
            acc_ref[...] = alpha * acc_ref[...] + jnp.dot(pr.astype(BF16), vc, preferred_element_type=F32)
            m_ref[...] = m_new
        acc = acc_ref[...]
        denom = acc[:, HEAD_DIM:HEAD_DIM + 1]
        o = acc[:, :HEAD_DIM] / denom
        o_ref[...] = o.astype(BF16)
        mix_ref[...] = (o * _silu(za_ref[...])).astype(BF16)
        lse_ref[0] = m_ref[...] + jnp.log2(denom)

    qspec = pl.BlockSpec((tq, HEAD_DIM), lambda h, i: (i, h))
    return pl.pallas_call(
        body, name=name, grid=(N_Q_HEADS, n // tq),
        in_specs=[qspec, pl.BlockSpec((s_all, HEAD_DIM), lambda h, i: (0, h // Q_PER_KV)),
                  pl.BlockSpec((s_all, wide), lambda h, i: (0, h // Q_PER_KV)),
                  pl.BlockSpec((tq, HEAD_DIM), lambda h, i: (i, za_block + h))],
        out_specs=[qspec, qspec, pl.BlockSpec((1, tq, 1), lambda h, i: (h, i, 0))],
        out_shape=[jax.ShapeDtypeStruct((n, ATTN_W), BF16), jax.ShapeDtypeStruct((n, mix_width), BF16),
                   jax.ShapeDtypeStruct((N_Q_HEADS, n, 1), F32)],
        scratch_shapes=[pltpu.VMEM((tq, 1), F32), pltpu.VMEM((tq, wide), F32)],
        compiler_params=_params("parallel", "parallel"),
    )(q, k_all, v_ext, p)


def attn_gate_bwd(dmix, o, p, *, za_block, name):
    n = o.shape[0]
    tt = _rows_tile(n, 512)

    def body(dm_ref, o_ref, za_ref, do_ref, dos_ref, dza_ref, delta_ref):
        dm = dm_ref[...]
        ov = o_ref[...].astype(F32)
        za = za_ref[...]
        do = dm * _silu(za)
        do_ref[...] = do.astype(BF16)
        dos_ref[...] = (do * LN2).astype(BF16)
        dza_ref[...] = (dm * ov * _dsilu(za)).astype(BF16)
        delta_ref[0] = jnp.sum(do * ov, axis=-1, keepdims=True) * LN2

    spec = pl.BlockSpec((tt, HEAD_DIM), lambda i, h: (i, h))
    shape = jax.ShapeDtypeStruct((n, ATTN_W), BF16)
    return pl.pallas_call(
        body, name=name, grid=(n // tt, N_Q_HEADS),
        in_specs=[spec, spec, pl.BlockSpec((tt, HEAD_DIM), lambda i, h: (i, za_block + h))],
        out_specs=[spec, spec, spec, pl.BlockSpec((1, tt, 1), lambda i, h: (h, i, 0))],
        out_shape=[shape, shape, shape, jax.ShapeDtypeStruct((N_Q_HEADS, n, 1), F32)],
        compiler_params=_params("parallel", "parallel"),
    )(dmix, o, p)


def flash_bwd(q, do, do_s, lse_row, delta_row, k_all, v_ext, *, name):
    n = q.shape[0]
    s_all = k_all.shape[0]
    tq = _rows_tile(n, 512)
    chunks = _kv_chunks(n, s_all)

    def body(q_ref, do_ref, dos_ref, lse_ref, dl_ref, k_ref, v_ref, dq_ref, dk_ref, dv_ref):
        g = pl.program_id(1)
        i = pl.program_id(2)

        @pl.when((g == 0) & (i == 0))
        def _():
            dk_ref[...] = jnp.zeros_like(dk_ref)
            dv_ref[...] = jnp.zeros_like(dv_ref)

        qv = q_ref[...]
        dov = do_ref[...]
        dosv = dos_ref[...]
        lse = lse_ref[0]
        dl = dl_ref[0]
        dq = jnp.zeros((tq, HEAD_DIM), F32)
        for start, size in chunks:
            kc = k_ref[pl.ds(start, size), :]
            vc = v_ref[pl.ds(start, size), :]
            st = lax.dot_general(kc, qv, _DIMS["nt"], preferred_element_type=F32)
            pt = jnp.exp2(st - lse)
            dpt = lax.dot_general(vc, dosv, _DIMS["nt"], preferred_element_type=F32)
            dst = (pt * (dpt - dl)).astype(BF16)
            dv_ref[pl.ds(start, size), :] += jnp.dot(pt.astype(BF16), dov, preferred_element_type=F32)
            dk_ref[pl.ds(start, size), :] += jnp.dot(dst, qv, preferred_element_type=F32)
            dq = dq + lax.dot_general(dst, kc, _DIMS["tn"], preferred_element_type=F32)
        dq_ref[...] = dq

    qspec = pl.BlockSpec((tq, HEAD_DIM), lambda kh, g, i: (i, kh * Q_PER_KV + g))
    rowspec = pl.BlockSpec((1, 1, tq), lambda kh, g, i: (kh * Q_PER_KV + g, 0, i))
    kvspec = pl.BlockSpec((s_all, HEAD_DIM), lambda kh, g, i: (0, kh))
    return pl.pallas_call(
        body, name=name, grid=(N_KV_HEADS, Q_PER_KV, n // tq),
        in_specs=[qspec, qspec, qspec, rowspec, rowspec, kvspec,
                  pl.BlockSpec((s_all, HEAD_DIM), lambda kh, g, i: (0, kh * (V_EXT_W // HEAD_DIM)))],
        out_specs=[qspec, kvspec, kvspec],
        out_shape=[jax.ShapeDtypeStruct((n, ATTN_W), F32),
                   jax.ShapeDtypeStruct((s_all, KV_W), F32), jax.ShapeDtypeStruct((s_all, KV_W), F32)],
        compiler_params=_params("arbitrary", "arbitrary", "arbitrary"),
    )(q, do, do_s, lse_row, delta_row, k_all, v_ext)


HALF = 512


def _tap_group(w_ref, cs, ext_ref, tt, r, tap_of):
    z = None
    for a in range(4):
        k = tap_of(8 * a + r)
        if 0 <= k < CONV_WIDTH:
            term = w_ref[k:k + 1, cs] * ext_ref[pl.ds(8 * a, tt + 8), :]
            z = term if z is None else z + term
    return z


def _shifted(z, z_ref, r, tt):
    if r == 0:
        return z[0:tt]
    z_ref[...] = z
    return z_ref[pl.ds(r, tt), :]


def _halo_specs(tt, n, cb):
    per = tt // HALO
    last = n // HALO - 1
    return [pl.BlockSpec((HALO, HALF), lambda i: (jnp.maximum(i * per - 1, 0), cb)),
            pl.BlockSpec((tt, HALF), lambda i: (i, cb)),
            pl.BlockSpec((HALO, HALF), lambda i: (jnp.minimum((i + 1) * per, last), cb))]


def conv_fwd(p, dw_w, dw_b, ln_g, ln_b, mix, *, glu_block, name):
    n = p.shape[0]
    ch = dw_w.shape[1]
    nh = ch // HALF
    tt = _rows_tile(n, 256)
    last = n // tt - 1

    def body(*refs):
        a_refs = [refs[3 * h:3 * h + 3] for h in range(nh)]
        b_refs = [refs[3 * (nh + h):3 * (nh + h) + 3] for h in range(nh)]
        pos = 6 * nh
        zb_refs = refs[pos:pos + nh]
        pos += nh
        w_ref, bias_ref, g_ref, be_ref, _, ycv_ref, mix_ref, ext_ref, z_ref = refs[pos:pos + 9]
        i = pl.program_id(0)
        for h in range(nh):
            cs = slice(h * HALF, (h + 1) * HALF)
            ap, am, an = a_refs[h]
            bp, bm, bn = b_refs[h]
            ext_ref[0:HALO, :] = jnp.where(i > 0, ap[...] * _sigmoid(bp[...]), 0.0)
            ext_ref[HALO:HALO + tt, :] = am[...] * _sigmoid(bm[...])
            ext_ref[HALO + tt:2 * HALO + tt, :] = jnp.where(i < last, an[...] * _sigmoid(bn[...]), 0.0)
            acc = jnp.broadcast_to(bias_ref[:, cs], (tt, HALF))
            for r in range(8):
                z = _tap_group(w_ref, cs, ext_ref, tt, r, lambda j: j - 1)
                acc = acc + _shifted(z, z_ref, r, tt)
            ycv_ref[:, cs] = acc
        yc = ycv_ref[...]
        mu = jnp.mean(yc, axis=-1, keepdims=True)
        var = jnp.mean(jnp.square(yc - mu), axis=-1, keepdims=True)
        ln = (yc - mu) * lax.rsqrt(var + EPS) * g_ref[...] + be_ref[...]
        out = _silu(ln)
        for h in range(nh):
            cs = slice(h * HALF, (h + 1) * HALF)
            mix_ref[:, cs] = (out[:, cs] * _silu(zb_refs[h][...])).astype(BF16)

    in_specs = []
    for h in range(2 * nh):
        in_specs += _halo_specs(tt, n, glu_block + h)
    in_specs += [pl.BlockSpec((tt, HALF), functools.partial(lambda i, cb: (i, cb), cb=glu_block + 2 * nh + h))
                 for h in range(nh)]
    vec = pl.BlockSpec((1, ch), lambda i: (0, 0))
    in_specs += [pl.BlockSpec((CONV_WIDTH, ch), lambda i: (0, 0)), vec, vec, vec, ANY]
    ops = [p] * (6 * nh + nh) + [dw_w, _row(dw_b), _row(ln_g), _row(ln_b), mix]
    big = pl.BlockSpec((tt, ch), lambda i: (i, 0))
    mix_block = (mix.shape[1] - ch) // ch
    return pl.pallas_call(
        body, name=name, grid=(n // tt,), in_specs=in_specs,
        out_specs=[big, pl.BlockSpec((tt, ch), lambda i: (i, mix_block))],
        out_shape=[jax.ShapeDtypeStruct((n, ch), F32), jax.ShapeDtypeStruct(mix.shape, BF16)],
        input_output_aliases={len(ops) - 1: 1},
        scratch_shapes=[pltpu.VMEM((tt + 2 * HALO, HALF), F32), pltpu.VMEM((tt + 8, HALF), F32)],
        compiler_params=_params("parallel"),
    )(*ops)


def conv_bwd_rows(dmix, ycv, p, ln_g, ln_b, *, mix_block, zb_block, name):
    n, ch = ycv.shape
    nh = ch // HALF
    tt = _rows_tile(n, 256)

    def body(*refs):
        dm_ref, ycv_ref = refs[:2]
        zb_refs = refs[2:2 + nh]
        g_ref, be_ref, dy_ref, dzb_ref, acc_ref = refs[2 + nh:]
        i = pl.program_id(0)

        @pl.when(i == 0)
        def _():
            acc_ref[...] = jnp.zeros_like(acc_ref)

        yc = ycv_ref[...]
        mu = jnp.mean(yc, axis=-1, keepdims=True)
        var = jnp.mean(jnp.square(yc - mu), axis=-1, keepdims=True)
        rstd = lax.rsqrt(var + EPS)
        xhat = (yc - mu) * rstd
        ln = xhat * g_ref[...] + be_ref[...]
        out = _silu(ln)
        dm = dm_ref[...]
        zb = jnp.concatenate([r[...] for r in zb_refs], axis=-1)
        dzb_ref[...] = (dm * out * _dsilu(zb)).astype(BF16)
        dln = dm * _silu(zb) * _dsilu(ln)
        acc_ref[0:1, :] += jnp.sum(dln * xhat, axis=0, keepdims=True)
        acc_ref[1:2, :] += jnp.sum(dln, axis=0, keepdims=True)
        dxhat = dln * g_ref[...]
        dy_ref[...] = rstd * (dxhat - jnp.mean(dxhat, axis=-1, keepdims=True)
                              - xhat * jnp.mean(dxhat * xhat, axis=-1, keepdims=True))

    big = pl.BlockSpec((tt, ch), lambda i: (i, 0))
    vec = pl.BlockSpec((1, ch), lambda i: (0, 0))
    in_specs = [pl.BlockSpec((tt, ch), lambda i: (i, mix_block)), big]
    in_specs += [pl.BlockSpec((tt, HALF), functools.partial(lambda i, cb: (i, cb), cb=zb_block + h)) for h in range(nh)]
    in_specs += [vec, vec]
    return pl.pallas_call(
        body, name=name, grid=(n // tt,), in_specs=in_specs,
        out_specs=[big, big, pl.BlockSpec((8, ch), lambda i: (0, 0))],
        out_shape=[jax.ShapeDtypeStruct((n, ch), F32), jax.ShapeDtypeStruct((n, ch), BF16),
                   jax.ShapeDtypeStruct((8, ch), F32)],
        compiler_params=_params("arbitrary"),
    )(dmix, ycv, *([p] * nh), _row(ln_g), _row(ln_b))


def conv_bwd_taps(dycv, p, dw_w, *, glu_block, name):
    n, ch = dycv.shape
    nh = ch // HALF
    tt = _rows_tile(n, 256)
    last = n // tt - 1

    def body(*refs):
        d_refs = [refs[3 * h:3 * h + 3] for h in range(nh)]
        a_refs = [refs[3 * (nh + h):3 * (nh + h) + 3] for h in range(nh)]
        b_refs = [refs[3 * (2 * nh + h):3 * (2 * nh + h) + 3] for h in range(nh)]
        w_ref, dglu_ref, dw_ref, db_ref, yext_ref, dext_ref, z_ref = refs[9 * nh:]
        i = pl.program_id(0)

        @pl.when(i == 0)
        def _():
            dw_ref[...] = jnp.zeros_like(dw_ref)
            db_ref[...] = jnp.zeros_like(db_ref)

        for h in range(nh):
            cs = slice(h * HALF, (h + 1) * HALF)
            ap, am, an = a_refs[h]
            bp, bm, bn = b_refs[h]
            dp, dm, dn = d_refs[h]
            av = am[...]
            sb = _sigmoid(bm[...])
            yext_ref[0:HALO, :] = jnp.where(i > 0, ap[...] * _sigmoid(bp[...]), 0.0)
            yext_ref[HALO:HALO + tt, :] = av * sb
            yext_ref[HALO + tt:2 * HALO + tt, :] = jnp.where(i < last, an[...] * _sigmoid(bn[...]), 0.0)
            dmain = dm[...]
            dext_ref[0:HALO, :] = jnp.where(i > 0, dp[...], 0.0)
            dext_ref[HALO:HALO + tt, :] = dmain
            dext_ref[HALO + tt:2 * HALO + tt, :] = jnp.where(i < last, dn[...], 0.0)
            dy = jnp.zeros((tt, HALF), F32)
            for r in range(8):
                z = _tap_group(w_ref, cs, dext_ref, tt, r, lambda j: CONV_WIDTH - j)
                dy = dy + _shifted(z, z_ref, r, tt)
                y_r = yext_ref[pl.ds(r, tt + 24), :]
                for a in range(4):
                    k = 8 * a + r - 1
                    if k >= 0:
                        dw_ref[k:k + 1, cs] += jnp.sum(dmain * y_r[8 * a:8 * a + tt], axis=0, keepdims=True)
            db_ref[0:1, cs] += jnp.sum(dmain, axis=0, keepdims=True)
            dglu_ref[:, cs] = (dy * sb).astype(BF16)
            dglu_ref[:, ch + h * HALF:ch + (h + 1) * HALF] = (dy * av * sb * (1.0 - sb)).astype(BF16)

    in_specs = []
    for h in range(nh):
        in_specs += _halo_specs(tt, n, h)
    for h in range(2 * nh):
        in_specs += _halo_specs(tt, n, glu_block + h)
    in_specs += [pl.BlockSpec((CONV_WIDTH, ch), lambda i: (0, 0))]
    ops = [dycv] * (3 * nh) + [p] * (6 * nh) + [dw_w]
    return pl.pallas_call(
        body, name=name, grid=(n // tt,), in_specs=in_specs,
        out_specs=[pl.BlockSpec((tt, 2 * ch), lambda i: (i, 0)), pl.BlockSpec((32, ch), lambda i: (0, 0)),
                   pl.BlockSpec((8, ch), lambda i: (0, 0))],
        out_shape=[jax.ShapeDtypeStruct((n, 2 * ch), BF16), jax.ShapeDtypeStruct((32, ch), F32),
                   jax.ShapeDtypeStruct((8, ch), F32)],
        scratch_shapes=[pltpu.VMEM((tt + 2 * HALO, HALF), F32), pltpu.VMEM((tt + 2 * HALO, HALF), F32),
                        pltpu.VMEM((tt + 8, HALF), F32)],
        compiler_params=_params("arbitrary"),
    )(*ops)


def _sgu_common(p_ref, g_ref, be_ref, ws_ref, bs_ref, w):
    gw = w // SGU_GROUPS
    u_pre = p_ref[:, 0:w]
    v_pre = p_ref[:, w:2 * w]
    zc = p_ref[:, 2 * w:3 * w]
    u = _gelu(u_pre)
    v = _gelu(v_pre)
    mu = jnp.mean(v, axis=-1, keepdims=True)
    var = jnp.mean(jnp.square(v - mu), axis=-1, keepdims=True)
    rstd = lax.rsqrt(var + EPS)
    vhat = (v - mu) * rstd
    vn = (vhat * g_ref[...] + be_ref[...]).astype(BF16)
    mixed = jnp.concatenate(
        [jnp.dot(ws_ref[gi].astype(BF16), vn[:, gi * gw:(gi + 1) * gw], preferred_element_type=F32)
         + bs_ref[:, gi:gi + 1] for gi in range(SGU_GROUPS)], axis=-1)
    return u_pre, v_pre, zc, u, rstd, vhat, vn, mixed


def sgu_fwd(p, ln_g, ln_b, ws, bs_t, *, name):
    n, w3 = p.shape
    w = w3 // 3

    def body(p_ref, g_ref, be_ref, ws_ref, bs_ref, m_ref):
        _, _, zc, u, _, _, _, mixed = _sgu_common(p_ref, g_ref, be_ref, ws_ref, bs_ref, w)
        m_ref[...] = (u * mixed * _silu(zc)).astype(BF16)

    vec = pl.BlockSpec((1, w), lambda i: (0, 0))
    return pl.pallas_call(
        body, name=name, grid=(n // CHUNK,),
        in_specs=[pl.BlockSpec((CHUNK, w3), lambda i: (i, 0)), vec, vec,
                  pl.BlockSpec((SGU_GROUPS, CHUNK, CHUNK), lambda i: (0, 0, 0)),
                  pl.BlockSpec((CHUNK, SGU_GROUPS), lambda i: (0, 0))],
        out_specs=pl.BlockSpec((CHUNK, w), lambda i: (i, 0)),
        out_shape=jax.ShapeDtypeStruct((n, w), BF16),
        compiler_params=_params("parallel"),
    )(p, _row(ln_g), _row(ln_b), ws, bs_t)


def sgu_bwd(p, dm, ln_g, ln_b, ws, ws_t, bs_t, *, name):
    n, w3 = p.shape
    w = w3 // 3
    gw = w // SGU_GROUPS

    def body(p_ref, dm_ref, g_ref, be_ref, ws_ref, wst_ref, bs_ref, dp_ref, dws_ref, dbs_ref, acc_ref):
        i = pl.program_id(0)

        @pl.when(i == 0)
        def _():
            dws_ref[...] = jnp.zeros_like(dws_ref)
            dbs_ref[...] = jnp.zeros_like(dbs_ref)
            acc_ref[...] = jnp.zeros_like(acc_ref)

        u_pre, v_pre, zc, u, rstd, vhat, vn, mixed = _sgu_common(p_ref, g_ref, be_ref, ws_ref, bs_ref, w)
        dmv = dm_ref[...]
        um = u * mixed
        dp_ref[:, 2 * w:3 * w] = (dmv * um * _dsilu(zc)).astype(BF16)
        dum = dmv * _silu(zc)
        dp_ref[:, 0:w] = (dum * mixed * _dgelu(u_pre)).astype(BF16)
        dmixed = dum * u
        dmixed_b = dmixed.astype(BF16)
        dvn_parts = []
        for gi in range(SGU_GROUPS):
            cs = slice(gi * gw, (gi + 1) * gw)
            dws_ref[gi] += lax.dot_general(dmixed_b[:, cs], vn[:, cs], _DIMS["nt"], preferred_element_type=F32)
            dbs_ref[:, gi:gi + 1] += jnp.sum(dmixed[:, cs], axis=-1, keepdims=True)
            dvn_parts.append(jnp.dot(wst_ref[gi].astype(BF16), dmixed_b[:, cs], preferred_element_type=F32))
        dvn = jnp.concatenate(dvn_parts, axis=-1)
        acc_ref[0:1, :] += jnp.sum(dvn * vhat, axis=0, keepdims=True)
        acc_ref[1:2, :] += jnp.sum(dvn, axis=0, keepdims=True)
        dvhat = dvn * g_ref[...]
        dv = rstd * (dvhat - jnp.mean(dvhat, axis=-1, keepdims=True)
                     - vhat * jnp.mean(dvhat * vhat, axis=-1, keepdims=True))
        dp_ref[:, w:2 * w] = (dv * _dgelu(v_pre)).astype(BF16)

    vec = pl.BlockSpec((1, w), lambda i: (0, 0))
    wspec = pl.BlockSpec((SGU_GROUPS, CHUNK, CHUNK), lambda i: (0, 0, 0))
    bspec = pl.BlockSpec((CHUNK, SGU_GROUPS), lambda i: (0, 0))
    return pl.pallas_call(
        body, name=name, grid=(n // CHUNK,),
        in_specs=[pl.BlockSpec((CHUNK, w3), lambda i: (i, 0)), pl.BlockSpec((CHUNK, w), lambda i: (i, 0)),
                  vec, vec, wspec, wspec, bspec],
        out_specs=[pl.BlockSpec((CHUNK, w3), lambda i: (i, 0)), wspec, bspec,
                   pl.BlockSpec((8, w), lambda i: (0, 0))],
        out_shape=[jax.ShapeDtypeStruct((n, w3), BF16), jax.ShapeDtypeStruct((SGU_GROUPS, CHUNK, CHUNK), F32),
                   jax.ShapeDtypeStruct((CHUNK, SGU_GROUPS), F32), jax.ShapeDtypeStruct((8, w), F32)],
        compiler_params=_params("arbitrary"),
    )(p, dm, _row(ln_g), _row(ln_b), ws, ws_t, bs_t)


def _adam_math(w, g, m, v):
    m_new = ADAM_B1 * m + (1.0 - ADAM_B1) * g
    v_new = ADAM_B2 * v + (1.0 - ADAM_B2) * (g * g)
    m_hat = m_new / (1.0 - ADAM_B1 ** ADAM_STEP)
    v_hat = v_new / (1.0 - ADAM_B2 ** ADAM_STEP)
    delta = -ADAM_LR * (m_hat / (jnp.sqrt(v_hat) + ADAM_EPS) + ADAM_WD * w)
    return delta, m_new, v_new


def adamw(w, g, m, v, *, name, slots=False, rows=512):
    r, c = w.shape
    tr = min(r, rows)
    assert r % tr == 0, (name, r, tr)

    def body(w_ref, g_ref, m_ref, v_ref, go_ref, d_ref, mo_ref, vo_ref):
        if slots:
            g = g_ref[0].astype(F32)
            for k in range(1, N_DEV):
                g = g + g_ref[k].astype(F32)
        else:
            g = g_ref[...].astype(F32)
        delta, m_new, v_new = _adam_math(w_ref[...], g, m_ref[...], v_ref[...])
        go_ref[...] = g
        d_ref[...] = delta
        mo_ref[...] = m_new
        vo_ref[...] = v_new

    spec = pl.BlockSpec((tr, c), lambda i: (i, 0))
    gspec = pl.BlockSpec((N_DEV, tr, c), lambda i: (0, i, 0)) if slots else spec
    shape = jax.ShapeDtypeStruct((r, c), F32)
    return pl.pallas_call(
        body, name=name, grid=(r // tr,),
        in_specs=[spec, gspec, spec, spec], out_specs=[spec] * 4, out_shape=[shape] * 4,
        compiler_params=_params("parallel"),
    )(w, g, m, v)


def _after(token, val):
    return val + token[0, 0]


def _pad_rows(a, rows):
    return jnp.pad(a, ((0, rows - a.shape[0]), (0, 0)))


def kernel(x, c, ctx, c_ctx, ada_w, ada_b, norm_g, ev_w_in, ev_q_norm, ev_k_norm, ev_dw_w, ev_dw_b, ev_ln_g, ev_ln_b, ev_w_out, od_w_in, od_ln_g, od_ln_b, od_ws, od_bs, od_w_out, final_g, loss_target, m_c_ctx, m_ada_w, m_ada_b, m_norm_g, m_ev_w_in, m_ev_q_norm, m_ev_k_norm, m_ev_dw_w, m_ev_dw_b, m_ev_ln_g, m_ev_ln_b, m_ev_w_out, m_od_w_in, m_od_ln_g, m_od_ln_b, m_od_ws, m_od_bs, m_od_w_out, m_final_g, v_c_ctx, v_ada_w, v_ada_b, v_norm_g, v_ev_w_in, v_ev_q_norm, v_ev_k_norm, v_ev_dw_w, v_ev_dw_b, v_ev_ln_g, v_ev_ln_b, v_ev_w_out, v_od_w_in, v_od_ln_g, v_od_ln_b, v_od_ws, v_od_bs, v_od_w_out, v_final_g):
    n, d = x.shape[1], x.shape[2]
    lc = ctx.shape[1]
    ev_in = ev_w_in.shape[2] * N_DEV
    od_in = od_w_in.shape[2] * N_DEV
    conv_ch = ev_dw_w.shape[2] * N_DEV
    ada_cols = ada_w.shape[2]
    me = 4 * lax.axis_index("x") + 2 * lax.axis_index("y") + lax.axis_index("c")
    xs, tgt, ctxs = x[0], loss_target[0], ctx[0]
    za_block = (2 * KV_W + ATTN_W) // HEAD_DIM
    glu_block = (2 * KV_W + 2 * ATTN_W) // HALF
    zb_block = glu_block + 2 * conv_ch // HALF

    small = jnp.concatenate([
        jax.nn.silu(c).reshape(1, d),
        od_ln_g.reshape(1, -1), od_ln_b.reshape(1, -1)], axis=1)
    small = _pad_rows(small, 8)
    dw_rows = _pad_rows(ev_dw_w[0], 32)
    small_g, dw_g = all_gather([small, dw_rows], name="gather_small", hbm=False)
    sc_all = small_g[:, 0, :d]
    shard = d // N_DEV
    od_ln_g_full = small_g[:, 0, d:d + shard].reshape(d)
    od_ln_b_full = small_g[:, 0, d + shard:d + 2 * shard].reshape(d)
    dw_w_full = jnp.moveaxis(dw_g, 0, 1).reshape(32, conv_ch)[:CONV_WIDTH]
    scc = jax.nn.silu(c_ctx)
    sc16 = _pad_rows(jnp.concatenate([sc_all, scc.reshape(1, d)], axis=0), 16)

    ada_bf = ada_w.astype(BF16)
    mod_loc = [matmul(sc16, ada_bf[l], mode="nn", tm=16, tn=ada_cols, tk=d, out_dtype=F32, name=f"ada_mod{l}")
               for l in range(2)]
    (mod_g,) = all_gather([jnp.stack(mod_loc)], name="gather_mod", hbm=False)
    mod_all = jnp.moveaxis(mod_g, 0, 2).reshape(2, 16, N_DEV * ada_cols) + ada_b[:, None, :]
    mod_me = lax.dynamic_index_in_dim(mod_all, me, axis=1, keepdims=False)
    shift = [mod_me[l, :d] for l in range(2)]
    scale = [mod_me[l, d:2 * d] for l in range(2)]
    gate = [mod_me[l, 2 * d:] for l in range(2)]
    shift_c, scale_c = mod_all[0, 8, :d], mod_all[0, 8, d:2 * d]

    (wi0_g,) = all_gather([ev_w_in[0].astype(BF16)], name="gather_w_in0", hbm=True)
    wi0 = jnp.moveaxis(wi0_g, 0, 1).reshape(d, ev_in)
    bits = lax.bitcast_convert_type(wi0_g[0, 0, 0], jnp.uint16)
    landed_zero = jnp.where((bits | 1) == 0, 1.0, 0.0).astype(F32)
    later = [(w[0] + landed_zero).astype(BF16) for w in (ev_w_out, od_w_in, od_w_out)]
    (h_wo0, h_wi1, h_wo1), w_token = exchange_start(later, scatter=False, name="gather_rest_start")

    def landed(handle, after, name):
        own, land = exchange_wait(handle, after, scatter=False, name=name)
        return lax.dynamic_update_slice_in_dim(land, own[None], me, axis=0)

    cexp, sexp = rope_tables(n)

    shift0 = _after(w_token, shift[0])
    h0 = norm_mod_fwd(xs, norm_g[0], shift0, scale[0], name="norm_mod_fwd0")
    hc = norm_mod_fwd(ctxs, norm_g[0], shift_c, scale_c, name="norm_mod_fwd_ctx")
    p0 = matmul(h0, wi0, mode="nn", tm=1024, tn=ev_in // 4, tk=d, out_dtype=F32, name="proj_in0")
    pc = matmul(hc, wi0, mode="nn", tm=lc, tn=2 * KV_W, tk=d, out_dtype=F32, n_out=2 * KV_W, name="proj_in_ctx")
    q_r, k_all, v_ext = qkv_prep_fwd(p0, ev_q_norm[0], ev_k_norm[0], cexp, sexp, latent=True, name="qkv_prep",
                                     rows_all=n + lc)
    k_all, v_ext = qkv_prep_fwd(pc, None, ev_k_norm[0], None, None, latent=False, name="kv_prep_ctx",
                                rows_all=n + lc, k_all=k_all, v_ext=v_ext)
    o_attn, mix_a, lse = flash_fwd(q_r, k_all, v_ext, p0, za_block=za_block, mix_width=ATTN_W + conv_ch,
                                   name="flash_fwd")
    ycv, mix0 = conv_fwd(p0, dw_w_full, ev_dw_b[0], ev_ln_g[0], ev_ln_b[0], mix_a, glu_block=glu_block,
                         name="conv_fwd")
    wo0 = landed(h_wo0, ycv, "gather_w_out0_wait").reshape(-1, d)
    x1, r0 = matmul(mix0, wo0, mode="nn", tm=1024, tn=1024, tk=mix0.shape[1], out_dtype=F32, name="proj_out0",
                    res=xs, gate=_row(gate[0]))

    h1 = norm_mod_fwd(x1, norm_g[1], shift[1], scale[1], name="norm_mod_fwd1")
    wi1 = jnp.moveaxis(landed(h_wi1, h1, "gather_w_in1_wait"), 0, 1).reshape(d, od_in)
    p1 = matmul(h1, wi1, mode="nn", tm=1024, tn=od_in // 4, tk=d, out_dtype=F32, name="proj_in1")
    ws_bf = od_ws[0]
    bs_t = od_bs[0].T
    m1 = sgu_fwd(p1, od_ln_g_full, od_ln_b_full, ws_bf, bs_t, name="sgu_fwd")
    wo1 = landed(h_wo1, m1, "gather_w_out1_wait").reshape(-1, d)
    x2, r1 = matmul(m1, wo1, mode="nn", tm=1024, tn=1024, tk=m1.shape[1], out_dtype=F32, name="proj_out1",
                    res=x1, gate=_row(gate[1]))

    dx2, dr1, acc_final, loss_tile = final_loss(x2, tgt, final_g, r1, gate[1], name="final_loss")

    dm1 = matmul(dr1, wo1, mode="nt", tm=1024, tn=1024, tk=d, out_dtype=F32, name="d_mix1")
    dwo1 = matmul(m1, dr1, mode="tn", tm=512, tn=512, tk=4096,out_dtype=BF16, name="d_wout1")
    dp1, dws, dbs_t, acc_sgu = sgu_bwd(p1, dm1, od_ln_g_full, od_ln_b_full, ws_bf, jnp.swapaxes(ws_bf, 1, 2), bs_t,
                                       name="sgu_bwd")
    dwi1_s = matmul(h1, dp1, mode="tn", tm=512, tn=od_in // N_DEV, tk=4096, out_dtype=BF16, name="d_win1",
                    split_out=True)
    dwo1_s = dwo1.reshape(N_DEV, -1, d)
    (h_gi1, h_go1), g1_token = exchange_start([dwi1_s, dwo1_s], scatter=True, name="grads1_start")
    dh1 = matmul(dp1, wi1, mode="nt", tm=1024, tn=512, tk=od_in, out_dtype=F32, name="d_h1")
    dx1, dr0, acc_norm1 = norm_mod_bwd(x1, dh1, dx2, norm_g[1], _after(g1_token, scale[1]), name="norm_mod_bwd1",
                                       branch=(r0, gate[0]))

    dmix0 = matmul(dr0, wo0, mode="nt", tm=1024, tn=1024, tk=d, out_dtype=F32, name="d_mix0")
    dwo0 = matmul(mix0, dr0, mode="tn", tm=512, tn=512, tk=4096,out_dtype=BF16, name="d_wout0")
    do_attn, do_s, dza, delta = attn_gate_bwd(dmix0, o_attn, p0, za_block=za_block, name="attn_gate_bwd")
    dycv, dzb, acc_ln = conv_bwd_rows(dmix0, ycv, p0, ev_ln_g[0], ev_ln_b[0], mix_block=ATTN_W // conv_ch,
                                      zb_block=zb_block, name="conv_bwd_rows")
    dglu, ddw_w, acc_dwb = conv_bwd_taps(dycv, p0, dw_w_full, glu_block=glu_block, name="conv_bwd_taps")
    lse_row = lse.reshape(N_Q_HEADS, 1, n)
    delta_row = delta.reshape(N_Q_HEADS, 1, n)
    dq_r, dk_all, dv_all = flash_bwd(q_r, do_attn, do_s, lse_row, delta_row, k_all, v_ext, name="flash_bwd")
    dkvq, acc_qk = qkv_prep_bwd(p0, dq_r, dk_all[:n], dv_all[:n], ev_q_norm[0], ev_k_norm[0], cexp, sexp,
                                latent=True, name="qkv_prep_bwd")
    dpc, acc_kc = qkv_prep_bwd(pc, None, dk_all[n:], dv_all[n:], None, ev_k_norm[0], None, None,
                               latent=False, name="kv_prep_ctx_bwd")
    dp0 = jnp.concatenate([dkvq, dza, dglu, dzb], axis=1)
    dwi0 = matmul(h0, dp0, mode="tn", tm=512, tn=512, tk=4096,out_dtype=F32, name="d_win0")
    dwi0 = matmul(hc, dpc, mode="tn", tm=512, tn=2 * KV_W, tk=lc, out_dtype=F32, name="d_win0_ctx", add=dwi0)
    dwi0_s = jnp.moveaxis(dwi0.astype(BF16).reshape(d, N_DEV, ev_in // N_DEV), 1, 0)
    dwo0_s = dwo0.reshape(N_DEV, -1, d)
    (h_gi0, h_go0), g0_token = exchange_start([dwi0_s, dwo0_s], scatter=True, name="grads0_start")
    dh0 = matmul(dp0, wi0, mode="nt", tm=1024, tn=512, tk=ev_in, out_dtype=F32, name="d_h0", after=g0_token)
    dhc = matmul(dpc, wi0, mode="nt", tm=lc, tn=512, tk=2 * KV_W, out_dtype=F32, name="d_h_ctx")
    grad_x, acc_norm0 = norm_mod_bwd(xs, dh0, dx1, norm_g[0], _after(g0_token, scale[0]), name="norm_mod_bwd0")
    _, acc_normc = norm_mod_bwd(ctxs, dhc, None, norm_g[0], scale_c, name="norm_mod_bwd_ctx")

    zeros_d = jnp.zeros((d,), F32)
    dmod0 = jnp.stack([acc_norm0[0], acc_norm0[1], acc_norm1[GATE_ROW]])
    dmod1 = jnp.stack([acc_norm1[0], acc_norm1[1], acc_final[GATE_ROW]])
    dmodc = jnp.stack([acc_normc[0], acc_normc[1]])
    half_pad = jnp.zeros((d - 2 * conv_ch,), F32) if d > 2 * conv_ch else jnp.zeros((0,), F32)
    row_a = jnp.concatenate([acc_dwb[0], acc_ln[0], half_pad])
    row_b = jnp.concatenate([acc_ln[1], acc_qk[0], acc_qk[1] + acc_kc[1],
                             jnp.zeros((d - conv_ch - 2 * HEAD_DIM,), F32)])
    row_c = jnp.concatenate([dbs_t.T.reshape(-1), jnp.zeros((d - SGU_GROUPS * CHUNK,), F32)])
    row_loss = jnp.concatenate([loss_tile[0, :1], jnp.zeros((d - 1,), F32)])
    pack = jnp.concatenate([
        dmod0, dmod1, dmodc,
        (acc_norm0[2] + acc_normc[2])[None], acc_norm1[2][None],
        acc_final[0][None],
        acc_sgu[0][None], acc_sgu[1][None],
        row_a[None], row_b[None], row_c[None], row_loss[None],
        ddw_w.reshape(-1, d),
        dws.reshape(-1, d),
    ], axis=0)
    n_rows = pack.shape[0]
    pack = _pad_rows(pack, -(-n_rows // 8) * 8)
    pack_g, pack_sum = all_gather([pack], name="gather_small_grads", hbm=False, sum_out=True)
    gsum = pack_sum
    loss = gsum[16, 0]
    dw_rows_n = 32 * conv_ch // d
    g_dw_w = gsum[17:17 + dw_rows_n].reshape(32, conv_ch)[:CONV_WIDTH]
    g_od_ws = gsum[17 + dw_rows_n:17 + dw_rows_n + SGU_GROUPS * CHUNK * CHUNK // d].reshape(od_ws.shape)

    dmodc_sum = jnp.concatenate([gsum[6], gsum[7], zeros_d])
    col0 = me * ada_cols
    dm_cols = []
    for l in range(2):
        rows = pack_g[:, 3 * l:3 * l + 3, :].reshape(N_DEV, 3 * d)
        extra = dmodc_sum[None] if l == 0 else jnp.zeros((1, 3 * d), F32)
        full = _pad_rows(jnp.concatenate([rows, extra], axis=0), 16)
        dm_cols.append(lax.dynamic_slice_in_dim(full, col0, ada_cols, axis=1))
    g_ada_w = jnp.stack([matmul(sc16, dm_cols[l], mode="tn", tm=512, tn=ada_cols, tk=16, out_dtype=F32,
                                name=f"d_ada_w{l}") for l in range(2)])
    dsc = matmul(dm_cols[0], ada_bf[0], mode="nt", tm=16, tn=512, tk=ada_cols, out_dtype=F32, name="d_scc")
    (_, dscc_sum) = all_gather([dsc[8:16]], name="gather_dscc", hbm=False, sum_out=True)
    sg = jax.nn.sigmoid(c_ctx)
    g_c_ctx = dscc_sum[0] * (sg * (1.0 + c_ctx * (1.0 - sg)))
    g_ada_b = jnp.stack([gsum[0:3].reshape(-1) + dmodc_sum, gsum[3:6].reshape(-1)])

    def summands(handle, after, name):
        mine, land = exchange_wait(handle, after, scatter=True, name=name)
        own = lax.dynamic_index_in_dim(mine, me, axis=0, keepdims=True)
        return lax.dynamic_update_slice_in_dim(land, own, me, axis=0)

    out = {}

    def upd(key, w, g, m, v, slots=False, rows=512):
        shp = w.shape
        w2 = w.reshape(-1, shp[-1])
        g2 = g.reshape((N_DEV, -1, shp[-1])) if slots else g.reshape(-1, shp[-1])
        res = adamw(w2, g2, m.reshape(w2.shape), v.reshape(w2.shape), name="adamw_" + key, slots=slots, rows=rows)
        out[key] = tuple(r.reshape(shp) for r in res)

    upd("ada_w", ada_w, g_ada_w, m_ada_w, v_ada_w)
    gi1 = summands(h_gi1,out["ada_w"][1], "grads_w_in1_wait")
    upd("od_w_in", od_w_in, gi1, m_od_w_in, v_od_w_in, slots=True, rows=256)
    go1 = summands(h_go1,out["od_w_in"][1], "grads_w_out1_wait")
    upd("od_w_out", od_w_out, go1, m_od_w_out, v_od_w_out, slots=True, rows=256)
    go0 = summands(h_go0,out["od_w_out"][1], "grads_w_out0_wait")
    upd("ev_w_out", ev_w_out, go0, m_ev_w_out, v_ev_w_out, slots=True, rows=256)
    gi0 = summands(h_gi0,out["ev_w_out"][1], "grads_w_in0_wait")
    upd("ev_w_in", ev_w_in, gi0, m_ev_w_in, v_ev_w_in, slots=True, rows=256)

    def my_shard(full, size):
        return lax.dynamic_slice_in_dim(full, me * size, size, axis=full.ndim - 1)

    small_items = [
        ("c_ctx", c_ctx, g_c_ctx, m_c_ctx, v_c_ctx),
        ("ada_b", ada_b, g_ada_b, m_ada_b, v_ada_b),
        ("norm_g", norm_g, gsum[8:10], m_norm_g, v_norm_g),
        ("ev_q_norm", ev_q_norm, gsum[14, conv_ch:conv_ch + HEAD_DIM], m_ev_q_norm, v_ev_q_norm),
        ("ev_k_norm", ev_k_norm, gsum[14, conv_ch + HEAD_DIM:conv_ch + 2 * HEAD_DIM], m_ev_k_norm, v_ev_k_norm),
        ("ev_dw_w", ev_dw_w, my_shard(g_dw_w, conv_ch // N_DEV), m_ev_dw_w, v_ev_dw_w),
        ("ev_dw_b", ev_dw_b, gsum[13, :conv_ch], m_ev_dw_b, v_ev_dw_b),
        ("ev_ln_g", ev_ln_g, gsum[13, conv_ch:2 * conv_ch], m_ev_ln_g, v_ev_ln_g),
        ("ev_ln_b", ev_ln_b, gsum[14, :conv_ch], m_ev_ln_b, v_ev_ln_b),
        ("od_ln_g", od_ln_g, my_shard(gsum[11], shard), m_od_ln_g, v_od_ln_g),
        ("od_ln_b", od_ln_b, my_shard(gsum[12], shard), m_od_ln_b, v_od_ln_b),
        ("od_ws", od_ws, g_od_ws, m_od_ws, v_od_ws),
        ("od_bs", od_bs, gsum[15, :SGU_GROUPS * CHUNK], m_od_bs, v_od_bs),
        ("final_g", final_g, gsum[10], m_final_g, v_final_g),
    ]
    sizes = [it[1].size for it in small_items]
    total = sum(sizes)
    lanes = 1024
    prow = -(-total // lanes)
    prow = -(-prow // 8) * 8

    def pack_small(idx):
        flat = jnp.concatenate([it[idx].reshape(-1).astype(F32) for it in small_items])
        return jnp.pad(flat, (0, prow * lanes - total)).reshape(prow, lanes)

    sres = adamw(pack_small(1), pack_small(2), pack_small(3), pack_small(4), name="adamw_small", rows=prow)
    off = 0
    for it, size in zip(small_items, sizes):
        out[it[0]] = tuple(r.reshape(-1)[off:off + size].reshape(it[1].shape) for r in sres)
        off += size

    names = ['c_ctx', 'ada_w', 'ada_b', 'norm_g', 'ev_w_in', 'ev_q_norm', 'ev_k_norm', 'ev_dw_w', 'ev_dw_b',
             'ev_ln_g', 'ev_ln_b', 'ev_w_out', 'od_w_in', 'od_ln_g', 'od_ln_b', 'od_ws', 'od_bs', 'od_w_out',
             'final_g']
    return (loss, grad_x[None], *[out[k][0] for k in names], *[out[k][1] for k in names],
            *[out[k][2] for k in names], *[out[k][3] for k in names])
```

```python
import functools
import math

import jax
import jax.numpy as jnp
from jax import lax
from jax.experimental import pallas as pl
from jax.experimental.pallas import tpu as pltpu

F32 = jnp.float32
BF16 = jnp.bfloat16
MESH = pl.DeviceIdType.MESH

EPS = 1e-6
HEAD_DIM = 128
N_Q_HEADS = 8
N_KV_HEADS = 2
Q_PER_KV = N_Q_HEADS // N_KV_HEADS
ATTN_W = N_Q_HEADS * HEAD_DIM
KV_W = N_KV_HEADS * HEAD_DIM
ATTN_SCALE = HEAD_DIM ** -0.5
LN2 = math.log(2.0)
Q_SCALE = ATTN_SCALE / LN2
ROPE_THETA = 10000.0
GRID_W = 64
CONV_WIDTH = 31
CONV_HALF = CONV_WIDTH // 2
HALO = 16
CHUNK = 128
SGU_GROUPS = 8
N_DEV = 8

ADAM_LR = 0.001
ADAM_B1 = 0.9
ADAM_B2 = 0.999
ADAM_EPS = 1e-08
ADAM_WD = 0.01
ADAM_STEP = 10

VMEM_LIMIT = 56 * 1024 * 1024
ANY = pl.BlockSpec(memory_space=pl.ANY)
VMEM_SPEC = pl.BlockSpec(memory_space=pltpu.VMEM)


def _params(*sem):
    return pltpu.CompilerParams(dimension_semantics=sem, vmem_limit_bytes=VMEM_LIMIT)


def _sigmoid(x):
    return 1.0 / (1.0 + jnp.exp(-x))


def _silu(x):
    return x * _sigmoid(x)


def _dsilu(x):
    s = _sigmoid(x)
    return s * (1.0 + x * (1.0 - s))


_GELU_C = math.sqrt(2.0 / math.pi)


def _gelu(x):
    t = jnp.tanh(_GELU_C * (x + 0.044715 * (x * x * x)))
    return 0.5 * x * (1.0 + t)


def _dgelu(x):
    t = jnp.tanh(_GELU_C * (x + 0.044715 * (x * x * x)))
    return 0.5 * (1.0 + t) + 0.5 * x * (1.0 - t * t) * (_GELU_C * (1.0 + 3.0 * 0.044715 * (x * x)))


def _row(v):
    return v.reshape(1, -1).astype(F32)


def _flat_id(p):
    return 4 * p[0] + 2 * p[1] + p[2]


def _gather_body(n_arr, sum_out):
    def body(*refs):
        x_refs = refs[:n_arr]
        out_refs = refs[n_arr:2 * n_arr]
        pos = 2 * n_arr
        sum_refs = refs[pos:pos + n_arr] if sum_out else ()
        pos += n_arr if sum_out else 0
        send_sems, recv_sems, local_sems = refs[pos:pos + 3]
        x, y, c = lax.axis_index("x"), lax.axis_index("y"), lax.axis_index("c")
        me, sibling = (x, y, c), (x, y, 1 - c)
        chips = [(1 - x, y), (x, 1 - y), (1 - x, 1 - y)]

        def copy(a, k, block, to, src=None):
            rows = out_refs[a].at[_flat_id(block)]
            return pltpu.make_async_remote_copy(
                src_ref=rows if src is None else src, dst_ref=rows,
                send_sem=send_sems.at[a, k], recv_sem=recv_sems.at[a, k],
                device_id=to, device_id_type=MESH)

        sends = []
        mine = []
        for a in range(n_arr):
            cp = pltpu.make_async_copy(x_refs[a], out_refs[a].at[_flat_id(me)], local_sems.at[a])
            cp.start()
            mine.append(cp)
            first = [copy(a, 0, me, sibling, src=x_refs[a])]
            first += [copy(a, 1 + j, me, (*chip, c), src=x_refs[a]) for j, chip in enumerate(chips)]
            for cp in first:
                cp.start()
            sends += first
        for a in range(n_arr):
            for j, chip in enumerate(chips):
                copy(a, 1 + j, (*chip, c), me).wait_recv()
                fwd = copy(a, 4 + j, (*chip, c), sibling)
                fwd.start()
                sends.append(fwd)
        for a in range(n_arr):
            copy(a, 0, sibling, me).wait_recv()
            for j, chip in enumerate(chips):
                copy(a, 4 + j, (*chip, 1 - c), me).wait_recv()
        for cp in sends:
            cp.wait_send()
        for cp in mine:
            cp.wait()
        if sum_out:
            for a in range(n_arr):
                acc = out_refs[a][0]
                for k in range(1, N_DEV):
                    acc = acc + out_refs[a][k]
                sum_refs[a][...] = acc

    return body


def all_gather(arrs, *, name, hbm, sum_out=False):
    n = len(arrs)
    spec = ANY if hbm else VMEM_SPEC
    out_shape = [jax.ShapeDtypeStruct((N_DEV,) + a.shape, a.dtype) for a in arrs]
    out_specs = [spec] * n
    if sum_out:
        out_shape += [jax.ShapeDtypeStruct(a.shape, a.dtype) for a in arrs]
        out_specs += [VMEM_SPEC] * n
    res = pl.pallas_call(
        _gather_body(n, sum_out), name=name,
        out_shape=out_shape, in_specs=[spec] * n, out_specs=out_specs,
        scratch_shapes=[pltpu.SemaphoreType.DMA((n, 7)), pltpu.SemaphoreType.DMA((n, 7)),
                        pltpu.SemaphoreType.DMA((n,))],
        compiler_params=pltpu.CompilerParams(vmem_limit_bytes=VMEM_LIMIT),
    )(*arrs)
    return res


_RELATIONS = [(rx, ry, rc) for rx in (0, 1) for ry in (0, 1) for rc in (0, 1)][1:]


HBM_SPEC = pl.BlockSpec(memory_space=pltpu.HBM)
SEM_SPEC = pl.BlockSpec(memory_space=pltpu.SEMAPHORE)
_DATAFLOW = pltpu.SideEffectType.DATAFLOW_SIDE_EFFECTING
N_PEERS = N_DEV - 1


def _peer_copy(src_ref, land_ref, send_sem, recv_sem, k, rel, scatter, sending):
    x, y, c = lax.axis_index("x"), lax.axis_index("y"), lax.axis_index("c")
    rx, ry, rc = rel
    peer = (1 - x if rx else x, 1 - y if ry else y, 1 - c if rc else c)
    src = src_ref.at[_flat_id(peer)] if scatter else src_ref
    dst = land_ref.at[_flat_id((x, y, c)) if sending else _flat_id(peer)]
    return pltpu.make_async_remote_copy(src_ref=src, dst_ref=dst, send_sem=send_sem.at[k], recv_sem=recv_sem.at[k],
                                        device_id=peer, device_id_type=MESH)


def exchange_start(arrs, *, scatter, name):
    n = len(arrs)
    lands = [lax.empty((N_DEV,) + (a.shape[1:] if scatter else a.shape), a.dtype) for a in arrs]

    def body(*refs):
        srcs, lnds, sems = refs[:n], refs[n:2 * n], refs[2 * n:4 * n]
        token = refs[6 * n]
        for a in range(n):
            for k, rel in enumerate(_RELATIONS):
                _peer_copy(srcs[a], lnds[a], sems[2 * a], sems[2 * a + 1], k, rel, scatter, True).start()
        token[...] = jnp.zeros_like(token)

    outs = pl.pallas_call(
        body, name=name,
        out_shape=[pltpu.SemaphoreType.DMA((N_PEERS,))] * (2 * n)
        + [pltpu.HBM(a.shape, a.dtype) for a in arrs] + [pltpu.HBM(l.shape, l.dtype) for l in lands]
        + [jax.ShapeDtypeStruct((8, 128), F32)],
        in_specs=[HBM_SPEC] * (2 * n),
        out_specs=[SEM_SPEC] * (2 * n) + [HBM_SPEC] * (2 * n) + [VMEM_SPEC],
        input_output_aliases={i: 2 * n + i for i in range(2 * n)},
        compiler_params=pltpu.CompilerParams(has_side_effects=_DATAFLOW),
    )(*[pltpu.with_memory_space_constraint(a, pltpu.HBM) for a in arrs],
      *[pltpu.with_memory_space_constraint(l, pltpu.HBM) for l in lands])
    handles = [(outs[2 * a], outs[2 * a + 1], outs[2 * n + a], outs[3 * n + a]) for a in range(n)]
    return handles, outs[4 * n]


def exchange_wait(handle, after, *, scatter, name):
    send_sem, recv_sem, src, land = handle

    def body(src_ref, land_ref, send_ref, recv_ref, after_ref, src_out, land_out):
        for k, rel in enumerate(_RELATIONS):
            cp = _peer_copy(src_ref, land_ref, send_ref, recv_ref, k, rel, scatter, False)
            cp.wait_send()
            cp.wait_recv()

    outs = pl.pallas_call(
        body, name=name,
        out_shape=[pltpu.HBM(src.shape, src.dtype), pltpu.HBM(land.shape, land.dtype)],
        in_specs=[HBM_SPEC, HBM_SPEC, SEM_SPEC, SEM_SPEC, ANY],
        out_specs=[HBM_SPEC, HBM_SPEC], input_output_aliases={0: 0, 1: 1},
        compiler_params=pltpu.CompilerParams(has_side_effects=_DATAFLOW),
    )(src, land, send_sem, recv_sem, after)
    return outs[0], outs[1]


_DIMS = {"nn": (((1,), (0,)), ((), ())), "nt": (((1,), (1,)), ((), ())), "tn": (((0,), (0,)), ((), ()))}


def matmul(a, b, *, mode, tm, tn, tk, out_dtype, name, n_out=None, res=None, gate=None, add=None, after=None,
           split_out=False):
    if mode == "tn":
        kdim, m = a.shape
    else:
        m, kdim = a.shape
    nfull = b.shape[0] if mode == "nt" else b.shape[1]
    n = nfull if n_out is None else n_out
    tm, tn, tk = min(tm, m), min(tn, n), min(tk, kdim)
    assert m % tm == 0 and n % tn == 0 and kdim % tk == 0, (name, m, n, kdim, tm, tn, tk)
    nk = kdim // tk
    dims = _DIMS[mode]
    a_spec = pl.BlockSpec((tk, tm), lambda j, i, k: (k, i)) if mode == "tn" else pl.BlockSpec((tm, tk), lambda j, i, k: (i, k))
    b_spec = pl.BlockSpec((tn, tk), lambda j, i, k: (j, k)) if mode == "nt" else pl.BlockSpec((tk, tn), lambda j, i, k: (k, j))
    o_spec = pl.BlockSpec((tm, tn), lambda j, i, k: (i, j))
    in_specs = [a_spec, b_spec]
    operands = [a, b]
    aliases = {}
    if res is not None:
        in_specs += [o_spec, pl.BlockSpec((1, tn), lambda j, i, k: (0, j))]
        operands += [res, gate]
    if add is not None:
        in_specs += [o_spec]
        aliases = {len(operands): 0}
        operands += [add]
    if after is not None:
        in_specs += [ANY]
        operands += [after]
    out_cols = n if add is None else add.shape[1]
    out_shape = [jax.ShapeDtypeStruct((m, out_cols), out_dtype)]
    out_specs = [o_spec]
    if split_out:
        out_shape = [jax.ShapeDtypeStruct((n // tn, m, tn), out_dtype)]
        out_specs = [pl.BlockSpec((None, tm, tn), lambda j, i, k: (j, i, 0))]
    if res is not None:
        out_shape.append(jax.ShapeDtypeStruct((m, n), BF16))
        out_specs.append(o_spec)

    def body(*refs):
        a_ref, b_ref = refs[:2]
        pos = 2
        if res is not None:
            res_ref, gate_ref = refs[pos:pos + 2]
            pos += 2
        if add is not None:
            add_ref = refs[pos]
            pos += 1
        if after is not None:
            pos += 1
        o_ref = refs[pos]
        pos += 1
        if res is not None:
            r_ref = refs[pos]
            pos += 1
        acc_ref = refs[pos] if nk > 1 else None
        prod = lax.dot_general(a_ref[...].astype(BF16), b_ref[...].astype(BF16), dims,
                               preferred_element_type=F32)

        def finish(acc):
            if res is not None:
                o_ref[...] = (res_ref[...] + gate_ref[...] * acc).astype(out_dtype)
                r_ref[...] = acc.astype(BF16)
            elif add is not None:
                o_ref[...] = (add_ref[...] + acc).astype(out_dtype)
            else:
                o_ref[...] = acc.astype(out_dtype)

        if nk == 1:
            finish(prod)
        else:
            k = pl.program_id(2)

            @pl.when(k == 0)
            def _():
                acc_ref[...] = prod

            @pl.when(k > 0)
            def _():
                acc_ref[...] += prod

            @pl.when(k == nk - 1)
            def _():
                finish(acc_ref[...])

    outs = pl.pallas_call(
        body, name=name, grid=(n // tn, m // tm, nk),
        in_specs=in_specs, out_specs=out_specs, out_shape=out_shape,
        scratch_shapes=[pltpu.VMEM((tm, tn), F32)] if nk > 1 else [],
        input_output_aliases=aliases,
        compiler_params=_params("parallel", "parallel", "arbitrary"),
    )(*operands)
    return outs if res is not None else outs[0]


def _rows_tile(n, want):
    return min(n, want)


def norm_mod_fwd(x, g, shift, scale, *, name):
    n, d = x.shape
    tt = _rows_tile(n, 256)

    def body(x_ref, g_ref, sh_ref, sc_ref, h_ref):
        xv = x_ref[...]
        rstd = lax.rsqrt(jnp.mean(xv * xv, axis=-1, keepdims=True) + EPS)
        y = xv * rstd * g_ref[...]
        h_ref[...] = (y * (1.0 + sc_ref[...]) + sh_ref[...]).astype(BF16)

    vec = pl.BlockSpec((1, d), lambda i: (0, 0))
    return pl.pallas_call(
        body, name=name, grid=(n // tt,),
        in_specs=[pl.BlockSpec((tt, d), lambda i: (i, 0)), vec, vec, vec],
        out_specs=pl.BlockSpec((tt, d), lambda i: (i, 0)),
        out_shape=jax.ShapeDtypeStruct((n, d), BF16),
        compiler_params=_params("parallel"),
    )(x, _row(g), _row(shift), _row(scale))


GATE_ROW = 3


def norm_mod_bwd(x, dh, dres, g, scale, *, name, branch=None):
    n, d = x.shape
    tt = _rows_tile(n, 256)
    has_res = dres is not None
    has_branch = branch is not None
    last = n // tt - 1

    def body(*refs):
        x_ref, dh_ref = refs[:2]
        pos = 2
        if has_res:
            dres_ref = refs[pos]
            pos += 1
        if has_branch:
            r_ref, gate_ref = refs[pos:pos + 2]
            pos += 2
        g_ref, sc_ref, dx_ref = refs[pos:pos + 3]
        pos += 3
        if has_branch:
            dr_ref = refs[pos]
            pos += 1
        acc_ref, s_ref = refs[pos:pos + 2]
        i = pl.program_id(0)
        xv = x_ref[...]
        dhv = dh_ref[...]
        rstd = lax.rsqrt(jnp.mean(xv * xv, axis=-1, keepdims=True) + EPS)
        xhat = xv * rstd
        dxhat = dhv * (g_ref[...] * (1.0 + sc_ref[...]))
        dx = rstd * (dxhat - xhat * jnp.mean(dxhat * xhat, axis=-1, keepdims=True))
        if has_res:
            dx = dx + dres_ref[...]
        dx_ref[...] = dx

        @pl.when(i == 0)
        def _():
            s_ref[...] = jnp.zeros_like(s_ref)

        s_ref[0:1, :] += jnp.sum(dhv, axis=0, keepdims=True)
        s_ref[1:2, :] += jnp.sum(dhv * xhat, axis=0, keepdims=True)
        if has_branch:
            dr_ref[...] = (dx * gate_ref[...]).astype(BF16)
            s_ref[2:3, :] += jnp.sum(dx * r_ref[...].astype(F32), axis=0, keepdims=True)

        @pl.when(i == last)
        def _():
            s1 = s_ref[0:1, :]
            s2 = s_ref[1:2, :]
            acc_ref[...] = jnp.zeros_like(acc_ref)
            acc_ref[0:1, :] = s1
            acc_ref[1:2, :] = s2 * g_ref[...]
            acc_ref[2:3, :] = s2 * (1.0 + sc_ref[...])
            acc_ref[GATE_ROW:GATE_ROW + 1, :] = s_ref[2:3, :]

    vec = pl.BlockSpec((1, d), lambda i: (0, 0))
    big = pl.BlockSpec((tt, d), lambda i: (i, 0))
    ops = [x, dh] + ([dres] if has_res else []) + ([branch[0], _row(branch[1])] if has_branch else [])
    ops += [_row(g), _row(scale)]
    return pl.pallas_call(
        body, name=name, grid=(n // tt,),
        in_specs=[big, big] + ([big] if has_res else []) + ([big, vec] if has_branch else []) + [vec, vec],
        out_specs=[big] + ([big] if has_branch else []) + [pl.BlockSpec((8, d), lambda i: (0, 0))],
        out_shape=[jax.ShapeDtypeStruct((n, d), F32)] + ([jax.ShapeDtypeStruct((n, d), BF16)] if has_branch else [])
        + [jax.ShapeDtypeStruct((8, d), F32)],
        scratch_shapes=[pltpu.VMEM((8, d), F32)],
        compiler_params=_params("arbitrary"),
    )(*ops)


def final_loss(x, target, g, r, gate, *, name):
    n, d = x.shape
    tt = _rows_tile(n, 256)

    def body(x_ref, t_ref, g_ref, r_ref, gate_ref, dx_ref, dr_ref, acc_ref, loss_ref):
        i = pl.program_id(0)
        xv = x_ref[...]
        rstd = lax.rsqrt(jnp.mean(xv * xv, axis=-1, keepdims=True) + EPS)
        xhat = xv * rstd
        e = xhat * g_ref[...] - t_ref[...]
        dy = e * (1.0 / d)
        dxhat = dy * g_ref[...]
        dx = rstd * (dxhat - xhat * jnp.mean(dxhat * xhat, axis=-1, keepdims=True))
        dx_ref[...] = dx
        dr_ref[...] = (dx * gate_ref[...]).astype(BF16)

        @pl.when(i == 0)
        def _():
            acc_ref[...] = jnp.zeros_like(acc_ref)
            loss_ref[...] = jnp.zeros_like(loss_ref)

        acc_ref[0:1, :] += jnp.sum(dy * xhat, axis=0, keepdims=True)
        acc_ref[GATE_ROW:GATE_ROW + 1, :] += jnp.sum(dx * r_ref[...].astype(F32), axis=0, keepdims=True)
        part = 0.5 * jnp.sum(jnp.mean(e * e, axis=-1, keepdims=True), axis=0, keepdims=True)
        loss_ref[...] += jnp.broadcast_to(part, loss_ref.shape)

    big = pl.BlockSpec((tt, d), lambda i: (i, 0))
    vec = pl.BlockSpec((1, d), lambda i: (0, 0))
    return pl.pallas_call(
        body, name=name, grid=(n // tt,),
        in_specs=[big, big, vec, big, vec],
        out_specs=[big, big, pl.BlockSpec((8, d), lambda i: (0, 0)), pl.BlockSpec((8, 128), lambda i: (0, 0))],
        out_shape=[jax.ShapeDtypeStruct((n, d), F32), jax.ShapeDtypeStruct((n, d), BF16),
                   jax.ShapeDtypeStruct((8, d), F32), jax.ShapeDtypeStruct((8, 128), F32)],
        compiler_params=_params("arbitrary"),
    )(x, target, _row(g), r, _row(gate))


def _as_row(col):
    t = col.shape[0]
    return jnp.transpose(jnp.broadcast_to(col, (t, HEAD_DIM)))[0:1, :]


def _swap_pairs(x):
    lane = lax.broadcasted_iota(jnp.int32, x.shape, 1)
    return jnp.where(lane % 2 == 0, pltpu.roll(x, HEAD_DIM - 1, 1), pltpu.roll(x, 1, 1))


def rope_tables(n):
    rows = n // GRID_W
    row = jnp.repeat(jnp.arange(rows, dtype=F32), GRID_W)
    col = jnp.tile(jnp.arange(GRID_W, dtype=F32), rows)
    n_freq = HEAD_DIM // 4
    inv = jnp.power(ROPE_THETA, jnp.arange(n_freq, dtype=F32) * (-2.0 / (HEAD_DIM // 2)))
    ang = jnp.concatenate([row[:, None] * inv, col[:, None] * inv], axis=-1)
    cos, sin = jnp.cos(ang), jnp.sin(ang)
    cexp = jnp.repeat(cos, 2, axis=-1)
    sexp = jnp.stack([-sin, sin], axis=-1).reshape(n, HEAD_DIM)
    return cexp, sexp


V_EXT_W = 2 * HEAD_DIM


def qkv_prep_fwd(p, wq, wk, cexp, sexp, *, latent, name, rows_all=None, k_all=None, v_ext=None):
    n = p.shape[0]
    tt = _rows_tile(n, 256)
    width = 2 * KV_W + (ATTN_W if latent else 0)

    def body(*refs):
        if latent:
            p_ref, wq_ref, wk_ref, c_ref, s_ref, q_ref, k_ref, v_ref = refs
        else:
            p_ref, wk_ref, _, _, k_ref, v_ref = refs

        def head(xv, w):
            rstd = lax.rsqrt(jnp.mean(xv * xv, axis=-1, keepdims=True) + EPS)
            yv = xv * rstd * w
            if latent:
                yv = yv * c_ref[...] + _swap_pairs(yv) * s_ref[...]
            return yv

        for h in range(N_KV_HEADS):
            sl = slice(h * HEAD_DIM, (h + 1) * HEAD_DIM)
            k_ref[:, sl] = head(p_ref[:, sl], wk_ref[...]).astype(BF16)
            v_ref[:, h * V_EXT_W:h * V_EXT_W + HEAD_DIM] = p_ref[:, KV_W + h * HEAD_DIM:KV_W + (h + 1) * HEAD_DIM].astype(BF16)
            lane = lax.broadcasted_iota(jnp.int32, (tt, HEAD_DIM), 1)
            v_ref[:, h * V_EXT_W + HEAD_DIM:(h + 1) * V_EXT_W] = jnp.where(lane == 0, 1.0, 0.0).astype(BF16)
        if latent:
            for h in range(N_Q_HEADS):
                sl = slice(2 * KV_W + h * HEAD_DIM, 2 * KV_W + (h + 1) * HEAD_DIM)
                q_ref[:, h * HEAD_DIM:(h + 1) * HEAD_DIM] = (head(p_ref[:, sl], wq_ref[...]) * Q_SCALE).astype(BF16)

    vec = pl.BlockSpec((1, HEAD_DIM), lambda i: (0, 0))
    tab = pl.BlockSpec((tt, HEAD_DIM), lambda i: (i, 0))
    vw = N_KV_HEADS * V_EXT_W
    k_shape = jax.ShapeDtypeStruct((rows_all, KV_W), BF16)
    v_shape = jax.ShapeDtypeStruct((rows_all, vw), BF16)
    aliases = {}
    if latent:
        in_specs = [pl.BlockSpec((tt, width), lambda i: (i, 0)), vec, vec, tab, tab]
        ops = [p, _row(wq), _row(wk), cexp, sexp]
        out_specs = [pl.BlockSpec((tt, ATTN_W), lambda i: (i, 0)), pl.BlockSpec((tt, KV_W), lambda i: (i, 0)),
                     pl.BlockSpec((tt, vw), lambda i: (i, 0))]
        out_shape = [jax.ShapeDtypeStruct((n, ATTN_W), BF16), k_shape, v_shape]
    else:
        assert n == tt and (rows_all - n) % n == 0, (n, tt, rows_all)
        first = (rows_all - n) // n
        in_specs = [pl.BlockSpec((tt, width), lambda i: (i, 0)), vec, ANY, ANY]
        ops = [p, _row(wk), k_all, v_ext]
        out_specs = [pl.BlockSpec((tt, KV_W), lambda i: (first, 0)), pl.BlockSpec((tt, vw), lambda i: (first, 0))]
        out_shape = [k_shape, v_shape]
        aliases = {2: 0, 3: 1}
    return pl.pallas_call(
        body, name=name, grid=(n // tt,), in_specs=in_specs, out_specs=out_specs, out_shape=out_shape,
        input_output_aliases=aliases, compiler_params=_params("parallel"),
    )(*ops)


def qkv_prep_bwd(p, dq, dk, dv, wq, wk, cexp, sexp, *, latent, name):
    n = p.shape[0]
    tt = _rows_tile(n, 256)
    width = 2 * KV_W + (ATTN_W if latent else 0)

    def body(*refs):
        if latent:
            p_ref, dq_ref, dk_ref, dv_ref, wq_ref, wk_ref, c_ref, s_ref, dp_ref, acc_ref = refs
        else:
            p_ref, dk_ref, dv_ref, wk_ref, dp_ref, acc_ref = refs
        i = pl.program_id(0)

        @pl.when(i == 0)
        def _():
            acc_ref[...] = jnp.zeros_like(acc_ref)

        def head(xv, dy, w, row):
            if latent:
                dy = dy * c_ref[...] + _swap_pairs(dy * s_ref[...])
            rstd = lax.rsqrt(jnp.mean(xv * xv, axis=-1, keepdims=True) + EPS)
            xhat = xv * rstd
            acc_ref[row:row + 1, :] += jnp.sum(dy * xhat, axis=0, keepdims=True)
            dxhat = dy * w
            return rstd * (dxhat - xhat * jnp.mean(dxhat * xhat, axis=-1, keepdims=True))

        for h in range(N_KV_HEADS):
            sl = slice(h * HEAD_DIM, (h + 1) * HEAD_DIM)
            dp_ref[:, sl] = head(p_ref[:, sl], dk_ref[:, sl], wk_ref[...], 1).astype(BF16)
        dp_ref[:, KV_W:2 * KV_W] = dv_ref[...].astype(BF16)
        if latent:
            for h in range(N_Q_HEADS):
                sl = slice(2 * KV_W + h * HEAD_DIM, 2 * KV_W + (h + 1) * HEAD_DIM)
                dyq = dq_ref[:, h * HEAD_DIM:(h + 1) * HEAD_DIM] * Q_SCALE
                dp_ref[:, sl] = head(p_ref[:, sl], dyq, wq_ref[...], 0).astype(BF16)

    vec = pl.BlockSpec((1, HEAD_DIM), lambda i: (0, 0))
    tab = pl.BlockSpec((tt, HEAD_DIM), lambda i: (i, 0))
    kv_spec = pl.BlockSpec((tt, KV_W), lambda i: (i, 0))
    p_spec = pl.BlockSpec((tt, width), lambda i: (i, 0))
    if latent:
        in_specs = [p_spec, pl.BlockSpec((tt, ATTN_W), lambda i: (i, 0)), kv_spec, kv_spec, vec, vec, tab, tab]
        ops = [p, dq, dk, dv, _row(wq), _row(wk), cexp, sexp]
    else:
        in_specs = [p_spec, kv_spec, kv_spec, vec]
        ops = [p, dk, dv, _row(wk)]
    return pl.pallas_call(
        body, name=name, grid=(n // tt,), in_specs=in_specs,
        out_specs=[p_spec, pl.BlockSpec((8, HEAD_DIM), lambda i: (0, 0))],
        out_shape=[jax.ShapeDtypeStruct((n, width), BF16), jax.ShapeDtypeStruct((8, HEAD_DIM), F32)],
        compiler_params=_params("arbitrary"),
    )(*ops)


def _kv_chunks(n, s_all):
    step = 1024 if n % 1024 == 0 else 256
    chunks = [(s, step) for s in range(0, n, step)]
    if s_all > n:
        chunks.append((n, s_all - n))
    return chunks


def flash_fwd(q, k_all, v_ext, p, *, za_block, mix_width, name):
    n = q.shape[0]
    s_all = k_all.shape[0]
    tq = _rows_tile(n, 512)
    chunks = _kv_chunks(n, s_all)
    wide = 2 * HEAD_DIM

    def body(q_ref, k_ref, v_ref, za_ref, o_ref, mix_ref, lse_ref, m_ref, acc_ref):
        qv = q_ref[...]
        m_ref[...] = jnp.full_like(m_ref, -jnp.inf)
        acc_ref[...] = jnp.zeros_like(acc_ref)
        for start, size in chunks:
            kc = k_ref[pl.ds(start, size), :]
            vc = v_ref[pl.ds(start, size), :]
            s = lax.dot_general(qv, kc, _DIMS["nt"], preferred_element_type=F32)
            m_old = m_ref[...]
            m_new = jnp.maximum(m_old, jnp.max(s, axis=-1, keepdims=True))
            pr = jnp.exp2(s - m_new)
            alpha = jnp.exp2(m_old - m_new)
            acc_ref[...] = alpha * acc_ref[...] + jnp.dot(pr.astype(BF16), vc, preferred_element_type=F32)
            m_ref[...] = m_new
        acc = acc_ref[...]
        denom = acc[:, HEAD_DIM:HEAD_DIM + 1]
        o = acc[:, :HEAD_DIM] / denom
        o_ref[...] = o.astype(BF16)
        mix_ref[...] = (o * _silu(za_ref[...])).astype(BF16)
        lse_ref[0] = _as_row(m_ref[...] + jnp.log2(denom))

    qspec = pl.BlockSpec((tq, HEAD_DIM), lambda h, i: (i, h))
    return pl.pallas_call(
        body, name=name, grid=(N_Q_HEADS, n // tq),
        in_specs=[qspec, pl.BlockSpec((s_all, HEAD_DIM), lambda h, i: (0, h // Q_PER_KV)),
                  pl.BlockSpec((s_all, wide), lambda h, i: (0, h // Q_PER_KV)),
                  pl.BlockSpec((tq, HEAD_DIM), lambda h, i: (i, za_block + h))],
        out_specs=[qspec, qspec, pl.BlockSpec((1, 1, tq), lambda h, i: (h, 0, i))],
        out_shape=[jax.ShapeDtypeStruct((n, ATTN_W), BF16), jax.ShapeDtypeStruct((n, mix_width), BF16),
                   jax.ShapeDtypeStruct((N_Q_HEADS, 1, n), F32)],
        scratch_shapes=[pltpu.VMEM((tq, 1), F32), pltpu.VMEM((tq, wide), F32)],
        compiler_params=_params("parallel", "parallel"),
    )(q, k_all, v_ext, p)


def attn_gate_bwd(dmix, o, p, *, za_block, name):
    n = o.shape[0]
    tt = _rows_tile(n, 512)
    za_half = za_block * HEAD_DIM // HALF
    n_half = ATTN_W // HALF

    def body(*refs):
        dm_ref, o_ref = refs[:2]
        za_refs = refs[2:2 + n_half]
        do_ref, dos_ref, dza_ref, delta_ref = refs[2 + n_half:]
        dm = dm_ref[...]
        ov = o_ref[...].astype(F32)
        za = jnp.concatenate([r[...] for r in za_refs], axis=-1)
        do = dm * _silu(za)
        do_ref[...] = do.astype(BF16)
        dos_ref[...] = (do * LN2).astype(BF16)
        dza_ref[...] = (dm * ov * _dsilu(za)).astype(BF16)
        prod = do * ov
        for h in range(N_Q_HEADS):
            col = jnp.sum(prod[:, h * HEAD_DIM:(h + 1) * HEAD_DIM], axis=-1, keepdims=True) * LN2
            delta_ref[h] = _as_row(col)

    spec = pl.BlockSpec((tt, ATTN_W), lambda i: (i, 0))
    shape = jax.ShapeDtypeStruct((n, ATTN_W), BF16)
    za_specs = [pl.BlockSpec((tt, HALF), functools.partial(lambda i, cb: (i, cb), cb=za_half + h))
                for h in range(n_half)]
    return pl.pallas_call(
        body, name=name, grid=(n // tt,),
        in_specs=[spec, spec] + za_specs,
        out_specs=[spec, spec, spec, pl.BlockSpec((N_Q_HEADS, 1, tt), lambda i: (0, 0, i))],
        out_shape=[shape, shape, shape, jax.ShapeDtypeStruct((N_Q_HEADS, 1, n), F32)],
        compiler_params=_params("parallel"),
    )(dmix, o, *([p] * n_half))


def flash_bwd(q, do, do_s, lse_row, delta_row, k_all, v_ext, *, name):
    n = q.shape[0]
    s_all = k_all.shape[0]
    tq = _rows_tile(n, 512)
    chunks = _kv_chunks(n, s_all)

    def body(q_ref, do_ref, dos_ref, lse_ref, dl_ref, k_ref, v_ref, dq_ref, dk_ref, dv_ref):
        g = pl.program_id(1)
        i = pl.program_id(2)

        @pl.when((g == 0) & (i == 0))
        def _():
            dk_ref[...] = jnp.zeros_like(dk_ref)
            dv_ref[...] = jnp.zeros_like(dv_ref)

        qv = q_ref[...]
        dov = do_ref[...]
        dosv = dos_ref[...]
        lse = lse_ref[0]
        dl = dl_ref[0]
        dq = jnp.zeros((tq, HEAD_DIM), F32)
        for start, size in chunks:
            kc = k_ref[pl.ds(start, size), :]
            vc = v_ref[pl.ds(start, size), :]
            st = lax.dot_general(kc, qv, _DIMS["nt"], preferred_element_type=F32)
            pt = jnp.exp2(st - lse)
            dpt = lax.dot_general(vc, dosv, _DIMS["nt"], preferred_element_type=F32)
            dst = (pt * (dpt - dl)).astype(BF16)
            dv_ref[pl.ds(start, size), :] += jnp.dot(pt.astype(BF16), dov, preferred_element_type=F32)
            dk_ref[pl.ds(start, size), :] += jnp.dot(dst, qv, preferred_element_type=F32)
            dq = dq + lax.dot_general(dst, kc, _DIMS["tn"], preferred_element_type=F32)
        dq_ref[...] = dq

    qspec = pl.BlockSpec((tq, HEAD_DIM), lambda kh, g, i: (i, kh * Q_PER_KV + g))
    rowspec = pl.BlockSpec((1, 1, tq), lambda kh, g, i: (kh * Q_PER_KV + g, 0, i))
    kvspec = pl.BlockSpec((s_all, HEAD_DIM), lambda kh, g, i: (0, kh))
    return pl.pallas_call(
        body, name=name, grid=(N_KV_HEADS, Q_PER_KV, n // tq),
        in_specs=[qspec, qspec, qspec, rowspec, rowspec, kvspec,
                  pl.BlockSpec((s_all, HEAD_DIM), lambda kh, g, i: (0, kh * (V_EXT_W // HEAD_DIM)))],
        out_specs=[qspec, kvspec, kvspec],
        out_shape=[jax.ShapeDtypeStruct((n, ATTN_W), F32),
                   jax.ShapeDtypeStruct((s_all, KV_W), F32), jax.ShapeDtypeStruct((s_all, KV_W), F32)],
        compiler_params=_params("arbitrary", "arbitrary", "arbitrary"),
    )(q, do, do_s, lse_row, delta_row, k_all, v_ext)


HALF = 512


def _tap_group(w_ref, cs, ext_ref, tt, r, tap_of):
    z = None
    for a in range(4):
        k = tap_of(8 * a + r)
        if 0 <= k < CONV_WIDTH:
            term = w_ref[k:k + 1, cs] * ext_ref[pl.ds(8 * a, tt + 8), :]
            z = term if z is None else z + term
    return z


def _shifted(z, z_ref, r, tt):
    if r == 0:
        return z[0:tt]
    z_ref[...] = z
    return z_ref[pl.ds(r, tt), :]


def _halo_specs(tt, n, cb):
    per = tt // HALO
    last = n // HALO - 1
    return [pl.BlockSpec((HALO, HALF), lambda i: (jnp.maximum(i * per - 1, 0), cb)),
            pl.BlockSpec((tt, HALF), lambda i: (i, cb)),
            pl.BlockSpec((HALO, HALF), lambda i: (jnp.minimum((i + 1) * per, last), cb))]


def conv_fwd(p, dw_w, dw_b, ln_g, ln_b, mix, *, glu_block, name):
    n = p.shape[0]
    ch = dw_w.shape[1]
    nh = ch // HALF
    tt = _rows_tile(n, 256)
    last = n // tt - 1

    def body(*refs):
        a_refs = [refs[3 * h:3 * h + 3] for h in range(nh)]
        b_refs = [refs[3 * (nh + h):3 * (nh + h) + 3] for h in range(nh)]
        pos = 6 * nh
        zb_refs = refs[pos:pos + nh]
        pos += nh
        w_ref, bias_ref, g_ref, be_ref, _, ycv_ref, mix_ref, ext_ref, z_ref = refs[pos:pos + 9]
        i = pl.program_id(0)
        for h in range(nh):
            cs = slice(h * HALF, (h + 1) * HALF)
            ap, am, an = a_refs[h]
            bp, bm, bn = b_refs[h]
            ext_ref[0:HALO, :] = jnp.where(i > 0, ap[...] * _sigmoid(bp[...]), 0.0)
            ext_ref[HALO:HALO + tt, :] = am[...] * _sigmoid(bm[...])
            ext_ref[HALO + tt:2 * HALO + tt, :] = jnp.where(i < last, an[...] * _sigmoid(bn[...]), 0.0)
            acc = jnp.broadcast_to(bias_ref[:, cs], (tt, HALF))
            for r in range(8):
                z = _tap_group(w_ref, cs, ext_ref, tt, r, lambda j: j - 1)
                acc = acc + _shifted(z, z_ref, r, tt)
            ycv_ref[:, cs] = acc
        yc = ycv_ref[...]
        mu = jnp.mean(yc, axis=-1, keepdims=True)
        var = jnp.mean(jnp.square(yc - mu), axis=-1, keepdims=True)
        ln = (yc - mu) * lax.rsqrt(var + EPS) * g_ref[...] + be_ref[...]
        out = _silu(ln)
        for h in range(nh):
            cs = slice(h * HALF, (h + 1) * HALF)
            mix_ref[:, cs] = (out[:, cs] * _silu(zb_refs[h][...])).astype(BF16)

    in_specs = []
    for h in range(2 * nh):
        in_specs += _halo_specs(tt, n, glu_block + h)
    in_specs += [pl.BlockSpec((tt, HALF), functools.partial(lambda i, cb: (i, cb), cb=glu_block + 2 * nh + h))
                 for h in range(nh)]
    vec = pl.BlockSpec((1, ch), lambda i: (0, 0))
    in_specs += [pl.BlockSpec((CONV_WIDTH, ch), lambda i: (0, 0)), vec, vec, vec, ANY]
    ops = [p] * (6 * nh + nh) + [dw_w, _row(dw_b), _row(ln_g), _row(ln_b), mix]
    big = pl.BlockSpec((tt, ch), lambda i: (i, 0))
    mix_block = (mix.shape[1] - ch) // ch
    return pl.pallas_call(
        body, name=name, grid=(n // tt,), in_specs=in_specs,
        out_specs=[big, pl.BlockSpec((tt, ch), lambda i: (i, mix_block))],
        out_shape=[jax.ShapeDtypeStruct((n, ch), F32), jax.ShapeDtypeStruct(mix.shape, BF16)],
        input_output_aliases={len(ops) - 1: 1},
        scratch_shapes=[pltpu.VMEM((tt + 2 * HALO, HALF), F32), pltpu.VMEM((tt + 8, HALF), F32)],
        compiler_params=_params("parallel"),
    )(*ops)


def conv_bwd_rows(dmix, ycv, p, ln_g, ln_b, *, mix_block, zb_block, name):
    n, ch = ycv.shape
    nh = ch // HALF
    tt = _rows_tile(n, 256)

    def body(*refs):
        dm_ref, ycv_ref = refs[:2]
        zb_refs = refs[2:2 + nh]
        g_ref, be_ref, dy_ref, dzb_ref, acc_ref = refs[2 + nh:]
        i = pl.program_id(0)

        @pl.when(i == 0)
        def _():
            acc_ref[...] = jnp.zeros_like(acc_ref)

        yc = ycv_ref[...]
        mu = jnp.mean(yc, axis=-1, keepdims=True)
        var = jnp.mean(jnp.square(yc - mu), axis=-1, keepdims=True)
        rstd = lax.rsqrt(var + EPS)
        xhat = (yc - mu) * rstd
        ln = xhat * g_ref[...] + be_ref[...]
        out = _silu(ln)
        dm = dm_ref[...]
        zb = jnp.concatenate([r[...] for r in zb_refs], axis=-1)
        dzb_ref[...] = (dm * out * _dsilu(zb)).astype(BF16)
        dln = dm * _silu(zb) * _dsilu(ln)
        acc_ref[0:1, :] += jnp.sum(dln * xhat, axis=0, keepdims=True)
        acc_ref[1:2, :] += jnp.sum(dln, axis=0, keepdims=True)
        dxhat = dln * g_ref[...]
        dy_ref[...] = rstd * (dxhat - jnp.mean(dxhat, axis=-1, keepdims=True)
                              - xhat * jnp.mean(dxhat * xhat, axis=-1, keepdims=True))

    big = pl.BlockSpec((tt, ch), lambda i: (i, 0))
    vec = pl.BlockSpec((1, ch), lambda i: (0, 0))
    in_specs = [pl.BlockSpec((tt, ch), lambda i: (i, mix_block)), big]
    in_specs += [pl.BlockSpec((tt, HALF), functools.partial(lambda i, cb: (i, cb), cb=zb_block + h)) for h in range(nh)]
    in_specs += [vec, vec]
    return pl.pallas_call(
        body, name=name, grid=(n // tt,), in_specs=in_specs,
        out_specs=[big, big, pl.BlockSpec((8, ch), lambda i: (0, 0))],
        out_shape=[jax.ShapeDtypeStruct((n, ch), F32), jax.ShapeDtypeStruct((n, ch), BF16),
                   jax.ShapeDtypeStruct((8, ch), F32)],
        compiler_params=_params("arbitrary"),
    )(dmix, ycv, *([p] * nh), _row(ln_g), _row(ln_b))


def conv_bwd_taps(dycv, p, dw_w, *, glu_block, name):
    n, ch = dycv.shape
    nh = ch // HALF
    tt = _rows_tile(n, 256)
    last = n // tt - 1

    def body(*refs):
        d_refs = [refs[3 * h:3 * h + 3] for h in range(nh)]
        a_refs = [refs[3 * (nh + h):3 * (nh + h) + 3] for h in range(nh)]
        b_refs = [refs[3 * (2 * nh + h):3 * (2 * nh + h) + 3] for h in range(nh)]
        w_ref, dglu_ref, dw_ref, db_ref, yext_ref, dext_ref, z_ref = refs[9 * nh:]
        i = pl.program_id(0)

        @pl.when(i == 0)
        def _():
            dw_ref[...] = jnp.zeros_like(dw_ref)
            db_ref[...] = jnp.zeros_like(db_ref)

        for h in range(nh):
            cs = slice(h * HALF, (h + 1) * HALF)
            ap, am, an = a_refs[h]
            bp, bm, bn = b_refs[h]
            dp, dm, dn = d_refs[h]
            av = am[...]
            sb = _sigmoid(bm[...])
            yext_ref[0:HALO, :] = jnp.where(i > 0, ap[...] * _sigmoid(bp[...]), 0.0)
            yext_ref[HALO:HALO + tt, :] = av * sb
            yext_ref[HALO + tt:2 * HALO + tt, :] = jnp.where(i < last, an[...] * _sigmoid(bn[...]), 0.0)
            dmain = dm[...]
            dext_ref[0:HALO, :] = jnp.where(i > 0, dp[...], 0.0)
            dext_ref[HALO:HALO + tt, :] = dmain
            dext_ref[HALO + tt:2 * HALO + tt, :] = jnp.where(i < last, dn[...], 0.0)
            dy = jnp.zeros((tt, HALF), F32)
            for r in range(8):
                z = _tap_group(w_ref, cs, dext_ref, tt, r, lambda j: CONV_WIDTH - j)
                dy = dy + _shifted(z, z_ref, r, tt)
                y_r = yext_ref[pl.ds(r, tt + 24), :]
                for a in range(4):
                    k = 8 * a + r - 1
                    if k >= 0:
                        dw_ref[k:k + 1, cs] += jnp.sum(dmain * y_r[8 * a:8 * a + tt], axis=0, keepdims=True)
            db_ref[0:1, cs] += jnp.sum(dmain, axis=0, keepdims=True)
            dglu_ref[:, cs] = (dy * sb).astype(BF16)
            dglu_ref[:, ch + h * HALF:ch + (h + 1) * HALF] = (dy * av * sb * (1.0 - sb)).astype(BF16)

    in_specs = []
    for h in range(nh):
        in_specs += _halo_specs(tt, n, h)
    for h in range(2 * nh):
        in_specs += _halo_specs(tt, n, glu_block + h)
    in_specs += [pl.BlockSpec((CONV_WIDTH, ch), lambda i: (0, 0))]
    ops = [dycv] * (3 * nh) + [p] * (6 * nh) + [dw_w]
    return pl.pallas_call(
        body, name=name, grid=(n // tt,), in_specs=in_specs,
        out_specs=[pl.BlockSpec((tt, 2 * ch), lambda i: (i, 0)), pl.BlockSpec((32, ch), lambda i: (0, 0)),
                   pl.BlockSpec((8, ch), lambda i: (0, 0))],
        out_shape=[jax.ShapeDtypeStruct((n, 2 * ch), BF16), jax.ShapeDtypeStruct((32, ch), F32),
                   jax.ShapeDtypeStruct((8, ch), F32)],
        scratch_shapes=[pltpu.VMEM((tt + 2 * HALO, HALF), F32), pltpu.VMEM((tt + 2 * HALO, HALF), F32),
                        pltpu.VMEM((tt + 8, HALF), F32)],
        compiler_params=_params("arbitrary"),
    )(*ops)


def _sgu_common(p_ref, g_ref, be_ref, ws_ref, bs_ref, w):
    gw = w // SGU_GROUPS
    u_pre = p_ref[:, 0:w]
    v_pre = p_ref[:, w:2 * w]
    zc = p_ref[:, 2 * w:3 * w]
    u = _gelu(u_pre)
    v = _gelu(v_pre)
    mu = jnp.mean(v, axis=-1, keepdims=True)
    var = jnp.mean(jnp.square(v - mu), axis=-1, keepdims=True)
    rstd = lax.rsqrt(var + EPS)
    vhat = (v - mu) * rstd
    vn = (vhat * g_ref[...] + be_ref[...]).astype(BF16)
    mixed = jnp.concatenate(
        [jnp.dot(ws_ref[gi].astype(BF16), vn[:, gi * gw:(gi + 1) * gw], preferred_element_type=F32)
         + bs_ref[:, gi:gi + 1] for gi in range(SGU_GROUPS)], axis=-1)
    return u_pre, v_pre, zc, u, rstd, vhat, vn, mixed


def sgu_fwd(p, ln_g, ln_b, ws, bs_t, *, name):
    n, w3 = p.shape
    w = w3 // 3

    def body(p_ref, g_ref, be_ref, ws_ref, bs_ref, m_ref):
        _, _, zc, u, _, _, _, mixed = _sgu_common(p_ref, g_ref, be_ref, ws_ref, bs_ref, w)
        m_ref[...] = (u * mixed * _silu(zc)).astype(BF16)

    vec = pl.BlockSpec((1, w), lambda i: (0, 0))
    return pl.pallas_call(
        body, name=name, grid=(n // CHUNK,),
        in_specs=[pl.BlockSpec((CHUNK, w3), lambda i: (i, 0)), vec, vec,
                  pl.BlockSpec((SGU_GROUPS, CHUNK, CHUNK), lambda i: (0, 0, 0)),
                  pl.BlockSpec((CHUNK, SGU_GROUPS), lambda i: (0, 0))],
        out_specs=pl.BlockSpec((CHUNK, w), lambda i: (i, 0)),
        out_shape=jax.ShapeDtypeStruct((n, w), BF16),
        compiler_params=_params("parallel"),
    )(p, _row(ln_g), _row(ln_b), ws, bs_t)


def sgu_bwd(p, dm, ln_g, ln_b, ws, ws_t, bs_t, *, name):
    n, w3 = p.shape
    w = w3 // 3
    gw = w // SGU_GROUPS

    def body(p_ref, dm_ref, g_ref, be_ref, ws_ref, wst_ref, bs_ref, dp_ref, dws_ref, dbs_ref, acc_ref):
        i = pl.program_id(0)

        @pl.when(i == 0)
        def _():
            dws_ref[...] = jnp.zeros_like(dws_ref)
            dbs_ref[...] = jnp.zeros_like(dbs_ref)
            acc_ref[...] = jnp.zeros_like(acc_ref)

        u_pre, v_pre, zc, u, rstd, vhat, vn, mixed = _sgu_common(p_ref, g_ref, be_ref, ws_ref, bs_ref, w)
        dmv = dm_ref[...]
        um = u * mixed
        dp_ref[:, 2 * w:3 * w] = (dmv * um * _dsilu(zc)).astype(BF16)
        dum = dmv * _silu(zc)
        dp_ref[:, 0:w] = (dum * mixed * _dgelu(u_pre)).astype(BF16)
        dmixed = dum * u
        dmixed_b = dmixed.astype(BF16)
        dvn_parts = []
        for gi in range(SGU_GROUPS):
            cs = slice(gi * gw, (gi + 1) * gw)
            dws_ref[gi] += lax.dot_general(dmixed_b[:, cs], vn[:, cs], _DIMS["nt"], preferred_element_type=F32)
            dbs_ref[:, gi:gi + 1] += jnp.sum(dmixed[:, cs], axis=-1, keepdims=True)
            dvn_parts.append(jnp.dot(wst_ref[gi].astype(BF16), dmixed_b[:, cs], preferred_element_type=F32))
        dvn = jnp.concatenate(dvn_parts, axis=-1)
        acc_ref[0:1, :] += jnp.sum(dvn * vhat, axis=0, keepdims=True)
        acc_ref[1:2, :] += jnp.sum(dvn, axis=0, keepdims=True)
        dvhat = dvn * g_ref[...]
        dv = rstd * (dvhat - jnp.mean(dvhat, axis=-1, keepdims=True)
                     - vhat * jnp.mean(dvhat * vhat, axis=-1, keepdims=True))
        dp_ref[:, w:2 * w] = (dv * _dgelu(v_pre)).astype(BF16)

    vec = pl.BlockSpec((1, w), lambda i: (0, 0))
    wspec = pl.BlockSpec((SGU_GROUPS, CHUNK, CHUNK), lambda i: (0, 0, 0))
    bspec = pl.BlockSpec((CHUNK, SGU_GROUPS), lambda i: (0, 0))
    return pl.pallas_call(
        body, name=name, grid=(n // CHUNK,),
        in_specs=[pl.BlockSpec((CHUNK, w3), lambda i: (i, 0)), pl.BlockSpec((CHUNK, w), lambda i: (i, 0)),
                  vec, vec, wspec, wspec, bspec],
        out_specs=[pl.BlockSpec((CHUNK, w3), lambda i: (i, 0)), wspec, bspec,
                   pl.BlockSpec((8, w), lambda i: (0, 0))],
        out_shape=[jax.ShapeDtypeStruct((n, w3), BF16), jax.ShapeDtypeStruct((SGU_GROUPS, CHUNK, CHUNK), F32),
                   jax.ShapeDtypeStruct((CHUNK, SGU_GROUPS), F32), jax.ShapeDtypeStruct((8, w), F32)],
        compiler_params=_params("arbitrary"),
    )(p, dm, _row(ln_g), _row(ln_b), ws, ws_t, bs_t)


def _adam_math(w, g, m, v):
    m_new = ADAM_B1 * m + (1.0 - ADAM_B1) * g
    v_new = ADAM_B2 * v + (1.0 - ADAM_B2) * (g * g)
    m_hat = m_new / (1.0 - ADAM_B1 ** ADAM_STEP)
    v_hat = v_new / (1.0 - ADAM_B2 ** ADAM_STEP)
    delta = -ADAM_LR * (m_hat / (jnp.sqrt(v_hat) + ADAM_EPS) + ADAM_WD * w)
    return delta, m_new, v_new


def adamw(w, g, m, v, *, name, slots=False, rows=512):
    r, c = w.shape
    tr = min(r, rows)
    assert r % tr == 0, (name, r, tr)

    def body(w_ref, g_ref, m_ref, v_ref, go_ref, d_ref, mo_ref, vo_ref):
        if slots:
            g = g_ref[0].astype(F32)
            for k in range(1, N_DEV):
                g = g + g_ref[k].astype(F32)
        else:
            g = g_ref[...].astype(F32)
        delta, m_new, v_new = _adam_math(w_ref[...], g, m_ref[...], v_ref[...])
        go_ref[...] = g
        d_ref[...] = delta
        mo_ref[...] = m_new
        vo_ref[...] = v_new

    spec = pl.BlockSpec((tr, c), lambda i: (i, 0))
    gspec = pl.BlockSpec((N_DEV, tr, c), lambda i: (0, i, 0)) if slots else spec
    shape = jax.ShapeDtypeStruct((r, c), F32)
    return pl.pallas_call(
        body, name=name, grid=(r // tr,),
        in_specs=[spec, gspec, spec, spec], out_specs=[spec] * 4, out_shape=[shape] * 4,
        compiler_params=_params("parallel"),
    )(w, g, m, v)


def _after(token, val):
    return val + token[0, 0]


def _pad_rows(a, rows):
    return jnp.pad(a, ((0, rows - a.shape[0]), (0, 0)))


def kernel(x, c, ctx, c_ctx, ada_w, ada_b, norm_g, ev_w_in, ev_q_norm, ev_k_norm, ev_dw_w, ev_dw_b, ev_ln_g, ev_ln_b, ev_w_out, od_w_in, od_ln_g, od_ln_b, od_ws, od_bs, od_w_out, final_g, loss_target, m_c_ctx, m_ada_w, m_ada_b, m_norm_g, m_ev_w_in, m_ev_q_norm, m_ev_k_norm, m_ev_dw_w, m_ev_dw_b, m_ev_ln_g, m_ev_ln_b, m_ev_w_out, m_od_w_in, m_od_ln_g, m_od_ln_b, m_od_ws, m_od_bs, m_od_w_out, m_final_g, v_c_ctx, v_ada_w, v_ada_b, v_norm_g, v_ev_w_in, v_ev_q_norm, v_ev_k_norm, v_ev_dw_w, v_ev_dw_b, v_ev_ln_g, v_ev_ln_b, v_ev_w_out, v_od_w_in, v_od_ln_g, v_od_ln_b, v_od_ws, v_od_bs, v_od_w_out, v_final_g):
    n, d = x.shape[1], x.shape[2]
    lc = ctx.shape[1]
    ev_in = ev_w_in.shape[2] * N_DEV
    od_in = od_w_in.shape[2] * N_DEV
    conv_ch = ev_dw_w.shape[2] * N_DEV
    ada_cols = ada_w.shape[2]
    me = 4 * lax.axis_index("x") + 2 * lax.axis_index("y") + lax.axis_index("c")
    xs, tgt, ctxs = x[0], loss_target[0], ctx[0]
    za_block = (2 * KV_W + ATTN_W) // HEAD_DIM
    glu_block = (2 * KV_W + 2 * ATTN_W) // HALF
    zb_block = glu_block + 2 * conv_ch // HALF

    small = jnp.concatenate([
        jax.nn.silu(c).reshape(1, d),
        od_ln_g.reshape(1, -1), od_ln_b.reshape(1, -1)], axis=1)
    small = _pad_rows(small, 8)
    dw_rows = _pad_rows(ev_dw_w[0], 32)
    small_g, dw_g = all_gather([small, dw_rows], name="gather_small", hbm=False)
    sc_all = small_g[:, 0, :d]
    shard = d // N_DEV
    od_ln_g_full = small_g[:, 0, d:d + shard].reshape(d)
    od_ln_b_full = small_g[:, 0, d + shard:d + 2 * shard].reshape(d)
    dw_w_full = jnp.moveaxis(dw_g, 0, 1).reshape(32, conv_ch)[:CONV_WIDTH]
    scc = jax.nn.silu(c_ctx)
    sc16 = _pad_rows(jnp.concatenate([sc_all, scc.reshape(1, d)], axis=0), 16)

    ada_bf = ada_w.astype(BF16)
    mod_loc = [matmul(sc16, ada_bf[l], mode="nn", tm=16, tn=ada_cols, tk=d, out_dtype=F32, name=f"ada_mod{l}")
               for l in range(2)]
    (mod_g,) = all_gather([jnp.stack(mod_loc)], name="gather_mod", hbm=False)
    mod_all = jnp.moveaxis(mod_g, 0, 2).reshape(2, 16, N_DEV * ada_cols) + ada_b[:, None, :]
    mod_me = lax.dynamic_index_in_dim(mod_all, me, axis=1, keepdims=False)
    shift = [mod_me[l, :d] for l in range(2)]
    scale = [mod_me[l, d:2 * d] for l in range(2)]
    gate = [mod_me[l, 2 * d:] for l in range(2)]
    shift_c, scale_c = mod_all[0, 8, :d], mod_all[0, 8, d:2 * d]

    (wi0_g,) = all_gather([ev_w_in[0].astype(BF16)], name="gather_w_in0", hbm=True)
    wi0 = jnp.moveaxis(wi0_g, 0, 1).reshape(d, ev_in)
    bits = lax.bitcast_convert_type(wi0_g[0, 0, 0], jnp.uint16)
    landed_zero = jnp.where((bits | 1) == 0, 1.0, 0.0).astype(F32)
    later = [(w[0] + landed_zero).astype(BF16) for w in (ev_w_out, od_w_in, od_w_out)]
    (h_wo0, h_wi1, h_wo1), w_token = exchange_start(later, scatter=False, name="gather_rest_start")

    def landed(handle, after, name):
        own, land = exchange_wait(handle, after, scatter=False, name=name)
        return lax.dynamic_update_slice_in_dim(land, own[None], me, axis=0)

    cexp, sexp = rope_tables(n)

    shift0 = _after(w_token, shift[0])
    h0 = norm_mod_fwd(xs, norm_g[0], shift0, scale[0], name="norm_mod_fwd0")
    hc = norm_mod_fwd(ctxs, norm_g[0], shift_c, scale_c, name="norm_mod_fwd_ctx")
    p0 = matmul(h0, wi0, mode="nn", tm=1024, tn=ev_in // 4, tk=d, out_dtype=F32, name="proj_in0")
    pc = matmul(hc, wi0, mode="nn", tm=lc, tn=2 * KV_W, tk=d, out_dtype=F32, n_out=2 * KV_W, name="proj_in_ctx")
    q_r, k_all, v_ext = qkv_prep_fwd(p0, ev_q_norm[0], ev_k_norm[0], cexp, sexp, latent=True, name="qkv_prep",
                                     rows_all=n + lc)
    k_all, v_ext = qkv_prep_fwd(pc, None, ev_k_norm[0], None, None, latent=False, name="kv_prep_ctx",
                                rows_all=n + lc, k_all=k_all, v_ext=v_ext)
    o_attn, mix_a, lse = flash_fwd(q_r, k_all, v_ext, p0, za_block=za_block, mix_width=ATTN_W + conv_ch,
                                   name="flash_fwd")
    ycv, mix0 = conv_fwd(p0, dw_w_full, ev_dw_b[0], ev_ln_g[0], ev_ln_b[0], mix_a, glu_block=glu_block,
                         name="conv_fwd")
    wo0 = landed(h_wo0, ycv, "gather_w_out0_wait").reshape(-1, d)
    x1, r0 = matmul(mix0, wo0, mode="nn", tm=1024, tn=1024, tk=mix0.shape[1], out_dtype=F32, name="proj_out0",
                    res=xs, gate=_row(gate[0]))

    h1 = norm_mod_fwd(x1, norm_g[1], shift[1], scale[1], name="norm_mod_fwd1")
    wi1 = jnp.moveaxis(landed(h_wi1, h1, "gather_w_in1_wait"), 0, 1).reshape(d, od_in)
    p1 = matmul(h1, wi1, mode="nn", tm=1024, tn=od_in // 4, tk=d, out_dtype=F32, name="proj_in1")
    ws_bf = od_ws[0]
    bs_t = od_bs[0].T
    m1 = sgu_fwd(p1, od_ln_g_full, od_ln_b_full, ws_bf, bs_t, name="sgu_fwd")
    wo1 = landed(h_wo1, m1, "gather_w_out1_wait").reshape(-1, d)
    x2, r1 = matmul(m1, wo1, mode="nn", tm=1024, tn=1024, tk=m1.shape[1], out_dtype=F32, name="proj_out1",
                    res=x1, gate=_row(gate[1]))

    dx2, dr1, acc_final, loss_tile = final_loss(x2, tgt, final_g, r1, gate[1], name="final_loss")

    dm1 = matmul(dr1, wo1, mode="nt", tm=1024, tn=1024, tk=d, out_dtype=F32, name="d_mix1")
    dwo1 = matmul(m1, dr1, mode="tn", tm=1024, tn=1024, tk=2048, out_dtype=BF16, name="d_wout1")
    dp1, dws, dbs_t, acc_sgu = sgu_bwd(p1, dm1, od_ln_g_full, od_ln_b_full, ws_bf, jnp.swapaxes(ws_bf, 1, 2), bs_t,
                                       name="sgu_bwd")
    dwi1_s = matmul(h1, dp1, mode="tn", tm=2048, tn=od_in // N_DEV, tk=2048, out_dtype=BF16, name="d_win1",
                    split_out=True)
    dwo1_s = dwo1.reshape(N_DEV, -1, d)
    (h_gi1, h_go1), g1_token = exchange_start([dwi1_s, dwo1_s], scatter=True, name="grads1_start")
    dh1 = matmul(dp1, wi1, mode="nt", tm=1024, tn=512, tk=od_in, out_dtype=F32, name="d_h1")
    dx1, dr0, acc_norm1 = norm_mod_bwd(x1, dh1, dx2, norm_g[1], _after(g1_token, scale[1]), name="norm_mod_bwd1",
                                       branch=(r0, gate[0]))

    dmix0 = matmul(dr0, wo0, mode="nt", tm=1024, tn=1024, tk=d, out_dtype=F32, name="d_mix0")
    dwo0 = matmul(mix0, dr0, mode="tn", tm=1024, tn=1024, tk=2048, out_dtype=BF16, name="d_wout0")
    do_attn, do_s, dza, delta = attn_gate_bwd(dmix0, o_attn, p0, za_block=za_block, name="attn_gate_bwd")
    dycv, dzb, acc_ln = conv_bwd_rows(dmix0, ycv, p0, ev_ln_g[0], ev_ln_b[0], mix_block=ATTN_W // conv_ch,
                                      zb_block=zb_block, name="conv_bwd_rows")
    dglu, ddw_w, acc_dwb = conv_bwd_taps(dycv, p0, dw_w_full, glu_block=glu_block, name="conv_bwd_taps")
    dq_r, dk_all, dv_all = flash_bwd(q_r, do_attn, do_s, lse, delta, k_all, v_ext, name="flash_bwd")
    dkvq, acc_qk = qkv_prep_bwd(p0, dq_r, dk_all[:n], dv_all[:n], ev_q_norm[0], ev_k_norm[0], cexp, sexp,
                                latent=True, name="qkv_prep_bwd")
    dpc, acc_kc = qkv_prep_bwd(pc, None, dk_all[n:], dv_all[n:], None, ev_k_norm[0], None, None,
                               latent=False, name="kv_prep_ctx_bwd")
    dp0 = jnp.concatenate([dkvq, dza, dglu, dzb], axis=1)
    dwi0 = matmul(h0, dp0, mode="tn", tm=1024, tn=ev_in // 4, tk=2048, out_dtype=F32, name="d_win0")
    dwi0 = matmul(hc, dpc, mode="tn", tm=512, tn=2 * KV_W, tk=lc, out_dtype=F32, name="d_win0_ctx", add=dwi0)
    dwi0_s = jnp.moveaxis(dwi0.astype(BF16).reshape(d, N_DEV, ev_in // N_DEV), 1, 0)
    dwo0_s = dwo0.reshape(N_DEV, -1, d)
    (h_gi0, h_go0), g0_token = exchange_start([dwi0_s, dwo0_s], scatter=True, name="grads0_start")
    dh0 = matmul(dp0, wi0, mode="nt", tm=1024, tn=512, tk=ev_in, out_dtype=F32, name="d_h0", after=g0_token)
    dhc = matmul(dpc, wi0, mode="nt", tm=lc, tn=512, tk=2 * KV_W, out_dtype=F32, name="d_h_ctx")
    grad_x, acc_norm0 = norm_mod_bwd(xs, dh0, dx1, norm_g[0], _after(g0_token, scale[0]), name="norm_mod_bwd0")
    _, acc_normc = norm_mod_bwd(ctxs, dhc, None, norm_g[0], scale_c, name="norm_mod_bwd_ctx")

    zeros_d = jnp.zeros((d,), F32)
    dmod0 = jnp.stack([acc_norm0[0], acc_norm0[1], acc_norm1[GATE_ROW]])
    dmod1 = jnp.stack([acc_norm1[0], acc_norm1[1], acc_final[GATE_ROW]])
    dmodc = jnp.stack([acc_normc[0], acc_normc[1]])
    half_pad = jnp.zeros((d - 2 * conv_ch,), F32) if d > 2 * conv_ch else jnp.zeros((0,), F32)
    row_a = jnp.concatenate([acc_dwb[0], acc_ln[0], half_pad])
    row_b = jnp.concatenate([acc_ln[1], acc_qk[0], acc_qk[1] + acc_kc[1],
                             jnp.zeros((d - conv_ch - 2 * HEAD_DIM,), F32)])
    row_c = jnp.concatenate([dbs_t.T.reshape(-1), jnp.zeros((d - SGU_GROUPS * CHUNK,), F32)])
    row_loss = jnp.concatenate([loss_tile[0, :1], jnp.zeros((d - 1,), F32)])
    pack = jnp.concatenate([
        dmod0, dmod1, dmodc,
        (acc_norm0[2] + acc_normc[2])[None], acc_norm1[2][None],
        acc_final[0][None],
        acc_sgu[0][None], acc_sgu[1][None],
        row_a[None], row_b[None], row_c[None], row_loss[None],
        ddw_w.reshape(-1, d),
        dws.reshape(-1, d),
    ], axis=0)
    n_rows = pack.shape[0]
    pack = _pad_rows(pack, -(-n_rows // 8) * 8)
    pack_g, pack_sum = all_gather([pack], name="gather_small_grads", hbm=False, sum_out=True)
    gsum = pack_sum
    loss = gsum[16, 0]
    dw_rows_n = 32 * conv_ch // d
    g_dw_w = gsum[17:17 + dw_rows_n].reshape(32, conv_ch)[:CONV_WIDTH]
    g_od_ws = gsum[17 + dw_rows_n:17 + dw_rows_n + SGU_GROUPS * CHUNK * CHUNK // d].reshape(od_ws.shape)

    dmodc_sum = jnp.concatenate([gsum[6], gsum[7], zeros_d])
    col0 = me * ada_cols
    dm_cols = []
    for l in range(2):
        rows = pack_g[:, 3 * l:3 * l + 3, :].reshape(N_DEV, 3 * d)
        extra = dmodc_sum[None] if l == 0 else jnp.zeros((1, 3 * d), F32)
        full = _pad_rows(jnp.concatenate([rows, extra], axis=0), 16)
        dm_cols.append(lax.dynamic_slice_in_dim(full, col0, ada_cols, axis=1))
    g_ada_w = jnp.stack([matmul(sc16, dm_cols[l], mode="tn", tm=512, tn=ada_cols, tk=16, out_dtype=F32,
                                name=f"d_ada_w{l}") for l in range(2)])
    dsc = matmul(dm_cols[0], ada_bf[0], mode="nt", tm=16, tn=512, tk=ada_cols, out_dtype=F32, name="d_scc")
    (_, dscc_sum) = all_gather([dsc[8:16]], name="gather_dscc", hbm=False, sum_out=True)
    sg = jax.nn.sigmoid(c_ctx)
    g_c_ctx = dscc_sum[0] * (sg * (1.0 + c_ctx * (1.0 - sg)))
    g_ada_b = jnp.stack([gsum[0:3].reshape(-1) + dmodc_sum, gsum[3:6].reshape(-1)])

    def summands(handle, after, name):
        mine, land = exchange_wait(handle, after, scatter=True, name=name)
        own = lax.dynamic_index_in_dim(mine, me, axis=0, keepdims=True)
        return lax.dynamic_update_slice_in_dim(land, own, me, axis=0)

    out = {}

    def upd(key, w, g, m, v, slots=False, rows=512):
        shp = w.shape
        w2 = w.reshape(-1, shp[-1])
        g2 = g.reshape((N_DEV, -1, shp[-1])) if slots else g.reshape(-1, shp[-1])
        res = adamw(w2, g2, m.reshape(w2.shape), v.reshape(w2.shape), name="adamw_" + key, slots=slots, rows=rows)
        out[key] = tuple(r.reshape(shp) for r in res)

    upd("ada_w", ada_w, g_ada_w, m_ada_w, v_ada_w)
    gi1 = summands(h_gi1,out["ada_w"][1], "grads_w_in1_wait")
    upd("od_w_in", od_w_in, gi1, m_od_w_in, v_od_w_in, slots=True, rows=256)
    go1 = summands(h_go1,out["od_w_in"][1], "grads_w_out1_wait")
    upd("od_w_out", od_w_out, go1, m_od_w_out, v_od_w_out, slots=True, rows=256)
    go0 = summands(h_go0,out["od_w_out"][1], "grads_w_out0_wait")
    upd("ev_w_out", ev_w_out, go0, m_ev_w_out, v_ev_w_out, slots=True, rows=256)
    gi0 = summands(h_gi0,out["ev_w_out"][1], "grads_w_in0_wait")
    upd("ev_w_in", ev_w_in, gi0, m_ev_w_in, v_ev_w_in, slots=True, rows=256)

    def my_shard(full, size):
        return lax.dynamic_slice_in_dim(full, me * size, size, axis=full.ndim - 1)

    small_items = [
        ("c_ctx", c_ctx, g_c_ctx, m_c_ctx, v_c_ctx),
        ("ada_b", ada_b, g_ada_b, m_ada_b, v_ada_b),
        ("norm_g", norm_g, gsum[8:10], m_norm_g, v_norm_g),
        ("ev_q_norm", ev_q_norm, gsum[14, conv_ch:conv_ch + HEAD_DIM], m_ev_q_norm, v_ev_q_norm),
        ("ev_k_norm", ev_k_norm, gsum[14, conv_ch + HEAD_DIM:conv_ch + 2 * HEAD_DIM], m_ev_k_norm, v_ev_k_norm),
        ("ev_dw_w", ev_dw_w, my_shard(g_dw_w, conv_ch // N_DEV), m_ev_dw_w, v_ev_dw_w),
        ("ev_dw_b", ev_dw_b, gsum[13, :conv_ch], m_ev_dw_b, v_ev_dw_b),
        ("ev_ln_g", ev_ln_g, gsum[13, conv_ch:2 * conv_ch], m_ev_ln_g, v_ev_ln_g),
        ("ev_ln_b", ev_ln_b, gsum[14, :conv_ch], m_ev_ln_b, v_ev_ln_b),
        ("od_ln_g", od_ln_g, my_shard(gsum[11], shard), m_od_ln_g, v_od_ln_g),
        ("od_ln_b", od_ln_b, my_shard(gsum[12], shard), m_od_ln_b, v_od_ln_b),
        ("od_ws", od_ws, g_od_ws, m_od_ws, v_od_ws),
        ("od_bs", od_bs, gsum[15, :SGU_GROUPS * CHUNK], m_od_bs, v_od_bs),
        ("final_g", final_g, gsum[10], m_final_g, v_final_g),
    ]
    sizes = [it[1].size for it in small_items]
    total = sum(sizes)
    lanes = 1024
    prow = -(-total // lanes)
    prow = -(-prow // 8) * 8

    def pack_small(idx):
        flat = jnp.concatenate([it[idx].reshape(-1).astype(F32) for it in small_items])
        return jnp.pad(flat, (0, prow * lanes - total)).reshape(prow, lanes)

    sres = adamw(pack_small(1), pack_small(2), pack_small(3), pack_small(4), name="adamw_small", rows=prow)
    off = 0
    for it, size in zip(small_items, sizes):
        out[it[0]] = tuple(r.reshape(-1)[off:off + size].reshape(it[1].shape) for r in sres)
        off += size

    names = ['c_ctx', 'ada_w', 'ada_b', 'norm_g', 'ev_w_in', 'ev_q_norm', 'ev_k_norm', 'ev_dw_w', 'ev_dw_b',
             'ev_ln_g', 'ev_ln_b', 'ev_w_out', 'od_w_in', 'od_ln_g', 'od_ln_b', 'od_ws', 'od_bs', 'od_w_out',
             'final_g']
    return (loss, grad_x[None], *[out[k][0] for k in names], *[out[k][1] for k in names],
            *[out[k][2] for k in names], *[out[k][3] for k in names])
```

```python
import functools
import math

import jax
import jax.numpy as jnp
from jax import lax
from jax.experimental import pallas as pl
from jax.experimental.pallas import tpu as pltpu

F32 = jnp.float32
BF16 = jnp.bfloat16
MESH = pl.DeviceIdType.MESH

EPS = 1e-6
HEAD_DIM = 128
N_Q_HEADS = 8
N_KV_HEADS = 2
Q_PER_KV = N_Q_HEADS // N_KV_HEADS
ATTN_W = N_Q_HEADS * HEAD_DIM
KV_W = N_KV_HEADS * HEAD_DIM
ATTN_SCALE = HEAD_DIM ** -0.5
LN2 = math.log(2.0)
Q_SCALE = ATTN_SCALE / LN2
ROPE_THETA = 10000.0
GRID_W = 64
CONV_WIDTH = 31
CONV_HALF = CONV_WIDTH // 2
HALO = 16
CHUNK = 128
SGU_GROUPS = 8
N_DEV = 8

ADAM_LR = 0.001
ADAM_B1 = 0.9
ADAM_B2 = 0.999
ADAM_EPS = 1e-08
ADAM_WD = 0.01
ADAM_STEP = 10

VMEM_LIMIT = 56 * 1024 * 1024
ANY = pl.BlockSpec(memory_space=pl.ANY)
VMEM_SPEC = pl.BlockSpec(memory_space=pltpu.VMEM)


def _params(*sem):
    return pltpu.CompilerParams(dimension_semantics=sem, vmem_limit_bytes=VMEM_LIMIT)


def _sigmoid(x):
    return 1.0 / (1.0 + jnp.exp(-x))


def _silu(x):
    return x * _sigmoid(x)


def _dsilu(x):
    s = _sigmoid(x)
    return s * (1.0 + x * (1.0 - s))


_GELU_C = math.sqrt(2.0 / math.pi)


def _gelu(x):
    t = jnp.tanh(_GELU_C * (x + 0.044715 * (x * x * x)))
    return 0.5 * x * (1.0 + t)


def _dgelu(x):
    t = jnp.tanh(_GELU_C * (x + 0.044715 * (x * x * x)))
    return 0.5 * (1.0 + t) + 0.5 * x * (1.0 - t * t) * (_GELU_C * (1.0 + 3.0 * 0.044715 * (x * x)))


def _row(v):
    return v.reshape(1, -1).astype(F32)


def _flat_id(p):
    return 4 * p[0] + 2 * p[1] + p[2]


def _gather_body(n_arr, sum_out):
    def body(*refs):
        x_refs = refs[:n_arr]
        out_refs = refs[n_arr:2 * n_arr]
        pos = 2 * n_arr
        sum_refs = refs[pos:pos + n_arr] if sum_out else ()
        pos += n_arr if sum_out else 0
        send_sems, recv_sems, local_sems = refs[pos:pos + 3]
        x, y, c = lax.axis_index("x"), lax.axis_index("y"), lax.axis_index("c")
        me, sibling = (x, y, c), (x, y, 1 - c)
        chips = [(1 - x, y), (x, 1 - y), (1 - x, 1 - y)]

        def copy(a, k, block, to, src=None):
            rows = out_refs[a].at[_flat_id(block)]
            return pltpu.make_async_remote_copy(
                src_ref=rows if src is None else src, dst_ref=rows,
                send_sem=send_sems.at[a, k], recv_sem=recv_sems.at[a, k],
                device_id=to, device_id_type=MESH)

        sends = []
        mine = []
        for a in range(n_arr):
            cp = pltpu.make_async_copy(x_refs[a], out_refs[a].at[_flat_id(me)], local_sems.at[a])
            cp.start()
            mine.append(cp)
            first = [copy(a, 0, me, sibling, src=x_refs[a])]
            first += [copy(a, 1 + j, me, (*chip, c), src=x_refs[a]) for j, chip in enumerate(chips)]
            for cp in first:
                cp.start()
            sends += first
        for a in range(n_arr):
            for j, chip in enumerate(chips):
                copy(a, 1 + j, (*chip, c), me).wait_recv()
                fwd = copy(a, 4 + j, (*chip, c), sibling)
                fwd.start()
                sends.append(fwd)
        for a in range(n_arr):
            copy(a, 0, sibling, me).wait_recv()
            for j, chip in enumerate(chips):
                copy(a, 4 + j, (*chip, 1 - c), me).wait_recv()
        for cp in sends:
            cp.wait_send()
        for cp in mine:
            cp.wait()
        if sum_out:
            for a in range(n_arr):
                acc = out_refs[a][0]
                for k in range(1, N_DEV):
                    acc = acc + out_refs[a][k]
                sum_refs[a][...] = acc

    return body


def all_gather(arrs, *, name, hbm, sum_out=False):
    n = len(arrs)
    spec = ANY if hbm else VMEM_SPEC
    out_shape = [jax.ShapeDtypeStruct((N_DEV,) + a.shape, a.dtype) for a in arrs]
    out_specs = [spec] * n
    if sum_out:
        out_shape += [jax.ShapeDtypeStruct(a.shape, a.dtype) for a in arrs]
        out_specs += [VMEM_SPEC] * n
    res = pl.pallas_call(
        _gather_body(n, sum_out), name=name,
        out_shape=out_shape, in_specs=[spec] * n, out_specs=out_specs,
        scratch_shapes=[pltpu.SemaphoreType.DMA((n, 7)), pltpu.SemaphoreType.DMA((n, 7)),
                        pltpu.SemaphoreType.DMA((n,))],
        compiler_params=pltpu.CompilerParams(vmem_limit_bytes=VMEM_LIMIT),
    )(*arrs)
    return res


_RELATIONS = [(rx, ry, rc) for rx in (0, 1) for ry in (0, 1) for rc in (0, 1)][1:]


HBM_SPEC = pl.BlockSpec(memory_space=pltpu.HBM)
SEM_SPEC = pl.BlockSpec(memory_space=pltpu.SEMAPHORE)
_DATAFLOW = pltpu.SideEffectType.DATAFLOW_SIDE_EFFECTING
N_PEERS = N_DEV - 1


def _peer_copy(src_ref, land_ref, send_sem, recv_sem, k, rel, scatter, sending):
    x, y, c = lax.axis_index("x"), lax.axis_index("y"), lax.axis_index("c")
    rx, ry, rc = rel
    peer = (1 - x if rx else x, 1 - y if ry else y, 1 - c if rc else c)
    src = src_ref.at[_flat_id(peer)] if scatter else src_ref
    dst = land_ref.at[_flat_id((x, y, c)) if sending else _flat_id(peer)]
    return pltpu.make_async_remote_copy(src_ref=src, dst_ref=dst, send_sem=send_sem.at[k], recv_sem=recv_sem.at[k],
                                        device_id=peer, device_id_type=MESH)


def exchange_start(arrs, *, scatter, name):
    n = len(arrs)
    lands = [lax.empty((N_DEV,) + (a.shape[1:] if scatter else a.shape), a.dtype) for a in arrs]

    def body(*refs):
        srcs, lnds, sems = refs[:n], refs[n:2 * n], refs[2 * n:4 * n]
        token = refs[6 * n]
        for a in range(n):
            for k, rel in enumerate(_RELATIONS):
                _peer_copy(srcs[a], lnds[a], sems[2 * a], sems[2 * a + 1], k, rel, scatter, True).start()
        token[...] = jnp.zeros_like(token)

    outs = pl.pallas_call(
        body, name=name,
        out_shape=[pltpu.SemaphoreType.DMA((N_PEERS,))] * (2 * n)
        + [pltpu.HBM(a.shape, a.dtype) for a in arrs] + [pltpu.HBM(l.shape, l.dtype) for l in lands]
        + [jax.ShapeDtypeStruct((8, 128), F32)],
        in_specs=[HBM_SPEC] * (2 * n),
        out_specs=[SEM_SPEC] * (2 * n) + [HBM_SPEC] * (2 * n) + [VMEM_SPEC],
        input_output_aliases={i: 2 * n + i for i in range(2 * n)},
        compiler_params=pltpu.CompilerParams(has_side_effects=_DATAFLOW),
    )(*[pltpu.with_memory_space_constraint(a, pltpu.HBM) for a in arrs],
      *[pltpu.with_memory_space_constraint(l, pltpu.HBM) for l in lands])
    handles = [(outs[2 * a], outs[2 * a + 1], outs[2 * n + a], outs[3 * n + a]) for a in range(n)]
    return handles, outs[4 * n]


def exchange_wait(handle, after, *, scatter, name):
    send_sem, recv_sem, src, land = handle

    def body(src_ref, land_ref, send_ref, recv_ref, after_ref, src_out, land_out):
        for k, rel in enumerate(_RELATIONS):
            cp = _peer_copy(src_ref, land_ref, send_ref, recv_ref, k, rel, scatter, False)
            cp.wait_send()
            cp.wait_recv()

    outs = pl.pallas_call(
        body, name=name,
        out_shape=[pltpu.HBM(src.shape, src.dtype), pltpu.HBM(land.shape, land.dtype)],
        in_specs=[HBM_SPEC, HBM_SPEC, SEM_SPEC, SEM_SPEC, ANY],
        out_specs=[HBM_SPEC, HBM_SPEC], input_output_aliases={0: 0, 1: 1},
        compiler_params=pltpu.CompilerParams(has_side_effects=_DATAFLOW),
    )(src, land, send_sem, recv_sem, after)
    return outs[0], outs[1]


_DIMS = {"nn": (((1,), (0,)), ((), ())), "nt": (((1,), (1,)), ((), ())), "tn": (((0,), (0,)), ((), ()))}


def matmul(a, b, *, mode, tm, tn, tk, out_dtype, name, n_out=None, res=None, gate=None, add=None, after=None,
           split_out=False):
    if mode == "tn":
        kdim, m = a.shape
    else:
        m, kdim = a.shape
    nfull = b.shape[0] if mode == "nt" else b.shape[1]
    n = nfull if n_out is None else n_out
    tm, tn, tk = min(tm, m), min(tn, n), min(tk, kdim)
    assert m % tm == 0 and n % tn == 0 and kdim % tk == 0, (name, m, n, kdim, tm, tn, tk)
    nk = kdim // tk
    dims = _DIMS[mode]
    a_spec = pl.BlockSpec((tk, tm), lambda j, i, k: (k, i)) if mode == "tn" else pl.BlockSpec((tm, tk), lambda j, i, k: (i, k))
    b_spec = pl.BlockSpec((tn, tk), lambda j, i, k: (j, k)) if mode == "nt" else pl.BlockSpec((tk, tn), lambda j, i, k: (k, j))
    o_spec = pl.BlockSpec((tm, tn), lambda j, i, k: (i, j))
    in_specs = [a_spec, b_spec]
    operands = [a, b]
    aliases = {}
    if res is not None:
        in_specs += [o_spec, pl.BlockSpec((1, tn), lambda j, i, k: (0, j))]
        operands += [res, gate]
    if add is not None:
        in_specs += [o_spec]
        aliases = {len(operands): 0}
        operands += [add]
    if after is not None:
        in_specs += [ANY]
        operands += [after]
    out_cols = n if add is None else add.shape[1]
    out_shape = [jax.ShapeDtypeStruct((m, out_cols), out_dtype)]
    out_specs = [o_spec]
    if split_out:
        out_shape = [jax.ShapeDtypeStruct((n // tn, m, tn), out_dtype)]
        out_specs = [pl.BlockSpec((None, tm, tn), lambda j, i, k: (j, i, 0))]
    if res is not None:
        out_shape.append(jax.ShapeDtypeStruct((m, n), BF16))
        out_specs.append(o_spec)

    def body(*refs):
        a_ref, b_ref = refs[:2]
        pos = 2
        if res is not None:
            res_ref, gate_ref = refs[pos:pos + 2]
            pos += 2
        if add is not None:
            add_ref = refs[pos]
            pos += 1
        if after is not None:
            pos += 1
        o_ref = refs[pos]
        pos += 1
        if res is not None:
            r_ref = refs[pos]
            pos += 1
        acc_ref = refs[pos] if nk > 1 else None
        prod = lax.dot_general(a_ref[...].astype(BF16), b_ref[...].astype(BF16), dims,
                               preferred_element_type=F32)

        def finish(acc):
            if res is not None:
                o_ref[...] = (res_ref[...] + gate_ref[...] * acc).astype(out_dtype)
                r_ref[...] = acc.astype(BF16)
            elif add is not None:
                o_ref[...] = (add_ref[...] + acc).astype(out_dtype)
            else:
                o_ref[...] = acc.astype(out_dtype)

        if nk == 1:
            finish(prod)
        else:
            k = pl.program_id(2)

            @pl.when(k == 0)
            def _():
                acc_ref[...] = prod

            @pl.when(k > 0)
            def _():
                acc_ref[...] += prod

            @pl.when(k == nk - 1)
            def _():
                finish(acc_ref[...])

    outs = pl.pallas_call(
        body, name=name, grid=(n // tn, m // tm, nk),
        in_specs=in_specs, out_specs=out_specs, out_shape=out_shape,
        scratch_shapes=[pltpu.VMEM((tm, tn), F32)] if nk > 1 else [],
        input_output_aliases=aliases,
        compiler_params=_params("parallel", "parallel", "arbitrary"),
    )(*operands)
    return outs if res is not None else outs[0]


def _rows_tile(n, want):
    return min(n, want)


def norm_mod_fwd(x, g, shift, scale, *, name):
    n, d = x.shape
    tt = _rows_tile(n, 256)

    def body(x_ref, g_ref, sh_ref, sc_ref, h_ref):
        xv = x_ref[...]
        rstd = lax.rsqrt(jnp.mean(xv * xv, axis=-1, keepdims=True) + EPS)
        y = xv * rstd * g_ref[...]
        h_ref[...] = (y * (1.0 + sc_ref[...]) + sh_ref[...]).astype(BF16)

    vec = pl.BlockSpec((1, d), lambda i: (0, 0))
    return pl.pallas_call(
        body, name=name, grid=(n // tt,),
        in_specs=[pl.BlockSpec((tt, d), lambda i: (i, 0)), vec, vec, vec],
        out_specs=pl.BlockSpec((tt, d), lambda i: (i, 0)),
        out_shape=jax.ShapeDtypeStruct((n, d), BF16),
        compiler_params=_params("parallel"),
    )(x, _row(g), _row(shift), _row(scale))


GATE_ROW = 3


def norm_mod_bwd(x, dh, dres, g, scale, *, name, branch=None):
    n, d = x.shape
    tt = _rows_tile(n, 256)
    has_res = dres is not None
    has_branch = branch is not None
    last = n // tt - 1

    def body(*refs):
        x_ref, dh_ref = refs[:2]
        pos = 2
        if has_res:
            dres_ref = refs[pos]
            pos += 1
        if has_branch:
            r_ref, gate_ref = refs[pos:pos + 2]
            pos += 2
        g_ref, sc_ref, dx_ref = refs[pos:pos + 3]
        pos += 3
        if has_branch:
            dr_ref = refs[pos]
            pos += 1
        acc_ref, s_ref = refs[pos:pos + 2]
        i = pl.program_id(0)
        xv = x_ref[...]
        dhv = dh_ref[...]
        rstd = lax.rsqrt(jnp.mean(xv * xv, axis=-1, keepdims=True) + EPS)
        xhat = xv * rstd
        dxhat = dhv * (g_ref[...] * (1.0 + sc_ref[...]))
        dx = rstd * (dxhat - xhat * jnp.mean(dxhat * xhat, axis=-1, keepdims=True))
        if has_res:
            dx = dx + dres_ref[...]
        dx_ref[...] = dx

        @pl.when(i == 0)
        def _():
            s_ref[...] = jnp.zeros_like(s_ref)

        s_ref[0:1, :] += jnp.sum(dhv, axis=0, keepdims=True)
        s_ref[1:2, :] += jnp.sum(dhv * xhat, axis=0, keepdims=True)
        if has_branch:
            dr_ref[...] = (dx * gate_ref[...]).astype(BF16)
            s_ref[2:3, :] += jnp.sum(dx * r_ref[...].astype(F32), axis=0, keepdims=True)

        @pl.when(i == last)
        def _():
            s1 = s_ref[0:1, :]
            s2 = s_ref[1:2, :]
            acc_ref[...] = jnp.zeros_like(acc_ref)
            acc_ref[0:1, :] = s1
            acc_ref[1:2, :] = s2 * g_ref[...]
            acc_ref[2:3, :] = s2 * (1.0 + sc_ref[...])
            acc_ref[GATE_ROW:GATE_ROW + 1, :] = s_ref[2:3, :]

    vec = pl.BlockSpec((1, d), lambda i: (0, 0))
    big = pl.BlockSpec((tt, d), lambda i: (i, 0))
    ops = [x, dh] + ([dres] if has_res else []) + ([branch[0], _row(branch[1])] if has_branch else [])
    ops += [_row(g), _row(scale)]
    return pl.pallas_call(
        body, name=name, grid=(n // tt,),
        in_specs=[big, big] + ([big] if has_res else []) + ([big, vec] if has_branch else []) + [vec, vec],
        out_specs=[big] + ([big] if has_branch else []) + [pl.BlockSpec((8, d), lambda i: (0, 0))],
        out_shape=[jax.ShapeDtypeStruct((n, d), F32)] + ([jax.ShapeDtypeStruct((n, d), BF16)] if has_branch else [])
        + [jax.ShapeDtypeStruct((8, d), F32)],
        scratch_shapes=[pltpu.VMEM((8, d), F32)],
        compiler_params=_params("arbitrary"),
    )(*ops)


def final_loss(x, target, g, r, gate, *, name):
    n, d = x.shape
    tt = _rows_tile(n, 256)

    def body(x_ref, t_ref, g_ref, r_ref, gate_ref, dx_ref, dr_ref, acc_ref, loss_ref):
        i = pl.program_id(0)
        xv = x_ref[...]
        rstd = lax.rsqrt(jnp.mean(xv * xv, axis=-1, keepdims=True) + EPS)
        xhat = xv * rstd
        e = xhat * g_ref[...] - t_ref[...]
        dy = e * (1.0 / d)
        dxhat = dy * g_ref[...]
        dx = rstd * (dxhat - xhat * jnp.mean(dxhat * xhat, axis=-1, keepdims=True))
        dx_ref[...] = dx
        dr_ref[...] = (dx * gate_ref[...]).astype(BF16)

        @pl.when(i == 0)
        def _():
            acc_ref[...] = jnp.zeros_like(acc_ref)
            loss_ref[...] = jnp.zeros_like(loss_ref)

        acc_ref[0:1, :] += jnp.sum(dy * xhat, axis=0, keepdims=True)
        acc_ref[GATE_ROW:GATE_ROW + 1, :] += jnp.sum(dx * r_ref[...].astype(F32), axis=0, keepdims=True)
        part = 0.5 * jnp.sum(jnp.mean(e * e, axis=-1, keepdims=True), axis=0, keepdims=True)
        loss_ref[...] += jnp.broadcast_to(part, loss_ref.shape)

    big = pl.BlockSpec((tt, d), lambda i: (i, 0))
    vec = pl.BlockSpec((1, d), lambda i: (0, 0))
    return pl.pallas_call(
        body, name=name, grid=(n // tt,),
        in_specs=[big, big, vec, big, vec],
        out_specs=[big, big, pl.BlockSpec((8, d), lambda i: (0, 0)), pl.BlockSpec((8, 128), lambda i: (0, 0))],
        out_shape=[jax.ShapeDtypeStruct((n, d), F32), jax.ShapeDtypeStruct((n, d), BF16),
                   jax.ShapeDtypeStruct((8, d), F32), jax.ShapeDtypeStruct((8, 128), F32)],
        compiler_params=_params("arbitrary"),
    )(x, target, _row(g), r, _row(gate))


def _as_row(col):
    t = col.shape[0]
    return jnp.transpose(jnp.broadcast_to(col, (t, HEAD_DIM)))[0:1, :]


def _swap_pairs(x):
    lane = lax.broadcasted_iota(jnp.int32, x.shape, 1)
    return jnp.where(lane % 2 == 0, pltpu.roll(x, HEAD_DIM - 1, 1), pltpu.roll(x, 1, 1))


def rope_tables(n):
    rows = n // GRID_W
    row = jnp.repeat(jnp.arange(rows, dtype=F32), GRID_W)
    col = jnp.tile(jnp.arange(GRID_W, dtype=F32), rows)
    n_freq = HEAD_DIM // 4
    inv = jnp.power(ROPE_THETA, jnp.arange(n_freq, dtype=F32) * (-2.0 / (HEAD_DIM // 2)))
    ang = jnp.concatenate([row[:, None] * inv, col[:, None] * inv], axis=-1)
    cos, sin = jnp.cos(ang), jnp.sin(ang)
    cexp = jnp.repeat(cos, 2, axis=-1)
    sexp = jnp.stack([-sin, sin], axis=-1).reshape(n, HEAD_DIM)
    return cexp, sexp


V_EXT_W = 2 * HEAD_DIM


def qkv_prep_fwd(p, wq, wk, cexp, sexp, *, latent, name, rows_all=None, k_all=None, v_ext=None):
    n = p.shape[0]
    tt = _rows_tile(n, 256)
    width = 2 * KV_W + (ATTN_W if latent else 0)

    def body(*refs):
        if latent:
            p_ref, wq_ref, wk_ref, c_ref, s_ref, q_ref, k_ref, v_ref = refs
        else:
            p_ref, wk_ref, _, _, k_ref, v_ref = refs

        def head(xv, w):
            rstd = lax.rsqrt(jnp.mean(xv * xv, axis=-1, keepdims=True) + EPS)
            yv = xv * rstd * w
            if latent:
                yv = yv * c_ref[...] + _swap_pairs(yv) * s_ref[...]
            return yv

        for h in range(N_KV_HEADS):
            sl = slice(h * HEAD_DIM, (h + 1) * HEAD_DIM)
            k_ref[:, sl] = head(p_ref[:, sl], wk_ref[...]).astype(BF16)
            v_ref[:, h * V_EXT_W:h * V_EXT_W + HEAD_DIM] = p_ref[:, KV_W + h * HEAD_DIM:KV_W + (h + 1) * HEAD_DIM].astype(BF16)
            lane = lax.broadcasted_iota(jnp.int32, (tt, HEAD_DIM), 1)
            v_ref[:, h * V_EXT_W + HEAD_DIM:(h + 1) * V_EXT_W] = jnp.where(lane == 0, 1.0, 0.0).astype(BF16)
        if latent:
            for h in range(N_Q_HEADS):
                sl = slice(2 * KV_W + h * HEAD_DIM, 2 * KV_W + (h + 1) * HEAD_DIM)
                q_ref[:, h * HEAD_DIM:(h + 1) * HEAD_DIM] = (head(p_ref[:, sl], wq_ref[...]) * Q_SCALE).astype(BF16)

    vec = pl.BlockSpec((1, HEAD_DIM), lambda i: (0, 0))
    tab = pl.BlockSpec((tt, HEAD_DIM), lambda i: (i, 0))
    vw = N_KV_HEADS * V_EXT_W
    k_shape = jax.ShapeDtypeStruct((rows_all, KV_W), BF16)
    v_shape = jax.ShapeDtypeStruct((rows_all, vw), BF16)
    aliases = {}
    if latent:
        in_specs = [pl.BlockSpec((tt, width), lambda i: (i, 0)), vec, vec, tab, tab]
        ops = [p, _row(wq), _row(wk), cexp, sexp]
        out_specs = [pl.BlockSpec((tt, ATTN_W), lambda i: (i, 0)), pl.BlockSpec((tt, KV_W), lambda i: (i, 0)),
                     pl.BlockSpec((tt, vw), lambda i: (i, 0))]
        out_shape = [jax.ShapeDtypeStruct((n, ATTN_W), BF16), k_shape, v_shape]
    else:
        assert n == tt and (rows_all - n) % n == 0, (n, tt, rows_all)
        first = (rows_all - n) // n
        in_specs = [pl.BlockSpec((tt, width), lambda i: (i, 0)), vec, ANY, ANY]
        ops = [p, _row(wk), k_all, v_ext]
        out_specs = [pl.BlockSpec((tt, KV_W), lambda i: (first, 0)), pl.BlockSpec((tt, vw), lambda i: (first, 0))]
        out_shape = [k_shape, v_shape]
        aliases = {2: 0, 3: 1}
    return pl.pallas_call(
        body, name=name, grid=(n // tt,), in_specs=in_specs, out_specs=out_specs, out_shape=out_shape,
        input_output_aliases=aliases, compiler_params=_params("parallel"),
    )(*ops)


def qkv_prep_bwd(p, dq, dk, dv, wq, wk, cexp, sexp, *, latent, name):
    n = p.shape[0]
    tt = _rows_tile(n, 256)
    width = 2 * KV_W + (ATTN_W if latent else 0)

    def body(*refs):
        if latent:
            p_ref, dq_ref, dk_ref, dv_ref, wq_ref, wk_ref, c_ref, s_ref, dp_ref, acc_ref = refs
        else:
            p_ref, dk_ref, dv_ref, wk_ref, dp_ref, acc_ref = refs
        i = pl.program_id(0)

        @pl.when(i == 0)
        def _():
            acc_ref[...] = jnp.zeros_like(acc_ref)

        def head(xv, dy, w, row):
            if latent:
                dy = dy * c_ref[...] + _swap_pairs(dy * s_ref[...])
            rstd = lax.rsqrt(jnp.mean(xv * xv, axis=-1, keepdims=True) + EPS)
            xhat = xv * rstd
            acc_ref[row:row + 1, :] += jnp.sum(dy * xhat, axis=0, keepdims=True)
            dxhat = dy * w
            return rstd * (dxhat - xhat * jnp.mean(dxhat * xhat, axis=-1, keepdims=True))

        for h in range(N_KV_HEADS):
            sl = slice(h * HEAD_DIM, (h + 1) * HEAD_DIM)
            dp_ref[:, sl] = head(p_ref[:, sl], dk_ref[:, sl], wk_ref[...], 1).astype(BF16)
        dp_ref[:, KV_W:2 * KV_W] = dv_ref[...].astype(BF16)
        if latent:
            for h in range(N_Q_HEADS):
                sl = slice(2 * KV_W + h * HEAD_DIM, 2 * KV_W + (h + 1) * HEAD_DIM)
                dyq = dq_ref[:, h * HEAD_DIM:(h + 1) * HEAD_DIM] * Q_SCALE
                dp_ref[:, sl] = head(p_ref[:, sl], dyq, wq_ref[...], 0).astype(BF16)

    vec = pl.BlockSpec((1, HEAD_DIM), lambda i: (0, 0))
    tab = pl.BlockSpec((tt, HEAD_DIM), lambda i: (i, 0))
    kv_spec = pl.BlockSpec((tt, KV_W), lambda i: (i, 0))
    p_spec = pl.BlockSpec((tt, width), lambda i: (i, 0))
    if latent:
        in_specs = [p_spec, pl.BlockSpec((tt, ATTN_W), lambda i: (i, 0)), kv_spec, kv_spec, vec, vec, tab, tab]
        ops = [p, dq, dk, dv, _row(wq), _row(wk), cexp, sexp]
    else:
        in_specs = [p_spec, kv_spec, kv_spec, vec]
        ops = [p, dk, dv, _row(wk)]
    return pl.pallas_call(
        body, name=name, grid=(n // tt,), in_specs=in_specs,
        out_specs=[p_spec, pl.BlockSpec((8, HEAD_DIM), lambda i: (0, 0))],
        out_shape=[jax.ShapeDtypeStruct((n, width), BF16), jax.ShapeDtypeStruct((8, HEAD_DIM), F32)],
        compiler_params=_params("arbitrary"),
    )(*ops)


def _kv_chunks(n, s_all):
    step = 1024 if n % 1024 == 0 else 256
    chunks = [(s, step) for s in range(0, n, step)]
    if s_all > n:
        chunks.append((n, s_all - n))
    return chunks


def flash_fwd(q, k_all, v_ext, p, *, za_block, mix_width, name):
    n = q.shape[0]
    s_all = k_all.shape[0]
    tq = _rows_tile(n, 512)
    chunks = _kv_chunks(n, s_all)
    wide = 2 * HEAD_DIM

    def body(q_ref, k_ref, v_ref, za_ref, o_ref, mix_ref, lse_ref, m_ref, acc_ref):
        qv = q_ref[...]
        m_ref[...] = jnp.full_like(m_ref, -jnp.inf)
        acc_ref[...] = jnp.zeros_like(acc_ref)
        for start, size in chunks:
            kc = k_ref[pl.ds(start, size), :]
            vc = v_ref[pl.ds(start, size), :]
            s = lax.dot_general(qv, kc, _DIMS["nt"], preferred_element_type=F32)
            m_old = m_ref[...]
            m_new = jnp.maximum(m_old, jnp.max(s, axis=-1, keepdims=True))
            pr = jnp.exp2(s - m_new)
            alpha = jnp.exp2(m_old - m_new)
            acc_ref[...] = alpha * acc_ref[...] + jnp.dot(pr.astype(BF16), vc, preferred_element_type=F32)
            m_ref[...] = m_new
        acc = acc_ref[...]
        denom = acc[:, HEAD_DIM:HEAD_DIM + 1]
        o = acc[:, :HEAD_DIM] / denom
        o_ref[...] = o.astype(BF16)
        mix_ref[...] = (o * _silu(za_ref[...])).astype(BF16)
        lse_ref[0] = _as_row(m_ref[...] + jnp.log2(denom))

    qspec = pl.BlockSpec((tq, HEAD_DIM), lambda h, i: (i, h))
    return pl.pallas_call(
        body, name=name, grid=(N_Q_HEADS, n // tq),
        in_specs=[qspec, pl.BlockSpec((s_all, HEAD_DIM), lambda h, i: (0, h // Q_PER_KV)),
                  pl.BlockSpec((s_all, wide), lambda h, i: (0, h // Q_PER_KV)),
                  pl.BlockSpec((tq, HEAD_DIM), lambda h, i: (i, za_block + h))],
        out_specs=[qspec, qspec, pl.BlockSpec((1, 1, tq), lambda h, i: (h, 0, i))],
        out_shape=[jax.ShapeDtypeStruct((n, ATTN_W), BF16), jax.ShapeDtypeStruct((n, mix_width), BF16),
                   jax.ShapeDtypeStruct((N_Q_HEADS, 1, n), F32)],
        scratch_shapes=[pltpu.VMEM((tq, 1), F32), pltpu.VMEM((tq, wide), F32)],
        compiler_params=_params("parallel", "parallel"),
    )(q, k_all, v_ext, p)


def attn_gate_bwd(dmix, o, p, *, za_block, name):
    n = o.shape[0]
    tt = _rows_tile(n, 512)
    za_half = za_block * HEAD_DIM // HALF
    n_half = ATTN_W // HALF

    def body(*refs):
        dm_ref, o_ref = refs[:2]
        za_refs = refs[2:2 + n_half]
        do_ref, dos_ref, dza_ref, delta_ref = refs[2 + n_half:]
        dm = dm_ref[...]
        ov = o_ref[...].astype(F32)
        za = jnp.concatenate([r[...] for r in za_refs], axis=-1)
        do = dm * _silu(za)
        do_ref[...] = do.astype(BF16)
        dos_ref[...] = (do * LN2).astype(BF16)
        dza_ref[...] = (dm * ov * _dsilu(za)).astype(BF16)
        prod = do * ov
        for h in range(N_Q_HEADS):
            col = jnp.sum(prod[:, h * HEAD_DIM:(h + 1) * HEAD_DIM], axis=-1, keepdims=True) * LN2
            delta_ref[h] = _as_row(col)

    spec = pl.BlockSpec((tt, ATTN_W), lambda i: (i, 0))
    shape = jax.ShapeDtypeStruct((n, ATTN_W), BF16)
    za_specs = [pl.BlockSpec((tt, HALF), functools.partial(lambda i, cb: (i, cb), cb=za_half + h))
                for h in range(n_half)]
    return pl.pallas_call(
        body, name=name, grid=(n // tt,),
        in_specs=[spec, spec] + za_specs,
        out_specs=[spec, spec, spec, pl.BlockSpec((N_Q_HEADS, 1, tt), lambda i: (0, 0, i))],
        out_shape=[shape, shape, shape, jax.ShapeDtypeStruct((N_Q_HEADS, 1, n), F32)],
        compiler_params=_params("parallel"),
    )(dmix, o, *([p] * n_half))


def flash_bwd(q, do, do_s, lse_row, delta_row, k_all, v_ext, *, name):
    n = q.shape[0]
    s_all = k_all.shape[0]
    tq = _rows_tile(n, 512)
    chunks = _kv_chunks(n, s_all)

    def body(q_ref, do_ref, dos_ref, lse_ref, dl_ref, k_ref, v_ref, dq_ref, dk_ref, dv_ref):
        g = pl.program_id(1)
        i = pl.program_id(2)

        @pl.when((g == 0) & (i == 0))
        def _():
            dk_ref[...] = jnp.zeros_like(dk_ref)
            dv_ref[...] = jnp.zeros_like(dv_ref)

        qv = q_ref[...]
        dov = do_ref[...]
        dosv = dos_ref[...]
        lse = lse_ref[0]
        dl = dl_ref[0]
        dq = jnp.zeros((tq, HEAD_DIM), F32)
        for start, size in chunks:
            kc = k_ref[pl.ds(start, size), :]
            vc = v_ref[pl.ds(start, size), :]
            st = lax.dot_general(kc, qv, _DIMS["nt"], preferred_element_type=F32)
            pt = jnp.exp2(st - lse)
            dpt = lax.dot_general(vc, dosv, _DIMS["nt"], preferred_element_type=F32)
            dst = (pt * (dpt - dl)).astype(BF16)
            dv_ref[pl.ds(start, size), :] += jnp.dot(pt.astype(BF16), dov, preferred_element_type=F32)
            dk_ref[pl.ds(start, size), :] += jnp.dot(dst, qv, preferred_element_type=F32)
            dq = dq + lax.dot_general(dst, kc, _DIMS["tn"], preferred_element_type=F32)
        dq_ref[...] = dq

    qspec = pl.BlockSpec((tq, HEAD_DIM), lambda kh, g, i: (i, kh * Q_PER_KV + g))
    rowspec = pl.BlockSpec((1, 1, tq), lambda kh, g, i: (kh * Q_PER_KV + g, 0, i))
    kvspec = pl.BlockSpec((s_all, HEAD_DIM), lambda kh, g, i: (0, kh))
    return pl.pallas_call(
        body, name=name, grid=(N_KV_HEADS, Q_PER_KV, n // tq),
        in_specs=[qspec, qspec, qspec, rowspec, rowspec, kvspec,
                  pl.BlockSpec((s_all, HEAD_DIM), lambda kh, g, i: (0, kh * (V_EXT_W // HEAD_DIM)))],
        out_specs=[qspec, kvspec, kvspec],
        out_shape=[jax.ShapeDtypeStruct((n, ATTN_W), F32),
                   jax.ShapeDtypeStruct((s_all, KV_W), F32), jax.ShapeDtypeStruct((s_all, KV_W), F32)],
        compiler_params=_params("arbitrary", "arbitrary", "arbitrary"),
    )(q, do, do_s, lse_row, delta_row, k_all, v_ext)


HALF = 512


SUB = 128


def _tap_group(w_ref, w_lanes, ext_ref, r0, lanes, r, tap_of):
    z = None
    for a in range(4):
        k = tap_of(8 * a + r)
        if 0 <= k < CONV_WIDTH:
            term = w_ref[k:k + 1, w_lanes] * ext_ref[pl.ds(r0 + 8 * a, SUB + 8), lanes]
            z = term if z is None else z + term
    return z


def _shifted(z, z_ref, r):
    if r == 0:
        return z[0:SUB]
    z_ref[...] = z
    return z_ref[pl.ds(r, SUB), :]


def _fold8(x):
    return jnp.sum(x.reshape(x.shape[0] // 8, 8, x.shape[1]), axis=0)


def _pieces(tt, h):
    return [(rh * SUB, slice(c * SUB, (c + 1) * SUB), slice(h * HALF + c * SUB, h * HALF + (c + 1) * SUB))
            for c in range(HALF // SUB) for rh in range(tt // SUB)]


def _halo_specs(tt, n, cb):
    per = tt // HALO
    last = n // HALO - 1
    return [pl.BlockSpec((HALO, HALF), lambda i: (jnp.maximum(i * per - 1, 0), cb)),
            pl.BlockSpec((tt, HALF), lambda i: (i, cb)),
            pl.BlockSpec((HALO, HALF), lambda i: (jnp.minimum((i + 1) * per, last), cb))]


def conv_fwd(p, dw_w, dw_b, ln_g, ln_b, mix, *, glu_block, name):
    n = p.shape[0]
    ch = dw_w.shape[1]
    nh = ch // HALF
    tt = _rows_tile(n, 256)
    last = n // tt - 1

    def body(*refs):
        a_refs = [refs[3 * h:3 * h + 3] for h in range(nh)]
        b_refs = [refs[3 * (nh + h):3 * (nh + h) + 3] for h in range(nh)]
        pos = 6 * nh
        zb_refs = refs[pos:pos + nh]
        pos += nh
        w_ref, bias_ref, g_ref, be_ref, _, ycv_ref, mix_ref, ext_ref, z_ref = refs[pos:pos + 9]
        i = pl.program_id(0)
        for h in range(nh):
            cs = slice(h * HALF, (h + 1) * HALF)
            ap, am, an = a_refs[h]
            bp, bm, bn = b_refs[h]
            ext_ref[0:HALO, :] = jnp.where(i > 0, ap[...] * _sigmoid(bp[...]), 0.0)
            ext_ref[HALO:HALO + tt, :] = am[...] * _sigmoid(bm[...])
            ext_ref[HALO + tt:2 * HALO + tt, :] = jnp.where(i < last, an[...] * _sigmoid(bn[...]), 0.0)
            for r0, lanes, w_lanes in _pieces(tt, h):
                acc = jnp.broadcast_to(bias_ref[:, w_lanes], (SUB, SUB))
                for r in range(8):
                    z = _tap_group(w_ref, w_lanes, ext_ref, r0, lanes, r, lambda j: j - 1)
                    acc = acc + _shifted(z, z_ref, r)
                ycv_ref[pl.ds(r0, SUB), w_lanes] = acc
        yc = ycv_ref[...]
        mu = jnp.mean(yc, axis=-1, keepdims=True)
        var = jnp.mean(jnp.square(yc - mu), axis=-1, keepdims=True)
        ln = (yc - mu) * lax.rsqrt(var + EPS) * g_ref[...] + be_ref[...]
        out = _silu(ln)
        for h in range(nh):
            cs = slice(h * HALF, (h + 1) * HALF)
            mix_ref[:, cs] = (out[:, cs] * _silu(zb_refs[h][...])).astype(BF16)

    in_specs = []
    for h in range(2 * nh):
        in_specs += _halo_specs(tt, n, glu_block + h)
    in_specs += [pl.BlockSpec((tt, HALF), functools.partial(lambda i, cb: (i, cb), cb=glu_block + 2 * nh + h))
                 for h in range(nh)]
    vec = pl.BlockSpec((1, ch), lambda i: (0, 0))
    in_specs += [pl.BlockSpec((CONV_WIDTH, ch), lambda i: (0, 0)), vec, vec, vec, ANY]
    ops = [p] * (6 * nh + nh) + [dw_w, _row(dw_b), _row(ln_g), _row(ln_b), mix]
    big = pl.BlockSpec((tt, ch), lambda i: (i, 0))
    mix_block = (mix.shape[1] - ch) // ch
    return pl.pallas_call(
        body, name=name, grid=(n // tt,), in_specs=in_specs,
        out_specs=[big, pl.BlockSpec((tt, ch), lambda i: (i, mix_block))],
        out_shape=[jax.ShapeDtypeStruct((n, ch), F32), jax.ShapeDtypeStruct(mix.shape, BF16)],
        input_output_aliases={len(ops) - 1: 1},
        scratch_shapes=[pltpu.VMEM((tt + 2 * HALO, HALF), F32), pltpu.VMEM((SUB + 8, SUB), F32)],
        compiler_params=_params("parallel"),
    )(*ops)


def conv_bwd_rows(dmix, ycv, p, ln_g, ln_b, *, mix_block, zb_block, name):
    n, ch = ycv.shape
    nh = ch // HALF
    tt = _rows_tile(n, 256)

    def body(*refs):
        dm_ref, ycv_ref = refs[:2]
        zb_refs = refs[2:2 + nh]
        g_ref, be_ref, dy_ref, dzb_ref, acc_ref = refs[2 + nh:]
        i = pl.program_id(0)

        @pl.when(i == 0)
        def _():
            acc_ref[...] = jnp.zeros_like(acc_ref)

        yc = ycv_ref[...]
        mu = jnp.mean(yc, axis=-1, keepdims=True)
        var = jnp.mean(jnp.square(yc - mu), axis=-1, keepdims=True)
        rstd = lax.rsqrt(var + EPS)
        xhat = (yc - mu) * rstd
        ln = xhat * g_ref[...] + be_ref[...]
        out = _silu(ln)
        dm = dm_ref[...]
        zb = jnp.concatenate([r[...] for r in zb_refs], axis=-1)
        dzb_ref[...] = (dm * out * _dsilu(zb)).astype(BF16)
        dln = dm * _silu(zb) * _dsilu(ln)
        acc_ref[0:1, :] += jnp.sum(dln * xhat, axis=0, keepdims=True)
        acc_ref[1:2, :] += jnp.sum(dln, axis=0, keepdims=True)
        dxhat = dln * g_ref[...]
        dy_ref[...] = rstd * (dxhat - jnp.mean(dxhat, axis=-1, keepdims=True)
                              - xhat * jnp.mean(dxhat * xhat, axis=-1, keepdims=True))

    big = pl.BlockSpec((tt, ch), lambda i: (i, 0))
    vec = pl.BlockSpec((1, ch), lambda i: (0, 0))
    in_specs = [pl.BlockSpec((tt, ch), lambda i: (i, mix_block)), big]
    in_specs += [pl.BlockSpec((tt, HALF), functools.partial(lambda i, cb: (i, cb), cb=zb_block + h)) for h in range(nh)]
    in_specs += [vec, vec]
    return pl.pallas_call(
        body, name=name, grid=(n // tt,), in_specs=in_specs,
        out_specs=[big, big, pl.BlockSpec((8, ch), lambda i: (0, 0))],
        out_shape=[jax.ShapeDtypeStruct((n, ch), F32), jax.ShapeDtypeStruct((n, ch), BF16),
                   jax.ShapeDtypeStruct((8, ch), F32)],
        compiler_params=_params("arbitrary"),
    )(dmix, ycv, *([p] * nh), _row(ln_g), _row(ln_b))


def conv_bwd_taps(dycv, p, dw_w, *, glu_block, name):
    n, ch = dycv.shape
    nh = ch // HALF
    tt = _rows_tile(n, 256)
    last = n // tt - 1

    def body(*refs):
        d_refs = [refs[3 * h:3 * h + 3] for h in range(nh)]
        a_refs = [refs[3 * (nh + h):3 * (nh + h) + 3] for h in range(nh)]
        b_refs = [refs[3 * (2 * nh + h):3 * (2 * nh + h) + 3] for h in range(nh)]
        w_ref, dglu_ref, dw_ref, db_ref, yext_ref, dext_ref, z_ref, dy_ref, dwp_ref, dbp_ref = refs[9 * nh:]
        i = pl.program_id(0)

        @pl.when(i == 0)
        def _():
            dwp_ref[...] = jnp.zeros_like(dwp_ref)
            dbp_ref[...] = jnp.zeros_like(dbp_ref)

        for h in range(nh):
            cs = slice(h * HALF, (h + 1) * HALF)
            ap, am, an = a_refs[h]
            bp, bm, bn = b_refs[h]
            dp, dm, dn = d_refs[h]
            av = am[...]
            sb = _sigmoid(bm[...])
            yext_ref[0:HALO, :] = jnp.where(i > 0, ap[...] * _sigmoid(bp[...]), 0.0)
            yext_ref[HALO:HALO + tt, :] = av * sb
            yext_ref[HALO + tt:2 * HALO + tt, :] = jnp.where(i < last, an[...] * _sigmoid(bn[...]), 0.0)
            dmain = dm[...]
            dext_ref[0:HALO, :] = jnp.where(i > 0, dp[...], 0.0)
            dext_ref[HALO:HALO + tt, :] = dmain
            dext_ref[HALO + tt:2 * HALO + tt, :] = jnp.where(i < last, dn[...], 0.0)
            for r0, lanes, w_lanes in _pieces(tt, h):
                dsub = dext_ref[pl.ds(HALO + r0, SUB), lanes]
                dy = jnp.zeros((SUB, SUB), F32)
                for r in range(8):
                    z = _tap_group(w_ref, w_lanes, dext_ref, r0, lanes, r, lambda j: CONV_WIDTH - j)
                    dy = dy + _shifted(z, z_ref, r)
                    y_r = yext_ref[pl.ds(r0 + r, SUB + 24), lanes]
                    for a in range(4):
                        k = 8 * a + r - 1
                        if k >= 0:
                            dwp_ref[8 * k:8 * k + 8, w_lanes] += _fold8(dsub * y_r[8 * a:8 * a + SUB])
                dbp_ref[:, w_lanes] += _fold8(dsub)
                dy_ref[pl.ds(r0, SUB), lanes] = dy
            dy = dy_ref[...]
            dglu_ref[:, cs] = (dy * sb).astype(BF16)
            dglu_ref[:, ch + h * HALF:ch + (h + 1) * HALF] = (dy * av * sb * (1.0 - sb)).astype(BF16)

        @pl.when(i == last)
        def _():
            dw_ref[...] = jnp.sum(dwp_ref[...].reshape(32, 8, ch), axis=1)
            db_ref[...] = jnp.broadcast_to(jnp.sum(dbp_ref[...], axis=0, keepdims=True), db_ref.shape)

    in_specs = []
    for h in range(nh):
        in_specs += _halo_specs(tt, n, h)
    for h in range(2 * nh):
        in_specs += _halo_specs(tt, n, glu_block + h)
    in_specs += [pl.BlockSpec((CONV_WIDTH, ch), lambda i: (0, 0))]
    ops = [dycv] * (3 * nh) + [p] * (6 * nh) + [dw_w]
    return pl.pallas_call(
        body, name=name, grid=(n // tt,), in_specs=in_specs,
        out_specs=[pl.BlockSpec((tt, 2 * ch), lambda i: (i, 0)), pl.BlockSpec((32, ch), lambda i: (0, 0)),
                   pl.BlockSpec((8, ch), lambda i: (0, 0))],
        out_shape=[jax.ShapeDtypeStruct((n, 2 * ch), BF16), jax.ShapeDtypeStruct((32, ch), F32),
                   jax.ShapeDtypeStruct((8, ch), F32)],
        scratch_shapes=[pltpu.VMEM((tt + 2 * HALO, HALF), F32), pltpu.VMEM((tt + 2 * HALO, HALF), F32),
                        pltpu.VMEM((SUB + 8, SUB), F32), pltpu.VMEM((tt, HALF), F32),
                        pltpu.VMEM((8 * 32, ch), F32), pltpu.VMEM((8, ch), F32)],
        compiler_params=_params("arbitrary"),
    )(*ops)


def _sgu_common(p_ref, g_ref, be_ref, ws_ref, bs_ref, w):
    gw = w // SGU_GROUPS
    u_pre = p_ref[:, 0:w]
    v_pre = p_ref[:, w:2 * w]
    zc = p_ref[:, 2 * w:3 * w]
    u = _gelu(u_pre)
    v = _gelu(v_pre)
    mu = jnp.mean(v, axis=-1, keepdims=True)
    var = jnp.mean(jnp.square(v - mu), axis=-1, keepdims=True)
    rstd = lax.rsqrt(var + EPS)
    vhat = (v - mu) * rstd
    vn = (vhat * g_ref[...] + be_ref[...]).astype(BF16)
    mixed = jnp.concatenate(
        [jnp.dot(ws_ref[gi].astype(BF16), vn[:, gi * gw:(gi + 1) * gw], preferred_element_type=F32)
         + bs_ref[:, gi:gi + 1] for gi in range(SGU_GROUPS)], axis=-1)
    return u_pre, v_pre, zc, u, rstd, vhat, vn, mixed


def sgu_fwd(p, ln_g, ln_b, ws, bs_t, *, name):
    n, w3 = p.shape
    w = w3 // 3

    def body(p_ref, g_ref, be_ref, ws_ref, bs_ref, m_ref):
        _, _, zc, u, _, _, _, mixed = _sgu_common(p_ref, g_ref, be_ref, ws_ref, bs_ref, w)
        m_ref[...] = (u * mixed * _silu(zc)).astype(BF16)

    vec = pl.BlockSpec((1, w), lambda i: (0, 0))
    return pl.pallas_call(
        body, name=name, grid=(n // CHUNK,),
        in_specs=[pl.BlockSpec((CHUNK, w3), lambda i: (i, 0)), vec, vec,
                  pl.BlockSpec((SGU_GROUPS, CHUNK, CHUNK), lambda i: (0, 0, 0)),
                  pl.BlockSpec((CHUNK, SGU_GROUPS), lambda i: (0, 0))],
        out_specs=pl.BlockSpec((CHUNK, w), lambda i: (i, 0)),
        out_shape=jax.ShapeDtypeStruct((n, w), BF16),
        compiler_params=_params("parallel"),
    )(p, _row(ln_g), _row(ln_b), ws, bs_t)


def sgu_bwd(p, dm, ln_g, ln_b, ws, ws_t, bs_t, *, name):
    n, w3 = p.shape
    w = w3 // 3
    gw = w // SGU_GROUPS

    def body(p_ref, dm_ref, g_ref, be_ref, ws_ref, wst_ref, bs_ref, dp_ref, dws_ref, dbs_ref, acc_ref):
        i = pl.program_id(0)

        @pl.when(i == 0)
        def _():
            dws_ref[...] = jnp.zeros_like(dws_ref)
            dbs_ref[...] = jnp.zeros_like(dbs_ref)
            acc_ref[...] = jnp.zeros_like(acc_ref)

        u_pre, v_pre, zc, u, rstd, vhat, vn, mixed = _sgu_common(p_ref, g_ref, be_ref, ws_ref, bs_ref, w)
        dmv = dm_ref[...]
        um = u * mixed
        dp_ref[:, 2 * w:3 * w] = (dmv * um * _dsilu(zc)).astype(BF16)
        dum = dmv * _silu(zc)
        dp_ref[:, 0:w] = (dum * mixed * _dgelu(u_pre)).astype(BF16)
        dmixed = dum * u
        dmixed_b = dmixed.astype(BF16)
        dvn_parts = []
        for gi in range(SGU_GROUPS):
            cs = slice(gi * gw, (gi + 1) * gw)
            dws_ref[gi] += lax.dot_general(dmixed_b[:, cs], vn[:, cs], _DIMS["nt"], preferred_element_type=F32)
            dbs_ref[:, gi:gi + 1] += jnp.sum(dmixed[:, cs], axis=-1, keepdims=True)
            dvn_parts.append(jnp.dot(wst_ref[gi].astype(BF16), dmixed_b[:, cs], preferred_element_type=F32))
        dvn = jnp.concatenate(dvn_parts, axis=-1)
        acc_ref[0:1, :] += jnp.sum(dvn * vhat, axis=0, keepdims=True)
        acc_ref[1:2, :] += jnp.sum(dvn, axis=0, keepdims=True)
        dvhat = dvn * g_ref[...]
        dv = rstd * (dvhat - jnp.mean(dvhat, axis=-1, keepdims=True)
                     - vhat * jnp.mean(dvhat * vhat, axis=-1, keepdims=True))
        dp_ref[:, w:2 * w] = (dv * _dgelu(v_pre)).astype(BF16)

    vec = pl.BlockSpec((1, w), lambda i: (0, 0))
    wspec = pl.BlockSpec((SGU_GROUPS, CHUNK, CHUNK), lambda i: (0, 0, 0))
    bspec = pl.BlockSpec((CHUNK, SGU_GROUPS), lambda i: (0, 0))
    return pl.pallas_call(
        body, name=name, grid=(n // CHUNK,),
        in_specs=[pl.BlockSpec((CHUNK, w3), lambda i: (i, 0)), pl.BlockSpec((CHUNK, w), lambda i: (i, 0)),
                  vec, vec, wspec, wspec, bspec],
        out_specs=[pl.BlockSpec((CHUNK, w3), lambda i: (i, 0)), wspec, bspec,
                   pl.BlockSpec((8, w), lambda i: (0, 0))],
        out_shape=[jax.ShapeDtypeStruct((n, w3), BF16), jax.ShapeDtypeStruct((SGU_GROUPS, CHUNK, CHUNK), F32),
                   jax.ShapeDtypeStruct((CHUNK, SGU_GROUPS), F32), jax.ShapeDtypeStruct((8, w), F32)],
        compiler_params=_params("arbitrary"),
    )(p, dm, _row(ln_g), _row(ln_b), ws, ws_t, bs_t)


def _adam_math(w, g, m, v):
    m_new = ADAM_B1 * m + (1.0 - ADAM_B1) * g
    v_new = ADAM_B2 * v + (1.0 - ADAM_B2) * (g * g)
    m_hat = m_new / (1.0 - ADAM_B1 ** ADAM_STEP)
    v_hat = v_new / (1.0 - ADAM_B2 ** ADAM_STEP)
    delta = -ADAM_LR * (m_hat / (jnp.sqrt(v_hat) + ADAM_EPS) + ADAM_WD * w)
    return delta, m_new, v_new


def adamw(w, g, m, v, *, name, slots=False, rows=512):
    r, c = w.shape
    tr = min(r, rows)
    assert r % tr == 0, (name, r, tr)

    def body(w_ref, g_ref, m_ref, v_ref, go_ref, d_ref, mo_ref, vo_ref):
        if slots:
            g = g_ref[0].astype(F32)
            for k in range(1, N_DEV):
                g = g + g_ref[k].astype(F32)
        else:
            g = g_ref[...].astype(F32)
        delta, m_new, v_new = _adam_math(w_ref[...], g, m_ref[...], v_ref[...])
        go_ref[...] = g
        d_ref[...] = delta
        mo_ref[...] = m_new
        vo_ref[...] = v_new

    spec = pl.BlockSpec((tr, c), lambda i: (i, 0))
    gspec = pl.BlockSpec((N_DEV, tr, c), lambda i: (0, i, 0)) if slots else spec
    shape = jax.ShapeDtypeStruct((r, c), F32)
    return pl.pallas_call(
        body, name=name, grid=(r // tr,),
        in_specs=[spec, gspec, spec, spec], out_specs=[spec] * 4, out_shape=[shape] * 4,
        compiler_params=_params("parallel"),
    )(w, g, m, v)


def _after(token, val):
    return val + token[0, 0]


def _pad_rows(a, rows):
    return jnp.pad(a, ((0, rows - a.shape[0]), (0, 0)))


def kernel(x, c, ctx, c_ctx, ada_w, ada_b, norm_g, ev_w_in, ev_q_norm, ev_k_norm, ev_dw_w, ev_dw_b, ev_ln_g, ev_ln_b, ev_w_out, od_w_in, od_ln_g, od_ln_b, od_ws, od_bs, od_w_out, final_g, loss_target, m_c_ctx, m_ada_w, m_ada_b, m_norm_g, m_ev_w_in, m_ev_q_norm, m_ev_k_norm, m_ev_dw_w, m_ev_dw_b, m_ev_ln_g, m_ev_ln_b, m_ev_w_out, m_od_w_in, m_od_ln_g, m_od_ln_b, m_od_ws, m_od_bs, m_od_w_out, m_final_g, v_c_ctx, v_ada_w, v_ada_b, v_norm_g, v_ev_w_in, v_ev_q_norm, v_ev_k_norm, v_ev_dw_w, v_ev_dw_b, v_ev_ln_g, v_ev_ln_b, v_ev_w_out, v_od_w_in, v_od_ln_g, v_od_ln_b, v_od_ws, v_od_bs, v_od_w_out, v_final_g):
    n, d = x.shape[1], x.shape[2]
    lc = ctx.shape[1]
    ev_in = ev_w_in.shape[2] * N_DEV
    od_in = od_w_in.shape[2] * N_DEV
    conv_ch = ev_dw_w.shape[2] * N_DEV
    ada_cols = ada_w.shape[2]
    me = 4 * lax.axis_index("x") + 2 * lax.axis_index("y") + lax.axis_index("c")
    xs, tgt, ctxs = x[0], loss_target[0], ctx[0]
    za_block = (2 * KV_W + ATTN_W) // HEAD_DIM
    glu_block = (2 * KV_W + 2 * ATTN_W) // HALF
    zb_block = glu_block + 2 * conv_ch // HALF

    small = jnp.concatenate([
        jax.nn.silu(c).reshape(1, d),
        od_ln_g.reshape(1, -1), od_ln_b.reshape(1, -1)], axis=1)
    small = _pad_rows(small, 8)
    dw_rows = _pad_rows(ev_dw_w[0], 32)
    small_g, dw_g = all_gather([small, dw_rows], name="gather_small", hbm=False)
    sc_all = small_g[:, 0, :d]
    shard = d // N_DEV
    od_ln_g_full = small_g[:, 0, d:d + shard].reshape(d)
    od_ln_b_full = small_g[:, 0, d + shard:d + 2 * shard].reshape(d)
    dw_w_full = jnp.moveaxis(dw_g, 0, 1).reshape(32, conv_ch)[:CONV_WIDTH]
    scc = jax.nn.silu(c_ctx)
    sc16 = _pad_rows(jnp.concatenate([sc_all, scc.reshape(1, d)], axis=0), 16)

    ada_bf = ada_w.astype(BF16)
    mod_loc = [matmul(sc16, ada_bf[l], mode="nn", tm=16, tn=ada_cols, tk=d, out_dtype=F32, name=f"ada_mod{l}")
               for l in range(2)]
    (mod_g,) = all_gather([jnp.stack(mod_loc)], name="gather_mod", hbm=False)
    mod_all = jnp.moveaxis(mod_g, 0, 2).reshape(2, 16, N_DEV * ada_cols) + ada_b[:, None, :]
    mod_me = lax.dynamic_index_in_dim(mod_all, me, axis=1, keepdims=False)
    shift = [mod_me[l, :d] for l in range(2)]
    scale = [mod_me[l, d:2 * d] for l in range(2)]
    gate = [mod_me[l, 2 * d:] for l in range(2)]
    shift_c, scale_c = mod_all[0, 8, :d], mod_all[0, 8, d:2 * d]

    (wi0_g,) = all_gather([ev_w_in[0].astype(BF16)], name="gather_w_in0", hbm=True)
    wi0 = jnp.moveaxis(wi0_g, 0, 1).reshape(d, ev_in)
    bits = lax.bitcast_convert_type(wi0_g[0, 0, 0], jnp.uint16)
    landed_zero = jnp.where((bits | 1) == 0, 1.0, 0.0).astype(F32)
    later = [(w[0] + landed_zero).astype(BF16) for w in (ev_w_out, od_w_in, od_w_out)]
    (h_wo0, h_wi1, h_wo1), w_token = exchange_start(later, scatter=False, name="gather_rest_start")

    def landed(handle, after, name):
        own, land = exchange_wait(handle, after, scatter=False, name=name)
        return lax.dynamic_update_slice_in_dim(land, own[None], me, axis=0)

    cexp, sexp = rope_tables(n)

    shift0 = _after(w_token, shift[0])
    h0 = norm_mod_fwd(xs, norm_g[0], shift0, scale[0], name="norm_mod_fwd0")
    hc = norm_mod_fwd(ctxs, norm_g[0], shift_c, scale_c, name="norm_mod_fwd_ctx")
    p0 = matmul(h0, wi0, mode="nn", tm=1024, tn=ev_in // 4, tk=d, out_dtype=F32, name="proj_in0")
    pc = matmul(hc, wi0, mode="nn", tm=lc, tn=2 * KV_W, tk=d, out_dtype=F32, n_out=2 * KV_W, name="proj_in_ctx")
    q_r, k_all, v_ext = qkv_prep_fwd(p0, ev_q_norm[0], ev_k_norm[0], cexp, sexp, latent=True, name="qkv_prep",
                                     rows_all=n + lc)
    k_all, v_ext = qkv_prep_fwd(pc, None, ev_k_norm[0], None, None, latent=False, name="kv_prep_ctx",
                                rows_all=n + lc, k_all=k_all, v_ext=v_ext)
    o_attn, mix_a, lse = flash_fwd(q_r, k_all, v_ext, p0, za_block=za_block, mix_width=ATTN_W + conv_ch,
                                   name="flash_fwd")
    ycv, mix0 = conv_fwd(p0, dw_w_full, ev_dw_b[0], ev_ln_g[0], ev_ln_b[0], mix_a, glu_block=glu_block,
                         name="conv_fwd")
    wo0 = landed(h_wo0, ycv, "gather_w_out0_wait").reshape(-1, d)
    x1, r0 = matmul(mix0, wo0, mode="nn", tm=1024, tn=1024, tk=mix0.shape[1], out_dtype=F32, name="proj_out0",
                    res=xs, gate=_row(gate[0]))

    h1 = norm_mod_fwd(x1, norm_g[1], shift[1], scale[1], name="norm_mod_fwd1")
    wi1 = jnp.moveaxis(landed(h_wi1, h1, "gather_w_in1_wait"), 0, 1).reshape(d, od_in)
    p1 = matmul(h1, wi1, mode="nn", tm=1024, tn=od_in // 4, tk=d, out_dtype=F32, name="proj_in1")
    ws_bf = od_ws[0]
    bs_t = od_bs[0].T
    m1 = sgu_fwd(p1, od_ln_g_full, od_ln_b_full, ws_bf, bs_t, name="sgu_fwd")
    wo1 = landed(h_wo1, m1, "gather_w_out1_wait").reshape(-1, d)
    x2, r1 = matmul(m1, wo1, mode="nn", tm=1024, tn=1024, tk=m1.shape[1], out_dtype=F32, name="proj_out1",
                    res=x1, gate=_row(gate[1]))

    dx2, dr1, acc_final, loss_tile = final_loss(x2, tgt, final_g, r1, gate[1], name="final_loss")

    dm1 = matmul(dr1, wo1, mode="nt", tm=1024, tn=1024, tk=d, out_dtype=F32, name="d_mix1")
    dwo1 = matmul(m1, dr1, mode="tn", tm=1024, tn=1024, tk=2048, out_dtype=BF16, name="d_wout1")
    dp1, dws, dbs_t, acc_sgu = sgu_bwd(p1, dm1, od_ln_g_full, od_ln_b_full, ws_bf, jnp.swapaxes(ws_bf, 1, 2), bs_t,
                                       name="sgu_bwd")
    dwi1_s = matmul(h1, dp1, mode="tn", tm=2048, tn=od_in // N_DEV, tk=2048, out_dtype=BF16, name="d_win1",
                    split_out=True)
    dwo1_s = dwo1.reshape(N_DEV, -1, d)
    (h_gi1, h_go1), g1_token = exchange_start([dwi1_s, dwo1_s], scatter=True, name="grads1_start")
    dh1 = matmul(dp1, wi1, mode="nt", tm=1024, tn=512, tk=od_in, out_dtype=F32, name="d_h1")
    dx1, dr0, acc_norm1 = norm_mod_bwd(x1, dh1, dx2, norm_g[1], _after(g1_token, scale[1]), name="norm_mod_bwd1",
                                       branch=(r0, gate[0]))

    dmix0 = matmul(dr0, wo0, mode="nt", tm=1024, tn=1024, tk=d, out_dtype=F32, name="d_mix0")
    dwo0 = matmul(mix0, dr0, mode="tn", tm=1024, tn=1024, tk=2048, out_dtype=BF16, name="d_wout0")
    do_attn, do_s, dza, delta = attn_gate_bwd(dmix0, o_attn, p0, za_block=za_block, name="attn_gate_bwd")
    dycv, dzb, acc_ln = conv_bwd_rows(dmix0, ycv, p0, ev_ln_g[0], ev_ln_b[0], mix_block=ATTN_W // conv_ch,
                                      zb_block=zb_block, name="conv_bwd_rows")
    dglu, ddw_w, acc_dwb = conv_bwd_taps(dycv, p0, dw_w_full, glu_block=glu_block, name="conv_bwd_taps")
    dq_r, dk_all, dv_all = flash_bwd(q_r, do_attn, do_s, lse, delta, k_all, v_ext, name="flash_bwd")
    dkvq, acc_qk = qkv_prep_bwd(p0, dq_r, dk_all[:n], dv_all[:n], ev_q_norm[0], ev_k_norm[0], cexp, sexp,
                                latent=True, name="qkv_prep_bwd")
    dpc, acc_kc = qkv_prep_bwd(pc, None, dk_all[n:], dv_all[n:], None, ev_k_norm[0], None, None,
                               latent=False, name="kv_prep_ctx_bwd")
    dp0 = jnp.concatenate([dkvq, dza, dglu, dzb], axis=1)
    dwi0 = matmul(h0, dp0, mode="tn", tm=1024, tn=ev_in // 4, tk=2048, out_dtype=F32, name="d_win0")
    dwi0 = matmul(hc, dpc, mode="tn", tm=512, tn=2 * KV_W, tk=lc, out_dtype=F32, name="d_win0_ctx", add=dwi0)
    dwi0_s = jnp.moveaxis(dwi0.astype(BF16).reshape(d, N_DEV, ev_in // N_DEV), 1, 0)
    dwo0_s = dwo0.reshape(N_DEV, -1, d)
    (h_gi0, h_go0), g0_token = exchange_start([dwi0_s, dwo0_s], scatter=True, name="grads0_start")
    dh0 = matmul(dp0, wi0, mode="nt", tm=1024, tn=512, tk=ev_in, out_dtype=F32, name="d_h0", after=g0_token)
    dhc = matmul(dpc, wi0, mode="nt", tm=lc, tn=512, tk=2 * KV_W, out_dtype=F32, name="d_h_ctx")
    grad_x, acc_norm0 = norm_mod_bwd(xs, dh0, dx1, norm_g[0], _after(g0_token, scale[0]), name="norm_mod_bwd0")
    _, acc_normc = norm_mod_bwd(ctxs, dhc, None, norm_g[0], scale_c, name="norm_mod_bwd_ctx")

    zeros_d = jnp.zeros((d,), F32)
    dmod0 = jnp.stack([acc_norm0[0], acc_norm0[1], acc_norm1[GATE_ROW]])
    dmod1 = jnp.stack([acc_norm1[0], acc_norm1[1], acc_final[GATE_ROW]])
    dmodc = jnp.stack([acc_normc[0], acc_normc[1]])
    half_pad = jnp.zeros((d - 2 * conv_ch,), F32) if d > 2 * conv_ch else jnp.zeros((0,), F32)
    row_a = jnp.concatenate([acc_dwb[0], acc_ln[0], half_pad])
    row_b = jnp.concatenate([acc_ln[1], acc_qk[0], acc_qk[1] + acc_kc[1],
                             jnp.zeros((d - conv_ch - 2 * HEAD_DIM,), F32)])
    row_c = jnp.concatenate([dbs_t.T.reshape(-1), jnp.zeros((d - SGU_GROUPS * CHUNK,), F32)])
    row_loss = jnp.concatenate([loss_tile[0, :1], jnp.zeros((d - 1,), F32)])
    pack = jnp.concatenate([
        dmod0, dmod1, dmodc,
        (acc_norm0[2] + acc_normc[2])[None], acc_norm1[2][None],
        acc_final[0][None],
        acc_sgu[0][None], acc_sgu[1][None],
        row_a[None], row_b[None], row_c[None], row_loss[None],
        ddw_w.reshape(-1, d),
        dws.reshape(-1, d),
    ], axis=0)
    n_rows = pack.shape[0]
    pack = _pad_rows(pack, -(-n_rows // 8) * 8)
    pack_g, pack_sum = all_gather([pack], name="gather_small_grads", hbm=False, sum_out=True)
    gsum = pack_sum
    loss = gsum[16, 0]
    dw_rows_n = 32 * conv_ch // d
    g_dw_w = gsum[17:17 + dw_rows_n].reshape(32, conv_ch)[:CONV_WIDTH]
    g_od_ws = gsum[17 + dw_rows_n:17 + dw_rows_n + SGU_GROUPS * CHUNK * CHUNK // d].reshape(od_ws.shape)

    dmodc_sum = jnp.concatenate([gsum[6], gsum[7], zeros_d])
    col0 = me * ada_cols
    dm_cols = []
    for l in range(2):
        rows = pack_g[:, 3 * l:3 * l + 3, :].reshape(N_DEV, 3 * d)
        extra = dmodc_sum[None] if l == 0 else jnp.zeros((1, 3 * d), F32)
        full = _pad_rows(jnp.concatenate([rows, extra], axis=0), 16)
        dm_cols.append(lax.dynamic_slice_in_dim(full, col0, ada_cols, axis=1))
    g_ada_w = jnp.stack([matmul(sc16, dm_cols[l], mode="tn", tm=512, tn=ada_cols, tk=16, out_dtype=F32,
                                name=f"d_ada_w{l}") for l in range(2)])
    dsc = matmul(dm_cols[0], ada_bf[0], mode="nt", tm=16, tn=512, tk=ada_cols, out_dtype=F32, name="d_scc")
    (_, dscc_sum) = all_gather([dsc[8:16]], name="gather_dscc", hbm=False, sum_out=True)
    sg = jax.nn.sigmoid(c_ctx)
    g_c_ctx = dscc_sum[0] * (sg * (1.0 + c_ctx * (1.0 - sg)))
    g_ada_b = jnp.stack([gsum[0:3].reshape(-1) + dmodc_sum, gsum[3:6].reshape(-1)])

    def summands(handle, after, name):
        mine, land = exchange_wait(handle, after, scatter=True, name=name)
        own = lax.dynamic_index_in_dim(mine, me, axis=0, keepdims=True)
        return lax.dynamic_update_slice_in_dim(land, own, me, axis=0)

    out = {}

    def upd(key, w, g, m, v, slots=False, rows=512):
        shp = w.shape
        w2 = w.reshape(-1, shp[-1])
        g2 = g.reshape((N_DEV, -1, shp[-1])) if slots else g.reshape(-1, shp[-1])
        res = adamw(w2, g2, m.reshape(w2.shape), v.reshape(w2.shape), name="adamw_" + key, slots=slots, rows=rows)
        out[key] = tuple(r.reshape(shp) for r in res)

    upd("ada_w", ada_w, g_ada_w, m_ada_w, v_ada_w)
    gi1 = summands(h_gi1,out["ada_w"][1], "grads_w_in1_wait")
    upd("od_w_in", od_w_in, gi1, m_od_w_in, v_od_w_in, slots=True, rows=256)
    go1 = summands(h_go1,out["od_w_in"][1], "grads_w_out1_wait")
    upd("od_w_out", od_w_out, go1, m_od_w_out, v_od_w_out, slots=True, rows=256)
    go0 = summands(h_go0,out["od_w_out"][1], "grads_w_out0_wait")
    upd("ev_w_out", ev_w_out, go0, m_ev_w_out, v_ev_w_out, slots=True, rows=256)
    gi0 = summands(h_gi0,out["ev_w_out"][1], "grads_w_in0_wait")
    upd("ev_w_in", ev_w_in, gi0, m_ev_w_in, v_ev_w_in, slots=True, rows=256)

    def my_shard(full, size):
        return lax.dynamic_slice_in_dim(full, me * size, size, axis=full.ndim - 1)

    small_items = [
        ("c_ctx", c_ctx, g_c_ctx, m_c_ctx, v_c_ctx),
        ("ada_b", ada_b, g_ada_b, m_ada_b, v_ada_b),
        ("norm_g", norm_g, gsum[8:10], m_norm_g, v_norm_g),
        ("ev_q_norm", ev_q_norm, gsum[14, conv_ch:conv_ch + HEAD_DIM], m_ev_q_norm, v_ev_q_norm),
        ("ev_k_norm", ev_k_norm, gsum[14, conv_ch + HEAD_DIM:conv_ch + 2 * HEAD_DIM], m_ev_k_norm, v_ev_k_norm),
        ("ev_dw_w", ev_dw_w, my_shard(g_dw_w, conv_ch // N_DEV), m_ev_dw_w, v_ev_dw_w),
        ("ev_dw_b", ev_dw_b, gsum[13, :conv_ch], m_ev_dw_b, v_ev_dw_b),
        ("ev_ln_g", ev_ln_g, gsum[13, conv_ch:2 * conv_ch], m_ev_ln_g, v_ev_ln_g),
        ("ev_ln_b", ev_ln_b, gsum[14, :conv_ch], m_ev_ln_b, v_ev_ln_b),
        ("od_ln_g", od_ln_g, my_shard(gsum[11], shard), m_od_ln_g, v_od_ln_g),
        ("od_ln_b", od_ln_b, my_shard(gsum[12], shard), m_od_ln_b, v_od_ln_b),
        ("od_ws", od_ws, g_od_ws, m_od_ws, v_od_ws),
        ("od_bs", od_bs, gsum[15, :SGU_GROUPS * CHUNK], m_od_bs, v_od_bs),
        ("final_g", final_g, gsum[10], m_final_g, v_final_g),
    ]
    sizes = [it[1].size for it in small_items]
    total = sum(sizes)
    lanes = 1024
    prow = -(-total // lanes)
    prow = -(-prow // 8) * 8

    def pack_small(idx):
        flat = jnp.concatenate([it[idx].reshape(-1).astype(F32) for it in small_items])
        return jnp.pad(flat, (0, prow * lanes - total)).reshape(prow, lanes)

    sres = adamw(pack_small(1), pack_small(2), pack_small(3), pack_small(4), name="adamw_small", rows=prow)
    off = 0
    for it, size in zip(small_items, sizes):
        out[it[0]] = tuple(r.reshape(-1)[off:off + size].reshape(it[1].shape) for r in sres)
        off += size

    names = ['c_ctx', 'ada_w', 'ada_b', 'norm_g', 'ev_w_in', 'ev_q_norm', 'ev_k_norm', 'ev_dw_w', 'ev_dw_b',
             'ev_ln_g', 'ev_ln_b', 'ev_w_out', 'od_w_in', 'od_ln_g', 'od_ln_b', 'od_ws', 'od_bs', 'od_w_out',
             'final_g']
    return (loss, grad_x[None], *[out[k][0] for k in names], *[out[k][1] for k in names],
            *[out[k][2] for k in names], *[out[k][3] for k in names])
```

```python
import functools
import math

import jax
import jax.numpy as jnp
from jax import lax
from jax.experimental import pallas as pl
from jax.experimental.pallas import tpu as pltpu

F32 = jnp.float32
BF16 = jnp.bfloat16
MESH = pl.DeviceIdType.MESH

EPS = 1e-6
HEAD_DIM = 128
N_Q_HEADS = 8
N_KV_HEADS = 2
Q_PER_KV = N_Q_HEADS // N_KV_HEADS
ATTN_W = N_Q_HEADS * HEAD_DIM
KV_W = N_KV_HEADS * HEAD_DIM
ATTN_SCALE = HEAD_DIM ** -0.5
LN2 = math.log(2.0)
Q_SCALE = ATTN_SCALE / LN2
ROPE_THETA = 10000.0
GRID_W = 64
CONV_WIDTH = 31
CONV_HALF = CONV_WIDTH // 2
HALO = 16
CHUNK = 128
SGU_GROUPS = 8
N_DEV = 8

ADAM_LR = 0.001
ADAM_B1 = 0.9
ADAM_B2 = 0.999
ADAM_EPS = 1e-08
ADAM_WD = 0.01
ADAM_STEP = 10

VMEM_LIMIT = 56 * 1024 * 1024
ANY = pl.BlockSpec(memory_space=pl.ANY)
VMEM_SPEC = pl.BlockSpec(memory_space=pltpu.VMEM)


def _params(*sem):
    return pltpu.CompilerParams(dimension_semantics=sem, vmem_limit_bytes=VMEM_LIMIT)


def _sigmoid(x):
    return 1.0 / (1.0 + jnp.exp(-x))


def _silu(x):
    return x * _sigmoid(x)


def _dsilu(x):
    s = _sigmoid(x)
    return s * (1.0 + x * (1.0 - s))


_GELU_C = math.sqrt(2.0 / math.pi)


def _gelu(x):
    t = jnp.tanh(_GELU_C * (x + 0.044715 * (x * x * x)))
    return 0.5 * x * (1.0 + t)


def _dgelu(x):
    t = jnp.tanh(_GELU_C * (x + 0.044715 * (x * x * x)))
    return 0.5 * (1.0 + t) + 0.5 * x * (1.0 - t * t) * (_GELU_C * (1.0 + 3.0 * 0.044715 * (x * x)))


def _row(v):
    return v.reshape(1, -1).astype(F32)


def _flat_id(p):
    return 4 * p[0] + 2 * p[1] + p[2]


def _gather_body(n_arr, sum_out):
    def body(*refs):
        x_refs = refs[:n_arr]
        out_refs = refs[n_arr:2 * n_arr]
        pos = 2 * n_arr
        sum_refs = refs[pos:pos + n_arr] if sum_out else ()
        pos += n_arr if sum_out else 0
        send_sems, recv_sems, local_sems = refs[pos:pos + 3]
        x, y, c = lax.axis_index("x"), lax.axis_index("y"), lax.axis_index("c")
        me, sibling = (x, y, c), (x, y, 1 - c)
        chips = [(1 - x, y), (x, 1 - y), (1 - x, 1 - y)]

        def copy(a, k, block, to, src=None):
            rows = out_refs[a].at[_flat_id(block)]
            return pltpu.make_async_remote_copy(
                src_ref=rows if src is None else src, dst_ref=rows,
                send_sem=send_sems.at[a, k], recv_sem=recv_sems.at[a, k],
                device_id=to, device_id_type=MESH)

        sends = []
        mine = []
        for a in range(n_arr):
            cp = pltpu.make_async_copy(x_refs[a], out_refs[a].at[_flat_id(me)], local_sems.at[a])
            cp.start()
            mine.append(cp)
            first = [copy(a, 0, me, sibling, src=x_refs[a])]
            first += [copy(a, 1 + j, me, (*chip, c), src=x_refs[a]) for j, chip in enumerate(chips)]
            for cp in first:
                cp.start()
            sends += first
        for a in range(n_arr):
            for j, chip in enumerate(chips):
                copy(a, 1 + j, (*chip, c), me).wait_recv()
                fwd = copy(a, 4 + j, (*chip, c), sibling)
                fwd.start()
                sends.append(fwd)
        for a in range(n_arr):
            copy(a, 0, sibling, me).wait_recv()
            for j, chip in enumerate(chips):
                copy(a, 4 + j, (*chip, 1 - c), me).wait_recv()
        for cp in sends:
            cp.wait_send()
        for cp in mine:
            cp.wait()
        if sum_out:
            for a in range(n_arr):
                acc = out_refs[a][0]
                for k in range(1, N_DEV):
                    acc = acc + out_refs[a][k]
                sum_refs[a][...] = acc

    return body


def all_gather(arrs, *, name, hbm, sum_out=False):
    n = len(arrs)
    spec = ANY if hbm else VMEM_SPEC
    out_shape = [jax.ShapeDtypeStruct((N_DEV,) + a.shape, a.dtype) for a in arrs]
    out_specs = [spec] * n
    if sum_out:
        out_shape += [jax.ShapeDtypeStruct(a.shape, a.dtype) for a in arrs]
        out_specs += [VMEM_SPEC] * n
    res = pl.pallas_call(
        _gather_body(n, sum_out), name=name,
        out_shape=out_shape, in_specs=[spec] * n, out_specs=out_specs,
        scratch_shapes=[pltpu.SemaphoreType.DMA((n, 7)), pltpu.SemaphoreType.DMA((n, 7)),
                        pltpu.SemaphoreType.DMA((n,))],
        compiler_params=pltpu.CompilerParams(vmem_limit_bytes=VMEM_LIMIT),
    )(*arrs)
    return res


_RELATIONS = [(rx, ry, rc) for rx in (0, 1) for ry in (0, 1) for rc in (0, 1)][1:]


HBM_SPEC = pl.BlockSpec(memory_space=pltpu.HBM)
SEM_SPEC = pl.BlockSpec(memory_space=pltpu.SEMAPHORE)
_DATAFLOW = pltpu.SideEffectType.DATAFLOW_SIDE_EFFECTING
N_PEERS = N_DEV - 1


def _peer_copy(src_ref, land_ref, send_sem, recv_sem, k, rel, scatter, sending):
    x, y, c = lax.axis_index("x"), lax.axis_index("y"), lax.axis_index("c")
    rx, ry, rc = rel
    peer = (1 - x if rx else x, 1 - y if ry else y, 1 - c if rc else c)
    src = src_ref.at[_flat_id(peer)] if scatter else src_ref
    dst = land_ref.at[_flat_id((x, y, c)) if sending else _flat_id(peer)]
    return pltpu.make_async_remote_copy(src_ref=src, dst_ref=dst, send_sem=send_sem.at[k], recv_sem=recv_sem.at[k],
                                        device_id=peer, device_id_type=MESH)


def exchange_start(arrs, *, scatter, name):
    n = len(arrs)
    lands = [lax.empty((N_DEV,) + (a.shape[1:] if scatter else a.shape), a.dtype) for a in arrs]

    def body(*refs):
        srcs, lnds, sems = refs[:n], refs[n:2 * n], refs[2 * n:4 * n]
        token = refs[6 * n]
        for a in range(n):
            for k, rel in enumerate(_RELATIONS):
                _peer_copy(srcs[a], lnds[a], sems[2 * a], sems[2 * a + 1], k, rel, scatter, True).start()
        token[...] = jnp.zeros_like(token)

    outs = pl.pallas_call(
        body, name=name,
        out_shape=[pltpu.SemaphoreType.DMA((N_PEERS,))] * (2 * n)
        + [pltpu.HBM(a.shape, a.dtype) for a in arrs] + [pltpu.HBM(l.shape, l.dtype) for l in lands]
        + [jax.ShapeDtypeStruct((8, 128), F32)],
        in_specs=[HBM_SPEC] * (2 * n),
        out_specs=[SEM_SPEC] * (2 * n) + [HBM_SPEC] * (2 * n) + [VMEM_SPEC],
        input_output_aliases={i: 2 * n + i for i in range(2 * n)},
        compiler_params=pltpu.CompilerParams(has_side_effects=_DATAFLOW),
    )(*[pltpu.with_memory_space_constraint(a, pltpu.HBM) for a in arrs],
      *[pltpu.with_memory_space_constraint(l, pltpu.HBM) for l in lands])
    handles = [(outs[2 * a], outs[2 * a + 1], outs[2 * n + a], outs[3 * n + a]) for a in range(n)]
    return handles, outs[4 * n]


def exchange_wait(handle, after, *, scatter, name):
    send_sem, recv_sem, src, land = handle

    def body(src_ref, land_ref, send_ref, recv_ref, after_ref, src_out, land_out):
        for k, rel in enumerate(_RELATIONS):
            cp = _peer_copy(src_ref, land_ref, send_ref, recv_ref, k, rel, scatter, False)
            cp.wait_send()
            cp.wait_recv()

    outs = pl.pallas_call(
        body, name=name,
        out_shape=[pltpu.HBM(src.shape, src.dtype), pltpu.HBM(land.shape, land.dtype)],
        in_specs=[HBM_SPEC, HBM_SPEC, SEM_SPEC, SEM_SPEC, ANY],
        out_specs=[HBM_SPEC, HBM_SPEC], input_output_aliases={0: 0, 1: 1},
        compiler_params=pltpu.CompilerParams(has_side_effects=_DATAFLOW),
    )(src, land, send_sem, recv_sem, after)
    return outs[0], outs[1]


_DIMS = {"nn": (((1,), (0,)), ((), ())), "nt": (((1,), (1,)), ((), ())), "tn": (((0,), (0,)), ((), ()))}


def matmul(a, b, *, mode, tm, tn, tk, out_dtype, name, n_out=None, res=None, gate=None, add=None, after=None,
           split_out=False, norm=None):
    if mode == "tn":
        kdim, m = a.shape
    else:
        m, kdim = a.shape
    nfull = b.shape[0] if mode == "nt" else b.shape[1]
    n = nfull if n_out is None else n_out
    tm, tn, tk = min(tm, m), min(tn, n), min(tk, kdim)
    assert m % tm == 0 and n % tn == 0 and kdim % tk == 0, (name, m, n, kdim, tm, tn, tk)
    nk = kdim // tk
    dims = _DIMS[mode]
    a_spec = pl.BlockSpec((tk, tm), lambda j, i, k: (k, i)) if mode == "tn" else pl.BlockSpec((tm, tk), lambda j, i, k: (i, k))
    b_spec = pl.BlockSpec((tn, tk), lambda j, i, k: (j, k)) if mode == "nt" else pl.BlockSpec((tk, tn), lambda j, i, k: (k, j))
    o_spec = pl.BlockSpec((tm, tn), lambda j, i, k: (i, j))
    in_specs = [a_spec, b_spec]
    operands = [a, b]
    aliases = {}
    vec = pl.BlockSpec((1, tn), lambda j, i, k: (0, j))
    if res is not None:
        in_specs += [o_spec, vec]
        operands += [res, gate]
    if norm is not None:
        assert res is not None and tn == n, (name, tn, n)
        in_specs += [vec, vec, vec]
        operands += [_row(v) for v in norm]
    if add is not None:
        in_specs += [o_spec]
        aliases = {len(operands): 0}
        operands += [add]
    if after is not None:
        in_specs += [ANY]
        operands += [after]
    out_cols = n if add is None else add.shape[1]
    out_shape = [jax.ShapeDtypeStruct((m, out_cols), out_dtype)]
    out_specs = [o_spec]
    if split_out:
        out_shape = [jax.ShapeDtypeStruct((n // tn, m, tn), out_dtype)]
        out_specs = [pl.BlockSpec((None, tm, tn), lambda j, i, k: (j, i, 0))]
    if res is not None:
        out_shape.append(jax.ShapeDtypeStruct((m, n), BF16))
        out_specs.append(o_spec)
    if norm is not None:
        out_shape.append(jax.ShapeDtypeStruct((m, n), BF16))
        out_specs.append(o_spec)

    def body(*refs):
        a_ref, b_ref = refs[:2]
        pos = 2
        if res is not None:
            res_ref, gate_ref = refs[pos:pos + 2]
            pos += 2
        if norm is not None:
            ng_ref, nshift_ref, nscale_ref = refs[pos:pos + 3]
            pos += 3
        if add is not None:
            add_ref = refs[pos]
            pos += 1
        if after is not None:
            pos += 1
        o_ref = refs[pos]
        pos += 1
        if res is not None:
            r_ref = refs[pos]
            pos += 1
        if norm is not None:
            h_ref = refs[pos]
            pos += 1
        acc_ref = refs[pos] if nk > 1 else None
        prod = lax.dot_general(a_ref[...].astype(BF16), b_ref[...].astype(BF16), dims,
                               preferred_element_type=F32)

        def finish(acc):
            if res is not None:
                xnew = res_ref[...] + gate_ref[...] * acc
                o_ref[...] = xnew.astype(out_dtype)
                r_ref[...] = acc.astype(BF16)
                if norm is not None:
                    rstd = lax.rsqrt(jnp.mean(xnew * xnew, axis=-1, keepdims=True) + EPS)
                    y = xnew * rstd * ng_ref[...]
                    h_ref[...] = (y * (1.0 + nscale_ref[...]) + nshift_ref[...]).astype(BF16)
            elif add is not None:
                o_ref[...] = (add_ref[...] + acc).astype(out_dtype)
            else:
                o_ref[...] = acc.astype(out_dtype)

        if nk == 1:
            finish(prod)
        else:
            k = pl.program_id(2)

            @pl.when(k == 0)
            def _():
                acc_ref[...] = prod

            @pl.when(k > 0)
            def _():
                acc_ref[...] += prod

            @pl.when(k == nk - 1)
            def _():
                finish(acc_ref[...])

    outs = pl.pallas_call(
        body, name=name, grid=(n // tn, m // tm, nk),
        in_specs=in_specs, out_specs=out_specs, out_shape=out_shape,
        scratch_shapes=[pltpu.VMEM((tm, tn), F32)] if nk > 1 else [],
        input_output_aliases=aliases,
        compiler_params=_params("parallel", "parallel", "arbitrary"),
    )(*operands)
    return outs if res is not None else outs[0]


def _rows_tile(n, want):
    return min(n, want)


def norm_mod_fwd(x, g, shift, scale, *, name):
    n, d = x.shape
    tt = _rows_tile(n, 256)

    def body(x_ref, g_ref, sh_ref, sc_ref, h_ref):
        xv = x_ref[...]
        rstd = lax.rsqrt(jnp.mean(xv * xv, axis=-1, keepdims=True) + EPS)
        y = xv * rstd * g_ref[...]
        h_ref[...] = (y * (1.0 + sc_ref[...]) + sh_ref[...]).astype(BF16)

    vec = pl.BlockSpec((1, d), lambda i: (0, 0))
    return pl.pallas_call(
        body, name=name, grid=(n // tt,),
        in_specs=[pl.BlockSpec((tt, d), lambda i: (i, 0)), vec, vec, vec],
        out_specs=pl.BlockSpec((tt, d), lambda i: (i, 0)),
        out_shape=jax.ShapeDtypeStruct((n, d), BF16),
        compiler_params=_params("parallel"),
    )(x, _row(g), _row(shift), _row(scale))


GATE_ROW = 3


def norm_mod_bwd(x, dh, dres, g, scale, *, name, branch=None):
    n, d = x.shape
    tt = _rows_tile(n, 256)
    has_res = dres is not None
    has_branch = branch is not None
    last = n // tt - 1

    def body(*refs):
        x_ref, dh_ref = refs[:2]
        pos = 2
        if has_res:
            dres_ref = refs[pos]
            pos += 1
        if has_branch:
            r_ref, gate_ref = refs[pos:pos + 2]
            pos += 2
        g_ref, sc_ref, dx_ref = refs[pos:pos + 3]
        pos += 3
        if has_branch:
            dr_ref = refs[pos]
            pos += 1
        acc_ref, s_ref = refs[pos:pos + 2]
        i = pl.program_id(0)
        xv = x_ref[...]
        dhv = dh_ref[...]
        rstd = lax.rsqrt(jnp.mean(xv * xv, axis=-1, keepdims=True) + EPS)
        xhat = xv * rstd
        dxhat = dhv * (g_ref[...] * (1.0 + sc_ref[...]))
        dx = rstd * (dxhat - xhat * jnp.mean(dxhat * xhat, axis=-1, keepdims=True))
        if has_res:
            dx = dx + dres_ref[...]
        dx_ref[...] = dx

        @pl.when(i == 0)
        def _():
            s_ref[...] = jnp.zeros_like(s_ref)

        s_ref[0:1, :] += jnp.sum(dhv, axis=0, keepdims=True)
        s_ref[1:2, :] += jnp.sum(dhv * xhat, axis=0, keepdims=True)
        if has_branch:
            dr_ref[...] = (dx * gate_ref[...]).astype(BF16)
            s_ref[2:3, :] += jnp.sum(dx * r_ref[...].astype(F32), axis=0, keepdims=True)

        @pl.when(i == last)
        def _():
            s1 = s_ref[0:1, :]
            s2 = s_ref[1:2, :]
            acc_ref[...] = jnp.zeros_like(acc_ref)
            acc_ref[0:1, :] = s1
            acc_ref[1:2, :] = s2 * g_ref[...]
            acc_ref[2:3, :] = s2 * (1.0 + sc_ref[...])
            acc_ref[GATE_ROW:GATE_ROW + 1, :] = s_ref[2:3, :]

    vec = pl.BlockSpec((1, d), lambda i: (0, 0))
    big = pl.BlockSpec((tt, d), lambda i: (i, 0))
    ops = [x, dh] + ([dres] if has_res else []) + ([branch[0], _row(branch[1])] if has_branch else [])
    ops += [_row(g), _row(scale)]
    return pl.pallas_call(
        body, name=name, grid=(n // tt,),
        in_specs=[big, big] + ([big] if has_res else []) + ([big, vec] if has_branch else []) + [vec, vec],
        out_specs=[big] + ([big] if has_branch else []) + [pl.BlockSpec((8, d), lambda i: (0, 0))],
        out_shape=[jax.ShapeDtypeStruct((n, d), F32)] + ([jax.ShapeDtypeStruct((n, d), BF16)] if has_branch else [])
        + [jax.ShapeDtypeStruct((8, d), F32)],
        scratch_shapes=[pltpu.VMEM((8, d), F32)],
        compiler_params=_params("arbitrary"),
    )(*ops)


def final_loss(x, target, g, r, gate, *, name):
    n, d = x.shape
    tt = _rows_tile(n, 256)

    def body(x_ref, t_ref, g_ref, r_ref, gate_ref, dx_ref, dr_ref, acc_ref, loss_ref):
        i = pl.program_id(0)
        xv = x_ref[...]
        rstd = lax.rsqrt(jnp.mean(xv * xv, axis=-1, keepdims=True) + EPS)
        xhat = xv * rstd
        e = xhat * g_ref[...] - t_ref[...]
        dy = e * (1.0 / d)
        dxhat = dy * g_ref[...]
        dx = rstd * (dxhat - xhat * jnp.mean(dxhat * xhat, axis=-1, keepdims=True))
        dx_ref[...] = dx
        dr_ref[...] = (dx * gate_ref[...]).astype(BF16)

        @pl.when(i == 0)
        def _():
            acc_ref[...] = jnp.zeros_like(acc_ref)
            loss_ref[...] = jnp.zeros_like(loss_ref)

        acc_ref[0:1, :] += jnp.sum(dy * xhat, axis=0, keepdims=True)
        acc_ref[GATE_ROW:GATE_ROW + 1, :] += jnp.sum(dx * r_ref[...].astype(F32), axis=0, keepdims=True)
        part = 0.5 * jnp.sum(jnp.mean(e * e, axis=-1, keepdims=True), axis=0, keepdims=True)
        loss_ref[...] += jnp.broadcast_to(part, loss_ref.shape)

    big = pl.BlockSpec((tt, d), lambda i: (i, 0))
    vec = pl.BlockSpec((1, d), lambda i: (0, 0))
    return pl.pallas_call(
        body, name=name, grid=(n // tt,),
        in_specs=[big, big, vec, big, vec],
        out_specs=[big, big, pl.BlockSpec((8, d), lambda i: (0, 0)), pl.BlockSpec((8, 128), lambda i: (0, 0))],
        out_shape=[jax.ShapeDtypeStruct((n, d), F32), jax.ShapeDtypeStruct((n, d), BF16),
                   jax.ShapeDtypeStruct((8, d), F32), jax.ShapeDtypeStruct((8, 128), F32)],
        compiler_params=_params("arbitrary"),
    )(x, target, _row(g), r, _row(gate))


def _as_row(col):
    t = col.shape[0]
    return jnp.transpose(jnp.broadcast_to(col, (t, HEAD_DIM)))[0:1, :]


def _swap_pairs(x):
    lane = lax.broadcasted_iota(jnp.int32, x.shape, 1)
    return jnp.where(lane % 2 == 0, pltpu.roll(x, HEAD_DIM - 1, 1), pltpu.roll(x, 1, 1))


def rope_tables(n):
    rows = n // GRID_W
    row = jnp.repeat(jnp.arange(rows, dtype=F32), GRID_W)
    col = jnp.tile(jnp.arange(GRID_W, dtype=F32), rows)
    n_freq = HEAD_DIM // 4
    inv = jnp.power(ROPE_THETA, jnp.arange(n_freq, dtype=F32) * (-2.0 / (HEAD_DIM // 2)))
    ang = jnp.concatenate([row[:, None] * inv, col[:, None] * inv], axis=-1)
    cos, sin = jnp.cos(ang), jnp.sin(ang)
    cexp = jnp.repeat(cos, 2, axis=-1)
    sexp = jnp.stack([-sin, sin], axis=-1).reshape(n, HEAD_DIM)
    return cexp, sexp


V_EXT_W = 2 * HEAD_DIM


def qkv_prep_fwd(p, wq, wk, cexp, sexp, *, latent, name, rows_all=None, k_all=None, v_ext=None):
    n = p.shape[0]
    tt = _rows_tile(n, 256)
    width = 2 * KV_W + (ATTN_W if latent else 0)

    def body(*refs):
        if latent:
            p_ref, wq_ref, wk_ref, c_ref, s_ref, q_ref, k_ref, v_ref = refs
        else:
            p_ref, wk_ref, _, _, k_ref, v_ref = refs

        def head(xv, w):
            rstd = lax.rsqrt(jnp.mean(xv * xv, axis=-1, keepdims=True) + EPS)
            yv = xv * rstd * w
            if latent:
                yv = yv * c_ref[...] + _swap_pairs(yv) * s_ref[...]
            return yv

        for h in range(N_KV_HEADS):
            sl = slice(h * HEAD_DIM, (h + 1) * HEAD_DIM)
            k_ref[:, sl] = head(p_ref[:, sl], wk_ref[...]).astype(BF16)
            v_ref[:, h * V_EXT_W:h * V_EXT_W + HEAD_DIM] = p_ref[:, KV_W + h * HEAD_DIM:KV_W + (h + 1) * HEAD_DIM].astype(BF16)
            lane = lax.broadcasted_iota(jnp.int32, (tt, HEAD_DIM), 1)
            v_ref[:, h * V_EXT_W + HEAD_DIM:(h + 1) * V_EXT_W] = jnp.where(lane == 0, 1.0, 0.0).astype(BF16)
        if latent:
            for h in range(N_Q_HEADS):
                sl = slice(2 * KV_W + h * HEAD_DIM, 2 * KV_W + (h + 1) * HEAD_DIM)
                q_ref[:, h * HEAD_DIM:(h + 1) * HEAD_DIM] = (head(p_ref[:, sl], wq_ref[...]) * Q_SCALE).astype(BF16)

    vec = pl.BlockSpec((1, HEAD_DIM), lambda i: (0, 0))
    tab = pl.BlockSpec((tt, HEAD_DIM), lambda i: (i, 0))
    vw = N_KV_HEADS * V_EXT_W
    k_shape = jax.ShapeDtypeStruct((rows_all, KV_W), BF16)
    v_shape = jax.ShapeDtypeStruct((rows_all, vw), BF16)
    aliases = {}
    if latent:
        in_specs = [pl.BlockSpec((tt, width), lambda i: (i, 0)), vec, vec, tab, tab]
        ops = [p, _row(wq), _row(wk), cexp, sexp]
        out_specs = [pl.BlockSpec((tt, ATTN_W), lambda i: (i, 0)), pl.BlockSpec((tt, KV_W), lambda i: (i, 0)),
                     pl.BlockSpec((tt, vw), lambda i: (i, 0))]
        out_shape = [jax.ShapeDtypeStruct((n, ATTN_W), BF16), k_shape, v_shape]
    else:
        assert n == tt and (rows_all - n) % n == 0, (n, tt, rows_all)
        first = (rows_all - n) // n
        in_specs = [pl.BlockSpec((tt, width), lambda i: (i, 0)), vec, ANY, ANY]
        ops = [p, _row(wk), k_all, v_ext]
        out_specs = [pl.BlockSpec((tt, KV_W), lambda i: (first, 0)), pl.BlockSpec((tt, vw), lambda i: (first, 0))]
        out_shape = [k_shape, v_shape]
        aliases = {2: 0, 3: 1}
    return pl.pallas_call(
        body, name=name, grid=(n // tt,), in_specs=in_specs, out_specs=out_specs, out_shape=out_shape,
        input_output_aliases=aliases, compiler_params=_params("parallel"),
    )(*ops)


def qkv_prep_bwd(p, dq, dk, dv, wq, wk, cexp, sexp, *, latent, name):
    n = p.shape[0]
    tt = _rows_tile(n, 256)
    width = 2 * KV_W + (ATTN_W if latent else 0)

    def body(*refs):
        if latent:
            p_ref, dq_ref, dk_ref, dv_ref, wq_ref, wk_ref, c_ref, s_ref, dp_ref, acc_ref = refs
        else:
            p_ref, dk_ref, dv_ref, wk_ref, dp_ref, acc_ref = refs
        i = pl.program_id(0)

        @pl.when(i == 0)
        def _():
            acc_ref[...] = jnp.zeros_like(acc_ref)

        def head(xv, dy, w, row):
            if latent:
                dy = dy * c_ref[...] + _swap_pairs(dy * s_ref[...])
            rstd = lax.rsqrt(jnp.mean(xv * xv, axis=-1, keepdims=True) + EPS)
            xhat = xv * rstd
            acc_ref[row:row + 1, :] += jnp.sum(dy * xhat, axis=0, keepdims=True)
            dxhat = dy * w
            return rstd * (dxhat - xhat * jnp.mean(dxhat * xhat, axis=-1, keepdims=True))

        for h in range(N_KV_HEADS):
            sl = slice(h * HEAD_DIM, (h + 1) * HEAD_DIM)
            dp_ref[:, sl] = head(p_ref[:, sl], dk_ref[:, sl], wk_ref[...], 1).astype(BF16)
        dp_ref[:, KV_W:2 * KV_W] = dv_ref[...].astype(BF16)
        if latent:
            for h in range(N_Q_HEADS):
                sl = slice(2 * KV_W + h * HEAD_DIM, 2 * KV_W + (h + 1) * HEAD_DIM)
                dyq = dq_ref[:, h * HEAD_DIM:(h + 1) * HEAD_DIM] * Q_SCALE
                dp_ref[:, sl] = head(p_ref[:, sl], dyq, wq_ref[...], 0).astype(BF16)

    vec = pl.BlockSpec((1, HEAD_DIM), lambda i: (0, 0))
    tab = pl.BlockSpec((tt, HEAD_DIM), lambda i: (i, 0))
    kv_spec = pl.BlockSpec((tt, KV_W), lambda i: (i, 0))
    p_spec = pl.BlockSpec((tt, width), lambda i: (i, 0))
    if latent:
        in_specs = [p_spec, pl.BlockSpec((tt, ATTN_W), lambda i: (i, 0)), kv_spec, kv_spec, vec, vec, tab, tab]
        ops = [p, dq, dk, dv, _row(wq), _row(wk), cexp, sexp]
    else:
        in_specs = [p_spec, kv_spec, kv_spec, vec]
        ops = [p, dk, dv, _row(wk)]
    return pl.pallas_call(
        body, name=name, grid=(n // tt,), in_specs=in_specs,
        out_specs=[p_spec, pl.BlockSpec((8, HEAD_DIM), lambda i: (0, 0))],
        out_shape=[jax.ShapeDtypeStruct((n, width), BF16), jax.ShapeDtypeStruct((8, HEAD_DIM), F32)],
        compiler_params=_params("arbitrary"),
    )(*ops)


def _kv_chunks(n, s_all):
    step = 1024 if n % 1024 == 0 else 256
    chunks = [(s, step) for s in range(0, n, step)]
    if s_all > n:
        chunks.append((n, s_all - n))
    return chunks


def flash_fwd(q, k_all, v_ext, p, *, za_block, mix_width, name):
    n = q.shape[0]
    s_all = k_all.shape[0]
    tq = _rows_tile(n, 512)
    chunks = _kv_chunks(n, s_all)
    wide = 2 * HEAD_DIM

    def body(q_ref, k_ref, v_ref, za_ref, o_ref, mix_ref, lse_ref, m_ref, acc_ref):
        qv = q_ref[...]
        m_ref[...] = jnp.full_like(m_ref, -jnp.inf)
        acc_ref[...] = jnp.zeros_like(acc_ref)
        for start, size in chunks:
            kc = k_ref[pl.ds(start, size), :]
            vc = v_ref[pl.ds(start, size), :]
            s = lax.dot_general(qv, kc, _DIMS["nt"], preferred_element_type=F32)
            m_old = m_ref[...]
            m_new = jnp.maximum(m_old, jnp.max(s, axis=-1, keepdims=True))
            pr = jnp.exp2(s - m_new)
            alpha = jnp.exp2(m_old - m_new)
            acc_ref[...] = alpha * acc_ref[...] + jnp.dot(pr.astype(BF16), vc, preferred_element_type=F32)
            m_ref[...] = m_new
        acc = acc_ref[...]
        denom = acc[:, HEAD_DIM:HEAD_DIM + 1]
        o = acc[:, :HEAD_DIM] / denom
        o_ref[...] = o.astype(BF16)
        mix_ref[...] = (o * _silu(za_ref[...])).astype(BF16)
        lse_ref[0] = _as_row(m_ref[...] + jnp.log2(denom))

    qspec = pl.BlockSpec((tq, HEAD_DIM), lambda h, i: (i, h))
    return pl.pallas_call(
        body, name=name, grid=(N_Q_HEADS, n // tq),
        in_specs=[qspec, pl.BlockSpec((s_all, HEAD_DIM), lambda h, i: (0, h // Q_PER_KV)),
                  pl.BlockSpec((s_all, wide), lambda h, i: (0, h // Q_PER_KV)),
                  pl.BlockSpec((tq, HEAD_DIM), lambda h, i: (i, za_block + h))],
        out_specs=[qspec, qspec, pl.BlockSpec((1, 1, tq), lambda h, i: (h, 0, i))],
        out_shape=[jax.ShapeDtypeStruct((n, ATTN_W), BF16), jax.ShapeDtypeStruct((n, mix_width), BF16),
                   jax.ShapeDtypeStruct((N_Q_HEADS, 1, n), F32)],
        scratch_shapes=[pltpu.VMEM((tq, 1), F32), pltpu.VMEM((tq, wide), F32)],
        compiler_params=_params("parallel", "parallel"),
    )(q, k_all, v_ext, p)


def attn_gate_bwd(dmix, o, p, *, za_block, name):
    n = o.shape[0]
    tt = _rows_tile(n, 512)
    za_half = za_block * HEAD_DIM // HALF
    n_half = ATTN_W // HALF

    def body(*refs):
        dm_ref, o_ref = refs[:2]
        za_refs = refs[2:2 + n_half]
        do_ref, dos_ref, dza_ref, delta_ref = refs[2 + n_half:]
        dm = dm_ref[...]
        ov = o_ref[...].astype(F32)
        za = jnp.concatenate([r[...] for r in za_refs], axis=-1)
        do = dm * _silu(za)
        do_ref[...] = do.astype(BF16)
        dos_ref[...] = (do * LN2).astype(BF16)
        dza_ref[...] = (dm * ov * _dsilu(za)).astype(BF16)
        prod = do * ov
        for h in range(N_Q_HEADS):
            col = jnp.sum(prod[:, h * HEAD_DIM:(h + 1) * HEAD_DIM], axis=-1, keepdims=True) * LN2
            delta_ref[h] = _as_row(col)

    spec = pl.BlockSpec((tt, ATTN_W), lambda i: (i, 0))
    shape = jax.ShapeDtypeStruct((n, ATTN_W), BF16)
    za_specs = [pl.BlockSpec((tt, HALF), functools.partial(lambda i, cb: (i, cb), cb=za_half + h))
                for h in range(n_half)]
    return pl.pallas_call(
        body, name=name, grid=(n // tt,),
        in_specs=[spec, spec] + za_specs,
        out_specs=[spec, spec, spec, pl.BlockSpec((N_Q_HEADS, 1, tt), lambda i: (0, 0, i))],
        out_shape=[shape, shape, shape, jax.ShapeDtypeStruct((N_Q_HEADS, 1, n), F32)],
        compiler_params=_params("parallel"),
    )(dmix, o, *([p] * n_half))


def flash_bwd(q, do, do_s, lse_row, delta_row, k_all, v_ext, *, name):
    n = q.shape[0]
    s_all = k_all.shape[0]
    tq = _rows_tile(n, 512)
    chunks = _kv_chunks(n, s_all)

    def body(q_ref, do_ref, dos_ref, lse_ref, dl_ref, k_ref, v_ref, dq_ref, dk_ref, dv_ref):
        g = pl.program_id(1)
        i = pl.program_id(2)

        @pl.when((g == 0) & (i == 0))
        def _():
            dk_ref[...] = jnp.zeros_like(dk_ref)
            dv_ref[...] = jnp.zeros_like(dv_ref)

        qv = q_ref[...]
        dov = do_ref[...]
        dosv = dos_ref[...]
        lse = lse_ref[0]
        dl = dl_ref[0]
        dq = jnp.zeros((tq, HEAD_DIM), F32)
        for start, size in chunks:
            kc = k_ref[pl.ds(start, size), :]
            vc = v_ref[pl.ds(start, size), :]
            st = lax.dot_general(kc, qv, _DIMS["nt"], preferred_element_type=F32)
            pt = jnp.exp2(st - lse)
            dpt = lax.dot_general(vc, dosv, _DIMS["nt"], preferred_element_type=F32)
            dst = (pt * (dpt - dl)).astype(BF16)
            dv_ref[pl.ds(start, size), :] += jnp.dot(pt.astype(BF16), dov, preferred_element_type=F32)
            dk_ref[pl.ds(start, size), :] += jnp.dot(dst, qv, preferred_element_type=F32)
            dq = dq + lax.dot_general(dst, kc, _DIMS["tn"], preferred_element_type=F32)
        dq_ref[...] = dq

    qspec = pl.BlockSpec((tq, HEAD_DIM), lambda kh, g, i: (i, kh * Q_PER_KV + g))
    rowspec = pl.BlockSpec((1, 1, tq), lambda kh, g, i: (kh * Q_PER_KV + g, 0, i))
    kvspec = pl.BlockSpec((s_all, HEAD_DIM), lambda kh, g, i: (0, kh))
    return pl.pallas_call(
        body, name=name, grid=(N_KV_HEADS, Q_PER_KV, n // tq),
        in_specs=[qspec, qspec, qspec, rowspec, rowspec, kvspec,
                  pl.BlockSpec((s_all, HEAD_DIM), lambda kh, g, i: (0, kh * (V_EXT_W // HEAD_DIM)))],
        out_specs=[qspec, kvspec, kvspec],
        out_shape=[jax.ShapeDtypeStruct((n, ATTN_W), F32),
                   jax.ShapeDtypeStruct((s_all, KV_W), F32), jax.ShapeDtypeStruct((s_all, KV_W), F32)],
        compiler_params=_params("arbitrary", "arbitrary", "arbitrary"),
    )(q, do, do_s, lse_row, delta_row, k_all, v_ext)


HALF = 512


SUB = 128


def _tap_group(w_ref, w_lanes, ext_ref, r0, lanes, r, tap_of):
    z = None
    for a in range(4):
        k = tap_of(8 * a + r)
        if 0 <= k < CONV_WIDTH:
            term = w_ref[k:k + 1, w_lanes] * ext_ref[pl.ds(r0 + 8 * a, SUB + 8), lanes]
            z = term if z is None else z + term
    return z


def _shifted(z, z_ref, r):
    if r == 0:
        return z[0:SUB]
    z_ref[...] = z
    return z_ref[pl.ds(r, SUB), :]


def _fold8(x):
    return jnp.sum(x.reshape(x.shape[0] // 8, 8, x.shape[1]), axis=0)


def _pieces(tt, h):
    return [(rh * SUB, slice(c * SUB, (c + 1) * SUB), slice(h * HALF + c * SUB, h * HALF + (c + 1) * SUB))
            for c in range(HALF // SUB) for rh in range(tt // SUB)]


def _halo_specs(tt, n, cb):
    per = tt // HALO
    last = n // HALO - 1
    return [pl.BlockSpec((HALO, HALF), lambda i: (jnp.maximum(i * per - 1, 0), cb)),
            pl.BlockSpec((tt, HALF), lambda i: (i, cb)),
            pl.BlockSpec((HALO, HALF), lambda i: (jnp.minimum((i + 1) * per, last), cb))]


def conv_fwd(p, dw_w, dw_b, ln_g, ln_b, mix, *, glu_block, name):
    n = p.shape[0]
    ch = dw_w.shape[1]
    nh = ch // HALF
    tt = _rows_tile(n, 256)
    last = n // tt - 1

    def body(*refs):
        a_refs = [refs[3 * h:3 * h + 3] for h in range(nh)]
        b_refs = [refs[3 * (nh + h):3 * (nh + h) + 3] for h in range(nh)]
        pos = 6 * nh
        zb_refs = refs[pos:pos + nh]
        pos += nh
        w_ref, bias_ref, g_ref, be_ref, _, ycv_ref, mix_ref, ext_ref, z_ref = refs[pos:pos + 9]
        i = pl.program_id(0)
        for h in range(nh):
            cs = slice(h * HALF, (h + 1) * HALF)
            ap, am, an = a_refs[h]
            bp, bm, bn = b_refs[h]
            ext_ref[0:HALO, :] = jnp.where(i > 0, ap[...] * _sigmoid(bp[...]), 0.0)
            ext_ref[HALO:HALO + tt, :] = am[...] * _sigmoid(bm[...])
            ext_ref[HALO + tt:2 * HALO + tt, :] = jnp.where(i < last, an[...] * _sigmoid(bn[...]), 0.0)
            for r0, lanes, w_lanes in _pieces(tt, h):
                acc = jnp.broadcast_to(bias_ref[:, w_lanes], (SUB, SUB))
                for r in range(8):
                    z = _tap_group(w_ref, w_lanes, ext_ref, r0, lanes, r, lambda j: j - 1)
                    acc = acc + _shifted(z, z_ref, r)
                ycv_ref[pl.ds(r0, SUB), w_lanes] = acc
        yc = ycv_ref[...]
        mu = jnp.mean(yc, axis=-1, keepdims=True)
        var = jnp.mean(jnp.square(yc - mu), axis=-1, keepdims=True)
        ln = (yc - mu) * lax.rsqrt(var + EPS) * g_ref[...] + be_ref[...]
        out = _silu(ln)
        for h in range(nh):
            cs = slice(h * HALF, (h + 1) * HALF)
            mix_ref[:, cs] = (out[:, cs] * _silu(zb_refs[h][...])).astype(BF16)

    in_specs = []
    for h in range(2 * nh):
        in_specs += _halo_specs(tt, n, glu_block + h)
    in_specs += [pl.BlockSpec((tt, HALF), functools.partial(lambda i, cb: (i, cb), cb=glu_block + 2 * nh + h))
                 for h in range(nh)]
    vec = pl.BlockSpec((1, ch), lambda i: (0, 0))
    in_specs += [pl.BlockSpec((CONV_WIDTH, ch), lambda i: (0, 0)), vec, vec, vec, ANY]
    ops = [p] * (6 * nh + nh) + [dw_w, _row(dw_b), _row(ln_g), _row(ln_b), mix]
    big = pl.BlockSpec((tt, ch), lambda i: (i, 0))
    mix_block = (mix.shape[1] - ch) // ch
    return pl.pallas_call(
        body, name=name, grid=(n // tt,), in_specs=in_specs,
        out_specs=[big, pl.BlockSpec((tt, ch), lambda i: (i, mix_block))],
        out_shape=[jax.ShapeDtypeStruct((n, ch), F32), jax.ShapeDtypeStruct(mix.shape, BF16)],
        input_output_aliases={len(ops) - 1: 1},
        scratch_shapes=[pltpu.VMEM((tt + 2 * HALO, HALF), F32), pltpu.VMEM((SUB + 8, SUB), F32)],
        compiler_params=_params("parallel"),
    )(*ops)


def conv_bwd_rows(dmix, ycv, p, ln_g, ln_b, *, mix_block, zb_block, name):
    n, ch = ycv.shape
    nh = ch // HALF
    tt = _rows_tile(n, 256)

    def body(*refs):
        dm_ref, ycv_ref = refs[:2]
        zb_refs = refs[2:2 + nh]
        g_ref, be_ref, dy_ref, dzb_ref, acc_ref = refs[2 + nh:]
        i = pl.program_id(0)

        @pl.when(i == 0)
        def _():
            acc_ref[...] = jnp.zeros_like(acc_ref)

        yc = ycv_ref[...]
        mu = jnp.mean(yc, axis=-1, keepdims=True)
        var = jnp.mean(jnp.square(yc - mu), axis=-1, keepdims=True)
        rstd = lax.rsqrt(var + EPS)
        xhat = (yc - mu) * rstd
        ln = xhat * g_ref[...] + be_ref[...]
        out = _silu(ln)
        dm = dm_ref[...]
        zb = jnp.concatenate([r[...] for r in zb_refs], axis=-1)
        dzb_ref[...] = (dm * out * _dsilu(zb)).astype(BF16)
        dln = dm * _silu(zb) * _dsilu(ln)
        acc_ref[0:1, :] += jnp.sum(dln * xhat, axis=0, keepdims=True)
        acc_ref[1:2, :] += jnp.sum(dln, axis=0, keepdims=True)
        dxhat = dln * g_ref[...]
        dy_ref[...] = rstd * (dxhat - jnp.mean(dxhat, axis=-1, keepdims=True)
                              - xhat * jnp.mean(dxhat * xhat, axis=-1, keepdims=True))

    big = pl.BlockSpec((tt, ch), lambda i: (i, 0))
    vec = pl.BlockSpec((1, ch), lambda i: (0, 0))
    in_specs = [pl.BlockSpec((tt, ch), lambda i: (i, mix_block)), big]
    in_specs += [pl.BlockSpec((tt, HALF), functools.partial(lambda i, cb: (i, cb), cb=zb_block + h)) for h in range(nh)]
    in_specs += [vec, vec]
    return pl.pallas_call(
        body, name=name, grid=(n // tt,), in_specs=in_specs,
        out_specs=[big, big, pl.BlockSpec((8, ch), lambda i: (0, 0))],
        out_shape=[jax.ShapeDtypeStruct((n, ch), F32), jax.ShapeDtypeStruct((n, ch), BF16),
                   jax.ShapeDtypeStruct((8, ch), F32)],
        compiler_params=_params("arbitrary"),
    )(dmix, ycv, *([p] * nh), _row(ln_g), _row(ln_b))


def conv_bwd_taps(dycv, p, dw_w, *, glu_block, name):
    n, ch = dycv.shape
    nh = ch // HALF
    tt = _rows_tile(n, 256)
    last = n // tt - 1

    def body(*refs):
        d_refs = [refs[3 * h:3 * h + 3] for h in range(nh)]
        a_refs = [refs[3 * (nh + h):3 * (nh + h) + 3] for h in range(nh)]
        b_refs = [refs[3 * (2 * nh + h):3 * (2 * nh + h) + 3] for h in range(nh)]
        w_ref, dglu_ref, dw_ref, db_ref, yext_ref, dext_ref, z_ref, dy_ref, dwp_ref, dbp_ref = refs[9 * nh:]
        i = pl.program_id(0)

        @pl.when(i == 0)
        def _():
            dwp_ref[...] = jnp.zeros_like(dwp_ref)
            dbp_ref[...] = jnp.zeros_like(dbp_ref)

        for h in range(nh):
            cs = slice(h * HALF, (h + 1) * HALF)
            ap, am, an = a_refs[h]
            bp, bm, bn = b_refs[h]
            dp, dm, dn = d_refs[h]
            av = am[...]
            sb = _sigmoid(bm[...])
            yext_ref[0:HALO, :] = jnp.where(i > 0, ap[...] * _sigmoid(bp[...]), 0.0)
            yext_ref[HALO:HALO + tt, :] = av * sb
            yext_ref[HALO + tt:2 * HALO + tt, :] = jnp.where(i < last, an[...] * _sigmoid(bn[...]), 0.0)
            dmain = dm[...]
            dext_ref[0:HALO, :] = jnp.where(i > 0, dp[...], 0.0)
            dext_ref[HALO:HALO + tt, :] = dmain
            dext_ref[HALO + tt:2 * HALO + tt, :] = jnp.where(i < last, dn[...], 0.0)
            for r0, lanes, w_lanes in _pieces(tt, h):
                dsub = dext_ref[pl.ds(HALO + r0, SUB), lanes]
                dy = jnp.zeros((SUB, SUB), F32)
                for r in range(8):
                    z = _tap_group(w_ref, w_lanes, dext_ref, r0, lanes, r, lambda j: CONV_WIDTH - j)
                    dy = dy + _shifted(z, z_ref, r)
                    y_r = yext_ref[pl.ds(r0 + r, SUB + 24), lanes]
                    for a in range(4):
                        k = 8 * a + r - 1
                        if k >= 0:
                            dwp_ref[8 * k:8 * k + 8, w_lanes] += _fold8(dsub * y_r[8 * a:8 * a + SUB])
                dbp_ref[:, w_lanes] += _fold8(dsub)
                dy_ref[pl.ds(r0, SUB), lanes] = dy
            dy = dy_ref[...]
            dglu_ref[:, cs] = (dy * sb).astype(BF16)
            dglu_ref[:, ch + h * HALF:ch + (h + 1) * HALF] = (dy * av * sb * (1.0 - sb)).astype(BF16)

        @pl.when(i == last)
        def _():
            dw_ref[...] = jnp.sum(dwp_ref[...].reshape(32, 8, ch), axis=1)
            db_ref[...] = jnp.broadcast_to(jnp.sum(dbp_ref[...], axis=0, keepdims=True), db_ref.shape)

    in_specs = []
    for h in range(nh):
        in_specs += _halo_specs(tt, n, h)
    for h in range(2 * nh):
        in_specs += _halo_specs(tt, n, glu_block + h)
    in_specs += [pl.BlockSpec((CONV_WIDTH, ch), lambda i: (0, 0))]
    ops = [dycv] * (3 * nh) + [p] * (6 * nh) + [dw_w]
    return pl.pallas_call(
        body, name=name, grid=(n // tt,), in_specs=in_specs,
        out_specs=[pl.BlockSpec((tt, 2 * ch), lambda i: (i, 0)), pl.BlockSpec((32, ch), lambda i: (0, 0)),
                   pl.BlockSpec((8, ch), lambda i: (0, 0))],
        out_shape=[jax.ShapeDtypeStruct((n, 2 * ch), BF16), jax.ShapeDtypeStruct((32, ch), F32),
                   jax.ShapeDtypeStruct((8, ch), F32)],
        scratch_shapes=[pltpu.VMEM((tt + 2 * HALO, HALF), F32), pltpu.VMEM((tt + 2 * HALO, HALF), F32),
                        pltpu.VMEM((SUB + 8, SUB), F32), pltpu.VMEM((tt, HALF), F32),
                        pltpu.VMEM((8 * 32, ch), F32), pltpu.VMEM((8, ch), F32)],
        compiler_params=_params("arbitrary"),
    )(*ops)


def _sgu_common(p_ref, g_ref, be_ref, ws_ref, bs_ref, w):
    gw = w // SGU_GROUPS
    u_pre = p_ref[:, 0:w]
    v_pre = p_ref[:, w:2 * w]
    zc = p_ref[:, 2 * w:3 * w]
    u = _gelu(u_pre)
    v = _gelu(v_pre)
    mu = jnp.mean(v, axis=-1, keepdims=True)
    var = jnp.mean(jnp.square(v - mu), axis=-1, keepdims=True)
    rstd = lax.rsqrt(var + EPS)
    vhat = (v - mu) * rstd
    vn = (vhat * g_ref[...] + be_ref[...]).astype(BF16)
    mixed = jnp.concatenate(
        [jnp.dot(ws_ref[gi].astype(BF16), vn[:, gi * gw:(gi + 1) * gw], preferred_element_type=F32)
         + bs_ref[:, gi:gi + 1] for gi in range(SGU_GROUPS)], axis=-1)
    return u_pre, v_pre, zc, u, rstd, vhat, vn, mixed


def sgu_fwd(p, ln_g, ln_b, ws, bs_t, *, name):
    n, w3 = p.shape
    w = w3 // 3

    def body(p_ref, g_ref, be_ref, ws_ref, bs_ref, m_ref):
        _, _, zc, u, _, _, _, mixed = _sgu_common(p_ref, g_ref, be_ref, ws_ref, bs_ref, w)
        m_ref[...] = (u * mixed * _silu(zc)).astype(BF16)

    vec = pl.BlockSpec((1, w), lambda i: (0, 0))
    return pl.pallas_call(
        body, name=name, grid=(n // CHUNK,),
        in_specs=[pl.BlockSpec((CHUNK, w3), lambda i: (i, 0)), vec, vec,
                  pl.BlockSpec((SGU_GROUPS, CHUNK, CHUNK), lambda i: (0, 0, 0)),
                  pl.BlockSpec((CHUNK, SGU_GROUPS), lambda i: (0, 0))],
        out_specs=pl.BlockSpec((CHUNK, w), lambda i: (i, 0)),
        out_shape=jax.ShapeDtypeStruct((n, w), BF16),
        compiler_params=_params("parallel"),
    )(p, _row(ln_g), _row(ln_b), ws, bs_t)


def sgu_bwd(p, dm, ln_g, ln_b, ws, ws_t, bs_t, *, name):
    n, w3 = p.shape
    w = w3 // 3
    gw = w // SGU_GROUPS

    def body(p_ref, dm_ref, g_ref, be_ref, ws_ref, wst_ref, bs_ref, dp_ref, dws_ref, dbs_ref, acc_ref):
        i = pl.program_id(0)

        @pl.when(i == 0)
        def _():
            dws_ref[...] = jnp.zeros_like(dws_ref)
            dbs_ref[...] = jnp.zeros_like(dbs_ref)
            acc_ref[...] = jnp.zeros_like(acc_ref)

        u_pre, v_pre, zc, u, rstd, vhat, vn, mixed = _sgu_common(p_ref, g_ref, be_ref, ws_ref, bs_ref, w)
        dmv = dm_ref[...]
        um = u * mixed
        dp_ref[:, 2 * w:3 * w] = (dmv * um * _dsilu(zc)).astype(BF16)
        dum = dmv * _silu(zc)
        dp_ref[:, 0:w] = (dum * mixed * _dgelu(u_pre)).astype(BF16)
        dmixed = dum * u
        dmixed_b = dmixed.astype(BF16)
        dvn_parts = []
        for gi in range(SGU_GROUPS):
            cs = slice(gi * gw, (gi + 1) * gw)
            dws_ref[gi] += lax.dot_general(dmixed_b[:, cs], vn[:, cs], _DIMS["nt"], preferred_element_type=F32)
            dbs_ref[:, gi:gi + 1] += jnp.sum(dmixed[:, cs], axis=-1, keepdims=True)
            dvn_parts.append(jnp.dot(wst_ref[gi].astype(BF16), dmixed_b[:, cs], preferred_element_type=F32))
        dvn = jnp.concatenate(dvn_parts, axis=-1)
        acc_ref[0:1, :] += jnp.sum(dvn * vhat, axis=0, keepdims=True)
        acc_ref[1:2, :] += jnp.sum(dvn, axis=0, keepdims=True)
        dvhat = dvn * g_ref[...]
        dv = rstd * (dvhat - jnp.mean(dvhat, axis=-1, keepdims=True)
                     - vhat * jnp.mean(dvhat * vhat, axis=-1, keepdims=True))
        dp_ref[:, w:2 * w] = (dv * _dgelu(v_pre)).astype(BF16)

    vec = pl.BlockSpec((1, w), lambda i: (0, 0))
    wspec = pl.BlockSpec((SGU_GROUPS, CHUNK, CHUNK), lambda i: (0, 0, 0))
    bspec = pl.BlockSpec((CHUNK, SGU_GROUPS), lambda i: (0, 0))
    return pl.pallas_call(
        body, name=name, grid=(n // CHUNK,),
        in_specs=[pl.BlockSpec((CHUNK, w3), lambda i: (i, 0)), pl.BlockSpec((CHUNK, w), lambda i: (i, 0)),
                  vec, vec, wspec, wspec, bspec],
        out_specs=[pl.BlockSpec((CHUNK, w3), lambda i: (i, 0)), wspec, bspec,
                   pl.BlockSpec((8, w), lambda i: (0, 0))],
        out_shape=[jax.ShapeDtypeStruct((n, w3), BF16), jax.ShapeDtypeStruct((SGU_GROUPS, CHUNK, CHUNK), F32),
                   jax.ShapeDtypeStruct((CHUNK, SGU_GROUPS), F32), jax.ShapeDtypeStruct((8, w), F32)],
        compiler_params=_params("arbitrary"),
    )(p, dm, _row(ln_g), _row(ln_b), ws, ws_t, bs_t)


def _adam_math(w, g, m, v):
    m_new = ADAM_B1 * m + (1.0 - ADAM_B1) * g
    v_new = ADAM_B2 * v + (1.0 - ADAM_B2) * (g * g)
    m_hat = m_new / (1.0 - ADAM_B1 ** ADAM_STEP)
    v_hat = v_new / (1.0 - ADAM_B2 ** ADAM_STEP)
    delta = -ADAM_LR * (m_hat / (jnp.sqrt(v_hat) + ADAM_EPS) + ADAM_WD * w)
    return delta, m_new, v_new


def adamw(w, g, m, v, *, name, slots=False, rows=512):
    r, c = w.shape
    tr = min(r, rows)
    assert r % tr == 0, (name, r, tr)

    def body(w_ref, g_ref, m_ref, v_ref, go_ref, d_ref, mo_ref, vo_ref):
        if slots:
            g = g_ref[0].astype(F32)
            for k in range(1, N_DEV):
                g = g + g_ref[k].astype(F32)
        else:
            g = g_ref[...].astype(F32)
        delta, m_new, v_new = _adam_math(w_ref[...], g, m_ref[...], v_ref[...])
        go_ref[...] = g
        d_ref[...] = delta
        mo_ref[...] = m_new
        vo_ref[...] = v_new

    spec = pl.BlockSpec((tr, c), lambda i: (i, 0))
    gspec = pl.BlockSpec((N_DEV, tr, c), lambda i: (0, i, 0)) if slots else spec
    shape = jax.ShapeDtypeStruct((r, c), F32)
    return pl.pallas_call(
        body, name=name, grid=(r // tr,),
        in_specs=[spec, gspec, spec, spec], out_specs=[spec] * 4, out_shape=[shape] * 4,
        compiler_params=_params("parallel"),
    )(w, g, m, v)


def slot_sum(slots, *, name):
    _, r, c = slots.shape

    def body(s_ref, o_ref):
        acc = s_ref[0]
        for k in range(1, N_DEV):
            acc = acc + s_ref[k]
        o_ref[...] = acc

    return pl.pallas_call(
        body, name=name, out_shape=jax.ShapeDtypeStruct((r, c), F32),
        in_specs=[VMEM_SPEC], out_specs=VMEM_SPEC,
        compiler_params=pltpu.CompilerParams(vmem_limit_bytes=VMEM_LIMIT),
    )(slots)


def _after(token, val):
    return val + token[0, 0]


def _pad_rows(a, rows):
    return jnp.pad(a, ((0, rows - a.shape[0]), (0, 0)))


def kernel(x, c, ctx, c_ctx, ada_w, ada_b, norm_g, ev_w_in, ev_q_norm, ev_k_norm, ev_dw_w, ev_dw_b, ev_ln_g, ev_ln_b, ev_w_out, od_w_in, od_ln_g, od_ln_b, od_ws, od_bs, od_w_out, final_g, loss_target, m_c_ctx, m_ada_w, m_ada_b, m_norm_g, m_ev_w_in, m_ev_q_norm, m_ev_k_norm, m_ev_dw_w, m_ev_dw_b, m_ev_ln_g, m_ev_ln_b, m_ev_w_out, m_od_w_in, m_od_ln_g, m_od_ln_b, m_od_ws, m_od_bs, m_od_w_out, m_final_g, v_c_ctx, v_ada_w, v_ada_b, v_norm_g, v_ev_w_in, v_ev_q_norm, v_ev_k_norm, v_ev_dw_w, v_ev_dw_b, v_ev_ln_g, v_ev_ln_b, v_ev_w_out, v_od_w_in, v_od_ln_g, v_od_ln_b, v_od_ws, v_od_bs, v_od_w_out, v_final_g):
    n, d = x.shape[1], x.shape[2]
    lc = ctx.shape[1]
    ev_in = ev_w_in.shape[2] * N_DEV
    od_in = od_w_in.shape[2] * N_DEV
    conv_ch = ev_dw_w.shape[2] * N_DEV
    ada_cols = ada_w.shape[2]
    me = 4 * lax.axis_index("x") + 2 * lax.axis_index("y") + lax.axis_index("c")
    xs, tgt, ctxs = x[0], loss_target[0], ctx[0]
    za_block = (2 * KV_W + ATTN_W) // HEAD_DIM
    glu_block = (2 * KV_W + 2 * ATTN_W) // HALF
    zb_block = glu_block + 2 * conv_ch // HALF

    small = jnp.concatenate([
        jax.nn.silu(c).reshape(1, d),
        od_ln_g.reshape(1, -1), od_ln_b.reshape(1, -1)], axis=1)
    small = _pad_rows(small, 8)
    dw_rows = _pad_rows(ev_dw_w[0], 32)
    small_g, dw_g = all_gather([small, dw_rows], name="gather_small", hbm=False)
    sc_all = small_g[:, 0, :d]
    shard = d // N_DEV
    od_ln_g_full = small_g[:, 0, d:d + shard].reshape(d)
    od_ln_b_full = small_g[:, 0, d + shard:d + 2 * shard].reshape(d)
    dw_w_full = jnp.moveaxis(dw_g, 0, 1).reshape(32, conv_ch)[:CONV_WIDTH]
    scc = jax.nn.silu(c_ctx)
    sc16 = _pad_rows(jnp.concatenate([sc_all, scc.reshape(1, d)], axis=0), 16)

    ada_bf = ada_w.astype(BF16)
    mod_loc = [matmul(sc16, ada_bf[l], mode="nn", tm=16, tn=ada_cols, tk=d, out_dtype=F32, name=f"ada_mod{l}")
               for l in range(2)]
    (mod_g,) = all_gather([jnp.stack(mod_loc)], name="gather_mod", hbm=False)
    mod_all = jnp.moveaxis(mod_g, 0, 2).reshape(2, 16, N_DEV * ada_cols) + ada_b[:, None, :]
    mod_me = lax.dynamic_index_in_dim(mod_all, me, axis=1, keepdims=False)
    shift = [mod_me[l, :d] for l in range(2)]
    scale = [mod_me[l, d:2 * d] for l in range(2)]
    gate = [mod_me[l, 2 * d:] for l in range(2)]
    shift_c, scale_c = mod_all[0, 8, :d], mod_all[0, 8, d:2 * d]

    (wi0_g,) = all_gather([ev_w_in[0].astype(BF16)], name="gather_w_in0", hbm=True)
    wi0 = jnp.moveaxis(wi0_g, 0, 1).reshape(d, ev_in)
    bits = lax.bitcast_convert_type(wi0_g[0, 0, 0], jnp.uint16)
    landed_zero = jnp.where((bits | 1) == 0, 1.0, 0.0).astype(F32)
    later = [(w[0] + landed_zero).astype(BF16) for w in (ev_w_out, od_w_in, od_w_out)]
    (h_wo0, h_wi1, h_wo1), w_token = exchange_start(later, scatter=False, name="gather_rest_start")

    def landed(handle, after, name):
        own, land = exchange_wait(handle, after, scatter=False, name=name)
        return lax.dynamic_update_slice_in_dim(land, own[None], me, axis=0)

    cexp, sexp = rope_tables(n)

    shift0 = _after(w_token, shift[0])
    h0 = norm_mod_fwd(xs, norm_g[0], shift0, scale[0], name="norm_mod_fwd0")
    hc = norm_mod_fwd(ctxs, norm_g[0], shift_c, scale_c, name="norm_mod_fwd_ctx")
    p0 = matmul(h0, wi0, mode="nn", tm=1024, tn=ev_in // 4, tk=d, out_dtype=F32, name="proj_in0")
    pc = matmul(hc, wi0, mode="nn", tm=lc, tn=2 * KV_W, tk=d, out_dtype=F32, n_out=2 * KV_W, name="proj_in_ctx")
    q_r, k_all, v_ext = qkv_prep_fwd(p0, ev_q_norm[0], ev_k_norm[0], cexp, sexp, latent=True, name="qkv_prep",
                                     rows_all=n + lc)
    k_all, v_ext = qkv_prep_fwd(pc, None, ev_k_norm[0], None, None, latent=False, name="kv_prep_ctx",
                                rows_all=n + lc, k_all=k_all, v_ext=v_ext)
    o_attn, mix_a, lse = flash_fwd(q_r, k_all, v_ext, p0, za_block=za_block, mix_width=ATTN_W + conv_ch,
                                   name="flash_fwd")
    ycv, mix0 = conv_fwd(p0, dw_w_full, ev_dw_b[0], ev_ln_g[0], ev_ln_b[0], mix_a, glu_block=glu_block,
                         name="conv_fwd")
    wo0 = landed(h_wo0, ycv, "gather_w_out0_wait").reshape(-1, d)
    x1, r0, h1 = matmul(mix0, wo0, mode="nn", tm=512, tn=d, tk=mix0.shape[1], out_dtype=F32, name="proj_out0",
                        res=xs, gate=_row(gate[0]), norm=(norm_g[1], shift[1], scale[1]))

    wi1 =jnp.moveaxis(landed(h_wi1, h1, "gather_w_in1_wait"), 0, 1).reshape(d, od_in)
    p1 = matmul(h1, wi1, mode="nn", tm=1024, tn=od_in // 4, tk=d, out_dtype=F32, name="proj_in1")
    ws_bf = od_ws[0]
    bs_t = od_bs[0].T
    m1 = sgu_fwd(p1, od_ln_g_full, od_ln_b_full, ws_bf, bs_t, name="sgu_fwd")
    wo1 = landed(h_wo1, m1, "gather_w_out1_wait").reshape(-1, d)
    x2, r1 = matmul(m1, wo1, mode="nn", tm=1024, tn=1024, tk=m1.shape[1], out_dtype=F32, name="proj_out1",
                    res=x1, gate=_row(gate[1]))

    dx2, dr1, acc_final, loss_tile = final_loss(x2, tgt, final_g, r1, gate[1], name="final_loss")

    dm1 = matmul(dr1, wo1, mode="nt", tm=1024, tn=1024, tk=d, out_dtype=F32, name="d_mix1")
    dwo1 = matmul(m1, dr1, mode="tn", tm=1024, tn=1024, tk=2048, out_dtype=BF16, name="d_wout1")
    dp1, dws, dbs_t, acc_sgu = sgu_bwd(p1, dm1, od_ln_g_full, od_ln_b_full, ws_bf, jnp.swapaxes(ws_bf, 1, 2), bs_t,
                                       name="sgu_bwd")
    dwi1_s = matmul(h1, dp1, mode="tn", tm=2048, tn=od_in // N_DEV, tk=2048, out_dtype=BF16, name="d_win1",
                    split_out=True)
    dwo1_s = dwo1.reshape(N_DEV, -1, d)
    (h_gi1, h_go1), g1_token = exchange_start([dwi1_s, dwo1_s], scatter=True, name="grads1_start")
    dh1 = matmul(dp1, wi1, mode="nt", tm=1024, tn=512, tk=od_in, out_dtype=F32, name="d_h1")
    dx1, dr0, acc_norm1 = norm_mod_bwd(x1, dh1, dx2, norm_g[1], _after(g1_token, scale[1]), name="norm_mod_bwd1",
                                       branch=(r0, gate[0]))

    dmix0 = matmul(dr0, wo0, mode="nt", tm=1024, tn=1024, tk=d, out_dtype=F32, name="d_mix0")
    dwo0 = matmul(mix0, dr0, mode="tn", tm=1024, tn=1024, tk=2048, out_dtype=BF16, name="d_wout0")
    do_attn, do_s, dza, delta = attn_gate_bwd(dmix0, o_attn, p0, za_block=za_block, name="attn_gate_bwd")
    dycv, dzb, acc_ln = conv_bwd_rows(dmix0, ycv, p0, ev_ln_g[0], ev_ln_b[0], mix_block=ATTN_W // conv_ch,
                                      zb_block=zb_block, name="conv_bwd_rows")
    dglu, ddw_w, acc_dwb = conv_bwd_taps(dycv, p0, dw_w_full, glu_block=glu_block, name="conv_bwd_taps")
    dq_r, dk_all, dv_all = flash_bwd(q_r, do_attn, do_s, lse, delta, k_all, v_ext, name="flash_bwd")
    dkvq, acc_qk = qkv_prep_bwd(p0, dq_r, dk_all[:n], dv_all[:n], ev_q_norm[0], ev_k_norm[0], cexp, sexp,
                                latent=True, name="qkv_prep_bwd")
    dpc, acc_kc = qkv_prep_bwd(pc, None, dk_all[n:], dv_all[n:], None, ev_k_norm[0], None, None,
                               latent=False, name="kv_prep_ctx_bwd")
    dp0 = jnp.concatenate([dkvq, dza, dglu, dzb], axis=1)
    dwi0 = matmul(h0, dp0, mode="tn", tm=1024, tn=ev_in // 4, tk=2048, out_dtype=F32, name="d_win0")
    dwi0 = matmul(hc, dpc, mode="tn", tm=512, tn=2 * KV_W, tk=lc, out_dtype=F32, name="d_win0_ctx", add=dwi0)
    dwi0_s = jnp.moveaxis(dwi0.astype(BF16).reshape(d, N_DEV, ev_in // N_DEV), 1, 0)
    dwo0_s = dwo0.reshape(N_DEV, -1, d)
    (h_gi0, h_go0), g0_token = exchange_start([dwi0_s, dwo0_s], scatter=True, name="grads0_start")
    dh0 = matmul(dp0, wi0, mode="nt", tm=1024, tn=512, tk=ev_in, out_dtype=F32, name="d_h0", after=g0_token)
    dhc = matmul(dpc, wi0, mode="nt", tm=lc, tn=512, tk=2 * KV_W, out_dtype=F32, name="d_h_ctx")
    grad_x, acc_norm0 = norm_mod_bwd(xs, dh0, dx1, norm_g[0], _after(g0_token, scale[0]), name="norm_mod_bwd0")
    _, acc_normc = norm_mod_bwd(ctxs, dhc, None, norm_g[0], scale_c, name="norm_mod_bwd_ctx")

    zeros_d = jnp.zeros((d,), F32)
    dmod0 = jnp.stack([acc_norm0[0], acc_norm0[1], acc_norm1[GATE_ROW]])
    dmod1 = jnp.stack([acc_norm1[0], acc_norm1[1], acc_final[GATE_ROW]])
    dmodc = jnp.stack([acc_normc[0], acc_normc[1]])
    half_pad = jnp.zeros((d - 2 * conv_ch,), F32) if d > 2 * conv_ch else jnp.zeros((0,), F32)
    row_a = jnp.concatenate([acc_dwb[0], acc_ln[0], half_pad])
    row_b = jnp.concatenate([acc_ln[1], acc_qk[0], acc_qk[1] + acc_kc[1],
                             jnp.zeros((d - conv_ch - 2 * HEAD_DIM,), F32)])
    row_c = jnp.concatenate([dbs_t.T.reshape(-1), jnp.zeros((d - SGU_GROUPS * CHUNK,), F32)])
    row_loss = jnp.concatenate([loss_tile[0, :1], jnp.zeros((d - 1,), F32)])
    pack = jnp.concatenate([
        dmod0, dmod1, dmodc,
        (acc_norm0[2] + acc_normc[2])[None], acc_norm1[2][None],
        acc_final[0][None],
        acc_sgu[0][None], acc_sgu[1][None],
        row_a[None], row_b[None], row_c[None], row_loss[None],
        ddw_w.reshape(-1, d),
        dws.reshape(-1, d),
    ], axis=0)
    n_rows = pack.shape[0]
    pack = _pad_rows(pack, -(-n_rows // 8) * 8)
    (h_pack,), pack_token = exchange_start([pack], scatter=False, name="small_grads_start")

    def summands(handle, after, name):
        mine, land = exchange_wait(handle, after, scatter=True, name=name)
        own = lax.dynamic_index_in_dim(mine, me, axis=0, keepdims=True)
        return lax.dynamic_update_slice_in_dim(land, own, me, axis=0)

    out = {}

    def upd(key, w, g, m, v, slots=False, rows=512):
        shp = w.shape
        w2 = w.reshape(-1, shp[-1])
        g2 = g.reshape((N_DEV, -1, shp[-1])) if slots else g.reshape(-1, shp[-1])
        res = adamw(w2, g2, m.reshape(w2.shape), v.reshape(w2.shape), name="adamw_" + key, slots=slots, rows=rows)
        out[key] = tuple(r.reshape(shp) for r in res)

    gi1 = summands(h_gi1, pack_token, "grads_w_in1_wait")
    upd("od_w_in", od_w_in, gi1, m_od_w_in, v_od_w_in, slots=True, rows=256)
    go1 = summands(h_go1, out["od_w_in"][1], "grads_w_out1_wait")
    upd("od_w_out", od_w_out, go1, m_od_w_out, v_od_w_out, slots=True, rows=256)
    go0 = summands(h_go0, out["od_w_out"][1], "grads_w_out0_wait")
    upd("ev_w_out", ev_w_out, go0, m_ev_w_out, v_ev_w_out, slots=True, rows=256)
    gi0 = summands(h_gi0, out["ev_w_out"][1], "grads_w_in0_wait")
    upd("ev_w_in", ev_w_in, gi0, m_ev_w_in, v_ev_w_in, slots=True, rows=256)

    pack_own, pack_land = exchange_wait(h_pack, out["ev_w_in"][1], scatter=False, name="small_grads_wait")
    pack_g = lax.dynamic_update_slice_in_dim(pack_land, pack_own[None], me, axis=0)
    gsum = slot_sum(pack_g, name="sum_small_grads")
    loss = gsum[16, 0]
    dw_rows_n = 32 * conv_ch // d
    g_dw_w = gsum[17:17 + dw_rows_n].reshape(32, conv_ch)[:CONV_WIDTH]
    g_od_ws = gsum[17 + dw_rows_n:17 + dw_rows_n + SGU_GROUPS * CHUNK * CHUNK // d].reshape(od_ws.shape)

    dmodc_sum = jnp.concatenate([gsum[6], gsum[7], zeros_d])
    col0 = me * ada_cols
    dm_cols = []
    for l in range(2):
        rows = pack_g[:, 3 * l:3 * l + 3, :].reshape(N_DEV, 3 * d)
        extra = dmodc_sum[None] if l == 0 else jnp.zeros((1, 3 * d), F32)
        full = _pad_rows(jnp.concatenate([rows, extra], axis=0), 16)
        dm_cols.append(lax.dynamic_slice_in_dim(full, col0, ada_cols, axis=1))
    g_ada_w = jnp.stack([matmul(sc16, dm_cols[l], mode="tn", tm=512, tn=ada_cols, tk=16, out_dtype=F32,
                                name=f"d_ada_w{l}") for l in range(2)])
    dsc = matmul(dm_cols[0], ada_bf[0], mode="nt", tm=16, tn=512, tk=ada_cols, out_dtype=F32, name="d_scc")
    (_, dscc_sum) = all_gather([dsc[8:16]], name="gather_dscc", hbm=False, sum_out=True)
    sg = jax.nn.sigmoid(c_ctx)
    g_c_ctx = dscc_sum[0] * (sg * (1.0 + c_ctx * (1.0 - sg)))
    g_ada_b = jnp.stack([gsum[0:3].reshape(-1) + dmodc_sum, gsum[3:6].reshape(-1)])

    upd("ada_w", ada_w, g_ada_w, m_ada_w, v_ada_w)

    def my_shard(full, size):
        return lax.dynamic_slice_in_dim(full, me * size, size, axis=full.ndim - 1)

    small_items = [
        ("c_ctx", c_ctx, g_c_ctx, m_c_ctx, v_c_ctx),
        ("ada_b", ada_b, g_ada_b, m_ada_b, v_ada_b),
        ("norm_g", norm_g, gsum[8:10], m_norm_g, v_norm_g),
        ("ev_q_norm", ev_q_norm, gsum[14, conv_ch:conv_ch + HEAD_DIM], m_ev_q_norm, v_ev_q_norm),
        ("ev_k_norm", ev_k_norm, gsum[14, conv_ch + HEAD_DIM:conv_ch + 2 * HEAD_DIM], m_ev_k_norm, v_ev_k_norm),
        ("ev_dw_w", ev_dw_w, my_shard(g_dw_w, conv_ch // N_DEV), m_ev_dw_w, v_ev_dw_w),
        ("ev_dw_b", ev_dw_b, gsum[13, :conv_ch], m_ev_dw_b, v_ev_dw_b),
        ("ev_ln_g", ev_ln_g, gsum[13, conv_ch:2 * conv_ch], m_ev_ln_g, v_ev_ln_g),
        ("ev_ln_b", ev_ln_b, gsum[14, :conv_ch], m_ev_ln_b, v_ev_ln_b),
        ("od_ln_g", od_ln_g, my_shard(gsum[11], shard), m_od_ln_g, v_od_ln_g),
        ("od_ln_b", od_ln_b, my_shard(gsum[12], shard), m_od_ln_b, v_od_ln_b),
        ("od_ws", od_ws, g_od_ws, m_od_ws, v_od_ws),
        ("od_bs", od_bs, gsum[15, :SGU_GROUPS * CHUNK], m_od_bs, v_od_bs),
        ("final_g", final_g, gsum[10], m_final_g, v_final_g),
    ]
    sizes = [it[1].size for it in small_items]
    total = sum(sizes)
    lanes = 1024
    prow = -(-total // lanes)
    prow = -(-prow // 8) * 8

    def pack_small(idx):
        flat = jnp.concatenate([it[idx].reshape(-1).astype(F32) for it in small_items])
        return jnp.pad(flat, (0, prow * lanes - total)).reshape(prow, lanes)

    sres = adamw(pack_small(1), pack_small(2), pack_small(3), pack_small(4), name="adamw_small", rows=prow)
    off = 0
    for it, size in zip(small_items, sizes):
        out[it[0]] = tuple(r.reshape(-1)[off:off + size].reshape(it[1].shape) for r in sres)
        off += size

    names = ['c_ctx', 'ada_w', 'ada_b', 'norm_g', 'ev_w_in', 'ev_q_norm', 'ev_k_norm', 'ev_dw_w', 'ev_dw_b',
             'ev_ln_g', 'ev_ln_b', 'ev_w_out', 'od_w_in', 'od_ln_g', 'od_ln_b', 'od_ws', 'od_bs', 'od_w_out',
             'final_g']
    return (loss, grad_x[None], *[out[k][0] for k in names], *[out[k][1] for k in names],
            *[out[k][2] for k in names], *[out[k][3] for k in names])
```

```python
import functools
import math

import jax
import jax.numpy as jnp
from jax import lax
from jax.experimental import pallas as pl
from jax.experimental.pallas import tpu as pltpu

F32 = jnp.float32
BF16 = jnp.bfloat16
MESH = pl.DeviceIdType.MESH

EPS = 1e-6
HEAD_DIM = 128
N_Q_HEADS = 8
N_KV_HEADS = 2
Q_PER_KV = N_Q_HEADS // N_KV_HEADS
ATTN_W = N_Q_HEADS * HEAD_DIM
KV_W = N_KV_HEADS * HEAD_DIM
ATTN_SCALE = HEAD_DIM ** -0.5
LN2 = math.log(2.0)
Q_SCALE = ATTN_SCALE / LN2
ROPE_THETA = 10000.0
GRID_W = 64
CONV_WIDTH = 31
CONV_HALF = CONV_WIDTH // 2
HALO = 16
CHUNK = 128
SGU_GROUPS = 8
N_DEV = 8

ADAM_LR = 0.001
ADAM_B1 = 0.9
ADAM_B2 = 0.999
ADAM_EPS = 1e-08
ADAM_WD = 0.01
ADAM_STEP = 10

VMEM_LIMIT = 56 * 1024 * 1024
ANY = pl.BlockSpec(memory_space=pl.ANY)
VMEM_SPEC = pl.BlockSpec(memory_space=pltpu.VMEM)


def _params(*sem):
    return pltpu.CompilerParams(dimension_semantics=sem, vmem_limit_bytes=VMEM_LIMIT)


def _sigmoid(x):
    return 1.0 / (1.0 + jnp.exp(-x))


def _silu(x):
    return x * _sigmoid(x)


def _dsilu(x):
    s = _sigmoid(x)
    return s * (1.0 + x * (1.0 - s))


_GELU_C = math.sqrt(2.0 / math.pi)


def _gelu(x):
    t = jnp.tanh(_GELU_C * (x + 0.044715 * (x * x * x)))
    return 0.5 * x * (1.0 + t)


def _dgelu(x):
    t = jnp.tanh(_GELU_C * (x + 0.044715 * (x * x * x)))
    return 0.5 * (1.0 + t) + 0.5 * x * (1.0 - t * t) * (_GELU_C * (1.0 + 3.0 * 0.044715 * (x * x)))


def _row(v):
    return v.reshape(1, -1).astype(F32)


def _flat_id(p):
    return 4 * p[0] + 2 * p[1] + p[2]


def _gather_body(n_arr, sum_out):
    def body(*refs):
        x_refs = refs[:n_arr]
        out_refs = refs[n_arr:2 * n_arr]
        pos = 2 * n_arr
        sum_refs = refs[pos:pos + n_arr] if sum_out else ()
        pos += n_arr if sum_out else 0
        send_sems, recv_sems, local_sems = refs[pos:pos + 3]
        x, y, c = lax.axis_index("x"), lax.axis_index("y"), lax.axis_index("c")
        me, sibling = (x, y, c), (x, y, 1 - c)
        chips = [(1 - x, y), (x, 1 - y), (1 - x, 1 - y)]

        def copy(a, k, block, to, src=None):
            rows = out_refs[a].at[_flat_id(block)]
            return pltpu.make_async_remote_copy(
                src_ref=rows if src is None else src, dst_ref=rows,
                send_sem=send_sems.at[a, k], recv_sem=recv_sems.at[a, k],
                device_id=to, device_id_type=MESH)

        sends = []
        mine = []
        for a in range(n_arr):
            cp = pltpu.make_async_copy(x_refs[a], out_refs[a].at[_flat_id(me)], local_sems.at[a])
            cp.start()
            mine.append(cp)
            first = [copy(a, 0, me, sibling, src=x_refs[a])]
            first += [copy(a, 1 + j, me, (*chip, c), src=x_refs[a]) for j, chip in enumerate(chips)]
            for cp in first:
                cp.start()
            sends += first
        for a in range(n_arr):
            for j, chip in enumerate(chips):
                copy(a, 1 + j, (*chip, c), me).wait_recv()
                fwd = copy(a, 4 + j, (*chip, c), sibling)
                fwd.start()
                sends.append(fwd)
        for a in range(n_arr):
            copy(a, 0, sibling, me).wait_recv()
            for j, chip in enumerate(chips):
                copy(a, 4 + j, (*chip, 1 - c), me).wait_recv()
        for cp in sends:
            cp.wait_send()
        for cp in mine:
            cp.wait()
        if sum_out:
            for a in range(n_arr):
                acc = out_refs[a][0]
                for k in range(1, N_DEV):
                    acc = acc + out_refs[a][k]
                sum_refs[a][...] = acc

    return body


def all_gather(arrs, *, name, hbm, sum_out=False):
    n = len(arrs)
    spec = ANY if hbm else VMEM_SPEC
    out_shape = [jax.ShapeDtypeStruct((N_DEV,) + a.shape, a.dtype) for a in arrs]
    out_specs = [spec] * n
    if sum_out:
        out_shape += [jax.ShapeDtypeStruct(a.shape, a.dtype) for a in arrs]
        out_specs += [VMEM_SPEC] * n
    res = pl.pallas_call(
        _gather_body(n, sum_out), name=name,
        out_shape=out_shape, in_specs=[spec] * n, out_specs=out_specs,
        scratch_shapes=[pltpu.SemaphoreType.DMA((n, 7)), pltpu.SemaphoreType.DMA((n, 7)),
                        pltpu.SemaphoreType.DMA((n,))],
        compiler_params=pltpu.CompilerParams(vmem_limit_bytes=VMEM_LIMIT),
    )(*arrs)
    return res


_RELATIONS = [(rx, ry, rc) for rx in (0, 1) for ry in (0, 1) for rc in (0, 1)][1:]


HBM_SPEC = pl.BlockSpec(memory_space=pltpu.HBM)
SEM_SPEC = pl.BlockSpec(memory_space=pltpu.SEMAPHORE)
_DATAFLOW = pltpu.SideEffectType.DATAFLOW_SIDE_EFFECTING
N_PEERS = N_DEV - 1


def _peer_copy(src_ref, land_ref, send_sem, recv_sem, k, rel, scatter, sending):
    x, y, c = lax.axis_index("x"), lax.axis_index("y"), lax.axis_index("c")
    rx, ry, rc = rel
    peer = (1 - x if rx else x, 1 - y if ry else y, 1 - c if rc else c)
    src = src_ref.at[_flat_id(peer)] if scatter else src_ref
    dst = land_ref.at[_flat_id((x, y, c)) if sending else _flat_id(peer)]
    return pltpu.make_async_remote_copy(src_ref=src, dst_ref=dst, send_sem=send_sem.at[k], recv_sem=recv_sem.at[k],
                                        device_id=peer, device_id_type=MESH)


def exchange_start(arrs, *, scatter, name):
    n = len(arrs)
    lands = [lax.empty((N_DEV,) + (a.shape[1:] if scatter else a.shape), a.dtype) for a in arrs]

    def body(*refs):
        srcs, lnds, sems = refs[:n], refs[n:2 * n], refs[2 * n:4 * n]
        token = refs[6 * n]
        for a in range(n):
            for k, rel in enumerate(_RELATIONS):
                _peer_copy(srcs[a], lnds[a], sems[2 * a], sems[2 * a + 1], k, rel, scatter, True).start()
        token[...] = jnp.zeros_like(token)

    outs = pl.pallas_call(
        body, name=name,
        out_shape=[pltpu.SemaphoreType.DMA((N_PEERS,))] * (2 * n)
        + [pltpu.HBM(a.shape, a.dtype) for a in arrs] + [pltpu.HBM(l.shape, l.dtype) for l in lands]
        + [jax.ShapeDtypeStruct((8, 128), F32)],
        in_specs=[HBM_SPEC] * (2 * n),
        out_specs=[SEM_SPEC] * (2 * n) + [HBM_SPEC] * (2 * n) + [VMEM_SPEC],
        input_output_aliases={i: 2 * n + i for i in range(2 * n)},
        compiler_params=pltpu.CompilerParams(has_side_effects=_DATAFLOW),
    )(*[pltpu.with_memory_space_constraint(a, pltpu.HBM) for a in arrs],
      *[pltpu.with_memory_space_constraint(l, pltpu.HBM) for l in lands])
    handles = [(outs[2 * a], outs[2 * a + 1], outs[2 * n + a], outs[3 * n + a]) for a in range(n)]
    return handles, outs[4 * n]


def exchange_wait(handle, after, *, scatter, name):
    send_sem, recv_sem, src, land = handle

    def body(src_ref, land_ref, send_ref, recv_ref, after_ref, src_out, land_out):
        for k, rel in enumerate(_RELATIONS):
            cp = _peer_copy(src_ref, land_ref, send_ref, recv_ref, k, rel, scatter, False)
            cp.wait_send()
            cp.wait_recv()

    outs = pl.pallas_call(
        body, name=name,
        out_shape=[pltpu.HBM(src.shape, src.dtype), pltpu.HBM(land.shape, land.dtype)],
        in_specs=[HBM_SPEC, HBM_SPEC, SEM_SPEC, SEM_SPEC, ANY],
        out_specs=[HBM_SPEC, HBM_SPEC], input_output_aliases={0: 0, 1: 1},
        compiler_params=pltpu.CompilerParams(has_side_effects=_DATAFLOW),
    )(src, land, send_sem, recv_sem, after)
    return outs[0], outs[1]


_STAGE1 = [(0, 0, 1), (1, 0, 0), (0, 1, 0), (1, 1, 0)]
_OTHER_CHIPS = [(1, 0), (0, 1), (1, 1)]


def _stage_copy(stage, k, shard_ref, land_ref, send_sem, recv_sem, sending):
    x, y, c = lax.axis_index("x"), lax.axis_index("y"), lax.axis_index("c")
    if stage == 1:
        rx, ry, rc = _STAGE1[k]
        peer = (1 - x if rx else x, 1 - y if ry else y, 1 - c if rc else c)
        src = shard_ref
        dst = land_ref.at[_flat_id((x, y, c)) if sending else _flat_id(peer)]
    else:
        cx, cy = _OTHER_CHIPS[k]
        peer = (x, y, 1 - c)
        chip = (1 - x if cx else x, 1 - y if cy else y)
        slot = _flat_id((*chip, c)) if sending else _flat_id((*chip, 1 - c))
        src = land_ref.at[_flat_id((*chip, c))]
        dst = land_ref.at[slot]
    return pltpu.make_async_remote_copy(src_ref=src, dst_ref=dst, send_sem=send_sem.at[k], recv_sem=recv_sem.at[k],
                                        device_id=peer, device_id_type=MESH)


def staged_gather_start(stage, shard, land, *, name):
    n_copies = len(_STAGE1) if stage == 1 else len(_OTHER_CHIPS)
    if land is None:
        land = lax.empty((N_DEV,) + shard.shape, shard.dtype)

    def body(shard_ref, land_ref, send_sem, recv_sem, shard_thru, land_thru, token):
        for k in range(n_copies):
            _stage_copy(stage, k, shard_ref, land_ref, send_sem, recv_sem, True).start()
        token[...] = jnp.zeros_like(token)

    outs = pl.pallas_call(
        body, name=name,
        out_shape=[pltpu.SemaphoreType.DMA((n_copies,)), pltpu.SemaphoreType.DMA((n_copies,)),
                   pltpu.HBM(shard.shape, shard.dtype), pltpu.HBM(land.shape, land.dtype),
                   jax.ShapeDtypeStruct((8, 128), F32)],
        in_specs=[HBM_SPEC, HBM_SPEC],
        out_specs=[SEM_SPEC, SEM_SPEC, HBM_SPEC, HBM_SPEC, VMEM_SPEC],
        input_output_aliases={0: 2, 1: 3},
        compiler_params=pltpu.CompilerParams(has_side_effects=_DATAFLOW),
    )(pltpu.with_memory_space_constraint(shard, pltpu.HBM), pltpu.with_memory_space_constraint(land, pltpu.HBM))
    return outs[:4], outs[4]


def staged_gather_wait(stage, handle, after, *, name):
    send_sem, recv_sem, shard, land = handle
    n_copies = len(_STAGE1) if stage == 1 else len(_OTHER_CHIPS)

    def body(shard_ref, land_ref, send_ref, recv_ref, after_ref, shard_out, land_out):
        for k in range(n_copies):
            cp = _stage_copy(stage, k, shard_ref, land_ref, send_ref, recv_ref, False)
            cp.wait_send()
            cp.wait_recv()

    outs = pl.pallas_call(
        body, name=name,
        out_shape=[pltpu.HBM(shard.shape, shard.dtype), pltpu.HBM(land.shape, land.dtype)],
        in_specs=[HBM_SPEC, HBM_SPEC, SEM_SPEC, SEM_SPEC, ANY],
        out_specs=[HBM_SPEC, HBM_SPEC], input_output_aliases={0: 0, 1: 1},
        compiler_params=pltpu.CompilerParams(has_side_effects=_DATAFLOW),
    )(shard, land, send_sem, recv_sem, after)
    return outs[0], outs[1]


_DIMS = {"nn": (((1,), (0,)), ((), ())), "nt": (((1,), (1,)), ((), ())), "tn": (((0,), (0,)), ((), ()))}


def matmul(a, b, *, mode, tm, tn, tk, out_dtype, name, n_out=None, res=None, gate=None, add=None, after=None,
           split_out=False, norm=None):
    if mode == "tn":
        kdim, m = a.shape
    else:
        m, kdim = a.shape
    nfull = b.shape[0] if mode == "nt" else b.shape[1]
    n = nfull if n_out is None else n_out
    tm, tn, tk = min(tm, m), min(tn, n), min(tk, kdim)
    assert m % tm == 0 and n % tn == 0 and kdim % tk == 0, (name, m, n, kdim, tm, tn, tk)
    nk = kdim // tk
    dims = _DIMS[mode]
    a_spec = pl.BlockSpec((tk, tm), lambda j, i, k: (k, i)) if mode == "tn" else pl.BlockSpec((tm, tk), lambda j, i, k: (i, k))
    b_spec = pl.BlockSpec((tn, tk), lambda j, i, k: (j, k)) if mode == "nt" else pl.BlockSpec((tk, tn), lambda j, i, k: (k, j))
    o_spec = pl.BlockSpec((tm, tn), lambda j, i, k: (i, j))
    in_specs = [a_spec, b_spec]
    operands = [a, b]
    aliases = {}
    vec = pl.BlockSpec((1, tn), lambda j, i, k: (0, j))
    if res is not None:
        in_specs += [o_spec, vec]
        operands += [res, gate]
    if norm is not None:
        assert res is not None and tn == n, (name, tn, n)
        in_specs += [vec, vec, vec]
        operands += [_row(v) for v in norm]
    if add is not None:
        in_specs += [o_spec]
        aliases = {len(operands): 0}
        operands += [add]
    if after is not None:
        in_specs += [ANY]
        operands += [after]
    out_cols = n if add is None else add.shape[1]
    out_shape = [jax.ShapeDtypeStruct((m, out_cols), out_dtype)]
    out_specs = [o_spec]
    if split_out:
        out_shape = [jax.ShapeDtypeStruct((n // tn, m, tn), out_dtype)]
        out_specs = [pl.BlockSpec((None, tm, tn), lambda j, i, k: (j, i, 0))]
    if res is not None:
        out_shape.append(jax.ShapeDtypeStruct((m, n), BF16))
        out_specs.append(o_spec)
    if norm is not None:
        out_shape.append(jax.ShapeDtypeStruct((m, n), BF16))
        out_specs.append(o_spec)

    def body(*refs):
        a_ref, b_ref = refs[:2]
        pos = 2
        if res is not None:
            res_ref, gate_ref = refs[pos:pos + 2]
            pos += 2
        if norm is not None:
            ng_ref, nshift_ref, nscale_ref = refs[pos:pos + 3]
            pos += 3
        if add is not None:
            add_ref = refs[pos]
            pos += 1
        if after is not None:
            pos += 1
        o_ref = refs[pos]
        pos += 1
        if res is not None:
            r_ref = refs[pos]
            pos += 1
        if norm is not None:
            h_ref = refs[pos]
            pos += 1
        acc_ref = refs[pos] if nk > 1 else None
        prod = lax.dot_general(a_ref[...].astype(BF16), b_ref[...].astype(BF16), dims,
                               preferred_element_type=F32)

        def finish(acc):
            if res is not None:
                xnew = res_ref[...] + gate_ref[...] * acc
                o_ref[...] = xnew.astype(out_dtype)
                r_ref[...] = acc.astype(BF16)
                if norm is not None:
                    rstd = lax.rsqrt(jnp.mean(xnew * xnew, axis=-1, keepdims=True) + EPS)
                    y = xnew * rstd * ng_ref[...]
                    h_ref[...] = (y * (1.0 + nscale_ref[...]) + nshift_ref[...]).astype(BF16)
            elif add is not None:
                o_ref[...] = (add_ref[...] + acc).astype(out_dtype)
            else:
                o_ref[...] = acc.astype(out_dtype)

        if nk == 1:
            finish(prod)
        else:
            k = pl.program_id(2)

            @pl.when(k == 0)
            def _():
                acc_ref[...] = prod

            @pl.when(k > 0)
            def _():
                acc_ref[...] += prod

            @pl.when(k == nk - 1)
            def _():
                finish(acc_ref[...])

    outs = pl.pallas_call(
        body, name=name, grid=(n // tn, m // tm, nk),
        in_specs=in_specs, out_specs=out_specs, out_shape=out_shape,
        scratch_shapes=[pltpu.VMEM((tm, tn), F32)] if nk > 1 else [],
        input_output_aliases=aliases,
        compiler_params=_params("parallel", "parallel", "arbitrary"),
    )(*operands)
    return outs if res is not None else outs[0]


def _rows_tile(n, want):
    return min(n, want)


def norm_mod_fwd(x, g, shift, scale, *, name):
    n, d = x.shape
    tt = _rows_tile(n, 256)

    def body(x_ref, g_ref, sh_ref, sc_ref, h_ref):
        xv = x_ref[...]
        rstd = lax.rsqrt(jnp.mean(xv * xv, axis=-1, keepdims=True) + EPS)
        y = xv * rstd * g_ref[...]
        h_ref[...] = (y * (1.0 + sc_ref[...]) + sh_ref[...]).astype(BF16)

    vec = pl.BlockSpec((1, d), lambda i: (0, 0))
    return pl.pallas_call(
        body, name=name, grid=(n // tt,),
        in_specs=[pl.BlockSpec((tt, d), lambda i: (i, 0)), vec, vec, vec],
        out_specs=pl.BlockSpec((tt, d), lambda i: (i, 0)),
        out_shape=jax.ShapeDtypeStruct((n, d), BF16),
        compiler_params=_params("parallel"),
    )(x, _row(g), _row(shift), _row(scale))


GATE_ROW = 3


def norm_mod_bwd(x, dh, dres, g, scale, *, name, branch=None):
    n, d = x.shape
    tt = _rows_tile(n, 256)
    has_res = dres is not None
    has_branch = branch is not None
    last = n // tt - 1

    def body(*refs):
        x_ref, dh_ref = refs[:2]
        pos = 2
        if has_res:
            dres_ref = refs[pos]
            pos += 1
        if has_branch:
            r_ref, gate_ref = refs[pos:pos + 2]
            pos += 2
        g_ref, sc_ref, dx_ref = refs[pos:pos + 3]
        pos += 3
        if has_branch:
            dr_ref = refs[pos]
            pos += 1
        acc_ref, s_ref = refs[pos:pos + 2]
        i = pl.program_id(0)
        xv = x_ref[...]
        dhv = dh_ref[...]
        rstd = lax.rsqrt(jnp.mean(xv * xv, axis=-1, keepdims=True) + EPS)
        xhat = xv * rstd
        dxhat = dhv * (g_ref[...] * (1.0 + sc_ref[...]))
        dx = rstd * (dxhat - xhat * jnp.mean(dxhat * xhat, axis=-1, keepdims=True))
        if has_res:
            dx = dx + dres_ref[...]
        dx_ref[...] = dx

        @pl.when(i == 0)
        def _():
            s_ref[...] = jnp.zeros_like(s_ref)

        s_ref[0:1, :] += jnp.sum(dhv, axis=0, keepdims=True)
        s_ref[1:2, :] += jnp.sum(dhv * xhat, axis=0, keepdims=True)
        if has_branch:
            dr_ref[...] = (dx * gate_ref[...]).astype(BF16)
            s_ref[2:3, :] += jnp.sum(dx * r_ref[...].astype(F32), axis=0, keepdims=True)

        @pl.when(i == last)
        def _():
            s1 = s_ref[0:1, :]
            s2 = s_ref[1:2, :]
            acc_ref[...] = jnp.zeros_like(acc_ref)
            acc_ref[0:1, :] = s1
            acc_ref[1:2, :] = s2 * g_ref[...]
            acc_ref[2:3, :] = s2 * (1.0 + sc_ref[...])
            acc_ref[GATE_ROW:GATE_ROW + 1, :] = s_ref[2:3, :]

    vec = pl.BlockSpec((1, d), lambda i: (0, 0))
    big = pl.BlockSpec((tt, d), lambda i: (i, 0))
    ops = [x, dh] + ([dres] if has_res else []) + ([branch[0], _row(branch[1])] if has_branch else [])
    ops += [_row(g), _row(scale)]
    return pl.pallas_call(
        body, name=name, grid=(n // tt,),
        in_specs=[big, big] + ([big] if has_res else []) + ([big, vec] if has_branch else []) + [vec, vec],
        out_specs=[big] + ([big] if has_branch else []) + [pl.BlockSpec((8, d), lambda i: (0, 0))],
        out_shape=[jax.ShapeDtypeStruct((n, d), F32)] + ([jax.ShapeDtypeStruct((n, d), BF16)] if has_branch else [])
        + [jax.ShapeDtypeStruct((8, d), F32)],
        scratch_shapes=[pltpu.VMEM((8, d), F32)],
        compiler_params=_params("arbitrary"),
    )(*ops)


def final_loss(x, target, g, r, gate, *, name):
    n, d = x.shape
    tt = _rows_tile(n, 256)

    def body(x_ref, t_ref, g_ref, r_ref, gate_ref, dx_ref, dr_ref, acc_ref, loss_ref):
        i = pl.program_id(0)
        xv = x_ref[...]
        rstd = lax.rsqrt(jnp.mean(xv * xv, axis=-1, keepdims=True) + EPS)
        xhat = xv * rstd
        e = xhat * g_ref[...] - t_ref[...]
        dy = e * (1.0 / d)
        dxhat = dy * g_ref[...]
        dx = rstd * (dxhat - xhat * jnp.mean(dxhat * xhat, axis=-1, keepdims=True))
        dx_ref[...] = dx
        dr_ref[...] = (dx * gate_ref[...]).astype(BF16)

        @pl.when(i == 0)
        def _():
            acc_ref[...] = jnp.zeros_like(acc_ref)
            loss_ref[...] = jnp.zeros_like(loss_ref)

        acc_ref[0:1, :] += jnp.sum(dy * xhat, axis=0, keepdims=True)
        acc_ref[GATE_ROW:GATE_ROW + 1, :] += jnp.sum(dx * r_ref[...].astype(F32), axis=0, keepdims=True)
        part = 0.5 * jnp.sum(jnp.mean(e * e, axis=-1, keepdims=True), axis=0, keepdims=True)
        loss_ref[...] += jnp.broadcast_to(part, loss_ref.shape)

    big = pl.BlockSpec((tt, d), lambda i: (i, 0))
    vec = pl.BlockSpec((1, d), lambda i: (0, 0))
    return pl.pallas_call(
        body, name=name, grid=(n // tt,),
        in_specs=[big, big, vec, big, vec],
        out_specs=[big, big, pl.BlockSpec((8, d), lambda i: (0, 0)), pl.BlockSpec((8, 128), lambda i: (0, 0))],
        out_shape=[jax.ShapeDtypeStruct((n, d), F32), jax.ShapeDtypeStruct((n, d), BF16),
                   jax.ShapeDtypeStruct((8, d), F32), jax.ShapeDtypeStruct((8, 128), F32)],
        compiler_params=_params("arbitrary"),
    )(x, target, _row(g), r, _row(gate))


def _as_row(col):
    t = col.shape[0]
    return jnp.transpose(jnp.broadcast_to(col, (t, HEAD_DIM)))[0:1, :]


def _swap_pairs(x):
    lane = lax.broadcasted_iota(jnp.int32, x.shape, 1)
    return jnp.where(lane % 2 == 0, pltpu.roll(x, HEAD_DIM - 1, 1), pltpu.roll(x, 1, 1))


def rope_tables(n):
    rows = n // GRID_W
    row = jnp.repeat(jnp.arange(rows, dtype=F32), GRID_W)
    col = jnp.tile(jnp.arange(GRID_W, dtype=F32), rows)
    n_freq = HEAD_DIM // 4
    inv = jnp.power(ROPE_THETA, jnp.arange(n_freq, dtype=F32) * (-2.0 / (HEAD_DIM // 2)))
    ang = jnp.concatenate([row[:, None] * inv, col[:, None] * inv], axis=-1)
    cos, sin = jnp.cos(ang), jnp.sin(ang)
    cexp = jnp.repeat(cos, 2, axis=-1)
    sexp = jnp.stack([-sin, sin], axis=-1).reshape(n, HEAD_DIM)
    return cexp, sexp


V_EXT_W = 2 * HEAD_DIM


def qkv_prep_fwd(p, wq, wk, cexp, sexp, *, latent, name, rows_all=None, k_all=None, v_ext=None):
    n = p.shape[0]
    tt = _rows_tile(n, 256)
    width = 2 * KV_W + (ATTN_W if latent else 0)

    def body(*refs):
        if latent:
            p_ref, wq_ref, wk_ref, c_ref, s_ref, q_ref, k_ref, v_ref = refs
        else:
            p_ref, wk_ref, _, _, k_ref, v_ref = refs

        def head(xv, w):
            rstd = lax.rsqrt(jnp.mean(xv * xv, axis=-1, keepdims=True) + EPS)
            yv = xv * rstd * w
            if latent:
                yv = yv * c_ref[...] + _swap_pairs(yv) * s_ref[...]
            return yv

        for h in range(N_KV_HEADS):
            sl = slice(h * HEAD_DIM, (h + 1) * HEAD_DIM)
            k_ref[:, sl] = head(p_ref[:, sl], wk_ref[...]).astype(BF16)
            v_ref[:, h * V_EXT_W:h * V_EXT_W + HEAD_DIM] = p_ref[:, KV_W + h * HEAD_DIM:KV_W + (h + 1) * HEAD_DIM].astype(BF16)
            lane = lax.broadcasted_iota(jnp.int32, (tt, HEAD_DIM), 1)
            v_ref[:, h * V_EXT_W + HEAD_DIM:(h + 1) * V_EXT_W] = jnp.where(lane == 0, 1.0, 0.0).astype(BF16)
        if latent:
            for h in range(N_Q_HEADS):
                sl = slice(2 * KV_W + h * HEAD_DIM, 2 * KV_W + (h + 1) * HEAD_DIM)
                q_ref[:, h * HEAD_DIM:(h + 1) * HEAD_DIM] = (head(p_ref[:, sl], wq_ref[...]) * Q_SCALE).astype(BF16)

    vec = pl.BlockSpec((1, HEAD_DIM), lambda i: (0, 0))
    tab = pl.BlockSpec((tt, HEAD_DIM), lambda i: (i, 0))
    vw = N_KV_HEADS * V_EXT_W
    k_shape = jax.ShapeDtypeStruct((rows_all, KV_W), BF16)
    v_shape = jax.ShapeDtypeStruct((rows_all, vw), BF16)
    aliases = {}
    if latent:
        in_specs = [pl.BlockSpec((tt, width), lambda i: (i, 0)), vec, vec, tab, tab]
        ops = [p, _row(wq), _row(wk), cexp, sexp]
        out_specs = [pl.BlockSpec((tt, ATTN_W), lambda i: (i, 0)), pl.BlockSpec((tt, KV_W), lambda i: (i, 0)),
                     pl.BlockSpec((tt, vw), lambda i: (i, 0))]
        out_shape = [jax.ShapeDtypeStruct((n, ATTN_W), BF16), k_shape, v_shape]
    else:
        assert n == tt and (rows_all - n) % n == 0, (n, tt, rows_all)
        first = (rows_all - n) // n
        in_specs = [pl.BlockSpec((tt, width), lambda i: (i, 0)), vec, ANY, ANY]
        ops = [p, _row(wk), k_all, v_ext]
        out_specs = [pl.BlockSpec((tt, KV_W), lambda i: (first, 0)), pl.BlockSpec((tt, vw), lambda i: (first, 0))]
        out_shape = [k_shape, v_shape]
        aliases = {2: 0, 3: 1}
    return pl.pallas_call(
        body, name=name, grid=(n // tt,), in_specs=in_specs, out_specs=out_specs, out_shape=out_shape,
        input_output_aliases=aliases, compiler_params=_params("parallel"),
    )(*ops)


def qkv_prep_bwd(p, dq, dk, dv, wq, wk, cexp, sexp, *, latent, name):
    n = p.shape[0]
    tt = _rows_tile(n, 256)
    width = 2 * KV_W + (ATTN_W if latent else 0)

    def body(*refs):
        if latent:
            p_ref, dq_ref, dk_ref, dv_ref, wq_ref, wk_ref, c_ref, s_ref, dp_ref, acc_ref = refs
        else:
            p_ref, dk_ref, dv_ref, wk_ref, dp_ref, acc_ref = refs
        i = pl.program_id(0)

        @pl.when(i == 0)
        def _():
            acc_ref[...] = jnp.zeros_like(acc_ref)

        def head(xv, dy, w, row):
            if latent:
                dy = dy * c_ref[...] + _swap_pairs(dy * s_ref[...])
            rstd = lax.rsqrt(jnp.mean(xv * xv, axis=-1, keepdims=True) + EPS)
            xhat = xv * rstd
            acc_ref[row:row + 1, :] += jnp.sum(dy * xhat, axis=0, keepdims=True)
            dxhat = dy * w
            return rstd * (dxhat - xhat * jnp.mean(dxhat * xhat, axis=-1, keepdims=True))

        for h in range(N_KV_HEADS):
            sl = slice(h * HEAD_DIM, (h + 1) * HEAD_DIM)
            dp_ref[:, sl] = head(p_ref[:, sl], dk_ref[:, sl], wk_ref[...], 1).astype(BF16)
        dp_ref[:, KV_W:2 * KV_W] = dv_ref[...].astype(BF16)
        if latent:
            for h in range(N_Q_HEADS):
                sl = slice(2 * KV_W + h * HEAD_DIM, 2 * KV_W + (h + 1) * HEAD_DIM)
                dyq = dq_ref[:, h * HEAD_DIM:(h + 1) * HEAD_DIM] * Q_SCALE
                dp_ref[:, sl] = head(p_ref[:, sl], dyq, wq_ref[...], 0).astype(BF16)

    vec = pl.BlockSpec((1, HEAD_DIM), lambda i: (0, 0))
    tab = pl.BlockSpec((tt, HEAD_DIM), lambda i: (i, 0))
    kv_spec = pl.BlockSpec((tt, KV_W), lambda i: (i, 0))
    p_spec = pl.BlockSpec((tt, width), lambda i: (i, 0))
    if latent:
        in_specs = [p_spec, pl.BlockSpec((tt, ATTN_W), lambda i: (i, 0)), kv_spec, kv_spec, vec, vec, tab, tab]
        ops = [p, dq, dk, dv, _row(wq), _row(wk), cexp, sexp]
    else:
        in_specs = [p_spec, kv_spec, kv_spec, vec]
        ops = [p, dk, dv, _row(wk)]
    return pl.pallas_call(
        body, name=name, grid=(n // tt,), in_specs=in_specs,
        out_specs=[p_spec, pl.BlockSpec((8, HEAD_DIM), lambda i: (0, 0))],
        out_shape=[jax.ShapeDtypeStruct((n, width), BF16), jax.ShapeDtypeStruct((8, HEAD_DIM), F32)],
        compiler_params=_params("arbitrary"),
    )(*ops)


def _kv_chunks(n, s_all):
    step = 1024 if n % 1024 == 0 else 256
    chunks = [(s, step) for s in range(0, n, step)]
    if s_all > n:
        chunks.append((n, s_all - n))
    return chunks


def flash_fwd(q, k_all, v_ext, p, *, za_block, mix_width, name):
    n = q.shape[0]
    s_all = k_all.shape[0]
    tq = _rows_tile(n, 512)
    chunks = _kv_chunks(n, s_all)
    wide = 2 * HEAD_DIM

    def body(q_ref, k_ref, v_ref, za_ref, o_ref, mix_ref, lse_ref, m_ref, acc_ref):
        qv = q_ref[...]
        m_ref[...] = jnp.full_like(m_ref, -jnp.inf)
        acc_ref[...] = jnp.zeros_like(acc_ref)
        for start, size in chunks:
            kc = k_ref[pl.ds(start, size), :]
            vc = v_ref[pl.ds(start, size), :]
            s = lax.dot_general(qv, kc, _DIMS["nt"], preferred_element_type=F32)
            m_old = m_ref[...]
            m_new = jnp.maximum(m_old, jnp.max(s, axis=-1, keepdims=True))
            pr = jnp.exp2(s - m_new)
            alpha = jnp.exp2(m_old - m_new)
            acc_ref[...] = alpha * acc_ref[...] + jnp.dot(pr.astype(BF16), vc, preferred_element_type=F32)
            m_ref[...] = m_new
        acc = acc_ref[...]
        denom = acc[:, HEAD_DIM:HEAD_DIM + 1]
        o = acc[:, :HEAD_DIM] / denom
        o_ref[...] = o.astype(BF16)
        mix_ref[...] = (o * _silu(za_ref[...])).astype(BF16)
        lse_ref[0] = _as_row(m_ref[...] + jnp.log2(denom))

    qspec = pl.BlockSpec((tq, HEAD_DIM), lambda h, i: (i, h))
    return pl.pallas_call(
        body, name=name, grid=(N_Q_HEADS, n // tq),
        in_specs=[qspec, pl.BlockSpec((s_all, HEAD_DIM), lambda h, i: (0, h // Q_PER_KV)),
                  pl.BlockSpec((s_all, wide), lambda h, i: (0, h // Q_PER_KV)),
                  pl.BlockSpec((tq, HEAD_DIM), lambda h, i: (i, za_block + h))],
        out_specs=[qspec, qspec, pl.BlockSpec((1, 1, tq), lambda h, i: (h, 0, i))],
        out_shape=[jax.ShapeDtypeStruct((n, ATTN_W), BF16), jax.ShapeDtypeStruct((n, mix_width), BF16),
                   jax.ShapeDtypeStruct((N_Q_HEADS, 1, n), F32)],
        scratch_shapes=[pltpu.VMEM((tq, 1), F32), pltpu.VMEM((tq, wide), F32)],
        compiler_params=_params("parallel", "parallel"),
    )(q, k_all, v_ext, p)


def attn_gate_bwd(dmix, o, p, *, za_block, name):
    n = o.shape[0]
    tt = _rows_tile(n, 512)
    za_half = za_block * HEAD_DIM // HALF
    n_half = ATTN_W // HALF

    def body(*refs):
        dm_ref, o_ref = refs[:2]
        za_refs = refs[2:2 + n_half]
        do_ref, dos_ref, dza_ref, delta_ref = refs[2 + n_half:]
        dm = dm_ref[...]
        ov = o_ref[...].astype(F32)
        za = jnp.concatenate([r[...] for r in za_refs], axis=-1)
        do = dm * _silu(za)
        do_ref[...] = do.astype(BF16)
        dos_ref[...] = (do * LN2).astype(BF16)
        dza_ref[...] = (dm * ov * _dsilu(za)).astype(BF16)
        prod = do * ov
        for h in range(N_Q_HEADS):
            col = jnp.sum(prod[:, h * HEAD_DIM:(h + 1) * HEAD_DIM], axis=-1, keepdims=True) * LN2
            delta_ref[h] = _as_row(col)

    spec = pl.BlockSpec((tt, ATTN_W), lambda i: (i, 0))
    shape = jax.ShapeDtypeStruct((n, ATTN_W), BF16)
    za_specs = [pl.BlockSpec((tt, HALF), functools.partial(lambda i, cb: (i, cb), cb=za_half + h))
                for h in range(n_half)]
    return pl.pallas_call(
        body, name=name, grid=(n // tt,),
        in_specs=[spec, spec] + za_specs,
        out_specs=[spec, spec, spec, pl.BlockSpec((N_Q_HEADS, 1, tt), lambda i: (0, 0, i))],
        out_shape=[shape, shape, shape, jax.ShapeDtypeStruct((N_Q_HEADS, 1, n), F32)],
        compiler_params=_params("parallel"),
    )(dmix, o, *([p] * n_half))


def flash_bwd(q, do, do_s, lse_row, delta_row, k_all, v_ext, *, name):
    n = q.shape[0]
    s_all = k_all.shape[0]
    tq = _rows_tile(n, 512)
    chunks = _kv_chunks(n, s_all)

    def body(q_ref, do_ref, dos_ref, lse_ref, dl_ref, k_ref, v_ref, dq_ref, dk_ref, dv_ref):
        g = pl.program_id(1)
        i = pl.program_id(2)

        @pl.when((g == 0) & (i == 0))
        def _():
            dk_ref[...] = jnp.zeros_like(dk_ref)
            dv_ref[...] = jnp.zeros_like(dv_ref)

        qv = q_ref[...]
        dov = do_ref[...]
        dosv = dos_ref[...]
        lse = lse_ref[0]
        dl = dl_ref[0]
        dq = jnp.zeros((tq, HEAD_DIM), F32)
        for start, size in chunks:
            kc = k_ref[pl.ds(start, size), :]
            vc = v_ref[pl.ds(start, size), :]
            st = lax.dot_general(kc, qv, _DIMS["nt"], preferred_element_type=F32)
            pt = jnp.exp2(st - lse)
            dpt = lax.dot_general(vc, dosv, _DIMS["nt"], preferred_element_type=F32)
            dst = (pt * (dpt - dl)).astype(BF16)
            dv_ref[pl.ds(start, size), :] += jnp.dot(pt.astype(BF16), dov, preferred_element_type=F32)
            dk_ref[pl.ds(start, size), :] += jnp.dot(dst, qv, preferred_element_type=F32)
            dq = dq + lax.dot_general(dst, kc, _DIMS["tn"], preferred_element_type=F32)
        dq_ref[...] = dq

    qspec = pl.BlockSpec((tq, HEAD_DIM), lambda kh, g, i: (i, kh * Q_PER_KV + g))
    rowspec = pl.BlockSpec((1, 1, tq), lambda kh, g, i: (kh * Q_PER_KV + g, 0, i))
    kvspec = pl.BlockSpec((s_all, HEAD_DIM), lambda kh, g, i: (0, kh))
    return pl.pallas_call(
        body, name=name, grid=(N_KV_HEADS, Q_PER_KV, n // tq),
        in_specs=[qspec, qspec, qspec, rowspec, rowspec, kvspec,
                  pl.BlockSpec((s_all, HEAD_DIM), lambda kh, g, i: (0, kh * (V_EXT_W // HEAD_DIM)))],
        out_specs=[qspec, kvspec, kvspec],
        out_shape=[jax.ShapeDtypeStruct((n, ATTN_W), F32),
                   jax.ShapeDtypeStruct((s_all, KV_W), F32), jax.ShapeDtypeStruct((s_all, KV_W), F32)],
        compiler_params=_params("arbitrary", "arbitrary", "arbitrary"),
    )(q, do, do_s, lse_row, delta_row, k_all, v_ext)


HALF = 512


SUB = 128


def _tap_group(w_ref, w_lanes, ext_ref, r0, lanes, r, tap_of):
    z = None
    for a in range(4):
        k = tap_of(8 * a + r)
        if 0 <= k < CONV_WIDTH:
            term = w_ref[k:k + 1, w_lanes] * ext_ref[pl.ds(r0 + 8 * a, SUB + 8), lanes]
            z = term if z is None else z + term
    return z


def _shifted(z, z_ref, r):
    if r == 0:
        return z[0:SUB]
    z_ref[...] = z
    return z_ref[pl.ds(r, SUB), :]


def _fold8(x):
    return jnp.sum(x.reshape(x.shape[0] // 8, 8, x.shape[1]), axis=0)


def _pieces(tt, h):
    return [(rh * SUB, slice(c * SUB, (c + 1) * SUB), slice(h * HALF + c * SUB, h * HALF + (c + 1) * SUB))
            for c in range(HALF // SUB) for rh in range(tt // SUB)]


def _halo_specs(tt, n, cb):
    per = tt // HALO
    last = n // HALO - 1
    return [pl.BlockSpec((HALO, HALF), lambda i: (jnp.maximum(i * per - 1, 0), cb)),
            pl.BlockSpec((tt, HALF), lambda i: (i, cb)),
            pl.BlockSpec((HALO, HALF), lambda i: (jnp.minimum((i + 1) * per, last), cb))]


def conv_fwd(p, dw_w, dw_b, ln_g, ln_b, mix, *, glu_block, name):
    n = p.shape[0]
    ch = dw_w.shape[1]
    nh = ch // HALF
    tt = _rows_tile(n, 256)
    last = n // tt - 1

    def body(*refs):
        a_refs = [refs[3 * h:3 * h + 3] for h in range(nh)]
        b_refs = [refs[3 * (nh + h):3 * (nh + h) + 3] for h in range(nh)]
        pos = 6 * nh
        zb_refs = refs[pos:pos + nh]
        pos += nh
        w_ref, bias_ref, g_ref, be_ref, _, ycv_ref, mix_ref, ext_ref, z_ref = refs[pos:pos + 9]
        i = pl.program_id(0)
        for h in range(nh):
            cs = slice(h * HALF, (h + 1) * HALF)
            ap, am, an = a_refs[h]
            bp, bm, bn = b_refs[h]
            ext_ref[0:HALO, :] = jnp.where(i > 0, ap[...] * _sigmoid(bp[...]), 0.0)
            ext_ref[HALO:HALO + tt, :] = am[...] * _sigmoid(bm[...])
            ext_ref[HALO + tt:2 * HALO + tt, :] = jnp.where(i < last, an[...] * _sigmoid(bn[...]), 0.0)
            for r0, lanes, w_lanes in _pieces(tt, h):
                acc = jnp.broadcast_to(bias_ref[:, w_lanes], (SUB, SUB))
                for r in range(8):
                    z = _tap_group(w_ref, w_lanes, ext_ref, r0, lanes, r, lambda j: j - 1)
                    acc = acc + _shifted(z, z_ref, r)
                ycv_ref[pl.ds(r0, SUB), w_lanes] = acc
        yc = ycv_ref[...]
        mu = jnp.mean(yc, axis=-1, keepdims=True)
        var = jnp.mean(jnp.square(yc - mu), axis=-1, keepdims=True)
        ln = (yc - mu) * lax.rsqrt(var + EPS) * g_ref[...] + be_ref[...]
        out = _silu(ln)
        for h in range(nh):
            cs = slice(h * HALF, (h + 1) * HALF)
            mix_ref[:, cs] = (out[:, cs] * _silu(zb_refs[h][...])).astype(BF16)

    in_specs = []
    for h in range(2 * nh):
        in_specs += _halo_specs(tt, n, glu_block + h)
    in_specs += [pl.BlockSpec((tt, HALF), functools.partial(lambda i, cb: (i, cb), cb=glu_block + 2 * nh + h))
                 for h in range(nh)]
    vec = pl.BlockSpec((1, ch), lambda i: (0, 0))
    in_specs += [pl.BlockSpec((CONV_WIDTH, ch), lambda i: (0, 0)), vec, vec, vec, ANY]
    ops = [p] * (6 * nh + nh) + [dw_w, _row(dw_b), _row(ln_g), _row(ln_b), mix]
    big = pl.BlockSpec((tt, ch), lambda i: (i, 0))
    mix_block = (mix.shape[1] - ch) // ch
    return pl.pallas_call(
        body, name=name, grid=(n // tt,), in_specs=in_specs,
        out_specs=[big, pl.BlockSpec((tt, ch), lambda i: (i, mix_block))],
        out_shape=[jax.ShapeDtypeStruct((n, ch), F32), jax.ShapeDtypeStruct(mix.shape, BF16)],
        input_output_aliases={len(ops) - 1: 1},
        scratch_shapes=[pltpu.VMEM((tt + 2 * HALO, HALF), F32), pltpu.VMEM((SUB + 8, SUB), F32)],
        compiler_params=_params("parallel"),
    )(*ops)


def conv_bwd_rows(dmix, ycv, p, ln_g, ln_b, *, mix_block, zb_block, name):
    n, ch = ycv.shape
    nh = ch // HALF
    tt = _rows_tile(n, 256)

    def body(*refs):
        dm_ref, ycv_ref = refs[:2]
        zb_refs = refs[2:2 + nh]
        g_ref, be_ref, dy_ref, dzb_ref, acc_ref = refs[2 + nh:]
        i = pl.program_id(0)

        @pl.when(i == 0)
        def _():
            acc_ref[...] = jnp.zeros_like(acc_ref)

        yc = ycv_ref[...]
        mu = jnp.mean(yc, axis=-1, keepdims=True)
        var = jnp.mean(jnp.square(yc - mu), axis=-1, keepdims=True)
        rstd = lax.rsqrt(var + EPS)
        xhat = (yc - mu) * rstd
        ln = xhat * g_ref[...] + be_ref[...]
        out = _silu(ln)
        dm = dm_ref[...]
        zb = jnp.concatenate([r[...] for r in zb_refs], axis=-1)
        dzb_ref[...] = (dm * out * _dsilu(zb)).astype(BF16)
        dln = dm * _silu(zb) * _dsilu(ln)
        acc_ref[0:1, :] += jnp.sum(dln * xhat, axis=0, keepdims=True)
        acc_ref[1:2, :] += jnp.sum(dln, axis=0, keepdims=True)
        dxhat = dln * g_ref[...]
        dy_ref[...] = rstd * (dxhat - jnp.mean(dxhat, axis=-1, keepdims=True)
                              - xhat * jnp.mean(dxhat * xhat, axis=-1, keepdims=True))

    big = pl.BlockSpec((tt, ch), lambda i: (i, 0))
    vec = pl.BlockSpec((1, ch), lambda i: (0, 0))
    in_specs = [pl.BlockSpec((tt, ch), lambda i: (i, mix_block)), big]
    in_specs += [pl.BlockSpec((tt, HALF), functools.partial(lambda i, cb: (i, cb), cb=zb_block + h)) for h in range(nh)]
    in_specs += [vec, vec]
    return pl.pallas_call(
        body, name=name, grid=(n // tt,), in_specs=in_specs,
        out_specs=[big, big, pl.BlockSpec((8, ch), lambda i: (0, 0))],
        out_shape=[jax.ShapeDtypeStruct((n, ch), F32), jax.ShapeDtypeStruct((n, ch), BF16),
                   jax.ShapeDtypeStruct((8, ch), F32)],
        compiler_params=_params("arbitrary"),
    )(dmix, ycv, *([p] * nh), _row(ln_g), _row(ln_b))


def conv_bwd_taps(dycv, p, dw_w, *, glu_block, name):
    n, ch = dycv.shape
    nh = ch // HALF
    tt = _rows_tile(n, 256)
    last = n // tt - 1

    def body(*refs):
        d_refs = [refs[3 * h:3 * h + 3] for h in range(nh)]
        a_refs = [refs[3 * (nh + h):3 * (nh + h) + 3] for h in range(nh)]
        b_refs = [refs[3 * (2 * nh + h):3 * (2 * nh + h) + 3] for h in range(nh)]
        w_ref, dglu_ref, dw_ref, db_ref, yext_ref, dext_ref, z_ref, dy_ref, dwp_ref, dbp_ref = refs[9 * nh:]
        i = pl.program_id(0)

        @pl.when(i == 0)
        def _():
            dwp_ref[...] = jnp.zeros_like(dwp_ref)
            dbp_ref[...] = jnp.zeros_like(dbp_ref)

        for h in range(nh):
            cs = slice(h * HALF, (h + 1) * HALF)
            ap, am, an = a_refs[h]
            bp, bm, bn = b_refs[h]
            dp, dm, dn = d_refs[h]
            av = am[...]
            sb = _sigmoid(bm[...])
            yext_ref[0:HALO, :] = jnp.where(i > 0, ap[...] * _sigmoid(bp[...]), 0.0)
            yext_ref[HALO:HALO + tt, :] = av * sb
            yext_ref[HALO + tt:2 * HALO + tt, :] = jnp.where(i < last, an[...] * _sigmoid(bn[...]), 0.0)
            dmain = dm[...]
            dext_ref[0:HALO, :] = jnp.where(i > 0, dp[...], 0.0)
            dext_ref[HALO:HALO + tt, :] = dmain
            dext_ref[HALO + tt:2 * HALO + tt, :] = jnp.where(i < last, dn[...], 0.0)
            for r0, lanes, w_lanes in _pieces(tt, h):
                dsub = dext_ref[pl.ds(HALO + r0, SUB), lanes]
                dy = jnp.zeros((SUB, SUB), F32)
                for r in range(8):
                    z = _tap_group(w_ref, w_lanes, dext_ref, r0, lanes, r, lambda j: CONV_WIDTH - j)
                    dy = dy + _shifted(z, z_ref, r)
                    y_r = yext_ref[pl.ds(r0 + r, SUB + 24), lanes]
                    for a in range(4):
                        k = 8 * a + r - 1
                        if k >= 0:
                            dwp_ref[8 * k:8 * k + 8, w_lanes] += _fold8(dsub * y_r[8 * a:8 * a + SUB])
                dbp_ref[:, w_lanes] += _fold8(dsub)
                dy_ref[pl.ds(r0, SUB), lanes] = dy
            dy = dy_ref[...]
            dglu_ref[:, cs] = (dy * sb).astype(BF16)
            dglu_ref[:, ch + h * HALF:ch + (h + 1) * HALF] = (dy * av * sb * (1.0 - sb)).astype(BF16)

        @pl.when(i == last)
        def _():
            dw_ref[...] = jnp.sum(dwp_ref[...].reshape(32, 8, ch), axis=1)
            db_ref[...] = jnp.broadcast_to(jnp.sum(dbp_ref[...], axis=0, keepdims=True), db_ref.shape)

    in_specs = []
    for h in range(nh):
        in_specs += _halo_specs(tt, n, h)
    for h in range(2 * nh):
        in_specs += _halo_specs(tt, n, glu_block + h)
    in_specs += [pl.BlockSpec((CONV_WIDTH, ch), lambda i: (0, 0))]
    ops = [dycv] * (3 * nh) + [p] * (6 * nh) + [dw_w]
    return pl.pallas_call(
        body, name=name, grid=(n // tt,), in_specs=in_specs,
        out_specs=[pl.BlockSpec((tt, 2 * ch), lambda i: (i, 0)), pl.BlockSpec((32, ch), lambda i: (0, 0)),
                   pl.BlockSpec((8, ch), lambda i: (0, 0))],
        out_shape=[jax.ShapeDtypeStruct((n, 2 * ch), BF16), jax.ShapeDtypeStruct((32, ch), F32),
                   jax.ShapeDtypeStruct((8, ch), F32)],
        scratch_shapes=[pltpu.VMEM((tt + 2 * HALO, HALF), F32), pltpu.VMEM((tt + 2 * HALO, HALF), F32),
                        pltpu.VMEM((SUB + 8, SUB), F32), pltpu.VMEM((tt, HALF), F32),
                        pltpu.VMEM((8 * 32, ch), F32), pltpu.VMEM((8, ch), F32)],
        compiler_params=_params("arbitrary"),
    )(*ops)


def _sgu_common(p_ref, g_ref, be_ref, ws_ref, bs_ref, w):
    gw = w // SGU_GROUPS
    u_pre = p_ref[:, 0:w]
    v_pre = p_ref[:, w:2 * w]
    zc = p_ref[:, 2 * w:3 * w]
    u = _gelu(u_pre)
    v = _gelu(v_pre)
    mu = jnp.mean(v, axis=-1, keepdims=True)
    var = jnp.mean(jnp.square(v - mu), axis=-1, keepdims=True)
    rstd = lax.rsqrt(var + EPS)
    vhat = (v - mu) * rstd
    vn = (vhat * g_ref[...] + be_ref[...]).astype(BF16)
    mixed = jnp.concatenate(
        [jnp.dot(ws_ref[gi].astype(BF16), vn[:, gi * gw:(gi + 1) * gw], preferred_element_type=F32)
         + bs_ref[:, gi:gi + 1] for gi in range(SGU_GROUPS)], axis=-1)
    return u_pre, v_pre, zc, u, rstd, vhat, vn, mixed


def sgu_fwd(p, ln_g, ln_b, ws, bs_t, *, name):
    n, w3 = p.shape
    w = w3 // 3

    def body(p_ref, g_ref, be_ref, ws_ref, bs_ref, m_ref):
        _, _, zc, u, _, _, _, mixed = _sgu_common(p_ref, g_ref, be_ref, ws_ref, bs_ref, w)
        m_ref[...] = (u * mixed * _silu(zc)).astype(BF16)

    vec = pl.BlockSpec((1, w), lambda i: (0, 0))
    return pl.pallas_call(
        body, name=name, grid=(n // CHUNK,),
        in_specs=[pl.BlockSpec((CHUNK, w3), lambda i: (i, 0)), vec, vec,
                  pl.BlockSpec((SGU_GROUPS, CHUNK, CHUNK), lambda i: (0, 0, 0)),
                  pl.BlockSpec((CHUNK, SGU_GROUPS), lambda i: (0, 0))],
        out_specs=pl.BlockSpec((CHUNK, w), lambda i: (i, 0)),
        out_shape=jax.ShapeDtypeStruct((n, w), BF16),
        compiler_params=_params("parallel"),
    )(p, _row(ln_g), _row(ln_b), ws, bs_t)


def sgu_bwd(p, dm, ln_g, ln_b, ws, ws_t, bs_t, *, name):
    n, w3 = p.shape
    w = w3 // 3
    gw = w // SGU_GROUPS

    def body(p_ref, dm_ref, g_ref, be_ref, ws_ref, wst_ref, bs_ref, dp_ref, dws_ref, dbs_ref, acc_ref):
        i = pl.program_id(0)

        @pl.when(i == 0)
        def _():
            dws_ref[...] = jnp.zeros_like(dws_ref)
            dbs_ref[...] = jnp.zeros_like(dbs_ref)
            acc_ref[...] = jnp.zeros_like(acc_ref)

        u_pre, v_pre, zc, u, rstd, vhat, vn, mixed = _sgu_common(p_ref, g_ref, be_ref, ws_ref, bs_ref, w)
        dmv = dm_ref[...]
        um = u * mixed
        dp_ref[:, 2 * w:3 * w] = (dmv * um * _dsilu(zc)).astype(BF16)
        dum = dmv * _silu(zc)
        dp_ref[:, 0:w] = (dum * mixed * _dgelu(u_pre)).astype(BF16)
        dmixed = dum * u
        dmixed_b = dmixed.astype(BF16)
        dvn_parts = []
        for gi in range(SGU_GROUPS):
            cs = slice(gi * gw, (gi + 1) * gw)
            dws_ref[gi] += lax.dot_general(dmixed_b[:, cs], vn[:, cs], _DIMS["nt"], preferred_element_type=F32)
            dbs_ref[:, gi:gi + 1] += jnp.sum(dmixed[:, cs], axis=-1, keepdims=True)
            dvn_parts.append(jnp.dot(wst_ref[gi].astype(BF16), dmixed_b[:, cs], preferred_element_type=F32))
        dvn = jnp.concatenate(dvn_parts, axis=-1)
        acc_ref[0:1, :] += jnp.sum(dvn * vhat, axis=0, keepdims=True)
        acc_ref[1:2, :] += jnp.sum(dvn, axis=0, keepdims=True)
        dvhat = dvn * g_ref[...]
        dv = rstd * (dvhat - jnp.mean(dvhat, axis=-1, keepdims=True)
                     - vhat * jnp.mean(dvhat * vhat, axis=-1, keepdims=True))
        dp_ref[:, w:2 * w] = (dv * _dgelu(v_pre)).astype(BF16)

    vec = pl.BlockSpec((1, w), lambda i: (0, 0))
    wspec = pl.BlockSpec((SGU_GROUPS, CHUNK, CHUNK), lambda i: (0, 0, 0))
    bspec = pl.BlockSpec((CHUNK, SGU_GROUPS), lambda i: (0, 0))
    return pl.pallas_call(
        body, name=name, grid=(n // CHUNK,),
        in_specs=[pl.BlockSpec((CHUNK, w3), lambda i: (i, 0)), pl.BlockSpec((CHUNK, w), lambda i: (i, 0)),
                  vec, vec, wspec, wspec, bspec],
        out_specs=[pl.BlockSpec((CHUNK, w3), lambda i: (i, 0)), wspec, bspec,
                   pl.BlockSpec((8, w), lambda i: (0, 0))],
        out_shape=[jax.ShapeDtypeStruct((n, w3), BF16), jax.ShapeDtypeStruct((SGU_GROUPS, CHUNK, CHUNK), F32),
                   jax.ShapeDtypeStruct((CHUNK, SGU_GROUPS), F32), jax.ShapeDtypeStruct((8, w), F32)],
        compiler_params=_params("arbitrary"),
    )(p, dm, _row(ln_g), _row(ln_b), ws, ws_t, bs_t)


def _adam_math(w, g, m, v):
    m_new = ADAM_B1 * m + (1.0 - ADAM_B1) * g
    v_new = ADAM_B2 * v + (1.0 - ADAM_B2) * (g * g)
    m_hat = m_new / (1.0 - ADAM_B1 ** ADAM_STEP)
    v_hat = v_new / (1.0 - ADAM_B2 ** ADAM_STEP)
    delta = -ADAM_LR * (m_hat / (jnp.sqrt(v_hat) + ADAM_EPS) + ADAM_WD * w)
    return delta, m_new, v_new


def adamw(w, g, m, v, *, name, slots=False, rows=512):
    r, c = w.shape
    tr = min(r, rows)
    assert r % tr == 0, (name, r, tr)

    def body(w_ref, g_ref, m_ref, v_ref, go_ref, d_ref, mo_ref, vo_ref):
        if slots:
            g = g_ref[0].astype(F32)
            for k in range(1, N_DEV):
                g = g + g_ref[k].astype(F32)
        else:
            g = g_ref[...].astype(F32)
        delta, m_new, v_new = _adam_math(w_ref[...], g, m_ref[...], v_ref[...])
        go_ref[...] = g
        d_ref[...] = delta
        mo_ref[...] = m_new
        vo_ref[...] = v_new

    spec = pl.BlockSpec((tr, c), lambda i: (i, 0))
    gspec = pl.BlockSpec((N_DEV, tr, c), lambda i: (0, i, 0)) if slots else spec
    shape = jax.ShapeDtypeStruct((r, c), F32)
    return pl.pallas_call(
        body, name=name, grid=(r // tr,),
        in_specs=[spec, gspec, spec, spec], out_specs=[spec] * 4, out_shape=[shape] * 4,
        compiler_params=_params("parallel"),
    )(w, g, m, v)


def slot_sum(slots, *, name):
    _, r, c = slots.shape

    def body(s_ref, o_ref):
        acc = s_ref[0]
        for k in range(1, N_DEV):
            acc = acc + s_ref[k]
        o_ref[...] = acc

    return pl.pallas_call(
        body, name=name, out_shape=jax.ShapeDtypeStruct((r, c), F32),
        in_specs=[VMEM_SPEC], out_specs=VMEM_SPEC,
        compiler_params=pltpu.CompilerParams(vmem_limit_bytes=VMEM_LIMIT),
    )(slots)


def _after(token, val):
    return val + token[0, 0]


def _pad_rows(a, rows):
    return jnp.pad(a, ((0, rows - a.shape[0]), (0, 0)))


def kernel(x, c, ctx, c_ctx, ada_w, ada_b, norm_g, ev_w_in, ev_q_norm, ev_k_norm, ev_dw_w, ev_dw_b, ev_ln_g, ev_ln_b, ev_w_out, od_w_in, od_ln_g, od_ln_b, od_ws, od_bs, od_w_out, final_g, loss_target, m_c_ctx, m_ada_w, m_ada_b, m_norm_g, m_ev_w_in, m_ev_q_norm, m_ev_k_norm, m_ev_dw_w, m_ev_dw_b, m_ev_ln_g, m_ev_ln_b, m_ev_w_out, m_od_w_in, m_od_ln_g, m_od_ln_b, m_od_ws, m_od_bs, m_od_w_out, m_final_g, v_c_ctx, v_ada_w, v_ada_b, v_norm_g, v_ev_w_in, v_ev_q_norm, v_ev_k_norm, v_ev_dw_w, v_ev_dw_b, v_ev_ln_g, v_ev_ln_b, v_ev_w_out, v_od_w_in, v_od_ln_g, v_od_ln_b, v_od_ws, v_od_bs, v_od_w_out, v_final_g):
    n, d = x.shape[1], x.shape[2]
    lc = ctx.shape[1]
    ev_in = ev_w_in.shape[2] * N_DEV
    od_in = od_w_in.shape[2] * N_DEV
    conv_ch = ev_dw_w.shape[2] * N_DEV
    ada_cols = ada_w.shape[2]
    me = 4 * lax.axis_index("x") + 2 * lax.axis_index("y") + lax.axis_index("c")
    xs, tgt, ctxs = x[0], loss_target[0], ctx[0]
    za_block = (2 * KV_W + ATTN_W) // HEAD_DIM
    glu_block = (2 * KV_W + 2 * ATTN_W) // HALF
    zb_block = glu_block + 2 * conv_ch // HALF

    h_w_in0, near_token = staged_gather_start(1, ev_w_in[0].astype(BF16), None, name="gather_w_in0_chips_start")

    small = jnp.concatenate([
        jax.nn.silu(c).reshape(1, d),
        od_ln_g.reshape(1, -1), od_ln_b.reshape(1, -1)], axis=1)
    small = _pad_rows(_after(near_token, small), 8)
    dw_rows = _pad_rows(ev_dw_w[0], 32)
    small_g, dw_g = all_gather([small, dw_rows], name="gather_small", hbm=False)
    sc_all = small_g[:, 0, :d]
    shard = d // N_DEV
    od_ln_g_full = small_g[:, 0, d:d + shard].reshape(d)
    od_ln_b_full = small_g[:, 0, d + shard:d + 2 * shard].reshape(d)
    dw_w_full = jnp.moveaxis(dw_g, 0, 1).reshape(32, conv_ch)[:CONV_WIDTH]
    scc = jax.nn.silu(c_ctx)
    sc16 = _pad_rows(jnp.concatenate([sc_all, scc.reshape(1, d)], axis=0), 16)

    ada_bf = ada_w.astype(BF16)
    mod_loc = [matmul(sc16, ada_bf[l], mode="nn", tm=16, tn=ada_cols, tk=d, out_dtype=F32, name=f"ada_mod{l}")
               for l in range(2)]
    (mod_g,) = all_gather([jnp.stack(mod_loc)], name="gather_mod", hbm=False)
    mod_all = jnp.moveaxis(mod_g, 0, 2).reshape(2, 16, N_DEV * ada_cols) + ada_b[:, None, :]
    mod_me = lax.dynamic_index_in_dim(mod_all, me, axis=1, keepdims=False)
    shift = [mod_me[l, :d] for l in range(2)]
    scale = [mod_me[l, d:2 * d] for l in range(2)]
    gate = [mod_me[l, 2 * d:] for l in range(2)]
    shift_c, scale_c = mod_all[0, 8, :d], mod_all[0, 8, d:2 * d]

    cexp, sexp = rope_tables(n)
    h0 = norm_mod_fwd(xs, norm_g[0], shift[0], scale[0], name="norm_mod_fwd0")
    hc = norm_mod_fwd(ctxs, norm_g[0], shift_c, scale_c, name="norm_mod_fwd_ctx")

    wi0_shard, wi0_land = staged_gather_wait(1, h_w_in0, h0, name="gather_w_in0_chips_wait")
    h_w_in0, far_token = staged_gather_start(2, wi0_shard, wi0_land, name="gather_w_in0_cores_start")
    wi0_shard, wi0_land = staged_gather_wait(2, h_w_in0, far_token, name="gather_w_in0_cores_wait")
    wi0_g = lax.dynamic_update_slice_in_dim(wi0_land, wi0_shard[None], me, axis=0)
    wi0 = jnp.moveaxis(wi0_g, 0, 1).reshape(d, ev_in)
    bits = lax.bitcast_convert_type(wi0_g[0, 0, 0], jnp.uint16)
    landed_zero = jnp.where((bits | 1) == 0, 1.0, 0.0).astype(F32)
    later = [(w[0] + landed_zero).astype(BF16) for w in (ev_w_out, od_w_in, od_w_out)]
    (h_wo0, h_wi1, h_wo1), w_token = exchange_start(later, scatter=False, name="gather_rest_start")

    def landed(handle, after, name):
        own, land = exchange_wait(handle, after, scatter=False, name=name)
        return lax.dynamic_update_slice_in_dim(land, own[None], me, axis=0)

    p0 = matmul(h0, wi0, mode="nn", tm=1024, tn=ev_in // 4, tk=d, out_dtype=F32, name="proj_in0", after=w_token)
    pc = matmul(hc, wi0, mode="nn", tm=lc, tn=2 * KV_W, tk=d, out_dtype=F32, n_out=2 * KV_W, name="proj_in_ctx")
    q_r, k_all, v_ext = qkv_prep_fwd(p0, ev_q_norm[0], ev_k_norm[0], cexp, sexp, latent=True, name="qkv_prep",
                                     rows_all=n + lc)
    k_all, v_ext = qkv_prep_fwd(pc, None, ev_k_norm[0], None, None, latent=False, name="kv_prep_ctx",
                                rows_all=n + lc, k_all=k_all, v_ext=v_ext)
    o_attn, mix_a, lse = flash_fwd(q_r, k_all, v_ext, p0, za_block=za_block, mix_width=ATTN_W + conv_ch,
                                   name="flash_fwd")
    ycv, mix0 = conv_fwd(p0, dw_w_full, ev_dw_b[0], ev_ln_g[0], ev_ln_b[0], mix_a, glu_block=glu_block,
                         name="conv_fwd")
    wo0 = landed(h_wo0, ycv, "gather_w_out0_wait").reshape(-1, d)
    x1, r0, h1 = matmul(mix0, wo0, mode="nn", tm=512, tn=d, tk=mix0.shape[1], out_dtype=F32, name="proj_out0",
                        res=xs, gate=_row(gate[0]), norm=(norm_g[1], shift[1], scale[1]))

    wi1 =jnp.moveaxis(landed(h_wi1, h1, "gather_w_in1_wait"), 0, 1).reshape(d, od_in)
    p1 = matmul(h1, wi1, mode="nn", tm=1024, tn=od_in // 4, tk=d, out_dtype=F32, name="proj_in1")
    ws_bf = od_ws[0]
    bs_t = od_bs[0].T
    m1 = sgu_fwd(p1, od_ln_g_full, od_ln_b_full, ws_bf, bs_t, name="sgu_fwd")
    wo1 = landed(h_wo1, m1, "gather_w_out1_wait").reshape(-1, d)
    x2, r1 = matmul(m1, wo1, mode="nn", tm=1024, tn=1024, tk=m1.shape[1], out_dtype=F32, name="proj_out1",
                    res=x1, gate=_row(gate[1]))

    dx2, dr1, acc_final, loss_tile = final_loss(x2, tgt, final_g, r1, gate[1], name="final_loss")

    dm1 = matmul(dr1, wo1, mode="nt", tm=1024, tn=1024, tk=d, out_dtype=F32, name="d_mix1")
    dwo1 = matmul(m1, dr1, mode="tn", tm=1024, tn=1024, tk=2048, out_dtype=BF16, name="d_wout1")
    dp1, dws, dbs_t, acc_sgu = sgu_bwd(p1, dm1, od_ln_g_full, od_ln_b_full, ws_bf, jnp.swapaxes(ws_bf, 1, 2), bs_t,
                                       name="sgu_bwd")
    dwi1_s = matmul(h1, dp1, mode="tn", tm=2048, tn=od_in // N_DEV, tk=2048, out_dtype=BF16, name="d_win1",
                    split_out=True)
    dwo1_s = dwo1.reshape(N_DEV, -1, d)
    (h_gi1, h_go1), g1_token = exchange_start([dwi1_s, dwo1_s], scatter=True, name="grads1_start")
    dh1 = matmul(dp1, wi1, mode="nt", tm=1024, tn=512, tk=od_in, out_dtype=F32, name="d_h1")
    dx1, dr0, acc_norm1 = norm_mod_bwd(x1, dh1, dx2, norm_g[1], _after(g1_token, scale[1]), name="norm_mod_bwd1",
                                       branch=(r0, gate[0]))

    dmix0 = matmul(dr0, wo0, mode="nt", tm=1024, tn=1024, tk=d, out_dtype=F32, name="d_mix0")
    dwo0 = matmul(mix0, dr0, mode="tn", tm=1024, tn=1024, tk=2048, out_dtype=BF16, name="d_wout0")
    do_attn, do_s, dza, delta = attn_gate_bwd(dmix0, o_attn, p0, za_block=za_block, name="attn_gate_bwd")
    dycv, dzb, acc_ln = conv_bwd_rows(dmix0, ycv, p0, ev_ln_g[0], ev_ln_b[0], mix_block=ATTN_W // conv_ch,
                                      zb_block=zb_block, name="conv_bwd_rows")
    dglu, ddw_w, acc_dwb = conv_bwd_taps(dycv, p0, dw_w_full, glu_block=glu_block, name="conv_bwd_taps")
    dq_r, dk_all, dv_all = flash_bwd(q_r, do_attn, do_s, lse, delta, k_all, v_ext, name="flash_bwd")
    dkvq, acc_qk = qkv_prep_bwd(p0, dq_r, dk_all[:n], dv_all[:n], ev_q_norm[0], ev_k_norm[0], cexp, sexp,
                                latent=True, name="qkv_prep_bwd")
    dpc, acc_kc = qkv_prep_bwd(pc, None, dk_all[n:], dv_all[n:], None, ev_k_norm[0], None, None,
                               latent=False, name="kv_prep_ctx_bwd")
    dp0 = jnp.concatenate([dkvq, dza, dglu, dzb], axis=1)
    dwi0 = matmul(h0, dp0, mode="tn", tm=1024, tn=ev_in // 4, tk=2048, out_dtype=F32, name="d_win0")
    dwi0 = matmul(hc, dpc, mode="tn", tm=512, tn=2 * KV_W, tk=lc, out_dtype=F32, name="d_win0_ctx", add=dwi0)
    dwi0_s = jnp.moveaxis(dwi0.astype(BF16).reshape(d, N_DEV, ev_in // N_DEV), 1, 0)
    dwo0_s = dwo0.reshape(N_DEV, -1, d)
    (h_gi0, h_go0), g0_token = exchange_start([dwi0_s, dwo0_s], scatter=True, name="grads0_start")
    dh0 = matmul(dp0, wi0, mode="nt", tm=1024, tn=512, tk=ev_in, out_dtype=F32, name="d_h0", after=g0_token)
    dhc = matmul(dpc, wi0, mode="nt", tm=lc, tn=512, tk=2 * KV_W, out_dtype=F32, name="d_h_ctx")
    grad_x, acc_norm0 = norm_mod_bwd(xs, dh0, dx1, norm_g[0], _after(g0_token, scale[0]), name="norm_mod_bwd0")
    _, acc_normc = norm_mod_bwd(ctxs, dhc, None, norm_g[0], scale_c, name="norm_mod_bwd_ctx")

    zeros_d = jnp.zeros((d,), F32)
    dmod0 = jnp.stack([acc_norm0[0], acc_norm0[1], acc_norm1[GATE_ROW]])
    dmod1 = jnp.stack([acc_norm1[0], acc_norm1[1], acc_final[GATE_ROW]])
    dmodc = jnp.stack([acc_normc[0], acc_normc[1]])
    half_pad = jnp.zeros((d - 2 * conv_ch,), F32) if d > 2 * conv_ch else jnp.zeros((0,), F32)
    row_a = jnp.concatenate([acc_dwb[0], acc_ln[0], half_pad])
    row_b = jnp.concatenate([acc_ln[1], acc_qk[0], acc_qk[1] + acc_kc[1],
                             jnp.zeros((d - conv_ch - 2 * HEAD_DIM,), F32)])
    row_c = jnp.concatenate([dbs_t.T.reshape(-1), jnp.zeros((d - SGU_GROUPS * CHUNK,), F32)])
    row_loss = jnp.concatenate([loss_tile[0, :1], jnp.zeros((d - 1,), F32)])
    pack = jnp.concatenate([
        dmod0, dmod1, dmodc,
        (acc_norm0[2] + acc_normc[2])[None], acc_norm1[2][None],
        acc_final[0][None],
        acc_sgu[0][None], acc_sgu[1][None],
        row_a[None], row_b[None], row_c[None], row_loss[None],
        ddw_w.reshape(-1, d),
        dws.reshape(-1, d),
    ], axis=0)
    n_rows = pack.shape[0]
    pack = _pad_rows(pack, -(-n_rows // 8) * 8)
    (h_pack,), pack_token = exchange_start([pack], scatter=False, name="small_grads_start")

    def summands(handle, after, name):
        mine, land = exchange_wait(handle, after, scatter=True, name=name)
        own = lax.dynamic_index_in_dim(mine, me, axis=0, keepdims=True)
        return lax.dynamic_update_slice_in_dim(land, own, me, axis=0)

    out = {}

    def upd(key, w, g, m, v, slots=False, rows=512):
        shp = w.shape
        w2 = w.reshape(-1, shp[-1])
        g2 = g.reshape((N_DEV, -1, shp[-1])) if slots else g.reshape(-1, shp[-1])
        res = adamw(w2, g2, m.reshape(w2.shape), v.reshape(w2.shape), name="adamw_" + key, slots=slots, rows=rows)
        out[key] = tuple(r.reshape(shp) for r in res)

    gi1 = summands(h_gi1, pack_token, "grads_w_in1_wait")
    upd("od_w_in", od_w_in, gi1, m_od_w_in, v_od_w_in, slots=True, rows=256)
    go1 = summands(h_go1, out["od_w_in"][1], "grads_w_out1_wait")
    upd("od_w_out", od_w_out, go1, m_od_w_out, v_od_w_out, slots=True, rows=256)
    go0 = summands(h_go0, out["od_w_out"][1], "grads_w_out0_wait")
    upd("ev_w_out", ev_w_out, go0, m_ev_w_out, v_ev_w_out, slots=True, rows=256)
    gi0 = summands(h_gi0, out["ev_w_out"][1], "grads_w_in0_wait")
    upd("ev_w_in", ev_w_in, gi0, m_ev_w_in, v_ev_w_in, slots=True, rows=256)

    pack_own, pack_land = exchange_wait(h_pack, out["ev_w_in"][1], scatter=False, name="small_grads_wait")
    pack_g = lax.dynamic_update_slice_in_dim(pack_land, pack_own[None], me, axis=0)
    gsum = slot_sum(pack_g, name="sum_small_grads")
    loss = gsum[16, 0]
    dw_rows_n = 32 * conv_ch // d
    g_dw_w = gsum[17:17 + dw_rows_n].reshape(32, conv_ch)[:CONV_WIDTH]
    g_od_ws = gsum[17 + dw_rows_n:17 + dw_rows_n + SGU_GROUPS * CHUNK * CHUNK // d].reshape(od_ws.shape)

    dmodc_sum = jnp.concatenate([gsum[6], gsum[7], zeros_d])
    col0 = me * ada_cols
    dm_cols = []
    for l in range(2):
        rows = pack_g[:, 3 * l:3 * l + 3, :].reshape(N_DEV, 3 * d)
        extra = dmodc_sum[None] if l == 0 else jnp.zeros((1, 3 * d), F32)
        full = _pad_rows(jnp.concatenate([rows, extra], axis=0), 16)
        dm_cols.append(lax.dynamic_slice_in_dim(full, col0, ada_cols, axis=1))
    g_ada_w = jnp.stack([matmul(sc16, dm_cols[l], mode="tn", tm=512, tn=ada_cols, tk=16, out_dtype=F32,
                                name=f"d_ada_w{l}") for l in range(2)])
    dsc = matmul(dm_cols[0], ada_bf[0], mode="nt", tm=16, tn=512, tk=ada_cols, out_dtype=F32, name="d_scc")
    (_, dscc_sum) = all_gather([dsc[8:16]], name="gather_dscc", hbm=False, sum_out=True)
    sg = jax.nn.sigmoid(c_ctx)
    g_c_ctx = dscc_sum[0] * (sg * (1.0 + c_ctx * (1.0 - sg)))
    g_ada_b = jnp.stack([gsum[0:3].reshape(-1) + dmodc_sum, gsum[3:6].reshape(-1)])

    upd("ada_w", ada_w, g_ada_w, m_ada_w, v_ada_w)

    def my_shard(full, size):
        return lax.dynamic_slice_in_dim(full, me * size, size, axis=full.ndim - 1)

    small_items = [
        ("c_ctx", c_ctx, g_c_ctx, m_c_ctx, v_c_ctx),
        ("ada_b", ada_b, g_ada_b, m_ada_b, v_ada_b),
        ("norm_g", norm_g, gsum[8:10], m_norm_g, v_norm_g),
        ("ev_q_norm", ev_q_norm, gsum[14, conv_ch:conv_ch + HEAD_DIM], m_ev_q_norm, v_ev_q_norm),
        ("ev_k_norm", ev_k_norm, gsum[14, conv_ch + HEAD_DIM:conv_ch + 2 * HEAD_DIM], m_ev_k_norm, v_ev_k_norm),
        ("ev_dw_w", ev_dw_w, my_shard(g_dw_w, conv_ch // N_DEV), m_ev_dw_w, v_ev_dw_w),
        ("ev_dw_b", ev_dw_b, gsum[13, :conv_ch], m_ev_dw_b, v_ev_dw_b),
        ("ev_ln_g", ev_ln_g, gsum[13, conv_ch:2 * conv_ch], m_ev_ln_g, v_ev_ln_g),
        ("ev_ln_b", ev_ln_b, gsum[14, :conv_ch], m_ev_ln_b, v_ev_ln_b),
        ("od_ln_g", od_ln_g, my_shard(gsum[11], shard), m_od_ln_g, v_od_ln_g),
        ("od_ln_b", od_ln_b, my_shard(gsum[12], shard), m_od_ln_b, v_od_ln_b),
        ("od_ws", od_ws, g_od_ws, m_od_ws, v_od_ws),
        ("od_bs", od_bs, gsum[15, :SGU_GROUPS * CHUNK], m_od_bs, v_od_bs),
        ("final_g", final_g, gsum[10], m_final_g, v_final_g),
    ]
    sizes = [it[1].size for it in small_items]
    total = sum(sizes)
    lanes = 1024
    prow = -(-total // lanes)
    prow = -(-prow // 8) * 8

    def pack_small(idx):
        flat = jnp.concatenate([it[idx].reshape(-1).astype(F32) for it in small_items])
        return jnp.pad(flat, (0, prow * lanes - total)).reshape(prow, lanes)

    sres = adamw(pack_small(1), pack_small(2), pack_small(3), pack_small(4), name="adamw_small", rows=prow)
    off = 0
    for it, size in zip(small_items, sizes):
        out[it[0]] = tuple(r.reshape(-1)[off:off + size].reshape(it[1].shape) for r in sres)
        off += size

    names = ['c_ctx', 'ada_w', 'ada_b', 'norm_g', 'ev_w_in', 'ev_q_norm', 'ev_k_norm', 'ev_dw_w', 'ev_dw_b',
             'ev_ln_g', 'ev_ln_b', 'ev_w_out', 'od_w_in', 'od_ln_g', 'od_ln_b', 'od_ws', 'od_bs', 'od_w_out',
             'final_g']
    return (loss, grad_x[None], *[out[k][0] for k in names], *[out[k][1] for k in names],
            *[out[k][2] for k in names], *[out[k][3] for k in names])
```

```python
import functools
import math

import jax
import jax.numpy as jnp
from jax import lax
from jax.experimental import pallas as pl
from jax.experimental.pallas import tpu as pltpu

F32 = jnp.float32
BF16 = jnp.bfloat16
MESH = pl.DeviceIdType.MESH

EPS = 1e-6
HEAD_DIM = 128
N_Q_HEADS = 8
N_KV_HEADS = 2
Q_PER_KV = N_Q_HEADS // N_KV_HEADS
ATTN_W = N_Q_HEADS * HEAD_DIM
KV_W = N_KV_HEADS * HEAD_DIM
ATTN_SCALE = HEAD_DIM ** -0.5
LN2 = math.log(2.0)
Q_SCALE = ATTN_SCALE / LN2
ROPE_THETA = 10000.0
GRID_W = 64
CONV_WIDTH = 31
CONV_HALF = CONV_WIDTH // 2
HALO = 16
CHUNK = 128
SGU_GROUPS = 8
N_DEV = 8

ADAM_LR = 0.001
ADAM_B1 = 0.9
ADAM_B2 = 0.999
ADAM_EPS = 1e-08
ADAM_WD = 0.01
ADAM_STEP = 10

VMEM_LIMIT = 56 * 1024 * 1024
ANY = pl.BlockSpec(memory_space=pl.ANY)
VMEM_SPEC = pl.BlockSpec(memory_space=pltpu.VMEM)


def _params(*sem):
    return pltpu.CompilerParams(dimension_semantics=sem, vmem_limit_bytes=VMEM_LIMIT)


def _sigmoid(x):
    return 1.0 / (1.0 + jnp.exp(-x))


def _silu(x):
    return x * _sigmoid(x)


def _dsilu(x):
    s = _sigmoid(x)
    return s * (1.0 + x * (1.0 - s))


_GELU_C = math.sqrt(2.0 / math.pi)


def _gelu(x):
    t = jnp.tanh(_GELU_C * (x + 0.044715 * (x * x * x)))
    return 0.5 * x * (1.0 + t)


def _dgelu(x):
    t = jnp.tanh(_GELU_C * (x + 0.044715 * (x * x * x)))
    return 0.5 * (1.0 + t) + 0.5 * x * (1.0 - t * t) * (_GELU_C * (1.0 + 3.0 * 0.044715 * (x * x)))


def _row(v):
    return v.reshape(1, -1).astype(F32)


def _flat_id(p):
    return 4 * p[0] + 2 * p[1] + p[2]


def _gather_body(n_arr, sum_out):
    def body(*refs):
        x_refs = refs[:n_arr]
        out_refs = refs[n_arr:2 * n_arr]
        pos = 2 * n_arr
        sum_refs = refs[pos:pos + n_arr] if sum_out else ()
        pos += n_arr if sum_out else 0
        send_sems, recv_sems, local_sems = refs[pos:pos + 3]
        x, y, c = lax.axis_index("x"), lax.axis_index("y"), lax.axis_index("c")
        me, sibling = (x, y, c), (x, y, 1 - c)
        chips = [(1 - x, y), (x, 1 - y), (1 - x, 1 - y)]

        def copy(a, k, block, to, src=None):
            rows = out_refs[a].at[_flat_id(block)]
            return pltpu.make_async_remote_copy(
                src_ref=rows if src is None else src, dst_ref=rows,
                send_sem=send_sems.at[a, k], recv_sem=recv_sems.at[a, k],
                device_id=to, device_id_type=MESH)

        sends = []
        mine = []
        for a in range(n_arr):
            cp = pltpu.make_async_copy(x_refs[a], out_refs[a].at[_flat_id(me)], local_sems.at[a])
            cp.start()
            mine.append(cp)
            first = [copy(a, 0, me, sibling, src=x_refs[a])]
            first += [copy(a, 1 + j, me, (*chip, c), src=x_refs[a]) for j, chip in enumerate(chips)]
            for cp in first:
                cp.start()
            sends += first
        for a in range(n_arr):
            for j, chip in enumerate(chips):
                copy(a, 1 + j, (*chip, c), me).wait_recv()
                fwd = copy(a, 4 + j, (*chip, c), sibling)
                fwd.start()
                sends.append(fwd)
        for a in range(n_arr):
            copy(a, 0, sibling, me).wait_recv()
            for j, chip in enumerate(chips):
                copy(a, 4 + j, (*chip, 1 - c), me).wait_recv()
        for cp in sends:
            cp.wait_send()
        for cp in mine:
            cp.wait()
        if sum_out:
            for a in range(n_arr):
                acc = out_refs[a][0]
                for k in range(1, N_DEV):
                    acc = acc + out_refs[a][k]
                sum_refs[a][...] = acc

    return body


def all_gather(arrs, *, name, hbm, sum_out=False):
    n = len(arrs)
    spec = ANY if hbm else VMEM_SPEC
    out_shape = [jax.ShapeDtypeStruct((N_DEV,) + a.shape, a.dtype) for a in arrs]
    out_specs = [spec] * n
    if sum_out:
        out_shape += [jax.ShapeDtypeStruct(a.shape, a.dtype) for a in arrs]
        out_specs += [VMEM_SPEC] * n
    res = pl.pallas_call(
        _gather_body(n, sum_out), name=name,
        out_shape=out_shape, in_specs=[spec] * n, out_specs=out_specs,
        scratch_shapes=[pltpu.SemaphoreType.DMA((n, 7)), pltpu.SemaphoreType.DMA((n, 7)),
                        pltpu.SemaphoreType.DMA((n,))],
        compiler_params=pltpu.CompilerParams(vmem_limit_bytes=VMEM_LIMIT),
    )(*arrs)
    return res


_RELATIONS = [(rx, ry, rc) for rx in (0, 1) for ry in (0, 1) for rc in (0, 1)][1:]


HBM_SPEC = pl.BlockSpec(memory_space=pltpu.HBM)
SEM_SPEC = pl.BlockSpec(memory_space=pltpu.SEMAPHORE)
_DATAFLOW = pltpu.SideEffectType.DATAFLOW_SIDE_EFFECTING
N_PEERS = N_DEV - 1


def _peer_copy(src_ref, land_ref, send_sem, recv_sem, k, rel, scatter, sending):
    x, y, c = lax.axis_index("x"), lax.axis_index("y"), lax.axis_index("c")
    rx, ry, rc = rel
    peer = (1 - x if rx else x, 1 - y if ry else y, 1 - c if rc else c)
    src = src_ref.at[_flat_id(peer)] if scatter else src_ref
    dst = land_ref.at[_flat_id((x, y, c)) if sending else _flat_id(peer)]
    return pltpu.make_async_remote_copy(src_ref=src, dst_ref=dst, send_sem=send_sem.at[k], recv_sem=recv_sem.at[k],
                                        device_id=peer, device_id_type=MESH)


def exchange_start(arrs, *, scatter, name):
    n = len(arrs)
    lands = [lax.empty((N_DEV,) + (a.shape[1:] if scatter else a.shape), a.dtype) for a in arrs]

    def body(*refs):
        srcs, lnds, sems = refs[:n], refs[n:2 * n], refs[2 * n:4 * n]
        token = refs[6 * n]
        for a in range(n):
            for k, rel in enumerate(_RELATIONS):
                _peer_copy(srcs[a], lnds[a], sems[2 * a], sems[2 * a + 1], k, rel, scatter, True).start()
        token[...] = jnp.zeros_like(token)

    outs = pl.pallas_call(
        body, name=name,
        out_shape=[pltpu.SemaphoreType.DMA((N_PEERS,))] * (2 * n)
        + [pltpu.HBM(a.shape, a.dtype) for a in arrs] + [pltpu.HBM(l.shape, l.dtype) for l in lands]
        + [jax.ShapeDtypeStruct((8, 128), F32)],
        in_specs=[HBM_SPEC] * (2 * n),
        out_specs=[SEM_SPEC] * (2 * n) + [HBM_SPEC] * (2 * n) + [VMEM_SPEC],
        input_output_aliases={i: 2 * n + i for i in range(2 * n)},
        compiler_params=pltpu.CompilerParams(has_side_effects=_DATAFLOW),
    )(*[pltpu.with_memory_space_constraint(a, pltpu.HBM) for a in arrs],
      *[pltpu.with_memory_space_constraint(l, pltpu.HBM) for l in lands])
    handles = [(outs[2 * a], outs[2 * a + 1], outs[2 * n + a], outs[3 * n + a]) for a in range(n)]
    return handles, outs[4 * n]


def exchange_wait(handle, after, *, scatter, name):
    send_sem, recv_sem, src, land = handle

    def body(src_ref, land_ref, send_ref, recv_ref, after_ref, src_out, land_out):
        for k, rel in enumerate(_RELATIONS):
            cp = _peer_copy(src_ref, land_ref, send_ref, recv_ref, k, rel, scatter, False)
            cp.wait_send()
            cp.wait_recv()

    outs = pl.pallas_call(
        body, name=name,
        out_shape=[pltpu.HBM(src.shape, src.dtype), pltpu.HBM(land.shape, land.dtype)],
        in_specs=[HBM_SPEC, HBM_SPEC, SEM_SPEC, SEM_SPEC, ANY],
        out_specs=[HBM_SPEC, HBM_SPEC], input_output_aliases={0: 0, 1: 1},
        compiler_params=pltpu.CompilerParams(has_side_effects=_DATAFLOW),
    )(src, land, send_sem, recv_sem, after)
    return outs[0], outs[1]


_STAGE1 = [(0, 0, 1), (1, 0, 0), (0, 1, 0), (1, 1, 0)]
_OTHER_CHIPS = [(1, 0), (0, 1), (1, 1)]


def _stage_copy(stage, k, shard_ref, land_ref, send_sem, recv_sem, sending):
    x, y, c = lax.axis_index("x"), lax.axis_index("y"), lax.axis_index("c")
    if stage == 1:
        rx, ry, rc = _STAGE1[k]
        peer = (1 - x if rx else x, 1 - y if ry else y, 1 - c if rc else c)
        src = shard_ref
        dst = land_ref.at[_flat_id((x, y, c)) if sending else _flat_id(peer)]
    else:
        cx, cy = _OTHER_CHIPS[k]
        peer = (x, y, 1 - c)
        chip = (1 - x if cx else x, 1 - y if cy else y)
        slot = _flat_id((*chip, c)) if sending else _flat_id((*chip, 1 - c))
        src = land_ref.at[_flat_id((*chip, c))]
        dst = land_ref.at[slot]
    return pltpu.make_async_remote_copy(src_ref=src, dst_ref=dst, send_sem=send_sem.at[k], recv_sem=recv_sem.at[k],
                                        device_id=peer, device_id_type=MESH)


def staged_gather_start(stage, shard, land, *, name):
    n_copies = len(_STAGE1) if stage == 1 else len(_OTHER_CHIPS)
    if land is None:
        land = lax.empty((N_DEV,) + shard.shape, shard.dtype)

    def body(shard_ref, land_ref, send_sem, recv_sem, shard_thru, land_thru, token):
        for k in range(n_copies):
            _stage_copy(stage, k, shard_ref, land_ref, send_sem, recv_sem, True).start()
        token[...] = jnp.zeros_like(token)

    outs = pl.pallas_call(
        body, name=name,
        out_shape=[pltpu.SemaphoreType.DMA((n_copies,)), pltpu.SemaphoreType.DMA((n_copies,)),
                   pltpu.HBM(shard.shape, shard.dtype), pltpu.HBM(land.shape, land.dtype),
                   jax.ShapeDtypeStruct((8, 128), F32)],
        in_specs=[HBM_SPEC, HBM_SPEC],
        out_specs=[SEM_SPEC, SEM_SPEC, HBM_SPEC, HBM_SPEC, VMEM_SPEC],
        input_output_aliases={0: 2, 1: 3},
        compiler_params=pltpu.CompilerParams(has_side_effects=_DATAFLOW),
    )(pltpu.with_memory_space_constraint(shard, pltpu.HBM), pltpu.with_memory_space_constraint(land, pltpu.HBM))
    return outs[:4], outs[4]


def staged_gather_wait(stage, handle, after, *, name):
    send_sem, recv_sem, shard, land = handle
    n_copies = len(_STAGE1) if stage == 1 else len(_OTHER_CHIPS)

    def body(shard_ref, land_ref, send_ref, recv_ref, after_ref, shard_out, land_out):
        for k in range(n_copies):
            cp = _stage_copy(stage, k, shard_ref, land_ref, send_ref, recv_ref, False)
            cp.wait_send()
            cp.wait_recv()

    outs = pl.pallas_call(
        body, name=name,
        out_shape=[pltpu.HBM(shard.shape, shard.dtype), pltpu.HBM(land.shape, land.dtype)],
        in_specs=[HBM_SPEC, HBM_SPEC, SEM_SPEC, SEM_SPEC, ANY],
        out_specs=[HBM_SPEC, HBM_SPEC], input_output_aliases={0: 0, 1: 1},
        compiler_params=pltpu.CompilerParams(has_side_effects=_DATAFLOW),
    )(shard, land, send_sem, recv_sem, after)
    return outs[0], outs[1]


_DIMS = {"nn": (((1,), (0,)), ((), ())), "nt": (((1,), (1,)), ((), ())), "tn": (((0,), (0,)), ((), ()))}


def matmul(a, b, *, mode, tm, tn, tk, out_dtype, name, n_out=None, res=None, gate=None, add=None, after=None,
           split_out=False, norm=None):
    if mode == "tn":
        kdim, m = a.shape
    else:
        m, kdim = a.shape
    nfull = b.shape[0] if mode == "nt" else b.shape[1]
    n = nfull if n_out is None else n_out
    tm, tn, tk = min(tm, m), min(tn, n), min(tk, kdim)
    assert m % tm == 0 and n % tn == 0 and kdim % tk == 0, (name, m, n, kdim, tm, tn, tk)
    nk = kdim // tk
    dims = _DIMS[mode]
    a_spec = pl.BlockSpec((tk, tm), lambda j, i, k: (k, i)) if mode == "tn" else pl.BlockSpec((tm, tk), lambda j, i, k: (i, k))
    b_spec = pl.BlockSpec((tn, tk), lambda j, i, k: (j, k)) if mode == "nt" else pl.BlockSpec((tk, tn), lambda j, i, k: (k, j))
    o_spec = pl.BlockSpec((tm, tn), lambda j, i, k: (i, j))
    in_specs = [a_spec, b_spec]
    operands = [a, b]
    aliases = {}
    vec = pl.BlockSpec((1, tn), lambda j, i, k: (0, j))
    if res is not None:
        in_specs += [o_spec, vec]
        operands += [res, gate]
    if norm is not None:
        assert res is not None and tn == n, (name, tn, n)
        in_specs += [vec, vec, vec]
        operands += [_row(v) for v in norm]
    if add is not None:
        in_specs += [o_spec]
        aliases = {len(operands): 0}
        operands += [add]
    if after is not None:
        in_specs += [ANY]
        operands += [after]
    out_cols = n if add is None else add.shape[1]
    out_shape = [jax.ShapeDtypeStruct((m, out_cols), out_dtype)]
    out_specs = [o_spec]
    if split_out:
        out_shape = [jax.ShapeDtypeStruct((n // tn, m, tn), out_dtype)]
        out_specs = [pl.BlockSpec((None, tm, tn), lambda j, i, k: (j, i, 0))]
    if res is not None:
        out_shape.append(jax.ShapeDtypeStruct((m, n), BF16))
        out_specs.append(o_spec)
    if norm is not None:
        out_shape.append(jax.ShapeDtypeStruct((m, n), BF16))
        out_specs.append(o_spec)

    def body(*refs):
        a_ref, b_ref = refs[:2]
        pos = 2
        if res is not None:
            res_ref, gate_ref = refs[pos:pos + 2]
            pos += 2
        if norm is not None:
            ng_ref, nshift_ref, nscale_ref = refs[pos:pos + 3]
            pos += 3
        if add is not None:
            add_ref = refs[pos]
            pos += 1
        if after is not None:
            pos += 1
        o_ref = refs[pos]
        pos += 1
        if res is not None:
            r_ref = refs[pos]
            pos += 1
        if norm is not None:
            h_ref = refs[pos]
            pos += 1
        acc_ref = refs[pos] if nk > 1 else None
        prod = lax.dot_general(a_ref[...].astype(BF16), b_ref[...].astype(BF16), dims,
                               preferred_element_type=F32)

        def finish(acc):
            if res is not None:
                xnew = res_ref[...] + gate_ref[...] * acc
                o_ref[...] = xnew.astype(out_dtype)
                r_ref[...] = acc.astype(BF16)
                if norm is not None:
                    rstd = lax.rsqrt(jnp.mean(xnew * xnew, axis=-1, keepdims=True) + EPS)
                    y = xnew * rstd * ng_ref[...]
                    h_ref[...] = (y * (1.0 + nscale_ref[...]) + nshift_ref[...]).astype(BF16)
            elif add is not None:
                o_ref[...] = (add_ref[...] + acc).astype(out_dtype)
            else:
                o_ref[...] = acc.astype(out_dtype)

        if nk == 1:
            finish(prod)
        else:
            k = pl.program_id(2)

            @pl.when(k == 0)
            def _():
                acc_ref[...] = prod

            @pl.when(k > 0)
            def _():
                acc_ref[...] += prod

            @pl.when(k == nk - 1)
            def _():
                finish(acc_ref[...])

    outs = pl.pallas_call(
        body, name=name, grid=(n // tn, m // tm, nk),
        in_specs=in_specs, out_specs=out_specs, out_shape=out_shape,
        scratch_shapes=[pltpu.VMEM((tm, tn), F32)] if nk > 1 else [],
        input_output_aliases=aliases,
        compiler_params=_params("parallel", "parallel", "arbitrary"),
    )(*operands)
    return outs if res is not None else outs[0]


def _rows_tile(n, want):
    return min(n, want)


def norm_mod_fwd(x, g, shift, scale, *, name):
    n, d = x.shape
    tt = _rows_tile(n, 256)

    def body(x_ref, g_ref, sh_ref, sc_ref, h_ref):
        xv = x_ref[...]
        rstd = lax.rsqrt(jnp.mean(xv * xv, axis=-1, keepdims=True) + EPS)
        y = xv * rstd * g_ref[...]
        h_ref[...] = (y * (1.0 + sc_ref[...]) + sh_ref[...]).astype(BF16)

    vec = pl.BlockSpec((1, d), lambda i: (0, 0))
    return pl.pallas_call(
        body, name=name, grid=(n // tt,),
        in_specs=[pl.BlockSpec((tt, d), lambda i: (i, 0)), vec, vec, vec],
        out_specs=pl.BlockSpec((tt, d), lambda i: (i, 0)),
        out_shape=jax.ShapeDtypeStruct((n, d), BF16),
        compiler_params=_params("parallel"),
    )(x, _row(g), _row(shift), _row(scale))


GATE_ROW = 3


def norm_mod_bwd(x, dh, dres, g, scale, *, name, branch=None):
    n, d = x.shape
    tt = _rows_tile(n, 256)
    has_res = dres is not None
    has_branch = branch is not None
    last = n // tt - 1

    def body(*refs):
        x_ref, dh_ref = refs[:2]
        pos = 2
        if has_res:
            dres_ref = refs[pos]
            pos += 1
        if has_branch:
            r_ref, gate_ref = refs[pos:pos + 2]
            pos += 2
        g_ref, sc_ref, dx_ref = refs[pos:pos + 3]
        pos += 3
        if has_branch:
            dr_ref = refs[pos]
            pos += 1
        acc_ref, s_ref = refs[pos:pos + 2]
        i = pl.program_id(0)
        xv = x_ref[...]
        dhv = dh_ref[...]
        rstd = lax.rsqrt(jnp.mean(xv * xv, axis=-1, keepdims=True) + EPS)
        xhat = xv * rstd
        dxhat = dhv * (g_ref[...] * (1.0 + sc_ref[...]))
        dx = rstd * (dxhat - xhat * jnp.mean(dxhat * xhat, axis=-1, keepdims=True))
        if has_res:
            dx = dx + dres_ref[...]
        dx_ref[...] = dx

        @pl.when(i == 0)
        def _():
            s_ref[...] = jnp.zeros_like(s_ref)

        s_ref[0:1, :] += jnp.sum(dhv, axis=0, keepdims=True)
        s_ref[1:2, :] += jnp.sum(dhv * xhat, axis=0, keepdims=True)
        if has_branch:
            dr_ref[...] = (dx * gate_ref[...]).astype(BF16)
            s_ref[2:3, :] += jnp.sum(dx * r_ref[...].astype(F32), axis=0, keepdims=True)

        @pl.when(i == last)
        def _():
            s1 = s_ref[0:1, :]
            s2 = s_ref[1:2, :]
            acc_ref[...] = jnp.zeros_like(acc_ref)
            acc_ref[0:1, :] = s1
            acc_ref[1:2, :] = s2 * g_ref[...]
            acc_ref[2:3, :] = s2 * (1.0 + sc_ref[...])
            acc_ref[GATE_ROW:GATE_ROW + 1, :] = s_ref[2:3, :]

    vec = pl.BlockSpec((1, d), lambda i: (0, 0))
    big = pl.BlockSpec((tt, d), lambda i: (i, 0))
    ops = [x, dh] + ([dres] if has_res else []) + ([branch[0], _row(branch[1])] if has_branch else [])
    ops += [_row(g), _row(scale)]
    return pl.pallas_call(
        body, name=name, grid=(n // tt,),
        in_specs=[big, big] + ([big] if has_res else []) + ([big, vec] if has_branch else []) + [vec, vec],
        out_specs=[big] + ([big] if has_branch else []) + [pl.BlockSpec((8, d), lambda i: (0, 0))],
        out_shape=[jax.ShapeDtypeStruct((n, d), F32)] + ([jax.ShapeDtypeStruct((n, d), BF16)] if has_branch else [])
        + [jax.ShapeDtypeStruct((8, d), F32)],
        scratch_shapes=[pltpu.VMEM((8, d), F32)],
        compiler_params=_params("arbitrary"),
    )(*ops)


def final_loss(x, target, g, r, gate, *, name):
    n, d = x.shape
    tt = _rows_tile(n, 256)

    def body(x_ref, t_ref, g_ref, r_ref, gate_ref, dx_ref, dr_ref, acc_ref, loss_ref):
        i = pl.program_id(0)
        xv = x_ref[...]
        rstd = lax.rsqrt(jnp.mean(xv * xv, axis=-1, keepdims=True) + EPS)
        xhat = xv * rstd
        e = xhat * g_ref[...] - t_ref[...]
        dy = e * (1.0 / d)
        dxhat = dy * g_ref[...]
        dx = rstd * (dxhat - xhat * jnp.mean(dxhat * xhat, axis=-1, keepdims=True))
        dx_ref[...] = dx
        dr_ref[...] = (dx * gate_ref[...]).astype(BF16)

        @pl.when(i == 0)
        def _():
            acc_ref[...] = jnp.zeros_like(acc_ref)
            loss_ref[...] = jnp.zeros_like(loss_ref)

        acc_ref[0:1, :] += jnp.sum(dy * xhat, axis=0, keepdims=True)
        acc_ref[GATE_ROW:GATE_ROW + 1, :] += jnp.sum(dx * r_ref[...].astype(F32), axis=0, keepdims=True)
        part = 0.5 * jnp.sum(jnp.mean(e * e, axis=-1, keepdims=True), axis=0, keepdims=True)
        loss_ref[...] += jnp.broadcast_to(part, loss_ref.shape)

    big = pl.BlockSpec((tt, d), lambda i: (i, 0))
    vec = pl.BlockSpec((1, d), lambda i: (0, 0))
    return pl.pallas_call(
        body, name=name, grid=(n // tt,),
        in_specs=[big, big, vec, big, vec],
        out_specs=[big, big, pl.BlockSpec((8, d), lambda i: (0, 0)), pl.BlockSpec((8, 128), lambda i: (0, 0))],
        out_shape=[jax.ShapeDtypeStruct((n, d), F32), jax.ShapeDtypeStruct((n, d), BF16),
                   jax.ShapeDtypeStruct((8, d), F32), jax.ShapeDtypeStruct((8, 128), F32)],
        compiler_params=_params("arbitrary"),
    )(x, target, _row(g), r, _row(gate))


def _as_row(col):
    t = col.shape[0]
    return jnp.transpose(jnp.broadcast_to(col, (t, HEAD_DIM)))[0:1, :]


def _swap_pairs(x):
    lane = lax.broadcasted_iota(jnp.int32, x.shape, 1)
    return jnp.where(lane % 2 == 0, pltpu.roll(x, HEAD_DIM - 1, 1), pltpu.roll(x, 1, 1))


def rope_tables(n):
    rows = n // GRID_W
    row = jnp.repeat(jnp.arange(rows, dtype=F32), GRID_W)
    col = jnp.tile(jnp.arange(GRID_W, dtype=F32), rows)
    n_freq = HEAD_DIM // 4
    inv = jnp.power(ROPE_THETA, jnp.arange(n_freq, dtype=F32) * (-2.0 / (HEAD_DIM // 2)))
    ang = jnp.concatenate([row[:, None] * inv, col[:, None] * inv], axis=-1)
    cos, sin = jnp.cos(ang), jnp.sin(ang)
    cexp = jnp.repeat(cos, 2, axis=-1)
    sexp = jnp.stack([-sin, sin], axis=-1).reshape(n, HEAD_DIM)
    return cexp, sexp


V_EXT_W = 2 * HEAD_DIM


def qkv_prep_fwd(p, wq, wk, cexp, sexp, *, latent, name, rows_all=None, k_all=None, v_ext=None):
    n = p.shape[0]
    tt = _rows_tile(n, 256)
    width = 2 * KV_W + (ATTN_W if latent else 0)

    def body(*refs):
        if latent:
            p_ref, wq_ref, wk_ref, c_ref, s_ref, q_ref, k_ref, v_ref = refs
        else:
            p_ref, wk_ref, _, _, k_ref, v_ref = refs

        def head(xv, w):
            rstd = lax.rsqrt(jnp.mean(xv * xv, axis=-1, keepdims=True) + EPS)
            yv = xv * rstd * w
            if latent:
                yv = yv * c_ref[...] + _swap_pairs(yv) * s_ref[...]
            return yv

        for h in range(N_KV_HEADS):
            sl = slice(h * HEAD_DIM, (h + 1) * HEAD_DIM)
            k_ref[:, sl] = head(p_ref[:, sl], wk_ref[...]).astype(BF16)
            v_ref[:, h * V_EXT_W:h * V_EXT_W + HEAD_DIM] = p_ref[:, KV_W + h * HEAD_DIM:KV_W + (h + 1) * HEAD_DIM].astype(BF16)
            lane = lax.broadcasted_iota(jnp.int32, (tt, HEAD_DIM), 1)
            v_ref[:, h * V_EXT_W + HEAD_DIM:(h + 1) * V_EXT_W] = jnp.where(lane == 0, 1.0, 0.0).astype(BF16)
        if latent:
            for h in range(N_Q_HEADS):
                sl = slice(2 * KV_W + h * HEAD_DIM, 2 * KV_W + (h + 1) * HEAD_DIM)
                q_ref[:, h * HEAD_DIM:(h + 1) * HEAD_DIM] = (head(p_ref[:, sl], wq_ref[...]) * Q_SCALE).astype(BF16)

    vec = pl.BlockSpec((1, HEAD_DIM), lambda i: (0, 0))
    tab = pl.BlockSpec((tt, HEAD_DIM), lambda i: (i, 0))
    vw = N_KV_HEADS * V_EXT_W
    k_shape = jax.ShapeDtypeStruct((rows_all, KV_W), BF16)
    v_shape = jax.ShapeDtypeStruct((rows_all, vw), BF16)
    aliases = {}
    if latent:
        in_specs = [pl.BlockSpec((tt, width), lambda i: (i, 0)), vec, vec, tab, tab]
        ops = [p, _row(wq), _row(wk), cexp, sexp]
        out_specs = [pl.BlockSpec((tt, ATTN_W), lambda i: (i, 0)), pl.BlockSpec((tt, KV_W), lambda i: (i, 0)),
                     pl.BlockSpec((tt, vw), lambda i: (i, 0))]
        out_shape = [jax.ShapeDtypeStruct((n, ATTN_W), BF16), k_shape, v_shape]
    else:
        assert n == tt and (rows_all - n) % n == 0, (n, tt, rows_all)
        first = (rows_all - n) // n
        in_specs = [pl.BlockSpec((tt, width), lambda i: (i, 0)), vec, ANY, ANY]
        ops = [p, _row(wk), k_all, v_ext]
        out_specs = [pl.BlockSpec((tt, KV_W), lambda i: (first, 0)), pl.BlockSpec((tt, vw), lambda i: (first, 0))]
        out_shape = [k_shape, v_shape]
        aliases = {2: 0, 3: 1}
    return pl.pallas_call(
        body, name=name, grid=(n // tt,), in_specs=in_specs, out_specs=out_specs, out_shape=out_shape,
        input_output_aliases=aliases, compiler_params=_params("parallel"),
    )(*ops)


def qkv_prep_bwd(p, dq, dk, dv, wq, wk, cexp, sexp, *, latent, name):
    n = p.shape[0]
    tt = _rows_tile(n, 256)
    width = 2 * KV_W + (ATTN_W if latent else 0)

    def body(*refs):
        if latent:
            p_ref, dq_ref, dk_ref, dv_ref, wq_ref, wk_ref, c_ref, s_ref, dp_ref, acc_ref = refs
        else:
            p_ref, dk_ref, dv_ref, wk_ref, dp_ref, acc_ref = refs
        i = pl.program_id(0)

        @pl.when(i == 0)
        def _():
            acc_ref[...] = jnp.zeros_like(acc_ref)

        def head(xv, dy, w, row):
            if latent:
                dy = dy * c_ref[...] + _swap_pairs(dy * s_ref[...])
            rstd = lax.rsqrt(jnp.mean(xv * xv, axis=-1, keepdims=True) + EPS)
            xhat = xv * rstd
            acc_ref[row:row + 1, :] += jnp.sum(dy * xhat, axis=0, keepdims=True)
            dxhat = dy * w
            return rstd * (dxhat - xhat * jnp.mean(dxhat * xhat, axis=-1, keepdims=True))

        for h in range(N_KV_HEADS):
            sl = slice(h * HEAD_DIM, (h + 1) * HEAD_DIM)
            dp_ref[:, sl] = head(p_ref[:, sl], dk_ref[:, sl], wk_ref[...], 1).astype(BF16)
        dp_ref[:, KV_W:2 * KV_W] = dv_ref[...].astype(BF16)
        if latent:
            for h in range(N_Q_HEADS):
                sl = slice(2 * KV_W + h * HEAD_DIM, 2 * KV_W + (h + 1) * HEAD_DIM)
                dyq = dq_ref[:, h * HEAD_DIM:(h + 1) * HEAD_DIM] * Q_SCALE
                dp_ref[:, sl] = head(p_ref[:, sl], dyq, wq_ref[...], 0).astype(BF16)

    vec = pl.BlockSpec((1, HEAD_DIM), lambda i: (0, 0))
    tab = pl.BlockSpec((tt, HEAD_DIM), lambda i: (i, 0))
    kv_spec = pl.BlockSpec((tt, KV_W), lambda i: (i, 0))
    p_spec = pl.BlockSpec((tt, width), lambda i: (i, 0))
    if latent:
        in_specs = [p_spec, pl.BlockSpec((tt, ATTN_W), lambda i: (i, 0)), kv_spec, kv_spec, vec, vec, tab, tab]
        ops = [p, dq, dk, dv, _row(wq), _row(wk), cexp, sexp]
    else:
        in_specs = [p_spec, kv_spec, kv_spec, vec]
        ops = [p, dk, dv, _row(wk)]
    return pl.pallas_call(
        body, name=name, grid=(n // tt,), in_specs=in_specs,
        out_specs=[p_spec, pl.BlockSpec((8, HEAD_DIM), lambda i: (0, 0))],
        out_shape=[jax.ShapeDtypeStruct((n, width), BF16), jax.ShapeDtypeStruct((8, HEAD_DIM), F32)],
        compiler_params=_params("arbitrary"),
    )(*ops)


def _kv_chunks(n, s_all):
    step = 1024 if n % 1024 == 0 else 256
    chunks = [(s, step) for s in range(0, n, step)]
    if s_all > n:
        chunks.append((n, s_all - n))
    return chunks


def flash_fwd(q, k_all, v_ext, p, *, za_block, mix_width, name):
    n = q.shape[0]
    s_all = k_all.shape[0]
    tq = _rows_tile(n, 512)
    chunks = _kv_chunks(n, s_all)
    wide = 2 * HEAD_DIM

    def body(q_ref, k_ref, v_ref, za_ref, o_ref, mix_ref, lse_ref, m_ref, acc_ref):
        qv = q_ref[...]
        m_ref[...] = jnp.full_like(m_ref, -jnp.inf)
        acc_ref[...] = jnp.zeros_like(acc_ref)
        for start, size in chunks:
            kc = k_ref[pl.ds(start, size), :]
            vc = v_ref[pl.ds(start, size), :]
            s = lax.dot_general(qv, kc, _DIMS["nt"], preferred_element_type=F32)
            m_old = m_ref[...]
            m_new = jnp.maximum(m_old, jnp.max(s, axis=-1, keepdims=True))
            pr = jnp.exp2(s - m_new)
            alpha = jnp.exp2(m_old - m_new)
            acc_ref[...] = alpha * acc_ref[...] + jnp.dot(pr.astype(BF16), vc, preferred_element_type=F32)
            m_ref[...] = m_new
        acc = acc_ref[...]
        denom = acc[:, HEAD_DIM:HEAD_DIM + 1]
        o = acc[:, :HEAD_DIM] / denom
        o_ref[...] = o.astype(BF16)
        mix_ref[...] = (o * _silu(za_ref[...])).astype(BF16)
        lse_ref[0] = _as_row(m_ref[...] + jnp.log2(denom))

    qspec = pl.BlockSpec((tq, HEAD_DIM), lambda h, i: (i, h))
    return pl.pallas_call(
        body, name=name, grid=(N_Q_HEADS, n // tq),
        in_specs=[qspec, pl.BlockSpec((s_all, HEAD_DIM), lambda h, i: (0, h // Q_PER_KV)),
                  pl.BlockSpec((s_all, wide), lambda h, i: (0, h // Q_PER_KV)),
                  pl.BlockSpec((tq, HEAD_DIM), lambda h, i: (i, za_block + h))],
        out_specs=[qspec, qspec, pl.BlockSpec((1, 1, tq), lambda h, i: (h, 0, i))],
        out_shape=[jax.ShapeDtypeStruct((n, ATTN_W), BF16), jax.ShapeDtypeStruct((n, mix_width), BF16),
                   jax.ShapeDtypeStruct((N_Q_HEADS, 1, n), F32)],
        scratch_shapes=[pltpu.VMEM((tq, 1), F32), pltpu.VMEM((tq, wide), F32)],
        compiler_params=_params("parallel", "parallel"),
    )(q, k_all, v_ext, p)


def attn_gate_bwd(dmix, o, p, *, za_block, name):
    n = o.shape[0]
    tt = _rows_tile(n, 512)
    za_half = za_block * HEAD_DIM // HALF
    n_half = ATTN_W // HALF

    def body(*refs):
        dm_ref, o_ref = refs[:2]
        za_refs = refs[2:2 + n_half]
        do_ref, dos_ref, dza_ref, delta_ref = refs[2 + n_half:]
        dm = dm_ref[...]
        ov = o_ref[...].astype(F32)
        za = jnp.concatenate([r[...] for r in za_refs], axis=-1)
        do = dm * _silu(za)
        do_ref[...] = do.astype(BF16)
        dos_ref[...] = (do * LN2).astype(BF16)
        dza_ref[...] = (dm * ov * _dsilu(za)).astype(BF16)
        prod = do * ov
        for h in range(N_Q_HEADS):
            col = jnp.sum(prod[:, h * HEAD_DIM:(h + 1) * HEAD_DIM], axis=-1, keepdims=True) * LN2
            delta_ref[h] = _as_row(col)

    spec = pl.BlockSpec((tt, ATTN_W), lambda i: (i, 0))
    shape = jax.ShapeDtypeStruct((n, ATTN_W), BF16)
    za_specs = [pl.BlockSpec((tt, HALF), functools.partial(lambda i, cb: (i, cb), cb=za_half + h))
                for h in range(n_half)]
    return pl.pallas_call(
        body, name=name, grid=(n // tt,),
        in_specs=[spec, spec] + za_specs,
        out_specs=[spec, spec, spec, pl.BlockSpec((N_Q_HEADS, 1, tt), lambda i: (0, 0, i))],
        out_shape=[shape, shape, shape, jax.ShapeDtypeStruct((N_Q_HEADS, 1, n), F32)],
        compiler_params=_params("parallel"),
    )(dmix, o, *([p] * n_half))


def flash_bwd(q, do, do_s, lse_row, delta_row, k_all, v_ext, *, name):
    n = q.shape[0]
    s_all = k_all.shape[0]
    tq = _rows_tile(n, 512)
    chunks = _kv_chunks(n, s_all)

    def body(q_ref, do_ref, dos_ref, lse_ref, dl_ref, k_ref, v_ref, dq_ref, dk_ref, dv_ref):
        g = pl.program_id(1)
        i = pl.program_id(2)

        @pl.when((g == 0) & (i == 0))
        def _():
            dk_ref[...] = jnp.zeros_like(dk_ref)
            dv_ref[...] = jnp.zeros_like(dv_ref)

        qv = q_ref[...]
        dov = do_ref[...]
        dosv = dos_ref[...]
        lse = lse_ref[0]
        dl = dl_ref[0]
        dq = jnp.zeros((tq, HEAD_DIM), F32)
        for start, size in chunks:
            kc = k_ref[pl.ds(start, size), :]
            vc = v_ref[pl.ds(start, size), :]
            st = lax.dot_general(kc, qv, _DIMS["nt"], preferred_element_type=F32)
            pt = jnp.exp2(st - lse)
            dpt = lax.dot_general(vc, dosv, _DIMS["nt"], preferred_element_type=F32)
            dst = (pt * (dpt - dl)).astype(BF16)
            dv_ref[pl.ds(start, size), :] += jnp.dot(pt.astype(BF16), dov, preferred_element_type=F32)
            dk_ref[pl.ds(start, size), :] += jnp.dot(dst, qv, preferred_element_type=F32)
            dq = dq + lax.dot_general(dst, kc, _DIMS["tn"], preferred_element_type=F32)
        dq_ref[...] = dq

    qspec = pl.BlockSpec((tq, HEAD_DIM), lambda kh, g, i: (i, kh * Q_PER_KV + g))
    rowspec = pl.BlockSpec((1, 1, tq), lambda kh, g, i: (kh * Q_PER_KV + g, 0, i))
    kvspec = pl.BlockSpec((s_all, HEAD_DIM), lambda kh, g, i: (0, kh))
    return pl.pallas_call(
        body, name=name, grid=(N_KV_HEADS, Q_PER_KV, n // tq),
        in_specs=[qspec, qspec, qspec, rowspec, rowspec, kvspec,
                  pl.BlockSpec((s_all, HEAD_DIM), lambda kh, g, i: (0, kh * (V_EXT_W // HEAD_DIM)))],
        out_specs=[qspec, kvspec, kvspec],
        out_shape=[jax.ShapeDtypeStruct((n, ATTN_W), F32),
                   jax.ShapeDtypeStruct((s_all, KV_W), F32), jax.ShapeDtypeStruct((s_all, KV_W), F32)],
        compiler_params=_params("arbitrary", "arbitrary", "arbitrary"),
    )(q, do, do_s, lse_row, delta_row, k_all, v_ext)


HALF = 512


SUB = 128


def _tap_group(w_ref, w_lanes, ext_ref, r0, lanes, r, tap_of):
    z = None
    for a in range(4):
        k = tap_of(8 * a + r)
        if 0 <= k < CONV_WIDTH:
            term = w_ref[k:k + 1, w_lanes] * ext_ref[pl.ds(r0 + 8 * a, SUB + 8), lanes]
            z = term if z is None else z + term
    return z


def _shifted(z, z_ref, r):
    if r == 0:
        return z[0:SUB]
    z_ref[...] = z
    return z_ref[pl.ds(r, SUB), :]


def _fold8(x):
    return jnp.sum(x.reshape(x.shape[0] // 8, 8, x.shape[1]), axis=0)


def _pieces(tt, h):
    return [(rh * SUB, slice(c * SUB, (c + 1) * SUB), slice(h * HALF + c * SUB, h * HALF + (c + 1) * SUB))
            for c in range(HALF // SUB) for rh in range(tt // SUB)]


def _halo_specs(tt, n, cb):
    per = tt // HALO
    last = n // HALO - 1
    return [pl.BlockSpec((HALO, HALF), lambda i: (jnp.maximum(i * per - 1, 0), cb)),
            pl.BlockSpec((tt, HALF), lambda i: (i, cb)),
            pl.BlockSpec((HALO, HALF), lambda i: (jnp.minimum((i + 1) * per, last), cb))]


def conv_fwd(p, dw_w, dw_b, ln_g, ln_b, mix, *, glu_block, name):
    n = p.shape[0]
    ch = dw_w.shape[1]
    nh = ch // HALF
    tt = _rows_tile(n, 256)
    last = n // tt - 1

    def body(*refs):
        a_refs = [refs[3 * h:3 * h + 3] for h in range(nh)]
        b_refs = [refs[3 * (nh + h):3 * (nh + h) + 3] for h in range(nh)]
        pos = 6 * nh
        zb_refs = refs[pos:pos + nh]
        pos += nh
        w_ref, bias_ref, g_ref, be_ref, _, ycv_ref, mix_ref, ext_ref, z_ref = refs[pos:pos + 9]
        i = pl.program_id(0)
        for h in range(nh):
            cs = slice(h * HALF, (h + 1) * HALF)
            ap, am, an = a_refs[h]
            bp, bm, bn = b_refs[h]
            ext_ref[0:HALO, :] = jnp.where(i > 0, ap[...] * _sigmoid(bp[...]), 0.0)
            ext_ref[HALO:HALO + tt, :] = am[...] * _sigmoid(bm[...])
            ext_ref[HALO + tt:2 * HALO + tt, :] = jnp.where(i < last, an[...] * _sigmoid(bn[...]), 0.0)
            for r0, lanes, w_lanes in _pieces(tt, h):
                acc = jnp.broadcast_to(bias_ref[:, w_lanes], (SUB, SUB))
                for r in range(8):
                    z = _tap_group(w_ref, w_lanes, ext_ref, r0, lanes, r, lambda j: j - 1)
                    acc = acc + _shifted(z, z_ref, r)
                ycv_ref[pl.ds(r0, SUB), w_lanes] = acc
        yc = ycv_ref[...]
        mu = jnp.mean(yc, axis=-1, keepdims=True)
        var = jnp.mean(jnp.square(yc - mu), axis=-1, keepdims=True)
        ln = (yc - mu) * lax.rsqrt(var + EPS) * g_ref[...] + be_ref[...]
        out = _silu(ln)
        for h in range(nh):
            cs = slice(h * HALF, (h + 1) * HALF)
            mix_ref[:, cs] = (out[:, cs] * _silu(zb_refs[h][...])).astype(BF16)

    in_specs = []
    for h in range(2 * nh):
        in_specs += _halo_specs(tt, n, glu_block + h)
    in_specs += [pl.BlockSpec((tt, HALF), functools.partial(lambda i, cb: (i, cb), cb=glu_block + 2 * nh + h))
                 for h in range(nh)]
    vec = pl.BlockSpec((1, ch), lambda i: (0, 0))
    in_specs += [pl.BlockSpec((CONV_WIDTH, ch), lambda i: (0, 0)), vec, vec, vec, ANY]
    ops = [p] * (6 * nh + nh) + [dw_w, _row(dw_b), _row(ln_g), _row(ln_b), mix]
    big = pl.BlockSpec((tt, ch), lambda i: (i, 0))
    mix_block = (mix.shape[1] - ch) // ch
    return pl.pallas_call(
        body, name=name, grid=(n // tt,), in_specs=in_specs,
        out_specs=[big, pl.BlockSpec((tt, ch), lambda i: (i, mix_block))],
        out_shape=[jax.ShapeDtypeStruct((n, ch), F32), jax.ShapeDtypeStruct(mix.shape, BF16)],
        input_output_aliases={len(ops) - 1: 1},
        scratch_shapes=[pltpu.VMEM((tt + 2 * HALO, HALF), F32), pltpu.VMEM((SUB + 8, SUB), F32)],
        compiler_params=_params("parallel"),
    )(*ops)


def conv_bwd_rows(dmix, ycv, p, ln_g, ln_b, *, mix_block, zb_block, name):
    n, ch = ycv.shape
    nh = ch // HALF
    tt = _rows_tile(n, 256)

    def body(*refs):
        dm_ref, ycv_ref = refs[:2]
        zb_refs = refs[2:2 + nh]
        g_ref, be_ref, dy_ref, dzb_ref, acc_ref = refs[2 + nh:]
        i = pl.program_id(0)

        @pl.when(i == 0)
        def _():
            acc_ref[...] = jnp.zeros_like(acc_ref)

        yc = ycv_ref[...]
        mu = jnp.mean(yc, axis=-1, keepdims=True)
        var = jnp.mean(jnp.square(yc - mu), axis=-1, keepdims=True)
        rstd = lax.rsqrt(var + EPS)
        xhat = (yc - mu) * rstd
        ln = xhat * g_ref[...] + be_ref[...]
        out = _silu(ln)
        dm = dm_ref[...]
        zb = jnp.concatenate([r[...] for r in zb_refs], axis=-1)
        dzb_ref[...] = (dm * out * _dsilu(zb)).astype(BF16)
        dln = dm * _silu(zb) * _dsilu(ln)
        acc_ref[0:1, :] += jnp.sum(dln * xhat, axis=0, keepdims=True)
        acc_ref[1:2, :] += jnp.sum(dln, axis=0, keepdims=True)
        dxhat = dln * g_ref[...]
        dy_ref[...] = rstd * (dxhat - jnp.mean(dxhat, axis=-1, keepdims=True)
                              - xhat * jnp.mean(dxhat * xhat, axis=-1, keepdims=True))

    big = pl.BlockSpec((tt, ch), lambda i: (i, 0))
    vec = pl.BlockSpec((1, ch), lambda i: (0, 0))
    in_specs = [pl.BlockSpec((tt, ch), lambda i: (i, mix_block)), big]
    in_specs += [pl.BlockSpec((tt, HALF), functools.partial(lambda i, cb: (i, cb), cb=zb_block + h)) for h in range(nh)]
    in_specs += [vec, vec]
    return pl.pallas_call(
        body, name=name, grid=(n // tt,), in_specs=in_specs,
        out_specs=[big, big, pl.BlockSpec((8, ch), lambda i: (0, 0))],
        out_shape=[jax.ShapeDtypeStruct((n, ch), F32), jax.ShapeDtypeStruct((n, ch), BF16),
                   jax.ShapeDtypeStruct((8, ch), F32)],
        compiler_params=_params("arbitrary"),
    )(dmix, ycv, *([p] * nh), _row(ln_g), _row(ln_b))


def conv_bwd_taps(dycv, p, dw_w, *, glu_block, name):
    n, ch = dycv.shape
    nh = ch // HALF
    tt = _rows_tile(n, 256)
    last = n // tt - 1

    def body(*refs):
        d_refs = [refs[3 * h:3 * h + 3] for h in range(nh)]
        a_refs = [refs[3 * (nh + h):3 * (nh + h) + 3] for h in range(nh)]
        b_refs = [refs[3 * (2 * nh + h):3 * (2 * nh + h) + 3] for h in range(nh)]
        w_ref, dglu_ref, dw_ref, db_ref, yext_ref, dext_ref, z_ref, dy_ref, dwp_ref, dbp_ref = refs[9 * nh:]
        i = pl.program_id(0)

        @pl.when(i == 0)
        def _():
            dwp_ref[...] = jnp.zeros_like(dwp_ref)
            dbp_ref[...] = jnp.zeros_like(dbp_ref)

        for h in range(nh):
            cs = slice(h * HALF, (h + 1) * HALF)
            ap, am, an = a_refs[h]
            bp, bm, bn = b_refs[h]
            dp, dm, dn = d_refs[h]
            av = am[...]
            sb = _sigmoid(bm[...])
            yext_ref[0:HALO, :] = jnp.where(i > 0, ap[...] * _sigmoid(bp[...]), 0.0)
            yext_ref[HALO:HALO + tt, :] = av * sb
            yext_ref[HALO + tt:2 * HALO + tt, :] = jnp.where(i < last, an[...] * _sigmoid(bn[...]), 0.0)
            dmain = dm[...]
            dext_ref[0:HALO, :] = jnp.where(i > 0, dp[...], 0.0)
            dext_ref[HALO:HALO + tt, :] = dmain
            dext_ref[HALO + tt:2 * HALO + tt, :] = jnp.where(i < last, dn[...], 0.0)
            for r0, lanes, w_lanes in _pieces(tt, h):
                dsub = dext_ref[pl.ds(HALO + r0, SUB), lanes]
                dy = jnp.zeros((SUB, SUB), F32)
                for r in range(8):
                    z = _tap_group(w_ref, w_lanes, dext_ref, r0, lanes, r, lambda j: CONV_WIDTH - j)
                    dy = dy + _shifted(z, z_ref, r)
                    y_r = yext_ref[pl.ds(r0 + r, SUB + 24), lanes]
                    for a in range(4):
                        k = 8 * a + r - 1
                        if k >= 0:
                            dwp_ref[8 * k:8 * k + 8, w_lanes] += _fold8(dsub * y_r[8 * a:8 * a + SUB])
                dbp_ref[:, w_lanes] += _fold8(dsub)
                dy_ref[pl.ds(r0, SUB), lanes] = dy
            dy = dy_ref[...]
            dglu_ref[:, cs] = (dy * sb).astype(BF16)
            dglu_ref[:, ch + h * HALF:ch + (h + 1) * HALF] = (dy * av * sb * (1.0 - sb)).astype(BF16)

        @pl.when(i == last)
        def _():
            dw_ref[...] = jnp.sum(dwp_ref[...].reshape(32, 8, ch), axis=1)
            db_ref[...] = jnp.broadcast_to(jnp.sum(dbp_ref[...], axis=0, keepdims=True), db_ref.shape)

    in_specs = []
    for h in range(nh):
        in_specs += _halo_specs(tt, n, h)
    for h in range(2 * nh):
        in_specs += _halo_specs(tt, n, glu_block + h)
    in_specs += [pl.BlockSpec((CONV_WIDTH, ch), lambda i: (0, 0))]
    ops = [dycv] * (3 * nh) + [p] * (6 * nh) + [dw_w]
    return pl.pallas_call(
        body, name=name, grid=(n // tt,), in_specs=in_specs,
        out_specs=[pl.BlockSpec((tt, 2 * ch), lambda i: (i, 0)), pl.BlockSpec((32, ch), lambda i: (0, 0)),
                   pl.BlockSpec((8, ch), lambda i: (0, 0))],
        out_shape=[jax.ShapeDtypeStruct((n, 2 * ch), BF16), jax.ShapeDtypeStruct((32, ch), F32),
                   jax.ShapeDtypeStruct((8, ch), F32)],
        scratch_shapes=[pltpu.VMEM((tt + 2 * HALO, HALF), F32), pltpu.VMEM((tt + 2 * HALO, HALF), F32),
                        pltpu.VMEM((SUB + 8, SUB), F32), pltpu.VMEM((tt, HALF), F32),
                        pltpu.VMEM((8 * 32, ch), F32), pltpu.VMEM((8, ch), F32)],
        compiler_params=_params("arbitrary"),
    )(*ops)


def _sgu_common(p_ref, g_ref, be_ref, ws_ref, bs_ref, w):
    gw = w // SGU_GROUPS
    u_pre = p_ref[:, 0:w]
    v_pre = p_ref[:, w:2 * w]
    zc = p_ref[:, 2 * w:3 * w]
    u = _gelu(u_pre)
    v = _gelu(v_pre)
    mu = jnp.mean(v, axis=-1, keepdims=True)
    var = jnp.mean(jnp.square(v - mu), axis=-1, keepdims=True)
    rstd = lax.rsqrt(var + EPS)
    vhat = (v - mu) * rstd
    vn = (vhat * g_ref[...] + be_ref[...]).astype(BF16)
    mixed = jnp.concatenate(
        [jnp.dot(ws_ref[gi].astype(BF16), vn[:, gi * gw:(gi + 1) * gw], preferred_element_type=F32)
         + bs_ref[:, gi:gi + 1] for gi in range(SGU_GROUPS)], axis=-1)
    return u_pre, v_pre, zc, u, rstd, vhat, vn, mixed


def sgu_fwd(p, ln_g, ln_b, ws, bs_t, *, name):
    n, w3 = p.shape
    w = w3 // 3

    def body(p_ref, g_ref, be_ref, ws_ref, bs_ref, m_ref):
        _, _, zc, u, _, _, _, mixed = _sgu_common(p_ref, g_ref, be_ref, ws_ref, bs_ref, w)
        m_ref[...] = (u * mixed * _silu(zc)).astype(BF16)

    vec = pl.BlockSpec((1, w), lambda i: (0, 0))
    return pl.pallas_call(
        body, name=name, grid=(n // CHUNK,),
        in_specs=[pl.BlockSpec((CHUNK, w3), lambda i: (i, 0)), vec, vec,
                  pl.BlockSpec((SGU_GROUPS, CHUNK, CHUNK), lambda i: (0, 0, 0)),
                  pl.BlockSpec((CHUNK, SGU_GROUPS), lambda i: (0, 0))],
        out_specs=pl.BlockSpec((CHUNK, w), lambda i: (i, 0)),
        out_shape=jax.ShapeDtypeStruct((n, w), BF16),
        compiler_params=_params("parallel"),
    )(p, _row(ln_g), _row(ln_b), ws, bs_t)


def sgu_bwd(p, dm, ln_g, ln_b, ws, ws_t, bs_t, *, name):
    n, w3 = p.shape
    w = w3 // 3
    gw = w // SGU_GROUPS

    def body(p_ref, dm_ref, g_ref, be_ref, ws_ref, wst_ref, bs_ref, dp_ref, dws_ref, dbs_ref, acc_ref):
        i = pl.program_id(0)

        @pl.when(i == 0)
        def _():
            dws_ref[...] = jnp.zeros_like(dws_ref)
            dbs_ref[...] = jnp.zeros_like(dbs_ref)
            acc_ref[...] = jnp.zeros_like(acc_ref)

        u_pre, v_pre, zc, u, rstd, vhat, vn, mixed = _sgu_common(p_ref, g_ref, be_ref, ws_ref, bs_ref, w)
        dmv = dm_ref[...]
        um = u * mixed
        dp_ref[:, 2 * w:3 * w] = (dmv * um * _dsilu(zc)).astype(BF16)
        dum = dmv * _silu(zc)
        dp_ref[:, 0:w] = (dum * mixed * _dgelu(u_pre)).astype(BF16)
        dmixed = dum * u
        dmixed_b = dmixed.astype(BF16)
        dvn_parts = []
        for gi in range(SGU_GROUPS):
            cs = slice(gi * gw, (gi + 1) * gw)
            dws_ref[gi] += lax.dot_general(dmixed_b[:, cs], vn[:, cs], _DIMS["nt"], preferred_element_type=F32)
            dbs_ref[:, gi:gi + 1] += jnp.sum(dmixed[:, cs], axis=-1, keepdims=True)
            dvn_parts.append(jnp.dot(wst_ref[gi].astype(BF16), dmixed_b[:, cs], preferred_element_type=F32))
        dvn = jnp.concatenate(dvn_parts, axis=-1)
        acc_ref[0:1, :] += jnp.sum(dvn * vhat, axis=0, keepdims=True)
        acc_ref[1:2, :] += jnp.sum(dvn, axis=0, keepdims=True)
        dvhat = dvn * g_ref[...]
        dv = rstd * (dvhat - jnp.mean(dvhat, axis=-1, keepdims=True)
                     - vhat * jnp.mean(dvhat * vhat, axis=-1, keepdims=True))
        dp_ref[:, w:2 * w] = (dv * _dgelu(v_pre)).astype(BF16)

    vec = pl.BlockSpec((1, w), lambda i: (0, 0))
    wspec = pl.BlockSpec((SGU_GROUPS, CHUNK, CHUNK), lambda i: (0, 0, 0))
    bspec = pl.BlockSpec((CHUNK, SGU_GROUPS), lambda i: (0, 0))
    return pl.pallas_call(
        body, name=name, grid=(n // CHUNK,),
        in_specs=[pl.BlockSpec((CHUNK, w3), lambda i: (i, 0)), pl.BlockSpec((CHUNK, w), lambda i: (i, 0)),
                  vec, vec, wspec, wspec, bspec],
        out_specs=[pl.BlockSpec((CHUNK, w3), lambda i: (i, 0)), wspec, bspec,
                   pl.BlockSpec((8, w), lambda i: (0, 0))],
        out_shape=[jax.ShapeDtypeStruct((n, w3), BF16), jax.ShapeDtypeStruct((SGU_GROUPS, CHUNK, CHUNK), F32),
                   jax.ShapeDtypeStruct((CHUNK, SGU_GROUPS), F32), jax.ShapeDtypeStruct((8, w), F32)],
        compiler_params=_params("arbitrary"),
    )(p, dm, _row(ln_g), _row(ln_b), ws, ws_t, bs_t)


def _adam_math(w, g, m, v):
    m_new = ADAM_B1 * m + (1.0 - ADAM_B1) * g
    v_new = ADAM_B2 * v + (1.0 - ADAM_B2) * (g * g)
    m_hat = m_new / (1.0 - ADAM_B1 ** ADAM_STEP)
    v_hat = v_new / (1.0 - ADAM_B2 ** ADAM_STEP)
    delta = -ADAM_LR * (m_hat / (jnp.sqrt(v_hat) + ADAM_EPS) + ADAM_WD * w)
    return delta, m_new, v_new


def adamw(w, g, m, v, *, name, slots=False, rows=512):
    r, c = w.shape
    tr = min(r, rows)
    assert r % tr == 0, (name, r, tr)

    def body(w_ref, g_ref, m_ref, v_ref, go_ref, d_ref, mo_ref, vo_ref):
        if slots:
            g = g_ref[0].astype(F32)
            for k in range(1, N_DEV):
                g = g + g_ref[k].astype(F32)
        else:
            g = g_ref[...].astype(F32)
        delta, m_new, v_new = _adam_math(w_ref[...], g, m_ref[...], v_ref[...])
        go_ref[...] = g
        d_ref[...] = delta
        mo_ref[...] = m_new
        vo_ref[...] = v_new

    spec = pl.BlockSpec((tr, c), lambda i: (i, 0))
    gspec = pl.BlockSpec((N_DEV, tr, c), lambda i: (0, i, 0)) if slots else spec
    shape = jax.ShapeDtypeStruct((r, c), F32)
    return pl.pallas_call(
        body, name=name, grid=(r // tr,),
        in_specs=[spec, gspec, spec, spec], out_specs=[spec] * 4, out_shape=[shape] * 4,
        compiler_params=_params("parallel"),
    )(w, g, m, v)


def slot_sum(slots, *, name):
    _, r, c = slots.shape

    def body(s_ref, o_ref):
        acc = s_ref[0]
        for k in range(1, N_DEV):
            acc = acc + s_ref[k]
        o_ref[...] = acc

    return pl.pallas_call(
        body, name=name, out_shape=jax.ShapeDtypeStruct((r, c), F32),
        in_specs=[VMEM_SPEC], out_specs=VMEM_SPEC,
        compiler_params=pltpu.CompilerParams(vmem_limit_bytes=VMEM_LIMIT),
    )(slots)


def _after(token, val):
    return val + token[0, 0]


def _zero_of(v):
    bits = lax.bitcast_convert_type(v, jnp.uint16 if v.dtype == BF16 else jnp.uint32)
    return jnp.where((bits | 1) == 0, 1.0, 0.0).astype(F32)


def _pad_rows(a, rows):
    return jnp.pad(a, ((0, rows - a.shape[0]), (0, 0)))


def kernel(x, c, ctx, c_ctx, ada_w, ada_b, norm_g, ev_w_in, ev_q_norm, ev_k_norm, ev_dw_w, ev_dw_b, ev_ln_g, ev_ln_b, ev_w_out, od_w_in, od_ln_g, od_ln_b, od_ws, od_bs, od_w_out, final_g, loss_target, m_c_ctx, m_ada_w, m_ada_b, m_norm_g, m_ev_w_in, m_ev_q_norm, m_ev_k_norm, m_ev_dw_w, m_ev_dw_b, m_ev_ln_g, m_ev_ln_b, m_ev_w_out, m_od_w_in, m_od_ln_g, m_od_ln_b, m_od_ws, m_od_bs, m_od_w_out, m_final_g, v_c_ctx, v_ada_w, v_ada_b, v_norm_g, v_ev_w_in, v_ev_q_norm, v_ev_k_norm, v_ev_dw_w, v_ev_dw_b, v_ev_ln_g, v_ev_ln_b, v_ev_w_out, v_od_w_in, v_od_ln_g, v_od_ln_b, v_od_ws, v_od_bs, v_od_w_out, v_final_g):
    n, d = x.shape[1], x.shape[2]
    lc = ctx.shape[1]
    ev_in = ev_w_in.shape[2] * N_DEV
    od_in = od_w_in.shape[2] * N_DEV
    conv_ch = ev_dw_w.shape[2] * N_DEV
    ada_cols = ada_w.shape[2]
    me = 4 * lax.axis_index("x") + 2 * lax.axis_index("y") + lax.axis_index("c")
    xs, tgt, ctxs = x[0], loss_target[0], ctx[0]
    za_block = (2 * KV_W + ATTN_W) // HEAD_DIM
    glu_block = (2 * KV_W + 2 * ATTN_W) // HALF
    zb_block = glu_block + 2 * conv_ch // HALF

    small = jnp.concatenate([
        jax.nn.silu(c).reshape(1, d),
        od_ln_g.reshape(1, -1), od_ln_b.reshape(1, -1)], axis=1)
    small = _pad_rows(small, 8)
    dw_rows = _pad_rows(ev_dw_w[0], 32)
    small_g, dw_g = all_gather([small, dw_rows], name="gather_small", hbm=False)
    sc_all = small_g[:, 0, :d]
    shard = d // N_DEV
    od_ln_g_full = small_g[:, 0, d:d + shard].reshape(d)
    od_ln_b_full = small_g[:, 0, d + shard:d + 2 * shard].reshape(d)
    dw_w_full = jnp.moveaxis(dw_g, 0, 1).reshape(32, conv_ch)[:CONV_WIDTH]
    scc = jax.nn.silu(c_ctx)
    sc16 = _pad_rows(jnp.concatenate([sc_all, scc.reshape(1, d)], axis=0), 16)

    ada_bf = ada_w.astype(BF16)
    mod_loc = [matmul(sc16, ada_bf[l], mode="nn", tm=16, tn=ada_cols, tk=d, out_dtype=F32, name=f"ada_mod{l}")
               for l in range(2)]
    (mod_g,) = all_gather([jnp.stack(mod_loc)], name="gather_mod", hbm=False)
    mod_all = jnp.moveaxis(mod_g, 0, 2).reshape(2, 16, N_DEV * ada_cols) + ada_b[:, None, :]
    mod_me = lax.dynamic_index_in_dim(mod_all, me, axis=1, keepdims=False)
    shift = [mod_me[l, :d] for l in range(2)]
    scale = [mod_me[l, d:2 * d] for l in range(2)]
    gate = [mod_me[l, 2 * d:] for l in range(2)]
    shift_c, scale_c = mod_all[0, 8, :d], mod_all[0, 8, d:2 * d]

    h_w_in0, near_token = staged_gather_start(1, (ev_w_in[0] + _zero_of(mod_g[0, 0, 0, 0])).astype(BF16), None,
                                              name="gather_w_in0_chips_start")

    cexp, sexp = rope_tables(n)
    h0 = norm_mod_fwd(xs, norm_g[0], _after(near_token, shift[0]), scale[0], name="norm_mod_fwd0")
    hc = norm_mod_fwd(ctxs, norm_g[0], shift_c, scale_c, name="norm_mod_fwd_ctx")

    wi0_shard, wi0_land = staged_gather_wait(1, h_w_in0, h0, name="gather_w_in0_chips_wait")
    h_w_in0, far_token = staged_gather_start(2, wi0_shard, wi0_land, name="gather_w_in0_cores_start")
    wi0_shard, wi0_land = staged_gather_wait(2, h_w_in0, far_token, name="gather_w_in0_cores_wait")
    wi0_g = lax.dynamic_update_slice_in_dim(wi0_land, wi0_shard[None], me, axis=0)
    wi0 = jnp.moveaxis(wi0_g, 0, 1).reshape(d, ev_in)
    landed_zero = _zero_of(wi0_g[0, 0, 0])
    later = [(w[0] + landed_zero).astype(BF16) for w in (ev_w_out, od_w_in, od_w_out)]
    (h_wo0, h_wi1, h_wo1), w_token = exchange_start(later, scatter=False, name="gather_rest_start")

    def landed(handle, after, name):
        own, land = exchange_wait(handle, after, scatter=False, name=name)
        return lax.dynamic_update_slice_in_dim(land, own[None], me, axis=0)

    p0 = matmul(h0, wi0, mode="nn", tm=1024, tn=ev_in // 4, tk=d, out_dtype=F32, name="proj_in0", after=w_token)
    pc = matmul(hc, wi0, mode="nn", tm=lc, tn=2 * KV_W, tk=d, out_dtype=F32, n_out=2 * KV_W, name="proj_in_ctx")
    q_r, k_all, v_ext = qkv_prep_fwd(p0, ev_q_norm[0], ev_k_norm[0], cexp, sexp, latent=True, name="qkv_prep",
                                     rows_all=n + lc)
    k_all, v_ext = qkv_prep_fwd(pc, None, ev_k_norm[0], None, None, latent=False, name="kv_prep_ctx",
                                rows_all=n + lc, k_all=k_all, v_ext=v_ext)
    o_attn, mix_a, lse = flash_fwd(q_r, k_all, v_ext, p0, za_block=za_block, mix_width=ATTN_W + conv_ch,
                                   name="flash_fwd")
    ycv, mix0 = conv_fwd(p0, dw_w_full, ev_dw_b[0], ev_ln_g[0], ev_ln_b[0], mix_a, glu_block=glu_block,
                         name="conv_fwd")
    wo0 = landed(h_wo0, ycv, "gather_w_out0_wait").reshape(-1, d)
    x1, r0, h1 = matmul(mix0, wo0, mode="nn", tm=512, tn=d, tk=mix0.shape[1], out_dtype=F32, name="proj_out0",
                        res=xs, gate=_row(gate[0]), norm=(norm_g[1], shift[1], scale[1]))

    wi1 =jnp.moveaxis(landed(h_wi1, h1, "gather_w_in1_wait"), 0, 1).reshape(d, od_in)
    p1 = matmul(h1, wi1, mode="nn", tm=1024, tn=od_in // 4, tk=d, out_dtype=F32, name="proj_in1")
    ws_bf = od_ws[0]
    bs_t = od_bs[0].T
    m1 = sgu_fwd(p1, od_ln_g_full, od_ln_b_full, ws_bf, bs_t, name="sgu_fwd")
    wo1 = landed(h_wo1, m1, "gather_w_out1_wait").reshape(-1, d)
    x2, r1 = matmul(m1, wo1, mode="nn", tm=1024, tn=1024, tk=m1.shape[1], out_dtype=F32, name="proj_out1",
                    res=x1, gate=_row(gate[1]))

    dx2, dr1, acc_final, loss_tile = final_loss(x2, tgt, final_g, r1, gate[1], name="final_loss")

    dm1 = matmul(dr1, wo1, mode="nt", tm=1024, tn=1024, tk=d, out_dtype=F32, name="d_mix1")
    dwo1 = matmul(m1, dr1, mode="tn", tm=1024, tn=1024, tk=2048, out_dtype=BF16, name="d_wout1")
    dp1, dws, dbs_t, acc_sgu = sgu_bwd(p1, dm1, od_ln_g_full, od_ln_b_full, ws_bf, jnp.swapaxes(ws_bf, 1, 2), bs_t,
                                       name="sgu_bwd")
    dwi1_s = matmul(h1, dp1, mode="tn", tm=2048, tn=od_in // N_DEV, tk=2048, out_dtype=BF16, name="d_win1",
                    split_out=True)
    dwo1_s = dwo1.reshape(N_DEV, -1, d)
    (h_gi1, h_go1), g1_token = exchange_start([dwi1_s, dwo1_s], scatter=True, name="grads1_start")
    dh1 = matmul(dp1, wi1, mode="nt", tm=1024, tn=512, tk=od_in, out_dtype=F32, name="d_h1")
    dx1, dr0, acc_norm1 = norm_mod_bwd(x1, dh1, dx2, norm_g[1], _after(g1_token, scale[1]), name="norm_mod_bwd1",
                                       branch=(r0, gate[0]))

    dmix0 = matmul(dr0, wo0, mode="nt", tm=1024, tn=1024, tk=d, out_dtype=F32, name="d_mix0")
    dwo0 = matmul(mix0, dr0, mode="tn", tm=1024, tn=1024, tk=2048, out_dtype=BF16, name="d_wout0")
    do_attn, do_s, dza, delta = attn_gate_bwd(dmix0, o_attn, p0, za_block=za_block, name="attn_gate_bwd")
    dycv, dzb, acc_ln = conv_bwd_rows(dmix0, ycv, p0, ev_ln_g[0], ev_ln_b[0], mix_block=ATTN_W // conv_ch,
                                      zb_block=zb_block, name="conv_bwd_rows")
    dglu, ddw_w, acc_dwb = conv_bwd_taps(dycv, p0, dw_w_full, glu_block=glu_block, name="conv_bwd_taps")
    dq_r, dk_all, dv_all = flash_bwd(q_r, do_attn, do_s, lse, delta, k_all, v_ext, name="flash_bwd")
    dkvq, acc_qk = qkv_prep_bwd(p0, dq_r, dk_all[:n], dv_all[:n], ev_q_norm[0], ev_k_norm[0], cexp, sexp,
                                latent=True, name="qkv_prep_bwd")
    dpc, acc_kc = qkv_prep_bwd(pc, None, dk_all[n:], dv_all[n:], None, ev_k_norm[0], None, None,
                               latent=False, name="kv_prep_ctx_bwd")
    dp0 = jnp.concatenate([dkvq, dza, dglu, dzb], axis=1)
    dwi0 = matmul(h0, dp0, mode="tn", tm=1024, tn=ev_in // 4, tk=2048, out_dtype=F32, name="d_win0")
    dwi0 = matmul(hc, dpc, mode="tn", tm=512, tn=2 * KV_W, tk=lc, out_dtype=F32, name="d_win0_ctx", add=dwi0)
    dwi0_s = jnp.moveaxis(dwi0.astype(BF16).reshape(d, N_DEV, ev_in // N_DEV), 1, 0)
    dwo0_s = dwo0.reshape(N_DEV, -1, d)
    (h_gi0, h_go0), g0_token = exchange_start([dwi0_s, dwo0_s], scatter=True, name="grads0_start")
    dh0 = matmul(dp0, wi0, mode="nt", tm=1024, tn=512, tk=ev_in, out_dtype=F32, name="d_h0", after=g0_token)
    dhc = matmul(dpc, wi0, mode="nt", tm=lc, tn=512, tk=2 * KV_W, out_dtype=F32, name="d_h_ctx")
    grad_x, acc_norm0 = norm_mod_bwd(xs, dh0, dx1, norm_g[0], _after(g0_token, scale[0]), name="norm_mod_bwd0")
    _, acc_normc = norm_mod_bwd(ctxs, dhc, None, norm_g[0], scale_c, name="norm_mod_bwd_ctx")

    zeros_d = jnp.zeros((d,), F32)
    dmod0 = jnp.stack([acc_norm0[0], acc_norm0[1], acc_norm1[GATE_ROW]])
    dmod1 = jnp.stack([acc_norm1[0], acc_norm1[1], acc_final[GATE_ROW]])
    dmodc = jnp.stack([acc_normc[0], acc_normc[1]])
    half_pad = jnp.zeros((d - 2 * conv_ch,), F32) if d > 2 * conv_ch else jnp.zeros((0,), F32)
    row_a = jnp.concatenate([acc_dwb[0], acc_ln[0], half_pad])
    row_b = jnp.concatenate([acc_ln[1], acc_qk[0], acc_qk[1] + acc_kc[1],
                             jnp.zeros((d - conv_ch - 2 * HEAD_DIM,), F32)])
    row_c = jnp.concatenate([dbs_t.T.reshape(-1), jnp.zeros((d - SGU_GROUPS * CHUNK,), F32)])
    row_loss = jnp.concatenate([loss_tile[0, :1], jnp.zeros((d - 1,), F32)])
    pack = jnp.concatenate([
        dmod0, dmod1, dmodc,
        (acc_norm0[2] + acc_normc[2])[None], acc_norm1[2][None],
        acc_final[0][None],
        acc_sgu[0][None], acc_sgu[1][None],
        row_a[None], row_b[None], row_c[None], row_loss[None],
        ddw_w.reshape(-1, d),
        dws.reshape(-1, d),
    ], axis=0)
    n_rows = pack.shape[0]
    pack = _pad_rows(pack, -(-n_rows // 8) * 8)
    (h_pack,), pack_token = exchange_start([pack], scatter=False, name="small_grads_start")

    def summands(handle, after, name):
        mine, land = exchange_wait(handle, after, scatter=True, name=name)
        own = lax.dynamic_index_in_dim(mine, me, axis=0, keepdims=True)
        return lax.dynamic_update_slice_in_dim(land, own, me, axis=0)

    out = {}

    def upd(key, w, g, m, v, slots=False, rows=512):
        shp = w.shape
        w2 = w.reshape(-1, shp[-1])
        g2 = g.reshape((N_DEV, -1, shp[-1])) if slots else g.reshape(-1, shp[-1])
        res = adamw(w2, g2, m.reshape(w2.shape), v.reshape(w2.shape), name="adamw_" + key, slots=slots, rows=rows)
        out[key] = tuple(r.reshape(shp) for r in res)

    gi1 = summands(h_gi1, pack_token, "grads_w_in1_wait")
    upd("od_w_in", od_w_in, gi1, m_od_w_in, v_od_w_in, slots=True, rows=256)
    go1 = summands(h_go1, out["od_w_in"][1], "grads_w_out1_wait")
    upd("od_w_out", od_w_out, go1, m_od_w_out, v_od_w_out, slots=True, rows=256)
    go0 = summands(h_go0, out["od_w_out"][1], "grads_w_out0_wait")
    upd("ev_w_out", ev_w_out, go0, m_ev_w_out, v_ev_w_out, slots=True, rows=256)
    gi0 = summands(h_gi0, out["ev_w_out"][1], "grads_w_in0_wait")
    upd("ev_w_in", ev_w_in, gi0, m_ev_w_in, v_ev_w_in, slots=True, rows=256)

    pack_own, pack_land = exchange_wait(h_pack, out["ev_w_in"][1], scatter=False, name="small_grads_wait")
    pack_g = lax.dynamic_update_slice_in_dim(pack_land, pack_own[None], me, axis=0)
    gsum = slot_sum(pack_g, name="sum_small_grads")
    loss = gsum[16, 0]
    dw_rows_n = 32 * conv_ch // d
    g_dw_w = gsum[17:17 + dw_rows_n].reshape(32, conv_ch)[:CONV_WIDTH]
    g_od_ws = gsum[17 + dw_rows_n:17 + dw_rows_n + SGU_GROUPS * CHUNK * CHUNK // d].reshape(od_ws.shape)

    dmodc_sum = jnp.concatenate([gsum[6], gsum[7], zeros_d])
    col0 = me * ada_cols
    dm_cols = []
    for l in range(2):
        rows = pack_g[:, 3 * l:3 * l + 3, :].reshape(N_DEV, 3 * d)
        extra = dmodc_sum[None] if l == 0 else jnp.zeros((1, 3 * d), F32)
        full = _pad_rows(jnp.concatenate([rows, extra], axis=0), 16)
        dm_cols.append(lax.dynamic_slice_in_dim(full, col0, ada_cols, axis=1))
    g_ada_w = jnp.stack([matmul(sc16, dm_cols[l], mode="tn", tm=512, tn=ada_cols, tk=16, out_dtype=F32,
                                name=f"d_ada_w{l}") for l in range(2)])
    dsc = matmul(dm_cols[0], ada_bf[0], mode="nt", tm=16, tn=512, tk=ada_cols, out_dtype=F32, name="d_scc")
    (_, dscc_sum) = all_gather([dsc[8:16]], name="gather_dscc", hbm=False, sum_out=True)
    sg = jax.nn.sigmoid(c_ctx)
    g_c_ctx = dscc_sum[0] * (sg * (1.0 + c_ctx * (1.0 - sg)))
    g_ada_b = jnp.stack([gsum[0:3].reshape(-1) + dmodc_sum, gsum[3:6].reshape(-1)])

    upd("ada_w", ada_w, g_ada_w, m_ada_w, v_ada_w)

    def my_shard(full, size):
        return lax.dynamic_slice_in_dim(full, me * size, size, axis=full.ndim - 1)

    small_items = [
        ("c_ctx", c_ctx, g_c_ctx, m_c_ctx, v_c_ctx),
        ("ada_b", ada_b, g_ada_b, m_ada_b, v_ada_b),
        ("norm_g", norm_g, gsum[8:10], m_norm_g, v_norm_g),
        ("ev_q_norm", ev_q_norm, gsum[14, conv_ch:conv_ch + HEAD_DIM], m_ev_q_norm, v_ev_q_norm),
        ("ev_k_norm", ev_k_norm, gsum[14, conv_ch + HEAD_DIM:conv_ch + 2 * HEAD_DIM], m_ev_k_norm, v_ev_k_norm),
        ("ev_dw_w", ev_dw_w, my_shard(g_dw_w, conv_ch // N_DEV), m_ev_dw_w, v_ev_dw_w),
        ("ev_dw_b", ev_dw_b, gsum[13, :conv_ch], m_ev_dw_b, v_ev_dw_b),
        ("ev_ln_g", ev_ln_g, gsum[13, conv_ch:2 * conv_ch], m_ev_ln_g, v_ev_ln_g),
        ("ev_ln_b", ev_ln_b, gsum[14, :conv_ch], m_ev_ln_b, v_ev_ln_b),
        ("od_ln_g", od_ln_g, my_shard(gsum[11], shard), m_od_ln_g, v_od_ln_g),
        ("od_ln_b", od_ln_b, my_shard(gsum[12], shard), m_od_ln_b, v_od_ln_b),
        ("od_ws", od_ws, g_od_ws, m_od_ws, v_od_ws),
        ("od_bs", od_bs, gsum[15, :SGU_GROUPS * CHUNK], m_od_bs, v_od_bs),
        ("final_g", final_g, gsum[10], m_final_g, v_final_g),
    ]
    sizes = [it[1].size for it in small_items]
    total = sum(sizes)
    lanes = 1024
    prow = -(-total // lanes)
    prow = -(-prow // 8) * 8

    def pack_small(idx):
        flat = jnp.concatenate([it[idx].reshape(-1).astype(F32) for it in small_items])
        return jnp.pad(flat, (0, prow * lanes - total)).reshape(prow, lanes)

    sres = adamw(pack_small(1), pack_small(2), pack_small(3), pack_small(4), name="adamw_small", rows=prow)
    off = 0
    for it, size in zip(small_items, sizes):
        out[it[0]] = tuple(r.reshape(-1)[off:off + size].reshape(it[1].shape) for r in sres)
        off += size

    names = ['c_ctx', 'ada_w', 'ada_b', 'norm_g', 'ev_w_in', 'ev_q_norm', 'ev_k_norm', 'ev_dw_w', 'ev_dw_b',
             'ev_ln_g', 'ev_ln_b', 'ev_w_out', 'od_w_in', 'od_ln_g', 'od_ln_b', 'od_ws', 'od_bs', 'od_w_out',
             'final_g']
    return (loss, grad_x[None], *[out[k][0] for k in names], *[out[k][1] for k in names],
            *[out[k][2] for k in names], *[out[k][3] for k in names])
```

```python
import functools
import math

import jax
import jax.numpy as jnp
from jax import lax
from jax.experimental import pallas as pl
from jax.experimental.pallas import tpu as pltpu

F32 = jnp.float32
BF16 = jnp.bfloat16
MESH = pl.DeviceIdType.MESH

EPS = 1e-6
HEAD_DIM = 128
N_Q_HEADS = 8
N_KV_HEADS = 2
Q_PER_KV = N_Q_HEADS // N_KV_HEADS
ATTN_W = N_Q_HEADS * HEAD_DIM
KV_W = N_KV_HEADS * HEAD_DIM
ATTN_SCALE = HEAD_DIM ** -0.5
LN2 = math.log(2.0)
Q_SCALE = ATTN_SCALE / LN2
ROPE_THETA = 10000.0
GRID_W = 64
CONV_WIDTH = 31
CONV_HALF = CONV_WIDTH // 2
HALO = 16
CHUNK = 128
SGU_GROUPS = 8
N_DEV = 8

ADAM_LR = 0.001
ADAM_B1 = 0.9
ADAM_B2 = 0.999
ADAM_EPS = 1e-08
ADAM_WD = 0.01
ADAM_STEP = 10

VMEM_LIMIT = 56 * 1024 * 1024
ANY = pl.BlockSpec(memory_space=pl.ANY)
VMEM_SPEC = pl.BlockSpec(memory_space=pltpu.VMEM)


def _params(*sem):
    return pltpu.CompilerParams(dimension_semantics=sem, vmem_limit_bytes=VMEM_LIMIT)


def _sigmoid(x):
    return 1.0 / (1.0 + jnp.exp(-x))


def _silu(x):
    return x * _sigmoid(x)


def _dsilu(x):
    s = _sigmoid(x)
    return s * (1.0 + x * (1.0 - s))


_GELU_C = math.sqrt(2.0 / math.pi)


def _gelu(x):
    t = jnp.tanh(_GELU_C * (x + 0.044715 * (x * x * x)))
    return 0.5 * x * (1.0 + t)


def _dgelu(x):
    t = jnp.tanh(_GELU_C * (x + 0.044715 * (x * x * x)))
    return 0.5 * (1.0 + t) + 0.5 * x * (1.0 - t * t) * (_GELU_C * (1.0 + 3.0 * 0.044715 * (x * x)))


def _row(v):
    return v.reshape(1, -1).astype(F32)


def _flat_id(p):
    return 4 * p[0] + 2 * p[1] + p[2]


def _gather_body(n_arr, sum_out):
    def body(*refs):
        x_refs = refs[:n_arr]
        out_refs = refs[n_arr:2 * n_arr]
        pos = 2 * n_arr
        sum_refs = refs[pos:pos + n_arr] if sum_out else ()
        pos += n_arr if sum_out else 0
        send_sems, recv_sems, local_sems = refs[pos:pos + 3]
        x, y, c = lax.axis_index("x"), lax.axis_index("y"), lax.axis_index("c")
        me, sibling = (x, y, c), (x, y, 1 - c)
        chips = [(1 - x, y), (x, 1 - y), (1 - x, 1 - y)]

        def copy(a, k, block, to, src=None):
            rows = out_refs[a].at[_flat_id(block)]
            return pltpu.make_async_remote_copy(
                src_ref=rows if src is None else src, dst_ref=rows,
                send_sem=send_sems.at[a, k], recv_sem=recv_sems.at[a, k],
                device_id=to, device_id_type=MESH)

        sends = []
        mine = []
        for a in range(n_arr):
            cp = pltpu.make_async_copy(x_refs[a], out_refs[a].at[_flat_id(me)], local_sems.at[a])
            cp.start()
            mine.append(cp)
            first = [copy(a, 0, me, sibling, src=x_refs[a])]
            first += [copy(a, 1 + j, me, (*chip, c), src=x_refs[a]) for j, chip in enumerate(chips)]
            for cp in first:
                cp.start()
            sends += first
        for a in range(n_arr):
            for j, chip in enumerate(chips):
                copy(a, 1 + j, (*chip, c), me).wait_recv()
                fwd = copy(a, 4 + j, (*chip, c), sibling)
                fwd.start()
                sends.append(fwd)
        for a in range(n_arr):
            copy(a, 0, sibling, me).wait_recv()
            for j, chip in enumerate(chips):
                copy(a, 4 + j, (*chip, 1 - c), me).wait_recv()
        for cp in sends:
            cp.wait_send()
        for cp in mine:
            cp.wait()
        if sum_out:
            for a in range(n_arr):
                acc = out_refs[a][0]
                for k in range(1, N_DEV):
                    acc = acc + out_refs[a][k]
                sum_refs[a][...] = acc

    return body


def all_gather(arrs, *, name, hbm, sum_out=False):
    n = len(arrs)
    spec = ANY if hbm else VMEM_SPEC
    out_shape = [jax.ShapeDtypeStruct((N_DEV,) + a.shape, a.dtype) for a in arrs]
    out_specs = [spec] * n
    if sum_out:
        out_shape += [jax.ShapeDtypeStruct(a.shape, a.dtype) for a in arrs]
        out_specs += [VMEM_SPEC] * n
    res = pl.pallas_call(
        _gather_body(n, sum_out), name=name,
        out_shape=out_shape, in_specs=[spec] * n, out_specs=out_specs,
        scratch_shapes=[pltpu.SemaphoreType.DMA((n, 7)), pltpu.SemaphoreType.DMA((n, 7)),
                        pltpu.SemaphoreType.DMA((n,))],
        compiler_params=pltpu.CompilerParams(vmem_limit_bytes=VMEM_LIMIT),
    )(*arrs)
    return res


_RELATIONS = [(rx, ry, rc) for rx in (0, 1) for ry in (0, 1) for rc in (0, 1)][1:]


HBM_SPEC = pl.BlockSpec(memory_space=pltpu.HBM)
SEM_SPEC = pl.BlockSpec(memory_space=pltpu.SEMAPHORE)
_DATAFLOW = pltpu.SideEffectType.DATAFLOW_SIDE_EFFECTING
N_PEERS = N_DEV - 1


def _peer_copy(src_ref, land_ref, send_sem, recv_sem, k, rel, scatter, sending):
    x, y, c = lax.axis_index("x"), lax.axis_index("y"), lax.axis_index("c")
    rx, ry, rc = rel
    peer = (1 - x if rx else x, 1 - y if ry else y, 1 - c if rc else c)
    src = src_ref.at[_flat_id(peer)] if scatter else src_ref
    dst = land_ref.at[_flat_id((x, y, c)) if sending else _flat_id(peer)]
    return pltpu.make_async_remote_copy(src_ref=src, dst_ref=dst, send_sem=send_sem.at[k], recv_sem=recv_sem.at[k],
                                        device_id=peer, device_id_type=MESH)


def exchange_start(arrs, *, scatter, name):
    n = len(arrs)
    lands = [lax.empty((N_DEV,) + (a.shape[1:] if scatter else a.shape), a.dtype) for a in arrs]

    def body(*refs):
        srcs, lnds, sems = refs[:n], refs[n:2 * n], refs[2 * n:4 * n]
        token = refs[6 * n]
        for a in range(n):
            for k, rel in enumerate(_RELATIONS):
                _peer_copy(srcs[a], lnds[a], sems[2 * a], sems[2 * a + 1], k, rel, scatter, True).start()
        token[...] = jnp.zeros_like(token)

    outs = pl.pallas_call(
        body, name=name,
        out_shape=[pltpu.SemaphoreType.DMA((N_PEERS,))] * (2 * n)
        + [pltpu.HBM(a.shape, a.dtype) for a in arrs] + [pltpu.HBM(l.shape, l.dtype) for l in lands]
        + [jax.ShapeDtypeStruct((8, 128), F32)],
        in_specs=[HBM_SPEC] * (2 * n),
        out_specs=[SEM_SPEC] * (2 * n) + [HBM_SPEC] * (2 * n) + [VMEM_SPEC],
        input_output_aliases={i: 2 * n + i for i in range(2 * n)},
        compiler_params=pltpu.CompilerParams(has_side_effects=_DATAFLOW),
    )(*[pltpu.with_memory_space_constraint(a, pltpu.HBM) for a in arrs],
      *[pltpu.with_memory_space_constraint(l, pltpu.HBM) for l in lands])
    handles = [(outs[2 * a], outs[2 * a + 1], outs[2 * n + a], outs[3 * n + a]) for a in range(n)]
    return handles, outs[4 * n]


def exchange_wait(handle, after, *, scatter, name):
    send_sem, recv_sem, src, land = handle

    def body(src_ref, land_ref, send_ref, recv_ref, after_ref, src_out, land_out):
        for k, rel in enumerate(_RELATIONS):
            cp = _peer_copy(src_ref, land_ref, send_ref, recv_ref, k, rel, scatter, False)
            cp.wait_send()
            cp.wait_recv()

    outs = pl.pallas_call(
        body, name=name,
        out_shape=[pltpu.HBM(src.shape, src.dtype), pltpu.HBM(land.shape, land.dtype)],
        in_specs=[HBM_SPEC, HBM_SPEC, SEM_SPEC, SEM_SPEC, ANY],
        out_specs=[HBM_SPEC, HBM_SPEC], input_output_aliases={0: 0, 1: 1},
        compiler_params=pltpu.CompilerParams(has_side_effects=_DATAFLOW),
    )(src, land, send_sem, recv_sem, after)
    return outs[0], outs[1]


_STAGE1 = [(0, 0, 1), (1, 0, 0), (0, 1, 0), (1, 1, 0)]
_OTHER_CHIPS = [(1, 0), (0, 1), (1, 1)]


def _stage_copy(stage, k, shard_ref, land_ref, send_sem, recv_sem, sending):
    x, y, c = lax.axis_index("x"), lax.axis_index("y"), lax.axis_index("c")
    if stage == 1:
        rx, ry, rc = _STAGE1[k]
        peer = (1 - x if rx else x, 1 - y if ry else y, 1 - c if rc else c)
        src = shard_ref
        dst = land_ref.at[_flat_id((x, y, c)) if sending else _flat_id(peer)]
    else:
        cx, cy = _OTHER_CHIPS[k]
        peer = (x, y, 1 - c)
        chip = (1 - x if cx else x, 1 - y if cy else y)
        slot = _flat_id((*chip, c)) if sending else _flat_id((*chip, 1 - c))
        src = land_ref.at[_flat_id((*chip, c))]
        dst = land_ref.at[slot]
    return pltpu.make_async_remote_copy(src_ref=src, dst_ref=dst, send_sem=send_sem.at[k], recv_sem=recv_sem.at[k],
                                        device_id=peer, device_id_type=MESH)


def staged_gather_start(stage, shard, land, *, name):
    n_copies = len(_STAGE1) if stage == 1 else len(_OTHER_CHIPS)
    if land is None:
        land = lax.empty((N_DEV,) + shard.shape, shard.dtype)

    def body(shard_ref, land_ref, send_sem, recv_sem, shard_thru, land_thru, token):
        for k in range(n_copies):
            _stage_copy(stage, k, shard_ref, land_ref, send_sem, recv_sem, True).start()
        token[...] = jnp.zeros_like(token)

    outs = pl.pallas_call(
        body, name=name,
        out_shape=[pltpu.SemaphoreType.DMA((n_copies,)), pltpu.SemaphoreType.DMA((n_copies,)),
                   pltpu.HBM(shard.shape, shard.dtype), pltpu.HBM(land.shape, land.dtype),
                   jax.ShapeDtypeStruct((8, 128), F32)],
        in_specs=[HBM_SPEC, HBM_SPEC],
        out_specs=[SEM_SPEC, SEM_SPEC, HBM_SPEC, HBM_SPEC, VMEM_SPEC],
        input_output_aliases={0: 2, 1: 3},
        compiler_params=pltpu.CompilerParams(has_side_effects=_DATAFLOW),
    )(pltpu.with_memory_space_constraint(shard, pltpu.HBM), pltpu.with_memory_space_constraint(land, pltpu.HBM))
    return outs[:4], outs[4]


def staged_gather_wait(stage, handle, after, *, name):
    send_sem, recv_sem, shard, land = handle
    n_copies = len(_STAGE1) if stage == 1 else len(_OTHER_CHIPS)

    def body(shard_ref, land_ref, send_ref, recv_ref, after_ref, shard_out, land_out):
        for k in range(n_copies):
            cp = _stage_copy(stage, k, shard_ref, land_ref, send_ref, recv_ref, False)
            cp.wait_send()
            cp.wait_recv()

    outs = pl.pallas_call(
        body, name=name,
        out_shape=[pltpu.HBM(shard.shape, shard.dtype), pltpu.HBM(land.shape, land.dtype)],
        in_specs=[HBM_SPEC, HBM_SPEC, SEM_SPEC, SEM_SPEC, ANY],
        out_specs=[HBM_SPEC, HBM_SPEC], input_output_aliases={0: 0, 1: 1},
        compiler_params=pltpu.CompilerParams(has_side_effects=_DATAFLOW),
    )(shard, land, send_sem, recv_sem, after)
    return outs[0], outs[1]


_DIMS = {"nn": (((1,), (0,)), ((), ())), "nt": (((1,), (1,)), ((), ())), "tn": (((0,), (0,)), ((), ()))}


def matmul(a, b, *, mode, tm, tn, tk, out_dtype, name, n_out=None, res=None, gate=None, add=None, after=None,
           split_out=False, norm=None):
    if mode == "tn":
        kdim, m = a.shape
    else:
        m, kdim = a.shape
    nfull = b.shape[0] if mode == "nt" else b.shape[1]
    n = nfull if n_out is None else n_out
    tm, tn, tk = min(tm, m), min(tn, n), min(tk, kdim)
    assert m % tm == 0 and n % tn == 0 and kdim % tk == 0, (name, m, n, kdim, tm, tn, tk)
    nk = kdim // tk
    dims = _DIMS[mode]
    a_spec = pl.BlockSpec((tk, tm), lambda j, i, k: (k, i)) if mode == "tn" else pl.BlockSpec((tm, tk), lambda j, i, k: (i, k))
    b_spec = pl.BlockSpec((tn, tk), lambda j, i, k: (j, k)) if mode == "nt" else pl.BlockSpec((tk, tn), lambda j, i, k: (k, j))
    o_spec = pl.BlockSpec((tm, tn), lambda j, i, k: (i, j))
    in_specs = [a_spec, b_spec]
    operands = [a, b]
    aliases = {}
    vec = pl.BlockSpec((1, tn), lambda j, i, k: (0, j))
    if res is not None:
        in_specs += [o_spec, vec]
        operands += [res, gate]
    if norm is not None:
        assert res is not None and tn == n, (name, tn, n)
        in_specs += [vec, vec, vec]
        operands += [_row(v) for v in norm]
    if add is not None:
        in_specs += [o_spec]
        aliases = {len(operands): 0}
        operands += [add]
    if after is not None:
        in_specs += [ANY]
        operands += [after]
    out_cols = n if add is None else add.shape[1]
    out_shape = [jax.ShapeDtypeStruct((m, out_cols), out_dtype)]
    out_specs = [o_spec]
    if split_out:
        out_shape = [jax.ShapeDtypeStruct((n // tn, m, tn), out_dtype)]
        out_specs = [pl.BlockSpec((None, tm, tn), lambda j, i, k: (j, i, 0))]
    if res is not None:
        out_shape.append(jax.ShapeDtypeStruct((m, n), BF16))
        out_specs.append(o_spec)
    if norm is not None:
        out_shape.append(jax.ShapeDtypeStruct((m, n), BF16))
        out_specs.append(o_spec)

    def body(*refs):
        a_ref, b_ref = refs[:2]
        pos = 2
        if res is not None:
            res_ref, gate_ref = refs[pos:pos + 2]
            pos += 2
        if norm is not None:
            ng_ref, nshift_ref, nscale_ref = refs[pos:pos + 3]
            pos += 3
        if add is not None:
            add_ref = refs[pos]
            pos += 1
        if after is not None:
            pos += 1
        o_ref = refs[pos]
        pos += 1
        if res is not None:
            r_ref = refs[pos]
            pos += 1
        if norm is not None:
            h_ref = refs[pos]
            pos += 1
        acc_ref = refs[pos] if nk > 1 else None
        prod = lax.dot_general(a_ref[...].astype(BF16), b_ref[...].astype(BF16), dims,
                               preferred_element_type=F32)

        def finish(acc):
            if res is not None:
                xnew = res_ref[...] + gate_ref[...] * acc
                o_ref[...] = xnew.astype(out_dtype)
                r_ref[...] = acc.astype(BF16)
                if norm is not None:
                    rstd = lax.rsqrt(jnp.mean(xnew * xnew, axis=-1, keepdims=True) + EPS)
                    y = xnew * rstd * ng_ref[...]
                    h_ref[...] = (y * (1.0 + nscale_ref[...]) + nshift_ref[...]).astype(BF16)
            elif add is not None:
                o_ref[...] = (add_ref[...] + acc).astype(out_dtype)
            else:
                o_ref[...] = acc.astype(out_dtype)

        if nk == 1:
            finish(prod)
        else:
            k = pl.program_id(2)

            @pl.when(k == 0)
            def _():
                acc_ref[...] = prod

            @pl.when(k > 0)
            def _():
                acc_ref[...] += prod

            @pl.when(k == nk - 1)
            def _():
                finish(acc_ref[...])

    outs = pl.pallas_call(
        body, name=name, grid=(n // tn, m // tm, nk),
        in_specs=in_specs, out_specs=out_specs, out_shape=out_shape,
        scratch_shapes=[pltpu.VMEM((tm, tn), F32)] if nk > 1 else [],
        input_output_aliases=aliases,
        compiler_params=_params("parallel", "parallel", "arbitrary"),
    )(*operands)
    return outs if res is not None else outs[0]


def _rows_tile(n, want):
    return min(n, want)


def norm_mod_fwd(x, g, shift, scale, *, name):
    n, d = x.shape
    tt = _rows_tile(n, 256)

    def body(x_ref, g_ref, sh_ref, sc_ref, h_ref):
        xv = x_ref[...]
        rstd = lax.rsqrt(jnp.mean(xv * xv, axis=-1, keepdims=True) + EPS)
        y = xv * rstd * g_ref[...]
        h_ref[...] = (y * (1.0 + sc_ref[...]) + sh_ref[...]).astype(BF16)

    vec = pl.BlockSpec((1, d), lambda i: (0, 0))
    return pl.pallas_call(
        body, name=name, grid=(n // tt,),
        in_specs=[pl.BlockSpec((tt, d), lambda i: (i, 0)), vec, vec, vec],
        out_specs=pl.BlockSpec((tt, d), lambda i: (i, 0)),
        out_shape=jax.ShapeDtypeStruct((n, d), BF16),
        compiler_params=_params("parallel"),
    )(x, _row(g), _row(shift), _row(scale))


GATE_ROW = 3


def norm_mod_bwd(x, dh, dres, g, scale, *, name, branch=None):
    n, d = x.shape
    tt = _rows_tile(n, 256)
    has_res = dres is not None
    has_branch = branch is not None
    last = n // tt - 1

    def body(*refs):
        x_ref, dh_ref = refs[:2]
        pos = 2
        if has_res:
            dres_ref = refs[pos]
            pos += 1
        if has_branch:
            r_ref, gate_ref = refs[pos:pos + 2]
            pos += 2
        g_ref, sc_ref, dx_ref = refs[pos:pos + 3]
        pos += 3
        if has_branch:
            dr_ref = refs[pos]
            pos += 1
        acc_ref, s_ref = refs[pos:pos + 2]
        i = pl.program_id(0)
        xv = x_ref[...]
        dhv = dh_ref[...]
        rstd = lax.rsqrt(jnp.mean(xv * xv, axis=-1, keepdims=True) + EPS)
        xhat = xv * rstd
        dxhat = dhv * (g_ref[...] * (1.0 + sc_ref[...]))
        dx = rstd * (dxhat - xhat * jnp.mean(dxhat * xhat, axis=-1, keepdims=True))
        if has_res:
            dx = dx + dres_ref[...]
        dx_ref[...] = dx

        @pl.when(i == 0)
        def _():
            s_ref[...] = jnp.zeros_like(s_ref)

        s_ref[0:1, :] += jnp.sum(dhv, axis=0, keepdims=True)
        s_ref[1:2, :] += jnp.sum(dhv * xhat, axis=0, keepdims=True)
        if has_branch:
            dr_ref[...] = (dx * gate_ref[...]).astype(BF16)
            s_ref[2:3, :] += jnp.sum(dx * r_ref[...].astype(F32), axis=0, keepdims=True)

        @pl.when(i == last)
        def _():
            s1 = s_ref[0:1, :]
            s2 = s_ref[1:2, :]
            acc_ref[...] = jnp.zeros_like(acc_ref)
            acc_ref[0:1, :] = s1
            acc_ref[1:2, :] = s2 * g_ref[...]
            acc_ref[2:3, :] = s2 * (1.0 + sc_ref[...])
            acc_ref[GATE_ROW:GATE_ROW + 1, :] = s_ref[2:3, :]

    vec = pl.BlockSpec((1, d), lambda i: (0, 0))
    big = pl.BlockSpec((tt, d), lambda i: (i, 0))
    ops = [x, dh] + ([dres] if has_res else []) + ([branch[0], _row(branch[1])] if has_branch else [])
    ops += [_row(g), _row(scale)]
    return pl.pallas_call(
        body, name=name, grid=(n // tt,),
        in_specs=[big, big] + ([big] if has_res else []) + ([big, vec] if has_branch else []) + [vec, vec],
        out_specs=[big] + ([big] if has_branch else []) + [pl.BlockSpec((8, d), lambda i: (0, 0))],
        out_shape=[jax.ShapeDtypeStruct((n, d), F32)] + ([jax.ShapeDtypeStruct((n, d), BF16)] if has_branch else [])
        + [jax.ShapeDtypeStruct((8, d), F32)],
        scratch_shapes=[pltpu.VMEM((8, d), F32)],
        compiler_params=_params("arbitrary"),
    )(*ops)


def final_loss(x, target, g, r, gate, *, name):
    n, d = x.shape
    tt = _rows_tile(n, 256)

    def body(x_ref, t_ref, g_ref, r_ref, gate_ref, dx_ref, dr_ref, acc_ref, loss_ref):
        i = pl.program_id(0)
        xv = x_ref[...]
        rstd = lax.rsqrt(jnp.mean(xv * xv, axis=-1, keepdims=True) + EPS)
        xhat = xv * rstd
        e = xhat * g_ref[...] - t_ref[...]
        dy = e * (1.0 / d)
        dxhat = dy * g_ref[...]
        dx = rstd * (dxhat - xhat * jnp.mean(dxhat * xhat, axis=-1, keepdims=True))
        dx_ref[...] = dx
        dr_ref[...] = (dx * gate_ref[...]).astype(BF16)

        @pl.when(i == 0)
        def _():
            acc_ref[...] = jnp.zeros_like(acc_ref)
            loss_ref[...] = jnp.zeros_like(loss_ref)

        acc_ref[0:1, :] += jnp.sum(dy * xhat, axis=0, keepdims=True)
        acc_ref[GATE_ROW:GATE_ROW + 1, :] += jnp.sum(dx * r_ref[...].astype(F32), axis=0, keepdims=True)
        part = 0.5 * jnp.sum(jnp.mean(e * e, axis=-1, keepdims=True), axis=0, keepdims=True)
        loss_ref[...] += jnp.broadcast_to(part, loss_ref.shape)

    big = pl.BlockSpec((tt, d), lambda i: (i, 0))
    vec = pl.BlockSpec((1, d), lambda i: (0, 0))
    return pl.pallas_call(
        body, name=name, grid=(n // tt,),
        in_specs=[big, big, vec, big, vec],
        out_specs=[big, big, pl.BlockSpec((8, d), lambda i: (0, 0)), pl.BlockSpec((8, 128), lambda i: (0, 0))],
        out_shape=[jax.ShapeDtypeStruct((n, d), F32), jax.ShapeDtypeStruct((n, d), BF16),
                   jax.ShapeDtypeStruct((8, d), F32), jax.ShapeDtypeStruct((8, 128), F32)],
        compiler_params=_params("arbitrary"),
    )(x, target, _row(g), r, _row(gate))


def _as_row(col):
    t = col.shape[0]
    return jnp.transpose(jnp.broadcast_to(col, (t, HEAD_DIM)))[0:1, :]


def _swap_pairs(x):
    lane = lax.broadcasted_iota(jnp.int32, x.shape, 1)
    return jnp.where(lane % 2 == 0, pltpu.roll(x, HEAD_DIM - 1, 1), pltpu.roll(x, 1, 1))


def rope_tables(n):
    rows = n // GRID_W
    row = jnp.repeat(jnp.arange(rows, dtype=F32), GRID_W)
    col = jnp.tile(jnp.arange(GRID_W, dtype=F32), rows)
    n_freq = HEAD_DIM // 4
    inv = jnp.power(ROPE_THETA, jnp.arange(n_freq, dtype=F32) * (-2.0 / (HEAD_DIM // 2)))
    ang = jnp.concatenate([row[:, None] * inv, col[:, None] * inv], axis=-1)
    cos, sin = jnp.cos(ang), jnp.sin(ang)
    cexp = jnp.repeat(cos, 2, axis=-1)
    sexp = jnp.stack([-sin, sin], axis=-1).reshape(n, HEAD_DIM)
    return cexp, sexp


V_EXT_W = 2 * HEAD_DIM


def qkv_prep_fwd(p, wq, wk, cexp, sexp, *, latent, name, rows_all=None, k_all=None, v_ext=None):
    n = p.shape[0]
    tt = _rows_tile(n, 256)
    width = 2 * KV_W + (ATTN_W if latent else 0)

    def body(*refs):
        if latent:
            p_ref, wq_ref, wk_ref, c_ref, s_ref, q_ref, k_ref, v_ref = refs
        else:
            p_ref, wk_ref, _, _, k_ref, v_ref = refs

        def head(xv, w):
            rstd = lax.rsqrt(jnp.mean(xv * xv, axis=-1, keepdims=True) + EPS)
            yv = xv * rstd * w
            if latent:
                yv = yv * c_ref[...] + _swap_pairs(yv) * s_ref[...]
            return yv

        for h in range(N_KV_HEADS):
            sl = slice(h * HEAD_DIM, (h + 1) * HEAD_DIM)
            k_ref[:, sl] = head(p_ref[:, sl], wk_ref[...]).astype(BF16)
            v_ref[:, h * V_EXT_W:h * V_EXT_W + HEAD_DIM] = p_ref[:, KV_W + h * HEAD_DIM:KV_W + (h + 1) * HEAD_DIM].astype(BF16)
            lane = lax.broadcasted_iota(jnp.int32, (tt, HEAD_DIM), 1)
            v_ref[:, h * V_EXT_W + HEAD_DIM:(h + 1) * V_EXT_W] = jnp.where(lane == 0, 1.0, 0.0).astype(BF16)
        if latent:
            for h in range(N_Q_HEADS):
                sl = slice(2 * KV_W + h * HEAD_DIM, 2 * KV_W + (h + 1) * HEAD_DIM)
                q_ref[:, h * HEAD_DIM:(h + 1) * HEAD_DIM] = (head(p_ref[:, sl], wq_ref[...]) * Q_SCALE).astype(BF16)

    vec = pl.BlockSpec((1, HEAD_DIM), lambda i: (0, 0))
    tab = pl.BlockSpec((tt, HEAD_DIM), lambda i: (i, 0))
    vw = N_KV_HEADS * V_EXT_W
    k_shape = jax.ShapeDtypeStruct((rows_all, KV_W), BF16)
    v_shape = jax.ShapeDtypeStruct((rows_all, vw), BF16)
    aliases = {}
    if latent:
        in_specs = [pl.BlockSpec((tt, width), lambda i: (i, 0)), vec, vec, tab, tab]
        ops = [p, _row(wq), _row(wk), cexp, sexp]
        out_specs = [pl.BlockSpec((tt, ATTN_W), lambda i: (i, 0)), pl.BlockSpec((tt, KV_W), lambda i: (i, 0)),
                     pl.BlockSpec((tt, vw), lambda i: (i, 0))]
        out_shape = [jax.ShapeDtypeStruct((n, ATTN_W), BF16), k_shape, v_shape]
    else:
        assert n == tt and (rows_all - n) % n == 0, (n, tt, rows_all)
        first = (rows_all - n) // n
        in_specs = [pl.BlockSpec((tt, width), lambda i: (i, 0)), vec, ANY, ANY]
        ops = [p, _row(wk), k_all, v_ext]
        out_specs = [pl.BlockSpec((tt, KV_W), lambda i: (first, 0)), pl.BlockSpec((tt, vw), lambda i: (first, 0))]
        out_shape = [k_shape, v_shape]
        aliases = {2: 0, 3: 1}
    return pl.pallas_call(
        body, name=name, grid=(n // tt,), in_specs=in_specs, out_specs=out_specs, out_shape=out_shape,
        input_output_aliases=aliases, compiler_params=_params("parallel"),
    )(*ops)


def qkv_prep_bwd(p, dq, dk, dv, wq, wk, cexp, sexp, *, latent, name):
    n = p.shape[0]
    tt = _rows_tile(n, 256)
    width = 2 * KV_W + (ATTN_W if latent else 0)

    def body(*refs):
        if latent:
            p_ref, dq_ref, dk_ref, dv_ref, wq_ref, wk_ref, c_ref, s_ref, dp_ref, acc_ref = refs
        else:
            p_ref, dk_ref, dv_ref, wk_ref, dp_ref, acc_ref = refs
        i = pl.program_id(0)

        @pl.when(i == 0)
        def _():
            acc_ref[...] = jnp.zeros_like(acc_ref)

        def head(xv, dy, w, row):
            if latent:
                dy = dy * c_ref[...] + _swap_pairs(dy * s_ref[...])
            rstd = lax.rsqrt(jnp.mean(xv * xv, axis=-1, keepdims=True) + EPS)
            xhat = xv * rstd
            acc_ref[row:row + 1, :] += jnp.sum(dy * xhat, axis=0, keepdims=True)
            dxhat = dy * w
            return rstd * (dxhat - xhat * jnp.mean(dxhat * xhat, axis=-1, keepdims=True))

        for h in range(N_KV_HEADS):
            sl = slice(h * HEAD_DIM, (h + 1) * HEAD_DIM)
            dp_ref[:, sl] = head(p_ref[:, sl], dk_ref[:, sl], wk_ref[...], 1).astype(BF16)
        dp_ref[:, KV_W:2 * KV_W] = dv_ref[...].astype(BF16)
        if latent:
            for h in range(N_Q_HEADS):
                sl = slice(2 * KV_W + h * HEAD_DIM, 2 * KV_W + (h + 1) * HEAD_DIM)
                dyq = dq_ref[:, h * HEAD_DIM:(h + 1) * HEAD_DIM] * Q_SCALE
                dp_ref[:, sl] = head(p_ref[:, sl], dyq, wq_ref[...], 0).astype(BF16)

    vec = pl.BlockSpec((1, HEAD_DIM), lambda i: (0, 0))
    tab = pl.BlockSpec((tt, HEAD_DIM), lambda i: (i, 0))
    kv_spec = pl.BlockSpec((tt, KV_W), lambda i: (i, 0))
    p_spec = pl.BlockSpec((tt, width), lambda i: (i, 0))
    if latent:
        in_specs = [p_spec, pl.BlockSpec((tt, ATTN_W), lambda i: (i, 0)), kv_spec, kv_spec, vec, vec, tab, tab]
        ops = [p, dq, dk, dv, _row(wq), _row(wk), cexp, sexp]
    else:
        in_specs = [p_spec, kv_spec, kv_spec, vec]
        ops = [p, dk, dv, _row(wk)]
    return pl.pallas_call(
        body, name=name, grid=(n // tt,), in_specs=in_specs,
        out_specs=[p_spec, pl.BlockSpec((8, HEAD_DIM), lambda i: (0, 0))],
        out_shape=[jax.ShapeDtypeStruct((n, width), BF16), jax.ShapeDtypeStruct((8, HEAD_DIM), F32)],
        compiler_params=_params("arbitrary"),
    )(*ops)


def _kv_chunks(n, s_all):
    step = 1024 if n % 1024 == 0 else 256
    chunks = [(s, step) for s in range(0, n, step)]
    if s_all > n:
        chunks.append((n, s_all - n))
    return chunks


def flash_fwd(q, k_all, v_ext, p, *, za_block, mix_width, name):
    n = q.shape[0]
    s_all = k_all.shape[0]
    tq = _rows_tile(n, 512)
    chunks = _kv_chunks(n, s_all)
    wide = 2 * HEAD_DIM

    def body(q_ref, k_ref, v_ref, za_ref, o_ref, mix_ref, lse_ref, m_ref, acc_ref):
        qv = q_ref[...]
        m_ref[...] = jnp.full_like(m_ref, -jnp.inf)
        acc_ref[...] = jnp.zeros_like(acc_ref)
        for start, size in chunks:
            kc = k_ref[pl.ds(start, size), :]
            vc = v_ref[pl.ds(start, size), :]
            s = lax.dot_general(qv, kc, _DIMS["nt"], preferred_element_type=F32)
            m_old = m_ref[...]
            m_new = jnp.maximum(m_old, jnp.max(s, axis=-1, keepdims=True))
            pr = jnp.exp2(s - m_new)
            alpha = jnp.exp2(m_old - m_new)
            acc_ref[...] = alpha * acc_ref[...] + jnp.dot(pr.astype(BF16), vc, preferred_element_type=F32)
            m_ref[...] = m_new
        acc = acc_ref[...]
        denom = acc[:, HEAD_DIM:HEAD_DIM + 1]
        o = acc[:, :HEAD_DIM] / denom
        o_ref[...] = o.astype(BF16)
        mix_ref[...] = (o * _silu(za_ref[...])).astype(BF16)
        lse_ref[0] = _as_row(m_ref[...] + jnp.log2(denom))

    qspec = pl.BlockSpec((tq, HEAD_DIM), lambda h, i: (i, h))
    return pl.pallas_call(
        body, name=name, grid=(N_Q_HEADS, n // tq),
        in_specs=[qspec, pl.BlockSpec((s_all, HEAD_DIM), lambda h, i: (0, h // Q_PER_KV)),
                  pl.BlockSpec((s_all, wide), lambda h, i: (0, h // Q_PER_KV)),
                  pl.BlockSpec((tq, HEAD_DIM), lambda h, i: (i, za_block + h))],
        out_specs=[qspec, qspec, pl.BlockSpec((1, 1, tq), lambda h, i: (h, 0, i))],
        out_shape=[jax.ShapeDtypeStruct((n, ATTN_W), BF16), jax.ShapeDtypeStruct((n, mix_width), BF16),
                   jax.ShapeDtypeStruct((N_Q_HEADS, 1, n), F32)],
        scratch_shapes=[pltpu.VMEM((tq, 1), F32), pltpu.VMEM((tq, wide), F32)],
        compiler_params=_params("parallel", "parallel"),
    )(q, k_all, v_ext, p)


def attn_gate_bwd(dmix, o, p, *, za_block, name):
    n = o.shape[0]
    tt = _rows_tile(n, 512)
    za_half = za_block * HEAD_DIM // HALF
    n_half = ATTN_W // HALF

    def body(*refs):
        dm_ref, o_ref = refs[:2]
        za_refs = refs[2:2 + n_half]
        do_ref, dos_ref, dza_ref, delta_ref = refs[2 + n_half:]
        dm = dm_ref[...]
        ov = o_ref[...].astype(F32)
        za = jnp.concatenate([r[...] for r in za_refs], axis=-1)
        do = dm * _silu(za)
        do_ref[...] = do.astype(BF16)
        dos_ref[...] = (do * LN2).astype(BF16)
        dza_ref[...] = (dm * ov * _dsilu(za)).astype(BF16)
        prod = do * ov
        for h in range(N_Q_HEADS):
            col = jnp.sum(prod[:, h * HEAD_DIM:(h + 1) * HEAD_DIM], axis=-1, keepdims=True) * LN2
            delta_ref[h] = _as_row(col)

    spec = pl.BlockSpec((tt, ATTN_W), lambda i: (i, 0))
    shape = jax.ShapeDtypeStruct((n, ATTN_W), BF16)
    za_specs = [pl.BlockSpec((tt, HALF), functools.partial(lambda i, cb: (i, cb), cb=za_half + h))
                for h in range(n_half)]
    return pl.pallas_call(
        body, name=name, grid=(n // tt,),
        in_specs=[spec, spec] + za_specs,
        out_specs=[spec, spec, spec, pl.BlockSpec((N_Q_HEADS, 1, tt), lambda i: (0, 0, i))],
        out_shape=[shape, shape, shape, jax.ShapeDtypeStruct((N_Q_HEADS, 1, n), F32)],
        compiler_params=_params("parallel"),
    )(dmix, o, *([p] * n_half))


def flash_bwd(q, do, do_s, lse_row, delta_row, k_all, v_ext, *, name):
    n = q.shape[0]
    s_all = k_all.shape[0]
    tq = _rows_tile(n, 1024)
    chunks = _kv_chunks(n, s_all)

    def body(q_ref, do_ref, dos_ref, lse_ref, dl_ref, k_ref, v_ref, dq_ref, dk_ref, dv_ref):
        g = pl.program_id(1)
        i = pl.program_id(2)

        @pl.when((g == 0) & (i == 0))
        def _():
            dk_ref[...] = jnp.zeros_like(dk_ref)
            dv_ref[...] = jnp.zeros_like(dv_ref)

        qv = q_ref[...]
        dov = do_ref[...]
        dosv = dos_ref[...]
        lse = lse_ref[0]
        dl = dl_ref[0]
        dq = jnp.zeros((tq, HEAD_DIM), F32)
        for start, size in chunks:
            kc = k_ref[pl.ds(start, size), :]
            vc = v_ref[pl.ds(start, size), :]
            st = lax.dot_general(kc, qv, _DIMS["nt"], preferred_element_type=F32)
            pt = jnp.exp2(st - lse)
            dpt = lax.dot_general(vc, dosv, _DIMS["nt"], preferred_element_type=F32)
            dst = (pt * (dpt - dl)).astype(BF16)
            dv_ref[pl.ds(start, size), :] += jnp.dot(pt.astype(BF16), dov, preferred_element_type=F32)
            dk_ref[pl.ds(start, size), :] += jnp.dot(dst, qv, preferred_element_type=F32)
            dq = dq + lax.dot_general(dst, kc, _DIMS["tn"], preferred_element_type=F32)
        dq_ref[...] = dq

    qspec = pl.BlockSpec((tq, HEAD_DIM), lambda kh, g, i: (i, kh * Q_PER_KV + g))
    rowspec = pl.BlockSpec((1, 1, tq), lambda kh, g, i: (kh * Q_PER_KV + g, 0, i))
    kvspec = pl.BlockSpec((s_all, HEAD_DIM), lambda kh, g, i: (0, kh))
    return pl.pallas_call(
        body, name=name, grid=(N_KV_HEADS, Q_PER_KV, n // tq),
        in_specs=[qspec, qspec, qspec, rowspec, rowspec, kvspec,
                  pl.BlockSpec((s_all, HEAD_DIM), lambda kh, g, i: (0, kh * (V_EXT_W // HEAD_DIM)))],
        out_specs=[qspec, kvspec, kvspec],
        out_shape=[jax.ShapeDtypeStruct((n, ATTN_W), F32),
                   jax.ShapeDtypeStruct((s_all, KV_W), F32), jax.ShapeDtypeStruct((s_all, KV_W), F32)],
        compiler_params=_params("arbitrary", "arbitrary", "arbitrary"),
    )(q, do, do_s, lse_row, delta_row, k_all, v_ext)


HALF = 512


SUB = 128


def _tap_group(w_ref, w_lanes, ext_ref, r0, lanes, r, tap_of):
    z = None
    for a in range(4):
        k = tap_of(8 * a + r)
        if 0 <= k < CONV_WIDTH:
            term = w_ref[k:k + 1, w_lanes] * ext_ref[pl.ds(r0 + 8 * a, SUB + 8), lanes]
            z = term if z is None else z + term
    return z


def _shifted(z, z_ref, r):
    if r == 0:
        return z[0:SUB]
    z_ref[...] = z
    return z_ref[pl.ds(r, SUB), :]


def _fold8(x):
    return jnp.sum(x.reshape(x.shape[0] // 8, 8, x.shape[1]), axis=0)


def _pieces(tt, h):
    return [(rh * SUB, slice(c * SUB, (c + 1) * SUB), slice(h * HALF + c * SUB, h * HALF + (c + 1) * SUB))
            for c in range(HALF // SUB) for rh in range(tt // SUB)]


def _halo_specs(tt, n, cb):
    per = tt // HALO
    last = n // HALO - 1
    return [pl.BlockSpec((HALO, HALF), lambda i: (jnp.maximum(i * per - 1, 0), cb)),
            pl.BlockSpec((tt, HALF), lambda i: (i, cb)),
            pl.BlockSpec((HALO, HALF), lambda i: (jnp.minimum((i + 1) * per, last), cb))]


def conv_fwd(p, dw_w, dw_b, ln_g, ln_b, mix, *, glu_block, name):
    n = p.shape[0]
    ch = dw_w.shape[1]
    nh = ch // HALF
    tt = _rows_tile(n, 256)
    last = n // tt - 1

    def body(*refs):
        a_refs = [refs[3 * h:3 * h + 3] for h in range(nh)]
        b_refs = [refs[3 * (nh + h):3 * (nh + h) + 3] for h in range(nh)]
        pos = 6 * nh
        zb_refs = refs[pos:pos + nh]
        pos += nh
        w_ref, bias_ref, g_ref, be_ref, _, ycv_ref, mix_ref, ext_ref, z_ref = refs[pos:pos + 9]
        i = pl.program_id(0)
        for h in range(nh):
            cs = slice(h * HALF, (h + 1) * HALF)
            ap, am, an = a_refs[h]
            bp, bm, bn = b_refs[h]
            ext_ref[0:HALO, :] = jnp.where(i > 0, ap[...] * _sigmoid(bp[...]), 0.0)
            ext_ref[HALO:HALO + tt, :] = am[...] * _sigmoid(bm[...])
            ext_ref[HALO + tt:2 * HALO + tt, :] = jnp.where(i < last, an[...] * _sigmoid(bn[...]), 0.0)
            for r0, lanes, w_lanes in _pieces(tt, h):
                acc = jnp.broadcast_to(bias_ref[:, w_lanes], (SUB, SUB))
                for r in range(8):
                    z = _tap_group(w_ref, w_lanes, ext_ref, r0, lanes, r, lambda j: j - 1)
                    acc = acc + _shifted(z, z_ref, r)
                ycv_ref[pl.ds(r0, SUB), w_lanes] = acc
        yc = ycv_ref[...]
        mu = jnp.mean(yc, axis=-1, keepdims=True)
        var = jnp.mean(jnp.square(yc - mu), axis=-1, keepdims=True)
        ln = (yc - mu) * lax.rsqrt(var + EPS) * g_ref[...] + be_ref[...]
        out = _silu(ln)
        for h in range(nh):
            cs = slice(h * HALF, (h + 1) * HALF)
            mix_ref[:, cs] = (out[:, cs] * _silu(zb_refs[h][...])).astype(BF16)

    in_specs = []
    for h in range(2 * nh):
        in_specs += _halo_specs(tt, n, glu_block + h)
    in_specs += [pl.BlockSpec((tt, HALF), functools.partial(lambda i, cb: (i, cb), cb=glu_block + 2 * nh + h))
                 for h in range(nh)]
    vec = pl.BlockSpec((1, ch), lambda i: (0, 0))
    in_specs += [pl.BlockSpec((CONV_WIDTH, ch), lambda i: (0, 0)), vec, vec, vec, ANY]
    ops = [p] * (6 * nh + nh) + [dw_w, _row(dw_b), _row(ln_g), _row(ln_b), mix]
    big = pl.BlockSpec((tt, ch), lambda i: (i, 0))
    mix_block = (mix.shape[1] - ch) // ch
    return pl.pallas_call(
        body, name=name, grid=(n // tt,), in_specs=in_specs,
        out_specs=[big, pl.BlockSpec((tt, ch), lambda i: (i, mix_block))],
        out_shape=[jax.ShapeDtypeStruct((n, ch), F32), jax.ShapeDtypeStruct(mix.shape, BF16)],
        input_output_aliases={len(ops) - 1: 1},
        scratch_shapes=[pltpu.VMEM((tt + 2 * HALO, HALF), F32), pltpu.VMEM((SUB + 8, SUB), F32)],
        compiler_params=_params("parallel"),
    )(*ops)


def conv_bwd_rows(dmix, ycv, p, ln_g, ln_b, *, mix_block, zb_block, name):
    n, ch = ycv.shape
    nh = ch // HALF
    tt = _rows_tile(n, 256)

    def body(*refs):
        dm_ref, ycv_ref = refs[:2]
        zb_refs = refs[2:2 + nh]
        g_ref, be_ref, dy_ref, dzb_ref, acc_ref = refs[2 + nh:]
        i = pl.program_id(0)

        @pl.when(i == 0)
        def _():
            acc_ref[...] = jnp.zeros_like(acc_ref)

        yc = ycv_ref[...]
        mu = jnp.mean(yc, axis=-1, keepdims=True)
        var = jnp.mean(jnp.square(yc - mu), axis=-1, keepdims=True)
        rstd = lax.rsqrt(var + EPS)
        xhat = (yc - mu) * rstd
        ln = xhat * g_ref[...] + be_ref[...]
        out = _silu(ln)
        dm = dm_ref[...]
        zb = jnp.concatenate([r[...] for r in zb_refs], axis=-1)
        dzb_ref[...] = (dm * out * _dsilu(zb)).astype(BF16)
        dln = dm * _silu(zb) * _dsilu(ln)
        acc_ref[0:1, :] += jnp.sum(dln * xhat, axis=0, keepdims=True)
        acc_ref[1:2, :] += jnp.sum(dln, axis=0, keepdims=True)
        dxhat = dln * g_ref[...]
        dy_ref[...] = rstd * (dxhat - jnp.mean(dxhat, axis=-1, keepdims=True)
                              - xhat * jnp.mean(dxhat * xhat, axis=-1, keepdims=True))

    big = pl.BlockSpec((tt, ch), lambda i: (i, 0))
    vec = pl.BlockSpec((1, ch), lambda i: (0, 0))
    in_specs = [pl.BlockSpec((tt, ch), lambda i: (i, mix_block)), big]
    in_specs += [pl.BlockSpec((tt, HALF), functools.partial(lambda i, cb: (i, cb), cb=zb_block + h)) for h in range(nh)]
    in_specs += [vec, vec]
    return pl.pallas_call(
        body, name=name, grid=(n // tt,), in_specs=in_specs,
        out_specs=[big, big, pl.BlockSpec((8, ch), lambda i: (0, 0))],
        out_shape=[jax.ShapeDtypeStruct((n, ch), F32), jax.ShapeDtypeStruct((n, ch), BF16),
                   jax.ShapeDtypeStruct((8, ch), F32)],
        compiler_params=_params("arbitrary"),
    )(dmix, ycv, *([p] * nh), _row(ln_g), _row(ln_b))


def conv_bwd_taps(dycv, p, dw_w, *, glu_block, name):
    n, ch = dycv.shape
    nh = ch // HALF
    tt = _rows_tile(n, 256)
    last = n // tt - 1

    def body(*refs):
        d_refs = [refs[3 * h:3 * h + 3] for h in range(nh)]
        a_refs = [refs[3 * (nh + h):3 * (nh + h) + 3] for h in range(nh)]
        b_refs = [refs[3 * (2 * nh + h):3 * (2 * nh + h) + 3] for h in range(nh)]
        w_ref, dglu_ref, dw_ref, db_ref, yext_ref, dext_ref, z_ref, dy_ref, dwp_ref, dbp_ref = refs[9 * nh:]
        i = pl.program_id(0)

        @pl.when(i == 0)
        def _():
            dwp_ref[...] = jnp.zeros_like(dwp_ref)
            dbp_ref[...] = jnp.zeros_like(dbp_ref)

        for h in range(nh):
            cs = slice(h * HALF, (h + 1) * HALF)
            ap, am, an = a_refs[h]
            bp, bm, bn = b_refs[h]
            dp, dm, dn = d_refs[h]
            av = am[...]
            sb = _sigmoid(bm[...])
            yext_ref[0:HALO, :] = jnp.where(i > 0, ap[...] * _sigmoid(bp[...]), 0.0)
            yext_ref[HALO:HALO + tt, :] = av * sb
            yext_ref[HALO + tt:2 * HALO + tt, :] = jnp.where(i < last, an[...] * _sigmoid(bn[...]), 0.0)
            dmain = dm[...]
            dext_ref[0:HALO, :] = jnp.where(i > 0, dp[...], 0.0)
            dext_ref[HALO:HALO + tt, :] = dmain
            dext_ref[HALO + tt:2 * HALO + tt, :] = jnp.where(i < last, dn[...], 0.0)
            for r0, lanes, w_lanes in _pieces(tt, h):
                dsub = dext_ref[pl.ds(HALO + r0, SUB), lanes]
                dy = jnp.zeros((SUB, SUB), F32)
                for r in range(8):
                    z = _tap_group(w_ref, w_lanes, dext_ref, r0, lanes, r, lambda j: CONV_WIDTH - j)
                    dy = dy + _shifted(z, z_ref, r)
                    y_r = yext_ref[pl.ds(r0 + r, SUB + 24), lanes]
                    for a in range(4):
                        k = 8 * a + r - 1
                        if k >= 0:
                            dwp_ref[8 * k:8 * k + 8, w_lanes] += _fold8(dsub * y_r[8 * a:8 * a + SUB])
                dbp_ref[:, w_lanes] += _fold8(dsub)
                dy_ref[pl.ds(r0, SUB), lanes] = dy
            dy = dy_ref[...]
            dglu_ref[:, cs] = (dy * sb).astype(BF16)
            dglu_ref[:, ch + h * HALF:ch + (h + 1) * HALF] = (dy * av * sb * (1.0 - sb)).astype(BF16)

        @pl.when(i == last)
        def _():
            dw_ref[...] = jnp.sum(dwp_ref[...].reshape(32, 8, ch), axis=1)
            db_ref[...] = jnp.broadcast_to(jnp.sum(dbp_ref[...], axis=0, keepdims=True), db_ref.shape)

    in_specs = []
    for h in range(nh):
        in_specs += _halo_specs(tt, n, h)
    for h in range(2 * nh):
        in_specs += _halo_specs(tt, n, glu_block + h)
    in_specs += [pl.BlockSpec((CONV_WIDTH, ch), lambda i: (0, 0))]
    ops = [dycv] * (3 * nh) + [p] * (6 * nh) + [dw_w]
    return pl.pallas_call(
        body, name=name, grid=(n // tt,), in_specs=in_specs,
        out_specs=[pl.BlockSpec((tt, 2 * ch), lambda i: (i, 0)), pl.BlockSpec((32, ch), lambda i: (0, 0)),
                   pl.BlockSpec((8, ch), lambda i: (0, 0))],
        out_shape=[jax.ShapeDtypeStruct((n, 2 * ch), BF16), jax.ShapeDtypeStruct((32, ch), F32),
                   jax.ShapeDtypeStruct((8, ch), F32)],
        scratch_shapes=[pltpu.VMEM((tt + 2 * HALO, HALF), F32), pltpu.VMEM((tt + 2 * HALO, HALF), F32),
                        pltpu.VMEM((SUB + 8, SUB), F32), pltpu.VMEM((tt, HALF), F32),
                        pltpu.VMEM((8 * 32, ch), F32), pltpu.VMEM((8, ch), F32)],
        compiler_params=_params("arbitrary"),
    )(*ops)


def _sgu_common(p_ref, g_ref, be_ref, ws_ref, bs_ref, w):
    gw = w // SGU_GROUPS
    u_pre = p_ref[:, 0:w]
    v_pre = p_ref[:, w:2 * w]
    zc = p_ref[:, 2 * w:3 * w]
    u = _gelu(u_pre)
    v = _gelu(v_pre)
    mu = jnp.mean(v, axis=-1, keepdims=True)
    var = jnp.mean(jnp.square(v - mu), axis=-1, keepdims=True)
    rstd = lax.rsqrt(var + EPS)
    vhat = (v - mu) * rstd
    vn = (vhat * g_ref[...] + be_ref[...]).astype(BF16)
    mixed = jnp.concatenate(
        [jnp.dot(ws_ref[gi].astype(BF16), vn[:, gi * gw:(gi + 1) * gw], preferred_element_type=F32)
         + bs_ref[:, gi:gi + 1] for gi in range(SGU_GROUPS)], axis=-1)
    return u_pre, v_pre, zc, u, rstd, vhat, vn, mixed


def sgu_fwd(p, ln_g, ln_b, ws, bs_t, *, name):
    n, w3 = p.shape
    w = w3 // 3

    def body(p_ref, g_ref, be_ref, ws_ref, bs_ref, m_ref):
        _, _, zc, u, _, _, _, mixed = _sgu_common(p_ref, g_ref, be_ref, ws_ref, bs_ref, w)
        m_ref[...] = (u * mixed * _silu(zc)).astype(BF16)

    vec = pl.BlockSpec((1, w), lambda i: (0, 0))
    return pl.pallas_call(
        body, name=name, grid=(n // CHUNK,),
        in_specs=[pl.BlockSpec((CHUNK, w3), lambda i: (i, 0)), vec, vec,
                  pl.BlockSpec((SGU_GROUPS, CHUNK, CHUNK), lambda i: (0, 0, 0)),
                  pl.BlockSpec((CHUNK, SGU_GROUPS), lambda i: (0, 0))],
        out_specs=pl.BlockSpec((CHUNK, w), lambda i: (i, 0)),
        out_shape=jax.ShapeDtypeStruct((n, w), BF16),
        compiler_params=_params("parallel"),
    )(p, _row(ln_g), _row(ln_b), ws, bs_t)


def sgu_bwd(p, dm, ln_g, ln_b, ws, ws_t, bs_t, *, name):
    n, w3 = p.shape
    w = w3 // 3
    gw = w // SGU_GROUPS

    def body(p_ref, dm_ref, g_ref, be_ref, ws_ref, wst_ref, bs_ref, dp_ref, dws_ref, dbs_ref, acc_ref):
        i = pl.program_id(0)

        @pl.when(i == 0)
        def _():
            dws_ref[...] = jnp.zeros_like(dws_ref)
            dbs_ref[...] = jnp.zeros_like(dbs_ref)
            acc_ref[...] = jnp.zeros_like(acc_ref)

        u_pre, v_pre, zc, u, rstd, vhat, vn, mixed = _sgu_common(p_ref, g_ref, be_ref, ws_ref, bs_ref, w)
        dmv = dm_ref[...]
        um = u * mixed
        dp_ref[:, 2 * w:3 * w] = (dmv * um * _dsilu(zc)).astype(BF16)
        dum = dmv * _silu(zc)
        dp_ref[:, 0:w] = (dum * mixed * _dgelu(u_pre)).astype(BF16)
        dmixed = dum * u
        dmixed_b = dmixed.astype(BF16)
        dvn_parts = []
        for gi in range(SGU_GROUPS):
            cs = slice(gi * gw, (gi + 1) * gw)
            dws_ref[gi] += lax.dot_general(dmixed_b[:, cs], vn[:, cs], _DIMS["nt"], preferred_element_type=F32)
            dbs_ref[:, gi:gi + 1] += jnp.sum(dmixed[:, cs], axis=-1, keepdims=True)
            dvn_parts.append(jnp.dot(wst_ref[gi].astype(BF16), dmixed_b[:, cs], preferred_element_type=F32))
        dvn = jnp.concatenate(dvn_parts, axis=-1)
        acc_ref[0:1, :] += jnp.sum(dvn * vhat, axis=0, keepdims=True)
        acc_ref[1:2, :] += jnp.sum(dvn, axis=0, keepdims=True)
        dvhat = dvn * g_ref[...]
        dv = rstd * (dvhat - jnp.mean(dvhat, axis=-1, keepdims=True)
                     - vhat * jnp.mean(dvhat * vhat, axis=-1, keepdims=True))
        dp_ref[:, w:2 * w] = (dv * _dgelu(v_pre)).astype(BF16)

    vec = pl.BlockSpec((1, w), lambda i: (0, 0))
    wspec = pl.BlockSpec((SGU_GROUPS, CHUNK, CHUNK), lambda i: (0, 0, 0))
    bspec = pl.BlockSpec((CHUNK, SGU_GROUPS), lambda i: (0, 0))
    return pl.pallas_call(
        body, name=name, grid=(n // CHUNK,),
        in_specs=[pl.BlockSpec((CHUNK, w3), lambda i: (i, 0)), pl.BlockSpec((CHUNK, w), lambda i: (i, 0)),
                  vec, vec, wspec, wspec, bspec],
        out_specs=[pl.BlockSpec((CHUNK, w3), lambda i: (i, 0)), wspec, bspec,
                   pl.BlockSpec((8, w), lambda i: (0, 0))],
        out_shape=[jax.ShapeDtypeStruct((n, w3), BF16), jax.ShapeDtypeStruct((SGU_GROUPS, CHUNK, CHUNK), F32),
                   jax.ShapeDtypeStruct((CHUNK, SGU_GROUPS), F32), jax.ShapeDtypeStruct((8, w), F32)],
        compiler_params=_params("arbitrary"),
    )(p, dm, _row(ln_g), _row(ln_b), ws, ws_t, bs_t)


def _adam_math(w, g, m, v):
    m_new = ADAM_B1 * m + (1.0 - ADAM_B1) * g
    v_new = ADAM_B2 * v + (1.0 - ADAM_B2) * (g * g)
    m_hat = m_new / (1.0 - ADAM_B1 ** ADAM_STEP)
    v_hat = v_new / (1.0 - ADAM_B2 ** ADAM_STEP)
    delta = -ADAM_LR * (m_hat / (jnp.sqrt(v_hat) + ADAM_EPS) + ADAM_WD * w)
    return delta, m_new, v_new


def adamw(w, g, m, v, *, name, slots=False, rows=512):
    r, c = w.shape
    tr = min(r, rows)
    assert r % tr == 0, (name, r, tr)

    def body(w_ref, g_ref, m_ref, v_ref, go_ref, d_ref, mo_ref, vo_ref):
        if slots:
            g = g_ref[0].astype(F32)
            for k in range(1, N_DEV):
                g = g + g_ref[k].astype(F32)
        else:
            g = g_ref[...].astype(F32)
        delta, m_new, v_new = _adam_math(w_ref[...], g, m_ref[...], v_ref[...])
        go_ref[...] = g
        d_ref[...] = delta
        mo_ref[...] = m_new
        vo_ref[...] = v_new

    spec = pl.BlockSpec((tr, c), lambda i: (i, 0))
    gspec = pl.BlockSpec((N_DEV, tr, c), lambda i: (0, i, 0)) if slots else spec
    shape = jax.ShapeDtypeStruct((r, c), F32)
    return pl.pallas_call(
        body, name=name, grid=(r // tr,),
        in_specs=[spec, gspec, spec, spec], out_specs=[spec] * 4, out_shape=[shape] * 4,
        compiler_params=_params("parallel"),
    )(w, g, m, v)


def slot_sum(slots, *, name):
    _, r, c = slots.shape

    def body(s_ref, o_ref):
        acc = s_ref[0]
        for k in range(1, N_DEV):
            acc = acc + s_ref[k]
        o_ref[...] = acc

    return pl.pallas_call(
        body, name=name, out_shape=jax.ShapeDtypeStruct((r, c), F32),
        in_specs=[VMEM_SPEC], out_specs=VMEM_SPEC,
        compiler_params=pltpu.CompilerParams(vmem_limit_bytes=VMEM_LIMIT),
    )(slots)


def _after(token, val):
    return val + token[0, 0]


def _zero_of(v):
    bits = lax.bitcast_convert_type(v, jnp.uint16 if v.dtype == BF16 else jnp.uint32)
    return jnp.where((bits | 1) == 0, 1.0, 0.0).astype(F32)


def _pad_rows(a, rows):
    return jnp.pad(a, ((0, rows - a.shape[0]), (0, 0)))


def kernel(x, c, ctx, c_ctx, ada_w, ada_b, norm_g, ev_w_in, ev_q_norm, ev_k_norm, ev_dw_w, ev_dw_b, ev_ln_g, ev_ln_b, ev_w_out, od_w_in, od_ln_g, od_ln_b, od_ws, od_bs, od_w_out, final_g, loss_target, m_c_ctx, m_ada_w, m_ada_b, m_norm_g, m_ev_w_in, m_ev_q_norm, m_ev_k_norm, m_ev_dw_w, m_ev_dw_b, m_ev_ln_g, m_ev_ln_b, m_ev_w_out, m_od_w_in, m_od_ln_g, m_od_ln_b, m_od_ws, m_od_bs, m_od_w_out, m_final_g, v_c_ctx, v_ada_w, v_ada_b, v_norm_g, v_ev_w_in, v_ev_q_norm, v_ev_k_norm, v_ev_dw_w, v_ev_dw_b, v_ev_ln_g, v_ev_ln_b, v_ev_w_out, v_od_w_in, v_od_ln_g, v_od_ln_b, v_od_ws, v_od_bs, v_od_w_out, v_final_g):
    n, d = x.shape[1], x.shape[2]
    lc = ctx.shape[1]
    ev_in = ev_w_in.shape[2] * N_DEV
    od_in = od_w_in.shape[2] * N_DEV
    conv_ch = ev_dw_w.shape[2] * N_DEV
    ada_cols = ada_w.shape[2]
    me = 4 * lax.axis_index("x") + 2 * lax.axis_index("y") + lax.axis_index("c")
    xs, tgt, ctxs = x[0], loss_target[0], ctx[0]
    za_block = (2 * KV_W + ATTN_W) // HEAD_DIM
    glu_block = (2 * KV_W + 2 * ATTN_W) // HALF
    zb_block = glu_block + 2 * conv_ch // HALF

    small = jnp.concatenate([
        jax.nn.silu(c).reshape(1, d),
        od_ln_g.reshape(1, -1), od_ln_b.reshape(1, -1)], axis=1)
    small = _pad_rows(small, 8)
    dw_rows = _pad_rows(ev_dw_w[0], 32)
    small_g, dw_g = all_gather([small, dw_rows], name="gather_small", hbm=False)
    sc_all = small_g[:, 0, :d]
    shard = d // N_DEV
    od_ln_g_full = small_g[:, 0, d:d + shard].reshape(d)
    od_ln_b_full = small_g[:, 0, d + shard:d + 2 * shard].reshape(d)
    dw_w_full = jnp.moveaxis(dw_g, 0, 1).reshape(32, conv_ch)[:CONV_WIDTH]
    scc = jax.nn.silu(c_ctx)
    sc16 = _pad_rows(jnp.concatenate([sc_all, scc.reshape(1, d)], axis=0), 16)

    ada_bf = ada_w.astype(BF16)
    mod_loc = [matmul(sc16, ada_bf[l], mode="nn", tm=16, tn=ada_cols, tk=d, out_dtype=F32, name=f"ada_mod{l}")
               for l in range(2)]
    (mod_g,) = all_gather([jnp.stack(mod_loc)], name="gather_mod", hbm=False)
    mod_all = jnp.moveaxis(mod_g, 0, 2).reshape(2, 16, N_DEV * ada_cols) + ada_b[:, None, :]
    mod_me = lax.dynamic_index_in_dim(mod_all, me, axis=1, keepdims=False)
    shift = [mod_me[l, :d] for l in range(2)]
    scale = [mod_me[l, d:2 * d] for l in range(2)]
    gate = [mod_me[l, 2 * d:] for l in range(2)]
    shift_c, scale_c = mod_all[0, 8, :d], mod_all[0, 8, d:2 * d]

    h_w_in0, near_token = staged_gather_start(1, (ev_w_in[0] + _zero_of(mod_g[0, 0, 0, 0])).astype(BF16), None,
                                              name="gather_w_in0_chips_start")

    cexp, sexp = rope_tables(n)
    h0 = norm_mod_fwd(xs, norm_g[0], _after(near_token, shift[0]), scale[0], name="norm_mod_fwd0")
    hc = norm_mod_fwd(ctxs, norm_g[0], shift_c, scale_c, name="norm_mod_fwd_ctx")

    wi0_shard, wi0_land = staged_gather_wait(1, h_w_in0, h0, name="gather_w_in0_chips_wait")
    h_w_in0, far_token = staged_gather_start(2, wi0_shard, wi0_land, name="gather_w_in0_cores_start")
    wi0_shard, wi0_land = staged_gather_wait(2, h_w_in0, far_token, name="gather_w_in0_cores_wait")
    wi0_g = lax.dynamic_update_slice_in_dim(wi0_land, wi0_shard[None], me, axis=0)
    wi0 = jnp.moveaxis(wi0_g, 0, 1).reshape(d, ev_in)
    landed_zero = _zero_of(wi0_g[0, 0, 0])
    later = [(w[0] + landed_zero).astype(BF16) for w in (ev_w_out, od_w_in, od_w_out)]
    (h_wo0, h_wi1, h_wo1), w_token = exchange_start(later, scatter=False, name="gather_rest_start")

    def landed(handle, after, name):
        own, land = exchange_wait(handle, after, scatter=False, name=name)
        return lax.dynamic_update_slice_in_dim(land, own[None], me, axis=0)

    p0 = matmul(h0, wi0, mode="nn", tm=1024, tn=ev_in // 4, tk=d, out_dtype=F32, name="proj_in0", after=w_token)
    pc = matmul(hc, wi0, mode="nn", tm=lc, tn=2 * KV_W, tk=d, out_dtype=F32, n_out=2 * KV_W, name="proj_in_ctx")
    q_r, k_all, v_ext = qkv_prep_fwd(p0, ev_q_norm[0], ev_k_norm[0], cexp, sexp, latent=True, name="qkv_prep",
                                     rows_all=n + lc)
    k_all, v_ext = qkv_prep_fwd(pc, None, ev_k_norm[0], None, None, latent=False, name="kv_prep_ctx",
                                rows_all=n + lc, k_all=k_all, v_ext=v_ext)
    o_attn, mix_a, lse = flash_fwd(q_r, k_all, v_ext, p0, za_block=za_block, mix_width=ATTN_W + conv_ch,
                                   name="flash_fwd")
    ycv, mix0 = conv_fwd(p0, dw_w_full, ev_dw_b[0], ev_ln_g[0], ev_ln_b[0], mix_a, glu_block=glu_block,
                         name="conv_fwd")
    wo0 = landed(h_wo0, ycv, "gather_w_out0_wait").reshape(-1, d)
    x1, r0, h1 = matmul(mix0, wo0, mode="nn", tm=512, tn=d, tk=mix0.shape[1], out_dtype=F32, name="proj_out0",
                        res=xs, gate=_row(gate[0]), norm=(norm_g[1], shift[1], scale[1]))

    wi1 =jnp.moveaxis(landed(h_wi1, h1, "gather_w_in1_wait"), 0, 1).reshape(d, od_in)
    p1 = matmul(h1, wi1, mode="nn", tm=1024, tn=od_in // 4, tk=d, out_dtype=F32, name="proj_in1")
    ws_bf = od_ws[0]
    bs_t = od_bs[0].T
    m1 = sgu_fwd(p1, od_ln_g_full, od_ln_b_full, ws_bf, bs_t, name="sgu_fwd")
    wo1 = landed(h_wo1, m1, "gather_w_out1_wait").reshape(-1, d)
    x2, r1 = matmul(m1, wo1, mode="nn", tm=1024, tn=1024, tk=m1.shape[1], out_dtype=F32, name="proj_out1",
                    res=x1, gate=_row(gate[1]))

    dx2, dr1, acc_final, loss_tile = final_loss(x2, tgt, final_g, r1, gate[1], name="final_loss")

    dm1 = matmul(dr1, wo1, mode="nt", tm=1024, tn=1024, tk=d, out_dtype=F32, name="d_mix1")
    dwo1 = matmul(m1, dr1, mode="tn", tm=1024, tn=1024, tk=2048, out_dtype=BF16, name="d_wout1")
    dp1, dws, dbs_t, acc_sgu = sgu_bwd(p1, dm1, od_ln_g_full, od_ln_b_full, ws_bf, jnp.swapaxes(ws_bf, 1, 2), bs_t,
                                       name="sgu_bwd")
    dwi1_s = matmul(h1, dp1, mode="tn", tm=2048, tn=od_in // N_DEV, tk=2048, out_dtype=BF16, name="d_win1",
                    split_out=True)
    dwo1_s = dwo1.reshape(N_DEV, -1, d)
    (h_gi1, h_go1), g1_token = exchange_start([dwi1_s, dwo1_s], scatter=True, name="grads1_start")
    dh1 = matmul(dp1, wi1, mode="nt", tm=1024, tn=512, tk=od_in, out_dtype=F32, name="d_h1")
    dx1, dr0, acc_norm1 = norm_mod_bwd(x1, dh1, dx2, norm_g[1], _after(g1_token, scale[1]), name="norm_mod_bwd1",
                                       branch=(r0, gate[0]))

    dmix0 = matmul(dr0, wo0, mode="nt", tm=1024, tn=1024, tk=d, out_dtype=F32, name="d_mix0")
    dwo0 = matmul(mix0, dr0, mode="tn", tm=1024, tn=1024, tk=2048, out_dtype=BF16, name="d_wout0")
    do_attn, do_s, dza, delta = attn_gate_bwd(dmix0, o_attn, p0, za_block=za_block, name="attn_gate_bwd")
    dycv, dzb, acc_ln = conv_bwd_rows(dmix0, ycv, p0, ev_ln_g[0], ev_ln_b[0], mix_block=ATTN_W // conv_ch,
                                      zb_block=zb_block, name="conv_bwd_rows")
    dglu, ddw_w, acc_dwb = conv_bwd_taps(dycv, p0, dw_w_full, glu_block=glu_block, name="conv_bwd_taps")
    dq_r, dk_all, dv_all = flash_bwd(q_r, do_attn, do_s, lse, delta, k_all, v_ext, name="flash_bwd")
    dkvq, acc_qk = qkv_prep_bwd(p0, dq_r, dk_all[:n], dv_all[:n], ev_q_norm[0], ev_k_norm[0], cexp, sexp,
                                latent=True, name="qkv_prep_bwd")
    dpc, acc_kc = qkv_prep_bwd(pc, None, dk_all[n:], dv_all[n:], None, ev_k_norm[0], None, None,
                               latent=False, name="kv_prep_ctx_bwd")
    dp0 = jnp.concatenate([dkvq, dza, dglu, dzb], axis=1)
    dwi0 = matmul(h0, dp0, mode="tn", tm=1024, tn=ev_in // 4, tk=2048, out_dtype=F32, name="d_win0")
    dwi0 = matmul(hc, dpc, mode="tn", tm=512, tn=2 * KV_W, tk=lc, out_dtype=F32, name="d_win0_ctx", add=dwi0)
    dwi0_s = jnp.moveaxis(dwi0.astype(BF16).reshape(d, N_DEV, ev_in // N_DEV), 1, 0)
    dwo0_s = dwo0.reshape(N_DEV, -1, d)
    (h_gi0, h_go0), g0_token = exchange_start([dwi0_s, dwo0_s], scatter=True, name="grads0_start")
    dh0 = matmul(dp0, wi0, mode="nt", tm=1024, tn=512, tk=ev_in, out_dtype=F32, name="d_h0", after=g0_token)
    dhc = matmul(dpc, wi0, mode="nt", tm=lc, tn=512, tk=2 * KV_W, out_dtype=F32, name="d_h_ctx")
    grad_x, acc_norm0 = norm_mod_bwd(xs, dh0, dx1, norm_g[0], _after(g0_token, scale[0]), name="norm_mod_bwd0")
    _, acc_normc = norm_mod_bwd(ctxs, dhc, None, norm_g[0], scale_c, name="norm_mod_bwd_ctx")

    zeros_d = jnp.zeros((d,), F32)
    dmod0 = jnp.stack([acc_norm0[0], acc_norm0[1], acc_norm1[GATE_ROW]])
    dmod1 = jnp.stack([acc_norm1[0], acc_norm1[1], acc_final[GATE_ROW]])
    dmodc = jnp.stack([acc_normc[0], acc_normc[1]])
    half_pad = jnp.zeros((d - 2 * conv_ch,), F32) if d > 2 * conv_ch else jnp.zeros((0,), F32)
    row_a = jnp.concatenate([acc_dwb[0], acc_ln[0], half_pad])
    row_b = jnp.concatenate([acc_ln[1], acc_qk[0], acc_qk[1] + acc_kc[1],
                             jnp.zeros((d - conv_ch - 2 * HEAD_DIM,), F32)])
    row_c = jnp.concatenate([dbs_t.T.reshape(-1), jnp.zeros((d - SGU_GROUPS * CHUNK,), F32)])
    row_loss = jnp.concatenate([loss_tile[0, :1], jnp.zeros((d - 1,), F32)])
    pack = jnp.concatenate([
        dmod0, dmod1, dmodc,
        (acc_norm0[2] + acc_normc[2])[None], acc_norm1[2][None],
        acc_final[0][None],
        acc_sgu[0][None], acc_sgu[1][None],
        row_a[None], row_b[None], row_c[None], row_loss[None],
        ddw_w.reshape(-1, d),
        dws.reshape(-1, d),
    ], axis=0)
    n_rows = pack.shape[0]
    pack = _pad_rows(pack, -(-n_rows // 8) * 8)
    (h_pack,), pack_token = exchange_start([pack], scatter=False, name="small_grads_start")

    def summands(handle, after, name):
        mine, land = exchange_wait(handle, after, scatter=True, name=name)
        own = lax.dynamic_index_in_dim(mine, me, axis=0, keepdims=True)
        return lax.dynamic_update_slice_in_dim(land, own, me, axis=0)

    out = {}

    def upd(key, w, g, m, v, slots=False, rows=512):
        shp = w.shape
        w2 = w.reshape(-1, shp[-1])
        g2 = g.reshape((N_DEV, -1, shp[-1])) if slots else g.reshape(-1, shp[-1])
        res = adamw(w2, g2, m.reshape(w2.shape), v.reshape(w2.shape), name="adamw_" + key, slots=slots, rows=rows)
        out[key] = tuple(r.reshape(shp) for r in res)

    gi1 = summands(h_gi1, pack_token, "grads_w_in1_wait")
    upd("od_w_in", od_w_in, gi1, m_od_w_in, v_od_w_in, slots=True, rows=256)
    go1 = summands(h_go1, out["od_w_in"][1], "grads_w_out1_wait")
    upd("od_w_out", od_w_out, go1, m_od_w_out, v_od_w_out, slots=True, rows=256)
    go0 = summands(h_go0, out["od_w_out"][1], "grads_w_out0_wait")
    upd("ev_w_out", ev_w_out, go0, m_ev_w_out, v_ev_w_out, slots=True, rows=256)
    gi0 = summands(h_gi0, out["ev_w_out"][1], "grads_w_in0_wait")
    upd("ev_w_in", ev_w_in, gi0, m_ev_w_in, v_ev_w_in, slots=True, rows=256)

    pack_own, pack_land = exchange_wait(h_pack, out["ev_w_in"][1], scatter=False, name="small_grads_wait")
    pack_g = lax.dynamic_update_slice_in_dim(pack_land, pack_own[None], me, axis=0)
    gsum = slot_sum(pack_g, name="sum_small_grads")
    loss = gsum[16, 0]
    dw_rows_n = 32 * conv_ch // d
    g_dw_w = gsum[17:17 + dw_rows_n].reshape(32, conv_ch)[:CONV_WIDTH]
    g_od_ws = gsum[17 + dw_rows_n:17 + dw_rows_n + SGU_GROUPS * CHUNK * CHUNK // d].reshape(od_ws.shape)

    dmodc_sum = jnp.concatenate([gsum[6], gsum[7], zeros_d])
    col0 = me * ada_cols
    dm_cols = []
    for l in range(2):
        rows = pack_g[:, 3 * l:3 * l + 3, :].reshape(N_DEV, 3 * d)
        extra = dmodc_sum[None] if l == 0 else jnp.zeros((1, 3 * d), F32)
        full = _pad_rows(jnp.concatenate([rows, extra], axis=0), 16)
        dm_cols.append(lax.dynamic_slice_in_dim(full, col0, ada_cols, axis=1))
    g_ada_w = jnp.stack([matmul(sc16, dm_cols[l], mode="tn", tm=512, tn=ada_cols, tk=16, out_dtype=F32,
                                name=f"d_ada_w{l}") for l in range(2)])
    dsc = matmul(dm_cols[0], ada_bf[0], mode="nt", tm=16, tn=512, tk=ada_cols, out_dtype=F32, name="d_scc")
    (_, dscc_sum) = all_gather([dsc[8:16]], name="gather_dscc", hbm=False, sum_out=True)
    sg = jax.nn.sigmoid(c_ctx)
    g_c_ctx = dscc_sum[0] * (sg * (1.0 + c_ctx * (1.0 - sg)))
    g_ada_b = jnp.stack([gsum[0:3].reshape(-1) + dmodc_sum, gsum[3:6].reshape(-1)])

    upd("ada_w", ada_w, g_ada_w, m_ada_w, v_ada_w)

    def my_shard(full, size):
        return lax.dynamic_slice_in_dim(full, me * size, size, axis=full.ndim - 1)

    small_items = [
        ("c_ctx", c_ctx, g_c_ctx, m_c_ctx, v_c_ctx),
        ("ada_b", ada_b, g_ada_b, m_ada_b, v_ada_b),
        ("norm_g", norm_g, gsum[8:10], m_norm_g, v_norm_g),
        ("ev_q_norm", ev_q_norm, gsum[14, conv_ch:conv_ch + HEAD_DIM], m_ev_q_norm, v_ev_q_norm),
        ("ev_k_norm", ev_k_norm, gsum[14, conv_ch + HEAD_DIM:conv_ch + 2 * HEAD_DIM], m_ev_k_norm, v_ev_k_norm),
        ("ev_dw_w", ev_dw_w, my_shard(g_dw_w, conv_ch // N_DEV), m_ev_dw_w, v_ev_dw_w),
        ("ev_dw_b", ev_dw_b, gsum[13, :conv_ch], m_ev_dw_b, v_ev_dw_b),
        ("ev_ln_g", ev_ln_g, gsum[13, conv_ch:2 * conv_ch], m_ev_ln_g, v_ev_ln_g),
        ("ev_ln_b", ev_ln_b, gsum[14, :conv_ch], m_ev_ln_b, v_ev_ln_b),
        ("od_ln_g", od_ln_g, my_shard(gsum[11], shard), m_od_ln_g, v_od_ln_g),
        ("od_ln_b", od_ln_b, my_shard(gsum[12], shard), m_od_ln_b, v_od_ln_b),
        ("od_ws", od_ws, g_od_ws, m_od_ws, v_od_ws),
        ("od_bs", od_bs, gsum[15, :SGU_GROUPS * CHUNK], m_od_bs, v_od_bs),
        ("final_g", final_g, gsum[10], m_final_g, v_final_g),
    ]
    sizes = [it[1].size for it in small_items]
    total = sum(sizes)
    lanes = 1024
    prow = -(-total // lanes)
    prow = -(-prow // 8) * 8

    def pack_small(idx):
        flat = jnp.concatenate([it[idx].reshape(-1).astype(F32) for it in small_items])
        return jnp.pad(flat, (0, prow * lanes - total)).reshape(prow, lanes)

    sres = adamw(pack_small(1), pack_small(2), pack_small(3), pack_small(4), name="adamw_small", rows=prow)
    off = 0
    for it, size in zip(small_items, sizes):
        out[it[0]] = tuple(r.reshape(-1)[off:off + size].reshape(it[1].shape) for r in sres)
        off += size

    names = ['c_ctx', 'ada_w', 'ada_b', 'norm_g', 'ev_w_in', 'ev_q_norm', 'ev_k_norm', 'ev_dw_w', 'ev_dw_b',
             'ev_ln_g', 'ev_ln_b', 'ev_w_out', 'od_w_in', 'od_ln_g', 'od_ln_b', 'od_ws', 'od_bs', 'od_w_out',
             'final_g']
    return (loss, grad_x[None], *[out[k][0] for k in names], *[out[k][1] for k in names],
            *[out[k][2] for k in names], *[out[k][3] for k in names])
```

```python
import functools
import math

import jax
import jax.numpy as jnp
from jax import lax
from jax.experimental import pallas as pl
from jax.experimental.pallas import tpu as pltpu

F32 = jnp.float32
BF16 = jnp.bfloat16
MESH = pl.DeviceIdType.MESH

EPS = 1e-6
HEAD_DIM = 128
N_Q_HEADS = 8
N_KV_HEADS = 2
Q_PER_KV = N_Q_HEADS // N_KV_HEADS
ATTN_W = N_Q_HEADS * HEAD_DIM
KV_W = N_KV_HEADS * HEAD_DIM
ATTN_SCALE = HEAD_DIM ** -0.5
LN2 = math.log(2.0)
Q_SCALE = ATTN_SCALE / LN2
ROPE_THETA = 10000.0
GRID_W = 64
CONV_WIDTH = 31
CONV_HALF = CONV_WIDTH // 2
HALO = 16
CHUNK = 128
SGU_GROUPS = 8
N_DEV = 8

ADAM_LR = 0.001
ADAM_B1 = 0.9
ADAM_B2 = 0.999
ADAM_EPS = 1e-08
ADAM_WD = 0.01
ADAM_STEP = 10

VMEM_LIMIT = 56 * 1024 * 1024
ANY = pl.BlockSpec(memory_space=pl.ANY)
VMEM_SPEC = pl.BlockSpec(memory_space=pltpu.VMEM)


def _params(*sem):
    return pltpu.CompilerParams(dimension_semantics=sem, vmem_limit_bytes=VMEM_LIMIT)


def _sigmoid(x):
    return 1.0 / (1.0 + jnp.exp(-x))


def _silu(x):
    return x * _sigmoid(x)


def _dsilu(x):
    s = _sigmoid(x)
    return s * (1.0 + x * (1.0 - s))


_GELU_C = math.sqrt(2.0 / math.pi)


def _gelu(x):
    t = jnp.tanh(_GELU_C * (x + 0.044715 * (x * x * x)))
    return 0.5 * x * (1.0 + t)


def _dgelu(x):
    t = jnp.tanh(_GELU_C * (x + 0.044715 * (x * x * x)))
    return 0.5 * (1.0 + t) + 0.5 * x * (1.0 - t * t) * (_GELU_C * (1.0 + 3.0 * 0.044715 * (x * x)))


def _row(v):
    return v.reshape(1, -1).astype(F32)


def _flat_id(p):
    return 4 * p[0] + 2 * p[1] + p[2]


def _gather_body(n_arr, sum_out):
    def body(*refs):
        x_refs = refs[:n_arr]
        out_refs = refs[n_arr:2 * n_arr]
        pos = 2 * n_arr
        sum_refs = refs[pos:pos + n_arr] if sum_out else ()
        pos += n_arr if sum_out else 0
        send_sems, recv_sems, local_sems = refs[pos:pos + 3]
        x, y, c = lax.axis_index("x"), lax.axis_index("y"), lax.axis_index("c")
        me, sibling = (x, y, c), (x, y, 1 - c)
        chips = [(1 - x, y), (x, 1 - y), (1 - x, 1 - y)]

        def copy(a, k, block, to, src=None):
            rows = out_refs[a].at[_flat_id(block)]
            return pltpu.make_async_remote_copy(
                src_ref=rows if src is None else src, dst_ref=rows,
                send_sem=send_sems.at[a, k], recv_sem=recv_sems.at[a, k],
                device_id=to, device_id_type=MESH)

        sends = []
        mine = []
        for a in range(n_arr):
            cp = pltpu.make_async_copy(x_refs[a], out_refs[a].at[_flat_id(me)], local_sems.at[a])
            cp.start()
            mine.append(cp)
            first = [copy(a, 0, me, sibling, src=x_refs[a])]
            first += [copy(a, 1 + j, me, (*chip, c), src=x_refs[a]) for j, chip in enumerate(chips)]
            for cp in first:
                cp.start()
            sends += first
        for a in range(n_arr):
            for j, chip in enumerate(chips):
                copy(a, 1 + j, (*chip, c), me).wait_recv()
                fwd = copy(a, 4 + j, (*chip, c), sibling)
                fwd.start()
                sends.append(fwd)
        for a in range(n_arr):
            copy(a, 0, sibling, me).wait_recv()
            for j, chip in enumerate(chips):
                copy(a, 4 + j, (*chip, 1 - c), me).wait_recv()
        for cp in sends:
            cp.wait_send()
        for cp in mine:
            cp.wait()
        if sum_out:
            for a in range(n_arr):
                acc = out_refs[a][0]
                for k in range(1, N_DEV):
                    acc = acc + out_refs[a][k]
                sum_refs[a][...] = acc

    return body


def all_gather(arrs, *, name, hbm, sum_out=False):
    n = len(arrs)
    spec = ANY if hbm else VMEM_SPEC
    out_shape = [jax.ShapeDtypeStruct((N_DEV,) + a.shape, a.dtype) for a in arrs]
    out_specs = [spec] * n
    if sum_out:
        out_shape += [jax.ShapeDtypeStruct(a.shape, a.dtype) for a in arrs]
        out_specs += [VMEM_SPEC] * n
    res = pl.pallas_call(
        _gather_body(n, sum_out), name=name,
        out_shape=out_shape, in_specs=[spec] * n, out_specs=out_specs,
        scratch_shapes=[pltpu.SemaphoreType.DMA((n, 7)), pltpu.SemaphoreType.DMA((n, 7)),
                        pltpu.SemaphoreType.DMA((n,))],
        compiler_params=pltpu.CompilerParams(vmem_limit_bytes=VMEM_LIMIT),
    )(*arrs)
    return res


_RELATIONS = [(rx, ry, rc) for rx in (0, 1) for ry in (0, 1) for rc in (0, 1)][1:]


HBM_SPEC = pl.BlockSpec(memory_space=pltpu.HBM)
SEM_SPEC = pl.BlockSpec(memory_space=pltpu.SEMAPHORE)
_DATAFLOW = pltpu.SideEffectType.DATAFLOW_SIDE_EFFECTING
N_PEERS = N_DEV - 1


def _peer_copy(src_ref, land_ref, send_sem, recv_sem, k, rel, scatter, sending):
    x, y, c = lax.axis_index("x"), lax.axis_index("y"), lax.axis_index("c")
    rx, ry, rc = rel
    peer = (1 - x if rx else x, 1 - y if ry else y, 1 - c if rc else c)
    src = src_ref.at[_flat_id(peer)] if scatter else src_ref
    dst = land_ref.at[_flat_id((x, y, c)) if sending else _flat_id(peer)]
    return pltpu.make_async_remote_copy(src_ref=src, dst_ref=dst, send_sem=send_sem.at[k], recv_sem=recv_sem.at[k],
                                        device_id=peer, device_id_type=MESH)


def exchange_start(arrs, *, scatter, name):
    n = len(arrs)
    lands = [lax.empty((N_DEV,) + (a.shape[1:] if scatter else a.shape), a.dtype) for a in arrs]

    def body(*refs):
        srcs, lnds, sems = refs[:n], refs[n:2 * n], refs[2 * n:4 * n]
        token = refs[6 * n]
        for a in range(n):
            for k, rel in enumerate(_RELATIONS):
                _peer_copy(srcs[a], lnds[a], sems[2 * a], sems[2 * a + 1], k, rel, scatter, True).start()
        token[...] = jnp.zeros_like(token)

    outs = pl.pallas_call(
        body, name=name,
        out_shape=[pltpu.SemaphoreType.DMA((N_PEERS,))] * (2 * n)
        + [pltpu.HBM(a.shape, a.dtype) for a in arrs] + [pltpu.HBM(l.shape, l.dtype) for l in lands]
        + [jax.ShapeDtypeStruct((8, 128), F32)],
        in_specs=[HBM_SPEC] * (2 * n),
        out_specs=[SEM_SPEC] * (2 * n) + [HBM_SPEC] * (2 * n) + [VMEM_SPEC],
        input_output_aliases={i: 2 * n + i for i in range(2 * n)},
        compiler_params=pltpu.CompilerParams(has_side_effects=_DATAFLOW),
    )(*[pltpu.with_memory_space_constraint(a, pltpu.HBM) for a in arrs],
      *[pltpu.with_memory_space_constraint(l, pltpu.HBM) for l in lands])
    handles = [(outs[2 * a], outs[2 * a + 1], outs[2 * n + a], outs[3 * n + a]) for a in range(n)]
    return handles, outs[4 * n]


def exchange_wait(handle, after, *, scatter, name):
    send_sem, recv_sem, src, land = handle

    def body(src_ref, land_ref, send_ref, recv_ref, after_ref, src_out, land_out):
        for k, rel in enumerate(_RELATIONS):
            cp = _peer_copy(src_ref, land_ref, send_ref, recv_ref, k, rel, scatter, False)
            cp.wait_send()
            cp.wait_recv()

    outs = pl.pallas_call(
        body, name=name,
        out_shape=[pltpu.HBM(src.shape, src.dtype), pltpu.HBM(land.shape, land.dtype)],
        in_specs=[HBM_SPEC, HBM_SPEC, SEM_SPEC, SEM_SPEC, ANY],
        out_specs=[HBM_SPEC, HBM_SPEC], input_output_aliases={0: 0, 1: 1},
        compiler_params=pltpu.CompilerParams(has_side_effects=_DATAFLOW),
    )(src, land, send_sem, recv_sem, after)
    return outs[0], outs[1]


_STAGE1 = [(0, 0, 1), (1, 0, 0), (0, 1, 0), (1, 1, 0)]
_OTHER_CHIPS = [(1, 0), (0, 1), (1, 1)]


def _stage_copy(stage, k, shard_ref, land_ref, send_sem, recv_sem, sending):
    x, y, c = lax.axis_index("x"), lax.axis_index("y"), lax.axis_index("c")
    if stage == 1:
        rx, ry, rc = _STAGE1[k]
        peer = (1 - x if rx else x, 1 - y if ry else y, 1 - c if rc else c)
        src = shard_ref
        dst = land_ref.at[_flat_id((x, y, c)) if sending else _flat_id(peer)]
    else:
        cx, cy = _OTHER_CHIPS[k]
        peer = (x, y, 1 - c)
        chip = (1 - x if cx else x, 1 - y if cy else y)
        slot = _flat_id((*chip, c)) if sending else _flat_id((*chip, 1 - c))
        src = land_ref.at[_flat_id((*chip, c))]
        dst = land_ref.at[slot]
    return pltpu.make_async_remote_copy(src_ref=src, dst_ref=dst, send_sem=send_sem.at[k], recv_sem=recv_sem.at[k],
                                        device_id=peer, device_id_type=MESH)


def staged_gather_start(stage, shard, land, *, name):
    n_copies = len(_STAGE1) if stage == 1 else len(_OTHER_CHIPS)
    if land is None:
        land = lax.empty((N_DEV,) + shard.shape, shard.dtype)

    def body(shard_ref, land_ref, send_sem, recv_sem, shard_thru, land_thru, token):
        for k in range(n_copies):
            _stage_copy(stage, k, shard_ref, land_ref, send_sem, recv_sem, True).start()
        token[...] = jnp.zeros_like(token)

    outs = pl.pallas_call(
        body, name=name,
        out_shape=[pltpu.SemaphoreType.DMA((n_copies,)), pltpu.SemaphoreType.DMA((n_copies,)),
                   pltpu.HBM(shard.shape, shard.dtype), pltpu.HBM(land.shape, land.dtype),
                   jax.ShapeDtypeStruct((8, 128), F32)],
        in_specs=[HBM_SPEC, HBM_SPEC],
        out_specs=[SEM_SPEC, SEM_SPEC, HBM_SPEC, HBM_SPEC, VMEM_SPEC],
        input_output_aliases={0: 2, 1: 3},
        compiler_params=pltpu.CompilerParams(has_side_effects=_DATAFLOW),
    )(pltpu.with_memory_space_constraint(shard, pltpu.HBM), pltpu.with_memory_space_constraint(land, pltpu.HBM))
    return outs[:4], outs[4]


def staged_gather_wait(stage, handle, after, *, name):
    send_sem, recv_sem, shard, land = handle
    n_copies = len(_STAGE1) if stage == 1 else len(_OTHER_CHIPS)

    def body(shard_ref, land_ref, send_ref, recv_ref, after_ref, shard_out, land_out):
        for k in range(n_copies):
            cp = _stage_copy(stage, k, shard_ref, land_ref, send_ref, recv_ref, False)
            cp.wait_send()
            cp.wait_recv()

    outs = pl.pallas_call(
        body, name=name,
        out_shape=[pltpu.HBM(shard.shape, shard.dtype), pltpu.HBM(land.shape, land.dtype)],
        in_specs=[HBM_SPEC, HBM_SPEC, SEM_SPEC, SEM_SPEC, ANY],
        out_specs=[HBM_SPEC, HBM_SPEC], input_output_aliases={0: 0, 1: 1},
        compiler_params=pltpu.CompilerParams(has_side_effects=_DATAFLOW),
    )(shard, land, send_sem, recv_sem, after)
    return outs[0], outs[1]


_DIMS = {"nn": (((1,), (0,)), ((), ())), "nt": (((1,), (1,)), ((), ())), "tn": (((0,), (0,)), ((), ()))}


def matmul(a, b, *, mode, tm, tn, tk, out_dtype, name, n_out=None, res=None, gate=None, add=None, after=None,
           split_out=False, norm=None):
    if mode == "tn":
        kdim, m = a.shape
    else:
        m, kdim = a.shape
    nfull = b.shape[0] if mode == "nt" else b.shape[1]
    n = nfull if n_out is None else n_out
    tm, tn, tk = min(tm, m), min(tn, n), min(tk, kdim)
    assert m % tm == 0 and n % tn == 0 and kdim % tk == 0, (name, m, n, kdim, tm, tn, tk)
    nk = kdim // tk
    dims = _DIMS[mode]
    a_spec = pl.BlockSpec((tk, tm), lambda j, i, k: (k, i)) if mode == "tn" else pl.BlockSpec((tm, tk), lambda j, i, k: (i, k))
    b_spec = pl.BlockSpec((tn, tk), lambda j, i, k: (j, k)) if mode == "nt" else pl.BlockSpec((tk, tn), lambda j, i, k: (k, j))
    o_spec = pl.BlockSpec((tm, tn), lambda j, i, k: (i, j))
    in_specs = [a_spec, b_spec]
    operands = [a, b]
    aliases = {}
    vec = pl.BlockSpec((1, tn), lambda j, i, k: (0, j))
    if res is not None:
        in_specs += [o_spec, vec]
        operands += [res, gate]
    if norm is not None:
        assert res is not None and tn == n, (name, tn, n)
        in_specs += [vec, vec, vec]
        operands += [_row(v) for v in norm]
    if add is not None:
        in_specs += [o_spec]
        aliases = {len(operands): 0}
        operands += [add]
    if after is not None:
        in_specs += [ANY]
        operands += [after]
    out_cols = n if add is None else add.shape[1]
    out_shape = [jax.ShapeDtypeStruct((m, out_cols), out_dtype)]
    out_specs = [o_spec]
    if split_out:
        out_shape = [jax.ShapeDtypeStruct((n // tn, m, tn), out_dtype)]
        out_specs = [pl.BlockSpec((None, tm, tn), lambda j, i, k: (j, i, 0))]
    if res is not None:
        out_shape.append(jax.ShapeDtypeStruct((m, n), BF16))
        out_specs.append(o_spec)
    if norm is not None:
        out_shape.append(jax.ShapeDtypeStruct((m, n), BF16))
        out_specs.append(o_spec)

    def body(*refs):
        a_ref, b_ref = refs[:2]
        pos = 2
        if res is not None:
            res_ref, gate_ref = refs[pos:pos + 2]
            pos += 2
        if norm is not None:
            ng_ref, nshift_ref, nscale_ref = refs[pos:pos + 3]
            pos += 3
        if add is not None:
            add_ref = refs[pos]
            pos += 1
        if after is not None:
            pos += 1
        o_ref = refs[pos]
        pos += 1
        if res is not None:
            r_ref = refs[pos]
            pos += 1
        if norm is not None:
            h_ref = refs[pos]
            pos += 1
        acc_ref = refs[pos] if nk > 1 else None
        prod = lax.dot_general(a_ref[...].astype(BF16), b_ref[...].astype(BF16), dims,
                               preferred_element_type=F32)

        def finish(acc):
            if res is not None:
                xnew = res_ref[...] + gate_ref[...] * acc
                o_ref[...] = xnew.astype(out_dtype)
                r_ref[...] = acc.astype(BF16)
                if norm is not None:
                    rstd = lax.rsqrt(jnp.mean(xnew * xnew, axis=-1, keepdims=True) + EPS)
                    y = xnew * rstd * ng_ref[...]
                    h_ref[...] = (y * (1.0 + nscale_ref[...]) + nshift_ref[...]).astype(BF16)
            elif add is not None:
                o_ref[...] = (add_ref[...] + acc).astype(out_dtype)
            else:
                o_ref[...] = acc.astype(out_dtype)

        if nk == 1:
            finish(prod)
        else:
            k = pl.program_id(2)

            @pl.when(k == 0)
            def _():
                acc_ref[...] = prod

            @pl.when(k > 0)
            def _():
                acc_ref[...] += prod

            @pl.when(k == nk - 1)
            def _():
                finish(acc_ref[...])

    outs = pl.pallas_call(
        body, name=name, grid=(n // tn, m // tm, nk),
        in_specs=in_specs, out_specs=out_specs, out_shape=out_shape,
        scratch_shapes=[pltpu.VMEM((tm, tn), F32)] if nk > 1 else [],
        input_output_aliases=aliases,
        compiler_params=_params("parallel", "parallel", "arbitrary"),
    )(*operands)
    return outs if res is not None else outs[0]


def _rows_tile(n, want):
    return min(n, want)


def norm_mod_fwd(x, g, shift, scale, *, name):
    n, d = x.shape
    tt = _rows_tile(n, 256)

    def body(x_ref, g_ref, sh_ref, sc_ref, h_ref):
        xv = x_ref[...]
        rstd = lax.rsqrt(jnp.mean(xv * xv, axis=-1, keepdims=True) + EPS)
        y = xv * rstd * g_ref[...]
        h_ref[...] = (y * (1.0 + sc_ref[...]) + sh_ref[...]).astype(BF16)

    vec = pl.BlockSpec((1, d), lambda i: (0, 0))
    return pl.pallas_call(
        body, name=name, grid=(n // tt,),
        in_specs=[pl.BlockSpec((tt, d), lambda i: (i, 0)), vec, vec, vec],
        out_specs=pl.BlockSpec((tt, d), lambda i: (i, 0)),
        out_shape=jax.ShapeDtypeStruct((n, d), BF16),
        compiler_params=_params("parallel"),
    )(x, _row(g), _row(shift), _row(scale))


GATE_ROW = 3


def norm_mod_bwd(x, dh, dres, g, scale, *, name, branch=None):
    n, d = x.shape
    tt = _rows_tile(n, 256)
    has_res = dres is not None
    has_branch = branch is not None
    last = n // tt - 1

    def body(*refs):
        x_ref, dh_ref = refs[:2]
        pos = 2
        if has_res:
            dres_ref = refs[pos]
            pos += 1
        if has_branch:
            r_ref, gate_ref = refs[pos:pos + 2]
            pos += 2
        g_ref, sc_ref, dx_ref = refs[pos:pos + 3]
        pos += 3
        if has_branch:
            dr_ref = refs[pos]
            pos += 1
        acc_ref, s_ref = refs[pos:pos + 2]
        i = pl.program_id(0)
        xv = x_ref[...]
        dhv = dh_ref[...]
        rstd = lax.rsqrt(jnp.mean(xv * xv, axis=-1, keepdims=True) + EPS)
        xhat = xv * rstd
        dxhat = dhv * (g_ref[...] * (1.0 + sc_ref[...]))
        dx = rstd * (dxhat - xhat * jnp.mean(dxhat * xhat, axis=-1, keepdims=True))
        if has_res:
            dx = dx + dres_ref[...]
        dx_ref[...] = dx

        @pl.when(i == 0)
        def _():
            s_ref[...] = jnp.zeros_like(s_ref)

        s_ref[0:1, :] += jnp.sum(dhv, axis=0, keepdims=True)
        s_ref[1:2, :] += jnp.sum(dhv * xhat, axis=0, keepdims=True)
        if has_branch:
            dr_ref[...] = (dx * gate_ref[...]).astype(BF16)
            s_ref[2:3, :] += jnp.sum(dx * r_ref[...].astype(F32), axis=0, keepdims=True)

        @pl.when(i == last)
        def _():
            s1 = s_ref[0:1, :]
            s2 = s_ref[1:2, :]
            acc_ref[...] = jnp.zeros_like(acc_ref)
            acc_ref[0:1, :] = s1
            acc_ref[1:2, :] = s2 * g_ref[...]
            acc_ref[2:3, :] = s2 * (1.0 + sc_ref[...])
            acc_ref[GATE_ROW:GATE_ROW + 1, :] = s_ref[2:3, :]

    vec = pl.BlockSpec((1, d), lambda i: (0, 0))
    big = pl.BlockSpec((tt, d), lambda i: (i, 0))
    ops = [x, dh] + ([dres] if has_res else []) + ([branch[0], _row(branch[1])] if has_branch else [])
    ops += [_row(g), _row(scale)]
    return pl.pallas_call(
        body, name=name, grid=(n // tt,),
        in_specs=[big, big] + ([big] if has_res else []) + ([big, vec] if has_branch else []) + [vec, vec],
        out_specs=[big] + ([big] if has_branch else []) + [pl.BlockSpec((8, d), lambda i: (0, 0))],
        out_shape=[jax.ShapeDtypeStruct((n, d), F32)] + ([jax.ShapeDtypeStruct((n, d), BF16)] if has_branch else [])
        + [jax.ShapeDtypeStruct((8, d), F32)],
        scratch_shapes=[pltpu.VMEM((8, d), F32)],
        compiler_params=_params("arbitrary"),
    )(*ops)


def proj_out_loss(a, b, x_prev, gate, target, g, *, name):
    n, kdim = a.shape
    d = b.shape[1]
    tt = _rows_tile(n, 256)

    def body(a_ref, b_ref, x_ref, t_ref, g_ref, gate_ref, dx_ref, dr_ref, acc_ref, loss_ref):
        i = pl.program_id(0)
        r = jnp.dot(a_ref[...], b_ref[...], preferred_element_type=F32)
        xv = x_ref[...] + gate_ref[...] * r
        rstd = lax.rsqrt(jnp.mean(xv * xv, axis=-1, keepdims=True) + EPS)
        xhat = xv * rstd
        e = xhat * g_ref[...] - t_ref[...]
        dy = e * (1.0 / d)
        dxhat = dy * g_ref[...]
        dx = rstd * (dxhat - xhat * jnp.mean(dxhat * xhat, axis=-1, keepdims=True))
        dx_ref[...] = dx
        dr_ref[...] = (dx * gate_ref[...]).astype(BF16)

        @pl.when(i == 0)
        def _():
            acc_ref[...] = jnp.zeros_like(acc_ref)
            loss_ref[...] = jnp.zeros_like(loss_ref)

        acc_ref[0:1, :] += jnp.sum(dy * xhat, axis=0, keepdims=True)
        acc_ref[GATE_ROW:GATE_ROW + 1, :] += jnp.sum(dx * r, axis=0, keepdims=True)
        part = 0.5 * jnp.sum(jnp.mean(e * e, axis=-1, keepdims=True), axis=0, keepdims=True)
        loss_ref[...] += jnp.broadcast_to(part, loss_ref.shape)

    big = pl.BlockSpec((tt, d), lambda i: (i, 0))
    vec = pl.BlockSpec((1, d), lambda i: (0, 0))
    return pl.pallas_call(
        body, name=name, grid=(n // tt,),
        in_specs=[pl.BlockSpec((tt, kdim), lambda i: (i, 0)), pl.BlockSpec((kdim, d), lambda i: (0, 0)),
                  big, big, vec, vec],
        out_specs=[big, big, pl.BlockSpec((8, d), lambda i: (0, 0)), pl.BlockSpec((8, 128), lambda i: (0, 0))],
        out_shape=[jax.ShapeDtypeStruct((n, d), F32), jax.ShapeDtypeStruct((n, d), BF16),
                   jax.ShapeDtypeStruct((8, d), F32), jax.ShapeDtypeStruct((8, 128), F32)],
        compiler_params=_params("arbitrary"),
    )(a, b, x_prev, target, _row(g), _row(gate))


def _as_row(col):
    t = col.shape[0]
    return jnp.transpose(jnp.broadcast_to(col, (t, HEAD_DIM)))[0:1, :]


def _swap_pairs(x):
    lane = lax.broadcasted_iota(jnp.int32, x.shape, 1)
    return jnp.where(lane % 2 == 0, pltpu.roll(x, HEAD_DIM - 1, 1), pltpu.roll(x, 1, 1))


def rope_tables(n):
    rows = n // GRID_W
    row = jnp.repeat(jnp.arange(rows, dtype=F32), GRID_W)
    col = jnp.tile(jnp.arange(GRID_W, dtype=F32), rows)
    n_freq = HEAD_DIM // 4
    inv = jnp.power(ROPE_THETA, jnp.arange(n_freq, dtype=F32) * (-2.0 / (HEAD_DIM // 2)))
    ang = jnp.concatenate([row[:, None] * inv, col[:, None] * inv], axis=-1)
    cos, sin = jnp.cos(ang), jnp.sin(ang)
    cexp = jnp.repeat(cos, 2, axis=-1)
    sexp = jnp.stack([-sin, sin], axis=-1).reshape(n, HEAD_DIM)
    return cexp, sexp


V_EXT_W = 2 * HEAD_DIM


def qkv_prep_fwd(p, wq, wk, cexp, sexp, *, latent, name, rows_all=None, k_all=None, v_ext=None):
    n = p.shape[0]
    tt = _rows_tile(n, 256)
    width = 2 * KV_W + (ATTN_W if latent else 0)

    def body(*refs):
        if latent:
            p_ref, wq_ref, wk_ref, c_ref, s_ref, q_ref, k_ref, v_ref = refs
        else:
            p_ref, wk_ref, _, _, k_ref, v_ref = refs

        def head(xv, w):
            rstd = lax.rsqrt(jnp.mean(xv * xv, axis=-1, keepdims=True) + EPS)
            yv = xv * rstd * w
            if latent:
                yv = yv * c_ref[...] + _swap_pairs(yv) * s_ref[...]
            return yv

        for h in range(N_KV_HEADS):
            sl = slice(h * HEAD_DIM, (h + 1) * HEAD_DIM)
            k_ref[:, sl] = head(p_ref[:, sl], wk_ref[...]).astype(BF16)
            v_ref[:, h * V_EXT_W:h * V_EXT_W + HEAD_DIM] = p_ref[:, KV_W + h * HEAD_DIM:KV_W + (h + 1) * HEAD_DIM].astype(BF16)
            lane = lax.broadcasted_iota(jnp.int32, (tt, HEAD_DIM), 1)
            v_ref[:, h * V_EXT_W + HEAD_DIM:(h + 1) * V_EXT_W] = jnp.where(lane == 0, 1.0, 0.0).astype(BF16)
        if latent:
            for h in range(N_Q_HEADS):
                sl = slice(2 * KV_W + h * HEAD_DIM, 2 * KV_W + (h + 1) * HEAD_DIM)
                q_ref[:, h * HEAD_DIM:(h + 1) * HEAD_DIM] = (head(p_ref[:, sl], wq_ref[...]) * Q_SCALE).astype(BF16)

    vec = pl.BlockSpec((1, HEAD_DIM), lambda i: (0, 0))
    tab = pl.BlockSpec((tt, HEAD_DIM), lambda i: (i, 0))
    vw = N_KV_HEADS * V_EXT_W
    k_shape = jax.ShapeDtypeStruct((rows_all, KV_W), BF16)
    v_shape = jax.ShapeDtypeStruct((rows_all, vw), BF16)
    aliases = {}
    if latent:
        in_specs = [pl.BlockSpec((tt, width), lambda i: (i, 0)), vec, vec, tab, tab]
        ops = [p, _row(wq), _row(wk), cexp, sexp]
        out_specs = [pl.BlockSpec((tt, ATTN_W), lambda i: (i, 0)), pl.BlockSpec((tt, KV_W), lambda i: (i, 0)),
                     pl.BlockSpec((tt, vw), lambda i: (i, 0))]
        out_shape = [jax.ShapeDtypeStruct((n, ATTN_W), BF16), k_shape, v_shape]
    else:
        assert n == tt and (rows_all - n) % n == 0, (n, tt, rows_all)
        first = (rows_all - n) // n
        in_specs = [pl.BlockSpec((tt, width), lambda i: (i, 0)), vec, ANY, ANY]
        ops = [p, _row(wk), k_all, v_ext]
        out_specs = [pl.BlockSpec((tt, KV_W), lambda i: (first, 0)), pl.BlockSpec((tt, vw), lambda i: (first, 0))]
        out_shape = [k_shape, v_shape]
        aliases = {2: 0, 3: 1}
    return pl.pallas_call(
        body, name=name, grid=(n // tt,), in_specs=in_specs, out_specs=out_specs, out_shape=out_shape,
        input_output_aliases=aliases, compiler_params=_params("parallel"),
    )(*ops)


def qkv_prep_bwd(p, dq, dk, dv, wq, wk, cexp, sexp, *, latent, name):
    n = p.shape[0]
    tt = _rows_tile(n, 256)
    width = 2 * KV_W + (ATTN_W if latent else 0)

    def body(*refs):
        if latent:
            p_ref, dq_ref, dk_ref, dv_ref, wq_ref, wk_ref, c_ref, s_ref, dp_ref, acc_ref = refs
        else:
            p_ref, dk_ref, dv_ref, wk_ref, dp_ref, acc_ref = refs
        i = pl.program_id(0)

        @pl.when(i == 0)
        def _():
            acc_ref[...] = jnp.zeros_like(acc_ref)

        def head(xv, dy, w, row):
            if latent:
                dy = dy * c_ref[...] + _swap_pairs(dy * s_ref[...])
            rstd = lax.rsqrt(jnp.mean(xv * xv, axis=-1, keepdims=True) + EPS)
            xhat = xv * rstd
            acc_ref[row:row + 1, :] += jnp.sum(dy * xhat, axis=0, keepdims=True)
            dxhat = dy * w
            return rstd * (dxhat - xhat * jnp.mean(dxhat * xhat, axis=-1, keepdims=True))

        for h in range(N_KV_HEADS):
            sl = slice(h * HEAD_DIM, (h + 1) * HEAD_DIM)
            dp_ref[:, sl] = head(p_ref[:, sl], dk_ref[:, sl], wk_ref[...], 1).astype(BF16)
        dp_ref[:, KV_W:2 * KV_W] = dv_ref[...].astype(BF16)
        if latent:
            for h in range(N_Q_HEADS):
                sl = slice(2 * KV_W + h * HEAD_DIM, 2 * KV_W + (h + 1) * HEAD_DIM)
                dyq = dq_ref[:, h * HEAD_DIM:(h + 1) * HEAD_DIM] * Q_SCALE
                dp_ref[:, sl] = head(p_ref[:, sl], dyq, wq_ref[...], 0).astype(BF16)

    vec = pl.BlockSpec((1, HEAD_DIM), lambda i: (0, 0))
    tab = pl.BlockSpec((tt, HEAD_DIM), lambda i: (i, 0))
    kv_spec = pl.BlockSpec((tt, KV_W), lambda i: (i, 0))
    p_spec = pl.BlockSpec((tt, width), lambda i: (i, 0))
    if latent:
        in_specs = [p_spec, pl.BlockSpec((tt, ATTN_W), lambda i: (i, 0)), kv_spec, kv_spec, vec, vec, tab, tab]
        ops = [p, dq, dk, dv, _row(wq), _row(wk), cexp, sexp]
    else:
        in_specs = [p_spec, kv_spec, kv_spec, vec]
        ops = [p, dk, dv, _row(wk)]
    return pl.pallas_call(
        body, name=name, grid=(n // tt,), in_specs=in_specs,
        out_specs=[p_spec, pl.BlockSpec((8, HEAD_DIM), lambda i: (0, 0))],
        out_shape=[jax.ShapeDtypeStruct((n, width), BF16), jax.ShapeDtypeStruct((8, HEAD_DIM), F32)],
        compiler_params=_params("arbitrary"),
    )(*ops)


def _kv_chunks(n, s_all, want=1024):
    step = want if n % want == 0 else 256
    chunks = [(s, step) for s in range(0, n, step)]
    if s_all > n:
        chunks.append((n, s_all - n))
    return chunks


def flash_fwd(q, k_all, v_ext, p, *, za_block, mix_width, name):
    n = q.shape[0]
    s_all = k_all.shape[0]
    tq = _rows_tile(n, 512)
    chunks = _kv_chunks(n, s_all)
    wide = 2 * HEAD_DIM

    def body(q_ref, k_ref, v_ref, za_ref, o_ref, mix_ref, lse_ref, m_ref, acc_ref):
        qv = q_ref[...]
        m_ref[...] = jnp.full_like(m_ref, -jnp.inf)
        acc_ref[...] = jnp.zeros_like(acc_ref)
        for start, size in chunks:
            kc = k_ref[pl.ds(start, size), :]
            vc = v_ref[pl.ds(start, size), :]
            s = lax.dot_general(qv, kc, _DIMS["nt"], preferred_element_type=F32)
            m_old = m_ref[...]
            m_new = jnp.maximum(m_old, jnp.max(s, axis=-1, keepdims=True))
            pr = jnp.exp2(s - m_new)
            alpha = jnp.exp2(m_old - m_new)
            acc_ref[...] = alpha * acc_ref[...] + jnp.dot(pr.astype(BF16), vc, preferred_element_type=F32)
            m_ref[...] = m_new
        acc = acc_ref[...]
        denom = acc[:, HEAD_DIM:HEAD_DIM + 1]
        o = acc[:, :HEAD_DIM] / denom
        o_ref[...] = o.astype(BF16)
        mix_ref[...] = (o * _silu(za_ref[...])).astype(BF16)
        lse_ref[0] = _as_row(m_ref[...] + jnp.log2(denom))

    qspec = pl.BlockSpec((tq, HEAD_DIM), lambda h, i: (i, h))
    return pl.pallas_call(
        body, name=name, grid=(N_Q_HEADS, n // tq),
        in_specs=[qspec, pl.BlockSpec((s_all, HEAD_DIM), lambda h, i: (0, h // Q_PER_KV)),
                  pl.BlockSpec((s_all, wide), lambda h, i: (0, h // Q_PER_KV)),
                  pl.BlockSpec((tq, HEAD_DIM), lambda h, i: (i, za_block + h))],
        out_specs=[qspec, qspec, pl.BlockSpec((1, 1, tq), lambda h, i: (h, 0, i))],
        out_shape=[jax.ShapeDtypeStruct((n, ATTN_W), BF16), jax.ShapeDtypeStruct((n, mix_width), BF16),
                   jax.ShapeDtypeStruct((N_Q_HEADS, 1, n), F32)],
        scratch_shapes=[pltpu.VMEM((tq, 1), F32), pltpu.VMEM((tq, wide), F32)],
        compiler_params=_params("parallel", "parallel"),
    )(q, k_all, v_ext, p)


def attn_gate_bwd(dmix, o, p, *, za_block, name):
    n = o.shape[0]
    tt = _rows_tile(n, 512)
    za_half = za_block * HEAD_DIM // HALF
    n_half = ATTN_W // HALF

    def body(*refs):
        dm_ref, o_ref = refs[:2]
        za_refs = refs[2:2 + n_half]
        do_ref, dos_ref, dza_ref, delta_ref = refs[2 + n_half:]
        dm = dm_ref[...]
        ov = o_ref[...].astype(F32)
        za = jnp.concatenate([r[...] for r in za_refs], axis=-1)
        do = dm * _silu(za)
        do_ref[...] = do.astype(BF16)
        dos_ref[...] = (do * LN2).astype(BF16)
        dza_ref[...] = (dm * ov * _dsilu(za)).astype(BF16)
        prod = do * ov
        for h in range(N_Q_HEADS):
            col = jnp.sum(prod[:, h * HEAD_DIM:(h + 1) * HEAD_DIM], axis=-1, keepdims=True) * LN2
            delta_ref[h] = _as_row(col)

    spec = pl.BlockSpec((tt, ATTN_W), lambda i: (i, 0))
    shape = jax.ShapeDtypeStruct((n, ATTN_W), BF16)
    za_specs = [pl.BlockSpec((tt, HALF), functools.partial(lambda i, cb: (i, cb), cb=za_half + h))
                for h in range(n_half)]
    return pl.pallas_call(
        body, name=name, grid=(n // tt,),
        in_specs=[spec, spec] + za_specs,
        out_specs=[spec, spec, spec, pl.BlockSpec((N_Q_HEADS, 1, tt), lambda i: (0, 0, i))],
        out_shape=[shape, shape, shape, jax.ShapeDtypeStruct((N_Q_HEADS, 1, n), F32)],
        compiler_params=_params("parallel"),
    )(dmix, o, *([p] * n_half))


def flash_bwd(q, do, do_s, lse_row, delta_row, k_all, v_ext, *, name):
    n = q.shape[0]
    s_all = k_all.shape[0]
    tq = _rows_tile(n, 1024)
    chunks = _kv_chunks(n, s_all)

    def body(q_ref, do_ref, dos_ref, lse_ref, dl_ref, k_ref, v_ref, dq_ref, dk_ref, dv_ref):
        g = pl.program_id(1)
        i = pl.program_id(2)

        @pl.when((g == 0) & (i == 0))
        def _():
            dk_ref[...] = jnp.zeros_like(dk_ref)
            dv_ref[...] = jnp.zeros_like(dv_ref)

        qv = q_ref[...]
        dov = do_ref[...]
        dosv = dos_ref[...]
        lse = lse_ref[0]
        dl = dl_ref[0]
        dq = jnp.zeros((tq, HEAD_DIM), F32)
        for start, size in chunks:
            kc = k_ref[pl.ds(start, size), :]
            vc = v_ref[pl.ds(start, size), :]
            st = lax.dot_general(kc, qv, _DIMS["nt"], preferred_element_type=F32)
            pt = jnp.exp2(st - lse)
            dpt = lax.dot_general(vc, dosv, _DIMS["nt"], preferred_element_type=F32)
            dst = (pt * (dpt - dl)).astype(BF16)
            dv_ref[pl.ds(start, size), :] += jnp.dot(pt.astype(BF16), dov, preferred_element_type=F32)
            dk_ref[pl.ds(start, size), :] += jnp.dot(dst, qv, preferred_element_type=F32)
            dq = dq + lax.dot_general(dst, kc, _DIMS["tn"], preferred_element_type=F32)
        dq_ref[...] = dq

    qspec = pl.BlockSpec((tq, HEAD_DIM), lambda kh, g, i: (i, kh * Q_PER_KV + g))
    rowspec = pl.BlockSpec((1, 1, tq), lambda kh, g, i: (kh * Q_PER_KV + g, 0, i))
    kvspec = pl.BlockSpec((s_all, HEAD_DIM), lambda kh, g, i: (0, kh))
    return pl.pallas_call(
        body, name=name, grid=(N_KV_HEADS, Q_PER_KV, n // tq),
        in_specs=[qspec, qspec, qspec, rowspec, rowspec, kvspec,
                  pl.BlockSpec((s_all, HEAD_DIM), lambda kh, g, i: (0, kh * (V_EXT_W // HEAD_DIM)))],
        out_specs=[qspec, kvspec, kvspec],
        out_shape=[jax.ShapeDtypeStruct((n, ATTN_W), F32),
                   jax.ShapeDtypeStruct((s_all, KV_W), F32), jax.ShapeDtypeStruct((s_all, KV_W), F32)],
        compiler_params=_params("arbitrary", "arbitrary", "arbitrary"),
    )(q, do, do_s, lse_row, delta_row, k_all, v_ext)


HALF = 512


SUB = 128


def _tap_group(w_ref, w_lanes, ext_ref, r0, lanes, r, tap_of):
    z = None
    for a in range(4):
        k = tap_of(8 * a + r)
        if 0 <= k < CONV_WIDTH:
            term = w_ref[k:k + 1, w_lanes] * ext_ref[pl.ds(r0 + 8 * a, SUB + 8), lanes]
            z = term if z is None else z + term
    return z


def _shifted(z, z_ref, r):
    if r == 0:
        return z[0:SUB]
    z_ref[...] = z
    return z_ref[pl.ds(r, SUB), :]


def _fold8(x):
    return jnp.sum(x.reshape(x.shape[0] // 8, 8, x.shape[1]), axis=0)


def _pieces(tt, h):
    return [(rh * SUB, slice(c * SUB, (c + 1) * SUB), slice(h * HALF + c * SUB, h * HALF + (c + 1) * SUB))
            for c in range(HALF // SUB) for rh in range(tt // SUB)]


def _halo_specs(tt, n, cb):
    per = tt // HALO
    last = n // HALO - 1
    return [pl.BlockSpec((HALO, HALF), lambda i: (jnp.maximum(i * per - 1, 0), cb)),
            pl.BlockSpec((tt, HALF), lambda i: (i, cb)),
            pl.BlockSpec((HALO, HALF), lambda i: (jnp.minimum((i + 1) * per, last), cb))]


def conv_fwd(p, dw_w, dw_b, ln_g, ln_b, mix, *, glu_block, name):
    n = p.shape[0]
    ch = dw_w.shape[1]
    nh = ch // HALF
    tt = _rows_tile(n, 256)
    last = n // tt - 1

    def body(*refs):
        a_refs = [refs[3 * h:3 * h + 3] for h in range(nh)]
        b_refs = [refs[3 * (nh + h):3 * (nh + h) + 3] for h in range(nh)]
        pos = 6 * nh
        zb_refs = refs[pos:pos + nh]
        pos += nh
        w_ref, bias_ref, g_ref, be_ref, _, ycv_ref, mix_ref, ext_ref, z_ref = refs[pos:pos + 9]
        i = pl.program_id(0)
        for h in range(nh):
            cs = slice(h * HALF, (h + 1) * HALF)
            ap, am, an = a_refs[h]
            bp, bm, bn = b_refs[h]
            ext_ref[0:HALO, :] = jnp.where(i > 0, ap[...] * _sigmoid(bp[...]), 0.0)
            ext_ref[HALO:HALO + tt, :] = am[...] * _sigmoid(bm[...])
            ext_ref[HALO + tt:2 * HALO + tt, :] = jnp.where(i < last, an[...] * _sigmoid(bn[...]), 0.0)
            for r0, lanes, w_lanes in _pieces(tt, h):
                acc = jnp.broadcast_to(bias_ref[:, w_lanes], (SUB, SUB))
                for r in range(8):
                    z = _tap_group(w_ref, w_lanes, ext_ref, r0, lanes, r, lambda j: j - 1)
                    acc = acc + _shifted(z, z_ref, r)
                ycv_ref[pl.ds(r0, SUB), w_lanes] = acc
        yc = ycv_ref[...]
        mu = jnp.mean(yc, axis=-1, keepdims=True)
        var = jnp.mean(jnp.square(yc - mu), axis=-1, keepdims=True)
        ln = (yc - mu) * lax.rsqrt(var + EPS) * g_ref[...] + be_ref[...]
        out = _silu(ln)
        for h in range(nh):
            cs = slice(h * HALF, (h + 1) * HALF)
            mix_ref[:, cs] = (out[:, cs] * _silu(zb_refs[h][...])).astype(BF16)

    in_specs = []
    for h in range(2 * nh):
        in_specs += _halo_specs(tt, n, glu_block + h)
    in_specs += [pl.BlockSpec((tt, HALF), functools.partial(lambda i, cb: (i, cb), cb=glu_block + 2 * nh + h))
                 for h in range(nh)]
    vec = pl.BlockSpec((1, ch), lambda i: (0, 0))
    in_specs += [pl.BlockSpec((CONV_WIDTH, ch), lambda i: (0, 0)), vec, vec, vec, ANY]
    ops = [p] * (6 * nh + nh) + [dw_w, _row(dw_b), _row(ln_g), _row(ln_b), mix]
    big = pl.BlockSpec((tt, ch), lambda i: (i, 0))
    mix_block = (mix.shape[1] - ch) // ch
    return pl.pallas_call(
        body, name=name, grid=(n // tt,), in_specs=in_specs,
        out_specs=[big, pl.BlockSpec((tt, ch), lambda i: (i, mix_block))],
        out_shape=[jax.ShapeDtypeStruct((n, ch), F32), jax.ShapeDtypeStruct(mix.shape, BF16)],
        input_output_aliases={len(ops) - 1: 1},
        scratch_shapes=[pltpu.VMEM((tt + 2 * HALO, HALF), F32), pltpu.VMEM((SUB + 8, SUB), F32)],
        compiler_params=_params("parallel"),
    )(*ops)


def conv_bwd_rows(dmix, ycv, p, ln_g, ln_b, *, mix_block, zb_block, name):
    n, ch = ycv.shape
    nh = ch // HALF
    tt = _rows_tile(n, 256)

    def body(*refs):
        dm_ref, ycv_ref = refs[:2]
        zb_refs = refs[2:2 + nh]
        g_ref, be_ref, dy_ref, dzb_ref, acc_ref = refs[2 + nh:]
        i = pl.program_id(0)

        @pl.when(i == 0)
        def _():
            acc_ref[...] = jnp.zeros_like(acc_ref)

        yc = ycv_ref[...]
        mu = jnp.mean(yc, axis=-1, keepdims=True)
        var = jnp.mean(jnp.square(yc - mu), axis=-1, keepdims=True)
        rstd = lax.rsqrt(var + EPS)
        xhat = (yc - mu) * rstd
        ln = xhat * g_ref[...] + be_ref[...]
        out = _silu(ln)
        dm = dm_ref[...]
        zb = jnp.concatenate([r[...] for r in zb_refs], axis=-1)
        dzb_ref[...] = (dm * out * _dsilu(zb)).astype(BF16)
        dln = dm * _silu(zb) * _dsilu(ln)
        acc_ref[0:1, :] += jnp.sum(dln * xhat, axis=0, keepdims=True)
        acc_ref[1:2, :] += jnp.sum(dln, axis=0, keepdims=True)
        dxhat = dln * g_ref[...]
        dy_ref[...] = rstd * (dxhat - jnp.mean(dxhat, axis=-1, keepdims=True)
                              - xhat * jnp.mean(dxhat * xhat, axis=-1, keepdims=True))

    big = pl.BlockSpec((tt, ch), lambda i: (i, 0))
    vec = pl.BlockSpec((1, ch), lambda i: (0, 0))
    in_specs = [pl.BlockSpec((tt, ch), lambda i: (i, mix_block)), big]
    in_specs += [pl.BlockSpec((tt, HALF), functools.partial(lambda i, cb: (i, cb), cb=zb_block + h)) for h in range(nh)]
    in_specs += [vec, vec]
    return pl.pallas_call(
        body, name=name, grid=(n // tt,), in_specs=in_specs,
        out_specs=[big, big, pl.BlockSpec((8, ch), lambda i: (0, 0))],
        out_shape=[jax.ShapeDtypeStruct((n, ch), F32), jax.ShapeDtypeStruct((n, ch), BF16),
                   jax.ShapeDtypeStruct((8, ch), F32)],
        compiler_params=_params("arbitrary"),
    )(dmix, ycv, *([p] * nh), _row(ln_g), _row(ln_b))


def conv_bwd_taps(dycv, p, dw_w, *, glu_block, name):
    n, ch = dycv.shape
    nh = ch // HALF
    tt = _rows_tile(n, 256)
    last = n // tt - 1

    def body(*refs):
        d_refs = [refs[3 * h:3 * h + 3] for h in range(nh)]
        a_refs = [refs[3 * (nh + h):3 * (nh + h) + 3] for h in range(nh)]
        b_refs = [refs[3 * (2 * nh + h):3 * (2 * nh + h) + 3] for h in range(nh)]
        w_ref, dglu_ref, dw_ref, db_ref, yext_ref, dext_ref, z_ref, dy_ref, dwp_ref, dbp_ref = refs[9 * nh:]
        i = pl.program_id(0)

        @pl.when(i == 0)
        def _():
            dwp_ref[...] = jnp.zeros_like(dwp_ref)
            dbp_ref[...] = jnp.zeros_like(dbp_ref)

        for h in range(nh):
            cs = slice(h * HALF, (h + 1) * HALF)
            ap, am, an = a_refs[h]
            bp, bm, bn = b_refs[h]
            dp, dm, dn = d_refs[h]
            av = am[...]
            sb = _sigmoid(bm[...])
            yext_ref[0:HALO, :] = jnp.where(i > 0, ap[...] * _sigmoid(bp[...]), 0.0)
            yext_ref[HALO:HALO + tt, :] = av * sb
            yext_ref[HALO + tt:2 * HALO + tt, :] = jnp.where(i < last, an[...] * _sigmoid(bn[...]), 0.0)
            dmain = dm[...]
            dext_ref[0:HALO, :] = jnp.where(i > 0, dp[...], 0.0)
            dext_ref[HALO:HALO + tt, :] = dmain
            dext_ref[HALO + tt:2 * HALO + tt, :] = jnp.where(i < last, dn[...], 0.0)
            for r0, lanes, w_lanes in _pieces(tt, h):
                dsub = dext_ref[pl.ds(HALO + r0, SUB), lanes]
                dy = jnp.zeros((SUB, SUB), F32)
                for r in range(8):
                    z = _tap_group(w_ref, w_lanes, dext_ref, r0, lanes, r, lambda j: CONV_WIDTH - j)
                    dy = dy + _shifted(z, z_ref, r)
                    y_r = yext_ref[pl.ds(r0 + r, SUB + 24), lanes]
                    for a in range(4):
                        k = 8 * a + r - 1
                        if k >= 0:
                            dwp_ref[8 * k:8 * k + 8, w_lanes] += _fold8(dsub * y_r[8 * a:8 * a + SUB])
                dbp_ref[:, w_lanes] += _fold8(dsub)
                dy_ref[pl.ds(r0, SUB), lanes] = dy
            dy = dy_ref[...]
            dglu_ref[:, cs] = (dy * sb).astype(BF16)
            dglu_ref[:, ch + h * HALF:ch + (h + 1) * HALF] = (dy * av * sb * (1.0 - sb)).astype(BF16)

        @pl.when(i == last)
        def _():
            dw_ref[...] = jnp.sum(dwp_ref[...].reshape(32, 8, ch), axis=1)
            db_ref[...] = jnp.broadcast_to(jnp.sum(dbp_ref[...], axis=0, keepdims=True), db_ref.shape)

    in_specs = []
    for h in range(nh):
        in_specs += _halo_specs(tt, n, h)
    for h in range(2 * nh):
        in_specs += _halo_specs(tt, n, glu_block + h)
    in_specs += [pl.BlockSpec((CONV_WIDTH, ch), lambda i: (0, 0))]
    ops = [dycv] * (3 * nh) + [p] * (6 * nh) + [dw_w]
    return pl.pallas_call(
        body, name=name, grid=(n // tt,), in_specs=in_specs,
        out_specs=[pl.BlockSpec((tt, 2 * ch), lambda i: (i, 0)), pl.BlockSpec((32, ch), lambda i: (0, 0)),
                   pl.BlockSpec((8, ch), lambda i: (0, 0))],
        out_shape=[jax.ShapeDtypeStruct((n, 2 * ch), BF16), jax.ShapeDtypeStruct((32, ch), F32),
                   jax.ShapeDtypeStruct((8, ch), F32)],
        scratch_shapes=[pltpu.VMEM((tt + 2 * HALO, HALF), F32), pltpu.VMEM((tt + 2 * HALO, HALF), F32),
                        pltpu.VMEM((SUB + 8, SUB), F32), pltpu.VMEM((tt, HALF), F32),
                        pltpu.VMEM((8 * 32, ch), F32), pltpu.VMEM((8, ch), F32)],
        compiler_params=_params("arbitrary"),
    )(*ops)


def _sgu_common(p_ref, g_ref, be_ref, ws_ref, bs_ref, w):
    gw = w // SGU_GROUPS
    u_pre = p_ref[:, 0:w]
    v_pre = p_ref[:, w:2 * w]
    zc = p_ref[:, 2 * w:3 * w]
    u = _gelu(u_pre)
    v = _gelu(v_pre)
    mu = jnp.mean(v, axis=-1, keepdims=True)
    var = jnp.mean(jnp.square(v - mu), axis=-1, keepdims=True)
    rstd = lax.rsqrt(var + EPS)
    vhat = (v - mu) * rstd
    vn = (vhat * g_ref[...] + be_ref[...]).astype(BF16)
    mixed = jnp.concatenate(
        [jnp.dot(ws_ref[gi].astype(BF16), vn[:, gi * gw:(gi + 1) * gw], preferred_element_type=F32)
         + bs_ref[:, gi:gi + 1] for gi in range(SGU_GROUPS)], axis=-1)
    return u_pre, v_pre, zc, u, rstd, vhat, vn, mixed


def sgu_fwd(p, ln_g, ln_b, ws, bs_t, *, name):
    n, w3 = p.shape
    w = w3 // 3

    def body(p_ref, g_ref, be_ref, ws_ref, bs_ref, m_ref):
        _, _, zc, u, _, _, _, mixed = _sgu_common(p_ref, g_ref, be_ref, ws_ref, bs_ref, w)
        m_ref[...] = (u * mixed * _silu(zc)).astype(BF16)

    vec = pl.BlockSpec((1, w), lambda i: (0, 0))
    return pl.pallas_call(
        body, name=name, grid=(n // CHUNK,),
        in_specs=[pl.BlockSpec((CHUNK, w3), lambda i: (i, 0)), vec, vec,
                  pl.BlockSpec((SGU_GROUPS, CHUNK, CHUNK), lambda i: (0, 0, 0)),
                  pl.BlockSpec((CHUNK, SGU_GROUPS), lambda i: (0, 0))],
        out_specs=pl.BlockSpec((CHUNK, w), lambda i: (i, 0)),
        out_shape=jax.ShapeDtypeStruct((n, w), BF16),
        compiler_params=_params("parallel"),
    )(p, _row(ln_g), _row(ln_b), ws, bs_t)


def sgu_bwd(p, dm, ln_g, ln_b, ws, ws_t, bs_t, *, name):
    n, w3 = p.shape
    w = w3 // 3
    gw = w // SGU_GROUPS

    def body(p_ref, dm_ref, g_ref, be_ref, ws_ref, wst_ref, bs_ref, dp_ref, dws_ref, dbs_ref, acc_ref):
        i = pl.program_id(0)

        @pl.when(i == 0)
        def _():
            dws_ref[...] = jnp.zeros_like(dws_ref)
            dbs_ref[...] = jnp.zeros_like(dbs_ref)
            acc_ref[...] = jnp.zeros_like(acc_ref)

        u_pre, v_pre, zc, u, rstd, vhat, vn, mixed = _sgu_common(p_ref, g_ref, be_ref, ws_ref, bs_ref, w)
        dmv = dm_ref[...]
        um = u * mixed
        dp_ref[:, 2 * w:3 * w] = (dmv * um * _dsilu(zc)).astype(BF16)
        dum = dmv * _silu(zc)
        dp_ref[:, 0:w] = (dum * mixed * _dgelu(u_pre)).astype(BF16)
        dmixed = dum * u
        dmixed_b = dmixed.astype(BF16)
        dvn_parts = []
        for gi in range(SGU_GROUPS):
            cs = slice(gi * gw, (gi + 1) * gw)
            dws_ref[gi] += lax.dot_general(dmixed_b[:, cs], vn[:, cs], _DIMS["nt"], preferred_element_type=F32)
            dbs_ref[:, gi:gi + 1] += jnp.sum(dmixed[:, cs], axis=-1, keepdims=True)
            dvn_parts.append(jnp.dot(wst_ref[gi].astype(BF16), dmixed_b[:, cs], preferred_element_type=F32))
        dvn = jnp.concatenate(dvn_parts, axis=-1)
        acc_ref[0:1, :] += jnp.sum(dvn * vhat, axis=0, keepdims=True)
        acc_ref[1:2, :] += jnp.sum(dvn, axis=0, keepdims=True)
        dvhat = dvn * g_ref[...]
        dv = rstd * (dvhat - jnp.mean(dvhat, axis=-1, keepdims=True)
                     - vhat * jnp.mean(dvhat * vhat, axis=-1, keepdims=True))
        dp_ref[:, w:2 * w] = (dv * _dgelu(v_pre)).astype(BF16)

    vec = pl.BlockSpec((1, w), lambda i: (0, 0))
    wspec = pl.BlockSpec((SGU_GROUPS, CHUNK, CHUNK), lambda i: (0, 0, 0))
    bspec = pl.BlockSpec((CHUNK, SGU_GROUPS), lambda i: (0, 0))
    return pl.pallas_call(
        body, name=name, grid=(n // CHUNK,),
        in_specs=[pl.BlockSpec((CHUNK, w3), lambda i: (i, 0)), pl.BlockSpec((CHUNK, w), lambda i: (i, 0)),
                  vec, vec, wspec, wspec, bspec],
        out_specs=[pl.BlockSpec((CHUNK, w3), lambda i: (i, 0)), wspec, bspec,
                   pl.BlockSpec((8, w), lambda i: (0, 0))],
        out_shape=[jax.ShapeDtypeStruct((n, w3), BF16), jax.ShapeDtypeStruct((SGU_GROUPS, CHUNK, CHUNK), F32),
                   jax.ShapeDtypeStruct((CHUNK, SGU_GROUPS), F32), jax.ShapeDtypeStruct((8, w), F32)],
        compiler_params=_params("arbitrary"),
    )(p, dm, _row(ln_g), _row(ln_b), ws, ws_t, bs_t)


def _adam_math(w, g, m, v):
    m_new = ADAM_B1 * m + (1.0 - ADAM_B1) * g
    v_new = ADAM_B2 * v + (1.0 - ADAM_B2) * (g * g)
    m_hat = m_new / (1.0 - ADAM_B1 ** ADAM_STEP)
    v_hat = v_new / (1.0 - ADAM_B2 ** ADAM_STEP)
    delta = -ADAM_LR * (m_hat / (jnp.sqrt(v_hat) + ADAM_EPS) + ADAM_WD * w)
    return delta, m_new, v_new


def adamw(w, g, m, v, *, name, slots=False, rows=512):
    r, c = w.shape
    tr = min(r, rows)
    assert r % tr == 0, (name, r, tr)

    def body(w_ref, g_ref, m_ref, v_ref, go_ref, d_ref, mo_ref, vo_ref):
        if slots:
            g = g_ref[0].astype(F32)
            for k in range(1, N_DEV):
                g = g + g_ref[k].astype(F32)
        else:
            g = g_ref[...].astype(F32)
        delta, m_new, v_new = _adam_math(w_ref[...], g, m_ref[...], v_ref[...])
        go_ref[...] = g
        d_ref[...] = delta
        mo_ref[...] = m_new
        vo_ref[...] = v_new

    spec = pl.BlockSpec((tr, c), lambda i: (i, 0))
    gspec = pl.BlockSpec((N_DEV, tr, c), lambda i: (0, i, 0)) if slots else spec
    shape = jax.ShapeDtypeStruct((r, c), F32)
    return pl.pallas_call(
        body, name=name, grid=(r // tr,),
        in_specs=[spec, gspec, spec, spec], out_specs=[spec] * 4, out_shape=[shape] * 4,
        compiler_params=_params("parallel"),
    )(w, g, m, v)


def slot_sum(slots, *, name):
    _, r, c = slots.shape

    def body(s_ref, o_ref):
        acc = s_ref[0]
        for k in range(1, N_DEV):
            acc = acc + s_ref[k]
        o_ref[...] = acc

    return pl.pallas_call(
        body, name=name, out_shape=jax.ShapeDtypeStruct((r, c), F32),
        in_specs=[VMEM_SPEC], out_specs=VMEM_SPEC,
        compiler_params=pltpu.CompilerParams(vmem_limit_bytes=VMEM_LIMIT),
    )(slots)


def _after(token, val):
    return val + token[0, 0]


def _zero_of(v):
    bits = lax.bitcast_convert_type(v, jnp.uint16 if v.dtype == BF16 else jnp.uint32)
    return jnp.where((bits | 1) == 0, 1.0, 0.0).astype(F32)


def _pad_rows(a, rows):
    return jnp.pad(a, ((0, rows - a.shape[0]), (0, 0)))


def kernel(x, c, ctx, c_ctx, ada_w, ada_b, norm_g, ev_w_in, ev_q_norm, ev_k_norm, ev_dw_w, ev_dw_b, ev_ln_g, ev_ln_b, ev_w_out, od_w_in, od_ln_g, od_ln_b, od_ws, od_bs, od_w_out, final_g, loss_target, m_c_ctx, m_ada_w, m_ada_b, m_norm_g, m_ev_w_in, m_ev_q_norm, m_ev_k_norm, m_ev_dw_w, m_ev_dw_b, m_ev_ln_g, m_ev_ln_b, m_ev_w_out, m_od_w_in, m_od_ln_g, m_od_ln_b, m_od_ws, m_od_bs, m_od_w_out, m_final_g, v_c_ctx, v_ada_w, v_ada_b, v_norm_g, v_ev_w_in, v_ev_q_norm, v_ev_k_norm, v_ev_dw_w, v_ev_dw_b, v_ev_ln_g, v_ev_ln_b, v_ev_w_out, v_od_w_in, v_od_ln_g, v_od_ln_b, v_od_ws, v_od_bs, v_od_w_out, v_final_g):
    n, d = x.shape[1], x.shape[2]
    lc = ctx.shape[1]
    ev_in = ev_w_in.shape[2] * N_DEV
    od_in = od_w_in.shape[2] * N_DEV
    conv_ch = ev_dw_w.shape[2] * N_DEV
    ada_cols = ada_w.shape[2]
    me = 4 * lax.axis_index("x") + 2 * lax.axis_index("y") + lax.axis_index("c")
    xs, tgt, ctxs = x[0], loss_target[0], ctx[0]
    za_block = (2 * KV_W + ATTN_W) // HEAD_DIM
    glu_block = (2 * KV_W + 2 * ATTN_W) // HALF
    zb_block = glu_block + 2 * conv_ch // HALF

    small = jnp.concatenate([
        jax.nn.silu(c).reshape(1, d),
        od_ln_g.reshape(1, -1), od_ln_b.reshape(1, -1)], axis=1)
    small = _pad_rows(small, 8)
    dw_rows = _pad_rows(ev_dw_w[0], 32)
    small_g, dw_g = all_gather([small, dw_rows], name="gather_small", hbm=False)
    sc_all = small_g[:, 0, :d]
    shard = d // N_DEV
    od_ln_g_full = small_g[:, 0, d:d + shard].reshape(d)
    od_ln_b_full = small_g[:, 0, d + shard:d + 2 * shard].reshape(d)
    dw_w_full = jnp.moveaxis(dw_g, 0, 1).reshape(32, conv_ch)[:CONV_WIDTH]
    scc = jax.nn.silu(c_ctx)
    sc16 = _pad_rows(jnp.concatenate([sc_all, scc.reshape(1, d)], axis=0), 16)

    ada_bf = ada_w.astype(BF16)
    mod_loc = [matmul(sc16, ada_bf[l], mode="nn", tm=16, tn=ada_cols, tk=d, out_dtype=F32, name=f"ada_mod{l}")
               for l in range(2)]
    (mod_g,) = all_gather([jnp.stack(mod_loc)], name="gather_mod", hbm=False)
    mod_all = jnp.moveaxis(mod_g, 0, 2).reshape(2, 16, N_DEV * ada_cols) + ada_b[:, None, :]
    mod_me = lax.dynamic_index_in_dim(mod_all, me, axis=1, keepdims=False)
    shift = [mod_me[l, :d] for l in range(2)]
    scale = [mod_me[l, d:2 * d] for l in range(2)]
    gate = [mod_me[l, 2 * d:] for l in range(2)]
    shift_c, scale_c = mod_all[0, 8, :d], mod_all[0, 8, d:2 * d]

    h_w_in0, near_token = staged_gather_start(1, (ev_w_in[0] + _zero_of(mod_g[0, 0, 0, 0])).astype(BF16), None,
                                              name="gather_w_in0_chips_start")

    cexp, sexp = rope_tables(n)
    h0 = norm_mod_fwd(xs, norm_g[0], _after(near_token, shift[0]), scale[0], name="norm_mod_fwd0")
    hc = norm_mod_fwd(ctxs, norm_g[0], shift_c, scale_c, name="norm_mod_fwd_ctx")

    wi0_shard, wi0_land = staged_gather_wait(1, h_w_in0, h0, name="gather_w_in0_chips_wait")
    h_w_in0, far_token = staged_gather_start(2, wi0_shard, wi0_land, name="gather_w_in0_cores_start")
    wi0_shard, wi0_land = staged_gather_wait(2, h_w_in0, far_token, name="gather_w_in0_cores_wait")
    wi0_g = lax.dynamic_update_slice_in_dim(wi0_land, wi0_shard[None], me, axis=0)
    wi0 = jnp.moveaxis(wi0_g, 0, 1).reshape(d, ev_in)
    landed_zero = _zero_of(wi0_g[0, 0, 0])
    later = [(w[0] + landed_zero).astype(BF16) for w in (ev_w_out, od_w_in, od_w_out)]
    (h_wo0, h_wi1, h_wo1), w_token = exchange_start(later, scatter=False, name="gather_rest_start")

    def landed(handle, after, name):
        own, land = exchange_wait(handle, after, scatter=False, name=name)
        return lax.dynamic_update_slice_in_dim(land, own[None], me, axis=0)

    p0 = matmul(h0, wi0, mode="nn", tm=1024, tn=ev_in // 4, tk=d, out_dtype=F32, name="proj_in0", after=w_token)
    pc = matmul(hc, wi0, mode="nn", tm=lc, tn=2 * KV_W, tk=d, out_dtype=F32, n_out=2 * KV_W, name="proj_in_ctx")
    q_r, k_all, v_ext = qkv_prep_fwd(p0, ev_q_norm[0], ev_k_norm[0], cexp, sexp, latent=True, name="qkv_prep",
                                     rows_all=n + lc)
    k_all, v_ext = qkv_prep_fwd(pc, None, ev_k_norm[0], None, None, latent=False, name="kv_prep_ctx",
                                rows_all=n + lc, k_all=k_all, v_ext=v_ext)
    o_attn, mix_a, lse = flash_fwd(q_r, k_all, v_ext, p0, za_block=za_block, mix_width=ATTN_W + conv_ch,
                                   name="flash_fwd")
    ycv, mix0 = conv_fwd(p0, dw_w_full, ev_dw_b[0], ev_ln_g[0], ev_ln_b[0], mix_a, glu_block=glu_block,
                         name="conv_fwd")
    wo0 = landed(h_wo0, ycv, "gather_w_out0_wait").reshape(-1, d)
    x1, r0, h1 = matmul(mix0, wo0, mode="nn", tm=512, tn=d, tk=mix0.shape[1], out_dtype=F32, name="proj_out0",
                        res=xs, gate=_row(gate[0]), norm=(norm_g[1], shift[1], scale[1]))

    wi1 =jnp.moveaxis(landed(h_wi1, h1, "gather_w_in1_wait"), 0, 1).reshape(d, od_in)
    p1 = matmul(h1, wi1, mode="nn", tm=1024, tn=od_in // 4, tk=d, out_dtype=F32, name="proj_in1")
    ws_bf = od_ws[0]
    bs_t = od_bs[0].T
    m1 = sgu_fwd(p1, od_ln_g_full, od_ln_b_full, ws_bf, bs_t, name="sgu_fwd")
    wo1 = landed(h_wo1, m1, "gather_w_out1_wait").reshape(-1, d)
    dx2, dr1, acc_final, loss_tile = proj_out_loss(m1, wo1, x1, gate[1], tgt, final_g, name="proj_out1_loss")

    dm1 = matmul(dr1, wo1, mode="nt", tm=1024, tn=1024, tk=d, out_dtype=F32, name="d_mix1")
    dwo1 = matmul(m1, dr1, mode="tn", tm=1024, tn=1024, tk=2048, out_dtype=BF16, name="d_wout1")
    dp1, dws, dbs_t, acc_sgu = sgu_bwd(p1, dm1, od_ln_g_full, od_ln_b_full, ws_bf, jnp.swapaxes(ws_bf, 1, 2), bs_t,
                                       name="sgu_bwd")
    dwi1_s = matmul(h1, dp1, mode="tn", tm=2048, tn=od_in // N_DEV, tk=2048, out_dtype=BF16, name="d_win1",
                    split_out=True)
    dwo1_s = dwo1.reshape(N_DEV, -1, d)
    (h_gi1, h_go1), g1_token = exchange_start([dwi1_s, dwo1_s], scatter=True, name="grads1_start")
    dh1 = matmul(dp1, wi1, mode="nt", tm=1024, tn=512, tk=od_in, out_dtype=F32, name="d_h1")
    dx1, dr0, acc_norm1 = norm_mod_bwd(x1, dh1, dx2, norm_g[1], _after(g1_token, scale[1]), name="norm_mod_bwd1",
                                       branch=(r0, gate[0]))

    dmix0 = matmul(dr0, wo0, mode="nt", tm=1024, tn=1024, tk=d, out_dtype=F32, name="d_mix0")
    dwo0 = matmul(mix0, dr0, mode="tn", tm=1024, tn=1024, tk=2048, out_dtype=BF16, name="d_wout0")
    do_attn, do_s, dza, delta = attn_gate_bwd(dmix0, o_attn, p0, za_block=za_block, name="attn_gate_bwd")
    dycv, dzb, acc_ln = conv_bwd_rows(dmix0, ycv, p0, ev_ln_g[0], ev_ln_b[0], mix_block=ATTN_W // conv_ch,
                                      zb_block=zb_block, name="conv_bwd_rows")
    dglu, ddw_w, acc_dwb = conv_bwd_taps(dycv, p0, dw_w_full, glu_block=glu_block, name="conv_bwd_taps")
    dq_r, dk_all, dv_all = flash_bwd(q_r, do_attn, do_s, lse, delta, k_all, v_ext, name="flash_bwd")
    dkvq, acc_qk = qkv_prep_bwd(p0, dq_r, dk_all[:n], dv_all[:n], ev_q_norm[0], ev_k_norm[0], cexp, sexp,
                                latent=True, name="qkv_prep_bwd")
    dpc, acc_kc = qkv_prep_bwd(pc, None, dk_all[n:], dv_all[n:], None, ev_k_norm[0], None, None,
                               latent=False, name="kv_prep_ctx_bwd")
    dp0 = jnp.concatenate([dkvq, dza, dglu, dzb], axis=1)
    dwi0 = matmul(h0, dp0, mode="tn", tm=1024, tn=ev_in // 4, tk=2048, out_dtype=F32, name="d_win0")
    dwi0 = matmul(hc, dpc, mode="tn", tm=512, tn=2 * KV_W, tk=lc, out_dtype=F32, name="d_win0_ctx", add=dwi0)
    dwi0_s = jnp.moveaxis(dwi0.astype(BF16).reshape(d, N_DEV, ev_in // N_DEV), 1, 0)
    dwo0_s = dwo0.reshape(N_DEV, -1, d)
    (h_gi0, h_go0), g0_token = exchange_start([dwi0_s, dwo0_s], scatter=True, name="grads0_start")
    dh0 = matmul(dp0, wi0, mode="nt", tm=1024, tn=512, tk=ev_in, out_dtype=F32, name="d_h0", after=g0_token)
    dhc = matmul(dpc, wi0, mode="nt", tm=lc, tn=512, tk=2 * KV_W, out_dtype=F32, name="d_h_ctx")
    grad_x, acc_norm0 = norm_mod_bwd(xs, dh0, dx1, norm_g[0], _after(g0_token, scale[0]), name="norm_mod_bwd0")
    _, acc_normc = norm_mod_bwd(ctxs, dhc, None, norm_g[0], scale_c, name="norm_mod_bwd_ctx")

    zeros_d = jnp.zeros((d,), F32)
    dmod0 = jnp.stack([acc_norm0[0], acc_norm0[1], acc_norm1[GATE_ROW]])
    dmod1 = jnp.stack([acc_norm1[0], acc_norm1[1], acc_final[GATE_ROW]])
    dmodc = jnp.stack([acc_normc[0], acc_normc[1]])
    half_pad = jnp.zeros((d - 2 * conv_ch,), F32) if d > 2 * conv_ch else jnp.zeros((0,), F32)
    row_a = jnp.concatenate([acc_dwb[0], acc_ln[0], half_pad])
    row_b = jnp.concatenate([acc_ln[1], acc_qk[0], acc_qk[1] + acc_kc[1],
                             jnp.zeros((d - conv_ch - 2 * HEAD_DIM,), F32)])
    row_c = jnp.concatenate([dbs_t.T.reshape(-1), jnp.zeros((d - SGU_GROUPS * CHUNK,), F32)])
    row_loss = jnp.concatenate([loss_tile[0, :1], jnp.zeros((d - 1,), F32)])
    pack = jnp.concatenate([
        dmod0, dmod1, dmodc,
        (acc_norm0[2] + acc_normc[2])[None], acc_norm1[2][None],
        acc_final[0][None],
        acc_sgu[0][None], acc_sgu[1][None],
        row_a[None], row_b[None], row_c[None], row_loss[None],
        ddw_w.reshape(-1, d),
        dws.reshape(-1, d),
    ], axis=0)
    n_rows = pack.shape[0]
    pack = _pad_rows(pack, -(-n_rows // 8) * 8)
    (h_pack,), pack_token = exchange_start([pack], scatter=False, name="small_grads_start")

    def summands(handle, after, name):
        mine, land = exchange_wait(handle, after, scatter=True, name=name)
        own = lax.dynamic_index_in_dim(mine, me, axis=0, keepdims=True)
        return lax.dynamic_update_slice_in_dim(land, own, me, axis=0)

    out = {}

    def upd(key, w, g, m, v, slots=False, rows=512):
        shp = w.shape
        w2 = w.reshape(-1, shp[-1])
        g2 = g.reshape((N_DEV, -1, shp[-1])) if slots else g.reshape(-1, shp[-1])
        res = adamw(w2, g2, m.reshape(w2.shape), v.reshape(w2.shape), name="adamw_" + key, slots=slots, rows=rows)
        out[key] = tuple(r.reshape(shp) for r in res)

    gi1 = summands(h_gi1, pack_token, "grads_w_in1_wait")
    upd("od_w_in", od_w_in, gi1, m_od_w_in, v_od_w_in, slots=True, rows=256)
    go1 = summands(h_go1, out["od_w_in"][1], "grads_w_out1_wait")
    upd("od_w_out", od_w_out, go1, m_od_w_out, v_od_w_out, slots=True, rows=256)
    go0 = summands(h_go0, out["od_w_out"][1], "grads_w_out0_wait")
    upd("ev_w_out", ev_w_out, go0, m_ev_w_out, v_ev_w_out, slots=True, rows=256)
    gi0 = summands(h_gi0, out["ev_w_out"][1], "grads_w_in0_wait")
    upd("ev_w_in", ev_w_in, gi0, m_ev_w_in, v_ev_w_in, slots=True, rows=256)

    pack_own, pack_land = exchange_wait(h_pack, out["ev_w_in"][1], scatter=False, name="small_grads_wait")
    pack_g = lax.dynamic_update_slice_in_dim(pack_land, pack_own[None], me, axis=0)
    gsum = slot_sum(pack_g, name="sum_small_grads")
    loss = gsum[16, 0]
    dw_rows_n = 32 * conv_ch // d
    g_dw_w = gsum[17:17 + dw_rows_n].reshape(32, conv_ch)[:CONV_WIDTH]
    g_od_ws = gsum[17 + dw_rows_n:17 + dw_rows_n + SGU_GROUPS * CHUNK * CHUNK // d].reshape(od_ws.shape)

    dmodc_sum = jnp.concatenate([gsum[6], gsum[7], zeros_d])
    col0 = me * ada_cols
    dm_cols = []
    for l in range(2):
        rows = pack_g[:, 3 * l:3 * l + 3, :].reshape(N_DEV, 3 * d)
        extra = dmodc_sum[None] if l == 0 else jnp.zeros((1, 3 * d), F32)
        full = _pad_rows(jnp.concatenate([rows, extra], axis=0), 16)
        dm_cols.append(lax.dynamic_slice_in_dim(full, col0, ada_cols, axis=1))
    g_ada_w = jnp.stack([matmul(sc16, dm_cols[l], mode="tn", tm=512, tn=ada_cols, tk=16, out_dtype=F32,
                                name=f"d_ada_w{l}") for l in range(2)])
    dsc = matmul(dm_cols[0], ada_bf[0], mode="nt", tm=16, tn=512, tk=ada_cols, out_dtype=F32, name="d_scc")
    (_, dscc_sum) = all_gather([dsc[8:16]], name="gather_dscc", hbm=False, sum_out=True)
    sg = jax.nn.sigmoid(c_ctx)
    g_c_ctx = dscc_sum[0] * (sg * (1.0 + c_ctx * (1.0 - sg)))
    g_ada_b = jnp.stack([gsum[0:3].reshape(-1) + dmodc_sum, gsum[3:6].reshape(-1)])

    upd("ada_w", ada_w, g_ada_w, m_ada_w, v_ada_w)

    def my_shard(full, size):
        return lax.dynamic_slice_in_dim(full, me * size, size, axis=full.ndim - 1)

    small_items = [
        ("c_ctx", c_ctx, g_c_ctx, m_c_ctx, v_c_ctx),
        ("ada_b", ada_b, g_ada_b, m_ada_b, v_ada_b),
        ("norm_g", norm_g, gsum[8:10], m_norm_g, v_norm_g),
        ("ev_q_norm", ev_q_norm, gsum[14, conv_ch:conv_ch + HEAD_DIM], m_ev_q_norm, v_ev_q_norm),
        ("ev_k_norm", ev_k_norm, gsum[14, conv_ch + HEAD_DIM:conv_ch + 2 * HEAD_DIM], m_ev_k_norm, v_ev_k_norm),
        ("ev_dw_w", ev_dw_w, my_shard(g_dw_w, conv_ch // N_DEV), m_ev_dw_w, v_ev_dw_w),
        ("ev_dw_b", ev_dw_b, gsum[13, :conv_ch], m_ev_dw_b, v_ev_dw_b),
        ("ev_ln_g", ev_ln_g, gsum[13, conv_ch:2 * conv_ch], m_ev_ln_g, v_ev_ln_g),
        ("ev_ln_b", ev_ln_b, gsum[14, :conv_ch], m_ev_ln_b, v_ev_ln_b),
        ("od_ln_g", od_ln_g, my_shard(gsum[11], shard), m_od_ln_g, v_od_ln_g),
        ("od_ln_b", od_ln_b, my_shard(gsum[12], shard), m_od_ln_b, v_od_ln_b),
        ("od_ws", od_ws, g_od_ws, m_od_ws, v_od_ws),
        ("od_bs", od_bs, gsum[15, :SGU_GROUPS * CHUNK], m_od_bs, v_od_bs),
        ("final_g", final_g, gsum[10], m_final_g, v_final_g),
    ]
    sizes = [it[1].size for it in small_items]
    total = sum(sizes)
    lanes = 1024
    prow = -(-total // lanes)
    prow = -(-prow // 8) * 8

    def pack_small(idx):
        flat = jnp.concatenate([it[idx].reshape(-1).astype(F32) for it in small_items])
        return jnp.pad(flat, (0, prow * lanes - total)).reshape(prow, lanes)

    sres = adamw(pack_small(1), pack_small(2), pack_small(3), pack_small(4), name="adamw_small", rows=prow)
    off = 0
    for it, size in zip(small_items, sizes):
        out[it[0]] = tuple(r.reshape(-1)[off:off + size].reshape(it[1].shape) for r in sres)
        off += size

    names = ['c_ctx', 'ada_w', 'ada_b', 'norm_g', 'ev_w_in', 'ev_q_norm', 'ev_k_norm', 'ev_dw_w', 'ev_dw_b',
             'ev_ln_g', 'ev_ln_b', 'ev_w_out', 'od_w_in', 'od_ln_g', 'od_ln_b', 'od_ws', 'od_bs', 'od_w_out',
             'final_g']
    return (loss, grad_x[None], *[out[k][0] for k in names], *[out[k][1] for k in names],
            *[out[k][2] for k in names], *[out[k][3] for k in names])
```

```python
import functools
import math

import jax
import jax.numpy as jnp
from jax import lax
from jax.experimental import pallas as pl
from jax.experimental.pallas import tpu as pltpu

F32 = jnp.float32
BF16 = jnp.bfloat16
MESH = pl.DeviceIdType.MESH

EPS = 1e-6
HEAD_DIM = 128
N_Q_HEADS = 8
N_KV_HEADS = 2
Q_PER_KV = N_Q_HEADS // N_KV_HEADS
ATTN_W = N_Q_HEADS * HEAD_DIM
KV_W = N_KV_HEADS * HEAD_DIM
ATTN_SCALE = HEAD_DIM ** -0.5
LN2 = math.log(2.0)
Q_SCALE = ATTN_SCALE / LN2
ROPE_THETA = 10000.0
GRID_W = 64
CONV_WIDTH = 31
HALO = 16
CHUNK = 128
SGU_GROUPS = 8
N_DEV = 8

ADAM_LR = 0.001
ADAM_B1 = 0.9
ADAM_B2 = 0.999
ADAM_EPS = 1e-08
ADAM_WD = 0.01
ADAM_STEP = 10

VMEM_LIMIT = 56 * 1024 * 1024
ANY = pl.BlockSpec(memory_space=pl.ANY)
VMEM_SPEC = pl.BlockSpec(memory_space=pltpu.VMEM)


def _params(*sem):
    return pltpu.CompilerParams(dimension_semantics=sem, vmem_limit_bytes=VMEM_LIMIT)


def _sigmoid(x):
    return 1.0 / (1.0 + jnp.exp(-x))


def _silu(x):
    return x * _sigmoid(x)


def _dsilu(x):
    s = _sigmoid(x)
    return s * (1.0 + x * (1.0 - s))


_GELU_C = math.sqrt(2.0 / math.pi)


def _gelu(x):
    t = jnp.tanh(_GELU_C * (x + 0.044715 * (x * x * x)))
    return 0.5 * x * (1.0 + t)


def _dgelu(x):
    t = jnp.tanh(_GELU_C * (x + 0.044715 * (x * x * x)))
    return 0.5 * (1.0 + t) + 0.5 * x * (1.0 - t * t) * (_GELU_C * (1.0 + 3.0 * 0.044715 * (x * x)))


def _row(v):
    return v.reshape(1, -1).astype(F32)


def _flat_id(p):
    return 4 * p[0] + 2 * p[1] + p[2]


def _gather_body(n_arr, sum_out):
    def body(*refs):
        x_refs = refs[:n_arr]
        out_refs = refs[n_arr:2 * n_arr]
        pos = 2 * n_arr
        sum_refs = refs[pos:pos + n_arr] if sum_out else ()
        pos += n_arr if sum_out else 0
        send_sems, recv_sems, local_sems = refs[pos:pos + 3]
        x, y, c = lax.axis_index("x"), lax.axis_index("y"), lax.axis_index("c")
        me, sibling = (x, y, c), (x, y, 1 - c)
        chips = [(1 - x, y), (x, 1 - y), (1 - x, 1 - y)]

        def copy(a, k, block, to, src=None):
            rows = out_refs[a].at[_flat_id(block)]
            return pltpu.make_async_remote_copy(
                src_ref=rows if src is None else src, dst_ref=rows,
                send_sem=send_sems.at[a, k], recv_sem=recv_sems.at[a, k],
                device_id=to, device_id_type=MESH)

        sends = []
        mine = []
        for a in range(n_arr):
            cp = pltpu.make_async_copy(x_refs[a], out_refs[a].at[_flat_id(me)], local_sems.at[a])
            cp.start()
            mine.append(cp)
            first = [copy(a, 0, me, sibling, src=x_refs[a])]
            first += [copy(a, 1 + j, me, (*chip, c), src=x_refs[a]) for j, chip in enumerate(chips)]
            for cp in first:
                cp.start()
            sends += first
        for a in range(n_arr):
            for j, chip in enumerate(chips):
                copy(a, 1 + j, (*chip, c), me).wait_recv()
                fwd = copy(a, 4 + j, (*chip, c), sibling)
                fwd.start()
                sends.append(fwd)
        for a in range(n_arr):
            copy(a, 0, sibling, me).wait_recv()
            for j, chip in enumerate(chips):
                copy(a, 4 + j, (*chip, 1 - c), me).wait_recv()
        for cp in sends:
            cp.wait_send()
        for cp in mine:
            cp.wait()
        if sum_out:
            for a in range(n_arr):
                acc = out_refs[a][0]
                for k in range(1, N_DEV):
                    acc = acc + out_refs[a][k]
                sum_refs[a][...] = acc

    return body


def all_gather(arrs, *, name, sum_out=False):
    n = len(arrs)
    spec = VMEM_SPEC
    out_shape = [jax.ShapeDtypeStruct((N_DEV,) + a.shape, a.dtype) for a in arrs]
    out_specs = [spec] * n
    if sum_out:
        out_shape += [jax.ShapeDtypeStruct(a.shape, a.dtype) for a in arrs]
        out_specs += [VMEM_SPEC] * n
    res = pl.pallas_call(
        _gather_body(n, sum_out), name=name,
        out_shape=out_shape, in_specs=[spec] * n, out_specs=out_specs,
        scratch_shapes=[pltpu.SemaphoreType.DMA((n, 7)), pltpu.SemaphoreType.DMA((n, 7)),
                        pltpu.SemaphoreType.DMA((n,))],
        compiler_params=pltpu.CompilerParams(vmem_limit_bytes=VMEM_LIMIT),
    )(*arrs)
    return res


_RELATIONS = [(rx, ry, rc) for rx in (0, 1) for ry in (0, 1) for rc in (0, 1)][1:]


HBM_SPEC = pl.BlockSpec(memory_space=pltpu.HBM)
SEM_SPEC = pl.BlockSpec(memory_space=pltpu.SEMAPHORE)
_DATAFLOW = pltpu.SideEffectType.DATAFLOW_SIDE_EFFECTING
N_PEERS = N_DEV - 1


def _peer_copy(src_ref, land_ref, send_sem, recv_sem, k, rel, scatter, sending):
    x, y, c = lax.axis_index("x"), lax.axis_index("y"), lax.axis_index("c")
    rx, ry, rc = rel
    peer = (1 - x if rx else x, 1 - y if ry else y, 1 - c if rc else c)
    src = src_ref.at[_flat_id(peer)] if scatter else src_ref
    dst = land_ref.at[_flat_id((x, y, c)) if sending else _flat_id(peer)]
    return pltpu.make_async_remote_copy(src_ref=src, dst_ref=dst, send_sem=send_sem.at[k], recv_sem=recv_sem.at[k],
                                        device_id=peer, device_id_type=MESH)


def exchange_start(arrs, *, scatter, name):
    n = len(arrs)
    lands = [lax.empty((N_DEV,) + (a.shape[1:] if scatter else a.shape), a.dtype) for a in arrs]

    def body(*refs):
        srcs, lnds, sems = refs[:n], refs[n:2 * n], refs[2 * n:4 * n]
        token = refs[6 * n]
        for a in range(n):
            for k, rel in enumerate(_RELATIONS):
                _peer_copy(srcs[a], lnds[a], sems[2 * a], sems[2 * a + 1], k, rel, scatter, True).start()
        token[...] = jnp.zeros_like(token)

    outs = pl.pallas_call(
        body, name=name,
        out_shape=[pltpu.SemaphoreType.DMA((N_PEERS,))] * (2 * n)
        + [pltpu.HBM(a.shape, a.dtype) for a in arrs] + [pltpu.HBM(l.shape, l.dtype) for l in lands]
        + [jax.ShapeDtypeStruct((8, 128), F32)],
        in_specs=[HBM_SPEC] * (2 * n),
        out_specs=[SEM_SPEC] * (2 * n) + [HBM_SPEC] * (2 * n) + [VMEM_SPEC],
        input_output_aliases={i: 2 * n + i for i in range(2 * n)},
        compiler_params=pltpu.CompilerParams(has_side_effects=_DATAFLOW),
    )(*[pltpu.with_memory_space_constraint(a, pltpu.HBM) for a in arrs],
      *[pltpu.with_memory_space_constraint(l, pltpu.HBM) for l in lands])
    handles = [(outs[2 * a], outs[2 * a + 1], outs[2 * n + a], outs[3 * n + a]) for a in range(n)]
    return handles, outs[4 * n]


def exchange_wait(handle, after, *, scatter, name):
    send_sem, recv_sem, src, land = handle

    def body(src_ref, land_ref, send_ref, recv_ref, after_ref, src_out, land_out):
        for k, rel in enumerate(_RELATIONS):
            cp = _peer_copy(src_ref, land_ref, send_ref, recv_ref, k, rel, scatter, False)
            cp.wait_send()
            cp.wait_recv()

    outs = pl.pallas_call(
        body, name=name,
        out_shape=[pltpu.HBM(src.shape, src.dtype), pltpu.HBM(land.shape, land.dtype)],
        in_specs=[HBM_SPEC, HBM_SPEC, SEM_SPEC, SEM_SPEC, ANY],
        out_specs=[HBM_SPEC, HBM_SPEC], input_output_aliases={0: 0, 1: 1},
        compiler_params=pltpu.CompilerParams(has_side_effects=_DATAFLOW),
    )(src, land, send_sem, recv_sem, after)
    return outs[0], outs[1]


_STAGE1 = [(0, 0, 1), (1, 0, 0), (0, 1, 0), (1, 1, 0)]
_OTHER_CHIPS = [(1, 0), (0, 1), (1, 1)]


def _stage_copy(stage, k, shard_ref, land_ref, send_sem, recv_sem, sending):
    x, y, c = lax.axis_index("x"), lax.axis_index("y"), lax.axis_index("c")
    if stage == 1:
        rx, ry, rc = _STAGE1[k]
        peer = (1 - x if rx else x, 1 - y if ry else y, 1 - c if rc else c)
        src = shard_ref
        dst = land_ref.at[_flat_id((x, y, c)) if sending else _flat_id(peer)]
    else:
        cx, cy = _OTHER_CHIPS[k]
        peer = (x, y, 1 - c)
        chip = (1 - x if cx else x, 1 - y if cy else y)
        slot = _flat_id((*chip, c)) if sending else _flat_id((*chip, 1 - c))
        src = land_ref.at[_flat_id((*chip, c))]
        dst = land_ref.at[slot]
    return pltpu.make_async_remote_copy(src_ref=src, dst_ref=dst, send_sem=send_sem.at[k], recv_sem=recv_sem.at[k],
                                        device_id=peer, device_id_type=MESH)


def staged_gather_start(stage, shard, land, *, name):
    n_copies = len(_STAGE1) if stage == 1 else len(_OTHER_CHIPS)
    if land is None:
        land = lax.empty((N_DEV,) + shard.shape, shard.dtype)

    def body(shard_ref, land_ref, send_sem, recv_sem, shard_thru, land_thru, token):
        for k in range(n_copies):
            _stage_copy(stage, k, shard_ref, land_ref, send_sem, recv_sem, True).start()
        token[...] = jnp.zeros_like(token)

    outs = pl.pallas_call(
        body, name=name,
        out_shape=[pltpu.SemaphoreType.DMA((n_copies,)), pltpu.SemaphoreType.DMA((n_copies,)),
                   pltpu.HBM(shard.shape, shard.dtype), pltpu.HBM(land.shape, land.dtype),
                   jax.ShapeDtypeStruct((8, 128), F32)],
        in_specs=[HBM_SPEC, HBM_SPEC],
        out_specs=[SEM_SPEC, SEM_SPEC, HBM_SPEC, HBM_SPEC, VMEM_SPEC],
        input_output_aliases={0: 2, 1: 3},
        compiler_params=pltpu.CompilerParams(has_side_effects=_DATAFLOW),
    )(pltpu.with_memory_space_constraint(shard, pltpu.HBM), pltpu.with_memory_space_constraint(land, pltpu.HBM))
    return outs[:4], outs[4]


def staged_gather_wait(stage, handle, after, *, name):
    send_sem, recv_sem, shard, land = handle
    n_copies = len(_STAGE1) if stage == 1 else len(_OTHER_CHIPS)

    def body(shard_ref, land_ref, send_ref, recv_ref, after_ref, shard_out, land_out):
        for k in range(n_copies):
            cp = _stage_copy(stage, k, shard_ref, land_ref, send_ref, recv_ref, False)
            cp.wait_send()
            cp.wait_recv()

    outs = pl.pallas_call(
        body, name=name,
        out_shape=[pltpu.HBM(shard.shape, shard.dtype), pltpu.HBM(land.shape, land.dtype)],
        in_specs=[HBM_SPEC, HBM_SPEC, SEM_SPEC, SEM_SPEC, ANY],
        out_specs=[HBM_SPEC, HBM_SPEC], input_output_aliases={0: 0, 1: 1},
        compiler_params=pltpu.CompilerParams(has_side_effects=_DATAFLOW),
    )(shard, land, send_sem, recv_sem, after)
    return outs[0], outs[1]


_DIMS = {"nn": (((1,), (0,)), ((), ())), "nt": (((1,), (1,)), ((), ())), "tn": (((0,), (0,)), ((), ()))}


def matmul(a, b, *, mode, tm, tn, tk, out_dtype, name, n_out=None, res=None, gate=None, add=None, after=None,
           split_out=False, norm=None):
    if mode == "tn":
        kdim, m = a.shape
    else:
        m, kdim = a.shape
    nfull = b.shape[0] if mode == "nt" else b.shape[1]
    n = nfull if n_out is None else n_out
    tm, tn, tk = min(tm, m), min(tn, n), min(tk, kdim)
    assert m % tm == 0 and n % tn == 0 and kdim % tk == 0, (name, m, n, kdim, tm, tn, tk)
    nk = kdim // tk
    dims = _DIMS[mode]
    a_spec = pl.BlockSpec((tk, tm), lambda j, i, k: (k, i)) if mode == "tn" else pl.BlockSpec((tm, tk), lambda j, i, k: (i, k))
    b_spec = pl.BlockSpec((tn, tk), lambda j, i, k: (j, k)) if mode == "nt" else pl.BlockSpec((tk, tn), lambda j, i, k: (k, j))
    o_spec = pl.BlockSpec((tm, tn), lambda j, i, k: (i, j))
    in_specs = [a_spec, b_spec]
    operands = [a, b]
    aliases = {}
    vec = pl.BlockSpec((1, tn), lambda j, i, k: (0, j))
    if res is not None:
        in_specs += [o_spec, vec]
        operands += [res, gate]
    if norm is not None:
        assert res is not None and tn == n, (name, tn, n)
        in_specs += [vec, vec, vec]
        operands += [_row(v) for v in norm]
    if add is not None:
        in_specs += [o_spec]
        aliases = {len(operands): 0}
        operands += [add]
    if after is not None:
        in_specs += [ANY]
        operands += [after]
    out_cols = n if add is None else add.shape[1]
    out_shape = [jax.ShapeDtypeStruct((m, out_cols), out_dtype)]
    out_specs = [o_spec]
    if split_out:
        out_shape = [jax.ShapeDtypeStruct((n // tn, m, tn), out_dtype)]
        out_specs = [pl.BlockSpec((None, tm, tn), lambda j, i, k: (j, i, 0))]
    if res is not None:
        out_shape.append(jax.ShapeDtypeStruct((m, n), BF16))
        out_specs.append(o_spec)
    if norm is not None:
        out_shape.append(jax.ShapeDtypeStruct((m, n), BF16))
        out_specs.append(o_spec)

    def body(*refs):
        a_ref, b_ref = refs[:2]
        pos = 2
        if res is not None:
            res_ref, gate_ref = refs[pos:pos + 2]
            pos += 2
        if norm is not None:
            ng_ref, nshift_ref, nscale_ref = refs[pos:pos + 3]
            pos += 3
        if add is not None:
            add_ref = refs[pos]
            pos += 1
        if after is not None:
            pos += 1
        o_ref = refs[pos]
        pos += 1
        if res is not None:
            r_ref = refs[pos]
            pos += 1
        if norm is not None:
            h_ref = refs[pos]
            pos += 1
        acc_ref = refs[pos] if nk > 1 else None
        prod = lax.dot_general(a_ref[...].astype(BF16), b_ref[...].astype(BF16), dims,
                               preferred_element_type=F32)

        def finish(acc):
            if res is not None:
                xnew = res_ref[...] + gate_ref[...] * acc
                o_ref[...] = xnew.astype(out_dtype)
                r_ref[...] = acc.astype(BF16)
                if norm is not None:
                    rstd = lax.rsqrt(jnp.mean(xnew * xnew, axis=-1, keepdims=True) + EPS)
                    y = xnew * rstd * ng_ref[...]
                    h_ref[...] = (y * (1.0 + nscale_ref[...]) + nshift_ref[...]).astype(BF16)
            elif add is not None:
                o_ref[...] = (add_ref[...] + acc).astype(out_dtype)
            else:
                o_ref[...] = acc.astype(out_dtype)

        if nk == 1:
            finish(prod)
        else:
            k = pl.program_id(2)

            @pl.when(k == 0)
            def _():
                acc_ref[...] = prod

            @pl.when(k > 0)
            def _():
                acc_ref[...] += prod

            @pl.when(k == nk - 1)
            def _():
                finish(acc_ref[...])

    outs = pl.pallas_call(
        body, name=name, grid=(n // tn, m // tm, nk),
        in_specs=in_specs, out_specs=out_specs, out_shape=out_shape,
        scratch_shapes=[pltpu.VMEM((tm, tn), F32)] if nk > 1 else [],
        input_output_aliases=aliases,
        compiler_params=_params("parallel", "parallel", "arbitrary"),
    )(*operands)
    return outs if res is not None else outs[0]


def _rows_tile(n, want):
    return min(n, want)


def norm_mod_fwd(x, g, shift, scale, *, name):
    n, d = x.shape
    tt = _rows_tile(n, 256)

    def body(x_ref, g_ref, sh_ref, sc_ref, h_ref):
        xv = x_ref[...]
        rstd = lax.rsqrt(jnp.mean(xv * xv, axis=-1, keepdims=True) + EPS)
        y = xv * rstd * g_ref[...]
        h_ref[...] = (y * (1.0 + sc_ref[...]) + sh_ref[...]).astype(BF16)

    vec = pl.BlockSpec((1, d), lambda i: (0, 0))
    return pl.pallas_call(
        body, name=name, grid=(n // tt,),
        in_specs=[pl.BlockSpec((tt, d), lambda i: (i, 0)), vec, vec, vec],
        out_specs=pl.BlockSpec((tt, d), lambda i: (i, 0)),
        out_shape=jax.ShapeDtypeStruct((n, d), BF16),
        compiler_params=_params("parallel"),
    )(x, _row(g), _row(shift), _row(scale))


GATE_ROW = 3


def norm_mod_bwd(x, dh, dres, g, scale, *, name, branch=None):
    n, d = x.shape
    tt = _rows_tile(n, 256)
    has_res = dres is not None
    has_branch = branch is not None
    last = n // tt - 1

    def body(*refs):
        x_ref, dh_ref = refs[:2]
        pos = 2
        if has_res:
            dres_ref = refs[pos]
            pos += 1
        if has_branch:
            r_ref, gate_ref = refs[pos:pos + 2]
            pos += 2
        g_ref, sc_ref, dx_ref = refs[pos:pos + 3]
        pos += 3
        if has_branch:
            dr_ref = refs[pos]
            pos += 1
        acc_ref, s_ref = refs[pos:pos + 2]
        i = pl.program_id(0)
        xv = x_ref[...]
        dhv = dh_ref[...]
        rstd = lax.rsqrt(jnp.mean(xv * xv, axis=-1, keepdims=True) + EPS)
        xhat = xv * rstd
        dxhat = dhv * (g_ref[...] * (1.0 + sc_ref[...]))
        dx = rstd * (dxhat - xhat * jnp.mean(dxhat * xhat, axis=-1, keepdims=True))
        if has_res:
            dx = dx + dres_ref[...]
        dx_ref[...] = dx

        @pl.when(i == 0)
        def _():
            s_ref[...] = jnp.zeros_like(s_ref)

        s_ref[0:1, :] += jnp.sum(dhv, axis=0, keepdims=True)
        s_ref[1:2, :] += jnp.sum(dhv * xhat, axis=0, keepdims=True)
        if has_branch:
            dr_ref[...] = (dx * gate_ref[...]).astype(BF16)
            s_ref[2:3, :] += jnp.sum(dx * r_ref[...].astype(F32), axis=0, keepdims=True)

        @pl.when(i == last)
        def _():
            s1 = s_ref[0:1, :]
            s2 = s_ref[1:2, :]
            acc_ref[...] = jnp.zeros_like(acc_ref)
            acc_ref[0:1, :] = s1
            acc_ref[1:2, :] = s2 * g_ref[...]
            acc_ref[2:3, :] = s2 * (1.0 + sc_ref[...])
            acc_ref[GATE_ROW:GATE_ROW + 1, :] = s_ref[2:3, :]

    vec = pl.BlockSpec((1, d), lambda i: (0, 0))
    big = pl.BlockSpec((tt, d), lambda i: (i, 0))
    ops = [x, dh] + ([dres] if has_res else []) + ([branch[0], _row(branch[1])] if has_branch else [])
    ops += [_row(g), _row(scale)]
    return pl.pallas_call(
        body, name=name, grid=(n // tt,),
        in_specs=[big, big] + ([big] if has_res else []) + ([big, vec] if has_branch else []) + [vec, vec],
        out_specs=[big] + ([big] if has_branch else []) + [pl.BlockSpec((8, d), lambda i: (0, 0))],
        out_shape=[jax.ShapeDtypeStruct((n, d), F32)] + ([jax.ShapeDtypeStruct((n, d), BF16)] if has_branch else [])
        + [jax.ShapeDtypeStruct((8, d), F32)],
        scratch_shapes=[pltpu.VMEM((8, d), F32)],
        compiler_params=_params("arbitrary"),
    )(*ops)


def proj_out_loss(a, b, x_prev, gate, target, g, *, name):
    n, kdim = a.shape
    d = b.shape[1]
    tt = _rows_tile(n, 256)

    def body(a_ref, b_ref, x_ref, t_ref, g_ref, gate_ref, dx_ref, dr_ref, acc_ref, loss_ref):
        i = pl.program_id(0)
        r = jnp.dot(a_ref[...], b_ref[...], preferred_element_type=F32)
        xv = x_ref[...] + gate_ref[...] * r
        rstd = lax.rsqrt(jnp.mean(xv * xv, axis=-1, keepdims=True) + EPS)
        xhat = xv * rstd
        e = xhat * g_ref[...] - t_ref[...]
        dy = e * (1.0 / d)
        dxhat = dy * g_ref[...]
        dx = rstd * (dxhat - xhat * jnp.mean(dxhat * xhat, axis=-1, keepdims=True))
        dx_ref[...] = dx
        dr_ref[...] = (dx * gate_ref[...]).astype(BF16)

        @pl.when(i == 0)
        def _():
            acc_ref[...] = jnp.zeros_like(acc_ref)
            loss_ref[...] = jnp.zeros_like(loss_ref)

        acc_ref[0:1, :] += jnp.sum(dy * xhat, axis=0, keepdims=True)
        acc_ref[GATE_ROW:GATE_ROW + 1, :] += jnp.sum(dx * r, axis=0, keepdims=True)
        part = 0.5 * jnp.sum(jnp.mean(e * e, axis=-1, keepdims=True), axis=0, keepdims=True)
        loss_ref[...] += jnp.broadcast_to(part, loss_ref.shape)

    big = pl.BlockSpec((tt, d), lambda i: (i, 0))
    vec = pl.BlockSpec((1, d), lambda i: (0, 0))
    return pl.pallas_call(
        body, name=name, grid=(n // tt,),
        in_specs=[pl.BlockSpec((tt, kdim), lambda i: (i, 0)), pl.BlockSpec((kdim, d), lambda i: (0, 0)),
                  big, big, vec, vec],
        out_specs=[big, big, pl.BlockSpec((8, d), lambda i: (0, 0)), pl.BlockSpec((8, 128), lambda i: (0, 0))],
        out_shape=[jax.ShapeDtypeStruct((n, d), F32), jax.ShapeDtypeStruct((n, d), BF16),
                   jax.ShapeDtypeStruct((8, d), F32), jax.ShapeDtypeStruct((8, 128), F32)],
        compiler_params=_params("arbitrary"),
    )(a, b, x_prev, target, _row(g), _row(gate))


def _as_row(col):
    t = col.shape[0]
    return jnp.transpose(jnp.broadcast_to(col, (t, HEAD_DIM)))[0:1, :]


def _swap_pairs(x):
    lane = lax.broadcasted_iota(jnp.int32, x.shape, 1)
    return jnp.where(lane % 2 == 0, pltpu.roll(x, HEAD_DIM - 1, 1), pltpu.roll(x, 1, 1))


def rope_tables(n):
    rows = n // GRID_W
    row = jnp.repeat(jnp.arange(rows, dtype=F32), GRID_W)
    col = jnp.tile(jnp.arange(GRID_W, dtype=F32), rows)
    n_freq = HEAD_DIM // 4
    inv = jnp.power(ROPE_THETA, jnp.arange(n_freq, dtype=F32) * (-2.0 / (HEAD_DIM // 2)))
    ang = jnp.concatenate([row[:, None] * inv, col[:, None] * inv], axis=-1)
    cos, sin = jnp.cos(ang), jnp.sin(ang)
    cexp = jnp.repeat(cos, 2, axis=-1)
    sexp = jnp.stack([-sin, sin], axis=-1).reshape(n, HEAD_DIM)
    return cexp, sexp


V_EXT_W = 2 * HEAD_DIM


def qkv_prep_fwd(p, wq, wk, cexp, sexp, *, latent, name, rows_all=None, k_all=None, v_ext=None):
    n = p.shape[0]
    tt = _rows_tile(n, 256)
    width = 2 * KV_W + (ATTN_W if latent else 0)

    def body(*refs):
        if latent:
            p_ref, wq_ref, wk_ref, c_ref, s_ref, q_ref, k_ref, v_ref = refs
        else:
            p_ref, wk_ref, _, _, k_ref, v_ref = refs

        def head(xv, w):
            rstd = lax.rsqrt(jnp.mean(xv * xv, axis=-1, keepdims=True) + EPS)
            yv = xv * rstd * w
            if latent:
                yv = yv * c_ref[...] + _swap_pairs(yv) * s_ref[...]
            return yv

        for h in range(N_KV_HEADS):
            sl = slice(h * HEAD_DIM, (h + 1) * HEAD_DIM)
            k_ref[:, sl] = head(p_ref[:, sl], wk_ref[...]).astype(BF16)
            v_ref[:, h * V_EXT_W:h * V_EXT_W + HEAD_DIM] = p_ref[:, KV_W + h * HEAD_DIM:KV_W + (h + 1) * HEAD_DIM].astype(BF16)
            lane = lax.broadcasted_iota(jnp.int32, (tt, HEAD_DIM), 1)
            v_ref[:, h * V_EXT_W + HEAD_DIM:(h + 1) * V_EXT_W] = jnp.where(lane == 0, 1.0, 0.0).astype(BF16)
        if latent:
            for h in range(N_Q_HEADS):
                sl = slice(2 * KV_W + h * HEAD_DIM, 2 * KV_W + (h + 1) * HEAD_DIM)
                q_ref[:, h * HEAD_DIM:(h + 1) * HEAD_DIM] = (head(p_ref[:, sl], wq_ref[...]) * Q_SCALE).astype(BF16)

    vec = pl.BlockSpec((1, HEAD_DIM), lambda i: (0, 0))
    tab = pl.BlockSpec((tt, HEAD_DIM), lambda i: (i, 0))
    vw = N_KV_HEADS * V_EXT_W
    k_shape = jax.ShapeDtypeStruct((rows_all, KV_W), BF16)
    v_shape = jax.ShapeDtypeStruct((rows_all, vw), BF16)
    aliases = {}
    if latent:
        in_specs = [pl.BlockSpec((tt, width), lambda i: (i, 0)), vec, vec, tab, tab]
        ops = [p, _row(wq), _row(wk), cexp, sexp]
        out_specs = [pl.BlockSpec((tt, ATTN_W), lambda i: (i, 0)), pl.BlockSpec((tt, KV_W), lambda i: (i, 0)),
                     pl.BlockSpec((tt, vw), lambda i: (i, 0))]
        out_shape = [jax.ShapeDtypeStruct((n, ATTN_W), BF16), k_shape, v_shape]
    else:
        assert n == tt and (rows_all - n) % n == 0, (n, tt, rows_all)
        first = (rows_all - n) // n
        in_specs = [pl.BlockSpec((tt, width), lambda i: (i, 0)), vec, ANY, ANY]
        ops = [p, _row(wk), k_all, v_ext]
        out_specs = [pl.BlockSpec((tt, KV_W), lambda i: (first, 0)), pl.BlockSpec((tt, vw), lambda i: (first, 0))]
        out_shape = [k_shape, v_shape]
        aliases = {2: 0, 3: 1}
    return pl.pallas_call(
        body, name=name, grid=(n // tt,), in_specs=in_specs, out_specs=out_specs, out_shape=out_shape,
        input_output_aliases=aliases, compiler_params=_params("parallel"),
    )(*ops)


def qkv_prep_bwd(p, dq, dk, dv, wq, wk, cexp, sexp, *, latent, name):
    n = p.shape[0]
    tt = _rows_tile(n, 256)
    width = 2 * KV_W + (ATTN_W if latent else 0)

    def body(*refs):
        if latent:
            p_ref, dq_ref, dk_ref, dv_ref, wq_ref, wk_ref, c_ref, s_ref, dp_ref, acc_ref = refs
        else:
            p_ref, dk_ref, dv_ref, wk_ref, dp_ref, acc_ref = refs
        i = pl.program_id(0)

        @pl.when(i == 0)
        def _():
            acc_ref[...] = jnp.zeros_like(acc_ref)

        def head(xv, dy, w, row):
            if latent:
                dy = dy * c_ref[...] + _swap_pairs(dy * s_ref[...])
            rstd = lax.rsqrt(jnp.mean(xv * xv, axis=-1, keepdims=True) + EPS)
            xhat = xv * rstd
            acc_ref[row:row + 1, :] += jnp.sum(dy * xhat, axis=0, keepdims=True)
            dxhat = dy * w
            return rstd * (dxhat - xhat * jnp.mean(dxhat * xhat, axis=-1, keepdims=True))

        for h in range(N_KV_HEADS):
            sl = slice(h * HEAD_DIM, (h + 1) * HEAD_DIM)
            dp_ref[:, sl] = head(p_ref[:, sl], dk_ref[:, sl], wk_ref[...], 1).astype(BF16)
        dp_ref[:, KV_W:2 * KV_W] = dv_ref[...].astype(BF16)
        if latent:
            for h in range(N_Q_HEADS):
                sl = slice(2 * KV_W + h * HEAD_DIM, 2 * KV_W + (h + 1) * HEAD_DIM)
                dyq = dq_ref[:, h * HEAD_DIM:(h + 1) * HEAD_DIM] * Q_SCALE
                dp_ref[:, sl] = head(p_ref[:, sl], dyq, wq_ref[...], 0).astype(BF16)

    vec = pl.BlockSpec((1, HEAD_DIM), lambda i: (0, 0))
    tab = pl.BlockSpec((tt, HEAD_DIM), lambda i: (i, 0))
    first = 0 if latent else (dk.shape[0] - n) // tt
    kv_spec = pl.BlockSpec((tt, KV_W), lambda i: (first + i, 0))
    p_spec = pl.BlockSpec((tt, width), lambda i: (i, 0))
    if latent:
        in_specs = [p_spec, pl.BlockSpec((tt, ATTN_W), lambda i: (i, 0)), kv_spec, kv_spec, vec, vec, tab, tab]
        ops = [p, dq, dk, dv, _row(wq), _row(wk), cexp, sexp]
    else:
        in_specs = [p_spec, kv_spec, kv_spec, vec]
        ops = [p, dk, dv, _row(wk)]
    return pl.pallas_call(
        body, name=name, grid=(n // tt,), in_specs=in_specs,
        out_specs=[p_spec, pl.BlockSpec((8, HEAD_DIM), lambda i: (0, 0))],
        out_shape=[jax.ShapeDtypeStruct((n, width), BF16), jax.ShapeDtypeStruct((8, HEAD_DIM), F32)],
        compiler_params=_params("arbitrary"),
    )(*ops)


def _kv_chunks(n, s_all, want=1024):
    step = want if n % want == 0 else 256
    chunks = [(s, step) for s in range(0, n, step)]
    if s_all > n:
        chunks.append((n, s_all - n))
    return chunks


def flash_fwd(q, k_all, v_ext, p, *, za_block, mix_width, name):
    n = q.shape[0]
    s_all = k_all.shape[0]
    tq = _rows_tile(n, 512)
    chunks = _kv_chunks(n, s_all)
    wide = 2 * HEAD_DIM

    def body(q_ref, k_ref, v_ref, za_ref, o_ref, mix_ref, lse_ref, m_ref, acc_ref):
        qv = q_ref[...]
        m_ref[...] = jnp.full_like(m_ref, -jnp.inf)
        acc_ref[...] = jnp.zeros_like(acc_ref)
        for start, size in chunks:
            kc = k_ref[pl.ds(start, size), :]
            vc = v_ref[pl.ds(start, size), :]
            s = lax.dot_general(qv, kc, _DIMS["nt"], preferred_element_type=F32)
            m_old = m_ref[...]
            m_new = jnp.maximum(m_old, jnp.max(s, axis=-1, keepdims=True))
            pr = jnp.exp2(s - m_new)
            alpha = jnp.exp2(m_old - m_new)
            acc_ref[...] = alpha * acc_ref[...] + jnp.dot(pr.astype(BF16), vc, preferred_element_type=F32)
            m_ref[...] = m_new
        acc = acc_ref[...]
        denom = acc[:, HEAD_DIM:HEAD_DIM + 1]
        o = acc[:, :HEAD_DIM] / denom
        o_ref[...] = o.astype(BF16)
        mix_ref[...] = (o * _silu(za_ref[...])).astype(BF16)
        lse_ref[0] = _as_row(m_ref[...] + jnp.log2(denom))

    qspec = pl.BlockSpec((tq, HEAD_DIM), lambda h, i: (i, h))
    return pl.pallas_call(
        body, name=name, grid=(N_Q_HEADS, n // tq),
        in_specs=[qspec, pl.BlockSpec((s_all, HEAD_DIM), lambda h, i: (0, h // Q_PER_KV)),
                  pl.BlockSpec((s_all, wide), lambda h, i: (0, h // Q_PER_KV)),
                  pl.BlockSpec((tq, HEAD_DIM), lambda h, i: (i, za_block + h))],
        out_specs=[qspec, qspec, pl.BlockSpec((1, 1, tq), lambda h, i: (h, 0, i))],
        out_shape=[jax.ShapeDtypeStruct((n, ATTN_W), BF16), jax.ShapeDtypeStruct((n, mix_width), BF16),
                   jax.ShapeDtypeStruct((N_Q_HEADS, 1, n), F32)],
        scratch_shapes=[pltpu.VMEM((tq, 1), F32), pltpu.VMEM((tq, wide), F32)],
        compiler_params=_params("parallel", "parallel"),
    )(q, k_all, v_ext, p)


def attn_gate_bwd(dmix, o, p, *, za_block, name):
    n = o.shape[0]
    tt = _rows_tile(n, 512)
    za_half = za_block * HEAD_DIM // HALF
    n_half = ATTN_W // HALF

    def body(*refs):
        dm_ref, o_ref = refs[:2]
        za_refs = refs[2:2 + n_half]
        do_ref, dos_ref, dza_ref, delta_ref = refs[2 + n_half:]
        dm = dm_ref[...]
        ov = o_ref[...].astype(F32)
        za = jnp.concatenate([r[...] for r in za_refs], axis=-1)
        do = dm * _silu(za)
        do_ref[...] = do.astype(BF16)
        dos_ref[...] = (do * LN2).astype(BF16)
        dza_ref[...] = (dm * ov * _dsilu(za)).astype(BF16)
        prod = do * ov
        for h in range(N_Q_HEADS):
            col = jnp.sum(prod[:, h * HEAD_DIM:(h + 1) * HEAD_DIM], axis=-1, keepdims=True) * LN2
            delta_ref[h] = _as_row(col)

    spec = pl.BlockSpec((tt, ATTN_W), lambda i: (i, 0))
    shape = jax.ShapeDtypeStruct((n, ATTN_W), BF16)
    za_specs = [pl.BlockSpec((tt, HALF), functools.partial(lambda i, cb: (i, cb), cb=za_half + h))
                for h in range(n_half)]
    return pl.pallas_call(
        body, name=name, grid=(n // tt,),
        in_specs=[spec, spec] + za_specs,
        out_specs=[spec, spec, spec, pl.BlockSpec((N_Q_HEADS, 1, tt), lambda i: (0, 0, i))],
        out_shape=[shape, shape, shape, jax.ShapeDtypeStruct((N_Q_HEADS, 1, n), F32)],
        compiler_params=_params("parallel"),
    )(dmix, o, *([p] * n_half))


def flash_bwd(q, do, do_s, lse_row, delta_row, k_all, v_ext, *, name):
    n = q.shape[0]
    s_all = k_all.shape[0]
    tq = _rows_tile(n, 1024)
    chunks = _kv_chunks(n, s_all)

    def body(q_ref, do_ref, dos_ref, lse_ref, dl_ref, k_ref, v_ref, dq_ref, dk_ref, dv_ref):
        g = pl.program_id(1)
        i = pl.program_id(2)

        @pl.when((g == 0) & (i == 0))
        def _():
            dk_ref[...] = jnp.zeros_like(dk_ref)
            dv_ref[...] = jnp.zeros_like(dv_ref)

        qv = q_ref[...]
        dov = do_ref[...]
        dosv = dos_ref[...]
        lse = lse_ref[0]
        dl = dl_ref[0]
        dq = jnp.zeros((tq, HEAD_DIM), F32)
        for start, size in chunks:
            kc = k_ref[pl.ds(start, size), :]
            vc = v_ref[pl.ds(start, size), :]
            st = lax.dot_general(kc, qv, _DIMS["nt"], preferred_element_type=F32)
            pt = jnp.exp2(st - lse)
            dpt = lax.dot_general(vc, dosv, _DIMS["nt"], preferred_element_type=F32)
            dst = (pt * (dpt - dl)).astype(BF16)
            dv_ref[pl.ds(start, size), :] += jnp.dot(pt.astype(BF16), dov, preferred_element_type=F32)
            dk_ref[pl.ds(start, size), :] += jnp.dot(dst, qv, preferred_element_type=F32)
            dq = dq + lax.dot_general(dst, kc, _DIMS["tn"], preferred_element_type=F32)
        dq_ref[...] = dq

    qspec = pl.BlockSpec((tq, HEAD_DIM), lambda kh, g, i: (i, kh * Q_PER_KV + g))
    rowspec = pl.BlockSpec((1, 1, tq), lambda kh, g, i: (kh * Q_PER_KV + g, 0, i))
    kvspec = pl.BlockSpec((s_all, HEAD_DIM), lambda kh, g, i: (0, kh))
    return pl.pallas_call(
        body, name=name, grid=(N_KV_HEADS, Q_PER_KV, n // tq),
        in_specs=[qspec, qspec, qspec, rowspec, rowspec, kvspec,
                  pl.BlockSpec((s_all, HEAD_DIM), lambda kh, g, i: (0, kh * (V_EXT_W // HEAD_DIM)))],
        out_specs=[qspec, kvspec, kvspec],
        out_shape=[jax.ShapeDtypeStruct((n, ATTN_W), F32),
                   jax.ShapeDtypeStruct((s_all, KV_W), F32), jax.ShapeDtypeStruct((s_all, KV_W), F32)],
        compiler_params=_params("arbitrary", "arbitrary", "arbitrary"),
    )(q, do, do_s, lse_row, delta_row, k_all, v_ext)


HALF = 512


SUB = 128


def _tap_group(w_ref, w_lanes, ext_ref, r0, lanes, r, tap_of):
    z = None
    for a in range(4):
        k = tap_of(8 * a + r)
        if 0 <= k < CONV_WIDTH:
            term = w_ref[k:k + 1, w_lanes] * ext_ref[pl.ds(r0 + 8 * a, SUB + 8), lanes]
            z = term if z is None else z + term
    return z


def _shifted(z, z_ref, r):
    if r == 0:
        return z[0:SUB]
    z_ref[...] = z
    return z_ref[pl.ds(r, SUB), :]


def _fold8(x):
    return jnp.sum(x.reshape(x.shape[0] // 8, 8, x.shape[1]), axis=0)


def _pieces(tt, h):
    return [(rh * SUB, slice(c * SUB, (c + 1) * SUB), slice(h * HALF + c * SUB, h * HALF + (c + 1) * SUB))
            for c in range(HALF // SUB) for rh in range(tt // SUB)]


def _halo_specs(tt, n, cb):
    per = tt // HALO
    last = n // HALO - 1
    return [pl.BlockSpec((HALO, HALF), lambda i: (jnp.maximum(i * per - 1, 0), cb)),
            pl.BlockSpec((tt, HALF), lambda i: (i, cb)),
            pl.BlockSpec((HALO, HALF), lambda i: (jnp.minimum((i + 1) * per, last), cb))]


def conv_fwd(p, dw_w, dw_b, ln_g, ln_b, mix, *, glu_block, name):
    n = p.shape[0]
    ch = dw_w.shape[1]
    nh = ch // HALF
    tt = _rows_tile(n, 256)
    last = n // tt - 1

    def body(*refs):
        a_refs = [refs[3 * h:3 * h + 3] for h in range(nh)]
        b_refs = [refs[3 * (nh + h):3 * (nh + h) + 3] for h in range(nh)]
        pos = 6 * nh
        zb_refs = refs[pos:pos + nh]
        pos += nh
        w_ref, bias_ref, g_ref, be_ref, _, ycv_ref, mix_ref, ext_ref, z_ref = refs[pos:pos + 9]
        i = pl.program_id(0)
        for h in range(nh):
            cs = slice(h * HALF, (h + 1) * HALF)
            ap, am, an = a_refs[h]
            bp, bm, bn = b_refs[h]
            ext_ref[0:HALO, :] = jnp.where(i > 0, ap[...] * _sigmoid(bp[...]), 0.0)
            ext_ref[HALO:HALO + tt, :] = am[...] * _sigmoid(bm[...])
            ext_ref[HALO + tt:2 * HALO + tt, :] = jnp.where(i < last, an[...] * _sigmoid(bn[...]), 0.0)
            for r0, lanes, w_lanes in _pieces(tt, h):
                acc = jnp.broadcast_to(bias_ref[:, w_lanes], (SUB, SUB))
                for r in range(8):
                    z = _tap_group(w_ref, w_lanes, ext_ref, r0, lanes, r, lambda j: j - 1)
                    acc = acc + _shifted(z, z_ref, r)
                ycv_ref[pl.ds(r0, SUB), w_lanes] = acc
        yc = ycv_ref[...]
        mu = jnp.mean(yc, axis=-1, keepdims=True)
        var = jnp.mean(jnp.square(yc - mu), axis=-1, keepdims=True)
        ln = (yc - mu) * lax.rsqrt(var + EPS) * g_ref[...] + be_ref[...]
        out = _silu(ln)
        for h in range(nh):
            cs = slice(h * HALF, (h + 1) * HALF)
            mix_ref[:, cs] = (out[:, cs] * _silu(zb_refs[h][...])).astype(BF16)

    in_specs = []
    for h in range(2 * nh):
        in_specs += _halo_specs(tt, n, glu_block + h)
    in_specs += [pl.BlockSpec((tt, HALF), functools.partial(lambda i, cb: (i, cb), cb=glu_block + 2 * nh + h))
                 for h in range(nh)]
    vec = pl.BlockSpec((1, ch), lambda i: (0, 0))
    in_specs += [pl.BlockSpec((CONV_WIDTH, ch), lambda i: (0, 0)), vec, vec, vec, ANY]
    ops = [p] * (6 * nh + nh) + [dw_w, _row(dw_b), _row(ln_g), _row(ln_b), mix]
    big = pl.BlockSpec((tt, ch), lambda i: (i, 0))
    mix_block = (mix.shape[1] - ch) // ch
    return pl.pallas_call(
        body, name=name, grid=(n // tt,), in_specs=in_specs,
        out_specs=[big, pl.BlockSpec((tt, ch), lambda i: (i, mix_block))],
        out_shape=[jax.ShapeDtypeStruct((n, ch), F32), jax.ShapeDtypeStruct(mix.shape, BF16)],
        input_output_aliases={len(ops) - 1: 1},
        scratch_shapes=[pltpu.VMEM((tt + 2 * HALO, HALF), F32), pltpu.VMEM((SUB + 8, SUB), F32)],
        compiler_params=_params("parallel"),
    )(*ops)


def conv_bwd_rows(dmix, ycv, p, ln_g, ln_b, *, mix_block, zb_block, name):
    n, ch = ycv.shape
    nh = ch // HALF
    tt = _rows_tile(n, 256)

    def body(*refs):
        dm_ref, ycv_ref = refs[:2]
        zb_refs = refs[2:2 + nh]
        g_ref, be_ref, dy_ref, dzb_ref, acc_ref = refs[2 + nh:]
        i = pl.program_id(0)

        @pl.when(i == 0)
        def _():
            acc_ref[...] = jnp.zeros_like(acc_ref)

        yc = ycv_ref[...]
        mu = jnp.mean(yc, axis=-1, keepdims=True)
        var = jnp.mean(jnp.square(yc - mu), axis=-1, keepdims=True)
        rstd = lax.rsqrt(var + EPS)
        xhat = (yc - mu) * rstd
        ln = xhat * g_ref[...] + be_ref[...]
        out = _silu(ln)
        dm = dm_ref[...]
        zb = jnp.concatenate([r[...] for r in zb_refs], axis=-1)
        dzb_ref[...] = (dm * out * _dsilu(zb)).astype(BF16)
        dln = dm * _silu(zb) * _dsilu(ln)
        acc_ref[0:1, :] += jnp.sum(dln * xhat, axis=0, keepdims=True)
        acc_ref[1:2, :] += jnp.sum(dln, axis=0, keepdims=True)
        dxhat = dln * g_ref[...]
        dy_ref[...] = rstd * (dxhat - jnp.mean(dxhat, axis=-1, keepdims=True)
                              - xhat * jnp.mean(dxhat * xhat, axis=-1, keepdims=True))

    big = pl.BlockSpec((tt, ch), lambda i: (i, 0))
    vec = pl.BlockSpec((1, ch), lambda i: (0, 0))
    in_specs = [pl.BlockSpec((tt, ch), lambda i: (i, mix_block)), big]
    in_specs += [pl.BlockSpec((tt, HALF), functools.partial(lambda i, cb: (i, cb), cb=zb_block + h)) for h in range(nh)]
    in_specs += [vec, vec]
    return pl.pallas_call(
        body, name=name, grid=(n // tt,), in_specs=in_specs,
        out_specs=[big, big, pl.BlockSpec((8, ch), lambda i: (0, 0))],
        out_shape=[jax.ShapeDtypeStruct((n, ch), F32), jax.ShapeDtypeStruct((n, ch), BF16),
                   jax.ShapeDtypeStruct((8, ch), F32)],
        compiler_params=_params("arbitrary"),
    )(dmix, ycv, *([p] * nh), _row(ln_g), _row(ln_b))


def conv_bwd_taps(dycv, p, dw_w, *, glu_block, name):
    n, ch = dycv.shape
    nh = ch // HALF
    tt = _rows_tile(n, 256)
    last = n // tt - 1

    def body(*refs):
        d_refs = [refs[3 * h:3 * h + 3] for h in range(nh)]
        a_refs = [refs[3 * (nh + h):3 * (nh + h) + 3] for h in range(nh)]
        b_refs = [refs[3 * (2 * nh + h):3 * (2 * nh + h) + 3] for h in range(nh)]
        w_ref, dglu_ref, dw_ref, db_ref, yext_ref, dext_ref, z_ref, dy_ref, dwp_ref, dbp_ref = refs[9 * nh:]
        i = pl.program_id(0)

        @pl.when(i == 0)
        def _():
            dwp_ref[...] = jnp.zeros_like(dwp_ref)
            dbp_ref[...] = jnp.zeros_like(dbp_ref)

        for h in range(nh):
            cs = slice(h * HALF, (h + 1) * HALF)
            ap, am, an = a_refs[h]
            bp, bm, bn = b_refs[h]
            dp, dm, dn = d_refs[h]
            av = am[...]
            sb = _sigmoid(bm[...])
            yext_ref[0:HALO, :] = jnp.where(i > 0, ap[...] * _sigmoid(bp[...]), 0.0)
            yext_ref[HALO:HALO + tt, :] = av * sb
            yext_ref[HALO + tt:2 * HALO + tt, :] = jnp.where(i < last, an[...] * _sigmoid(bn[...]), 0.0)
            dmain = dm[...]
            dext_ref[0:HALO, :] = jnp.where(i > 0, dp[...], 0.0)
            dext_ref[HALO:HALO + tt, :] = dmain
            dext_ref[HALO + tt:2 * HALO + tt, :] = jnp.where(i < last, dn[...], 0.0)
            for r0, lanes, w_lanes in _pieces(tt, h):
                dsub = dext_ref[pl.ds(HALO + r0, SUB), lanes]
                dy = jnp.zeros((SUB, SUB), F32)
                for r in range(8):
                    z = _tap_group(w_ref, w_lanes, dext_ref, r0, lanes, r, lambda j: CONV_WIDTH - j)
                    dy = dy + _shifted(z, z_ref, r)
                    y_r = yext_ref[pl.ds(r0 + r, SUB + 24), lanes]
                    for a in range(4):
                        k = 8 * a + r - 1
                        if k >= 0:
                            dwp_ref[8 * k:8 * k + 8, w_lanes] += _fold8(dsub * y_r[8 * a:8 * a + SUB])
                dbp_ref[:, w_lanes] += _fold8(dsub)
                dy_ref[pl.ds(r0, SUB), lanes] = dy
            dy = dy_ref[...]
            dglu_ref[:, cs] = (dy * sb).astype(BF16)
            dglu_ref[:, ch + h * HALF:ch + (h + 1) * HALF] = (dy * av * sb * (1.0 - sb)).astype(BF16)

        @pl.when(i == last)
        def _():
            dw_ref[...] = jnp.sum(dwp_ref[...].reshape(32, 8, ch), axis=1)
            db_ref[...] = jnp.broadcast_to(jnp.sum(dbp_ref[...], axis=0, keepdims=True), db_ref.shape)

    in_specs = []
    for h in range(nh):
        in_specs += _halo_specs(tt, n, h)
    for h in range(2 * nh):
        in_specs += _halo_specs(tt, n, glu_block + h)
    in_specs += [pl.BlockSpec((CONV_WIDTH, ch), lambda i: (0, 0))]
    ops = [dycv] * (3 * nh) + [p] * (6 * nh) + [dw_w]
    return pl.pallas_call(
        body, name=name, grid=(n // tt,), in_specs=in_specs,
        out_specs=[pl.BlockSpec((tt, 2 * ch), lambda i: (i, 0)), pl.BlockSpec((32, ch), lambda i: (0, 0)),
                   pl.BlockSpec((8, ch), lambda i: (0, 0))],
        out_shape=[jax.ShapeDtypeStruct((n, 2 * ch), BF16), jax.ShapeDtypeStruct((32, ch), F32),
                   jax.ShapeDtypeStruct((8, ch), F32)],
        scratch_shapes=[pltpu.VMEM((tt + 2 * HALO, HALF), F32), pltpu.VMEM((tt + 2 * HALO, HALF), F32),
                        pltpu.VMEM((SUB + 8, SUB), F32), pltpu.VMEM((tt, HALF), F32),
                        pltpu.VMEM((8 * 32, ch), F32), pltpu.VMEM((8, ch), F32)],
        compiler_params=_params("arbitrary"),
    )(*ops)


def _sgu_common(p_ref, g_ref, be_ref, ws_ref, bs_ref, w):
    gw = w // SGU_GROUPS
    u_pre = p_ref[:, 0:w]
    v_pre = p_ref[:, w:2 * w]
    zc = p_ref[:, 2 * w:3 * w]
    u = _gelu(u_pre)
    v = _gelu(v_pre)
    mu = jnp.mean(v, axis=-1, keepdims=True)
    var = jnp.mean(jnp.square(v - mu), axis=-1, keepdims=True)
    rstd = lax.rsqrt(var + EPS)
    vhat = (v - mu) * rstd
    vn = (vhat * g_ref[...] + be_ref[...]).astype(BF16)
    mixed = jnp.concatenate(
        [jnp.dot(ws_ref[gi].astype(BF16), vn[:, gi * gw:(gi + 1) * gw], preferred_element_type=F32)
         + bs_ref[:, gi:gi + 1] for gi in range(SGU_GROUPS)], axis=-1)
    return u_pre, v_pre, zc, u, rstd, vhat, vn, mixed


def sgu_fwd(p, ln_g, ln_b, ws, bs_t, *, name):
    n, w3 = p.shape
    w = w3 // 3

    def body(p_ref, g_ref, be_ref, ws_ref, bs_ref, m_ref):
        _, _, zc, u, _, _, _, mixed = _sgu_common(p_ref, g_ref, be_ref, ws_ref, bs_ref, w)
        m_ref[...] = (u * mixed * _silu(zc)).astype(BF16)

    vec = pl.BlockSpec((1, w), lambda i: (0, 0))
    return pl.pallas_call(
        body, name=name, grid=(n // CHUNK,),
        in_specs=[pl.BlockSpec((CHUNK, w3), lambda i: (i, 0)), vec, vec,
                  pl.BlockSpec((SGU_GROUPS, CHUNK, CHUNK), lambda i: (0, 0, 0)),
                  pl.BlockSpec((CHUNK, SGU_GROUPS), lambda i: (0, 0))],
        out_specs=pl.BlockSpec((CHUNK, w), lambda i: (i, 0)),
        out_shape=jax.ShapeDtypeStruct((n, w), BF16),
        compiler_params=_params("parallel"),
    )(p, _row(ln_g), _row(ln_b), ws, bs_t)


def sgu_bwd(p, dm, ln_g, ln_b, ws, ws_t, bs_t, *, name):
    n, w3 = p.shape
    w = w3 // 3
    gw = w // SGU_GROUPS

    def body(p_ref, dm_ref, g_ref, be_ref, ws_ref, wst_ref, bs_ref, dp_ref, dws_ref, dbs_ref, acc_ref):
        i = pl.program_id(0)

        @pl.when(i == 0)
        def _():
            dws_ref[...] = jnp.zeros_like(dws_ref)
            dbs_ref[...] = jnp.zeros_like(dbs_ref)
            acc_ref[...] = jnp.zeros_like(acc_ref)

        u_pre, v_pre, zc, u, rstd, vhat, vn, mixed = _sgu_common(p_ref, g_ref, be_ref, ws_ref, bs_ref, w)
        dmv = dm_ref[...]
        um = u * mixed
        dp_ref[:, 2 * w:3 * w] = (dmv * um * _dsilu(zc)).astype(BF16)
        dum = dmv * _silu(zc)
        dp_ref[:, 0:w] = (dum * mixed * _dgelu(u_pre)).astype(BF16)
        dmixed = dum * u
        dmixed_b = dmixed.astype(BF16)
        dvn_parts = []
        for gi in range(SGU_GROUPS):
            cs = slice(gi * gw, (gi + 1) * gw)
            dws_ref[gi] += lax.dot_general(dmixed_b[:, cs], vn[:, cs], _DIMS["nt"], preferred_element_type=F32)
            dbs_ref[:, gi:gi + 1] += jnp.sum(dmixed[:, cs], axis=-1, keepdims=True)
            dvn_parts.append(jnp.dot(wst_ref[gi].astype(BF16), dmixed_b[:, cs], preferred_element_type=F32))
        dvn = jnp.concatenate(dvn_parts, axis=-1)
        acc_ref[0:1, :] += jnp.sum(dvn * vhat, axis=0, keepdims=True)
        acc_ref[1:2, :] += jnp.sum(dvn, axis=0, keepdims=True)
        dvhat = dvn * g_ref[...]
        dv = rstd * (dvhat - jnp.mean(dvhat, axis=-1, keepdims=True)
                     - vhat * jnp.mean(dvhat * vhat, axis=-1, keepdims=True))
        dp_ref[:, w:2 * w] = (dv * _dgelu(v_pre)).astype(BF16)

    vec = pl.BlockSpec((1, w), lambda i: (0, 0))
    wspec = pl.BlockSpec((SGU_GROUPS, CHUNK, CHUNK), lambda i: (0, 0, 0))
    bspec = pl.BlockSpec((CHUNK, SGU_GROUPS), lambda i: (0, 0))
    return pl.pallas_call(
        body, name=name, grid=(n // CHUNK,),
        in_specs=[pl.BlockSpec((CHUNK, w3), lambda i: (i, 0)), pl.BlockSpec((CHUNK, w), lambda i: (i, 0)),
                  vec, vec, wspec, wspec, bspec],
        out_specs=[pl.BlockSpec((CHUNK, w3), lambda i: (i, 0)), wspec, bspec,
                   pl.BlockSpec((8, w), lambda i: (0, 0))],
        out_shape=[jax.ShapeDtypeStruct((n, w3), BF16), jax.ShapeDtypeStruct((SGU_GROUPS, CHUNK, CHUNK), F32),
                   jax.ShapeDtypeStruct((CHUNK, SGU_GROUPS), F32), jax.ShapeDtypeStruct((8, w), F32)],
        compiler_params=_params("arbitrary"),
    )(p, dm, _row(ln_g), _row(ln_b), ws, ws_t, bs_t)


def _adam_math(w, g, m, v):
    m_new = ADAM_B1 * m + (1.0 - ADAM_B1) * g
    v_new = ADAM_B2 * v + (1.0 - ADAM_B2) * (g * g)
    m_hat = m_new / (1.0 - ADAM_B1 ** ADAM_STEP)
    v_hat = v_new / (1.0 - ADAM_B2 ** ADAM_STEP)
    delta = -ADAM_LR * (m_hat / (jnp.sqrt(v_hat) + ADAM_EPS) + ADAM_WD * w)
    return delta, m_new, v_new


def adamw(w, g, m, v, *, name, slots=False, rows=512):
    r, c = w.shape
    tr = min(r, rows)
    assert r % tr == 0, (name, r, tr)

    def body(w_ref, g_ref, m_ref, v_ref, go_ref, d_ref, mo_ref, vo_ref):
        if slots:
            g = g_ref[0].astype(F32)
            for k in range(1, N_DEV):
                g = g + g_ref[k].astype(F32)
        else:
            g = g_ref[...].astype(F32)
        delta, m_new, v_new = _adam_math(w_ref[...], g, m_ref[...], v_ref[...])
        go_ref[...] = g
        d_ref[...] = delta
        mo_ref[...] = m_new
        vo_ref[...] = v_new

    spec = pl.BlockSpec((tr, c), lambda i: (i, 0))
    gspec = pl.BlockSpec((N_DEV, tr, c), lambda i: (0, i, 0)) if slots else spec
    shape = jax.ShapeDtypeStruct((r, c), F32)
    return pl.pallas_call(
        body, name=name, grid=(r // tr,),
        in_specs=[spec, gspec, spec, spec], out_specs=[spec] * 4, out_shape=[shape] * 4,
        compiler_params=_params("parallel"),
    )(w, g, m, v)


def slot_sum(slots, *, name):
    _, r, c = slots.shape

    def body(s_ref, o_ref):
        acc = s_ref[0]
        for k in range(1, N_DEV):
            acc = acc + s_ref[k]
        o_ref[...] = acc

    return pl.pallas_call(
        body, name=name, out_shape=jax.ShapeDtypeStruct((r, c), F32),
        in_specs=[VMEM_SPEC], out_specs=VMEM_SPEC,
        compiler_params=pltpu.CompilerParams(vmem_limit_bytes=VMEM_LIMIT),
    )(slots)


def _after(token, val):
    return val + token[0, 0]


def _zero_of(v):
    bits = lax.bitcast_convert_type(v, jnp.uint16 if v.dtype == BF16 else jnp.uint32)
    return jnp.where((bits | 1) == 0, 1.0, 0.0).astype(F32)


def _pad_rows(a, rows):
    return jnp.pad(a, ((0, rows - a.shape[0]), (0, 0)))


def kernel(x, c, ctx, c_ctx, ada_w, ada_b, norm_g, ev_w_in, ev_q_norm, ev_k_norm, ev_dw_w, ev_dw_b, ev_ln_g, ev_ln_b, ev_w_out, od_w_in, od_ln_g, od_ln_b, od_ws, od_bs, od_w_out, final_g, loss_target, m_c_ctx, m_ada_w, m_ada_b, m_norm_g, m_ev_w_in, m_ev_q_norm, m_ev_k_norm, m_ev_dw_w, m_ev_dw_b, m_ev_ln_g, m_ev_ln_b, m_ev_w_out, m_od_w_in, m_od_ln_g, m_od_ln_b, m_od_ws, m_od_bs, m_od_w_out, m_final_g, v_c_ctx, v_ada_w, v_ada_b, v_norm_g, v_ev_w_in, v_ev_q_norm, v_ev_k_norm, v_ev_dw_w, v_ev_dw_b, v_ev_ln_g, v_ev_ln_b, v_ev_w_out, v_od_w_in, v_od_ln_g, v_od_ln_b, v_od_ws, v_od_bs, v_od_w_out, v_final_g):
    n, d = x.shape[1], x.shape[2]
    lc = ctx.shape[1]
    ev_in = ev_w_in.shape[2] * N_DEV
    od_in = od_w_in.shape[2] * N_DEV
    conv_ch = ev_dw_w.shape[2] * N_DEV
    ada_cols = ada_w.shape[2]
    me = 4 * lax.axis_index("x") + 2 * lax.axis_index("y") + lax.axis_index("c")
    xs, tgt, ctxs = x[0], loss_target[0], ctx[0]
    za_block = (2 * KV_W + ATTN_W) // HEAD_DIM
    glu_block = (2 * KV_W + 2 * ATTN_W) // HALF
    zb_block = glu_block + 2 * conv_ch // HALF

    small = jnp.concatenate([
        jax.nn.silu(c).reshape(1, d),
        od_ln_g.reshape(1, -1), od_ln_b.reshape(1, -1)], axis=1)
    small = _pad_rows(small, 8)
    dw_rows = _pad_rows(ev_dw_w[0], 32)
    small_g, dw_g = all_gather([small, dw_rows], name="gather_small")
    sc_all = small_g[:, 0, :d]
    shard = d // N_DEV
    od_ln_g_full = small_g[:, 0, d:d + shard].reshape(d)
    od_ln_b_full = small_g[:, 0, d + shard:d + 2 * shard].reshape(d)
    dw_w_full = jnp.moveaxis(dw_g, 0, 1).reshape(32, conv_ch)[:CONV_WIDTH]
    scc = jax.nn.silu(c_ctx)
    sc16 = _pad_rows(jnp.concatenate([sc_all, scc.reshape(1, d)], axis=0), 16)

    ada_bf = ada_w.astype(BF16)
    mod_loc = [matmul(sc16, ada_bf[l], mode="nn", tm=16, tn=ada_cols, tk=d, out_dtype=F32, name=f"ada_mod{l}")
               for l in range(2)]
    (mod_g,) = all_gather([jnp.stack(mod_loc)], name="gather_mod")
    mod_all = jnp.moveaxis(mod_g, 0, 2).reshape(2, 16, N_DEV * ada_cols) + ada_b[:, None, :]
    mod_me = lax.dynamic_index_in_dim(mod_all, me, axis=1, keepdims=False)
    shift = [mod_me[l, :d] for l in range(2)]
    scale = [mod_me[l, d:2 * d] for l in range(2)]
    gate = [mod_me[l, 2 * d:] for l in range(2)]
    shift_c, scale_c = mod_all[0, 8, :d], mod_all[0, 8, d:2 * d]

    h_w_in0, near_token = staged_gather_start(1, (ev_w_in[0] + _zero_of(mod_g[0, 0, 0, 0])).astype(BF16), None,
                                              name="gather_w_in0_chips_start")

    cexp, sexp = rope_tables(n)
    h0 = norm_mod_fwd(xs, norm_g[0], _after(near_token, shift[0]), scale[0], name="norm_mod_fwd0")
    hc = norm_mod_fwd(ctxs, norm_g[0], shift_c, scale_c, name="norm_mod_fwd_ctx")

    wi0_shard, wi0_land = staged_gather_wait(1, h_w_in0, h0, name="gather_w_in0_chips_wait")
    h_w_in0, far_token = staged_gather_start(2, wi0_shard, wi0_land, name="gather_w_in0_cores_start")
    wi0_shard, wi0_land = staged_gather_wait(2, h_w_in0, far_token, name="gather_w_in0_cores_wait")
    wi0_g = lax.dynamic_update_slice_in_dim(wi0_land, wi0_shard[None], me, axis=0)
    wi0 = jnp.moveaxis(wi0_g, 0, 1).reshape(d, ev_in)
    landed_zero = _zero_of(wi0_g[0, 0, 0])
    later = [(w[0] + landed_zero).astype(BF16) for w in (ev_w_out, od_w_in, od_w_out)]
    (h_wo0, h_wi1, h_wo1), w_token = exchange_start(later, scatter=False, name="gather_rest_start")

    def landed(handle, after, name):
        own, land = exchange_wait(handle, after, scatter=False, name=name)
        return lax.dynamic_update_slice_in_dim(land, own[None], me, axis=0)

    p0 = matmul(h0, wi0, mode="nn", tm=1024, tn=ev_in // 4, tk=d, out_dtype=F32, name="proj_in0", after=w_token)
    pc = matmul(hc, wi0, mode="nn", tm=lc, tn=2 * KV_W, tk=d, out_dtype=F32, n_out=2 * KV_W, name="proj_in_ctx")
    q_r, k_all, v_ext = qkv_prep_fwd(p0, ev_q_norm[0], ev_k_norm[0], cexp, sexp, latent=True, name="qkv_prep",
                                     rows_all=n + lc)
    k_all, v_ext = qkv_prep_fwd(pc, None, ev_k_norm[0], None, None, latent=False, name="kv_prep_ctx",
                                rows_all=n + lc, k_all=k_all, v_ext=v_ext)
    o_attn, mix_a, lse = flash_fwd(q_r, k_all, v_ext, p0, za_block=za_block, mix_width=ATTN_W + conv_ch,
                                   name="flash_fwd")
    ycv, mix0 = conv_fwd(p0, dw_w_full, ev_dw_b[0], ev_ln_g[0], ev_ln_b[0], mix_a, glu_block=glu_block,
                         name="conv_fwd")
    wo0 = landed(h_wo0, ycv, "gather_w_out0_wait").reshape(-1, d)
    x1, r0, h1 = matmul(mix0, wo0, mode="nn", tm=512, tn=d, tk=mix0.shape[1], out_dtype=F32, name="proj_out0",
                        res=xs, gate=_row(gate[0]), norm=(norm_g[1], shift[1], scale[1]))

    wi1 =jnp.moveaxis(landed(h_wi1, h1, "gather_w_in1_wait"), 0, 1).reshape(d, od_in)
    p1 = matmul(h1, wi1, mode="nn", tm=1024, tn=od_in // 4, tk=d, out_dtype=F32, name="proj_in1")
    ws_bf = od_ws[0]
    bs_t = od_bs[0].T
    m1 = sgu_fwd(p1, od_ln_g_full, od_ln_b_full, ws_bf, bs_t, name="sgu_fwd")
    wo1 = landed(h_wo1, m1, "gather_w_out1_wait").reshape(-1, d)
    dx2, dr1, acc_final, loss_tile = proj_out_loss(m1, wo1, x1, gate[1], tgt, final_g, name="proj_out1_loss")

    dm1 = matmul(dr1, wo1, mode="nt", tm=1024, tn=1024, tk=d, out_dtype=F32, name="d_mix1")
    dwo1 = matmul(m1, dr1, mode="tn", tm=1024, tn=1024, tk=2048, out_dtype=BF16, name="d_wout1")
    dp1, dws, dbs_t, acc_sgu = sgu_bwd(p1, dm1, od_ln_g_full, od_ln_b_full, ws_bf, jnp.swapaxes(ws_bf, 1, 2), bs_t,
                                       name="sgu_bwd")
    dwi1_s = matmul(h1, dp1, mode="tn", tm=2048, tn=od_in // N_DEV, tk=2048, out_dtype=BF16, name="d_win1",
                    split_out=True)
    dwo1_s = dwo1.reshape(N_DEV, -1, d)
    (h_gi1, h_go1), g1_token = exchange_start([dwi1_s, dwo1_s], scatter=True, name="grads1_start")
    dh1 = matmul(dp1, wi1, mode="nt", tm=1024, tn=512, tk=od_in, out_dtype=F32, name="d_h1")
    dx1, dr0, acc_norm1 = norm_mod_bwd(x1, dh1, dx2, norm_g[1], _after(g1_token, scale[1]), name="norm_mod_bwd1",
                                       branch=(r0, gate[0]))

    dmix0 = matmul(dr0, wo0, mode="nt", tm=1024, tn=1024, tk=d, out_dtype=F32, name="d_mix0")
    dwo0 = matmul(mix0, dr0, mode="tn", tm=1024, tn=1024, tk=2048, out_dtype=BF16, name="d_wout0")
    do_attn, do_s, dza, delta = attn_gate_bwd(dmix0, o_attn, p0, za_block=za_block, name="attn_gate_bwd")
    dycv, dzb, acc_ln = conv_bwd_rows(dmix0, ycv, p0, ev_ln_g[0], ev_ln_b[0], mix_block=ATTN_W // conv_ch,
                                      zb_block=zb_block, name="conv_bwd_rows")
    dglu, ddw_w, acc_dwb = conv_bwd_taps(dycv, p0, dw_w_full, glu_block=glu_block, name="conv_bwd_taps")
    dq_r, dk_all, dv_all = flash_bwd(q_r, do_attn, do_s, lse, delta, k_all, v_ext, name="flash_bwd")
    dkvq, acc_qk = qkv_prep_bwd(p0, dq_r, dk_all, dv_all, ev_q_norm[0], ev_k_norm[0], cexp, sexp,
                                latent=True, name="qkv_prep_bwd")
    dpc, acc_kc = qkv_prep_bwd(pc, None, dk_all, dv_all, None, ev_k_norm[0], None, None,
                               latent=False, name="kv_prep_ctx_bwd")
    dp0 = jnp.concatenate([dkvq, dza, dglu, dzb], axis=1)
    dwi0 = matmul(h0, dp0, mode="tn", tm=1024, tn=ev_in // 4, tk=2048, out_dtype=F32, name="d_win0")
    dwi0 = matmul(hc, dpc, mode="tn", tm=512, tn=2 * KV_W, tk=lc, out_dtype=F32, name="d_win0_ctx", add=dwi0)
    dwi0_s = jnp.moveaxis(dwi0.astype(BF16).reshape(d, N_DEV, ev_in // N_DEV), 1, 0)
    dwo0_s = dwo0.reshape(N_DEV, -1, d)
    (h_gi0, h_go0), g0_token = exchange_start([dwi0_s, dwo0_s], scatter=True, name="grads0_start")
    dh0 = matmul(dp0, wi0, mode="nt", tm=1024, tn=512, tk=ev_in, out_dtype=F32, name="d_h0", after=g0_token)
    dhc = matmul(dpc, wi0, mode="nt", tm=lc, tn=512, tk=2 * KV_W, out_dtype=F32, name="d_h_ctx")
    grad_x, acc_norm0 = norm_mod_bwd(xs, dh0, dx1, norm_g[0], _after(g0_token, scale[0]), name="norm_mod_bwd0")
    _, acc_normc = norm_mod_bwd(ctxs, dhc, None, norm_g[0], scale_c, name="norm_mod_bwd_ctx")

    zeros_d = jnp.zeros((d,), F32)
    dmod0 = jnp.stack([acc_norm0[0], acc_norm0[1], acc_norm1[GATE_ROW]])
    dmod1 = jnp.stack([acc_norm1[0], acc_norm1[1], acc_final[GATE_ROW]])
    dmodc = jnp.stack([acc_normc[0], acc_normc[1]])
    half_pad = jnp.zeros((d - 2 * conv_ch,), F32) if d > 2 * conv_ch else jnp.zeros((0,), F32)
    row_a = jnp.concatenate([acc_dwb[0], acc_ln[0], half_pad])
    row_b = jnp.concatenate([acc_ln[1], acc_qk[0], acc_qk[1] + acc_kc[1],
                             jnp.zeros((d - conv_ch - 2 * HEAD_DIM,), F32)])
    row_c = jnp.concatenate([dbs_t.T.reshape(-1), jnp.zeros((d - SGU_GROUPS * CHUNK,), F32)])
    row_loss = jnp.concatenate([loss_tile[0, :1], jnp.zeros((d - 1,), F32)])
    pack = jnp.concatenate([
        dmod0, dmod1, dmodc,
        (acc_norm0[2] + acc_normc[2])[None], acc_norm1[2][None],
        acc_final[0][None],
        acc_sgu[0][None], acc_sgu[1][None],
        row_a[None], row_b[None], row_c[None], row_loss[None],
        ddw_w.reshape(-1, d),
        dws.reshape(-1, d),
    ], axis=0)
    n_rows = pack.shape[0]
    pack = _pad_rows(pack, -(-n_rows // 8) * 8)
    (h_pack,), pack_token = exchange_start([pack], scatter=False, name="small_grads_start")

    def summands(handle, after, name):
        mine, land = exchange_wait(handle, after, scatter=True, name=name)
        own = lax.dynamic_index_in_dim(mine, me, axis=0, keepdims=True)
        return lax.dynamic_update_slice_in_dim(land, own, me, axis=0)

    out = {}

    def upd(key, w, g, m, v, slots=False, rows=512):
        shp = w.shape
        w2 = w.reshape(-1, shp[-1])
        g2 = g.reshape((N_DEV, -1, shp[-1])) if slots else g.reshape(-1, shp[-1])
        res = adamw(w2, g2, m.reshape(w2.shape), v.reshape(w2.shape), name="adamw_" + key, slots=slots, rows=rows)
        out[key] = tuple(r.reshape(shp) for r in res)

    gi1 = summands(h_gi1, pack_token, "grads_w_in1_wait")
    upd("od_w_in", od_w_in, gi1, m_od_w_in, v_od_w_in, slots=True, rows=256)
    go1 = summands(h_go1, out["od_w_in"][1], "grads_w_out1_wait")
    upd("od_w_out", od_w_out, go1, m_od_w_out, v_od_w_out, slots=True, rows=256)
    go0 = summands(h_go0, out["od_w_out"][1], "grads_w_out0_wait")
    upd("ev_w_out", ev_w_out, go0, m_ev_w_out, v_ev_w_out, slots=True, rows=256)
    gi0 = summands(h_gi0, out["ev_w_out"][1], "grads_w_in0_wait")
    upd("ev_w_in", ev_w_in, gi0, m_ev_w_in, v_ev_w_in, slots=True, rows=256)

    pack_own, pack_land = exchange_wait(h_pack, out["ev_w_in"][1], scatter=False, name="small_grads_wait")
    pack_g = lax.dynamic_update_slice_in_dim(pack_land, pack_own[None], me, axis=0)
    gsum = slot_sum(pack_g, name="sum_small_grads")
    loss = gsum[16, 0]
    dw_rows_n = 32 * conv_ch // d
    g_dw_w = gsum[17:17 + dw_rows_n].reshape(32, conv_ch)[:CONV_WIDTH]
    g_od_ws = gsum[17 + dw_rows_n:17 + dw_rows_n + SGU_GROUPS * CHUNK * CHUNK // d].reshape(od_ws.shape)

    dmodc_sum = jnp.concatenate([gsum[6], gsum[7], zeros_d])
    col0 = me * ada_cols
    dm_cols = []
    for l in range(2):
        rows = pack_g[:, 3 * l:3 * l + 3, :].reshape(N_DEV, 3 * d)
        extra = dmodc_sum[None] if l == 0 else jnp.zeros((1, 3 * d), F32)
        full = _pad_rows(jnp.concatenate([rows, extra], axis=0), 16)
        dm_cols.append(lax.dynamic_slice_in_dim(full, col0, ada_cols, axis=1))
    g_ada_w = jnp.stack([matmul(sc16, dm_cols[l], mode="tn", tm=512, tn=ada_cols, tk=16, out_dtype=F32,
                                name=f"d_ada_w{l}") for l in range(2)])
    dsc = matmul(dm_cols[0], ada_bf[0], mode="nt", tm=16, tn=512, tk=ada_cols, out_dtype=F32, name="d_scc")
    (_, dscc_sum) = all_gather([dsc[8:16]], name="gather_dscc", sum_out=True)
    sg = jax.nn.sigmoid(c_ctx)
    g_c_ctx = dscc_sum[0] * (sg * (1.0 + c_ctx * (1.0 - sg)))
    g_ada_b = jnp.stack([gsum[0:3].reshape(-1) + dmodc_sum, gsum[3:6].reshape(-1)])

    upd("ada_w", ada_w, g_ada_w, m_ada_w, v_ada_w)

    def my_shard(full, size):
        return lax.dynamic_slice_in_dim(full, me * size, size, axis=full.ndim - 1)

    small_items = [
        ("c_ctx", c_ctx, g_c_ctx, m_c_ctx, v_c_ctx),
        ("ada_b", ada_b, g_ada_b, m_ada_b, v_ada_b),
        ("norm_g", norm_g, gsum[8:10], m_norm_g, v_norm_g),
        ("ev_q_norm", ev_q_norm, gsum[14, conv_ch:conv_ch + HEAD_DIM], m_ev_q_norm, v_ev_q_norm),
        ("ev_k_norm", ev_k_norm, gsum[14, conv_ch + HEAD_DIM:conv_ch + 2 * HEAD_DIM], m_ev_k_norm, v_ev_k_norm),
        ("ev_dw_w", ev_dw_w, my_shard(g_dw_w, conv_ch // N_DEV), m_ev_dw_w, v_ev_dw_w),
        ("ev_dw_b", ev_dw_b, gsum[13, :conv_ch], m_ev_dw_b, v_ev_dw_b),
        ("ev_ln_g", ev_ln_g, gsum[13, conv_ch:2 * conv_ch], m_ev_ln_g, v_ev_ln_g),
        ("ev_ln_b", ev_ln_b, gsum[14, :conv_ch], m_ev_ln_b, v_ev_ln_b),
        ("od_ln_g", od_ln_g, my_shard(gsum[11], shard), m_od_ln_g, v_od_ln_g),
        ("od_ln_b", od_ln_b, my_shard(gsum[12], shard), m_od_ln_b, v_od_ln_b),
        ("od_ws", od_ws, g_od_ws, m_od_ws, v_od_ws),
        ("od_bs", od_bs, gsum[15, :SGU_GROUPS * CHUNK], m_od_bs, v_od_bs),
        ("final_g", final_g, gsum[10], m_final_g, v_final_g),
    ]
    sizes = [it[1].size for it in small_items]
    total = sum(sizes)
    lanes = 1024
    prow = -(-total // lanes)
    prow = -(-prow // 8) * 8

    def pack_small(idx):
        flat = jnp.concatenate([it[idx].reshape(-1).astype(F32) for it in small_items])
        return jnp.pad(flat, (0, prow * lanes - total)).reshape(prow, lanes)

    sres = adamw(pack_small(1), pack_small(2), pack_small(3), pack_small(4), name="adamw_small", rows=prow)
    off = 0
    for it, size in zip(small_items, sizes):
        out[it[0]] = tuple(r.reshape(-1)[off:off + size].reshape(it[1].shape) for r in sres)
        off += size

    names = ['c_ctx', 'ada_w', 'ada_b', 'norm_g', 'ev_w_in', 'ev_q_norm', 'ev_k_norm', 'ev_dw_w', 'ev_dw_b',
             'ev_ln_g', 'ev_ln_b', 'ev_w_out', 'od_w_in', 'od_ln_g', 'od_ln_b', 'od_ws', 'od_bs', 'od_w_out',
             'final_g']
    return (loss, grad_x[None], *[out[k][0] for k in names], *[out[k][1] for k in names],
            *[out[k][2] for k in names], *[out[k][3] for k in names])
```

```python
import functools
import math

import jax
import jax.numpy as jnp
from jax import lax
from jax.experimental import pallas as pl
from jax.experimental.pallas import tpu as pltpu

F32 = jnp.float32
BF16 = jnp.bfloat16
MESH = pl.DeviceIdType.MESH

EPS = 1e-6
HEAD_DIM = 128
N_Q_HEADS = 8
N_KV_HEADS = 2
Q_PER_KV = N_Q_HEADS // N_KV_HEADS
ATTN_W = N_Q_HEADS * HEAD_DIM
KV_W = N_KV_HEADS * HEAD_DIM
ATTN_SCALE = HEAD_DIM ** -0.5
LN2 = math.log(2.0)
Q_SCALE = ATTN_SCALE / LN2
ROPE_THETA = 10000.0
GRID_W = 64
CONV_WIDTH = 31
HALO = 16
CHUNK = 128
SGU_GROUPS = 8
N_DEV = 8

ADAM_LR = 0.001
ADAM_B1 = 0.9
ADAM_B2 = 0.999
ADAM_EPS = 1e-08
ADAM_WD = 0.01
ADAM_STEP = 10

VMEM_LIMIT = 56 * 1024 * 1024
ANY = pl.BlockSpec(memory_space=pl.ANY)
VMEM_SPEC = pl.BlockSpec(memory_space=pltpu.VMEM)


def _params(*sem):
    return pltpu.CompilerParams(dimension_semantics=sem, vmem_limit_bytes=VMEM_LIMIT)


def _sigmoid(x):
    return 1.0 / (1.0 + jnp.exp(-x))


def _silu(x):
    return x * _sigmoid(x)


def _dsilu(x):
    s = _sigmoid(x)
    return s * (1.0 + x * (1.0 - s))


_GELU_C = math.sqrt(2.0 / math.pi)


def _gelu(x):
    t = jnp.tanh(_GELU_C * (x + 0.044715 * (x * x * x)))
    return 0.5 * x * (1.0 + t)


def _dgelu(x):
    t = jnp.tanh(_GELU_C * (x + 0.044715 * (x * x * x)))
    return 0.5 * (1.0 + t) + 0.5 * x * (1.0 - t * t) * (_GELU_C * (1.0 + 3.0 * 0.044715 * (x * x)))


def _row(v):
    return v.reshape(1, -1).astype(F32)


def _flat_id(p):
    return 4 * p[0] + 2 * p[1] + p[2]


def _gather_body(n_arr, sum_out):
    def body(*refs):
        x_refs = refs[:n_arr]
        out_refs = refs[n_arr:2 * n_arr]
        pos = 2 * n_arr
        sum_refs = refs[pos:pos + n_arr] if sum_out else ()
        pos += n_arr if sum_out else 0
        send_sems, recv_sems, local_sems = refs[pos:pos + 3]
        x, y, c = lax.axis_index("x"), lax.axis_index("y"), lax.axis_index("c")
        me, sibling = (x, y, c), (x, y, 1 - c)
        chips = [(1 - x, y), (x, 1 - y), (1 - x, 1 - y)]

        def copy(a, k, block, to, src=None):
            rows = out_refs[a].at[_flat_id(block)]
            return pltpu.make_async_remote_copy(
                src_ref=rows if src is None else src, dst_ref=rows,
                send_sem=send_sems.at[a, k], recv_sem=recv_sems.at[a, k],
                device_id=to, device_id_type=MESH)

        sends = []
        mine = []
        for a in range(n_arr):
            cp = pltpu.make_async_copy(x_refs[a], out_refs[a].at[_flat_id(me)], local_sems.at[a])
            cp.start()
            mine.append(cp)
            first = [copy(a, 0, me, sibling, src=x_refs[a])]
            first += [copy(a, 1 + j, me, (*chip, c), src=x_refs[a]) for j, chip in enumerate(chips)]
            for cp in first:
                cp.start()
            sends += first
        for a in range(n_arr):
            for j, chip in enumerate(chips):
                copy(a, 1 + j, (*chip, c), me).wait_recv()
                fwd = copy(a, 4 + j, (*chip, c), sibling)
                fwd.start()
                sends.append(fwd)
        for a in range(n_arr):
            copy(a, 0, sibling, me).wait_recv()
            for j, chip in enumerate(chips):
                copy(a, 4 + j, (*chip, 1 - c), me).wait_recv()
        for cp in sends:
            cp.wait_send()
        for cp in mine:
            cp.wait()
        if sum_out:
            for a in range(n_arr):
                acc = out_refs[a][0]
                for k in range(1, N_DEV):
                    acc = acc + out_refs[a][k]
                sum_refs[a][...] = acc

    return body


def all_gather(arrs, *, name, sum_out=False):
    n = len(arrs)
    spec = VMEM_SPEC
    out_shape = [jax.ShapeDtypeStruct((N_DEV,) + a.shape, a.dtype) for a in arrs]
    out_specs = [spec] * n
    if sum_out:
        out_shape += [jax.ShapeDtypeStruct(a.shape, a.dtype) for a in arrs]
        out_specs += [VMEM_SPEC] * n
    res = pl.pallas_call(
        _gather_body(n, sum_out), name=name,
        out_shape=out_shape, in_specs=[spec] * n, out_specs=out_specs,
        scratch_shapes=[pltpu.SemaphoreType.DMA((n, 7)), pltpu.SemaphoreType.DMA((n, 7)),
                        pltpu.SemaphoreType.DMA((n,))],
        compiler_params=pltpu.CompilerParams(vmem_limit_bytes=VMEM_LIMIT),
    )(*arrs)
    return res


_RELATIONS = [(rx, ry, rc) for rx in (0, 1) for ry in (0, 1) for rc in (0, 1)][1:]


HBM_SPEC = pl.BlockSpec(memory_space=pltpu.HBM)
SEM_SPEC = pl.BlockSpec(memory_space=pltpu.SEMAPHORE)
_DATAFLOW = pltpu.SideEffectType.DATAFLOW_SIDE_EFFECTING
N_PEERS = N_DEV - 1


def _peer_copy(src_ref, land_ref, send_sem, recv_sem, k, rel, scatter, sending):
    x, y, c = lax.axis_index("x"), lax.axis_index("y"), lax.axis_index("c")
    rx, ry, rc = rel
    peer = (1 - x if rx else x, 1 - y if ry else y, 1 - c if rc else c)
    src = src_ref.at[_flat_id(peer)] if scatter else src_ref
    dst = land_ref.at[_flat_id((x, y, c)) if sending else _flat_id(peer)]
    return pltpu.make_async_remote_copy(src_ref=src, dst_ref=dst, send_sem=send_sem.at[k], recv_sem=recv_sem.at[k],
                                        device_id=peer, device_id_type=MESH)


def exchange_start(arrs, *, scatter, name, after=None):
    n = len(arrs)
    lands = [lax.empty((N_DEV,) + (a.shape[1:] if scatter else a.shape), a.dtype) for a in arrs]
    n_in = 2 * n + (after is not None)

    def body(*refs):
        srcs, lnds, sems = refs[:n], refs[n:2 * n], refs[n_in:n_in + 2 * n]
        token = refs[n_in + 4 * n]
        for a in range(n):
            for k, rel in enumerate(_RELATIONS):
                _peer_copy(srcs[a], lnds[a], sems[2 * a], sems[2 * a + 1], k, rel, scatter, True).start()
        token[...] = jnp.zeros_like(token)

    outs = pl.pallas_call(
        body, name=name,
        out_shape=[pltpu.SemaphoreType.DMA((N_PEERS,))] * (2 * n)
        + [pltpu.HBM(a.shape, a.dtype) for a in arrs] + [pltpu.HBM(l.shape, l.dtype) for l in lands]
        + [jax.ShapeDtypeStruct((8, 128), F32)],
        in_specs=[HBM_SPEC] * (2 * n) + ([ANY] if after is not None else []),
        out_specs=[SEM_SPEC] * (2 * n) + [HBM_SPEC] * (2 * n) + [VMEM_SPEC],
        input_output_aliases={i: 2 * n + i for i in range(2 * n)},
        compiler_params=pltpu.CompilerParams(has_side_effects=_DATAFLOW),
    )(*[pltpu.with_memory_space_constraint(a, pltpu.HBM) for a in arrs],
      *[pltpu.with_memory_space_constraint(l, pltpu.HBM) for l in lands],
      *([after] if after is not None else []))
    handles = [(outs[2 * a], outs[2 * a + 1], outs[2 * n + a], outs[3 * n + a]) for a in range(n)]
    return handles, outs[4 * n]


def exchange_wait(handle, after, *, scatter, name):
    send_sem, recv_sem, src, land = handle

    def body(src_ref, land_ref, send_ref, recv_ref, after_ref, src_out, land_out):
        for k, rel in enumerate(_RELATIONS):
            cp = _peer_copy(src_ref, land_ref, send_ref, recv_ref, k, rel, scatter, False)
            cp.wait_send()
            cp.wait_recv()

    outs = pl.pallas_call(
        body, name=name,
        out_shape=[pltpu.HBM(src.shape, src.dtype), pltpu.HBM(land.shape, land.dtype)],
        in_specs=[HBM_SPEC, HBM_SPEC, SEM_SPEC, SEM_SPEC, ANY],
        out_specs=[HBM_SPEC, HBM_SPEC], input_output_aliases={0: 0, 1: 1},
        compiler_params=pltpu.CompilerParams(has_side_effects=_DATAFLOW),
    )(src, land, send_sem, recv_sem, after)
    return outs[0], outs[1]


_STAGE1 = [(0, 0, 1), (1, 0, 0), (0, 1, 0), (1, 1, 0)]
_OTHER_CHIPS = [(1, 0), (0, 1), (1, 1)]


def _stage_copy(stage, k, shard_ref, land_ref, send_sem, recv_sem, sending):
    x, y, c = lax.axis_index("x"), lax.axis_index("y"), lax.axis_index("c")
    if stage == 1:
        rx, ry, rc = _STAGE1[k]
        peer = (1 - x if rx else x, 1 - y if ry else y, 1 - c if rc else c)
        src = shard_ref
        dst = land_ref.at[_flat_id((x, y, c)) if sending else _flat_id(peer)]
    else:
        cx, cy = _OTHER_CHIPS[k]
        peer = (x, y, 1 - c)
        chip = (1 - x if cx else x, 1 - y if cy else y)
        slot = _flat_id((*chip, c)) if sending else _flat_id((*chip, 1 - c))
        src = land_ref.at[_flat_id((*chip, c))]
        dst = land_ref.at[slot]
    return pltpu.make_async_remote_copy(src_ref=src, dst_ref=dst, send_sem=send_sem.at[k], recv_sem=recv_sem.at[k],
                                        device_id=peer, device_id_type=MESH)


def staged_gather_start(stage, shard, land, *, name):
    n_copies = len(_STAGE1) if stage == 1 else len(_OTHER_CHIPS)
    if land is None:
        land = lax.empty((N_DEV,) + shard.shape, shard.dtype)

    def body(shard_ref, land_ref, send_sem, recv_sem, shard_thru, land_thru, token):
        for k in range(n_copies):
            _stage_copy(stage, k, shard_ref, land_ref, send_sem, recv_sem, True).start()
        token[...] = jnp.zeros_like(token)

    outs = pl.pallas_call(
        body, name=name,
        out_shape=[pltpu.SemaphoreType.DMA((n_copies,)), pltpu.SemaphoreType.DMA((n_copies,)),
                   pltpu.HBM(shard.shape, shard.dtype), pltpu.HBM(land.shape, land.dtype),
                   jax.ShapeDtypeStruct((8, 128), F32)],
        in_specs=[HBM_SPEC, HBM_SPEC],
        out_specs=[SEM_SPEC, SEM_SPEC, HBM_SPEC, HBM_SPEC, VMEM_SPEC],
        input_output_aliases={0: 2, 1: 3},
        compiler_params=pltpu.CompilerParams(has_side_effects=_DATAFLOW),
    )(pltpu.with_memory_space_constraint(shard, pltpu.HBM), pltpu.with_memory_space_constraint(land, pltpu.HBM))
    return outs[:4], outs[4]


def staged_gather_wait(stage, handle, after, *, name):
    send_sem, recv_sem, shard, land = handle
    n_copies = len(_STAGE1) if stage == 1 else len(_OTHER_CHIPS)

    def body(shard_ref, land_ref, send_ref, recv_ref, after_ref, shard_out, land_out):
        for k in range(n_copies):
            cp = _stage_copy(stage, k, shard_ref, land_ref, send_ref, recv_ref, False)
            cp.wait_send()
            cp.wait_recv()

    outs = pl.pallas_call(
        body, name=name,
        out_shape=[pltpu.HBM(shard.shape, shard.dtype), pltpu.HBM(land.shape, land.dtype)],
        in_specs=[HBM_SPEC, HBM_SPEC, SEM_SPEC, SEM_SPEC, ANY],
        out_specs=[HBM_SPEC, HBM_SPEC], input_output_aliases={0: 0, 1: 1},
        compiler_params=pltpu.CompilerParams(has_side_effects=_DATAFLOW),
    )(shard, land, send_sem, recv_sem, after)
    return outs[0], outs[1]


_DIMS = {"nn": (((1,), (0,)), ((), ())), "nt": (((1,), (1,)), ((), ())), "tn": (((0,), (0,)), ((), ()))}


def matmul(a, b, *, mode, tm, tn, tk, out_dtype, name, n_out=None, res=None, gate=None, add=None, after=None,
           split_out=False, norm=None):
    if mode == "tn":
        kdim, m = a.shape
    else:
        m, kdim = a.shape
    nfull = b.shape[0] if mode == "nt" else b.shape[1]
    n = nfull if n_out is None else n_out
    tm, tn, tk = min(tm, m), min(tn, n), min(tk, kdim)
    assert m % tm == 0 and n % tn == 0 and kdim % tk == 0, (name, m, n, kdim, tm, tn, tk)
    nk = kdim // tk
    dims = _DIMS[mode]
    a_spec = pl.BlockSpec((tk, tm), lambda j, i, k: (k, i)) if mode == "tn" else pl.BlockSpec((tm, tk), lambda j, i, k: (i, k))
    b_spec = pl.BlockSpec((tn, tk), lambda j, i, k: (j, k)) if mode == "nt" else pl.BlockSpec((tk, tn), lambda j, i, k: (k, j))
    o_spec = pl.BlockSpec((tm, tn), lambda j, i, k: (i, j))
    in_specs = [a_spec, b_spec]
    operands = [a, b]
    aliases = {}
    vec = pl.BlockSpec((1, tn), lambda j, i, k: (0, j))
    if res is not None:
        in_specs += [o_spec, vec]
        operands += [res, gate]
    if norm is not None:
        assert res is not None and tn == n, (name, tn, n)
        in_specs += [vec, vec, vec]
        operands += [_row(v) for v in norm]
    if add is not None:
        in_specs += [o_spec]
        aliases = {len(operands): 0}
        operands += [add]
    if after is not None:
        in_specs += [ANY]
        operands += [after]
    out_cols = n if add is None else add.shape[1]
    out_shape = [jax.ShapeDtypeStruct((m, out_cols), out_dtype)]
    out_specs = [o_spec]
    if split_out:
        out_shape = [jax.ShapeDtypeStruct((n // tn, m, tn), out_dtype)]
        out_specs = [pl.BlockSpec((None, tm, tn), lambda j, i, k: (j, i, 0))]
    if res is not None:
        out_shape.append(jax.ShapeDtypeStruct((m, n), BF16))
        out_specs.append(o_spec)
    if norm is not None:
        out_shape.append(jax.ShapeDtypeStruct((m, n), BF16))
        out_specs.append(o_spec)

    def body(*refs):
        a_ref, b_ref = refs[:2]
        pos = 2
        if res is not None:
            res_ref, gate_ref = refs[pos:pos + 2]
            pos += 2
        if norm is not None:
            ng_ref, nshift_ref, nscale_ref = refs[pos:pos + 3]
            pos += 3
        if add is not None:
            add_ref = refs[pos]
            pos += 1
        if after is not None:
            pos += 1
        o_ref = refs[pos]
        pos += 1
        if res is not None:
            r_ref = refs[pos]
            pos += 1
        if norm is not None:
            h_ref = refs[pos]
            pos += 1
        acc_ref = refs[pos] if nk > 1 else None
        prod = lax.dot_general(a_ref[...].astype(BF16), b_ref[...].astype(BF16), dims,
                               preferred_element_type=F32)

        def finish(acc):
            if res is not None:
                xnew = res_ref[...] + gate_ref[...] * acc
                o_ref[...] = xnew.astype(out_dtype)
                r_ref[...] = acc.astype(BF16)
                if norm is not None:
                    rstd = lax.rsqrt(jnp.mean(xnew * xnew, axis=-1, keepdims=True) + EPS)
                    y = xnew * rstd * ng_ref[...]
                    h_ref[...] = (y * (1.0 + nscale_ref[...]) + nshift_ref[...]).astype(BF16)
            elif add is not None:
                o_ref[...] = (add_ref[...] + acc).astype(out_dtype)
            else:
                o_ref[...] = acc.astype(out_dtype)

        if nk == 1:
            finish(prod)
        else:
            k = pl.program_id(2)

            @pl.when(k == 0)
            def _():
                acc_ref[...] = prod

            @pl.when(k > 0)
            def _():
                acc_ref[...] += prod

            @pl.when(k == nk - 1)
            def _():
                finish(acc_ref[...])

    outs = pl.pallas_call(
        body, name=name, grid=(n // tn, m // tm, nk),
        in_specs=in_specs, out_specs=out_specs, out_shape=out_shape,
        scratch_shapes=[pltpu.VMEM((tm, tn), F32)] if nk > 1 else [],
        input_output_aliases=aliases,
        compiler_params=_params("parallel", "parallel", "arbitrary"),
    )(*operands)
    return outs if res is not None else outs[0]


def _rows_tile(n, want):
    return min(n, want)


def norm_mod_fwd(x, g, shift, scale, *, name):
    n, d = x.shape
    tt = _rows_tile(n, 256)

    def body(x_ref, g_ref, sh_ref, sc_ref, h_ref):
        xv = x_ref[...]
        rstd = lax.rsqrt(jnp.mean(xv * xv, axis=-1, keepdims=True) + EPS)
        y = xv * rstd * g_ref[...]
        h_ref[...] = (y * (1.0 + sc_ref[...]) + sh_ref[...]).astype(BF16)

    vec = pl.BlockSpec((1, d), lambda i: (0, 0))
    return pl.pallas_call(
        body, name=name, grid=(n // tt,),
        in_specs=[pl.BlockSpec((tt, d), lambda i: (i, 0)), vec, vec, vec],
        out_specs=pl.BlockSpec((tt, d), lambda i: (i, 0)),
        out_shape=jax.ShapeDtypeStruct((n, d), BF16),
        compiler_params=_params("parallel"),
    )(x, _row(g), _row(shift), _row(scale))


GATE_ROW = 3


def norm_mod_bwd(x, dh, dres, g, scale, *, name, branch=None):
    n, d = x.shape
    tt = _rows_tile(n, 256)
    has_res = dres is not None
    has_branch = branch is not None
    last = n // tt - 1

    def body(*refs):
        x_ref, dh_ref = refs[:2]
        pos = 2
        if has_res:
            dres_ref = refs[pos]
            pos += 1
        if has_branch:
            r_ref, gate_ref = refs[pos:pos + 2]
            pos += 2
        g_ref, sc_ref, dx_ref = refs[pos:pos + 3]
        pos += 3
        if has_branch:
            dr_ref = refs[pos]
            pos += 1
        acc_ref, s_ref = refs[pos:pos + 2]
        i = pl.program_id(0)
        xv = x_ref[...]
        dhv = dh_ref[...]
        rstd = lax.rsqrt(jnp.mean(xv * xv, axis=-1, keepdims=True) + EPS)
        xhat = xv * rstd
        dxhat = dhv * (g_ref[...] * (1.0 + sc_ref[...]))
        dx = rstd * (dxhat - xhat * jnp.mean(dxhat * xhat, axis=-1, keepdims=True))
        if has_res:
            dx = dx + dres_ref[...]
        dx_ref[...] = dx

        @pl.when(i == 0)
        def _():
            s_ref[...] = jnp.zeros_like(s_ref)

        s_ref[0:1, :] += jnp.sum(dhv, axis=0, keepdims=True)
        s_ref[1:2, :] += jnp.sum(dhv * xhat, axis=0, keepdims=True)
        if has_branch:
            dr_ref[...] = (dx * gate_ref[...]).astype(BF16)
            s_ref[2:3, :] += jnp.sum(dx * r_ref[...].astype(F32), axis=0, keepdims=True)

        @pl.when(i == last)
        def _():
            s1 = s_ref[0:1, :]
            s2 = s_ref[1:2, :]
            acc_ref[...] = jnp.zeros_like(acc_ref)
            acc_ref[0:1, :] = s1
            acc_ref[1:2, :] = s2 * g_ref[...]
            acc_ref[2:3, :] = s2 * (1.0 + sc_ref[...])
            acc_ref[GATE_ROW:GATE_ROW + 1, :] = s_ref[2:3, :]

    vec = pl.BlockSpec((1, d), lambda i: (0, 0))
    big = pl.BlockSpec((tt, d), lambda i: (i, 0))
    ops = [x, dh] + ([dres] if has_res else []) + ([branch[0], _row(branch[1])] if has_branch else [])
    ops += [_row(g), _row(scale)]
    return pl.pallas_call(
        body, name=name, grid=(n // tt,),
        in_specs=[big, big] + ([big] if has_res else []) + ([big, vec] if has_branch else []) + [vec, vec],
        out_specs=[big] + ([big] if has_branch else []) + [pl.BlockSpec((8, d), lambda i: (0, 0))],
        out_shape=[jax.ShapeDtypeStruct((n, d), F32)] + ([jax.ShapeDtypeStruct((n, d), BF16)] if has_branch else [])
        + [jax.ShapeDtypeStruct((8, d), F32)],
        scratch_shapes=[pltpu.VMEM((8, d), F32)],
        compiler_params=_params("arbitrary"),
    )(*ops)


def proj_out_loss(a, b, x_prev, gate, target, g, *, name):
    n, kdim = a.shape
    d = b.shape[1]
    tt = _rows_tile(n, 512)
    part_rows = min(tt, 256)

    def body(a_ref, b_ref, x_ref, t_ref, g_ref, gate_ref, dx_ref, dr_ref, acc_ref, loss_ref):
        i = pl.program_id(0)

        @pl.when(i == 0)
        def _():
            acc_ref[...] = jnp.zeros_like(acc_ref)
            loss_ref[...] = jnp.zeros_like(loss_ref)

        for h in range(tt // part_rows):
            rows = pl.ds(h * part_rows, part_rows)
            r = jnp.dot(a_ref[rows, :], b_ref[...], preferred_element_type=F32)
            xv = x_ref[rows, :] + gate_ref[...] * r
            rstd = lax.rsqrt(jnp.mean(xv * xv, axis=-1, keepdims=True) + EPS)
            xhat = xv * rstd
            e = xhat * g_ref[...] - t_ref[rows, :]
            dy = e * (1.0 / d)
            dxhat = dy * g_ref[...]
            dx = rstd * (dxhat - xhat * jnp.mean(dxhat * xhat, axis=-1, keepdims=True))
            dx_ref[rows, :] = dx
            dr_ref[rows, :] = (dx * gate_ref[...]).astype(BF16)
            acc_ref[0:1, :] += jnp.sum(dy * xhat, axis=0, keepdims=True)
            acc_ref[GATE_ROW:GATE_ROW + 1, :] += jnp.sum(dx * r, axis=0, keepdims=True)
            part = 0.5 * jnp.sum(jnp.mean(e * e, axis=-1, keepdims=True), axis=0, keepdims=True)
            loss_ref[...] += jnp.broadcast_to(part, loss_ref.shape)

    big = pl.BlockSpec((tt, d), lambda i: (i, 0))
    vec = pl.BlockSpec((1, d), lambda i: (0, 0))
    return pl.pallas_call(
        body, name=name, grid=(n // tt,),
        in_specs=[pl.BlockSpec((tt, kdim), lambda i: (i, 0)), pl.BlockSpec((kdim, d), lambda i: (0, 0)),
                  big, big, vec, vec],
        out_specs=[big, big, pl.BlockSpec((8, d), lambda i: (0, 0)), pl.BlockSpec((8, 128), lambda i: (0, 0))],
        out_shape=[jax.ShapeDtypeStruct((n, d), F32), jax.ShapeDtypeStruct((n, d), BF16),
                   jax.ShapeDtypeStruct((8, d), F32), jax.ShapeDtypeStruct((8, 128), F32)],
        compiler_params=_params("arbitrary"),
    )(a, b, x_prev, target, _row(g), _row(gate))


def _as_row(col):
    t = col.shape[0]
    return jnp.transpose(jnp.broadcast_to(col, (t, HEAD_DIM)))[0:1, :]


def _swap_pairs(x):
    lane = lax.broadcasted_iota(jnp.int32, x.shape, 1)
    return jnp.where(lane % 2 == 0, pltpu.roll(x, HEAD_DIM - 1, 1), pltpu.roll(x, 1, 1))


def rope_tables(n):
    rows = n // GRID_W
    row = jnp.repeat(jnp.arange(rows, dtype=F32), GRID_W)
    col = jnp.tile(jnp.arange(GRID_W, dtype=F32), rows)
    n_freq = HEAD_DIM // 4
    inv = jnp.power(ROPE_THETA, jnp.arange(n_freq, dtype=F32) * (-2.0 / (HEAD_DIM // 2)))
    ang = jnp.concatenate([row[:, None] * inv, col[:, None] * inv], axis=-1)
    cos, sin = jnp.cos(ang), jnp.sin(ang)
    cexp = jnp.repeat(cos, 2, axis=-1)
    sexp = jnp.stack([-sin, sin], axis=-1).reshape(n, HEAD_DIM)
    return cexp, sexp


V_EXT_W = 2 * HEAD_DIM


def qkv_prep_fwd(p, wq, wk, cexp, sexp, *, latent, name, rows_all=None, k_all=None, v_ext=None):
    n = p.shape[0]
    tt = _rows_tile(n, 256)
    width = 2 * KV_W + (ATTN_W if latent else 0)

    def body(*refs):
        if latent:
            p_ref, wq_ref, wk_ref, c_ref, s_ref, q_ref, k_ref, v_ref = refs
        else:
            p_ref, wk_ref, _, _, k_ref, v_ref = refs

        def head(xv, w):
            rstd = lax.rsqrt(jnp.mean(xv * xv, axis=-1, keepdims=True) + EPS)
            yv = xv * rstd * w
            if latent:
                yv = yv * c_ref[...] + _swap_pairs(yv) * s_ref[...]
            return yv

        for h in range(N_KV_HEADS):
            sl = slice(h * HEAD_DIM, (h + 1) * HEAD_DIM)
            k_ref[:, sl] = head(p_ref[:, sl], wk_ref[...]).astype(BF16)
            v_ref[:, h * V_EXT_W:h * V_EXT_W + HEAD_DIM] = p_ref[:, KV_W + h * HEAD_DIM:KV_W + (h + 1) * HEAD_DIM].astype(BF16)
            lane = lax.broadcasted_iota(jnp.int32, (tt, HEAD_DIM), 1)
            v_ref[:, h * V_EXT_W + HEAD_DIM:(h + 1) * V_EXT_W] = jnp.where(lane == 0, 1.0, 0.0).astype(BF16)
        if latent:
            for h in range(N_Q_HEADS):
                sl = slice(2 * KV_W + h * HEAD_DIM, 2 * KV_W + (h + 1) * HEAD_DIM)
                q_ref[:, h * HEAD_DIM:(h + 1) * HEAD_DIM] = (head(p_ref[:, sl], wq_ref[...]) * Q_SCALE).astype(BF16)

    vec = pl.BlockSpec((1, HEAD_DIM), lambda i: (0, 0))
    tab = pl.BlockSpec((tt, HEAD_DIM), lambda i: (i, 0))
    vw = N_KV_HEADS * V_EXT_W
    k_shape = jax.ShapeDtypeStruct((rows_all, KV_W), BF16)
    v_shape = jax.ShapeDtypeStruct((rows_all, vw), BF16)
    aliases = {}
    if latent:
        in_specs = [pl.BlockSpec((tt, width), lambda i: (i, 0)), vec, vec, tab, tab]
        ops = [p, _row(wq), _row(wk), cexp, sexp]
        out_specs = [pl.BlockSpec((tt, ATTN_W), lambda i: (i, 0)), pl.BlockSpec((tt, KV_W), lambda i: (i, 0)),
                     pl.BlockSpec((tt, vw), lambda i: (i, 0))]
        out_shape = [jax.ShapeDtypeStruct((n, ATTN_W), BF16), k_shape, v_shape]
    else:
        assert n == tt and (rows_all - n) % n == 0, (n, tt, rows_all)
        first = (rows_all - n) // n
        in_specs = [pl.BlockSpec((tt, width), lambda i: (i, 0)), vec, ANY, ANY]
        ops = [p, _row(wk), k_all, v_ext]
        out_specs = [pl.BlockSpec((tt, KV_W), lambda i: (first, 0)), pl.BlockSpec((tt, vw), lambda i: (first, 0))]
        out_shape = [k_shape, v_shape]
        aliases = {2: 0, 3: 1}
    return pl.pallas_call(
        body, name=name, grid=(n // tt,), in_specs=in_specs, out_specs=out_specs, out_shape=out_shape,
        input_output_aliases=aliases, compiler_params=_params("parallel"),
    )(*ops)


def qkv_prep_bwd(p, dq, dk, dv, wq, wk, cexp, sexp, *, latent, name):
    n = p.shape[0]
    tt = _rows_tile(n, 256)
    width = 2 * KV_W + (ATTN_W if latent else 0)

    def body(*refs):
        if latent:
            p_ref, dq_ref, dk_ref, dv_ref, wq_ref, wk_ref, c_ref, s_ref, dp_ref, acc_ref = refs
        else:
            p_ref, dk_ref, dv_ref, wk_ref, dp_ref, acc_ref = refs
        i = pl.program_id(0)

        @pl.when(i == 0)
        def _():
            acc_ref[...] = jnp.zeros_like(acc_ref)

        def head(xv, dy, w, row):
            if latent:
                dy = dy * c_ref[...] + _swap_pairs(dy * s_ref[...])
            rstd = lax.rsqrt(jnp.mean(xv * xv, axis=-1, keepdims=True) + EPS)
            xhat = xv * rstd
            acc_ref[row:row + 1, :] += jnp.sum(dy * xhat, axis=0, keepdims=True)
            dxhat = dy * w
            return rstd * (dxhat - xhat * jnp.mean(dxhat * xhat, axis=-1, keepdims=True))

        for h in range(N_KV_HEADS):
            sl = slice(h * HEAD_DIM, (h + 1) * HEAD_DIM)
            dp_ref[:, sl] = head(p_ref[:, sl], dk_ref[:, sl], wk_ref[...], 1).astype(BF16)
        dp_ref[:, KV_W:2 * KV_W] = dv_ref[...].astype(BF16)
        if latent:
            for h in range(N_Q_HEADS):
                sl = slice(2 * KV_W + h * HEAD_DIM, 2 * KV_W + (h + 1) * HEAD_DIM)
                dyq = dq_ref[:, h * HEAD_DIM:(h + 1) * HEAD_DIM] * Q_SCALE
                dp_ref[:, sl] = head(p_ref[:, sl], dyq, wq_ref[...], 0).astype(BF16)

    vec = pl.BlockSpec((1, HEAD_DIM), lambda i: (0, 0))
    tab = pl.BlockSpec((tt, HEAD_DIM), lambda i: (i, 0))
    first = 0 if latent else (dk.shape[0] - n) // tt
    kv_spec = pl.BlockSpec((tt, KV_W), lambda i: (first + i, 0))
    p_spec = pl.BlockSpec((tt, width), lambda i: (i, 0))
    if latent:
        in_specs = [p_spec, pl.BlockSpec((tt, ATTN_W), lambda i: (i, 0)), kv_spec, kv_spec, vec, vec, tab, tab]
        ops = [p, dq, dk, dv, _row(wq), _row(wk), cexp, sexp]
    else:
        in_specs = [p_spec, kv_spec, kv_spec, vec]
        ops = [p, dk, dv, _row(wk)]
    return pl.pallas_call(
        body, name=name, grid=(n // tt,), in_specs=in_specs,
        out_specs=[p_spec, pl.BlockSpec((8, HEAD_DIM), lambda i: (0, 0))],
        out_shape=[jax.ShapeDtypeStruct((n, width), BF16), jax.ShapeDtypeStruct((8, HEAD_DIM), F32)],
        compiler_params=_params("arbitrary"),
    )(*ops)


def _kv_chunks(n, s_all, want=1024):
    step = want if n % want == 0 else 256
    chunks = [(s, step) for s in range(0, n, step)]
    if s_all > n:
        chunks.append((n, s_all - n))
    return chunks


def flash_fwd(q, k_all, v_ext, p, *, za_block, mix_width, name):
    n = q.shape[0]
    s_all = k_all.shape[0]
    tq = _rows_tile(n, 512)
    chunks = _kv_chunks(n, s_all)
    wide = 2 * HEAD_DIM

    def body(q_ref, k_ref, v_ref, za_ref, o_ref, mix_ref, lse_ref, m_ref, acc_ref):
        qv = q_ref[...]
        m_ref[...] = jnp.full_like(m_ref, -jnp.inf)
        acc_ref[...] = jnp.zeros_like(acc_ref)
        for start, size in chunks:
            kc = k_ref[pl.ds(start, size), :]
            vc = v_ref[pl.ds(start, size), :]
            s = lax.dot_general(qv, kc, _DIMS["nt"], preferred_element_type=F32)
            m_old = m_ref[...]
            m_new = jnp.maximum(m_old, jnp.max(s, axis=-1, keepdims=True))
            pr = jnp.exp2(s - m_new)
            alpha = jnp.exp2(m_old - m_new)
            acc_ref[...] = alpha * acc_ref[...] + jnp.dot(pr.astype(BF16), vc, preferred_element_type=F32)
            m_ref[...] = m_new
        acc = acc_ref[...]
        denom = acc[:, HEAD_DIM:HEAD_DIM + 1]
        o = acc[:, :HEAD_DIM] / denom
        o_ref[...] = o.astype(BF16)
        mix_ref[...] = (o * _silu(za_ref[...])).astype(BF16)
        lse_ref[0] = _as_row(m_ref[...] + jnp.log2(denom))

    qspec = pl.BlockSpec((tq, HEAD_DIM), lambda h, i: (i, h))
    return pl.pallas_call(
        body, name=name, grid=(N_Q_HEADS, n // tq),
        in_specs=[qspec, pl.BlockSpec((s_all, HEAD_DIM), lambda h, i: (0, h // Q_PER_KV)),
                  pl.BlockSpec((s_all, wide), lambda h, i: (0, h // Q_PER_KV)),
                  pl.BlockSpec((tq, HEAD_DIM), lambda h, i: (i, za_block + h))],
        out_specs=[qspec, qspec, pl.BlockSpec((1, 1, tq), lambda h, i: (h, 0, i))],
        out_shape=[jax.ShapeDtypeStruct((n, ATTN_W), BF16), jax.ShapeDtypeStruct((n, mix_width), BF16),
                   jax.ShapeDtypeStruct((N_Q_HEADS, 1, n), F32)],
        scratch_shapes=[pltpu.VMEM((tq, 1), F32), pltpu.VMEM((tq, wide), F32)],
        compiler_params=_params("parallel", "parallel"),
    )(q, k_all, v_ext, p)


def attn_gate_bwd(dmix, o, p, *, za_block, name):
    n = o.shape[0]
    tt = _rows_tile(n, 512)
    za_half = za_block * HEAD_DIM // HALF
    n_half = ATTN_W // HALF

    def body(*refs):
        dm_ref, o_ref = refs[:2]
        za_refs = refs[2:2 + n_half]
        do_ref, dos_ref, dza_ref, delta_ref = refs[2 + n_half:]
        dm = dm_ref[...]
        ov = o_ref[...].astype(F32)
        za = jnp.concatenate([r[...] for r in za_refs], axis=-1)
        do = dm * _silu(za)
        do_ref[...] = do.astype(BF16)
        dos_ref[...] = (do * LN2).astype(BF16)
        dza_ref[...] = (dm * ov * _dsilu(za)).astype(BF16)
        prod = do * ov
        for h in range(N_Q_HEADS):
            col = jnp.sum(prod[:, h * HEAD_DIM:(h + 1) * HEAD_DIM], axis=-1, keepdims=True) * LN2
            delta_ref[h] = _as_row(col)

    spec = pl.BlockSpec((tt, ATTN_W), lambda i: (i, 0))
    shape = jax.ShapeDtypeStruct((n, ATTN_W), BF16)
    za_specs = [pl.BlockSpec((tt, HALF), functools.partial(lambda i, cb: (i, cb), cb=za_half + h))
                for h in range(n_half)]
    return pl.pallas_call(
        body, name=name, grid=(n // tt,),
        in_specs=[spec, spec] + za_specs,
        out_specs=[spec, spec, spec, pl.BlockSpec((N_Q_HEADS, 1, tt), lambda i: (0, 0, i))],
        out_shape=[shape, shape, shape, jax.ShapeDtypeStruct((N_Q_HEADS, 1, n), F32)],
        compiler_params=_params("parallel"),
    )(dmix, o, *([p] * n_half))


def flash_bwd(q, do, do_s, lse_row, delta_row, k_all, v_ext, *, name):
    n = q.shape[0]
    s_all = k_all.shape[0]
    tq = _rows_tile(n, 1024)
    chunks = _kv_chunks(n, s_all)

    def body(q_ref, do_ref, dos_ref, lse_ref, dl_ref, k_ref, v_ref, dq_ref, dk_ref, dv_ref):
        g = pl.program_id(1)
        i = pl.program_id(2)

        @pl.when((g == 0) & (i == 0))
        def _():
            dk_ref[...] = jnp.zeros_like(dk_ref)
            dv_ref[...] = jnp.zeros_like(dv_ref)

        qv = q_ref[...]
        dov = do_ref[...]
        dosv = dos_ref[...]
        lse = lse_ref[0]
        dl = dl_ref[0]
        dq = jnp.zeros((tq, HEAD_DIM), F32)
        for start, size in chunks:
            kc = k_ref[pl.ds(start, size), :]
            vc = v_ref[pl.ds(start, size), :]
            st = lax.dot_general(kc, qv, _DIMS["nt"], preferred_element_type=F32)
            pt = jnp.exp2(st - lse)
            dpt = lax.dot_general(vc, dosv, _DIMS["nt"], preferred_element_type=F32)
            dst = (pt * (dpt - dl)).astype(BF16)
            dv_ref[pl.ds(start, size), :] += jnp.dot(pt.astype(BF16), dov, preferred_element_type=F32)
            dk_ref[pl.ds(start, size), :] += jnp.dot(dst, qv, preferred_element_type=F32)
            dq = dq + lax.dot_general(dst, kc, _DIMS["tn"], preferred_element_type=F32)
        dq_ref[...] = dq

    qspec = pl.BlockSpec((tq, HEAD_DIM), lambda kh, g, i: (i, kh * Q_PER_KV + g))
    rowspec = pl.BlockSpec((1, 1, tq), lambda kh, g, i: (kh * Q_PER_KV + g, 0, i))
    kvspec = pl.BlockSpec((s_all, HEAD_DIM), lambda kh, g, i: (0, kh))
    return pl.pallas_call(
        body, name=name, grid=(N_KV_HEADS, Q_PER_KV, n // tq),
        in_specs=[qspec, qspec, qspec, rowspec, rowspec, kvspec,
                  pl.BlockSpec((s_all, HEAD_DIM), lambda kh, g, i: (0, kh * (V_EXT_W // HEAD_DIM)))],
        out_specs=[qspec, kvspec, kvspec],
        out_shape=[jax.ShapeDtypeStruct((n, ATTN_W), F32),
                   jax.ShapeDtypeStruct((s_all, KV_W), F32), jax.ShapeDtypeStruct((s_all, KV_W), F32)],
        compiler_params=_params("arbitrary", "arbitrary", "arbitrary"),
    )(q, do, do_s, lse_row, delta_row, k_all, v_ext)


HALF = 512


SUB = 128


def _tap_group(w_ref, w_lanes, ext_ref, r0, lanes, r, tap_of):
    z = None
    for a in range(4):
        k = tap_of(8 * a + r)
        if 0 <= k < CONV_WIDTH:
            term = w_ref[k:k + 1, w_lanes] * ext_ref[pl.ds(r0 + 8 * a, SUB + 8), lanes]
            z = term if z is None else z + term
    return z


def _shifted(z, z_ref, r):
    if r == 0:
        return z[0:SUB]
    z_ref[...] = z
    return z_ref[pl.ds(r, SUB), :]


def _fold8(x):
    return jnp.sum(x.reshape(x.shape[0] // 8, 8, x.shape[1]), axis=0)


def _pieces(tt, h):
    return [(rh * SUB, slice(c * SUB, (c + 1) * SUB), slice(h * HALF + c * SUB, h * HALF + (c + 1) * SUB))
            for c in range(HALF // SUB) for rh in range(tt // SUB)]


def _halo_specs(tt, n, cb):
    per = tt // HALO
    last = n // HALO - 1
    return [pl.BlockSpec((HALO, HALF), lambda i: (jnp.maximum(i * per - 1, 0), cb)),
            pl.BlockSpec((tt, HALF), lambda i: (i, cb)),
            pl.BlockSpec((HALO, HALF), lambda i: (jnp.minimum((i + 1) * per, last), cb))]


def conv_fwd(p, dw_w, dw_b, ln_g, ln_b, mix, *, glu_block, name):
    n = p.shape[0]
    ch = dw_w.shape[1]
    nh = ch // HALF
    tt = _rows_tile(n, 256)
    last = n // tt - 1

    def body(*refs):
        a_refs = [refs[3 * h:3 * h + 3] for h in range(nh)]
        b_refs = [refs[3 * (nh + h):3 * (nh + h) + 3] for h in range(nh)]
        pos = 6 * nh
        zb_refs = refs[pos:pos + nh]
        pos += nh
        w_ref, bias_ref, g_ref, be_ref, _, ycv_ref, mix_ref, ext_ref, z_ref = refs[pos:pos + 9]
        i = pl.program_id(0)
        for h in range(nh):
            cs = slice(h * HALF, (h + 1) * HALF)
            ap, am, an = a_refs[h]
            bp, bm, bn = b_refs[h]
            ext_ref[0:HALO, :] = jnp.where(i > 0, ap[...] * _sigmoid(bp[...]), 0.0)
            ext_ref[HALO:HALO + tt, :] = am[...] * _sigmoid(bm[...])
            ext_ref[HALO + tt:2 * HALO + tt, :] = jnp.where(i < last, an[...] * _sigmoid(bn[...]), 0.0)
            for r0, lanes, w_lanes in _pieces(tt, h):
                acc = jnp.broadcast_to(bias_ref[:, w_lanes], (SUB, SUB))
                for r in range(8):
                    z = _tap_group(w_ref, w_lanes, ext_ref, r0, lanes, r, lambda j: j - 1)
                    acc = acc + _shifted(z, z_ref, r)
                ycv_ref[pl.ds(r0, SUB), w_lanes] = acc
        yc = ycv_ref[...]
        mu = jnp.mean(yc, axis=-1, keepdims=True)
        var = jnp.mean(jnp.square(yc - mu), axis=-1, keepdims=True)
        ln = (yc - mu) * lax.rsqrt(var + EPS) * g_ref[...] + be_ref[...]
        out = _silu(ln)
        for h in range(nh):
            cs = slice(h * HALF, (h + 1) * HALF)
            mix_ref[:, cs] = (out[:, cs] * _silu(zb_refs[h][...])).astype(BF16)

    in_specs = []
    for h in range(2 * nh):
        in_specs += _halo_specs(tt, n, glu_block + h)
    in_specs += [pl.BlockSpec((tt, HALF), functools.partial(lambda i, cb: (i, cb), cb=glu_block + 2 * nh + h))
                 for h in range(nh)]
    vec = pl.BlockSpec((1, ch), lambda i: (0, 0))
    in_specs += [pl.BlockSpec((CONV_WIDTH, ch), lambda i: (0, 0)), vec, vec, vec, ANY]
    ops = [p] * (6 * nh + nh) + [dw_w, _row(dw_b), _row(ln_g), _row(ln_b), mix]
    big = pl.BlockSpec((tt, ch), lambda i: (i, 0))
    mix_block = (mix.shape[1] - ch) // ch
    return pl.pallas_call(
        body, name=name, grid=(n // tt,), in_specs=in_specs,
        out_specs=[big, pl.BlockSpec((tt, ch), lambda i: (i, mix_block))],
        out_shape=[jax.ShapeDtypeStruct((n, ch), F32), jax.ShapeDtypeStruct(mix.shape, BF16)],
        input_output_aliases={len(ops) - 1: 1},
        scratch_shapes=[pltpu.VMEM((tt + 2 * HALO, HALF), F32), pltpu.VMEM((SUB + 8, SUB), F32)],
        compiler_params=_params("parallel"),
    )(*ops)


def conv_bwd_rows(dmix, ycv, p, ln_g, ln_b, *, mix_block, zb_block, name):
    n, ch = ycv.shape
    nh = ch // HALF
    tt = _rows_tile(n, 256)

    def body(*refs):
        dm_ref, ycv_ref = refs[:2]
        zb_refs = refs[2:2 + nh]
        g_ref, be_ref, dy_ref, dzb_ref, acc_ref = refs[2 + nh:]
        i = pl.program_id(0)

        @pl.when(i == 0)
        def _():
            acc_ref[...] = jnp.zeros_like(acc_ref)

        yc = ycv_ref[...]
        mu = jnp.mean(yc, axis=-1, keepdims=True)
        var = jnp.mean(jnp.square(yc - mu), axis=-1, keepdims=True)
        rstd = lax.rsqrt(var + EPS)
        xhat = (yc - mu) * rstd
        ln = xhat * g_ref[...] + be_ref[...]
        out = _silu(ln)
        dm = dm_ref[...]
        zb = jnp.concatenate([r[...] for r in zb_refs], axis=-1)
        dzb_ref[...] = (dm * out * _dsilu(zb)).astype(BF16)
        dln = dm * _silu(zb) * _dsilu(ln)
        acc_ref[0:1, :] += jnp.sum(dln * xhat, axis=0, keepdims=True)
        acc_ref[1:2, :] += jnp.sum(dln, axis=0, keepdims=True)
        dxhat = dln * g_ref[...]
        dy_ref[...] = rstd * (dxhat - jnp.mean(dxhat, axis=-1, keepdims=True)
                              - xhat * jnp.mean(dxhat * xhat, axis=-1, keepdims=True))

    big = pl.BlockSpec((tt, ch), lambda i: (i, 0))
    vec = pl.BlockSpec((1, ch), lambda i: (0, 0))
    in_specs = [pl.BlockSpec((tt, ch), lambda i: (i, mix_block)), big]
    in_specs += [pl.BlockSpec((tt, HALF), functools.partial(lambda i, cb: (i, cb), cb=zb_block + h)) for h in range(nh)]
    in_specs += [vec, vec]
    return pl.pallas_call(
        body, name=name, grid=(n // tt,), in_specs=in_specs,
        out_specs=[big, big, pl.BlockSpec((8, ch), lambda i: (0, 0))],
        out_shape=[jax.ShapeDtypeStruct((n, ch), F32), jax.ShapeDtypeStruct((n, ch), BF16),
                   jax.ShapeDtypeStruct((8, ch), F32)],
        compiler_params=_params("arbitrary"),
    )(dmix, ycv, *([p] * nh), _row(ln_g), _row(ln_b))


def conv_bwd_taps(dycv, p, dw_w, *, glu_block, name):
    n, ch = dycv.shape
    nh = ch // HALF
    tt = _rows_tile(n, 256)
    last = n // tt - 1

    def body(*refs):
        d_refs = [refs[3 * h:3 * h + 3] for h in range(nh)]
        a_refs = [refs[3 * (nh + h):3 * (nh + h) + 3] for h in range(nh)]
        b_refs = [refs[3 * (2 * nh + h):3 * (2 * nh + h) + 3] for h in range(nh)]
        w_ref, dglu_ref, dw_ref, db_ref, yext_ref, dext_ref, z_ref, dy_ref, dwp_ref, dbp_ref = refs[9 * nh:]
        i = pl.program_id(0)

        @pl.when(i == 0)
        def _():
            dwp_ref[...] = jnp.zeros_like(dwp_ref)
            dbp_ref[...] = jnp.zeros_like(dbp_ref)

        for h in range(nh):
            cs = slice(h * HALF, (h + 1) * HALF)
            ap, am, an = a_refs[h]
            bp, bm, bn = b_refs[h]
            dp, dm, dn = d_refs[h]
            av = am[...]
            sb = _sigmoid(bm[...])
            yext_ref[0:HALO, :] = jnp.where(i > 0, ap[...] * _sigmoid(bp[...]), 0.0)
            yext_ref[HALO:HALO + tt, :] = av * sb
            yext_ref[HALO + tt:2 * HALO + tt, :] = jnp.where(i < last, an[...] * _sigmoid(bn[...]), 0.0)
            dmain = dm[...]
            dext_ref[0:HALO, :] = jnp.where(i > 0, dp[...], 0.0)
            dext_ref[HALO:HALO + tt, :] = dmain
            dext_ref[HALO + tt:2 * HALO + tt, :] = jnp.where(i < last, dn[...], 0.0)
            for r0, lanes, w_lanes in _pieces(tt, h):
                dsub = dext_ref[pl.ds(HALO + r0, SUB), lanes]
                dy = jnp.zeros((SUB, SUB), F32)
                for r in range(8):
                    z = _tap_group(w_ref, w_lanes, dext_ref, r0, lanes, r, lambda j: CONV_WIDTH - j)
                    dy = dy + _shifted(z, z_ref, r)
                    y_r = yext_ref[pl.ds(r0 + r, SUB + 24), lanes]
                    for a in range(4):
                        k = 8 * a + r - 1
                        if k >= 0:
                            dwp_ref[8 * k:8 * k + 8, w_lanes] += _fold8(dsub * y_r[8 * a:8 * a + SUB])
                dbp_ref[:, w_lanes] += _fold8(dsub)
                dy_ref[pl.ds(r0, SUB), lanes] = dy
            dy = dy_ref[...]
            dglu_ref[:, cs] = (dy * sb).astype(BF16)
            dglu_ref[:, ch + h * HALF:ch + (h + 1) * HALF] = (dy * av * sb * (1.0 - sb)).astype(BF16)

        @pl.when(i == last)
        def _():
            dw_ref[...] = jnp.sum(dwp_ref[...].reshape(32, 8, ch), axis=1)
            db_ref[...] = jnp.broadcast_to(jnp.sum(dbp_ref[...], axis=0, keepdims=True), db_ref.shape)

    in_specs = []
    for h in range(nh):
        in_specs += _halo_specs(tt, n, h)
    for h in range(2 * nh):
        in_specs += _halo_specs(tt, n, glu_block + h)
    in_specs += [pl.BlockSpec((CONV_WIDTH, ch), lambda i: (0, 0))]
    ops = [dycv] * (3 * nh) + [p] * (6 * nh) + [dw_w]
    return pl.pallas_call(
        body, name=name, grid=(n // tt,), in_specs=in_specs,
        out_specs=[pl.BlockSpec((tt, 2 * ch), lambda i: (i, 0)), pl.BlockSpec((32, ch), lambda i: (0, 0)),
                   pl.BlockSpec((8, ch), lambda i: (0, 0))],
        out_shape=[jax.ShapeDtypeStruct((n, 2 * ch), BF16), jax.ShapeDtypeStruct((32, ch), F32),
                   jax.ShapeDtypeStruct((8, ch), F32)],
        scratch_shapes=[pltpu.VMEM((tt + 2 * HALO, HALF), F32), pltpu.VMEM((tt + 2 * HALO, HALF), F32),
                        pltpu.VMEM((SUB + 8, SUB), F32), pltpu.VMEM((tt, HALF), F32),
                        pltpu.VMEM((8 * 32, ch), F32), pltpu.VMEM((8, ch), F32)],
        compiler_params=_params("arbitrary"),
    )(*ops)


def _sgu_common(p_ref, g_ref, be_ref, ws_ref, bs_ref, w):
    gw = w // SGU_GROUPS
    u_pre = p_ref[:, 0:w]
    v_pre = p_ref[:, w:2 * w]
    zc = p_ref[:, 2 * w:3 * w]
    u = _gelu(u_pre)
    v = _gelu(v_pre)
    mu = jnp.mean(v, axis=-1, keepdims=True)
    var = jnp.mean(jnp.square(v - mu), axis=-1, keepdims=True)
    rstd = lax.rsqrt(var + EPS)
    vhat = (v - mu) * rstd
    vn = (vhat * g_ref[...] + be_ref[...]).astype(BF16)
    mixed = jnp.concatenate(
        [jnp.dot(ws_ref[gi].astype(BF16), vn[:, gi * gw:(gi + 1) * gw], preferred_element_type=F32)
         + bs_ref[:, gi:gi + 1] for gi in range(SGU_GROUPS)], axis=-1)
    return u_pre, v_pre, zc, u, rstd, vhat, vn, mixed


def sgu_fwd(p, ln_g, ln_b, ws, bs_t, *, name):
    n, w3 = p.shape
    w = w3 // 3

    def body(p_ref, g_ref, be_ref, ws_ref, bs_ref, m_ref):
        _, _, zc, u, _, _, _, mixed = _sgu_common(p_ref, g_ref, be_ref, ws_ref, bs_ref, w)
        m_ref[...] = (u * mixed * _silu(zc)).astype(BF16)

    vec = pl.BlockSpec((1, w), lambda i: (0, 0))
    return pl.pallas_call(
        body, name=name, grid=(n // CHUNK,),
        in_specs=[pl.BlockSpec((CHUNK, w3), lambda i: (i, 0)), vec, vec,
                  pl.BlockSpec((SGU_GROUPS, CHUNK, CHUNK), lambda i: (0, 0, 0)),
                  pl.BlockSpec((CHUNK, SGU_GROUPS), lambda i: (0, 0))],
        out_specs=pl.BlockSpec((CHUNK, w), lambda i: (i, 0)),
        out_shape=jax.ShapeDtypeStruct((n, w), BF16),
        compiler_params=_params("parallel"),
    )(p, _row(ln_g), _row(ln_b), ws, bs_t)


def sgu_bwd(p, dm, ln_g, ln_b, ws, ws_t, bs_t, *, name):
    n, w3 = p.shape
    w = w3 // 3
    gw = w // SGU_GROUPS

    def body(p_ref, dm_ref, g_ref, be_ref, ws_ref, wst_ref, bs_ref, dp_ref, dws_ref, dbs_ref, acc_ref):
        i = pl.program_id(0)

        @pl.when(i == 0)
        def _():
            dws_ref[...] = jnp.zeros_like(dws_ref)
            dbs_ref[...] = jnp.zeros_like(dbs_ref)
            acc_ref[...] = jnp.zeros_like(acc_ref)

        u_pre, v_pre, zc, u, rstd, vhat, vn, mixed = _sgu_common(p_ref, g_ref, be_ref, ws_ref, bs_ref, w)
        dmv = dm_ref[...]
        um = u * mixed
        dp_ref[:, 2 * w:3 * w] = (dmv * um * _dsilu(zc)).astype(BF16)
        dum = dmv * _silu(zc)
        dp_ref[:, 0:w] = (dum * mixed * _dgelu(u_pre)).astype(BF16)
        dmixed = dum * u
        dmixed_b = dmixed.astype(BF16)
        dvn_parts = []
        for gi in range(SGU_GROUPS):
            cs = slice(gi * gw, (gi + 1) * gw)
            dws_ref[gi] += lax.dot_general(dmixed_b[:, cs], vn[:, cs], _DIMS["nt"], preferred_element_type=F32)
            dbs_ref[:, gi:gi + 1] += jnp.sum(dmixed[:, cs], axis=-1, keepdims=True)
            dvn_parts.append(jnp.dot(wst_ref[gi].astype(BF16), dmixed_b[:, cs], preferred_element_type=F32))
        dvn = jnp.concatenate(dvn_parts, axis=-1)
        acc_ref[0:1, :] += jnp.sum(dvn * vhat, axis=0, keepdims=True)
        acc_ref[1:2, :] += jnp.sum(dvn, axis=0, keepdims=True)
        dvhat = dvn * g_ref[...]
        dv = rstd * (dvhat - jnp.mean(dvhat, axis=-1, keepdims=True)
                     - vhat * jnp.mean(dvhat * vhat, axis=-1, keepdims=True))
        dp_ref[:, w:2 * w] = (dv * _dgelu(v_pre)).astype(BF16)

    vec = pl.BlockSpec((1, w), lambda i: (0, 0))
    wspec = pl.BlockSpec((SGU_GROUPS, CHUNK, CHUNK), lambda i: (0, 0, 0))
    bspec = pl.BlockSpec((CHUNK, SGU_GROUPS), lambda i: (0, 0))
    return pl.pallas_call(
        body, name=name, grid=(n // CHUNK,),
        in_specs=[pl.BlockSpec((CHUNK, w3), lambda i: (i, 0)), pl.BlockSpec((CHUNK, w), lambda i: (i, 0)),
                  vec, vec, wspec, wspec, bspec],
        out_specs=[pl.BlockSpec((CHUNK, w3), lambda i: (i, 0)), wspec, bspec,
                   pl.BlockSpec((8, w), lambda i: (0, 0))],
        out_shape=[jax.ShapeDtypeStruct((n, w3), BF16), jax.ShapeDtypeStruct((SGU_GROUPS, CHUNK, CHUNK), F32),
                   jax.ShapeDtypeStruct((CHUNK, SGU_GROUPS), F32), jax.ShapeDtypeStruct((8, w), F32)],
        compiler_params=_params("arbitrary"),
    )(p, dm, _row(ln_g), _row(ln_b), ws, ws_t, bs_t)


def _adam_math(w, g, m, v):
    m_new = ADAM_B1 * m + (1.0 - ADAM_B1) * g
    v_new = ADAM_B2 * v + (1.0 - ADAM_B2) * (g * g)
    m_hat = m_new / (1.0 - ADAM_B1 ** ADAM_STEP)
    v_hat = v_new / (1.0 - ADAM_B2 ** ADAM_STEP)
    delta = -ADAM_LR * (m_hat / (jnp.sqrt(v_hat) + ADAM_EPS) + ADAM_WD * w)
    return delta, m_new, v_new


def adamw(w, g, m, v, *, name, slots=False, rows=512):
    r, c = w.shape
    tr = min(r, rows)
    assert r % tr == 0, (name, r, tr)

    def body(w_ref, g_ref, m_ref, v_ref, go_ref, d_ref, mo_ref, vo_ref):
        if slots:
            g = g_ref[0].astype(F32)
            for k in range(1, N_DEV):
                g = g + g_ref[k].astype(F32)
        else:
            g = g_ref[...].astype(F32)
        delta, m_new, v_new = _adam_math(w_ref[...], g, m_ref[...], v_ref[...])
        go_ref[...] = g
        d_ref[...] = delta
        mo_ref[...] = m_new
        vo_ref[...] = v_new

    spec = pl.BlockSpec((tr, c), lambda i: (i, 0))
    gspec = pl.BlockSpec((N_DEV, tr, c), lambda i: (0, i, 0)) if slots else spec
    shape = jax.ShapeDtypeStruct((r, c), F32)
    return pl.pallas_call(
        body, name=name, grid=(r // tr,),
        in_specs=[spec, gspec, spec, spec], out_specs=[spec] * 4, out_shape=[shape] * 4,
        compiler_params=_params("parallel"),
    )(w, g, m, v)


def slot_sum(slots, *, name):
    _, r, c = slots.shape

    def body(s_ref, o_ref):
        acc = s_ref[0]
        for k in range(1, N_DEV):
            acc = acc + s_ref[k]
        o_ref[...] = acc

    return pl.pallas_call(
        body, name=name, out_shape=jax.ShapeDtypeStruct((r, c), F32),
        in_specs=[VMEM_SPEC], out_specs=VMEM_SPEC,
        compiler_params=pltpu.CompilerParams(vmem_limit_bytes=VMEM_LIMIT),
    )(slots)


def _after(token, val):
    return val + token[0, 0]


def _zero_of(v):
    bits = lax.bitcast_convert_type(v, jnp.uint16 if v.dtype == BF16 else jnp.uint32)
    return jnp.where((bits | 1) == 0, 1.0, 0.0).astype(F32)


def _pad_rows(a, rows):
    return jnp.pad(a, ((0, rows - a.shape[0]), (0, 0)))


def kernel(x, c, ctx, c_ctx, ada_w, ada_b, norm_g, ev_w_in, ev_q_norm, ev_k_norm, ev_dw_w, ev_dw_b, ev_ln_g, ev_ln_b, ev_w_out, od_w_in, od_ln_g, od_ln_b, od_ws, od_bs, od_w_out, final_g, loss_target, m_c_ctx, m_ada_w, m_ada_b, m_norm_g, m_ev_w_in, m_ev_q_norm, m_ev_k_norm, m_ev_dw_w, m_ev_dw_b, m_ev_ln_g, m_ev_ln_b, m_ev_w_out, m_od_w_in, m_od_ln_g, m_od_ln_b, m_od_ws, m_od_bs, m_od_w_out, m_final_g, v_c_ctx, v_ada_w, v_ada_b, v_norm_g, v_ev_w_in, v_ev_q_norm, v_ev_k_norm, v_ev_dw_w, v_ev_dw_b, v_ev_ln_g, v_ev_ln_b, v_ev_w_out, v_od_w_in, v_od_ln_g, v_od_ln_b, v_od_ws, v_od_bs, v_od_w_out, v_final_g):
    n, d = x.shape[1], x.shape[2]
    lc = ctx.shape[1]
    ev_in = ev_w_in.shape[2] * N_DEV
    od_in = od_w_in.shape[2] * N_DEV
    conv_ch = ev_dw_w.shape[2] * N_DEV
    ada_cols = ada_w.shape[2]
    me = 4 * lax.axis_index("x") + 2 * lax.axis_index("y") + lax.axis_index("c")
    xs, tgt, ctxs = x[0], loss_target[0], ctx[0]
    za_block = (2 * KV_W + ATTN_W) // HEAD_DIM
    glu_block = (2 * KV_W + 2 * ATTN_W) // HALF
    zb_block = glu_block + 2 * conv_ch // HALF

    small = jnp.concatenate([
        jax.nn.silu(c).reshape(1, d),
        od_ln_g.reshape(1, -1), od_ln_b.reshape(1, -1)], axis=1)
    small = _pad_rows(small, 8)
    dw_rows = _pad_rows(ev_dw_w[0], 32)
    small_g, dw_g = all_gather([small, dw_rows], name="gather_small")
    sc_all = small_g[:, 0, :d]
    shard = d // N_DEV
    od_ln_g_full = small_g[:, 0, d:d + shard].reshape(d)
    od_ln_b_full = small_g[:, 0, d + shard:d + 2 * shard].reshape(d)
    dw_w_full = jnp.moveaxis(dw_g, 0, 1).reshape(32, conv_ch)[:CONV_WIDTH]
    scc = jax.nn.silu(c_ctx)
    sc16 = _pad_rows(jnp.concatenate([sc_all, scc.reshape(1, d)], axis=0), 16)

    ada_bf = ada_w.astype(BF16)
    mod_loc = [matmul(sc16, ada_bf[l], mode="nn", tm=16, tn=ada_cols, tk=d, out_dtype=F32, name=f"ada_mod{l}")
               for l in range(2)]
    (mod_g,) = all_gather([jnp.stack(mod_loc)], name="gather_mod")
    mod_all = jnp.moveaxis(mod_g, 0, 2).reshape(2, 16, N_DEV * ada_cols) + ada_b[:, None, :]
    mod_me = lax.dynamic_index_in_dim(mod_all, me, axis=1, keepdims=False)
    shift = [mod_me[l, :d] for l in range(2)]
    scale = [mod_me[l, d:2 * d] for l in range(2)]
    gate = [mod_me[l, 2 * d:] for l in range(2)]
    shift_c, scale_c = mod_all[0, 8, :d], mod_all[0, 8, d:2 * d]

    h_w_in0, near_token = staged_gather_start(1, (ev_w_in[0] + _zero_of(mod_g[0, 0, 0, 0])).astype(BF16), None,
                                              name="gather_w_in0_chips_start")

    cexp, sexp = rope_tables(n)
    h0 = norm_mod_fwd(xs, norm_g[0], _after(near_token, shift[0]), scale[0], name="norm_mod_fwd0")
    hc = norm_mod_fwd(ctxs, norm_g[0], shift_c, scale_c, name="norm_mod_fwd_ctx")

    wi0_shard, wi0_land = staged_gather_wait(1, h_w_in0, h0, name="gather_w_in0_chips_wait")
    h_w_in0, far_token = staged_gather_start(2, wi0_shard, wi0_land, name="gather_w_in0_cores_start")
    wi0_shard, wi0_land = staged_gather_wait(2, h_w_in0, far_token, name="gather_w_in0_cores_wait")
    wi0_g = lax.dynamic_update_slice_in_dim(wi0_land, wi0_shard[None], me, axis=0)
    wi0 = jnp.moveaxis(wi0_g, 0, 1).reshape(d, ev_in)
    later = [w[0].astype(BF16) for w in (ev_w_out, od_w_in, od_w_out)]
    (h_wo0, h_wi1, h_wo1), w_token = exchange_start(later, scatter=False, name="gather_rest_start", after=wi0_g)

    def landed(handle, after, name):
        own, land = exchange_wait(handle, after, scatter=False, name=name)
        return lax.dynamic_update_slice_in_dim(land, own[None], me, axis=0)

    p0 = matmul(h0, wi0, mode="nn", tm=1024, tn=ev_in // 4, tk=d, out_dtype=F32, name="proj_in0", after=w_token)
    pc = matmul(hc, wi0, mode="nn", tm=lc, tn=2 * KV_W, tk=d, out_dtype=F32, n_out=2 * KV_W, name="proj_in_ctx")
    q_r, k_all, v_ext = qkv_prep_fwd(p0, ev_q_norm[0], ev_k_norm[0], cexp, sexp, latent=True, name="qkv_prep",
                                     rows_all=n + lc)
    k_all, v_ext = qkv_prep_fwd(pc, None, ev_k_norm[0], None, None, latent=False, name="kv_prep_ctx",
                                rows_all=n + lc, k_all=k_all, v_ext=v_ext)
    o_attn, mix_a, lse = flash_fwd(q_r, k_all, v_ext, p0, za_block=za_block, mix_width=ATTN_W + conv_ch,
                                   name="flash_fwd")
    ycv, mix0 = conv_fwd(p0, dw_w_full, ev_dw_b[0], ev_ln_g[0], ev_ln_b[0], mix_a, glu_block=glu_block,
                         name="conv_fwd")
    wo0 = landed(h_wo0, ycv, "gather_w_out0_wait").reshape(-1, d)
    x1, r0, h1 = matmul(mix0, wo0, mode="nn", tm=512, tn=d, tk=mix0.shape[1], out_dtype=F32, name="proj_out0",
                        res=xs, gate=_row(gate[0]), norm=(norm_g[1], shift[1], scale[1]))

    wi1 =jnp.moveaxis(landed(h_wi1, h1, "gather_w_in1_wait"), 0, 1).reshape(d, od_in)
    p1 = matmul(h1, wi1, mode="nn", tm=1024, tn=od_in // 4, tk=d, out_dtype=F32, name="proj_in1")
    ws_bf = od_ws[0]
    bs_t = od_bs[0].T
    m1 = sgu_fwd(p1, od_ln_g_full, od_ln_b_full, ws_bf, bs_t, name="sgu_fwd")
    wo1 = landed(h_wo1, m1, "gather_w_out1_wait").reshape(-1, d)
    dx2, dr1, acc_final, loss_tile = proj_out_loss(m1, wo1, x1, gate[1], tgt, final_g, name="proj_out1_loss")

    dm1 = matmul(dr1, wo1, mode="nt", tm=1024, tn=1024, tk=d, out_dtype=F32, name="d_mix1")
    dwo1 = matmul(m1, dr1, mode="tn", tm=1024, tn=1024, tk=2048, out_dtype=BF16, name="d_wout1")
    dp1, dws, dbs_t, acc_sgu = sgu_bwd(p1, dm1, od_ln_g_full, od_ln_b_full, ws_bf, jnp.swapaxes(ws_bf, 1, 2), bs_t,
                                       name="sgu_bwd")
    dwi1_s = matmul(h1, dp1, mode="tn", tm=2048, tn=od_in // N_DEV, tk=2048, out_dtype=BF16, name="d_win1",
                    split_out=True)
    dwo1_s = dwo1.reshape(N_DEV, -1, d)
    (h_gi1, h_go1), g1_token = exchange_start([dwi1_s, dwo1_s], scatter=True, name="grads1_start")
    dh1 = matmul(dp1, wi1, mode="nt", tm=1024, tn=512, tk=od_in, out_dtype=F32, name="d_h1")
    dx1, dr0, acc_norm1 = norm_mod_bwd(x1, dh1, dx2, norm_g[1], _after(g1_token, scale[1]), name="norm_mod_bwd1",
                                       branch=(r0, gate[0]))

    dmix0 = matmul(dr0, wo0, mode="nt", tm=1024, tn=1024, tk=d, out_dtype=F32, name="d_mix0")
    dwo0 = matmul(mix0, dr0, mode="tn", tm=1024, tn=1024, tk=2048, out_dtype=BF16, name="d_wout0")
    do_attn, do_s, dza, delta = attn_gate_bwd(dmix0, o_attn, p0, za_block=za_block, name="attn_gate_bwd")
    dycv, dzb, acc_ln = conv_bwd_rows(dmix0, ycv, p0, ev_ln_g[0], ev_ln_b[0], mix_block=ATTN_W // conv_ch,
                                      zb_block=zb_block, name="conv_bwd_rows")
    dglu, ddw_w, acc_dwb = conv_bwd_taps(dycv, p0, dw_w_full, glu_block=glu_block, name="conv_bwd_taps")
    dq_r, dk_all, dv_all = flash_bwd(q_r, do_attn, do_s, lse, delta, k_all, v_ext, name="flash_bwd")
    dkvq, acc_qk = qkv_prep_bwd(p0, dq_r, dk_all, dv_all, ev_q_norm[0], ev_k_norm[0], cexp, sexp,
                                latent=True, name="qkv_prep_bwd")
    dpc, acc_kc = qkv_prep_bwd(pc, None, dk_all, dv_all, None, ev_k_norm[0], None, None,
                               latent=False, name="kv_prep_ctx_bwd")
    dp0 = jnp.concatenate([dkvq, dza, dglu, dzb], axis=1)
    dwi0 = matmul(h0, dp0, mode="tn", tm=1024, tn=ev_in // 4, tk=2048, out_dtype=F32, name="d_win0")
    dwi0 = matmul(hc, dpc, mode="tn", tm=512, tn=2 * KV_W, tk=lc, out_dtype=F32, name="d_win0_ctx", add=dwi0)
    dwi0_s = jnp.moveaxis(dwi0.astype(BF16).reshape(d, N_DEV, ev_in // N_DEV), 1, 0)
    dwo0_s = dwo0.reshape(N_DEV, -1, d)
    (h_gi0, h_go0), g0_token = exchange_start([dwi0_s, dwo0_s], scatter=True, name="grads0_start")
    dh0 = matmul(dp0, wi0, mode="nt", tm=1024, tn=512, tk=ev_in, out_dtype=F32, name="d_h0", after=g0_token)
    dhc = matmul(dpc, wi0, mode="nt", tm=lc, tn=512, tk=2 * KV_W, out_dtype=F32, name="d_h_ctx")
    grad_x, acc_norm0 = norm_mod_bwd(xs, dh0, dx1, norm_g[0], _after(g0_token, scale[0]), name="norm_mod_bwd0")
    _, acc_normc = norm_mod_bwd(ctxs, dhc, None, norm_g[0], scale_c, name="norm_mod_bwd_ctx")

    zeros_d = jnp.zeros((d,), F32)
    dmod0 = jnp.stack([acc_norm0[0], acc_norm0[1], acc_norm1[GATE_ROW]])
    dmod1 = jnp.stack([acc_norm1[0], acc_norm1[1], acc_final[GATE_ROW]])
    dmodc = jnp.stack([acc_normc[0], acc_normc[1]])
    half_pad = jnp.zeros((d - 2 * conv_ch,), F32) if d > 2 * conv_ch else jnp.zeros((0,), F32)
    row_a = jnp.concatenate([acc_dwb[0], acc_ln[0], half_pad])
    row_b = jnp.concatenate([acc_ln[1], acc_qk[0], acc_qk[1] + acc_kc[1],
                             jnp.zeros((d - conv_ch - 2 * HEAD_DIM,), F32)])
    row_c = jnp.concatenate([dbs_t.T.reshape(-1), jnp.zeros((d - SGU_GROUPS * CHUNK,), F32)])
    row_loss = jnp.concatenate([loss_tile[0, :1], jnp.zeros((d - 1,), F32)])
    pack = jnp.concatenate([
        dmod0, dmod1, dmodc,
        (acc_norm0[2] + acc_normc[2])[None], acc_norm1[2][None],
        acc_final[0][None],
        acc_sgu[0][None], acc_sgu[1][None],
        row_a[None], row_b[None], row_c[None], row_loss[None],
        ddw_w.reshape(-1, d),
        dws.reshape(-1, d),
    ], axis=0)
    n_rows = pack.shape[0]
    pack = _pad_rows(pack, -(-n_rows // 8) * 8)
    (h_pack,), pack_token = exchange_start([pack], scatter=False, name="small_grads_start")

    def summands(handle, after, name):
        mine, land = exchange_wait(handle, after, scatter=True, name=name)
        own = lax.dynamic_index_in_dim(mine, me, axis=0, keepdims=True)
        return lax.dynamic_update_slice_in_dim(land, own, me, axis=0)

    out = {}

    def upd(key, w, g, m, v, slots=False, rows=512):
        shp = w.shape
        w2 = w.reshape(-1, shp[-1])
        g2 = g.reshape((N_DEV, -1, shp[-1])) if slots else g.reshape(-1, shp[-1])
        res = adamw(w2, g2, m.reshape(w2.shape), v.reshape(w2.shape), name="adamw_" + key, slots=slots, rows=rows)
        out[key] = tuple(r.reshape(shp) for r in res)

    gi1 = summands(h_gi1, pack_token, "grads_w_in1_wait")
    upd("od_w_in", od_w_in, gi1, m_od_w_in, v_od_w_in, slots=True, rows=256)
    go1 = summands(h_go1, out["od_w_in"][1], "grads_w_out1_wait")
    upd("od_w_out", od_w_out, go1, m_od_w_out, v_od_w_out, slots=True, rows=256)
    go0 = summands(h_go0, out["od_w_out"][1], "grads_w_out0_wait")
    upd("ev_w_out", ev_w_out, go0, m_ev_w_out, v_ev_w_out, slots=True, rows=256)
    gi0 = summands(h_gi0, out["ev_w_out"][1], "grads_w_in0_wait")
    upd("ev_w_in", ev_w_in, gi0, m_ev_w_in, v_ev_w_in, slots=True, rows=256)

    pack_own, pack_land = exchange_wait(h_pack, out["ev_w_in"][1], scatter=False, name="small_grads_wait")
    pack_g = lax.dynamic_update_slice_in_dim(pack_land, pack_own[None], me, axis=0)
    gsum = slot_sum(pack_g, name="sum_small_grads")
    loss = gsum[16, 0]
    dw_rows_n = 32 * conv_ch // d
    g_dw_w = gsum[17:17 + dw_rows_n].reshape(32, conv_ch)[:CONV_WIDTH]
    g_od_ws = gsum[17 + dw_rows_n:17 + dw_rows_n + SGU_GROUPS * CHUNK * CHUNK // d].reshape(od_ws.shape)

    dmodc_sum = jnp.concatenate([gsum[6], gsum[7], zeros_d])
    col0 = me * ada_cols
    dm_cols = []
    for l in range(2):
        rows = pack_g[:, 3 * l:3 * l + 3, :].reshape(N_DEV, 3 * d)
        extra = dmodc_sum[None] if l == 0 else jnp.zeros((1, 3 * d), F32)
        full = _pad_rows(jnp.concatenate([rows, extra], axis=0), 16)
        dm_cols.append(lax.dynamic_slice_in_dim(full, col0, ada_cols, axis=1))
    g_ada_w = jnp.stack([matmul(sc16, dm_cols[l], mode="tn", tm=512, tn=ada_cols, tk=16, out_dtype=F32,
                                name=f"d_ada_w{l}") for l in range(2)])
    dsc = matmul(dm_cols[0], ada_bf[0], mode="nt", tm=16, tn=512, tk=ada_cols, out_dtype=F32, name="d_scc")
    (_, dscc_sum) = all_gather([dsc[8:16]], name="gather_dscc", sum_out=True)
    sg = jax.nn.sigmoid(c_ctx)
    g_c_ctx = dscc_sum[0] * (sg * (1.0 + c_ctx * (1.0 - sg)))
    g_ada_b = jnp.stack([gsum[0:3].reshape(-1) + dmodc_sum, gsum[3:6].reshape(-1)])

    upd("ada_w", ada_w, g_ada_w, m_ada_w, v_ada_w)

    def my_shard(full, size):
        return lax.dynamic_slice_in_dim(full, me * size, size, axis=full.ndim - 1)

    small_items = [
        ("c_ctx", c_ctx, g_c_ctx, m_c_ctx, v_c_ctx),
        ("ada_b", ada_b, g_ada_b, m_ada_b, v_ada_b),
        ("norm_g", norm_g, gsum[8:10], m_norm_g, v_norm_g),
        ("ev_q_norm", ev_q_norm, gsum[14, conv_ch:conv_ch + HEAD_DIM], m_ev_q_norm, v_ev_q_norm),
        ("ev_k_norm", ev_k_norm, gsum[14, conv_ch + HEAD_DIM:conv_ch + 2 * HEAD_DIM], m_ev_k_norm, v_ev_k_norm),
        ("ev_dw_w", ev_dw_w, my_shard(g_dw_w, conv_ch // N_DEV), m_ev_dw_w, v_ev_dw_w),
        ("ev_dw_b", ev_dw_b, gsum[13, :conv_ch], m_ev_dw_b, v_ev_dw_b),
        ("ev_ln_g", ev_ln_g, gsum[13, conv_ch:2 * conv_ch], m_ev_ln_g, v_ev_ln_g),
        ("ev_ln_b", ev_ln_b, gsum[14, :conv_ch], m_ev_ln_b, v_ev_ln_b),
        ("od_ln_g", od_ln_g, my_shard(gsum[11], shard), m_od_ln_g, v_od_ln_g),
        ("od_ln_b", od_ln_b, my_shard(gsum[12], shard), m_od_ln_b, v_od_ln_b),
        ("od_ws", od_ws, g_od_ws, m_od_ws, v_od_ws),
        ("od_bs", od_bs, gsum[15, :SGU_GROUPS * CHUNK], m_od_bs, v_od_bs),
        ("final_g", final_g, gsum[10], m_final_g, v_final_g),
    ]
    sizes = [it[1].size for it in small_items]
    total = sum(sizes)
    lanes = 1024
    prow = -(-total // lanes)
    prow = -(-prow // 8) * 8

    def pack_small(idx):
        flat = jnp.concatenate([it[idx].reshape(-1).astype(F32) for it in small_items])
        return jnp.pad(flat, (0, prow * lanes - total)).reshape(prow, lanes)

    sres = adamw(pack_small(1), pack_small(2), pack_small(3), pack_small(4), name="adamw_small", rows=prow)
    off = 0
    for it, size in zip(small_items, sizes):
        out[it[0]] = tuple(r.reshape(-1)[off:off + size].reshape(it[1].shape) for r in sres)
        off += size

    names = ['c_ctx', 'ada_w', 'ada_b', 'norm_g', 'ev_w_in', 'ev_q_norm', 'ev_k_norm', 'ev_dw_w', 'ev_dw_b',
             'ev_ln_g', 'ev_ln_b', 'ev_w_out', 'od_w_in', 'od_ln_g', 'od_ln_b', 'od_ws', 'od_bs', 'od_w_out',
             'final_g']
    return (loss, grad_x[None], *[out[k][0] for k in names], *[out[k][1] for k in names],
            *[out[k][2] for k in names], *[out[k][3] for k in names])
```

```python
import functools
import math

import jax
import jax.numpy as jnp
from jax import lax
from jax.experimental import pallas as pl
from jax.experimental.pallas import tpu as pltpu

F32 = jnp.float32
BF16 = jnp.bfloat16
MESH = pl.DeviceIdType.MESH

EPS = 1e-6
HEAD_DIM = 128
N_Q_HEADS = 8
N_KV_HEADS = 2
Q_PER_KV = N_Q_HEADS // N_KV_HEADS
ATTN_W = N_Q_HEADS * HEAD_DIM
KV_W = N_KV_HEADS * HEAD_DIM
ATTN_SCALE = HEAD_DIM ** -0.5
LN2 = math.log(2.0)
Q_SCALE = ATTN_SCALE / LN2
ROPE_THETA = 10000.0
GRID_W = 64
CONV_WIDTH = 31
HALO = 16
CHUNK = 128
SGU_GROUPS = 8
N_DEV = 8

ADAM_LR = 0.001
ADAM_B1 = 0.9
ADAM_B2 = 0.999
ADAM_EPS = 1e-08
ADAM_WD = 0.01
ADAM_STEP = 10

VMEM_LIMIT = 56 * 1024 * 1024
ANY = pl.BlockSpec(memory_space=pl.ANY)
VMEM_SPEC = pl.BlockSpec(memory_space=pltpu.VMEM)


def _params(*sem):
    return pltpu.CompilerParams(dimension_semantics=sem, vmem_limit_bytes=VMEM_LIMIT)


def _sigmoid(x):
    return 1.0 / (1.0 + jnp.exp(-x))


def _silu(x):
    return x * _sigmoid(x)


def _dsilu(x):
    s = _sigmoid(x)
    return s * (1.0 + x * (1.0 - s))


_GELU_C = math.sqrt(2.0 / math.pi)


def _gelu(x):
    t = jnp.tanh(_GELU_C * (x + 0.044715 * (x * x * x)))
    return 0.5 * x * (1.0 + t)


def _dgelu(x):
    t = jnp.tanh(_GELU_C * (x + 0.044715 * (x * x * x)))
    return 0.5 * (1.0 + t) + 0.5 * x * (1.0 - t * t) * (_GELU_C * (1.0 + 3.0 * 0.044715 * (x * x)))


def _row(v):
    return v.reshape(1, -1).astype(F32)


def _flat_id(p):
    return 4 * p[0] + 2 * p[1] + p[2]


def _gather_body(n_arr, sum_out):
    def body(*refs):
        x_refs = refs[:n_arr]
        out_refs = refs[n_arr:2 * n_arr]
        pos = 2 * n_arr
        sum_refs = refs[pos:pos + n_arr] if sum_out else ()
        pos += n_arr if sum_out else 0
        send_sems, recv_sems, local_sems = refs[pos:pos + 3]
        x, y, c = lax.axis_index("x"), lax.axis_index("y"), lax.axis_index("c")
        me, sibling = (x, y, c), (x, y, 1 - c)
        chips = [(1 - x, y), (x, 1 - y), (1 - x, 1 - y)]

        def copy(a, k, block, to, src=None):
            rows = out_refs[a].at[_flat_id(block)]
            return pltpu.make_async_remote_copy(
                src_ref=rows if src is None else src, dst_ref=rows,
                send_sem=send_sems.at[a, k], recv_sem=recv_sems.at[a, k],
                device_id=to, device_id_type=MESH)

        sends = []
        mine = []
        for a in range(n_arr):
            cp = pltpu.make_async_copy(x_refs[a], out_refs[a].at[_flat_id(me)], local_sems.at[a])
            cp.start()
            mine.append(cp)
            first = [copy(a, 0, me, sibling, src=x_refs[a])]
            first += [copy(a, 1 + j, me, (*chip, c), src=x_refs[a]) for j, chip in enumerate(chips)]
            for cp in first:
                cp.start()
            sends += first
        for a in range(n_arr):
            for j, chip in enumerate(chips):
                copy(a, 1 + j, (*chip, c), me).wait_recv()
                fwd = copy(a, 4 + j, (*chip, c), sibling)
                fwd.start()
                sends.append(fwd)
        for a in range(n_arr):
            copy(a, 0, sibling, me).wait_recv()
            for j, chip in enumerate(chips):
                copy(a, 4 + j, (*chip, 1 - c), me).wait_recv()
        for cp in sends:
            cp.wait_send()
        for cp in mine:
            cp.wait()
        if sum_out:
            for a in range(n_arr):
                acc = out_refs[a][0]
                for k in range(1, N_DEV):
                    acc = acc + out_refs[a][k]
                sum_refs[a][...] = acc

    return body


def all_gather(arrs, *, name, sum_out=False):
    n = len(arrs)
    spec = VMEM_SPEC
    out_shape = [jax.ShapeDtypeStruct((N_DEV,) + a.shape, a.dtype) for a in arrs]
    out_specs = [spec] * n
    if sum_out:
        out_shape += [jax.ShapeDtypeStruct(a.shape, a.dtype) for a in arrs]
        out_specs += [VMEM_SPEC] * n
    res = pl.pallas_call(
        _gather_body(n, sum_out), name=name,
        out_shape=out_shape, in_specs=[spec] * n, out_specs=out_specs,
        scratch_shapes=[pltpu.SemaphoreType.DMA((n, 7)), pltpu.SemaphoreType.DMA((n, 7)),
                        pltpu.SemaphoreType.DMA((n,))],
        compiler_params=pltpu.CompilerParams(vmem_limit_bytes=VMEM_LIMIT),
    )(*arrs)
    return res


_RELATIONS = [(rx, ry, rc) for rx in (0, 1) for ry in (0, 1) for rc in (0, 1)][1:]


HBM_SPEC = pl.BlockSpec(memory_space=pltpu.HBM)
SEM_SPEC = pl.BlockSpec(memory_space=pltpu.SEMAPHORE)
_DATAFLOW = pltpu.SideEffectType.DATAFLOW_SIDE_EFFECTING
N_PEERS = N_DEV - 1


def _peer_copy(src_ref, land_ref, send_sem, recv_sem, k, rel, scatter, sending):
    x, y, c = lax.axis_index("x"), lax.axis_index("y"), lax.axis_index("c")
    rx, ry, rc = rel
    peer = (1 - x if rx else x, 1 - y if ry else y, 1 - c if rc else c)
    src = src_ref.at[_flat_id(peer)] if scatter else src_ref
    dst = land_ref.at[_flat_id((x, y, c)) if sending else _flat_id(peer)]
    return pltpu.make_async_remote_copy(src_ref=src, dst_ref=dst, send_sem=send_sem.at[k], recv_sem=recv_sem.at[k],
                                        device_id=peer, device_id_type=MESH)


def exchange_start(arrs, *, scatter, name, after=None):
    n = len(arrs)
    lands = [lax.empty((N_DEV,) + (a.shape[1:] if scatter else a.shape), a.dtype) for a in arrs]
    n_in = 2 * n + (after is not None)

    def body(*refs):
        srcs, lnds, sems = refs[:n], refs[n:2 * n], refs[n_in:n_in + 2 * n]
        token = refs[n_in + 4 * n]
        for a in range(n):
            for k, rel in enumerate(_RELATIONS):
                _peer_copy(srcs[a], lnds[a], sems[2 * a], sems[2 * a + 1], k, rel, scatter, True).start()
        token[...] = jnp.zeros_like(token)

    outs = pl.pallas_call(
        body, name=name,
        out_shape=[pltpu.SemaphoreType.DMA((N_PEERS,))] * (2 * n)
        + [pltpu.HBM(a.shape, a.dtype) for a in arrs] + [pltpu.HBM(l.shape, l.dtype) for l in lands]
        + [jax.ShapeDtypeStruct((8, 128), F32)],
        in_specs=[HBM_SPEC] * (2 * n) + ([ANY] if after is not None else []),
        out_specs=[SEM_SPEC] * (2 * n) + [HBM_SPEC] * (2 * n) + [VMEM_SPEC],
        input_output_aliases={i: 2 * n + i for i in range(2 * n)},
        compiler_params=pltpu.CompilerParams(has_side_effects=_DATAFLOW),
    )(*[pltpu.with_memory_space_constraint(a, pltpu.HBM) for a in arrs],
      *[pltpu.with_memory_space_constraint(l, pltpu.HBM) for l in lands],
      *([after] if after is not None else []))
    handles = [(outs[2 * a], outs[2 * a + 1], outs[2 * n + a], outs[3 * n + a]) for a in range(n)]
    return handles, outs[4 * n]


def exchange_wait(handle, after, *, scatter, name):
    send_sem, recv_sem, src, land = handle

    def body(src_ref, land_ref, send_ref, recv_ref, after_ref, src_out, land_out):
        for k, rel in enumerate(_RELATIONS):
            cp = _peer_copy(src_ref, land_ref, send_ref, recv_ref, k, rel, scatter, False)
            cp.wait_send()
            cp.wait_recv()

    outs = pl.pallas_call(
        body, name=name,
        out_shape=[pltpu.HBM(src.shape, src.dtype), pltpu.HBM(land.shape, land.dtype)],
        in_specs=[HBM_SPEC, HBM_SPEC, SEM_SPEC, SEM_SPEC, ANY],
        out_specs=[HBM_SPEC, HBM_SPEC], input_output_aliases={0: 0, 1: 1},
        compiler_params=pltpu.CompilerParams(has_side_effects=_DATAFLOW),
    )(src, land, send_sem, recv_sem, after)
    return outs[0], outs[1]


_STAGE1 = [(0, 0, 1), (1, 0, 0), (0, 1, 0), (1, 1, 0)]
_OTHER_CHIPS = [(1, 0), (0, 1), (1, 1)]


def _stage_copy(stage, k, shard_ref, land_ref, send_sem, recv_sem, sending):
    x, y, c = lax.axis_index("x"), lax.axis_index("y"), lax.axis_index("c")
    if stage == 1:
        rx, ry, rc = _STAGE1[k]
        peer = (1 - x if rx else x, 1 - y if ry else y, 1 - c if rc else c)
        src = shard_ref
        dst = land_ref.at[_flat_id((x, y, c)) if sending else _flat_id(peer)]
    else:
        cx, cy = _OTHER_CHIPS[k]
        peer = (x, y, 1 - c)
        chip = (1 - x if cx else x, 1 - y if cy else y)
        slot = _flat_id((*chip, c)) if sending else _flat_id((*chip, 1 - c))
        src = land_ref.at[_flat_id((*chip, c))]
        dst = land_ref.at[slot]
    return pltpu.make_async_remote_copy(src_ref=src, dst_ref=dst, send_sem=send_sem.at[k], recv_sem=recv_sem.at[k],
                                        device_id=peer, device_id_type=MESH)


def staged_gather_start(stage, shard, land, *, name):
    n_copies = len(_STAGE1) if stage == 1 else len(_OTHER_CHIPS)
    if land is None:
        land = lax.empty((N_DEV,) + shard.shape, shard.dtype)

    def body(shard_ref, land_ref, send_sem, recv_sem, shard_thru, land_thru, token):
        for k in range(n_copies):
            _stage_copy(stage, k, shard_ref, land_ref, send_sem, recv_sem, True).start()
        token[...] = jnp.zeros_like(token)

    outs = pl.pallas_call(
        body, name=name,
        out_shape=[pltpu.SemaphoreType.DMA((n_copies,)), pltpu.SemaphoreType.DMA((n_copies,)),
                   pltpu.HBM(shard.shape, shard.dtype), pltpu.HBM(land.shape, land.dtype),
                   jax.ShapeDtypeStruct((8, 128), F32)],
        in_specs=[HBM_SPEC, HBM_SPEC],
        out_specs=[SEM_SPEC, SEM_SPEC, HBM_SPEC, HBM_SPEC, VMEM_SPEC],
        input_output_aliases={0: 2, 1: 3},
        compiler_params=pltpu.CompilerParams(has_side_effects=_DATAFLOW),
    )(pltpu.with_memory_space_constraint(shard, pltpu.HBM), pltpu.with_memory_space_constraint(land, pltpu.HBM))
    return outs[:4], outs[4]


def staged_gather_wait(stage, handle, after, *, name):
    send_sem, recv_sem, shard, land = handle
    n_copies = len(_STAGE1) if stage == 1 else len(_OTHER_CHIPS)

    def body(shard_ref, land_ref, send_ref, recv_ref, after_ref, shard_out, land_out):
        for k in range(n_copies):
            cp = _stage_copy(stage, k, shard_ref, land_ref, send_ref, recv_ref, False)
            cp.wait_send()
            cp.wait_recv()

    outs = pl.pallas_call(
        body, name=name,
        out_shape=[pltpu.HBM(shard.shape, shard.dtype), pltpu.HBM(land.shape, land.dtype)],
        in_specs=[HBM_SPEC, HBM_SPEC, SEM_SPEC, SEM_SPEC, ANY],
        out_specs=[HBM_SPEC, HBM_SPEC], input_output_aliases={0: 0, 1: 1},
        compiler_params=pltpu.CompilerParams(has_side_effects=_DATAFLOW),
    )(shard, land, send_sem, recv_sem, after)
    return outs[0], outs[1]


_DIMS = {"nn": (((1,), (0,)), ((), ())), "nt": (((1,), (1,)), ((), ())), "tn": (((0,), (0,)), ((), ()))}


def matmul(a, b, *, mode, tm, tn, tk, out_dtype, name, n_out=None, res=None, gate=None, add=None, after=None,
           split_out=False, norm=None):
    if mode == "tn":
        kdim, m = a.shape
    else:
        m, kdim = a.shape
    nfull = b.shape[0] if mode == "nt" else b.shape[1]
    n = nfull if n_out is None else n_out
    tm, tn, tk = min(tm, m), min(tn, n), min(tk, kdim)
    assert m % tm == 0 and n % tn == 0 and kdim % tk == 0, (name, m, n, kdim, tm, tn, tk)
    nk = kdim // tk
    dims = _DIMS[mode]
    a_spec = pl.BlockSpec((tk, tm), lambda j, i, k: (k, i)) if mode == "tn" else pl.BlockSpec((tm, tk), lambda j, i, k: (i, k))
    b_spec = pl.BlockSpec((tn, tk), lambda j, i, k: (j, k)) if mode == "nt" else pl.BlockSpec((tk, tn), lambda j, i, k: (k, j))
    o_spec = pl.BlockSpec((tm, tn), lambda j, i, k: (i, j))
    in_specs = [a_spec, b_spec]
    operands = [a, b]
    aliases = {}
    vec = pl.BlockSpec((1, tn), lambda j, i, k: (0, j))
    if res is not None:
        in_specs += [o_spec, vec]
        operands += [res, gate]
    if norm is not None:
        assert res is not None and tn == n, (name, tn, n)
        in_specs += [vec, vec, vec]
        operands += [_row(v) for v in norm]
    if add is not None:
        in_specs += [o_spec]
        aliases = {len(operands): 0}
        operands += [add]
    if after is not None:
        in_specs += [ANY]
        operands += [after]
    out_cols = n if add is None else add.shape[1]
    out_shape = [jax.ShapeDtypeStruct((m, out_cols), out_dtype)]
    out_specs = [o_spec]
    if split_out:
        out_shape = [jax.ShapeDtypeStruct((n // tn, m, tn), out_dtype)]
        out_specs = [pl.BlockSpec((None, tm, tn), lambda j, i, k: (j, i, 0))]
    if res is not None:
        out_shape.append(jax.ShapeDtypeStruct((m, n), BF16))
        out_specs.append(o_spec)
    if norm is not None:
        out_shape.append(jax.ShapeDtypeStruct((m, n), BF16))
        out_specs.append(o_spec)

    def body(*refs):
        a_ref, b_ref = refs[:2]
        pos = 2
        if res is not None:
            res_ref, gate_ref = refs[pos:pos + 2]
            pos += 2
        if norm is not None:
            ng_ref, nshift_ref, nscale_ref = refs[pos:pos + 3]
            pos += 3
        if add is not None:
            add_ref = refs[pos]
            pos += 1
        if after is not None:
            pos += 1
        o_ref = refs[pos]
        pos += 1
        if res is not None:
            r_ref = refs[pos]
            pos += 1
        if norm is not None:
            h_ref = refs[pos]
            pos += 1
        acc_ref = refs[pos] if nk > 1 else None
        prod = lax.dot_general(a_ref[...].astype(BF16), b_ref[...].astype(BF16), dims,
                               preferred_element_type=F32)

        def finish(acc):
            if res is not None:
                xnew = res_ref[...] + gate_ref[...] * acc
                o_ref[...] = xnew.astype(out_dtype)
                r_ref[...] = acc.astype(BF16)
                if norm is not None:
                    rstd = lax.rsqrt(jnp.mean(xnew * xnew, axis=-1, keepdims=True) + EPS)
                    y = xnew * rstd * ng_ref[...]
                    h_ref[...] = (y * (1.0 + nscale_ref[...]) + nshift_ref[...]).astype(BF16)
            elif add is not None:
                o_ref[...] = (add_ref[...] + acc).astype(out_dtype)
            else:
                o_ref[...] = acc.astype(out_dtype)

        if nk == 1:
            finish(prod)
        else:
            k = pl.program_id(2)

            @pl.when(k == 0)
            def _():
                acc_ref[...] = prod

            @pl.when(k > 0)
            def _():
                acc_ref[...] += prod

            @pl.when(k == nk - 1)
            def _():
                finish(acc_ref[...])

    outs = pl.pallas_call(
        body, name=name, grid=(n // tn, m // tm, nk),
        in_specs=in_specs, out_specs=out_specs, out_shape=out_shape,
        scratch_shapes=[pltpu.VMEM((tm, tn), F32)] if nk > 1 else [],
        input_output_aliases=aliases,
        compiler_params=_params("parallel", "parallel", "arbitrary"),
    )(*operands)
    return outs if res is not None else outs[0]


def _rows_tile(n, want):
    return min(n, want)


def norm_mod_fwd(x, g, shift, scale, *, name):
    n, d = x.shape
    tt = _rows_tile(n, 512)

    def body(x_ref, g_ref, sh_ref, sc_ref, h_ref):
        xv = x_ref[...]
        rstd = lax.rsqrt(jnp.mean(xv * xv, axis=-1, keepdims=True) + EPS)
        y = xv * rstd * g_ref[...]
        h_ref[...] = (y * (1.0 + sc_ref[...]) + sh_ref[...]).astype(BF16)

    vec = pl.BlockSpec((1, d), lambda i: (0, 0))
    return pl.pallas_call(
        body, name=name, grid=(n // tt,),
        in_specs=[pl.BlockSpec((tt, d), lambda i: (i, 0)), vec, vec, vec],
        out_specs=pl.BlockSpec((tt, d), lambda i: (i, 0)),
        out_shape=jax.ShapeDtypeStruct((n, d), BF16),
        compiler_params=_params("parallel"),
    )(x, _row(g), _row(shift), _row(scale))


GATE_ROW = 3


def norm_mod_bwd(x, dh, dres, g, scale, *, name, branch=None):
    n, d = x.shape
    tt = _rows_tile(n, 256)
    has_res = dres is not None
    has_branch = branch is not None
    last = n // tt - 1

    def body(*refs):
        x_ref, dh_ref = refs[:2]
        pos = 2
        if has_res:
            dres_ref = refs[pos]
            pos += 1
        if has_branch:
            r_ref, gate_ref = refs[pos:pos + 2]
            pos += 2
        g_ref, sc_ref, dx_ref = refs[pos:pos + 3]
        pos += 3
        if has_branch:
            dr_ref = refs[pos]
            pos += 1
        acc_ref, s_ref = refs[pos:pos + 2]
        i = pl.program_id(0)
        xv = x_ref[...]
        dhv = dh_ref[...]
        rstd = lax.rsqrt(jnp.mean(xv * xv, axis=-1, keepdims=True) + EPS)
        xhat = xv * rstd
        dxhat = dhv * (g_ref[...] * (1.0 + sc_ref[...]))
        dx = rstd * (dxhat - xhat * jnp.mean(dxhat * xhat, axis=-1, keepdims=True))
        if has_res:
            dx = dx + dres_ref[...]
        dx_ref[...] = dx

        @pl.when(i == 0)
        def _():
            s_ref[...] = jnp.zeros_like(s_ref)

        s_ref[0:1, :] += jnp.sum(dhv, axis=0, keepdims=True)
        s_ref[1:2, :] += jnp.sum(dhv * xhat, axis=0, keepdims=True)
        if has_branch:
            dr_ref[...] = (dx * gate_ref[...]).astype(BF16)
            s_ref[2:3, :] += jnp.sum(dx * r_ref[...].astype(F32), axis=0, keepdims=True)

        @pl.when(i == last)
        def _():
            s1 = s_ref[0:1, :]
            s2 = s_ref[1:2, :]
            acc_ref[...] = jnp.zeros_like(acc_ref)
            acc_ref[0:1, :] = s1
            acc_ref[1:2, :] = s2 * g_ref[...]
            acc_ref[2:3, :] = s2 * (1.0 + sc_ref[...])
            acc_ref[GATE_ROW:GATE_ROW + 1, :] = s_ref[2:3, :]

    vec = pl.BlockSpec((1, d), lambda i: (0, 0))
    big = pl.BlockSpec((tt, d), lambda i: (i, 0))
    ops = [x, dh] + ([dres] if has_res else []) + ([branch[0], _row(branch[1])] if has_branch else [])
    ops += [_row(g), _row(scale)]
    return pl.pallas_call(
        body, name=name, grid=(n // tt,),
        in_specs=[big, big] + ([big] if has_res else []) + ([big, vec] if has_branch else []) + [vec, vec],
        out_specs=[big] + ([big] if has_branch else []) + [pl.BlockSpec((8, d), lambda i: (0, 0))],
        out_shape=[jax.ShapeDtypeStruct((n, d), F32)] + ([jax.ShapeDtypeStruct((n, d), BF16)] if has_branch else [])
        + [jax.ShapeDtypeStruct((8, d), F32)],
        scratch_shapes=[pltpu.VMEM((8, d), F32)],
        compiler_params=_params("arbitrary"),
    )(*ops)


def proj_out_loss(a, b, x_prev, gate, target, g, *, name):
    n, kdim = a.shape
    d = b.shape[1]
    tt = _rows_tile(n, 512)
    part_rows = min(tt, 256)

    def body(a_ref, b_ref, x_ref, t_ref, g_ref, gate_ref, dx_ref, dr_ref, acc_ref, loss_ref):
        i = pl.program_id(0)

        @pl.when(i == 0)
        def _():
            acc_ref[...] = jnp.zeros_like(acc_ref)
            loss_ref[...] = jnp.zeros_like(loss_ref)

        for h in range(tt // part_rows):
            rows = pl.ds(h * part_rows, part_rows)
            r = jnp.dot(a_ref[rows, :], b_ref[...], preferred_element_type=F32)
            xv = x_ref[rows, :] + gate_ref[...] * r
            rstd = lax.rsqrt(jnp.mean(xv * xv, axis=-1, keepdims=True) + EPS)
            xhat = xv * rstd
            e = xhat * g_ref[...] - t_ref[rows, :]
            dy = e * (1.0 / d)
            dxhat = dy * g_ref[...]
            dx = rstd * (dxhat - xhat * jnp.mean(dxhat * xhat, axis=-1, keepdims=True))
            dx_ref[rows, :] = dx
            dr_ref[rows, :] = (dx * gate_ref[...]).astype(BF16)
            acc_ref[0:1, :] += jnp.sum(dy * xhat, axis=0, keepdims=True)
            acc_ref[GATE_ROW:GATE_ROW + 1, :] += jnp.sum(dx * r, axis=0, keepdims=True)
            part = 0.5 * jnp.sum(jnp.mean(e * e, axis=-1, keepdims=True), axis=0, keepdims=True)
            loss_ref[...] += jnp.broadcast_to(part, loss_ref.shape)

    big = pl.BlockSpec((tt, d), lambda i: (i, 0))
    vec = pl.BlockSpec((1, d), lambda i: (0, 0))
    return pl.pallas_call(
        body, name=name, grid=(n // tt,),
        in_specs=[pl.BlockSpec((tt, kdim), lambda i: (i, 0)), pl.BlockSpec((kdim, d), lambda i: (0, 0)),
                  big, big, vec, vec],
        out_specs=[big, big, pl.BlockSpec((8, d), lambda i: (0, 0)), pl.BlockSpec((8, 128), lambda i: (0, 0))],
        out_shape=[jax.ShapeDtypeStruct((n, d), F32), jax.ShapeDtypeStruct((n, d), BF16),
                   jax.ShapeDtypeStruct((8, d), F32), jax.ShapeDtypeStruct((8, 128), F32)],
        compiler_params=_params("arbitrary"),
    )(a, b, x_prev, target, _row(g), _row(gate))


def _as_row(col):
    t = col.shape[0]
    return jnp.transpose(jnp.broadcast_to(col, (t, HEAD_DIM)))[0:1, :]


def _swap_pairs(x):
    lane = lax.broadcasted_iota(jnp.int32, x.shape, 1)
    return jnp.where(lane % 2 == 0, pltpu.roll(x, HEAD_DIM - 1, 1), pltpu.roll(x, 1, 1))


def rope_tables(n):
    rows = n // GRID_W
    row = jnp.repeat(jnp.arange(rows, dtype=F32), GRID_W)
    col = jnp.tile(jnp.arange(GRID_W, dtype=F32), rows)
    n_freq = HEAD_DIM // 4
    inv = jnp.power(ROPE_THETA, jnp.arange(n_freq, dtype=F32) * (-2.0 / (HEAD_DIM // 2)))
    ang = jnp.concatenate([row[:, None] * inv, col[:, None] * inv], axis=-1)
    cos, sin = jnp.cos(ang), jnp.sin(ang)
    cexp = jnp.repeat(cos, 2, axis=-1)
    sexp = jnp.stack([-sin, sin], axis=-1).reshape(n, HEAD_DIM)
    return cexp, sexp


V_EXT_W = 2 * HEAD_DIM


def qkv_prep_fwd(p, wq, wk, cexp, sexp, *, latent, name, rows_all=None, k_all=None, v_ext=None):
    n = p.shape[0]
    tt = _rows_tile(n, 512)
    width = 2 * KV_W + (ATTN_W if latent else 0)

    def body(*refs):
        if latent:
            p_ref, wq_ref, wk_ref, c_ref, s_ref, q_ref, k_ref, v_ref = refs
        else:
            p_ref, wk_ref, _, _, k_ref, v_ref = refs

        def head(xv, w):
            rstd = lax.rsqrt(jnp.mean(xv * xv, axis=-1, keepdims=True) + EPS)
            yv = xv * rstd * w
            if latent:
                yv = yv * c_ref[...] + _swap_pairs(yv) * s_ref[...]
            return yv

        for h in range(N_KV_HEADS):
            sl = slice(h * HEAD_DIM, (h + 1) * HEAD_DIM)
            k_ref[:, sl] = head(p_ref[:, sl], wk_ref[...]).astype(BF16)
            v_ref[:, h * V_EXT_W:h * V_EXT_W + HEAD_DIM] = p_ref[:, KV_W + h * HEAD_DIM:KV_W + (h + 1) * HEAD_DIM].astype(BF16)
            lane = lax.broadcasted_iota(jnp.int32, (tt, HEAD_DIM), 1)
            v_ref[:, h * V_EXT_W + HEAD_DIM:(h + 1) * V_EXT_W] = jnp.where(lane == 0, 1.0, 0.0).astype(BF16)
        if latent:
            for h in range(N_Q_HEADS):
                sl = slice(2 * KV_W + h * HEAD_DIM, 2 * KV_W + (h + 1) * HEAD_DIM)
                q_ref[:, h * HEAD_DIM:(h + 1) * HEAD_DIM] = (head(p_ref[:, sl], wq_ref[...]) * Q_SCALE).astype(BF16)

    vec = pl.BlockSpec((1, HEAD_DIM), lambda i: (0, 0))
    tab = pl.BlockSpec((tt, HEAD_DIM), lambda i: (i, 0))
    vw = N_KV_HEADS * V_EXT_W
    k_shape = jax.ShapeDtypeStruct((rows_all, KV_W), BF16)
    v_shape = jax.ShapeDtypeStruct((rows_all, vw), BF16)
    aliases = {}
    if latent:
        in_specs = [pl.BlockSpec((tt, width), lambda i: (i, 0)), vec, vec, tab, tab]
        ops = [p, _row(wq), _row(wk), cexp, sexp]
        out_specs = [pl.BlockSpec((tt, ATTN_W), lambda i: (i, 0)), pl.BlockSpec((tt, KV_W), lambda i: (i, 0)),
                     pl.BlockSpec((tt, vw), lambda i: (i, 0))]
        out_shape = [jax.ShapeDtypeStruct((n, ATTN_W), BF16), k_shape, v_shape]
    else:
        assert n == tt and (rows_all - n) % n == 0, (n, tt, rows_all)
        first = (rows_all - n) // n
        in_specs = [pl.BlockSpec((tt, width), lambda i: (i, 0)), vec, ANY, ANY]
        ops = [p, _row(wk), k_all, v_ext]
        out_specs = [pl.BlockSpec((tt, KV_W), lambda i: (first, 0)), pl.BlockSpec((tt, vw), lambda i: (first, 0))]
        out_shape = [k_shape, v_shape]
        aliases = {2: 0, 3: 1}
    return pl.pallas_call(
        body, name=name, grid=(n // tt,), in_specs=in_specs, out_specs=out_specs, out_shape=out_shape,
        input_output_aliases=aliases, compiler_params=_params("parallel"),
    )(*ops)


def qkv_prep_bwd(p, dq, dk, dv, wq, wk, cexp, sexp, *, latent, name):
    n = p.shape[0]
    tt = _rows_tile(n, 512)
    width = 2 * KV_W + (ATTN_W if latent else 0)

    def body(*refs):
        if latent:
            p_ref, dq_ref, dk_ref, dv_ref, wq_ref, wk_ref, c_ref, s_ref, dp_ref, acc_ref = refs
        else:
            p_ref, dk_ref, dv_ref, wk_ref, dp_ref, acc_ref = refs
        i = pl.program_id(0)

        @pl.when(i == 0)
        def _():
            acc_ref[...] = jnp.zeros_like(acc_ref)

        def head(xv, dy, w, row):
            if latent:
                dy = dy * c_ref[...] + _swap_pairs(dy * s_ref[...])
            rstd = lax.rsqrt(jnp.mean(xv * xv, axis=-1, keepdims=True) + EPS)
            xhat = xv * rstd
            acc_ref[row:row + 1, :] += jnp.sum(dy * xhat, axis=0, keepdims=True)
            dxhat = dy * w
            return rstd * (dxhat - xhat * jnp.mean(dxhat * xhat, axis=-1, keepdims=True))

        for h in range(N_KV_HEADS):
            sl = slice(h * HEAD_DIM, (h + 1) * HEAD_DIM)
            dp_ref[:, sl] = head(p_ref[:, sl], dk_ref[:, sl], wk_ref[...], 1).astype(BF16)
        dp_ref[:, KV_W:2 * KV_W] = dv_ref[...].astype(BF16)
        if latent:
            for h in range(N_Q_HEADS):
                sl = slice(2 * KV_W + h * HEAD_DIM, 2 * KV_W + (h + 1) * HEAD_DIM)
                dyq = dq_ref[:, h * HEAD_DIM:(h + 1) * HEAD_DIM] * Q_SCALE
                dp_ref[:, sl] = head(p_ref[:, sl], dyq, wq_ref[...], 0).astype(BF16)

    vec = pl.BlockSpec((1, HEAD_DIM), lambda i: (0, 0))
    tab = pl.BlockSpec((tt, HEAD_DIM), lambda i: (i, 0))
    first = 0 if latent else (dk.shape[0] - n) // tt
    kv_spec = pl.BlockSpec((tt, KV_W), lambda i: (first + i, 0))
    p_spec = pl.BlockSpec((tt, width), lambda i: (i, 0))
    if latent:
        in_specs = [p_spec, pl.BlockSpec((tt, ATTN_W), lambda i: (i, 0)), kv_spec, kv_spec, vec, vec, tab, tab]
        ops = [p, dq, dk, dv, _row(wq), _row(wk), cexp, sexp]
    else:
        in_specs = [p_spec, kv_spec, kv_spec, vec]
        ops = [p, dk, dv, _row(wk)]
    return pl.pallas_call(
        body, name=name, grid=(n // tt,), in_specs=in_specs,
        out_specs=[p_spec, pl.BlockSpec((8, HEAD_DIM), lambda i: (0, 0))],
        out_shape=[jax.ShapeDtypeStruct((n, width), BF16), jax.ShapeDtypeStruct((8, HEAD_DIM), F32)],
        compiler_params=_params("arbitrary"),
    )(*ops)


def _kv_chunks(n, s_all, want=1024):
    step = want if n % want == 0 else 256
    chunks = [(s, step) for s in range(0, n, step)]
    if s_all > n:
        chunks.append((n, s_all - n))
    return chunks


def flash_fwd(q, k_all, v_ext, p, *, za_block, mix_width, name):
    n = q.shape[0]
    s_all = k_all.shape[0]
    tq = _rows_tile(n, 512)
    chunks = _kv_chunks(n, s_all)
    wide = 2 * HEAD_DIM

    def body(q_ref, k_ref, v_ref, za_ref, o_ref, mix_ref, lse_ref, m_ref, acc_ref):
        qv = q_ref[...]
        m_ref[...] = jnp.full_like(m_ref, -jnp.inf)
        acc_ref[...] = jnp.zeros_like(acc_ref)
        for start, size in chunks:
            kc = k_ref[pl.ds(start, size), :]
            vc = v_ref[pl.ds(start, size), :]
            s = lax.dot_general(qv, kc, _DIMS["nt"], preferred_element_type=F32)
            m_old = m_ref[...]
            m_new = jnp.maximum(m_old, jnp.max(s, axis=-1, keepdims=True))
            pr = jnp.exp2(s - m_new)
            alpha = jnp.exp2(m_old - m_new)
            acc_ref[...] = alpha * acc_ref[...] + jnp.dot(pr.astype(BF16), vc, preferred_element_type=F32)
            m_ref[...] = m_new
        acc = acc_ref[...]
        denom = acc[:, HEAD_DIM:HEAD_DIM + 1]
        o = acc[:, :HEAD_DIM] / denom
        o_ref[...] = o.astype(BF16)
        mix_ref[...] = (o * _silu(za_ref[...])).astype(BF16)
        lse_ref[0] = _as_row(m_ref[...] + jnp.log2(denom))

    qspec = pl.BlockSpec((tq, HEAD_DIM), lambda h, i: (i, h))
    return pl.pallas_call(
        body, name=name, grid=(N_Q_HEADS, n // tq),
        in_specs=[qspec, pl.BlockSpec((s_all, HEAD_DIM), lambda h, i: (0, h // Q_PER_KV)),
                  pl.BlockSpec((s_all, wide), lambda h, i: (0, h // Q_PER_KV)),
                  pl.BlockSpec((tq, HEAD_DIM), lambda h, i: (i, za_block + h))],
        out_specs=[qspec, qspec, pl.BlockSpec((1, 1, tq), lambda h, i: (h, 0, i))],
        out_shape=[jax.ShapeDtypeStruct((n, ATTN_W), BF16), jax.ShapeDtypeStruct((n, mix_width), BF16),
                   jax.ShapeDtypeStruct((N_Q_HEADS, 1, n), F32)],
        scratch_shapes=[pltpu.VMEM((tq, 1), F32), pltpu.VMEM((tq, wide), F32)],
        compiler_params=_params("parallel", "parallel"),
    )(q, k_all, v_ext, p)


def attn_gate_bwd(dmix, o, p, *, za_block, name):
    n = o.shape[0]
    tt = _rows_tile(n, 512)
    za_half = za_block * HEAD_DIM // HALF
    n_half = ATTN_W // HALF

    def body(*refs):
        dm_ref, o_ref = refs[:2]
        za_refs = refs[2:2 + n_half]
        do_ref, dos_ref, dza_ref, delta_ref = refs[2 + n_half:]
        dm = dm_ref[...]
        ov = o_ref[...].astype(F32)
        za = jnp.concatenate([r[...] for r in za_refs], axis=-1)
        do = dm * _silu(za)
        do_ref[...] = do.astype(BF16)
        dos_ref[...] = (do * LN2).astype(BF16)
        dza_ref[...] = (dm * ov * _dsilu(za)).astype(BF16)
        prod = do * ov
        for h in range(N_Q_HEADS):
            col = jnp.sum(prod[:, h * HEAD_DIM:(h + 1) * HEAD_DIM], axis=-1, keepdims=True) * LN2
            delta_ref[h] = _as_row(col)

    spec = pl.BlockSpec((tt, ATTN_W), lambda i: (i, 0))
    shape = jax.ShapeDtypeStruct((n, ATTN_W), BF16)
    za_specs = [pl.BlockSpec((tt, HALF), functools.partial(lambda i, cb: (i, cb), cb=za_half + h))
                for h in range(n_half)]
    return pl.pallas_call(
        body, name=name, grid=(n // tt,),
        in_specs=[spec, spec] + za_specs,
        out_specs=[spec, spec, spec, pl.BlockSpec((N_Q_HEADS, 1, tt), lambda i: (0, 0, i))],
        out_shape=[shape, shape, shape, jax.ShapeDtypeStruct((N_Q_HEADS, 1, n), F32)],
        compiler_params=_params("parallel"),
    )(dmix, o, *([p] * n_half))


def flash_bwd(q, do, do_s, lse_row, delta_row, k_all, v_ext, *, name):
    n = q.shape[0]
    s_all = k_all.shape[0]
    tq = _rows_tile(n, 1024)
    chunks = _kv_chunks(n, s_all)

    def body(q_ref, do_ref, dos_ref, lse_ref, dl_ref, k_ref, v_ref, dq_ref, dk_ref, dv_ref):
        g = pl.program_id(1)
        i = pl.program_id(2)

        @pl.when((g == 0) & (i == 0))
        def _():
            dk_ref[...] = jnp.zeros_like(dk_ref)
            dv_ref[...] = jnp.zeros_like(dv_ref)

        qv = q_ref[...]
        dov = do_ref[...]
        dosv = dos_ref[...]
        lse = lse_ref[0]
        dl = dl_ref[0]
        dq = jnp.zeros((tq, HEAD_DIM), F32)
        for start, size in chunks:
            kc = k_ref[pl.ds(start, size), :]
            vc = v_ref[pl.ds(start, size), :]
            st = lax.dot_general(kc, qv, _DIMS["nt"], preferred_element_type=F32)
            pt = jnp.exp2(st - lse)
            dpt = lax.dot_general(vc, dosv, _DIMS["nt"], preferred_element_type=F32)
            dst = (pt * (dpt - dl)).astype(BF16)
            dv_ref[pl.ds(start, size), :] += jnp.dot(pt.astype(BF16), dov, preferred_element_type=F32)
            dk_ref[pl.ds(start, size), :] += jnp.dot(dst, qv, preferred_element_type=F32)
            dq = dq + lax.dot_general(dst, kc, _DIMS["tn"], preferred_element_type=F32)
        dq_ref[...] = dq

    qspec = pl.BlockSpec((tq, HEAD_DIM), lambda kh, g, i: (i, kh * Q_PER_KV + g))
    rowspec = pl.BlockSpec((1, 1, tq), lambda kh, g, i: (kh * Q_PER_KV + g, 0, i))
    kvspec = pl.BlockSpec((s_all, HEAD_DIM), lambda kh, g, i: (0, kh))
    return pl.pallas_call(
        body, name=name, grid=(N_KV_HEADS, Q_PER_KV, n // tq),
        in_specs=[qspec, qspec, qspec, rowspec, rowspec, kvspec,
                  pl.BlockSpec((s_all, HEAD_DIM), lambda kh, g, i: (0, kh * (V_EXT_W // HEAD_DIM)))],
        out_specs=[qspec, kvspec, kvspec],
        out_shape=[jax.ShapeDtypeStruct((n, ATTN_W), F32),
                   jax.ShapeDtypeStruct((s_all, KV_W), F32), jax.ShapeDtypeStruct((s_all, KV_W), F32)],
        compiler_params=_params("arbitrary", "arbitrary", "arbitrary"),
    )(q, do, do_s, lse_row, delta_row, k_all, v_ext)


HALF = 512


SUB = 128


def _tap_group(w_ref, w_lanes, ext_ref, r0, lanes, r, tap_of):
    z = None
    for a in range(4):
        k = tap_of(8 * a + r)
        if 0 <= k < CONV_WIDTH:
            term = w_ref[k:k + 1, w_lanes] * ext_ref[pl.ds(r0 + 8 * a, SUB + 8), lanes]
            z = term if z is None else z + term
    return z


def _shifted(z, z_ref, r):
    if r == 0:
        return z[0:SUB]
    z_ref[...] = z
    return z_ref[pl.ds(r, SUB), :]


def _fold8(x):
    return jnp.sum(x.reshape(x.shape[0] // 8, 8, x.shape[1]), axis=0)


def _pieces(tt, h):
    return [(rh * SUB, slice(c * SUB, (c + 1) * SUB), slice(h * HALF + c * SUB, h * HALF + (c + 1) * SUB))
            for c in range(HALF // SUB) for rh in range(tt // SUB)]


def _halo_specs(tt, n, cb):
    per = tt // HALO
    last = n // HALO - 1
    return [pl.BlockSpec((HALO, HALF), lambda i: (jnp.maximum(i * per - 1, 0), cb)),
            pl.BlockSpec((tt, HALF), lambda i: (i, cb)),
            pl.BlockSpec((HALO, HALF), lambda i: (jnp.minimum((i + 1) * per, last), cb))]


def conv_fwd(p, dw_w, dw_b, ln_g, ln_b, mix, *, glu_block, name):
    n = p.shape[0]
    ch = dw_w.shape[1]
    nh = ch // HALF
    tt = _rows_tile(n, 512)
    last = n // tt - 1

    def body(*refs):
        a_refs = [refs[3 * h:3 * h + 3] for h in range(nh)]
        b_refs = [refs[3 * (nh + h):3 * (nh + h) + 3] for h in range(nh)]
        pos = 6 * nh
        zb_refs = refs[pos:pos + nh]
        pos += nh
        w_ref, bias_ref, g_ref, be_ref, _, ycv_ref, mix_ref, ext_ref, z_ref = refs[pos:pos + 9]
        i = pl.program_id(0)
        for h in range(nh):
            cs = slice(h * HALF, (h + 1) * HALF)
            ap, am, an = a_refs[h]
            bp, bm, bn = b_refs[h]
            ext_ref[0:HALO, :] = jnp.where(i > 0, ap[...] * _sigmoid(bp[...]), 0.0)
            ext_ref[HALO:HALO + tt, :] = am[...] * _sigmoid(bm[...])
            ext_ref[HALO + tt:2 * HALO + tt, :] = jnp.where(i < last, an[...] * _sigmoid(bn[...]), 0.0)
            for r0, lanes, w_lanes in _pieces(tt, h):
                acc = jnp.broadcast_to(bias_ref[:, w_lanes], (SUB, SUB))
                for r in range(8):
                    z = _tap_group(w_ref, w_lanes, ext_ref, r0, lanes, r, lambda j: j - 1)
                    acc = acc + _shifted(z, z_ref, r)
                ycv_ref[pl.ds(r0, SUB), w_lanes] = acc
        yc = ycv_ref[...]
        mu = jnp.mean(yc, axis=-1, keepdims=True)
        var = jnp.mean(jnp.square(yc - mu), axis=-1, keepdims=True)
        ln = (yc - mu) * lax.rsqrt(var + EPS) * g_ref[...] + be_ref[...]
        out = _silu(ln)
        for h in range(nh):
            cs = slice(h * HALF, (h + 1) * HALF)
            mix_ref[:, cs] = (out[:, cs] * _silu(zb_refs[h][...])).astype(BF16)

    in_specs = []
    for h in range(2 * nh):
        in_specs += _halo_specs(tt, n, glu_block + h)
    in_specs += [pl.BlockSpec((tt, HALF), functools.partial(lambda i, cb: (i, cb), cb=glu_block + 2 * nh + h))
                 for h in range(nh)]
    vec = pl.BlockSpec((1, ch), lambda i: (0, 0))
    in_specs += [pl.BlockSpec((CONV_WIDTH, ch), lambda i: (0, 0)), vec, vec, vec, ANY]
    ops = [p] * (6 * nh + nh) + [dw_w, _row(dw_b), _row(ln_g), _row(ln_b), mix]
    big = pl.BlockSpec((tt, ch), lambda i: (i, 0))
    mix_block = (mix.shape[1] - ch) // ch
    return pl.pallas_call(
        body, name=name, grid=(n // tt,), in_specs=in_specs,
        out_specs=[big, pl.BlockSpec((tt, ch), lambda i: (i, mix_block))],
        out_shape=[jax.ShapeDtypeStruct((n, ch), F32), jax.ShapeDtypeStruct(mix.shape, BF16)],
        input_output_aliases={len(ops) - 1: 1},
        scratch_shapes=[pltpu.VMEM((tt + 2 * HALO, HALF), F32), pltpu.VMEM((SUB + 8, SUB), F32)],
        compiler_params=_params("parallel"),
    )(*ops)


def conv_bwd_rows(dmix, ycv, p, ln_g, ln_b, *, mix_block, zb_block, name):
    n, ch = ycv.shape
    nh = ch // HALF
    tt = _rows_tile(n, 512)

    def body(*refs):
        dm_ref, ycv_ref = refs[:2]
        zb_refs = refs[2:2 + nh]
        g_ref, be_ref, dy_ref, dzb_ref, acc_ref = refs[2 + nh:]
        i = pl.program_id(0)

        @pl.when(i == 0)
        def _():
            acc_ref[...] = jnp.zeros_like(acc_ref)

        yc = ycv_ref[...]
        mu = jnp.mean(yc, axis=-1, keepdims=True)
        var = jnp.mean(jnp.square(yc - mu), axis=-1, keepdims=True)
        rstd = lax.rsqrt(var + EPS)
        xhat = (yc - mu) * rstd
        ln = xhat * g_ref[...] + be_ref[...]
        out = _silu(ln)
        dm = dm_ref[...]
        zb = jnp.concatenate([r[...] for r in zb_refs], axis=-1)
        dzb_ref[...] = (dm * out * _dsilu(zb)).astype(BF16)
        dln = dm * _silu(zb) * _dsilu(ln)
        acc_ref[0:1, :] += jnp.sum(dln * xhat, axis=0, keepdims=True)
        acc_ref[1:2, :] += jnp.sum(dln, axis=0, keepdims=True)
        dxhat = dln * g_ref[...]
        dy_ref[...] = rstd * (dxhat - jnp.mean(dxhat, axis=-1, keepdims=True)
                              - xhat * jnp.mean(dxhat * xhat, axis=-1, keepdims=True))

    big = pl.BlockSpec((tt, ch), lambda i: (i, 0))
    vec = pl.BlockSpec((1, ch), lambda i: (0, 0))
    in_specs = [pl.BlockSpec((tt, ch), lambda i: (i, mix_block)), big]
    in_specs += [pl.BlockSpec((tt, HALF), functools.partial(lambda i, cb: (i, cb), cb=zb_block + h)) for h in range(nh)]
    in_specs += [vec, vec]
    return pl.pallas_call(
        body, name=name, grid=(n // tt,), in_specs=in_specs,
        out_specs=[big, big, pl.BlockSpec((8, ch), lambda i: (0, 0))],
        out_shape=[jax.ShapeDtypeStruct((n, ch), F32), jax.ShapeDtypeStruct((n, ch), BF16),
                   jax.ShapeDtypeStruct((8, ch), F32)],
        compiler_params=_params("arbitrary"),
    )(dmix, ycv, *([p] * nh), _row(ln_g), _row(ln_b))


def conv_bwd_taps(dycv, p, dw_w, *, glu_block, name):
    n, ch = dycv.shape
    nh = ch // HALF
    tt = _rows_tile(n, 512)
    last = n // tt - 1

    def body(*refs):
        d_refs = [refs[3 * h:3 * h + 3] for h in range(nh)]
        a_refs = [refs[3 * (nh + h):3 * (nh + h) + 3] for h in range(nh)]
        b_refs = [refs[3 * (2 * nh + h):3 * (2 * nh + h) + 3] for h in range(nh)]
        w_ref, dglu_ref, dw_ref, db_ref, yext_ref, dext_ref, z_ref, dy_ref, dwp_ref, dbp_ref = refs[9 * nh:]
        i = pl.program_id(0)

        @pl.when(i == 0)
        def _():
            dwp_ref[...] = jnp.zeros_like(dwp_ref)
            dbp_ref[...] = jnp.zeros_like(dbp_ref)

        for h in range(nh):
            cs = slice(h * HALF, (h + 1) * HALF)
            ap, am, an = a_refs[h]
            bp, bm, bn = b_refs[h]
            dp, dm, dn = d_refs[h]
            av = am[...]
            sb = _sigmoid(bm[...])
            yext_ref[0:HALO, :] = jnp.where(i > 0, ap[...] * _sigmoid(bp[...]), 0.0)
            yext_ref[HALO:HALO + tt, :] = av * sb
            yext_ref[HALO + tt:2 * HALO + tt, :] = jnp.where(i < last, an[...] * _sigmoid(bn[...]), 0.0)
            dmain = dm[...]
            dext_ref[0:HALO, :] = jnp.where(i > 0, dp[...], 0.0)
            dext_ref[HALO:HALO + tt, :] = dmain
            dext_ref[HALO + tt:2 * HALO + tt, :] = jnp.where(i < last, dn[...], 0.0)
            for r0, lanes, w_lanes in _pieces(tt, h):
                dsub = dext_ref[pl.ds(HALO + r0, SUB), lanes]
                dy = jnp.zeros((SUB, SUB), F32)
                for r in range(8):
                    z = _tap_group(w_ref, w_lanes, dext_ref, r0, lanes, r, lambda j: CONV_WIDTH - j)
                    dy = dy + _shifted(z, z_ref, r)
                    y_r = yext_ref[pl.ds(r0 + r, SUB + 24), lanes]
                    for a in range(4):
                        k = 8 * a + r - 1
                        if k >= 0:
                            dwp_ref[8 * k:8 * k + 8, w_lanes] += _fold8(dsub * y_r[8 * a:8 * a + SUB])
                dbp_ref[:, w_lanes] += _fold8(dsub)
                dy_ref[pl.ds(r0, SUB), lanes] = dy
            dy = dy_ref[...]
            dglu_ref[:, cs] = (dy * sb).astype(BF16)
            dglu_ref[:, ch + h * HALF:ch + (h + 1) * HALF] = (dy * av * sb * (1.0 - sb)).astype(BF16)

        @pl.when(i == last)
        def _():
            dw_ref[...] = jnp.sum(dwp_ref[...].reshape(32, 8, ch), axis=1)
            db_ref[...] = jnp.broadcast_to(jnp.sum(dbp_ref[...], axis=0, keepdims=True), db_ref.shape)

    in_specs = []
    for h in range(nh):
        in_specs += _halo_specs(tt, n, h)
    for h in range(2 * nh):
        in_specs += _halo_specs(tt, n, glu_block + h)
    in_specs += [pl.BlockSpec((CONV_WIDTH, ch), lambda i: (0, 0))]
    ops = [dycv] * (3 * nh) + [p] * (6 * nh) + [dw_w]
    return pl.pallas_call(
        body, name=name, grid=(n // tt,), in_specs=in_specs,
        out_specs=[pl.BlockSpec((tt, 2 * ch), lambda i: (i, 0)), pl.BlockSpec((32, ch), lambda i: (0, 0)),
                   pl.BlockSpec((8, ch), lambda i: (0, 0))],
        out_shape=[jax.ShapeDtypeStruct((n, 2 * ch), BF16), jax.ShapeDtypeStruct((32, ch), F32),
                   jax.ShapeDtypeStruct((8, ch), F32)],
        scratch_shapes=[pltpu.VMEM((tt + 2 * HALO, HALF), F32), pltpu.VMEM((tt + 2 * HALO, HALF), F32),
                        pltpu.VMEM((SUB + 8, SUB), F32), pltpu.VMEM((tt, HALF), F32),
                        pltpu.VMEM((8 * 32, ch), F32), pltpu.VMEM((8, ch), F32)],
        compiler_params=_params("arbitrary"),
    )(*ops)


def _sgu_common(p_ref, g_ref, be_ref, ws_ref, bs_ref, w):
    gw = w // SGU_GROUPS
    u_pre = p_ref[:, 0:w]
    v_pre = p_ref[:, w:2 * w]
    zc = p_ref[:, 2 * w:3 * w]
    u = _gelu(u_pre)
    v = _gelu(v_pre)
    mu = jnp.mean(v, axis=-1, keepdims=True)
    var = jnp.mean(jnp.square(v - mu), axis=-1, keepdims=True)
    rstd = lax.rsqrt(var + EPS)
    vhat = (v - mu) * rstd
    vn = (vhat * g_ref[...] + be_ref[...]).astype(BF16)
    mixed = jnp.concatenate(
        [jnp.dot(ws_ref[gi].astype(BF16), vn[:, gi * gw:(gi + 1) * gw], preferred_element_type=F32)
         + bs_ref[:, gi:gi + 1] for gi in range(SGU_GROUPS)], axis=-1)
    return u_pre, v_pre, zc, u, rstd, vhat, vn, mixed


def sgu_fwd(p, ln_g, ln_b, ws, bs_t, *, name):
    n, w3 = p.shape
    w = w3 // 3

    def body(p_ref, g_ref, be_ref, ws_ref, bs_ref, m_ref):
        _, _, zc, u, _, _, _, mixed = _sgu_common(p_ref, g_ref, be_ref, ws_ref, bs_ref, w)
        m_ref[...] = (u * mixed * _silu(zc)).astype(BF16)

    vec = pl.BlockSpec((1, w), lambda i: (0, 0))
    return pl.pallas_call(
        body, name=name, grid=(n // CHUNK,),
        in_specs=[pl.BlockSpec((CHUNK, w3), lambda i: (i, 0)), vec, vec,
                  pl.BlockSpec((SGU_GROUPS, CHUNK, CHUNK), lambda i: (0, 0, 0)),
                  pl.BlockSpec((CHUNK, SGU_GROUPS), lambda i: (0, 0))],
        out_specs=pl.BlockSpec((CHUNK, w), lambda i: (i, 0)),
        out_shape=jax.ShapeDtypeStruct((n, w), BF16),
        compiler_params=_params("parallel"),
    )(p, _row(ln_g), _row(ln_b), ws, bs_t)


def sgu_bwd(p, dm, ln_g, ln_b, ws, ws_t, bs_t, *, name):
    n, w3 = p.shape
    w = w3 // 3
    gw = w // SGU_GROUPS

    def body(p_ref, dm_ref, g_ref, be_ref, ws_ref, wst_ref, bs_ref, dp_ref, dws_ref, dbs_ref, acc_ref):
        i = pl.program_id(0)

        @pl.when(i == 0)
        def _():
            dws_ref[...] = jnp.zeros_like(dws_ref)
            dbs_ref[...] = jnp.zeros_like(dbs_ref)
            acc_ref[...] = jnp.zeros_like(acc_ref)

        u_pre, v_pre, zc, u, rstd, vhat, vn, mixed = _sgu_common(p_ref, g_ref, be_ref, ws_ref, bs_ref, w)
        dmv = dm_ref[...]
        um = u * mixed
        dp_ref[:, 2 * w:3 * w] = (dmv * um * _dsilu(zc)).astype(BF16)
        dum = dmv * _silu(zc)
        dp_ref[:, 0:w] = (dum * mixed * _dgelu(u_pre)).astype(BF16)
        dmixed = dum * u
        dmixed_b = dmixed.astype(BF16)
        dvn_parts = []
        for gi in range(SGU_GROUPS):
            cs = slice(gi * gw, (gi + 1) * gw)
            dws_ref[gi] += lax.dot_general(dmixed_b[:, cs], vn[:, cs], _DIMS["nt"], preferred_element_type=F32)
            dbs_ref[:, gi:gi + 1] += jnp.sum(dmixed[:, cs], axis=-1, keepdims=True)
            dvn_parts.append(jnp.dot(wst_ref[gi].astype(BF16), dmixed_b[:, cs], preferred_element_type=F32))
        dvn = jnp.concatenate(dvn_parts, axis=-1)
        acc_ref[0:1, :] += jnp.sum(dvn * vhat, axis=0, keepdims=True)
        acc_ref[1:2, :] += jnp.sum(dvn, axis=0, keepdims=True)
        dvhat = dvn * g_ref[...]
        dv = rstd * (dvhat - jnp.mean(dvhat, axis=-1, keepdims=True)
                     - vhat * jnp.mean(dvhat * vhat, axis=-1, keepdims=True))
        dp_ref[:, w:2 * w] = (dv * _dgelu(v_pre)).astype(BF16)

    vec = pl.BlockSpec((1, w), lambda i: (0, 0))
    wspec = pl.BlockSpec((SGU_GROUPS, CHUNK, CHUNK), lambda i: (0, 0, 0))
    bspec = pl.BlockSpec((CHUNK, SGU_GROUPS), lambda i: (0, 0))
    return pl.pallas_call(
        body, name=name, grid=(n // CHUNK,),
        in_specs=[pl.BlockSpec((CHUNK, w3), lambda i: (i, 0)), pl.BlockSpec((CHUNK, w), lambda i: (i, 0)),
                  vec, vec, wspec, wspec, bspec],
        out_specs=[pl.BlockSpec((CHUNK, w3), lambda i: (i, 0)), wspec, bspec,
                   pl.BlockSpec((8, w), lambda i: (0, 0))],
        out_shape=[jax.ShapeDtypeStruct((n, w3), BF16), jax.ShapeDtypeStruct((SGU_GROUPS, CHUNK, CHUNK), F32),
                   jax.ShapeDtypeStruct((CHUNK, SGU_GROUPS), F32), jax.ShapeDtypeStruct((8, w), F32)],
        compiler_params=_params("arbitrary"),
    )(p, dm, _row(ln_g), _row(ln_b), ws, ws_t, bs_t)


def _adam_math(w, g, m, v):
    m_new = ADAM_B1 * m + (1.0 - ADAM_B1) * g
    v_new = ADAM_B2 * v + (1.0 - ADAM_B2) * (g * g)
    m_hat = m_new / (1.0 - ADAM_B1 ** ADAM_STEP)
    v_hat = v_new / (1.0 - ADAM_B2 ** ADAM_STEP)
    delta = -ADAM_LR * (m_hat / (jnp.sqrt(v_hat) + ADAM_EPS) + ADAM_WD * w)
    return delta, m_new, v_new


def adamw(w, g, m, v, *, name, slots=False, rows=512):
    r, c = w.shape
    tr = min(r, rows)
    assert r % tr == 0, (name, r, tr)

    def body(w_ref, g_ref, m_ref, v_ref, go_ref, d_ref, mo_ref, vo_ref):
        if slots:
            g = g_ref[0].astype(F32)
            for k in range(1, N_DEV):
                g = g + g_ref[k].astype(F32)
        else:
            g = g_ref[...].astype(F32)
        delta, m_new, v_new = _adam_math(w_ref[...], g, m_ref[...], v_ref[...])
        go_ref[...] = g
        d_ref[...] = delta
        mo_ref[...] = m_new
        vo_ref[...] = v_new

    spec = pl.BlockSpec((tr, c), lambda i: (i, 0))
    gspec = pl.BlockSpec((N_DEV, tr, c), lambda i: (0, i, 0)) if slots else spec
    shape = jax.ShapeDtypeStruct((r, c), F32)
    return pl.pallas_call(
        body, name=name, grid=(r // tr,),
        in_specs=[spec, gspec, spec, spec], out_specs=[spec] * 4, out_shape=[shape] * 4,
        compiler_params=_params("parallel"),
    )(w, g, m, v)


def slot_sum(slots, *, name):
    _, r, c = slots.shape

    def body(s_ref, o_ref):
        acc = s_ref[0]
        for k in range(1, N_DEV):
            acc = acc + s_ref[k]
        o_ref[...] = acc

    return pl.pallas_call(
        body, name=name, out_shape=jax.ShapeDtypeStruct((r, c), F32),
        in_specs=[VMEM_SPEC], out_specs=VMEM_SPEC,
        compiler_params=pltpu.CompilerParams(vmem_limit_bytes=VMEM_LIMIT),
    )(slots)


def _after(token, val):
    return val + token[0, 0]


def _zero_of(v):
    bits = lax.bitcast_convert_type(v, jnp.uint16 if v.dtype == BF16 else jnp.uint32)
    return jnp.where((bits | 1) == 0, 1.0, 0.0).astype(F32)


def _pad_rows(a, rows):
    return jnp.pad(a, ((0, rows - a.shape[0]), (0, 0)))


def kernel(x, c, ctx, c_ctx, ada_w, ada_b, norm_g, ev_w_in, ev_q_norm, ev_k_norm, ev_dw_w, ev_dw_b, ev_ln_g, ev_ln_b, ev_w_out, od_w_in, od_ln_g, od_ln_b, od_ws, od_bs, od_w_out, final_g, loss_target, m_c_ctx, m_ada_w, m_ada_b, m_norm_g, m_ev_w_in, m_ev_q_norm, m_ev_k_norm, m_ev_dw_w, m_ev_dw_b, m_ev_ln_g, m_ev_ln_b, m_ev_w_out, m_od_w_in, m_od_ln_g, m_od_ln_b, m_od_ws, m_od_bs, m_od_w_out, m_final_g, v_c_ctx, v_ada_w, v_ada_b, v_norm_g, v_ev_w_in, v_ev_q_norm, v_ev_k_norm, v_ev_dw_w, v_ev_dw_b, v_ev_ln_g, v_ev_ln_b, v_ev_w_out, v_od_w_in, v_od_ln_g, v_od_ln_b, v_od_ws, v_od_bs, v_od_w_out, v_final_g):
    n, d = x.shape[1], x.shape[2]
    lc = ctx.shape[1]
    ev_in = ev_w_in.shape[2] * N_DEV
    od_in = od_w_in.shape[2] * N_DEV
    conv_ch = ev_dw_w.shape[2] * N_DEV
    ada_cols = ada_w.shape[2]
    me = 4 * lax.axis_index("x") + 2 * lax.axis_index("y") + lax.axis_index("c")
    xs, tgt, ctxs = x[0], loss_target[0], ctx[0]
    za_block = (2 * KV_W + ATTN_W) // HEAD_DIM
    glu_block = (2 * KV_W + 2 * ATTN_W) // HALF
    zb_block = glu_block + 2 * conv_ch // HALF

    small = jnp.concatenate([
        jax.nn.silu(c).reshape(1, d),
        od_ln_g.reshape(1, -1), od_ln_b.reshape(1, -1)], axis=1)
    small = _pad_rows(small, 8)
    dw_rows = _pad_rows(ev_dw_w[0], 32)
    small_g, dw_g = all_gather([small, dw_rows], name="gather_small")
    sc_all = small_g[:, 0, :d]
    shard = d // N_DEV
    od_ln_g_full = small_g[:, 0, d:d + shard].reshape(d)
    od_ln_b_full = small_g[:, 0, d + shard:d + 2 * shard].reshape(d)
    dw_w_full = jnp.moveaxis(dw_g, 0, 1).reshape(32, conv_ch)[:CONV_WIDTH]
    scc = jax.nn.silu(c_ctx)
    sc16 = _pad_rows(jnp.concatenate([sc_all, scc.reshape(1, d)], axis=0), 16)

    ada_bf = ada_w.astype(BF16)
    mod_loc = [matmul(sc16, ada_bf[l], mode="nn", tm=16, tn=ada_cols, tk=d, out_dtype=F32, name=f"ada_mod{l}")
               for l in range(2)]
    (mod_g,) = all_gather([jnp.stack(mod_loc)], name="gather_mod")
    mod_all = jnp.moveaxis(mod_g, 0, 2).reshape(2, 16, N_DEV * ada_cols) + ada_b[:, None, :]
    mod_me = lax.dynamic_index_in_dim(mod_all, me, axis=1, keepdims=False)
    shift = [mod_me[l, :d] for l in range(2)]
    scale = [mod_me[l, d:2 * d] for l in range(2)]
    gate = [mod_me[l, 2 * d:] for l in range(2)]
    shift_c, scale_c = mod_all[0, 8, :d], mod_all[0, 8, d:2 * d]

    h_w_in0, near_token = staged_gather_start(1, (ev_w_in[0] + _zero_of(mod_g[0, 0, 0, 0])).astype(BF16), None,
                                              name="gather_w_in0_chips_start")

    cexp, sexp = rope_tables(n)
    h0 = norm_mod_fwd(xs, norm_g[0], _after(near_token, shift[0]), scale[0], name="norm_mod_fwd0")
    hc = norm_mod_fwd(ctxs, norm_g[0], shift_c, scale_c, name="norm_mod_fwd_ctx")

    wi0_shard, wi0_land = staged_gather_wait(1, h_w_in0, h0, name="gather_w_in0_chips_wait")
    h_w_in0, far_token = staged_gather_start(2, wi0_shard, wi0_land, name="gather_w_in0_cores_start")
    wi0_shard, wi0_land = staged_gather_wait(2, h_w_in0, far_token, name="gather_w_in0_cores_wait")
    wi0_g = lax.dynamic_update_slice_in_dim(wi0_land, wi0_shard[None], me, axis=0)
    wi0 = jnp.moveaxis(wi0_g, 0, 1).reshape(d, ev_in)
    later = [w[0].astype(BF16) for w in (ev_w_out, od_w_in, od_w_out)]
    (h_wo0, h_wi1, h_wo1), w_token = exchange_start(later, scatter=False, name="gather_rest_start", after=wi0_g)

    def landed(handle, after, name):
        own, land = exchange_wait(handle, after, scatter=False, name=name)
        return lax.dynamic_update_slice_in_dim(land, own[None], me, axis=0)

    p0 = matmul(h0, wi0, mode="nn", tm=1024, tn=ev_in // 4, tk=d, out_dtype=F32, name="proj_in0", after=w_token)
    pc = matmul(hc, wi0, mode="nn", tm=lc, tn=2 * KV_W, tk=d, out_dtype=F32, n_out=2 * KV_W, name="proj_in_ctx")
    q_r, k_all, v_ext = qkv_prep_fwd(p0, ev_q_norm[0], ev_k_norm[0], cexp, sexp, latent=True, name="qkv_prep",
                                     rows_all=n + lc)
    k_all, v_ext = qkv_prep_fwd(pc, None, ev_k_norm[0], None, None, latent=False, name="kv_prep_ctx",
                                rows_all=n + lc, k_all=k_all, v_ext=v_ext)
    o_attn, mix_a, lse = flash_fwd(q_r, k_all, v_ext, p0, za_block=za_block, mix_width=ATTN_W + conv_ch,
                                   name="flash_fwd")
    ycv, mix0 = conv_fwd(p0, dw_w_full, ev_dw_b[0], ev_ln_g[0], ev_ln_b[0], mix_a, glu_block=glu_block,
                         name="conv_fwd")
    wo0 = landed(h_wo0, ycv, "gather_w_out0_wait").reshape(-1, d)
    x1, r0, h1 = matmul(mix0, wo0, mode="nn", tm=512, tn=d, tk=mix0.shape[1], out_dtype=F32, name="proj_out0",
                        res=xs, gate=_row(gate[0]), norm=(norm_g[1], shift[1], scale[1]))

    wi1 =jnp.moveaxis(landed(h_wi1, h1, "gather_w_in1_wait"), 0, 1).reshape(d, od_in)
    p1 = matmul(h1, wi1, mode="nn", tm=1024, tn=od_in // 4, tk=d, out_dtype=F32, name="proj_in1")
    ws_bf = od_ws[0]
    bs_t = od_bs[0].T
    m1 = sgu_fwd(p1, od_ln_g_full, od_ln_b_full, ws_bf, bs_t, name="sgu_fwd")
    wo1 = landed(h_wo1, m1, "gather_w_out1_wait").reshape(-1, d)
    dx2, dr1, acc_final, loss_tile = proj_out_loss(m1, wo1, x1, gate[1], tgt, final_g, name="proj_out1_loss")

    dm1 = matmul(dr1, wo1, mode="nt", tm=1024, tn=1024, tk=d, out_dtype=F32, name="d_mix1")
    dwo1 = matmul(m1, dr1, mode="tn", tm=1024, tn=1024, tk=2048, out_dtype=BF16, name="d_wout1")
    dp1, dws, dbs_t, acc_sgu = sgu_bwd(p1, dm1, od_ln_g_full, od_ln_b_full, ws_bf, jnp.swapaxes(ws_bf, 1, 2), bs_t,
                                       name="sgu_bwd")
    dwi1_s = matmul(h1, dp1, mode="tn", tm=2048, tn=od_in // N_DEV, tk=2048, out_dtype=BF16, name="d_win1",
                    split_out=True)
    dwo1_s = dwo1.reshape(N_DEV, -1, d)
    (h_gi1, h_go1), g1_token = exchange_start([dwi1_s, dwo1_s], scatter=True, name="grads1_start")
    dh1 = matmul(dp1, wi1, mode="nt", tm=1024, tn=512, tk=od_in, out_dtype=F32, name="d_h1")
    dx1, dr0, acc_norm1 = norm_mod_bwd(x1, dh1, dx2, norm_g[1], _after(g1_token, scale[1]), name="norm_mod_bwd1",
                                       branch=(r0, gate[0]))

    dmix0 = matmul(dr0, wo0, mode="nt", tm=1024, tn=1024, tk=d, out_dtype=F32, name="d_mix0")
    dwo0 = matmul(mix0, dr0, mode="tn", tm=1024, tn=1024, tk=2048, out_dtype=BF16, name="d_wout0")
    do_attn, do_s, dza, delta = attn_gate_bwd(dmix0, o_attn, p0, za_block=za_block, name="attn_gate_bwd")
    dycv, dzb, acc_ln = conv_bwd_rows(dmix0, ycv, p0, ev_ln_g[0], ev_ln_b[0], mix_block=ATTN_W // conv_ch,
                                      zb_block=zb_block, name="conv_bwd_rows")
    dglu, ddw_w, acc_dwb = conv_bwd_taps(dycv, p0, dw_w_full, glu_block=glu_block, name="conv_bwd_taps")
    dq_r, dk_all, dv_all = flash_bwd(q_r, do_attn, do_s, lse, delta, k_all, v_ext, name="flash_bwd")
    dkvq, acc_qk = qkv_prep_bwd(p0, dq_r, dk_all, dv_all, ev_q_norm[0], ev_k_norm[0], cexp, sexp,
                                latent=True, name="qkv_prep_bwd")
    dpc, acc_kc = qkv_prep_bwd(pc, None, dk_all, dv_all, None, ev_k_norm[0], None, None,
                               latent=False, name="kv_prep_ctx_bwd")
    dp0 = jnp.concatenate([dkvq, dza, dglu, dzb], axis=1)
    dwi0 = matmul(h0, dp0, mode="tn", tm=1024, tn=ev_in // 4, tk=2048, out_dtype=F32, name="d_win0")
    dwi0 = matmul(hc, dpc, mode="tn", tm=512, tn=2 * KV_W, tk=lc, out_dtype=F32, name="d_win0_ctx", add=dwi0)
    dwi0_s = jnp.moveaxis(dwi0.astype(BF16).reshape(d, N_DEV, ev_in // N_DEV), 1, 0)
    dwo0_s = dwo0.reshape(N_DEV, -1, d)
    (h_gi0, h_go0), g0_token = exchange_start([dwi0_s, dwo0_s], scatter=True, name="grads0_start")
    dh0 = matmul(dp0, wi0, mode="nt", tm=1024, tn=512, tk=ev_in, out_dtype=F32, name="d_h0", after=g0_token)
    dhc = matmul(dpc, wi0, mode="nt", tm=lc, tn=512, tk=2 * KV_W, out_dtype=F32, name="d_h_ctx")
    grad_x, acc_norm0 = norm_mod_bwd(xs, dh0, dx1, norm_g[0], _after(g0_token, scale[0]), name="norm_mod_bwd0")
    _, acc_normc = norm_mod_bwd(ctxs, dhc, None, norm_g[0], scale_c, name="norm_mod_bwd_ctx")

    zeros_d = jnp.zeros((d,), F32)
    dmod0 = jnp.stack([acc_norm0[0], acc_norm0[1], acc_norm1[GATE_ROW]])
    dmod1 = jnp.stack([acc_norm1[0], acc_norm1[1], acc_final[GATE_ROW]])
    dmodc = jnp.stack([acc_normc[0], acc_normc[1]])
    half_pad = jnp.zeros((d - 2 * conv_ch,), F32) if d > 2 * conv_ch else jnp.zeros((0,), F32)
    row_a = jnp.concatenate([acc_dwb[0], acc_ln[0], half_pad])
    row_b = jnp.concatenate([acc_ln[1], acc_qk[0], acc_qk[1] + acc_kc[1],
                             jnp.zeros((d - conv_ch - 2 * HEAD_DIM,), F32)])
    row_c = jnp.concatenate([dbs_t.T.reshape(-1), jnp.zeros((d - SGU_GROUPS * CHUNK,), F32)])
    row_loss = jnp.concatenate([loss_tile[0, :1], jnp.zeros((d - 1,), F32)])
    pack = jnp.concatenate([
        dmod0, dmod1, dmodc,
        (acc_norm0[2] + acc_normc[2])[None], acc_norm1[2][None],
        acc_final[0][None],
        acc_sgu[0][None], acc_sgu[1][None],
        row_a[None], row_b[None], row_c[None], row_loss[None],
        ddw_w.reshape(-1, d),
        dws.reshape(-1, d),
    ], axis=0)
    n_rows = pack.shape[0]
    pack = _pad_rows(pack, -(-n_rows // 8) * 8)
    (h_pack,), pack_token = exchange_start([pack], scatter=False, name="small_grads_start")

    def summands(handle, after, name):
        mine, land = exchange_wait(handle, after, scatter=True, name=name)
        own = lax.dynamic_index_in_dim(mine, me, axis=0, keepdims=True)
        return lax.dynamic_update_slice_in_dim(land, own, me, axis=0)

    out = {}

    def upd(key, w, g, m, v, slots=False, rows=512):
        shp = w.shape
        w2 = w.reshape(-1, shp[-1])
        g2 = g.reshape((N_DEV, -1, shp[-1])) if slots else g.reshape(-1, shp[-1])
        res = adamw(w2, g2, m.reshape(w2.shape), v.reshape(w2.shape), name="adamw_" + key, slots=slots, rows=rows)
        out[key] = tuple(r.reshape(shp) for r in res)

    gi1 = summands(h_gi1, pack_token, "grads_w_in1_wait")
    upd("od_w_in", od_w_in, gi1, m_od_w_in, v_od_w_in, slots=True, rows=256)
    go1 = summands(h_go1, out["od_w_in"][1], "grads_w_out1_wait")
    upd("od_w_out", od_w_out, go1, m_od_w_out, v_od_w_out, slots=True, rows=256)
    go0 = summands(h_go0, out["od_w_out"][1], "grads_w_out0_wait")
    upd("ev_w_out", ev_w_out, go0, m_ev_w_out, v_ev_w_out, slots=True, rows=256)
    gi0 = summands(h_gi0, out["ev_w_out"][1], "grads_w_in0_wait")
    upd("ev_w_in", ev_w_in, gi0, m_ev_w_in, v_ev_w_in, slots=True, rows=256)

    pack_own, pack_land = exchange_wait(h_pack, out["ev_w_in"][1], scatter=False, name="small_grads_wait")
    pack_g = lax.dynamic_update_slice_in_dim(pack_land, pack_own[None], me, axis=0)
    gsum = slot_sum(pack_g, name="sum_small_grads")
    loss = gsum[16, 0]
    dw_rows_n = 32 * conv_ch // d
    g_dw_w = gsum[17:17 + dw_rows_n].reshape(32, conv_ch)[:CONV_WIDTH]
    g_od_ws = gsum[17 + dw_rows_n:17 + dw_rows_n + SGU_GROUPS * CHUNK * CHUNK // d].reshape(od_ws.shape)

    dmodc_sum = jnp.concatenate([gsum[6], gsum[7], zeros_d])
    col0 = me * ada_cols
    dm_cols = []
    for l in range(2):
        rows = pack_g[:, 3 * l:3 * l + 3, :].reshape(N_DEV, 3 * d)
        extra = dmodc_sum[None] if l == 0 else jnp.zeros((1, 3 * d), F32)
        full = _pad_rows(jnp.concatenate([rows, extra], axis=0), 16)
        dm_cols.append(lax.dynamic_slice_in_dim(full, col0, ada_cols, axis=1))
    g_ada_w = jnp.stack([matmul(sc16, dm_cols[l], mode="tn", tm=512, tn=ada_cols, tk=16, out_dtype=F32,
                                name=f"d_ada_w{l}") for l in range(2)])
    dsc = matmul(dm_cols[0], ada_bf[0], mode="nt", tm=16, tn=512, tk=ada_cols, out_dtype=F32, name="d_scc")
    (_, dscc_sum) = all_gather([dsc[8:16]], name="gather_dscc", sum_out=True)
    sg = jax.nn.sigmoid(c_ctx)
    g_c_ctx = dscc_sum[0] * (sg * (1.0 + c_ctx * (1.0 - sg)))
    g_ada_b = jnp.stack([gsum[0:3].reshape(-1) + dmodc_sum, gsum[3:6].reshape(-1)])

    upd("ada_w", ada_w, g_ada_w, m_ada_w, v_ada_w)

    def my_shard(full, size):
        return lax.dynamic_slice_in_dim(full, me * size, size, axis=full.ndim - 1)

    small_items = [
        ("c_ctx", c_ctx, g_c_ctx, m_c_ctx, v_c_ctx),
        ("ada_b", ada_b, g_ada_b, m_ada_b, v_ada_b),
        ("norm_g", norm_g, gsum[8:10], m_norm_g, v_norm_g),
        ("ev_q_norm", ev_q_norm, gsum[14, conv_ch:conv_ch + HEAD_DIM], m_ev_q_norm, v_ev_q_norm),
        ("ev_k_norm", ev_k_norm, gsum[14, conv_ch + HEAD_DIM:conv_ch + 2 * HEAD_DIM], m_ev_k_norm, v_ev_k_norm),
        ("ev_dw_w", ev_dw_w, my_shard(g_dw_w, conv_ch // N_DEV), m_ev_dw_w, v_ev_dw_w),
        ("ev_dw_b", ev_dw_b, gsum[13, :conv_ch], m_ev_dw_b, v_ev_dw_b),
        ("ev_ln_g", ev_ln_g, gsum[13, conv_ch:2 * conv_ch], m_ev_ln_g, v_ev_ln_g),
        ("ev_ln_b", ev_ln_b, gsum[14, :conv_ch], m_ev_ln_b, v_ev_ln_b),
        ("od_ln_g", od_ln_g, my_shard(gsum[11], shard), m_od_ln_g, v_od_ln_g),
        ("od_ln_b", od_ln_b, my_shard(gsum[12], shard), m_od_ln_b, v_od_ln_b),
        ("od_ws", od_ws, g_od_ws, m_od_ws, v_od_ws),
        ("od_bs", od_bs, gsum[15, :SGU_GROUPS * CHUNK], m_od_bs, v_od_bs),
        ("final_g", final_g, gsum[10], m_final_g, v_final_g),
    ]
    sizes = [it[1].size for it in small_items]
    total = sum(sizes)
    lanes = 1024
    prow = -(-total // lanes)
    prow = -(-prow // 8) * 8

    def pack_small(idx):
        flat = jnp.concatenate([it[idx].reshape(-1).astype(F32) for it in small_items])
        return jnp.pad(flat, (0, prow * lanes - total)).reshape(prow, lanes)

    sres = adamw(pack_small(1), pack_small(2), pack_small(3), pack_small(4), name="adamw_small", rows=prow)
    off = 0
    for it, size in zip(small_items, sizes):
        out[it[0]] = tuple(r.reshape(-1)[off:off + size].reshape(it[1].shape) for r in sres)
        off += size

    names = ['c_ctx', 'ada_w', 'ada_b', 'norm_g', 'ev_w_in', 'ev_q_norm', 'ev_k_norm', 'ev_dw_w', 'ev_dw_b',
             'ev_ln_g', 'ev_ln_b', 'ev_w_out', 'od_w_in', 'od_ln_g', 'od_ln_b', 'od_ws', 'od_bs', 'od_w_out',
             'final_g']
    return (loss, grad_x[None], *[out[k][0] for k in names], *[out[k][1] for k in names],
            *[out[k][2] for k in names], *[out[k][3] for k in names])
```

```python
import functools
import math

import jax
import jax.numpy as jnp
from jax import lax
from jax.experimental import pallas as pl
from jax.experimental.pallas import tpu as pltpu

F32 = jnp.float32
BF16 = jnp.bfloat16
MESH = pl.DeviceIdType.MESH

EPS = 1e-6
HEAD_DIM = 128
N_Q_HEADS = 8
N_KV_HEADS = 2
Q_PER_KV = N_Q_HEADS // N_KV_HEADS
ATTN_W = N_Q_HEADS * HEAD_DIM
KV_W = N_KV_HEADS * HEAD_DIM
ATTN_SCALE = HEAD_DIM ** -0.5
LN2 = math.log(2.0)
Q_SCALE = ATTN_SCALE / LN2
ROPE_THETA = 10000.0
GRID_W = 64
CONV_WIDTH = 31
HALO = 16
CHUNK = 128
SGU_GROUPS = 8
N_DEV = 8

ADAM_LR = 0.001
ADAM_B1 = 0.9
ADAM_B2 = 0.999
ADAM_EPS = 1e-08
ADAM_WD = 0.01
ADAM_STEP = 10

VMEM_LIMIT = 56 * 1024 * 1024
ANY = pl.BlockSpec(memory_space=pl.ANY)
VMEM_SPEC = pl.BlockSpec(memory_space=pltpu.VMEM)


def _params(*sem):
    return pltpu.CompilerParams(dimension_semantics=sem, vmem_limit_bytes=VMEM_LIMIT)


def _sigmoid(x):
    return 1.0 / (1.0 + jnp.exp(-x))


def _silu(x):
    return x * _sigmoid(x)


def _dsilu(x):
    s = _sigmoid(x)
    return s * (1.0 + x * (1.0 - s))


_GELU_C = math.sqrt(2.0 / math.pi)


def _gelu(x):
    t = jnp.tanh(_GELU_C * (x + 0.044715 * (x * x * x)))
    return 0.5 * x * (1.0 + t)


def _dgelu(x):
    t = jnp.tanh(_GELU_C * (x + 0.044715 * (x * x * x)))
    return 0.5 * (1.0 + t) + 0.5 * x * (1.0 - t * t) * (_GELU_C * (1.0 + 3.0 * 0.044715 * (x * x)))


def _row(v):
    return v.reshape(1, -1).astype(F32)


def _flat_id(p):
    return 4 * p[0] + 2 * p[1] + p[2]


def _gather_body(n_arr, sum_out):
    def body(*refs):
        x_refs = refs[:n_arr]
        out_refs = refs[n_arr:2 * n_arr]
        pos = 2 * n_arr
        sum_refs = refs[pos:pos + n_arr] if sum_out else ()
        pos += n_arr if sum_out else 0
        send_sems, recv_sems, local_sems = refs[pos:pos + 3]
        x, y, c = lax.axis_index("x"), lax.axis_index("y"), lax.axis_index("c")
        me, sibling = (x, y, c), (x, y, 1 - c)
        chips = [(1 - x, y), (x, 1 - y), (1 - x, 1 - y)]

        def copy(a, k, block, to, src=None):
            rows = out_refs[a].at[_flat_id(block)]
            return pltpu.make_async_remote_copy(
                src_ref=rows if src is None else src, dst_ref=rows,
                send_sem=send_sems.at[a, k], recv_sem=recv_sems.at[a, k],
                device_id=to, device_id_type=MESH)

        sends = []
        mine = []
        for a in range(n_arr):
            cp = pltpu.make_async_copy(x_refs[a], out_refs[a].at[_flat_id(me)], local_sems.at[a])
            cp.start()
            mine.append(cp)
            first = [copy(a, 0, me, sibling, src=x_refs[a])]
            first += [copy(a, 1 + j, me, (*chip, c), src=x_refs[a]) for j, chip in enumerate(chips)]
            for cp in first:
                cp.start()
            sends += first
        for a in range(n_arr):
            for j, chip in enumerate(chips):
                copy(a, 1 + j, (*chip, c), me).wait_recv()
                fwd = copy(a, 4 + j, (*chip, c), sibling)
                fwd.start()
                sends.append(fwd)
        for a in range(n_arr):
            copy(a, 0, sibling, me).wait_recv()
            for j, chip in enumerate(chips):
                copy(a, 4 + j, (*chip, 1 - c), me).wait_recv()
        for cp in sends:
            cp.wait_send()
        for cp in mine:
            cp.wait()
        if sum_out:
            for a in range(n_arr):
                acc = out_refs[a][0]
                for k in range(1, N_DEV):
                    acc = acc + out_refs[a][k]
                sum_refs[a][...] = acc

    return body


def all_gather(arrs, *, name, sum_out=False):
    n = len(arrs)
    spec = VMEM_SPEC
    out_shape = [jax.ShapeDtypeStruct((N_DEV,) + a.shape, a.dtype) for a in arrs]
    out_specs = [spec] * n
    if sum_out:
        out_shape += [jax.ShapeDtypeStruct(a.shape, a.dtype) for a in arrs]
        out_specs += [VMEM_SPEC] * n
    res = pl.pallas_call(
        _gather_body(n, sum_out), name=name,
        out_shape=out_shape, in_specs=[spec] * n, out_specs=out_specs,
        scratch_shapes=[pltpu.SemaphoreType.DMA((n, 7)), pltpu.SemaphoreType.DMA((n, 7)),
                        pltpu.SemaphoreType.DMA((n,))],
        compiler_params=pltpu.CompilerParams(vmem_limit_bytes=VMEM_LIMIT),
    )(*arrs)
    return res


_RELATIONS = [(rx, ry, rc) for rx in (0, 1) for ry in (0, 1) for rc in (0, 1)][1:]


HBM_SPEC = pl.BlockSpec(memory_space=pltpu.HBM)
SEM_SPEC = pl.BlockSpec(memory_space=pltpu.SEMAPHORE)
_DATAFLOW = pltpu.SideEffectType.DATAFLOW_SIDE_EFFECTING
N_PEERS = N_DEV - 1


def _peer_copy(src_ref, land_ref, send_sem, recv_sem, k, rel, scatter, sending):
    x, y, c = lax.axis_index("x"), lax.axis_index("y"), lax.axis_index("c")
    rx, ry, rc = rel
    peer = (1 - x if rx else x, 1 - y if ry else y, 1 - c if rc else c)
    src = src_ref.at[_flat_id(peer)] if scatter else src_ref
    dst = land_ref.at[_flat_id((x, y, c)) if sending else _flat_id(peer)]
    return pltpu.make_async_remote_copy(src_ref=src, dst_ref=dst, send_sem=send_sem.at[k], recv_sem=recv_sem.at[k],
                                        device_id=peer, device_id_type=MESH)


def exchange_start(arrs, *, scatter, name, after=None):
    n = len(arrs)
    lands = [lax.empty((N_DEV,) + (a.shape[1:] if scatter else a.shape), a.dtype) for a in arrs]
    n_in = 2 * n + (after is not None)

    def body(*refs):
        srcs, lnds, sems = refs[:n], refs[n:2 * n], refs[n_in:n_in + 2 * n]
        token = refs[n_in + 4 * n]
        for a in range(n):
            for k, rel in enumerate(_RELATIONS):
                _peer_copy(srcs[a], lnds[a], sems[2 * a], sems[2 * a + 1], k, rel, scatter, True).start()
        token[...] = jnp.zeros_like(token)

    outs = pl.pallas_call(
        body, name=name,
        out_shape=[pltpu.SemaphoreType.DMA((N_PEERS,))] * (2 * n)
        + [pltpu.HBM(a.shape, a.dtype) for a in arrs] + [pltpu.HBM(l.shape, l.dtype) for l in lands]
        + [jax.ShapeDtypeStruct((8, 128), F32)],
        in_specs=[HBM_SPEC] * (2 * n) + ([ANY] if after is not None else []),
        out_specs=[SEM_SPEC] * (2 * n) + [HBM_SPEC] * (2 * n) + [VMEM_SPEC],
        input_output_aliases={i: 2 * n + i for i in range(2 * n)},
        compiler_params=pltpu.CompilerParams(has_side_effects=_DATAFLOW),
    )(*[pltpu.with_memory_space_constraint(a, pltpu.HBM) for a in arrs],
      *[pltpu.with_memory_space_constraint(l, pltpu.HBM) for l in lands],
      *([after] if after is not None else []))
    handles = [(outs[2 * a], outs[2 * a + 1], outs[2 * n + a], outs[3 * n + a]) for a in range(n)]
    return handles, outs[4 * n]


def exchange_wait(handle, after, *, scatter, name):
    send_sem, recv_sem, src, land = handle

    def body(src_ref, land_ref, send_ref, recv_ref, after_ref, src_out, land_out):
        for k, rel in enumerate(_RELATIONS):
            cp = _peer_copy(src_ref, land_ref, send_ref, recv_ref, k, rel, scatter, False)
            cp.wait_send()
            cp.wait_recv()

    outs = pl.pallas_call(
        body, name=name,
        out_shape=[pltpu.HBM(src.shape, src.dtype), pltpu.HBM(land.shape, land.dtype)],
        in_specs=[HBM_SPEC, HBM_SPEC, SEM_SPEC, SEM_SPEC, ANY],
        out_specs=[HBM_SPEC, HBM_SPEC], input_output_aliases={0: 0, 1: 1},
        compiler_params=pltpu.CompilerParams(has_side_effects=_DATAFLOW),
    )(src, land, send_sem, recv_sem, after)
    return outs[0], outs[1]


_STAGE1 = [(0, 0, 1), (1, 0, 0), (0, 1, 0), (1, 1, 0)]
_OTHER_CHIPS = [(1, 0), (0, 1), (1, 1)]


def _stage_copy(stage, k, shard_ref, land_ref, send_sem, recv_sem, sending):
    x, y, c = lax.axis_index("x"), lax.axis_index("y"), lax.axis_index("c")
    if stage == 1:
        rx, ry, rc = _STAGE1[k]
        peer = (1 - x if rx else x, 1 - y if ry else y, 1 - c if rc else c)
        src = shard_ref
        dst = land_ref.at[_flat_id((x, y, c)) if sending else _flat_id(peer)]
    else:
        cx, cy = _OTHER_CHIPS[k]
        peer = (x, y, 1 - c)
        chip = (1 - x if cx else x, 1 - y if cy else y)
        slot = _flat_id((*chip, c)) if sending else _flat_id((*chip, 1 - c))
        src = land_ref.at[_flat_id((*chip, c))]
        dst = land_ref.at[slot]
    return pltpu.make_async_remote_copy(src_ref=src, dst_ref=dst, send_sem=send_sem.at[k], recv_sem=recv_sem.at[k],
                                        device_id=peer, device_id_type=MESH)


def staged_gather_start(stage, shard, land, *, name):
    n_copies = len(_STAGE1) if stage == 1 else len(_OTHER_CHIPS)
    if land is None:
        land = lax.empty((N_DEV,) + shard.shape, shard.dtype)

    def body(shard_ref, land_ref, send_sem, recv_sem, shard_thru, land_thru, token):
        for k in range(n_copies):
            _stage_copy(stage, k, shard_ref, land_ref, send_sem, recv_sem, True).start()
        token[...] = jnp.zeros_like(token)

    outs = pl.pallas_call(
        body, name=name,
        out_shape=[pltpu.SemaphoreType.DMA((n_copies,)), pltpu.SemaphoreType.DMA((n_copies,)),
                   pltpu.HBM(shard.shape, shard.dtype), pltpu.HBM(land.shape, land.dtype),
                   jax.ShapeDtypeStruct((8, 128), F32)],
        in_specs=[HBM_SPEC, HBM_SPEC],
        out_specs=[SEM_SPEC, SEM_SPEC, HBM_SPEC, HBM_SPEC, VMEM_SPEC],
        input_output_aliases={0: 2, 1: 3},
        compiler_params=pltpu.CompilerParams(has_side_effects=_DATAFLOW),
    )(pltpu.with_memory_space_constraint(shard, pltpu.HBM), pltpu.with_memory_space_constraint(land, pltpu.HBM))
    return outs[:4], outs[4]


def staged_gather_wait(stage, handle, after, *, name):
    send_sem, recv_sem, shard, land = handle
    n_copies = len(_STAGE1) if stage == 1 else len(_OTHER_CHIPS)

    def body(shard_ref, land_ref, send_ref, recv_ref, after_ref, shard_out, land_out):
        for k in range(n_copies):
            cp = _stage_copy(stage, k, shard_ref, land_ref, send_ref, recv_ref, False)
            cp.wait_send()
            cp.wait_recv()

    outs = pl.pallas_call(
        body, name=name,
        out_shape=[pltpu.HBM(shard.shape, shard.dtype), pltpu.HBM(land.shape, land.dtype)],
        in_specs=[HBM_SPEC, HBM_SPEC, SEM_SPEC, SEM_SPEC, ANY],
        out_specs=[HBM_SPEC, HBM_SPEC], input_output_aliases={0: 0, 1: 1},
        compiler_params=pltpu.CompilerParams(has_side_effects=_DATAFLOW),
    )(shard, land, send_sem, recv_sem, after)
    return outs[0], outs[1]


_DIMS = {"nn": (((1,), (0,)), ((), ())), "nt": (((1,), (1,)), ((), ())), "tn": (((0,), (0,)), ((), ()))}


def matmul(a, b, *, mode, tm, tn, tk, out_dtype, name, n_out=None, res=None, gate=None, add=None, after=None,
           split_out=False, norm=None):
    if mode == "tn":
        kdim, m = a.shape
    else:
        m, kdim = a.shape
    nfull = b.shape[0] if mode == "nt" else b.shape[1]
    n = nfull if n_out is None else n_out
    tm, tn, tk = min(tm, m), min(tn, n), min(tk, kdim)
    assert m % tm == 0 and n % tn == 0 and kdim % tk == 0, (name, m, n, kdim, tm, tn, tk)
    nk = kdim // tk
    dims = _DIMS[mode]
    a_spec = pl.BlockSpec((tk, tm), lambda j, i, k: (k, i)) if mode == "tn" else pl.BlockSpec((tm, tk), lambda j, i, k: (i, k))
    b_spec = pl.BlockSpec((tn, tk), lambda j, i, k: (j, k)) if mode == "nt" else pl.BlockSpec((tk, tn), lambda j, i, k: (k, j))
    o_spec = pl.BlockSpec((tm, tn), lambda j, i, k: (i, j))
    in_specs = [a_spec, b_spec]
    operands = [a, b]
    aliases = {}
    vec = pl.BlockSpec((1, tn), lambda j, i, k: (0, j))
    if res is not None:
        in_specs += [o_spec, vec]
        operands += [res, gate]
    if norm is not None:
        assert res is not None and tn == n, (name, tn, n)
        in_specs += [vec, vec, vec]
        operands += [_row(v) for v in norm]
    if add is not None:
        in_specs += [o_spec]
        aliases = {len(operands): 0}
        operands += [add]
    if after is not None:
        in_specs += [ANY]
        operands += [after]
    out_cols = n if add is None else add.shape[1]
    out_shape = [jax.ShapeDtypeStruct((m, out_cols), out_dtype)]
    out_specs = [o_spec]
    if split_out:
        out_shape = [jax.ShapeDtypeStruct((n // tn, m, tn), out_dtype)]
        out_specs = [pl.BlockSpec((None, tm, tn), lambda j, i, k: (j, i, 0))]
    if res is not None:
        out_shape.append(jax.ShapeDtypeStruct((m, n), BF16))
        out_specs.append(o_spec)
    if norm is not None:
        out_shape.append(jax.ShapeDtypeStruct((m, n), BF16))
        out_specs.append(o_spec)

    def body(*refs):
        a_ref, b_ref = refs[:2]
        pos = 2
        if res is not None:
            res_ref, gate_ref = refs[pos:pos + 2]
            pos += 2
        if norm is not None:
            ng_ref, nshift_ref, nscale_ref = refs[pos:pos + 3]
            pos += 3
        if add is not None:
            add_ref = refs[pos]
            pos += 1
        if after is not None:
            pos += 1
        o_ref = refs[pos]
        pos += 1
        if res is not None:
            r_ref = refs[pos]
            pos += 1
        if norm is not None:
            h_ref = refs[pos]
            pos += 1
        acc_ref = refs[pos] if nk > 1 else None
        prod = lax.dot_general(a_ref[...].astype(BF16), b_ref[...].astype(BF16), dims,
                               preferred_element_type=F32)

        def finish(acc):
            if res is not None:
                xnew = res_ref[...] + gate_ref[...] * acc
                o_ref[...] = xnew.astype(out_dtype)
                r_ref[...] = acc.astype(BF16)
                if norm is not None:
                    rstd = lax.rsqrt(jnp.mean(xnew * xnew, axis=-1, keepdims=True) + EPS)
                    y = xnew * rstd * ng_ref[...]
                    h_ref[...] = (y * (1.0 + nscale_ref[...]) + nshift_ref[...]).astype(BF16)
            elif add is not None:
                o_ref[...] = (add_ref[...] + acc).astype(out_dtype)
            else:
                o_ref[...] = acc.astype(out_dtype)

        if nk == 1:
            finish(prod)
        else:
            k = pl.program_id(2)

            @pl.when(k == 0)
            def _():
                acc_ref[...] = prod

            @pl.when(k > 0)
            def _():
                acc_ref[...] += prod

            @pl.when(k == nk - 1)
            def _():
                finish(acc_ref[...])

    outs = pl.pallas_call(
        body, name=name, grid=(n // tn, m // tm, nk),
        in_specs=in_specs, out_specs=out_specs, out_shape=out_shape,
        scratch_shapes=[pltpu.VMEM((tm, tn), F32)] if nk > 1 else [],
        input_output_aliases=aliases,
        compiler_params=_params("parallel", "parallel", "arbitrary"),
    )(*operands)
    return outs if res is not None else outs[0]


def _rows_tile(n, want):
    return min(n, want)


def norm_mod_fwd(x, g, shift, scale, *, name):
    n, d = x.shape
    tt = _rows_tile(n, 512)

    def body(x_ref, g_ref, sh_ref, sc_ref, h_ref):
        xv = x_ref[...]
        rstd = lax.rsqrt(jnp.mean(xv * xv, axis=-1, keepdims=True) + EPS)
        y = xv * rstd * g_ref[...]
        h_ref[...] = (y * (1.0 + sc_ref[...]) + sh_ref[...]).astype(BF16)

    vec = pl.BlockSpec((1, d), lambda i: (0, 0))
    return pl.pallas_call(
        body, name=name, grid=(n // tt,),
        in_specs=[pl.BlockSpec((tt, d), lambda i: (i, 0)), vec, vec, vec],
        out_specs=pl.BlockSpec((tt, d), lambda i: (i, 0)),
        out_shape=jax.ShapeDtypeStruct((n, d), BF16),
        compiler_params=_params("parallel"),
    )(x, _row(g), _row(shift), _row(scale))


GATE_ROW = 3


def norm_mod_bwd(x, dh, dres, g, scale, *, name, branch=None):
    n, d = x.shape
    tt = _rows_tile(n, 256)
    has_res = dres is not None
    has_branch = branch is not None
    last = n // tt - 1

    def body(*refs):
        x_ref, dh_ref = refs[:2]
        pos = 2
        if has_res:
            dres_ref = refs[pos]
            pos += 1
        if has_branch:
            r_ref, gate_ref = refs[pos:pos + 2]
            pos += 2
        g_ref, sc_ref, dx_ref = refs[pos:pos + 3]
        pos += 3
        if has_branch:
            dr_ref = refs[pos]
            pos += 1
        acc_ref, s_ref = refs[pos:pos + 2]
        i = pl.program_id(0)
        xv = x_ref[...]
        dhv = dh_ref[...]
        rstd = lax.rsqrt(jnp.mean(xv * xv, axis=-1, keepdims=True) + EPS)
        xhat = xv * rstd
        dxhat = dhv * (g_ref[...] * (1.0 + sc_ref[...]))
        dx = rstd * (dxhat - xhat * jnp.mean(dxhat * xhat, axis=-1, keepdims=True))
        if has_res:
            dx = dx + dres_ref[...]
        dx_ref[...] = dx

        @pl.when(i == 0)
        def _():
            s_ref[...] = jnp.zeros_like(s_ref)

        s_ref[0:1, :] += jnp.sum(dhv, axis=0, keepdims=True)
        s_ref[1:2, :] += jnp.sum(dhv * xhat, axis=0, keepdims=True)
        if has_branch:
            dr_ref[...] = (dx * gate_ref[...]).astype(BF16)
            s_ref[2:3, :] += jnp.sum(dx * r_ref[...].astype(F32), axis=0, keepdims=True)

        @pl.when(i == last)
        def _():
            s1 = s_ref[0:1, :]
            s2 = s_ref[1:2, :]
            acc_ref[...] = jnp.zeros_like(acc_ref)
            acc_ref[0:1, :] = s1
            acc_ref[1:2, :] = s2 * g_ref[...]
            acc_ref[2:3, :] = s2 * (1.0 + sc_ref[...])
            acc_ref[GATE_ROW:GATE_ROW + 1, :] = s_ref[2:3, :]

    vec = pl.BlockSpec((1, d), lambda i: (0, 0))
    big = pl.BlockSpec((tt, d), lambda i: (i, 0))
    ops = [x, dh] + ([dres] if has_res else []) + ([branch[0], _row(branch[1])] if has_branch else [])
    ops += [_row(g), _row(scale)]
    return pl.pallas_call(
        body, name=name, grid=(n // tt,),
        in_specs=[big, big] + ([big] if has_res else []) + ([big, vec] if has_branch else []) + [vec, vec],
        out_specs=[big] + ([big] if has_branch else []) + [pl.BlockSpec((8, d), lambda i: (0, 0))],
        out_shape=[jax.ShapeDtypeStruct((n, d), F32)] + ([jax.ShapeDtypeStruct((n, d), BF16)] if has_branch else [])
        + [jax.ShapeDtypeStruct((8, d), F32)],
        scratch_shapes=[pltpu.VMEM((8, d), F32)],
        compiler_params=_params("arbitrary"),
    )(*ops)


def proj_out_loss(a, b, x_prev, gate, target, g, *, name):
    n, kdim = a.shape
    d = b.shape[1]
    tt = _rows_tile(n, 512)
    part_rows = min(tt, 256)

    def body(a_ref, b_ref, x_ref, t_ref, g_ref, gate_ref, dx_ref, dr_ref, acc_ref, loss_ref):
        i = pl.program_id(0)

        @pl.when(i == 0)
        def _():
            acc_ref[...] = jnp.zeros_like(acc_ref)
            loss_ref[...] = jnp.zeros_like(loss_ref)

        for h in range(tt // part_rows):
            rows = pl.ds(h * part_rows, part_rows)
            r = jnp.dot(a_ref[rows, :], b_ref[...], preferred_element_type=F32)
            xv = x_ref[rows, :] + gate_ref[...] * r
            rstd = lax.rsqrt(jnp.mean(xv * xv, axis=-1, keepdims=True) + EPS)
            xhat = xv * rstd
            e = xhat * g_ref[...] - t_ref[rows, :]
            dy = e * (1.0 / d)
            dxhat = dy * g_ref[...]
            dx = rstd * (dxhat - xhat * jnp.mean(dxhat * xhat, axis=-1, keepdims=True))
            dx_ref[rows, :] = dx
            dr_ref[rows, :] = (dx * gate_ref[...]).astype(BF16)
            acc_ref[0:1, :] += jnp.sum(dy * xhat, axis=0, keepdims=True)
            acc_ref[GATE_ROW:GATE_ROW + 1, :] += jnp.sum(dx * r, axis=0, keepdims=True)
            part = 0.5 * jnp.sum(jnp.mean(e * e, axis=-1, keepdims=True), axis=0, keepdims=True)
            loss_ref[...] += jnp.broadcast_to(part, loss_ref.shape)

    big = pl.BlockSpec((tt, d), lambda i: (i, 0))
    vec = pl.BlockSpec((1, d), lambda i: (0, 0))
    return pl.pallas_call(
        body, name=name, grid=(n // tt,),
        in_specs=[pl.BlockSpec((tt, kdim), lambda i: (i, 0)), pl.BlockSpec((kdim, d), lambda i: (0, 0)),
                  big, big, vec, vec],
        out_specs=[big, big, pl.BlockSpec((8, d), lambda i: (0, 0)), pl.BlockSpec((8, 128), lambda i: (0, 0))],
        out_shape=[jax.ShapeDtypeStruct((n, d), F32), jax.ShapeDtypeStruct((n, d), BF16),
                   jax.ShapeDtypeStruct((8, d), F32), jax.ShapeDtypeStruct((8, 128), F32)],
        compiler_params=_params("arbitrary"),
    )(a, b, x_prev, target, _row(g), _row(gate))


def _as_row(col):
    t = col.shape[0]
    return jnp.transpose(jnp.broadcast_to(col, (t, HEAD_DIM)))[0:1, :]


def _swap_pairs(x):
    lane = lax.broadcasted_iota(jnp.int32, x.shape, 1)
    return jnp.where(lane % 2 == 0, pltpu.roll(x, HEAD_DIM - 1, 1), pltpu.roll(x, 1, 1))


def rope_tables(n):
    rows = n // GRID_W
    row = jnp.repeat(jnp.arange(rows, dtype=F32), GRID_W)
    col = jnp.tile(jnp.arange(GRID_W, dtype=F32), rows)
    n_freq = HEAD_DIM // 4
    inv = jnp.power(ROPE_THETA, jnp.arange(n_freq, dtype=F32) * (-2.0 / (HEAD_DIM // 2)))
    ang = jnp.concatenate([row[:, None] * inv, col[:, None] * inv], axis=-1)
    cos, sin = jnp.cos(ang), jnp.sin(ang)
    cexp = jnp.repeat(cos, 2, axis=-1)
    sexp = jnp.stack([-sin, sin], axis=-1).reshape(n, HEAD_DIM)
    return cexp, sexp


V_EXT_W = 2 * HEAD_DIM


def qkv_prep_fwd(p, wq, wk, cexp, sexp, *, latent, name, rows_all=None, k_all=None, v_ext=None):
    n = p.shape[0]
    tt = _rows_tile(n, 1024)
    width = 2 * KV_W + (ATTN_W if latent else 0)

    def body(*refs):
        if latent:
            p_ref, wq_ref, wk_ref, c_ref, s_ref, q_ref, k_ref, v_ref = refs
        else:
            p_ref, wk_ref, _, _, k_ref, v_ref = refs

        def head(xv, w):
            rstd = lax.rsqrt(jnp.mean(xv * xv, axis=-1, keepdims=True) + EPS)
            yv = xv * rstd * w
            if latent:
                yv = yv * c_ref[...] + _swap_pairs(yv) * s_ref[...]
            return yv

        for h in range(N_KV_HEADS):
            sl = slice(h * HEAD_DIM, (h + 1) * HEAD_DIM)
            k_ref[:, sl] = head(p_ref[:, sl], wk_ref[...]).astype(BF16)
            v_ref[:, h * V_EXT_W:h * V_EXT_W + HEAD_DIM] = p_ref[:, KV_W + h * HEAD_DIM:KV_W + (h + 1) * HEAD_DIM].astype(BF16)
            lane = lax.broadcasted_iota(jnp.int32, (tt, HEAD_DIM), 1)
            v_ref[:, h * V_EXT_W + HEAD_DIM:(h + 1) * V_EXT_W] = jnp.where(lane == 0, 1.0, 0.0).astype(BF16)
        if latent:
            for h in range(N_Q_HEADS):
                sl = slice(2 * KV_W + h * HEAD_DIM, 2 * KV_W + (h + 1) * HEAD_DIM)
                q_ref[:, h * HEAD_DIM:(h + 1) * HEAD_DIM] = (head(p_ref[:, sl], wq_ref[...]) * Q_SCALE).astype(BF16)

    vec = pl.BlockSpec((1, HEAD_DIM), lambda i: (0, 0))
    tab = pl.BlockSpec((tt, HEAD_DIM), lambda i: (i, 0))
    vw = N_KV_HEADS * V_EXT_W
    k_shape = jax.ShapeDtypeStruct((rows_all, KV_W), BF16)
    v_shape = jax.ShapeDtypeStruct((rows_all, vw), BF16)
    aliases = {}
    if latent:
        in_specs = [pl.BlockSpec((tt, width), lambda i: (i, 0)), vec, vec, tab, tab]
        ops = [p, _row(wq), _row(wk), cexp, sexp]
        out_specs = [pl.BlockSpec((tt, ATTN_W), lambda i: (i, 0)), pl.BlockSpec((tt, KV_W), lambda i: (i, 0)),
                     pl.BlockSpec((tt, vw), lambda i: (i, 0))]
        out_shape = [jax.ShapeDtypeStruct((n, ATTN_W), BF16), k_shape, v_shape]
    else:
        assert n == tt and (rows_all - n) % n == 0, (n, tt, rows_all)
        first = (rows_all - n) // n
        in_specs = [pl.BlockSpec((tt, width), lambda i: (i, 0)), vec, ANY, ANY]
        ops = [p, _row(wk), k_all, v_ext]
        out_specs = [pl.BlockSpec((tt, KV_W), lambda i: (first, 0)), pl.BlockSpec((tt, vw), lambda i: (first, 0))]
        out_shape = [k_shape, v_shape]
        aliases = {2: 0, 3: 1}
    return pl.pallas_call(
        body, name=name, grid=(n // tt,), in_specs=in_specs, out_specs=out_specs, out_shape=out_shape,
        input_output_aliases=aliases, compiler_params=_params("parallel"),
    )(*ops)


def qkv_prep_bwd(p, dq, dk, dv, wq, wk, cexp, sexp, *, latent, name):
    n = p.shape[0]
    tt = _rows_tile(n, 1024)
    width = 2 * KV_W + (ATTN_W if latent else 0)

    def body(*refs):
        if latent:
            p_ref, dq_ref, dk_ref, dv_ref, wq_ref, wk_ref, c_ref, s_ref, dp_ref, acc_ref = refs
        else:
            p_ref, dk_ref, dv_ref, wk_ref, dp_ref, acc_ref = refs
        i = pl.program_id(0)

        @pl.when(i == 0)
        def _():
            acc_ref[...] = jnp.zeros_like(acc_ref)

        def head(xv, dy, w, row):
            if latent:
                dy = dy * c_ref[...] + _swap_pairs(dy * s_ref[...])
            rstd = lax.rsqrt(jnp.mean(xv * xv, axis=-1, keepdims=True) + EPS)
            xhat = xv * rstd
            acc_ref[row:row + 1, :] += jnp.sum(dy * xhat, axis=0, keepdims=True)
            dxhat = dy * w
            return rstd * (dxhat - xhat * jnp.mean(dxhat * xhat, axis=-1, keepdims=True))

        for h in range(N_KV_HEADS):
            sl = slice(h * HEAD_DIM, (h + 1) * HEAD_DIM)
            dp_ref[:, sl] = head(p_ref[:, sl], dk_ref[:, sl], wk_ref[...], 1).astype(BF16)
        dp_ref[:, KV_W:2 * KV_W] = dv_ref[...].astype(BF16)
        if latent:
            for h in range(N_Q_HEADS):
                sl = slice(2 * KV_W + h * HEAD_DIM, 2 * KV_W + (h + 1) * HEAD_DIM)
                dyq = dq_ref[:, h * HEAD_DIM:(h + 1) * HEAD_DIM] * Q_SCALE
                dp_ref[:, sl] = head(p_ref[:, sl], dyq, wq_ref[...], 0).astype(BF16)

    vec = pl.BlockSpec((1, HEAD_DIM), lambda i: (0, 0))
    tab = pl.BlockSpec((tt, HEAD_DIM), lambda i: (i, 0))
    first = 0 if latent else (dk.shape[0] - n) // tt
    kv_spec = pl.BlockSpec((tt, KV_W), lambda i: (first + i, 0))
    p_spec = pl.BlockSpec((tt, width), lambda i: (i, 0))
    if latent:
        in_specs = [p_spec, pl.BlockSpec((tt, ATTN_W), lambda i: (i, 0)), kv_spec, kv_spec, vec, vec, tab, tab]
        ops = [p, dq, dk, dv, _row(wq), _row(wk), cexp, sexp]
    else:
        in_specs = [p_spec, kv_spec, kv_spec, vec]
        ops = [p, dk, dv, _row(wk)]
    return pl.pallas_call(
        body, name=name, grid=(n // tt,), in_specs=in_specs,
        out_specs=[p_spec, pl.BlockSpec((8, HEAD_DIM), lambda i: (0, 0))],
        out_shape=[jax.ShapeDtypeStruct((n, width), BF16), jax.ShapeDtypeStruct((8, HEAD_DIM), F32)],
        compiler_params=_params("arbitrary"),
    )(*ops)


def _kv_chunks(n, s_all, want=1024):
    step = want if n % want == 0 else 256
    chunks = [(s, step) for s in range(0, n, step)]
    if s_all > n:
        chunks.append((n, s_all - n))
    return chunks


def flash_fwd(q, k_all, v_ext, p, *, za_block, mix_width, name):
    n = q.shape[0]
    s_all = k_all.shape[0]
    tq = _rows_tile(n, 512)
    chunks = _kv_chunks(n, s_all)
    wide = 2 * HEAD_DIM

    def body(q_ref, k_ref, v_ref, za_ref, o_ref, mix_ref, lse_ref, m_ref, acc_ref):
        qv = q_ref[...]
        m_ref[...] = jnp.full_like(m_ref, -jnp.inf)
        acc_ref[...] = jnp.zeros_like(acc_ref)
        for start, size in chunks:
            kc = k_ref[pl.ds(start, size), :]
            vc = v_ref[pl.ds(start, size), :]
            s = lax.dot_general(qv, kc, _DIMS["nt"], preferred_element_type=F32)
            m_old = m_ref[...]
            m_new = jnp.maximum(m_old, jnp.max(s, axis=-1, keepdims=True))
            pr = jnp.exp2(s - m_new)
            alpha = jnp.exp2(m_old - m_new)
            acc_ref[...] = alpha * acc_ref[...] + jnp.dot(pr.astype(BF16), vc, preferred_element_type=F32)
            m_ref[...] = m_new
        acc = acc_ref[...]
        denom = acc[:, HEAD_DIM:HEAD_DIM + 1]
        o = acc[:, :HEAD_DIM] / denom
        o_ref[...] = o.astype(BF16)
        mix_ref[...] = (o * _silu(za_ref[...])).astype(BF16)
        lse_ref[0] = _as_row(m_ref[...] + jnp.log2(denom))

    qspec = pl.BlockSpec((tq, HEAD_DIM), lambda h, i: (i, h))
    return pl.pallas_call(
        body, name=name, grid=(N_Q_HEADS, n // tq),
        in_specs=[qspec, pl.BlockSpec((s_all, HEAD_DIM), lambda h, i: (0, h // Q_PER_KV)),
                  pl.BlockSpec((s_all, wide), lambda h, i: (0, h // Q_PER_KV)),
                  pl.BlockSpec((tq, HEAD_DIM), lambda h, i: (i, za_block + h))],
        out_specs=[qspec, qspec, pl.BlockSpec((1, 1, tq), lambda h, i: (h, 0, i))],
        out_shape=[jax.ShapeDtypeStruct((n, ATTN_W), BF16), jax.ShapeDtypeStruct((n, mix_width), BF16),
                   jax.ShapeDtypeStruct((N_Q_HEADS, 1, n), F32)],
        scratch_shapes=[pltpu.VMEM((tq, 1), F32), pltpu.VMEM((tq, wide), F32)],
        compiler_params=_params("parallel", "parallel"),
    )(q, k_all, v_ext, p)


def attn_gate_bwd(dmix, o, p, *, za_block, name):
    n = o.shape[0]
    tt = _rows_tile(n, 1024)
    za_half = za_block * HEAD_DIM // HALF
    n_half = ATTN_W // HALF

    def body(*refs):
        dm_ref, o_ref = refs[:2]
        za_refs = refs[2:2 + n_half]
        do_ref, dos_ref, dza_ref, delta_ref = refs[2 + n_half:]
        dm = dm_ref[...]
        ov = o_ref[...].astype(F32)
        za = jnp.concatenate([r[...] for r in za_refs], axis=-1)
        do = dm * _silu(za)
        do_ref[...] = do.astype(BF16)
        dos_ref[...] = (do * LN2).astype(BF16)
        dza_ref[...] = (dm * ov * _dsilu(za)).astype(BF16)
        prod = do * ov
        for h in range(N_Q_HEADS):
            col = jnp.sum(prod[:, h * HEAD_DIM:(h + 1) * HEAD_DIM], axis=-1, keepdims=True) * LN2
            delta_ref[h] = _as_row(col)

    spec = pl.BlockSpec((tt, ATTN_W), lambda i: (i, 0))
    shape = jax.ShapeDtypeStruct((n, ATTN_W), BF16)
    za_specs = [pl.BlockSpec((tt, HALF), functools.partial(lambda i, cb: (i, cb), cb=za_half + h))
                for h in range(n_half)]
    return pl.pallas_call(
        body, name=name, grid=(n // tt,),
        in_specs=[spec, spec] + za_specs,
        out_specs=[spec, spec, spec, pl.BlockSpec((N_Q_HEADS, 1, tt), lambda i: (0, 0, i))],
        out_shape=[shape, shape, shape, jax.ShapeDtypeStruct((N_Q_HEADS, 1, n), F32)],
        compiler_params=_params("parallel"),
    )(dmix, o, *([p] * n_half))


def flash_bwd(q, do, do_s, lse_row, delta_row, k_all, v_ext, *, name):
    n = q.shape[0]
    s_all = k_all.shape[0]
    tq = _rows_tile(n, 1024)
    chunks = _kv_chunks(n, s_all)

    def body(q_ref, do_ref, dos_ref, lse_ref, dl_ref, k_ref, v_ref, dq_ref, dk_ref, dv_ref):
        g = pl.program_id(1)
        i = pl.program_id(2)

        @pl.when((g == 0) & (i == 0))
        def _():
            dk_ref[...] = jnp.zeros_like(dk_ref)
            dv_ref[...] = jnp.zeros_like(dv_ref)

        qv = q_ref[...]
        dov = do_ref[...]
        dosv = dos_ref[...]
        lse = lse_ref[0]
        dl = dl_ref[0]
        dq = jnp.zeros((tq, HEAD_DIM), F32)
        for start, size in chunks:
            kc = k_ref[pl.ds(start, size), :]
            vc = v_ref[pl.ds(start, size), :]
            st = lax.dot_general(kc, qv, _DIMS["nt"], preferred_element_type=F32)
            pt = jnp.exp2(st - lse)
            dpt = lax.dot_general(vc, dosv, _DIMS["nt"], preferred_element_type=F32)
            dst = (pt * (dpt - dl)).astype(BF16)
            dv_ref[pl.ds(start, size), :] += jnp.dot(pt.astype(BF16), dov, preferred_element_type=F32)
            dk_ref[pl.ds(start, size), :] += jnp.dot(dst, qv, preferred_element_type=F32)
            dq = dq + lax.dot_general(dst, kc, _DIMS["tn"], preferred_element_type=F32)
        dq_ref[...] = dq

    qspec = pl.BlockSpec((tq, HEAD_DIM), lambda kh, g, i: (i, kh * Q_PER_KV + g))
    rowspec = pl.BlockSpec((1, 1, tq), lambda kh, g, i: (kh * Q_PER_KV + g, 0, i))
    kvspec = pl.BlockSpec((s_all, HEAD_DIM), lambda kh, g, i: (0, kh))
    return pl.pallas_call(
        body, name=name, grid=(N_KV_HEADS, Q_PER_KV, n // tq),
        in_specs=[qspec, qspec, qspec, rowspec, rowspec, kvspec,
                  pl.BlockSpec((s_all, HEAD_DIM), lambda kh, g, i: (0, kh * (V_EXT_W // HEAD_DIM)))],
        out_specs=[qspec, kvspec, kvspec],
        out_shape=[jax.ShapeDtypeStruct((n, ATTN_W), F32),
                   jax.ShapeDtypeStruct((s_all, KV_W), F32), jax.ShapeDtypeStruct((s_all, KV_W), F32)],
        compiler_params=_params("arbitrary", "arbitrary", "arbitrary"),
    )(q, do, do_s, lse_row, delta_row, k_all, v_ext)


HALF = 512


SUB = 128


def _tap_group(w_ref, w_lanes, ext_ref, r0, lanes, r, tap_of):
    z = None
    for a in range(4):
        k = tap_of(8 * a + r)
        if 0 <= k < CONV_WIDTH:
            term = w_ref[k:k + 1, w_lanes] * ext_ref[pl.ds(r0 + 8 * a, SUB + 8), lanes]
            z = term if z is None else z + term
    return z


def _shifted(z, z_ref, r):
    if r == 0:
        return z[0:SUB]
    z_ref[...] = z
    return z_ref[pl.ds(r, SUB), :]


def _fold8(x):
    return jnp.sum(x.reshape(x.shape[0] // 8, 8, x.shape[1]), axis=0)


def _pieces(tt, h):
    return [(rh * SUB, slice(c * SUB, (c + 1) * SUB), slice(h * HALF + c * SUB, h * HALF + (c + 1) * SUB))
            for c in range(HALF // SUB) for rh in range(tt // SUB)]


def _halo_specs(tt, n, cb):
    per = tt // HALO
    last = n // HALO - 1
    return [pl.BlockSpec((HALO, HALF), lambda i: (jnp.maximum(i * per - 1, 0), cb)),
            pl.BlockSpec((tt, HALF), lambda i: (i, cb)),
            pl.BlockSpec((HALO, HALF), lambda i: (jnp.minimum((i + 1) * per, last), cb))]


def conv_fwd(p, dw_w, dw_b, ln_g, ln_b, mix, *, glu_block, name):
    n = p.shape[0]
    ch = dw_w.shape[1]
    nh = ch // HALF
    tt = _rows_tile(n, 512)
    last = n // tt - 1

    def body(*refs):
        a_refs = [refs[3 * h:3 * h + 3] for h in range(nh)]
        b_refs = [refs[3 * (nh + h):3 * (nh + h) + 3] for h in range(nh)]
        pos = 6 * nh
        zb_refs = refs[pos:pos + nh]
        pos += nh
        w_ref, bias_ref, g_ref, be_ref, _, ycv_ref, mix_ref, ext_ref, z_ref = refs[pos:pos + 9]
        i = pl.program_id(0)
        for h in range(nh):
            cs = slice(h * HALF, (h + 1) * HALF)
            ap, am, an = a_refs[h]
            bp, bm, bn = b_refs[h]
            ext_ref[0:HALO, :] = jnp.where(i > 0, ap[...] * _sigmoid(bp[...]), 0.0)
            ext_ref[HALO:HALO + tt, :] = am[...] * _sigmoid(bm[...])
            ext_ref[HALO + tt:2 * HALO + tt, :] = jnp.where(i < last, an[...] * _sigmoid(bn[...]), 0.0)
            for r0, lanes, w_lanes in _pieces(tt, h):
                acc = jnp.broadcast_to(bias_ref[:, w_lanes], (SUB, SUB))
                for r in range(8):
                    z = _tap_group(w_ref, w_lanes, ext_ref, r0, lanes, r, lambda j: j - 1)
                    acc = acc + _shifted(z, z_ref, r)
                ycv_ref[pl.ds(r0, SUB), w_lanes] = acc
        yc = ycv_ref[...]
        mu = jnp.mean(yc, axis=-1, keepdims=True)
        var = jnp.mean(jnp.square(yc - mu), axis=-1, keepdims=True)
        ln = (yc - mu) * lax.rsqrt(var + EPS) * g_ref[...] + be_ref[...]
        out = _silu(ln)
        for h in range(nh):
            cs = slice(h * HALF, (h + 1) * HALF)
            mix_ref[:, cs] = (out[:, cs] * _silu(zb_refs[h][...])).astype(BF16)

    in_specs = []
    for h in range(2 * nh):
        in_specs += _halo_specs(tt, n, glu_block + h)
    in_specs += [pl.BlockSpec((tt, HALF), functools.partial(lambda i, cb: (i, cb), cb=glu_block + 2 * nh + h))
                 for h in range(nh)]
    vec = pl.BlockSpec((1, ch), lambda i: (0, 0))
    in_specs += [pl.BlockSpec((CONV_WIDTH, ch), lambda i: (0, 0)), vec, vec, vec, ANY]
    ops = [p] * (6 * nh + nh) + [dw_w, _row(dw_b), _row(ln_g), _row(ln_b), mix]
    big = pl.BlockSpec((tt, ch), lambda i: (i, 0))
    mix_block = (mix.shape[1] - ch) // ch
    return pl.pallas_call(
        body, name=name, grid=(n // tt,), in_specs=in_specs,
        out_specs=[big, pl.BlockSpec((tt, ch), lambda i: (i, mix_block))],
        out_shape=[jax.ShapeDtypeStruct((n, ch), F32), jax.ShapeDtypeStruct(mix.shape, BF16)],
        input_output_aliases={len(ops) - 1: 1},
        scratch_shapes=[pltpu.VMEM((tt + 2 * HALO, HALF), F32), pltpu.VMEM((SUB + 8, SUB), F32)],
        compiler_params=_params("parallel"),
    )(*ops)


def conv_bwd_rows(dmix, ycv, p, ln_g, ln_b, *, mix_block, zb_block, name):
    n, ch = ycv.shape
    nh = ch // HALF
    tt = _rows_tile(n, 512)

    def body(*refs):
        dm_ref, ycv_ref = refs[:2]
        zb_refs = refs[2:2 + nh]
        g_ref, be_ref, dy_ref, dzb_ref, acc_ref = refs[2 + nh:]
        i = pl.program_id(0)

        @pl.when(i == 0)
        def _():
            acc_ref[...] = jnp.zeros_like(acc_ref)

        yc = ycv_ref[...]
        mu = jnp.mean(yc, axis=-1, keepdims=True)
        var = jnp.mean(jnp.square(yc - mu), axis=-1, keepdims=True)
        rstd = lax.rsqrt(var + EPS)
        xhat = (yc - mu) * rstd
        ln = xhat * g_ref[...] + be_ref[...]
        out = _silu(ln)
        dm = dm_ref[...]
        zb = jnp.concatenate([r[...] for r in zb_refs], axis=-1)
        dzb_ref[...] = (dm * out * _dsilu(zb)).astype(BF16)
        dln = dm * _silu(zb) * _dsilu(ln)
        acc_ref[0:1, :] += jnp.sum(dln * xhat, axis=0, keepdims=True)
        acc_ref[1:2, :] += jnp.sum(dln, axis=0, keepdims=True)
        dxhat = dln * g_ref[...]
        dy_ref[...] = rstd * (dxhat - jnp.mean(dxhat, axis=-1, keepdims=True)
                              - xhat * jnp.mean(dxhat * xhat, axis=-1, keepdims=True))

    big = pl.BlockSpec((tt, ch), lambda i: (i, 0))
    vec = pl.BlockSpec((1, ch), lambda i: (0, 0))
    in_specs = [pl.BlockSpec((tt, ch), lambda i: (i, mix_block)), big]
    in_specs += [pl.BlockSpec((tt, HALF), functools.partial(lambda i, cb: (i, cb), cb=zb_block + h)) for h in range(nh)]
    in_specs += [vec, vec]
    return pl.pallas_call(
        body, name=name, grid=(n // tt,), in_specs=in_specs,
        out_specs=[big, big, pl.BlockSpec((8, ch), lambda i: (0, 0))],
        out_shape=[jax.ShapeDtypeStruct((n, ch), F32), jax.ShapeDtypeStruct((n, ch), BF16),
                   jax.ShapeDtypeStruct((8, ch), F32)],
        compiler_params=_params("arbitrary"),
    )(dmix, ycv, *([p] * nh), _row(ln_g), _row(ln_b))


def conv_bwd_taps(dycv, p, dw_w, *, glu_block, name):
    n, ch = dycv.shape
    nh = ch // HALF
    tt = _rows_tile(n, 512)
    last = n // tt - 1

    def body(*refs):
        d_refs = [refs[3 * h:3 * h + 3] for h in range(nh)]
        a_refs = [refs[3 * (nh + h):3 * (nh + h) + 3] for h in range(nh)]
        b_refs = [refs[3 * (2 * nh + h):3 * (2 * nh + h) + 3] for h in range(nh)]
        w_ref, dglu_ref, dw_ref, db_ref, yext_ref, dext_ref, z_ref, dy_ref, dwp_ref, dbp_ref = refs[9 * nh:]
        i = pl.program_id(0)

        @pl.when(i == 0)
        def _():
            dwp_ref[...] = jnp.zeros_like(dwp_ref)
            dbp_ref[...] = jnp.zeros_like(dbp_ref)

        for h in range(nh):
            cs = slice(h * HALF, (h + 1) * HALF)
            ap, am, an = a_refs[h]
            bp, bm, bn = b_refs[h]
            dp, dm, dn = d_refs[h]
            av = am[...]
            sb = _sigmoid(bm[...])
            yext_ref[0:HALO, :] = jnp.where(i > 0, ap[...] * _sigmoid(bp[...]), 0.0)
            yext_ref[HALO:HALO + tt, :] = av * sb
            yext_ref[HALO + tt:2 * HALO + tt, :] = jnp.where(i < last, an[...] * _sigmoid(bn[...]), 0.0)
            dmain = dm[...]
            dext_ref[0:HALO, :] = jnp.where(i > 0, dp[...], 0.0)
            dext_ref[HALO:HALO + tt, :] = dmain
            dext_ref[HALO + tt:2 * HALO + tt, :] = jnp.where(i < last, dn[...], 0.0)
            for r0, lanes, w_lanes in _pieces(tt, h):
                dsub = dext_ref[pl.ds(HALO + r0, SUB), lanes]
                dy = jnp.zeros((SUB, SUB), F32)
                for r in range(8):
                    z = _tap_group(w_ref, w_lanes, dext_ref, r0, lanes, r, lambda j: CONV_WIDTH - j)
                    dy = dy + _shifted(z, z_ref, r)
                    y_r = yext_ref[pl.ds(r0 + r, SUB + 24), lanes]
                    for a in range(4):
                        k = 8 * a + r - 1
                        if k >= 0:
                            dwp_ref[8 * k:8 * k + 8, w_lanes] += _fold8(dsub * y_r[8 * a:8 * a + SUB])
                dbp_ref[:, w_lanes] += _fold8(dsub)
                dy_ref[pl.ds(r0, SUB), lanes] = dy
            dy = dy_ref[...]
            dglu_ref[:, cs] = (dy * sb).astype(BF16)
            dglu_ref[:, ch + h * HALF:ch + (h + 1) * HALF] = (dy * av * sb * (1.0 - sb)).astype(BF16)

        @pl.when(i == last)
        def _():
            dw_ref[...] = jnp.sum(dwp_ref[...].reshape(32, 8, ch), axis=1)
            db_ref[...] = jnp.broadcast_to(jnp.sum(dbp_ref[...], axis=0, keepdims=True), db_ref.shape)

    in_specs = []
    for h in range(nh):
        in_specs += _halo_specs(tt, n, h)
    for h in range(2 * nh):
        in_specs += _halo_specs(tt, n, glu_block + h)
    in_specs += [pl.BlockSpec((CONV_WIDTH, ch), lambda i: (0, 0))]
    ops = [dycv] * (3 * nh) + [p] * (6 * nh) + [dw_w]
    return pl.pallas_call(
        body, name=name, grid=(n // tt,), in_specs=in_specs,
        out_specs=[pl.BlockSpec((tt, 2 * ch), lambda i: (i, 0)), pl.BlockSpec((32, ch), lambda i: (0, 0)),
                   pl.BlockSpec((8, ch), lambda i: (0, 0))],
        out_shape=[jax.ShapeDtypeStruct((n, 2 * ch), BF16), jax.ShapeDtypeStruct((32, ch), F32),
                   jax.ShapeDtypeStruct((8, ch), F32)],
        scratch_shapes=[pltpu.VMEM((tt + 2 * HALO, HALF), F32), pltpu.VMEM((tt + 2 * HALO, HALF), F32),
                        pltpu.VMEM((SUB + 8, SUB), F32), pltpu.VMEM((tt, HALF), F32),
                        pltpu.VMEM((8 * 32, ch), F32), pltpu.VMEM((8, ch), F32)],
        compiler_params=_params("arbitrary"),
    )(*ops)


def _sgu_common(p_ref, g_ref, be_ref, ws_ref, bs_ref, w):
    gw = w // SGU_GROUPS
    u_pre = p_ref[:, 0:w]
    v_pre = p_ref[:, w:2 * w]
    zc = p_ref[:, 2 * w:3 * w]
    u = _gelu(u_pre)
    v = _gelu(v_pre)
    mu = jnp.mean(v, axis=-1, keepdims=True)
    var = jnp.mean(jnp.square(v - mu), axis=-1, keepdims=True)
    rstd = lax.rsqrt(var + EPS)
    vhat = (v - mu) * rstd
    vn = (vhat * g_ref[...] + be_ref[...]).astype(BF16)
    mixed = jnp.concatenate(
        [jnp.dot(ws_ref[gi].astype(BF16), vn[:, gi * gw:(gi + 1) * gw], preferred_element_type=F32)
         + bs_ref[:, gi:gi + 1] for gi in range(SGU_GROUPS)], axis=-1)
    return u_pre, v_pre, zc, u, rstd, vhat, vn, mixed


def sgu_fwd(p, ln_g, ln_b, ws, bs_t, *, name):
    n, w3 = p.shape
    w = w3 // 3

    def body(p_ref, g_ref, be_ref, ws_ref, bs_ref, m_ref):
        _, _, zc, u, _, _, _, mixed = _sgu_common(p_ref, g_ref, be_ref, ws_ref, bs_ref, w)
        m_ref[...] = (u * mixed * _silu(zc)).astype(BF16)

    vec = pl.BlockSpec((1, w), lambda i: (0, 0))
    return pl.pallas_call(
        body, name=name, grid=(n // CHUNK,),
        in_specs=[pl.BlockSpec((CHUNK, w3), lambda i: (i, 0)), vec, vec,
                  pl.BlockSpec((SGU_GROUPS, CHUNK, CHUNK), lambda i: (0, 0, 0)),
                  pl.BlockSpec((CHUNK, SGU_GROUPS), lambda i: (0, 0))],
        out_specs=pl.BlockSpec((CHUNK, w), lambda i: (i, 0)),
        out_shape=jax.ShapeDtypeStruct((n, w), BF16),
        compiler_params=_params("parallel"),
    )(p, _row(ln_g), _row(ln_b), ws, bs_t)


def sgu_bwd(p, dm, ln_g, ln_b, ws, ws_t, bs_t, *, name):
    n, w3 = p.shape
    w = w3 // 3
    gw = w // SGU_GROUPS

    def body(p_ref, dm_ref, g_ref, be_ref, ws_ref, wst_ref, bs_ref, dp_ref, dws_ref, dbs_ref, acc_ref):
        i = pl.program_id(0)

        @pl.when(i == 0)
        def _():
            dws_ref[...] = jnp.zeros_like(dws_ref)
            dbs_ref[...] = jnp.zeros_like(dbs_ref)
            acc_ref[...] = jnp.zeros_like(acc_ref)

        u_pre, v_pre, zc, u, rstd, vhat, vn, mixed = _sgu_common(p_ref, g_ref, be_ref, ws_ref, bs_ref, w)
        dmv = dm_ref[...]
        um = u * mixed
        dp_ref[:, 2 * w:3 * w] = (dmv * um * _dsilu(zc)).astype(BF16)
        dum = dmv * _silu(zc)
        dp_ref[:, 0:w] = (dum * mixed * _dgelu(u_pre)).astype(BF16)
        dmixed = dum * u
        dmixed_b = dmixed.astype(BF16)
        dvn_parts = []
        for gi in range(SGU_GROUPS):
            cs = slice(gi * gw, (gi + 1) * gw)
            dws_ref[gi] += lax.dot_general(dmixed_b[:, cs], vn[:, cs], _DIMS["nt"], preferred_element_type=F32)
            dbs_ref[:, gi:gi + 1] += jnp.sum(dmixed[:, cs], axis=-1, keepdims=True)
            dvn_parts.append(jnp.dot(wst_ref[gi].astype(BF16), dmixed_b[:, cs], preferred_element_type=F32))
        dvn = jnp.concatenate(dvn_parts, axis=-1)
        acc_ref[0:1, :] += jnp.sum(dvn * vhat, axis=0, keepdims=True)
        acc_ref[1:2, :] += jnp.sum(dvn, axis=0, keepdims=True)
        dvhat = dvn * g_ref[...]
        dv = rstd * (dvhat - jnp.mean(dvhat, axis=-1, keepdims=True)
                     - vhat * jnp.mean(dvhat * vhat, axis=-1, keepdims=True))
        dp_ref[:, w:2 * w] = (dv * _dgelu(v_pre)).astype(BF16)

    vec = pl.BlockSpec((1, w), lambda i: (0, 0))
    wspec = pl.BlockSpec((SGU_GROUPS, CHUNK, CHUNK), lambda i: (0, 0, 0))
    bspec = pl.BlockSpec((CHUNK, SGU_GROUPS), lambda i: (0, 0))
    return pl.pallas_call(
        body, name=name, grid=(n // CHUNK,),
        in_specs=[pl.BlockSpec((CHUNK, w3), lambda i: (i, 0)), pl.BlockSpec((CHUNK, w), lambda i: (i, 0)),
                  vec, vec, wspec, wspec, bspec],
        out_specs=[pl.BlockSpec((CHUNK, w3), lambda i: (i, 0)), wspec, bspec,
                   pl.BlockSpec((8, w), lambda i: (0, 0))],
        out_shape=[jax.ShapeDtypeStruct((n, w3), BF16), jax.ShapeDtypeStruct((SGU_GROUPS, CHUNK, CHUNK), F32),
                   jax.ShapeDtypeStruct((CHUNK, SGU_GROUPS), F32), jax.ShapeDtypeStruct((8, w), F32)],
        compiler_params=_params("arbitrary"),
    )(p, dm, _row(ln_g), _row(ln_b), ws, ws_t, bs_t)


def _adam_math(w, g, m, v):
    m_new = ADAM_B1 * m + (1.0 - ADAM_B1) * g
    v_new = ADAM_B2 * v + (1.0 - ADAM_B2) * (g * g)
    m_hat = m_new / (1.0 - ADAM_B1 ** ADAM_STEP)
    v_hat = v_new / (1.0 - ADAM_B2 ** ADAM_STEP)
    delta = -ADAM_LR * (m_hat / (jnp.sqrt(v_hat) + ADAM_EPS) + ADAM_WD * w)
    return delta, m_new, v_new


def adamw(w, g, m, v, *, name, slots=False, rows=512):
    r, c = w.shape
    tr = min(r, rows)
    assert r % tr == 0, (name, r, tr)

    def body(w_ref, g_ref, m_ref, v_ref, go_ref, d_ref, mo_ref, vo_ref):
        if slots:
            g = g_ref[0].astype(F32)
            for k in range(1, N_DEV):
                g = g + g_ref[k].astype(F32)
        else:
            g = g_ref[...].astype(F32)
        delta, m_new, v_new = _adam_math(w_ref[...], g, m_ref[...], v_ref[...])
        go_ref[...] = g
        d_ref[...] = delta
        mo_ref[...] = m_new
        vo_ref[...] = v_new

    spec = pl.BlockSpec((tr, c), lambda i: (i, 0))
    gspec = pl.BlockSpec((N_DEV, tr, c), lambda i: (0, i, 0)) if slots else spec
    shape = jax.ShapeDtypeStruct((r, c), F32)
    return pl.pallas_call(
        body, name=name, grid=(r // tr,),
        in_specs=[spec, gspec, spec, spec], out_specs=[spec] * 4, out_shape=[shape] * 4,
        compiler_params=_params("parallel"),
    )(w, g, m, v)


def slot_sum(slots, *, name):
    _, r, c = slots.shape

    def body(s_ref, o_ref):
        acc = s_ref[0]
        for k in range(1, N_DEV):
            acc = acc + s_ref[k]
        o_ref[...] = acc

    return pl.pallas_call(
        body, name=name, out_shape=jax.ShapeDtypeStruct((r, c), F32),
        in_specs=[VMEM_SPEC], out_specs=VMEM_SPEC,
        compiler_params=pltpu.CompilerParams(vmem_limit_bytes=VMEM_LIMIT),
    )(slots)


def _after(token, val):
    return val + token[0, 0]


def _zero_of(v):
    bits = lax.bitcast_convert_type(v, jnp.uint16 if v.dtype == BF16 else jnp.uint32)
    return jnp.where((bits | 1) == 0, 1.0, 0.0).astype(F32)


def _pad_rows(a, rows):
    return jnp.pad(a, ((0, rows - a.shape[0]), (0, 0)))


def kernel(x, c, ctx, c_ctx, ada_w, ada_b, norm_g, ev_w_in, ev_q_norm, ev_k_norm, ev_dw_w, ev_dw_b, ev_ln_g, ev_ln_b, ev_w_out, od_w_in, od_ln_g, od_ln_b, od_ws, od_bs, od_w_out, final_g, loss_target, m_c_ctx, m_ada_w, m_ada_b, m_norm_g, m_ev_w_in, m_ev_q_norm, m_ev_k_norm, m_ev_dw_w, m_ev_dw_b, m_ev_ln_g, m_ev_ln_b, m_ev_w_out, m_od_w_in, m_od_ln_g, m_od_ln_b, m_od_ws, m_od_bs, m_od_w_out, m_final_g, v_c_ctx, v_ada_w, v_ada_b, v_norm_g, v_ev_w_in, v_ev_q_norm, v_ev_k_norm, v_ev_dw_w, v_ev_dw_b, v_ev_ln_g, v_ev_ln_b, v_ev_w_out, v_od_w_in, v_od_ln_g, v_od_ln_b, v_od_ws, v_od_bs, v_od_w_out, v_final_g):
    n, d = x.shape[1], x.shape[2]
    lc = ctx.shape[1]
    ev_in = ev_w_in.shape[2] * N_DEV
    od_in = od_w_in.shape[2] * N_DEV
    conv_ch = ev_dw_w.shape[2] * N_DEV
    ada_cols = ada_w.shape[2]
    me = 4 * lax.axis_index("x") + 2 * lax.axis_index("y") + lax.axis_index("c")
    xs, tgt, ctxs = x[0], loss_target[0], ctx[0]
    za_block = (2 * KV_W + ATTN_W) // HEAD_DIM
    glu_block = (2 * KV_W + 2 * ATTN_W) // HALF
    zb_block = glu_block + 2 * conv_ch // HALF

    small = jnp.concatenate([
        jax.nn.silu(c).reshape(1, d),
        od_ln_g.reshape(1, -1), od_ln_b.reshape(1, -1)], axis=1)
    small = _pad_rows(small, 8)
    dw_rows = _pad_rows(ev_dw_w[0], 32)
    small_g, dw_g = all_gather([small, dw_rows], name="gather_small")
    sc_all = small_g[:, 0, :d]
    shard = d // N_DEV
    od_ln_g_full = small_g[:, 0, d:d + shard].reshape(d)
    od_ln_b_full = small_g[:, 0, d + shard:d + 2 * shard].reshape(d)
    dw_w_full = jnp.moveaxis(dw_g, 0, 1).reshape(32, conv_ch)[:CONV_WIDTH]
    scc = jax.nn.silu(c_ctx)
    sc16 = _pad_rows(jnp.concatenate([sc_all, scc.reshape(1, d)], axis=0), 16)

    ada_bf = ada_w.astype(BF16)
    mod_loc = [matmul(sc16, ada_bf[l], mode="nn", tm=16, tn=ada_cols, tk=d, out_dtype=F32, name=f"ada_mod{l}")
               for l in range(2)]
    (mod_g,) = all_gather([jnp.stack(mod_loc)], name="gather_mod")
    mod_all = jnp.moveaxis(mod_g, 0, 2).reshape(2, 16, N_DEV * ada_cols) + ada_b[:, None, :]
    mod_me = lax.dynamic_index_in_dim(mod_all, me, axis=1, keepdims=False)
    shift = [mod_me[l, :d] for l in range(2)]
    scale = [mod_me[l, d:2 * d] for l in range(2)]
    gate = [mod_me[l, 2 * d:] for l in range(2)]
    shift_c, scale_c = mod_all[0, 8, :d], mod_all[0, 8, d:2 * d]

    h_w_in0, near_token = staged_gather_start(1, (ev_w_in[0] + _zero_of(mod_g[0, 0, 0, 0])).astype(BF16), None,
                                              name="gather_w_in0_chips_start")

    cexp, sexp = rope_tables(n)
    h0 = norm_mod_fwd(xs, norm_g[0], _after(near_token, shift[0]), scale[0], name="norm_mod_fwd0")
    hc = norm_mod_fwd(ctxs, norm_g[0], shift_c, scale_c, name="norm_mod_fwd_ctx")

    wi0_shard, wi0_land = staged_gather_wait(1, h_w_in0, h0, name="gather_w_in0_chips_wait")
    h_w_in0, far_token = staged_gather_start(2, wi0_shard, wi0_land, name="gather_w_in0_cores_start")
    wi0_shard, wi0_land = staged_gather_wait(2, h_w_in0, far_token, name="gather_w_in0_cores_wait")
    wi0_g = lax.dynamic_update_slice_in_dim(wi0_land, wi0_shard[None], me, axis=0)
    wi0 = jnp.moveaxis(wi0_g, 0, 1).reshape(d, ev_in)
    later = [w[0].astype(BF16) for w in (ev_w_out, od_w_in, od_w_out)]
    (h_wo0, h_wi1, h_wo1), w_token = exchange_start(later, scatter=False, name="gather_rest_start", after=wi0_g)

    def landed(handle, after, name):
        own, land = exchange_wait(handle, after, scatter=False, name=name)
        return lax.dynamic_update_slice_in_dim(land, own[None], me, axis=0)

    p0 = matmul(h0, wi0, mode="nn", tm=1024, tn=ev_in // 4, tk=d, out_dtype=F32, name="proj_in0", after=w_token)
    pc = matmul(hc, wi0, mode="nn", tm=lc, tn=2 * KV_W, tk=d, out_dtype=F32, n_out=2 * KV_W, name="proj_in_ctx")
    q_r, k_all, v_ext = qkv_prep_fwd(p0, ev_q_norm[0], ev_k_norm[0], cexp, sexp, latent=True, name="qkv_prep",
                                     rows_all=n + lc)
    k_all, v_ext = qkv_prep_fwd(pc, None, ev_k_norm[0], None, None, latent=False, name="kv_prep_ctx",
                                rows_all=n + lc, k_all=k_all, v_ext=v_ext)
    o_attn, mix_a, lse = flash_fwd(q_r, k_all, v_ext, p0, za_block=za_block, mix_width=ATTN_W + conv_ch,
                                   name="flash_fwd")
    ycv, mix0 = conv_fwd(p0, dw_w_full, ev_dw_b[0], ev_ln_g[0], ev_ln_b[0], mix_a, glu_block=glu_block,
                         name="conv_fwd")
    wo0 = landed(h_wo0, ycv, "gather_w_out0_wait").reshape(-1, d)
    x1, r0, h1 = matmul(mix0, wo0, mode="nn", tm=512, tn=d, tk=mix0.shape[1], out_dtype=F32, name="proj_out0",
                        res=xs, gate=_row(gate[0]), norm=(norm_g[1], shift[1], scale[1]))

    wi1 =jnp.moveaxis(landed(h_wi1, h1, "gather_w_in1_wait"), 0, 1).reshape(d, od_in)
    p1 = matmul(h1, wi1, mode="nn", tm=1024, tn=od_in // 4, tk=d, out_dtype=F32, name="proj_in1")
    ws_bf = od_ws[0]
    bs_t = od_bs[0].T
    m1 = sgu_fwd(p1, od_ln_g_full, od_ln_b_full, ws_bf, bs_t, name="sgu_fwd")
    wo1 = landed(h_wo1, m1, "gather_w_out1_wait").reshape(-1, d)
    dx2, dr1, acc_final, loss_tile = proj_out_loss(m1, wo1, x1, gate[1], tgt, final_g, name="proj_out1_loss")

    dm1 = matmul(dr1, wo1, mode="nt", tm=1024, tn=1024, tk=d, out_dtype=F32, name="d_mix1")
    dwo1 = matmul(m1, dr1, mode="tn", tm=1024, tn=1024, tk=2048, out_dtype=BF16, name="d_wout1")
    dp1, dws, dbs_t, acc_sgu = sgu_bwd(p1, dm1, od_ln_g_full, od_ln_b_full, ws_bf, jnp.swapaxes(ws_bf, 1, 2), bs_t,
                                       name="sgu_bwd")
    dwi1_s = matmul(h1, dp1, mode="tn", tm=2048, tn=od_in // N_DEV, tk=2048, out_dtype=BF16, name="d_win1",
                    split_out=True)
    dwo1_s = dwo1.reshape(N_DEV, -1, d)
    (h_gi1, h_go1), g1_token = exchange_start([dwi1_s, dwo1_s], scatter=True, name="grads1_start")
    dh1 = matmul(dp1, wi1, mode="nt", tm=1024, tn=512, tk=od_in, out_dtype=F32, name="d_h1")
    dx1, dr0, acc_norm1 = norm_mod_bwd(x1, dh1, dx2, norm_g[1], _after(g1_token, scale[1]), name="norm_mod_bwd1",
                                       branch=(r0, gate[0]))

    dmix0 = matmul(dr0, wo0, mode="nt", tm=1024, tn=1024, tk=d, out_dtype=F32, name="d_mix0")
    dwo0 = matmul(mix0, dr0, mode="tn", tm=1024, tn=1024, tk=2048, out_dtype=BF16, name="d_wout0")
    do_attn, do_s, dza, delta = attn_gate_bwd(dmix0, o_attn, p0, za_block=za_block, name="attn_gate_bwd")
    dycv, dzb, acc_ln = conv_bwd_rows(dmix0, ycv, p0, ev_ln_g[0], ev_ln_b[0], mix_block=ATTN_W // conv_ch,
                                      zb_block=zb_block, name="conv_bwd_rows")
    dglu, ddw_w, acc_dwb = conv_bwd_taps(dycv, p0, dw_w_full, glu_block=glu_block, name="conv_bwd_taps")
    dq_r, dk_all, dv_all = flash_bwd(q_r, do_attn, do_s, lse, delta, k_all, v_ext, name="flash_bwd")
    dkvq, acc_qk = qkv_prep_bwd(p0, dq_r, dk_all, dv_all, ev_q_norm[0], ev_k_norm[0], cexp, sexp,
                                latent=True, name="qkv_prep_bwd")
    dpc, acc_kc = qkv_prep_bwd(pc, None, dk_all, dv_all, None, ev_k_norm[0], None, None,
                               latent=False, name="kv_prep_ctx_bwd")
    dp0 = jnp.concatenate([dkvq, dza, dglu, dzb], axis=1)
    dwi0 = matmul(h0, dp0, mode="tn", tm=1024, tn=ev_in // 4, tk=2048, out_dtype=F32, name="d_win0")
    dwi0 = matmul(hc, dpc, mode="tn", tm=512, tn=2 * KV_W, tk=lc, out_dtype=F32, name="d_win0_ctx", add=dwi0)
    dwi0_s = jnp.moveaxis(dwi0.astype(BF16).reshape(d, N_DEV, ev_in // N_DEV), 1, 0)
    dwo0_s = dwo0.reshape(N_DEV, -1, d)
    (h_gi0, h_go0), g0_token = exchange_start([dwi0_s, dwo0_s], scatter=True, name="grads0_start")
    dh0 = matmul(dp0, wi0, mode="nt", tm=1024, tn=512, tk=ev_in, out_dtype=F32, name="d_h0", after=g0_token)
    dhc = matmul(dpc, wi0, mode="nt", tm=lc, tn=512, tk=2 * KV_W, out_dtype=F32, name="d_h_ctx")
    grad_x, acc_norm0 = norm_mod_bwd(xs, dh0, dx1, norm_g[0], _after(g0_token, scale[0]), name="norm_mod_bwd0")
    _, acc_normc = norm_mod_bwd(ctxs, dhc, None, norm_g[0], scale_c, name="norm_mod_bwd_ctx")

    zeros_d = jnp.zeros((d,), F32)
    dmod0 = jnp.stack([acc_norm0[0], acc_norm0[1], acc_norm1[GATE_ROW]])
    dmod1 = jnp.stack([acc_norm1[0], acc_norm1[1], acc_final[GATE_ROW]])
    dmodc = jnp.stack([acc_normc[0], acc_normc[1]])
    half_pad = jnp.zeros((d - 2 * conv_ch,), F32) if d > 2 * conv_ch else jnp.zeros((0,), F32)
    row_a = jnp.concatenate([acc_dwb[0], acc_ln[0], half_pad])
    row_b = jnp.concatenate([acc_ln[1], acc_qk[0], acc_qk[1] + acc_kc[1],
                             jnp.zeros((d - conv_ch - 2 * HEAD_DIM,), F32)])
    row_c = jnp.concatenate([dbs_t.T.reshape(-1), jnp.zeros((d - SGU_GROUPS * CHUNK,), F32)])
    row_loss = jnp.concatenate([loss_tile[0, :1], jnp.zeros((d - 1,), F32)])
    pack = jnp.concatenate([
        dmod0, dmod1, dmodc,
        (acc_norm0[2] + acc_normc[2])[None], acc_norm1[2][None],
        acc_final[0][None],
        acc_sgu[0][None], acc_sgu[1][None],
        row_a[None], row_b[None], row_c[None], row_loss[None],
        ddw_w.reshape(-1, d),
        dws.reshape(-1, d),
    ], axis=0)
    n_rows = pack.shape[0]
    pack = _pad_rows(pack, -(-n_rows // 8) * 8)
    (h_pack,), pack_token = exchange_start([pack], scatter=False, name="small_grads_start")

    def summands(handle, after, name):
        mine, land = exchange_wait(handle, after, scatter=True, name=name)
        own = lax.dynamic_index_in_dim(mine, me, axis=0, keepdims=True)
        return lax.dynamic_update_slice_in_dim(land, own, me, axis=0)

    out = {}

    def upd(key, w, g, m, v, slots=False, rows=512):
        shp = w.shape
        w2 = w.reshape(-1, shp[-1])
        g2 = g.reshape((N_DEV, -1, shp[-1])) if slots else g.reshape(-1, shp[-1])
        res = adamw(w2, g2, m.reshape(w2.shape), v.reshape(w2.shape), name="adamw_" + key, slots=slots, rows=rows)
        out[key] = tuple(r.reshape(shp) for r in res)

    gi1 = summands(h_gi1, pack_token, "grads_w_in1_wait")
    upd("od_w_in", od_w_in, gi1, m_od_w_in, v_od_w_in, slots=True, rows=256)
    go1 = summands(h_go1, out["od_w_in"][1], "grads_w_out1_wait")
    upd("od_w_out", od_w_out, go1, m_od_w_out, v_od_w_out, slots=True, rows=256)
    go0 = summands(h_go0, out["od_w_out"][1], "grads_w_out0_wait")
    upd("ev_w_out", ev_w_out, go0, m_ev_w_out, v_ev_w_out, slots=True, rows=256)
    gi0 = summands(h_gi0, out["ev_w_out"][1], "grads_w_in0_wait")
    upd("ev_w_in", ev_w_in, gi0, m_ev_w_in, v_ev_w_in, slots=True, rows=256)

    pack_own, pack_land = exchange_wait(h_pack, out["ev_w_in"][1], scatter=False, name="small_grads_wait")
    pack_g = lax.dynamic_update_slice_in_dim(pack_land, pack_own[None], me, axis=0)
    gsum = slot_sum(pack_g, name="sum_small_grads")
    loss = gsum[16, 0]
    dw_rows_n = 32 * conv_ch // d
    g_dw_w = gsum[17:17 + dw_rows_n].reshape(32, conv_ch)[:CONV_WIDTH]
    g_od_ws = gsum[17 + dw_rows_n:17 + dw_rows_n + SGU_GROUPS * CHUNK * CHUNK // d].reshape(od_ws.shape)

    dmodc_sum = jnp.concatenate([gsum[6], gsum[7], zeros_d])
    col0 = me * ada_cols
    dm_cols = []
    for l in range(2):
        rows = pack_g[:, 3 * l:3 * l + 3, :].reshape(N_DEV, 3 * d)
        extra = dmodc_sum[None] if l == 0 else jnp.zeros((1, 3 * d), F32)
        full = _pad_rows(jnp.concatenate([rows, extra], axis=0), 16)
        dm_cols.append(lax.dynamic_slice_in_dim(full, col0, ada_cols, axis=1))
    g_ada_w = jnp.stack([matmul(sc16, dm_cols[l], mode="tn", tm=512, tn=ada_cols, tk=16, out_dtype=F32,
                                name=f"d_ada_w{l}") for l in range(2)])
    dsc = matmul(dm_cols[0], ada_bf[0], mode="nt", tm=16, tn=512, tk=ada_cols, out_dtype=F32, name="d_scc")
    (_, dscc_sum) = all_gather([dsc[8:16]], name="gather_dscc", sum_out=True)
    sg = jax.nn.sigmoid(c_ctx)
    g_c_ctx = dscc_sum[0] * (sg * (1.0 + c_ctx * (1.0 - sg)))
    g_ada_b = jnp.stack([gsum[0:3].reshape(-1) + dmodc_sum, gsum[3:6].reshape(-1)])

    upd("ada_w", ada_w, g_ada_w, m_ada_w, v_ada_w)

    def my_shard(full, size):
        return lax.dynamic_slice_in_dim(full, me * size, size, axis=full.ndim - 1)

    small_items = [
        ("c_ctx", c_ctx, g_c_ctx, m_c_ctx, v_c_ctx),
        ("ada_b", ada_b, g_ada_b, m_ada_b, v_ada_b),
        ("norm_g", norm_g, gsum[8:10], m_norm_g, v_norm_g),
        ("ev_q_norm", ev_q_norm, gsum[14, conv_ch:conv_ch + HEAD_DIM], m_ev_q_norm, v_ev_q_norm),
        ("ev_k_norm", ev_k_norm, gsum[14, conv_ch + HEAD_DIM:conv_ch + 2 * HEAD_DIM], m_ev_k_norm, v_ev_k_norm),
        ("ev_dw_w", ev_dw_w, my_shard(g_dw_w, conv_ch // N_DEV), m_ev_dw_w, v_ev_dw_w),
        ("ev_dw_b", ev_dw_b, gsum[13, :conv_ch], m_ev_dw_b, v_ev_dw_b),
        ("ev_ln_g", ev_ln_g, gsum[13, conv_ch:2 * conv_ch], m_ev_ln_g, v_ev_ln_g),
        ("ev_ln_b", ev_ln_b, gsum[14, :conv_ch], m_ev_ln_b, v_ev_ln_b),
        ("od_ln_g", od_ln_g, my_shard(gsum[11], shard), m_od_ln_g, v_od_ln_g),
        ("od_ln_b", od_ln_b, my_shard(gsum[12], shard), m_od_ln_b, v_od_ln_b),
        ("od_ws", od_ws, g_od_ws, m_od_ws, v_od_ws),
        ("od_bs", od_bs, gsum[15, :SGU_GROUPS * CHUNK], m_od_bs, v_od_bs),
        ("final_g", final_g, gsum[10], m_final_g, v_final_g),
    ]
    sizes = [it[1].size for it in small_items]
    total = sum(sizes)
    lanes = 1024
    prow = -(-total // lanes)
    prow = -(-prow // 8) * 8

    def pack_small(idx):
        flat = jnp.concatenate([it[idx].reshape(-1).astype(F32) for it in small_items])
        return jnp.pad(flat, (0, prow * lanes - total)).reshape(prow, lanes)

    sres = adamw(pack_small(1), pack_small(2), pack_small(3), pack_small(4), name="adamw_small", rows=prow)
    off = 0
    for it, size in zip(small_items, sizes):
        out[it[0]] = tuple(r.reshape(-1)[off:off + size].reshape(it[1].shape) for r in sres)
        off += size

    names = ['c_ctx', 'ada_w', 'ada_b', 'norm_g', 'ev_w_in', 'ev_q_norm', 'ev_k_norm', 'ev_dw_w', 'ev_dw_b',
             'ev_ln_g', 'ev_ln_b', 'ev_w_out', 'od_w_in', 'od_ln_g', 'od_ln_b', 'od_ws', 'od_bs', 'od_w_out',
             'final_g']
    return (loss, grad_x[None], *[out[k][0] for k in names], *[out[k][1] for k in names],
            *[out[k][2] for k in names], *[out[k][3] for k in names])
```

```python
import functools
import math

import jax
import jax.numpy as jnp
from jax import lax
from jax.experimental import pallas as pl
from jax.experimental.pallas import tpu as pltpu

F32 = jnp.float32
BF16 = jnp.bfloat16
MESH = pl.DeviceIdType.MESH

EPS = 1e-6
HEAD_DIM = 128
N_Q_HEADS = 8
N_KV_HEADS = 2
Q_PER_KV = N_Q_HEADS // N_KV_HEADS
ATTN_W = N_Q_HEADS * HEAD_DIM
KV_W = N_KV_HEADS * HEAD_DIM
ATTN_SCALE = HEAD_DIM ** -0.5
LN2 = math.log(2.0)
Q_SCALE = ATTN_SCALE / LN2
ROPE_THETA = 10000.0
GRID_W = 64
CONV_WIDTH = 31
HALO = 16
CHUNK = 128
SGU_GROUPS = 8
N_DEV = 8

ADAM_LR = 0.001
ADAM_B1 = 0.9
ADAM_B2 = 0.999
ADAM_EPS = 1e-08
ADAM_WD = 0.01
ADAM_STEP = 10

VMEM_LIMIT = 56 * 1024 * 1024
ANY = pl.BlockSpec(memory_space=pl.ANY)
VMEM_SPEC = pl.BlockSpec(memory_space=pltpu.VMEM)


def _params(*sem):
    return pltpu.CompilerParams(dimension_semantics=sem, vmem_limit_bytes=VMEM_LIMIT)


def _sigmoid(x):
    return 1.0 / (1.0 + jnp.exp(-x))


def _silu(x):
    return x * _sigmoid(x)


def _dsilu(x):
    s = _sigmoid(x)
    return s * (1.0 + x * (1.0 - s))


_GELU_C = math.sqrt(2.0 / math.pi)


def _gelu(x):
    t = jnp.tanh(_GELU_C * (x + 0.044715 * (x * x * x)))
    return 0.5 * x * (1.0 + t)


def _dgelu(x):
    t = jnp.tanh(_GELU_C * (x + 0.044715 * (x * x * x)))
    return 0.5 * (1.0 + t) + 0.5 * x * (1.0 - t * t) * (_GELU_C * (1.0 + 3.0 * 0.044715 * (x * x)))


def _row(v):
    return v.reshape(1, -1).astype(F32)


def _flat_id(p):
    return 4 * p[0] + 2 * p[1] + p[2]


def _gather_body(n_arr, sum_out):
    def body(*refs):
        x_refs = refs[:n_arr]
        out_refs = refs[n_arr:2 * n_arr]
        pos = 2 * n_arr
        sum_refs = refs[pos:pos + n_arr] if sum_out else ()
        pos += n_arr if sum_out else 0
        send_sems, recv_sems, local_sems = refs[pos:pos + 3]
        x, y, c = lax.axis_index("x"), lax.axis_index("y"), lax.axis_index("c")
        me, sibling = (x, y, c), (x, y, 1 - c)
        chips = [(1 - x, y), (x, 1 - y), (1 - x, 1 - y)]

        def copy(a, k, block, to, src=None):
            rows = out_refs[a].at[_flat_id(block)]
            return pltpu.make_async_remote_copy(
                src_ref=rows if src is None else src, dst_ref=rows,
                send_sem=send_sems.at[a, k], recv_sem=recv_sems.at[a, k],
                device_id=to, device_id_type=MESH)

        sends = []
        mine = []
        for a in range(n_arr):
            cp = pltpu.make_async_copy(x_refs[a], out_refs[a].at[_flat_id(me)], local_sems.at[a])
            cp.start()
            mine.append(cp)
            first = [copy(a, 0, me, sibling, src=x_refs[a])]
            first += [copy(a, 1 + j, me, (*chip, c), src=x_refs[a]) for j, chip in enumerate(chips)]
            for cp in first:
                cp.start()
            sends += first
        for a in range(n_arr):
            for j, chip in enumerate(chips):
                copy(a, 1 + j, (*chip, c), me).wait_recv()
                fwd = copy(a, 4 + j, (*chip, c), sibling)
                fwd.start()
                sends.append(fwd)
        for a in range(n_arr):
            copy(a, 0, sibling, me).wait_recv()
            for j, chip in enumerate(chips):
                copy(a, 4 + j, (*chip, 1 - c), me).wait_recv()
        for cp in sends:
            cp.wait_send()
        for cp in mine:
            cp.wait()
        if sum_out:
            for a in range(n_arr):
                acc = out_refs[a][0]
                for k in range(1, N_DEV):
                    acc = acc + out_refs[a][k]
                sum_refs[a][...] = acc

    return body


def all_gather(arrs, *, name, sum_out=False):
    n = len(arrs)
    spec = VMEM_SPEC
    out_shape = [jax.ShapeDtypeStruct((N_DEV,) + a.shape, a.dtype) for a in arrs]
    out_specs = [spec] * n
    if sum_out:
        out_shape += [jax.ShapeDtypeStruct(a.shape, a.dtype) for a in arrs]
        out_specs += [VMEM_SPEC] * n
    res = pl.pallas_call(
        _gather_body(n, sum_out), name=name,
        out_shape=out_shape, in_specs=[spec] * n, out_specs=out_specs,
        scratch_shapes=[pltpu.SemaphoreType.DMA((n, 7)), pltpu.SemaphoreType.DMA((n, 7)),
                        pltpu.SemaphoreType.DMA((n,))],
        compiler_params=pltpu.CompilerParams(vmem_limit_bytes=VMEM_LIMIT),
    )(*arrs)
    return res


_RELATIONS = [(rx, ry, rc) for rx in (0, 1) for ry in (0, 1) for rc in (0, 1)][1:]


HBM_SPEC = pl.BlockSpec(memory_space=pltpu.HBM)
SEM_SPEC = pl.BlockSpec(memory_space=pltpu.SEMAPHORE)
_DATAFLOW = pltpu.SideEffectType.DATAFLOW_SIDE_EFFECTING
N_PEERS = N_DEV - 1


def _peer_copy(src_ref, land_ref, send_sem, recv_sem, k, rel, scatter, sending):
    x, y, c = lax.axis_index("x"), lax.axis_index("y"), lax.axis_index("c")
    rx, ry, rc = rel
    peer = (1 - x if rx else x, 1 - y if ry else y, 1 - c if rc else c)
    src = src_ref.at[_flat_id(peer)] if scatter else src_ref
    dst = land_ref.at[_flat_id((x, y, c)) if sending else _flat_id(peer)]
    return pltpu.make_async_remote_copy(src_ref=src, dst_ref=dst, send_sem=send_sem.at[k], recv_sem=recv_sem.at[k],
                                        device_id=peer, device_id_type=MESH)


def exchange_start(arrs, *, scatter, name, after=None):
    n = len(arrs)
    lands = [lax.empty((N_DEV,) + (a.shape[1:] if scatter else a.shape), a.dtype) for a in arrs]
    n_in = 2 * n + (after is not None)

    def body(*refs):
        srcs, lnds, sems = refs[:n], refs[n:2 * n], refs[n_in:n_in + 2 * n]
        token = refs[n_in + 4 * n]
        for a in range(n):
            for k, rel in enumerate(_RELATIONS):
                _peer_copy(srcs[a], lnds[a], sems[2 * a], sems[2 * a + 1], k, rel, scatter, True).start()
        token[...] = jnp.zeros_like(token)

    outs = pl.pallas_call(
        body, name=name,
        out_shape=[pltpu.SemaphoreType.DMA((N_PEERS,))] * (2 * n)
        + [pltpu.HBM(a.shape, a.dtype) for a in arrs] + [pltpu.HBM(l.shape, l.dtype) for l in lands]
        + [jax.ShapeDtypeStruct((8, 128), F32)],
        in_specs=[HBM_SPEC] * (2 * n) + ([ANY] if after is not None else []),
        out_specs=[SEM_SPEC] * (2 * n) + [HBM_SPEC] * (2 * n) + [VMEM_SPEC],
        input_output_aliases={i: 2 * n + i for i in range(2 * n)},
        compiler_params=pltpu.CompilerParams(has_side_effects=_DATAFLOW),
    )(*[pltpu.with_memory_space_constraint(a, pltpu.HBM) for a in arrs],
      *[pltpu.with_memory_space_constraint(l, pltpu.HBM) for l in lands],
      *([after] if after is not None else []))
    handles = [(outs[2 * a], outs[2 * a + 1], outs[2 * n + a], outs[3 * n + a]) for a in range(n)]
    return handles, outs[4 * n]


def exchange_wait(handle, after, *, scatter, name):
    send_sem, recv_sem, src, land = handle

    def body(src_ref, land_ref, send_ref, recv_ref, after_ref, src_out, land_out):
        for k, rel in enumerate(_RELATIONS):
            cp = _peer_copy(src_ref, land_ref, send_ref, recv_ref, k, rel, scatter, False)
            cp.wait_send()
            cp.wait_recv()

    outs = pl.pallas_call(
        body, name=name,
        out_shape=[pltpu.HBM(src.shape, src.dtype), pltpu.HBM(land.shape, land.dtype)],
        in_specs=[HBM_SPEC, HBM_SPEC, SEM_SPEC, SEM_SPEC, ANY],
        out_specs=[HBM_SPEC, HBM_SPEC], input_output_aliases={0: 0, 1: 1},
        compiler_params=pltpu.CompilerParams(has_side_effects=_DATAFLOW),
    )(src, land, send_sem, recv_sem, after)
    return outs[0], outs[1]


_STAGE1 = [(0, 0, 1), (1, 0, 0), (0, 1, 0), (1, 1, 0)]
_OTHER_CHIPS = [(1, 0), (0, 1), (1, 1)]


def _stage_copy(stage, k, shard_ref, land_ref, send_sem, recv_sem, sending):
    x, y, c = lax.axis_index("x"), lax.axis_index("y"), lax.axis_index("c")
    if stage == 1:
        rx, ry, rc = _STAGE1[k]
        peer = (1 - x if rx else x, 1 - y if ry else y, 1 - c if rc else c)
        src = shard_ref
        dst = land_ref.at[_flat_id((x, y, c)) if sending else _flat_id(peer)]
    else:
        cx, cy = _OTHER_CHIPS[k]
        peer = (x, y, 1 - c)
        chip = (1 - x if cx else x, 1 - y if cy else y)
        slot = _flat_id((*chip, c)) if sending else _flat_id((*chip, 1 - c))
        src = land_ref.at[_flat_id((*chip, c))]
        dst = land_ref.at[slot]
    return pltpu.make_async_remote_copy(src_ref=src, dst_ref=dst, send_sem=send_sem.at[k], recv_sem=recv_sem.at[k],
                                        device_id=peer, device_id_type=MESH)


def staged_gather_start(stage, shard, land, *, name):
    n_copies = len(_STAGE1) if stage == 1 else len(_OTHER_CHIPS)
    if land is None:
        land = lax.empty((N_DEV,) + shard.shape, shard.dtype)

    def body(shard_ref, land_ref, send_sem, recv_sem, shard_thru, land_thru, token):
        for k in range(n_copies):
            _stage_copy(stage, k, shard_ref, land_ref, send_sem, recv_sem, True).start()
        token[...] = jnp.zeros_like(token)

    outs = pl.pallas_call(
        body, name=name,
        out_shape=[pltpu.SemaphoreType.DMA((n_copies,)), pltpu.SemaphoreType.DMA((n_copies,)),
                   pltpu.HBM(shard.shape, shard.dtype), pltpu.HBM(land.shape, land.dtype),
                   jax.ShapeDtypeStruct((8, 128), F32)],
        in_specs=[HBM_SPEC, HBM_SPEC],
        out_specs=[SEM_SPEC, SEM_SPEC, HBM_SPEC, HBM_SPEC, VMEM_SPEC],
        input_output_aliases={0: 2, 1: 3},
        compiler_params=pltpu.CompilerParams(has_side_effects=_DATAFLOW),
    )(pltpu.with_memory_space_constraint(shard, pltpu.HBM), pltpu.with_memory_space_constraint(land, pltpu.HBM))
    return outs[:4], outs[4]


def staged_gather_wait(stage, handle, after, *, name):
    send_sem, recv_sem, shard, land = handle
    n_copies = len(_STAGE1) if stage == 1 else len(_OTHER_CHIPS)

    def body(shard_ref, land_ref, send_ref, recv_ref, after_ref, shard_out, land_out):
        for k in range(n_copies):
            cp = _stage_copy(stage, k, shard_ref, land_ref, send_ref, recv_ref, False)
            cp.wait_send()
            cp.wait_recv()

    outs = pl.pallas_call(
        body, name=name,
        out_shape=[pltpu.HBM(shard.shape, shard.dtype), pltpu.HBM(land.shape, land.dtype)],
        in_specs=[HBM_SPEC, HBM_SPEC, SEM_SPEC, SEM_SPEC, ANY],
        out_specs=[HBM_SPEC, HBM_SPEC], input_output_aliases={0: 0, 1: 1},
        compiler_params=pltpu.CompilerParams(has_side_effects=_DATAFLOW),
    )(shard, land, send_sem, recv_sem, after)
    return outs[0], outs[1]


_DIMS = {"nn": (((1,), (0,)), ((), ())), "nt": (((1,), (1,)), ((), ())), "tn": (((0,), (0,)), ((), ()))}


def matmul(a, b, *, mode, tm, tn, tk, out_dtype, name, n_out=None, res=None, gate=None, add=None, after=None,
           split_out=False, norm=None):
    if mode == "tn":
        kdim, m = a.shape
    else:
        m, kdim = a.shape
    nfull = b.shape[0] if mode == "nt" else b.shape[1]
    n = nfull if n_out is None else n_out
    tm, tn, tk = min(tm, m), min(tn, n), min(tk, kdim)
    assert m % tm == 0 and n % tn == 0 and kdim % tk == 0, (name, m, n, kdim, tm, tn, tk)
    nk = kdim // tk
    dims = _DIMS[mode]
    a_spec = pl.BlockSpec((tk, tm), lambda j, i, k: (k, i)) if mode == "tn" else pl.BlockSpec((tm, tk), lambda j, i, k: (i, k))
    b_spec = pl.BlockSpec((tn, tk), lambda j, i, k: (j, k)) if mode == "nt" else pl.BlockSpec((tk, tn), lambda j, i, k: (k, j))
    o_spec = pl.BlockSpec((tm, tn), lambda j, i, k: (i, j))
    in_specs = [a_spec, b_spec]
    operands = [a, b]
    aliases = {}
    vec = pl.BlockSpec((1, tn), lambda j, i, k: (0, j))
    if res is not None:
        in_specs += [o_spec, vec]
        operands += [res, gate]
    if norm is not None:
        assert res is not None and tn == n, (name, tn, n)
        in_specs += [vec, vec, vec]
        operands += [_row(v) for v in norm]
    if add is not None:
        in_specs += [o_spec]
        aliases = {len(operands): 0}
        operands += [add]
    if after is not None:
        in_specs += [ANY]
        operands += [after]
    out_cols = n if add is None else add.shape[1]
    out_shape = [jax.ShapeDtypeStruct((m, out_cols), out_dtype)]
    out_specs = [o_spec]
    if split_out:
        out_shape = [jax.ShapeDtypeStruct((n // tn, m, tn), out_dtype)]
        out_specs = [pl.BlockSpec((None, tm, tn), lambda j, i, k: (j, i, 0))]
    if res is not None:
        out_shape.append(jax.ShapeDtypeStruct((m, n), BF16))
        out_specs.append(o_spec)
    if norm is not None:
        out_shape.append(jax.ShapeDtypeStruct((m, n), BF16))
        out_specs.append(o_spec)

    def body(*refs):
        a_ref, b_ref = refs[:2]
        pos = 2
        if res is not None:
            res_ref, gate_ref = refs[pos:pos + 2]
            pos += 2
        if norm is not None:
            ng_ref, nshift_ref, nscale_ref = refs[pos:pos + 3]
            pos += 3
        if add is not None:
            add_ref = refs[pos]
            pos += 1
        if after is not None:
            pos += 1
        o_ref = refs[pos]
        pos += 1
        if res is not None:
            r_ref = refs[pos]
            pos += 1
        if norm is not None:
            h_ref = refs[pos]
            pos += 1
        acc_ref = refs[pos] if nk > 1 else None
        prod = lax.dot_general(a_ref[...].astype(BF16), b_ref[...].astype(BF16), dims,
                               preferred_element_type=F32)

        def finish(acc):
            if res is not None:
                xnew = res_ref[...] + gate_ref[...] * acc
                o_ref[...] = xnew.astype(out_dtype)
                r_ref[...] = acc.astype(BF16)
                if norm is not None:
                    rstd = lax.rsqrt(jnp.mean(xnew * xnew, axis=-1, keepdims=True) + EPS)
                    y = xnew * rstd * ng_ref[...]
                    h_ref[...] = (y * (1.0 + nscale_ref[...]) + nshift_ref[...]).astype(BF16)
            elif add is not None:
                o_ref[...] = (add_ref[...] + acc).astype(out_dtype)
            else:
                o_ref[...] = acc.astype(out_dtype)

        if nk == 1:
            finish(prod)
        else:
            k = pl.program_id(2)

            @pl.when(k == 0)
            def _():
                acc_ref[...] = prod

            @pl.when(k > 0)
            def _():
                acc_ref[...] += prod

            @pl.when(k == nk - 1)
            def _():
                finish(acc_ref[...])

    outs = pl.pallas_call(
        body, name=name, grid=(n // tn, m // tm, nk),
        in_specs=in_specs, out_specs=out_specs, out_shape=out_shape,
        scratch_shapes=[pltpu.VMEM((tm, tn), F32)] if nk > 1 else [],
        input_output_aliases=aliases,
        compiler_params=_params("parallel", "parallel", "arbitrary"),
    )(*operands)
    return outs if res is not None else outs[0]


def _rows_tile(n, want):
    return min(n, want)


def norm_mod_fwd(x, g, shift, scale, *, name):
    n, d = x.shape
    tt = _rows_tile(n, 512)

    def body(x_ref, g_ref, sh_ref, sc_ref, h_ref):
        xv = x_ref[...]
        rstd = lax.rsqrt(jnp.mean(xv * xv, axis=-1, keepdims=True) + EPS)
        y = xv * rstd * g_ref[...]
        h_ref[...] = (y * (1.0 + sc_ref[...]) + sh_ref[...]).astype(BF16)

    vec = pl.BlockSpec((1, d), lambda i: (0, 0))
    return pl.pallas_call(
        body, name=name, grid=(n // tt,),
        in_specs=[pl.BlockSpec((tt, d), lambda i: (i, 0)), vec, vec, vec],
        out_specs=pl.BlockSpec((tt, d), lambda i: (i, 0)),
        out_shape=jax.ShapeDtypeStruct((n, d), BF16),
        compiler_params=_params("parallel"),
    )(x, _row(g), _row(shift), _row(scale))


GATE_ROW = 3


def norm_mod_bwd(x, dh, dres, g, scale, *, name, branch=None):
    n, d = x.shape
    tt = _rows_tile(n, 256)
    has_res = dres is not None
    has_branch = branch is not None
    last = n // tt - 1

    def body(*refs):
        x_ref, dh_ref = refs[:2]
        pos = 2
        if has_res:
            dres_ref = refs[pos]
            pos += 1
        if has_branch:
            r_ref, gate_ref = refs[pos:pos + 2]
            pos += 2
        g_ref, sc_ref, dx_ref = refs[pos:pos + 3]
        pos += 3
        if has_branch:
            dr_ref = refs[pos]
            pos += 1
        acc_ref, s_ref = refs[pos:pos + 2]
        i = pl.program_id(0)
        xv = x_ref[...]
        dhv = dh_ref[...]
        rstd = lax.rsqrt(jnp.mean(xv * xv, axis=-1, keepdims=True) + EPS)
        xhat = xv * rstd
        dxhat = dhv * (g_ref[...] * (1.0 + sc_ref[...]))
        dx = rstd * (dxhat - xhat * jnp.mean(dxhat * xhat, axis=-1, keepdims=True))
        if has_res:
            dx = dx + dres_ref[...]
        dx_ref[...] = dx

        @pl.when(i == 0)
        def _():
            s_ref[...] = jnp.zeros_like(s_ref)

        s_ref[0:1, :] += jnp.sum(dhv, axis=0, keepdims=True)
        s_ref[1:2, :] += jnp.sum(dhv * xhat, axis=0, keepdims=True)
        if has_branch:
            dr_ref[...] = (dx * gate_ref[...]).astype(BF16)
            s_ref[2:3, :] += jnp.sum(dx * r_ref[...].astype(F32), axis=0, keepdims=True)

        @pl.when(i == last)
        def _():
            s1 = s_ref[0:1, :]
            s2 = s_ref[1:2, :]
            acc_ref[...] = jnp.zeros_like(acc_ref)
            acc_ref[0:1, :] = s1
            acc_ref[1:2, :] = s2 * g_ref[...]
            acc_ref[2:3, :] = s2 * (1.0 + sc_ref[...])
            acc_ref[GATE_ROW:GATE_ROW + 1, :] = s_ref[2:3, :]

    vec = pl.BlockSpec((1, d), lambda i: (0, 0))
    big = pl.BlockSpec((tt, d), lambda i: (i, 0))
    ops = [x, dh] + ([dres] if has_res else []) + ([branch[0], _row(branch[1])] if has_branch else [])
    ops += [_row(g), _row(scale)]
    return pl.pallas_call(
        body, name=name, grid=(n // tt,),
        in_specs=[big, big] + ([big] if has_res else []) + ([big, vec] if has_branch else []) + [vec, vec],
        out_specs=[big] + ([big] if has_branch else []) + [pl.BlockSpec((8, d), lambda i: (0, 0))],
        out_shape=[jax.ShapeDtypeStruct((n, d), F32)] + ([jax.ShapeDtypeStruct((n, d), BF16)] if has_branch else [])
        + [jax.ShapeDtypeStruct((8, d), F32)],
        scratch_shapes=[pltpu.VMEM((8, d), F32)],
        compiler_params=_params("arbitrary"),
    )(*ops)


def proj_out_loss(a, b, x_prev, gate, target, g, *, name):
    n, kdim = a.shape
    d = b.shape[1]
    tt = _rows_tile(n, 512)
    part_rows = min(tt, 256)

    def body(a_ref, b_ref, x_ref, t_ref, g_ref, gate_ref, dx_ref, dr_ref, acc_ref, loss_ref):
        i = pl.program_id(0)

        @pl.when(i == 0)
        def _():
            acc_ref[...] = jnp.zeros_like(acc_ref)
            loss_ref[...] = jnp.zeros_like(loss_ref)

        for h in range(tt // part_rows):
            rows = pl.ds(h * part_rows, part_rows)
            r = jnp.dot(a_ref[rows, :], b_ref[...], preferred_element_type=F32)
            xv = x_ref[rows, :] + gate_ref[...] * r
            rstd = lax.rsqrt(jnp.mean(xv * xv, axis=-1, keepdims=True) + EPS)
            xhat = xv * rstd
            e = xhat * g_ref[...] - t_ref[rows, :]
            dy = e * (1.0 / d)
            dxhat = dy * g_ref[...]
            dx = rstd * (dxhat - xhat * jnp.mean(dxhat * xhat, axis=-1, keepdims=True))
            dx_ref[rows, :] = dx
            dr_ref[rows, :] = (dx * gate_ref[...]).astype(BF16)
            acc_ref[0:1, :] += jnp.sum(dy * xhat, axis=0, keepdims=True)
            acc_ref[GATE_ROW:GATE_ROW + 1, :] += jnp.sum(dx * r, axis=0, keepdims=True)
            part = 0.5 * jnp.sum(jnp.mean(e * e, axis=-1, keepdims=True), axis=0, keepdims=True)
            loss_ref[...] += jnp.broadcast_to(part, loss_ref.shape)

    big = pl.BlockSpec((tt, d), lambda i: (i, 0))
    vec = pl.BlockSpec((1, d), lambda i: (0, 0))
    return pl.pallas_call(
        body, name=name, grid=(n // tt,),
        in_specs=[pl.BlockSpec((tt, kdim), lambda i: (i, 0)), pl.BlockSpec((kdim, d), lambda i: (0, 0)),
                  big, big, vec, vec],
        out_specs=[big, big, pl.BlockSpec((8, d), lambda i: (0, 0)), pl.BlockSpec((8, 128), lambda i: (0, 0))],
        out_shape=[jax.ShapeDtypeStruct((n, d), F32), jax.ShapeDtypeStruct((n, d), BF16),
                   jax.ShapeDtypeStruct((8, d), F32), jax.ShapeDtypeStruct((8, 128), F32)],
        compiler_params=_params("arbitrary"),
    )(a, b, x_prev, target, _row(g), _row(gate))


def _as_row(col):
    t = col.shape[0]
    return jnp.transpose(jnp.broadcast_to(col, (t, HEAD_DIM)))[0:1, :]


def _swap_pairs(x):
    lane = lax.broadcasted_iota(jnp.int32, x.shape, 1)
    return jnp.where(lane % 2 == 0, pltpu.roll(x, HEAD_DIM - 1, 1), pltpu.roll(x, 1, 1))


def rope_tables(n):
    rows = n // GRID_W
    row = jnp.repeat(jnp.arange(rows, dtype=F32), GRID_W)
    col = jnp.tile(jnp.arange(GRID_W, dtype=F32), rows)
    n_freq = HEAD_DIM // 4
    inv = jnp.power(ROPE_THETA, jnp.arange(n_freq, dtype=F32) * (-2.0 / (HEAD_DIM // 2)))
    ang = jnp.concatenate([row[:, None] * inv, col[:, None] * inv], axis=-1)
    cos, sin = jnp.cos(ang), jnp.sin(ang)
    cexp = jnp.repeat(cos, 2, axis=-1)
    sexp = jnp.stack([-sin, sin], axis=-1).reshape(n, HEAD_DIM)
    return cexp, sexp


V_EXT_W = 2 * HEAD_DIM


def qkv_prep_fwd(p, wq, wk, cexp, sexp, *, latent, name, rows_all=None, k_all=None, v_ext=None):
    n = p.shape[0]
    tt = _rows_tile(n, 1024)
    width = 2 * KV_W + (ATTN_W if latent else 0)

    def body(*refs):
        if latent:
            p_ref, wq_ref, wk_ref, c_ref, s_ref, q_ref, k_ref, v_ref = refs
        else:
            p_ref, wk_ref, _, _, k_ref, v_ref = refs

        def head(xv, w):
            rstd = lax.rsqrt(jnp.mean(xv * xv, axis=-1, keepdims=True) + EPS)
            yv = xv * rstd * w
            if latent:
                yv = yv * c_ref[...] + _swap_pairs(yv) * s_ref[...]
            return yv

        for h in range(N_KV_HEADS):
            sl = slice(h * HEAD_DIM, (h + 1) * HEAD_DIM)
            k_ref[:, sl] = head(p_ref[:, sl], wk_ref[...]).astype(BF16)
            v_ref[:, h * V_EXT_W:h * V_EXT_W + HEAD_DIM] = p_ref[:, KV_W + h * HEAD_DIM:KV_W + (h + 1) * HEAD_DIM].astype(BF16)
            lane = lax.broadcasted_iota(jnp.int32, (tt, HEAD_DIM), 1)
            v_ref[:, h * V_EXT_W + HEAD_DIM:(h + 1) * V_EXT_W] = jnp.where(lane == 0, 1.0, 0.0).astype(BF16)
        if latent:
            for h in range(N_Q_HEADS):
                sl = slice(2 * KV_W + h * HEAD_DIM, 2 * KV_W + (h + 1) * HEAD_DIM)
                q_ref[:, h * HEAD_DIM:(h + 1) * HEAD_DIM] = (head(p_ref[:, sl], wq_ref[...]) * Q_SCALE).astype(BF16)

    vec = pl.BlockSpec((1, HEAD_DIM), lambda i: (0, 0))
    tab = pl.BlockSpec((tt, HEAD_DIM), lambda i: (i, 0))
    vw = N_KV_HEADS * V_EXT_W
    k_shape = jax.ShapeDtypeStruct((rows_all, KV_W), BF16)
    v_shape = jax.ShapeDtypeStruct((rows_all, vw), BF16)
    aliases = {}
    if latent:
        in_specs = [pl.BlockSpec((tt, width), lambda i: (i, 0)), vec, vec, tab, tab]
        ops = [p, _row(wq), _row(wk), cexp, sexp]
        out_specs = [pl.BlockSpec((tt, ATTN_W), lambda i: (i, 0)), pl.BlockSpec((tt, KV_W), lambda i: (i, 0)),
                     pl.BlockSpec((tt, vw), lambda i: (i, 0))]
        out_shape = [jax.ShapeDtypeStruct((n, ATTN_W), BF16), k_shape, v_shape]
    else:
        assert n == tt and (rows_all - n) % n == 0, (n, tt, rows_all)
        first = (rows_all - n) // n
        in_specs = [pl.BlockSpec((tt, width), lambda i: (i, 0)), vec, ANY, ANY]
        ops = [p, _row(wk), k_all, v_ext]
        out_specs = [pl.BlockSpec((tt, KV_W), lambda i: (first, 0)), pl.BlockSpec((tt, vw), lambda i: (first, 0))]
        out_shape = [k_shape, v_shape]
        aliases = {2: 0, 3: 1}
    return pl.pallas_call(
        body, name=name, grid=(n // tt,), in_specs=in_specs, out_specs=out_specs, out_shape=out_shape,
        input_output_aliases=aliases, compiler_params=_params("parallel"),
    )(*ops)


def qkv_prep_bwd(p, dq, dk, dv, wq, wk, cexp, sexp, *, latent, name):
    n = p.shape[0]
    tt = _rows_tile(n, 1024)
    width = 2 * KV_W + (ATTN_W if latent else 0)

    def body(*refs):
        if latent:
            p_ref, dq_ref, dk_ref, dv_ref, wq_ref, wk_ref, c_ref, s_ref, dp_ref, acc_ref = refs
        else:
            p_ref, dk_ref, dv_ref, wk_ref, dp_ref, acc_ref = refs
        i = pl.program_id(0)

        @pl.when(i == 0)
        def _():
            acc_ref[...] = jnp.zeros_like(acc_ref)

        def head(xv, dy, w, row):
            if latent:
                dy = dy * c_ref[...] + _swap_pairs(dy * s_ref[...])
            rstd = lax.rsqrt(jnp.mean(xv * xv, axis=-1, keepdims=True) + EPS)
            xhat = xv * rstd
            acc_ref[row:row + 1, :] += jnp.sum(dy * xhat, axis=0, keepdims=True)
            dxhat = dy * w
            return rstd * (dxhat - xhat * jnp.mean(dxhat * xhat, axis=-1, keepdims=True))

        for h in range(N_KV_HEADS):
            sl = slice(h * HEAD_DIM, (h + 1) * HEAD_DIM)
            dp_ref[:, sl] = head(p_ref[:, sl], dk_ref[:, sl], wk_ref[...], 1).astype(BF16)
        dp_ref[:, KV_W:2 * KV_W] = dv_ref[...].astype(BF16)
        if latent:
            for h in range(N_Q_HEADS):
                sl = slice(2 * KV_W + h * HEAD_DIM, 2 * KV_W + (h + 1) * HEAD_DIM)
                dyq = dq_ref[:, h * HEAD_DIM:(h + 1) * HEAD_DIM] * Q_SCALE
                dp_ref[:, sl] = head(p_ref[:, sl], dyq, wq_ref[...], 0).astype(BF16)

    vec = pl.BlockSpec((1, HEAD_DIM), lambda i: (0, 0))
    tab = pl.BlockSpec((tt, HEAD_DIM), lambda i: (i, 0))
    first = 0 if latent else (dk.shape[0] - n) // tt
    kv_spec = pl.BlockSpec((tt, KV_W), lambda i: (first + i, 0))
    p_spec = pl.BlockSpec((tt, width), lambda i: (i, 0))
    if latent:
        in_specs = [p_spec, pl.BlockSpec((tt, ATTN_W), lambda i: (i, 0)), kv_spec, kv_spec, vec, vec, tab, tab]
        ops = [p, dq, dk, dv, _row(wq), _row(wk), cexp, sexp]
    else:
        in_specs = [p_spec, kv_spec, kv_spec, vec]
        ops = [p, dk, dv, _row(wk)]
    return pl.pallas_call(
        body, name=name, grid=(n // tt,), in_specs=in_specs,
        out_specs=[p_spec, pl.BlockSpec((8, HEAD_DIM), lambda i: (0, 0))],
        out_shape=[jax.ShapeDtypeStruct((n, width), BF16), jax.ShapeDtypeStruct((8, HEAD_DIM), F32)],
        compiler_params=_params("arbitrary"),
    )(*ops)


def _kv_chunks(n, s_all, want=1024):
    step = want if n % want == 0 else 256
    chunks = [(s, step) for s in range(0, n, step)]
    if s_all > n:
        chunks.append((n, s_all - n))
    return chunks


def flash_fwd(q, k_all, v_ext, p, *, za_block, mix_width, name):
    n = q.shape[0]
    s_all = k_all.shape[0]
    tq = _rows_tile(n, 512)
    chunks = _kv_chunks(n, s_all)
    wide = 2 * HEAD_DIM

    def body(q_ref, k_ref, v_ref, za_ref, o_ref, mix_ref, lse_ref, m_ref, acc_ref):
        qv = q_ref[...]
        m_ref[...] = jnp.full_like(m_ref, -jnp.inf)
        acc_ref[...] = jnp.zeros_like(acc_ref)
        for start, size in chunks:
            kc = k_ref[pl.ds(start, size), :]
            vc = v_ref[pl.ds(start, size), :]
            s = lax.dot_general(qv, kc, _DIMS["nt"], preferred_element_type=F32)
            m_old = m_ref[...]
            m_new = jnp.maximum(m_old, jnp.max(s, axis=-1, keepdims=True))
            pr = jnp.exp2(s - m_new)
            alpha = jnp.exp2(m_old - m_new)
            acc_ref[...] = alpha * acc_ref[...] + jnp.dot(pr.astype(BF16), vc, preferred_element_type=F32)
            m_ref[...] = m_new
        acc = acc_ref[...]
        denom = acc[:, HEAD_DIM:HEAD_DIM + 1]
        o = acc[:, :HEAD_DIM] / denom
        o_ref[...] = o.astype(BF16)
        mix_ref[...] = (o * _silu(za_ref[...])).astype(BF16)
        lse_ref[0] = _as_row(m_ref[...] + jnp.log2(denom))

    qspec = pl.BlockSpec((tq, HEAD_DIM), lambda h, i: (i, h))
    return pl.pallas_call(
        body, name=name, grid=(N_Q_HEADS, n // tq),
        in_specs=[qspec, pl.BlockSpec((s_all, HEAD_DIM), lambda h, i: (0, h // Q_PER_KV)),
                  pl.BlockSpec((s_all, wide), lambda h, i: (0, h // Q_PER_KV)),
                  pl.BlockSpec((tq, HEAD_DIM), lambda h, i: (i, za_block + h))],
        out_specs=[qspec, qspec, pl.BlockSpec((1, 1, tq), lambda h, i: (h, 0, i))],
        out_shape=[jax.ShapeDtypeStruct((n, ATTN_W), BF16), jax.ShapeDtypeStruct((n, mix_width), BF16),
                   jax.ShapeDtypeStruct((N_Q_HEADS, 1, n), F32)],
        scratch_shapes=[pltpu.VMEM((tq, 1), F32), pltpu.VMEM((tq, wide), F32)],
        compiler_params=_params("parallel", "parallel"),
    )(q, k_all, v_ext, p)


def attn_gate_bwd(dmix, o, p, *, za_block, name):
    n = o.shape[0]
    tt = _rows_tile(n, 1024)
    za_half = za_block * HEAD_DIM // HALF
    n_half = ATTN_W // HALF

    def body(*refs):
        dm_ref, o_ref = refs[:2]
        za_refs = refs[2:2 + n_half]
        do_ref, dos_ref, dza_ref, delta_ref = refs[2 + n_half:]
        dm = dm_ref[...]
        ov = o_ref[...].astype(F32)
        za = jnp.concatenate([r[...] for r in za_refs], axis=-1)
        do = dm * _silu(za)
        do_ref[...] = do.astype(BF16)
        dos_ref[...] = (do * LN2).astype(BF16)
        dza_ref[...] = (dm * ov * _dsilu(za)).astype(BF16)
        prod = do * ov
        for h in range(N_Q_HEADS):
            col = jnp.sum(prod[:, h * HEAD_DIM:(h + 1) * HEAD_DIM], axis=-1, keepdims=True) * LN2
            delta_ref[h] = _as_row(col)

    spec = pl.BlockSpec((tt, ATTN_W), lambda i: (i, 0))
    shape = jax.ShapeDtypeStruct((n, ATTN_W), BF16)
    za_specs = [pl.BlockSpec((tt, HALF), functools.partial(lambda i, cb: (i, cb), cb=za_half + h))
                for h in range(n_half)]
    return pl.pallas_call(
        body, name=name, grid=(n // tt,),
        in_specs=[spec, spec] + za_specs,
        out_specs=[spec, spec, spec, pl.BlockSpec((N_Q_HEADS, 1, tt), lambda i: (0, 0, i))],
        out_shape=[shape, shape, shape, jax.ShapeDtypeStruct((N_Q_HEADS, 1, n), F32)],
        compiler_params=_params("parallel"),
    )(dmix, o, *([p] * n_half))


def flash_bwd(q, do, do_s, lse_row, delta_row, k_all, v_ext, *, name):
    n = q.shape[0]
    s_all = k_all.shape[0]
    tq = _rows_tile(n, 1024)
    chunks = _kv_chunks(n, s_all)

    def body(q_ref, do_ref, dos_ref, lse_ref, dl_ref, k_ref, v_ref, dq_ref, dk_ref, dv_ref):
        g = pl.program_id(1)
        i = pl.program_id(2)

        @pl.when((g == 0) & (i == 0))
        def _():
            dk_ref[...] = jnp.zeros_like(dk_ref)
            dv_ref[...] = jnp.zeros_like(dv_ref)

        qv = q_ref[...]
        dov = do_ref[...]
        dosv = dos_ref[...]
        lse = lse_ref[0]
        dl = dl_ref[0]
        dq = jnp.zeros((tq, HEAD_DIM), F32)
        for start, size in chunks:
            kc = k_ref[pl.ds(start, size), :]
            vc = v_ref[pl.ds(start, size), :]
            st = lax.dot_general(kc, qv, _DIMS["nt"], preferred_element_type=F32)
            pt = jnp.exp2(st - lse)
            dpt = lax.dot_general(vc, dosv, _DIMS["nt"], preferred_element_type=F32)
            dst = (pt * (dpt - dl)).astype(BF16)
            dv_ref[pl.ds(start, size), :] += jnp.dot(pt.astype(BF16), dov, preferred_element_type=F32)
            dk_ref[pl.ds(start, size), :] += jnp.dot(dst, qv, preferred_element_type=F32)
            dq = dq + lax.dot_general(dst, kc, _DIMS["tn"], preferred_element_type=F32)
        dq_ref[...] = dq

    qspec = pl.BlockSpec((tq, HEAD_DIM), lambda kh, g, i: (i, kh * Q_PER_KV + g))
    rowspec = pl.BlockSpec((1, 1, tq), lambda kh, g, i: (kh * Q_PER_KV + g, 0, i))
    kvspec = pl.BlockSpec((s_all, HEAD_DIM), lambda kh, g, i: (0, kh))
    return pl.pallas_call(
        body, name=name, grid=(N_KV_HEADS, Q_PER_KV, n // tq),
        in_specs=[qspec, qspec, qspec, rowspec, rowspec, kvspec,
                  pl.BlockSpec((s_all, HEAD_DIM), lambda kh, g, i: (0, kh * (V_EXT_W // HEAD_DIM)))],
        out_specs=[qspec, kvspec, kvspec],
        out_shape=[jax.ShapeDtypeStruct((n, ATTN_W), F32),
                   jax.ShapeDtypeStruct((s_all, KV_W), F32), jax.ShapeDtypeStruct((s_all, KV_W), F32)],
        compiler_params=_params("arbitrary", "arbitrary", "arbitrary"),
    )(q, do, do_s, lse_row, delta_row, k_all, v_ext)


HALF = 512


SUB = 128


def _tap_group(w_ref, w_lanes, ext_ref, r0, lanes, r, tap_of):
    z = None
    for a in range(4):
        k = tap_of(8 * a + r)
        if 0 <= k < CONV_WIDTH:
            term = w_ref[k:k + 1, w_lanes] * ext_ref[pl.ds(r0 + 8 * a, SUB + 8), lanes]
            z = term if z is None else z + term
    return z


def _shifted(z, z_ref, r):
    if r == 0:
        return z[0:SUB]
    z_ref[...] = z
    return z_ref[pl.ds(r, SUB), :]


def _fold8(x):
    return jnp.sum(x.reshape(x.shape[0] // 8, 8, x.shape[1]), axis=0)


def _pieces(tt, h):
    return [(rh * SUB, slice(c * SUB, (c + 1) * SUB), slice(h * HALF + c * SUB, h * HALF + (c + 1) * SUB))
            for c in range(HALF // SUB) for rh in range(tt // SUB)]


def _halo_specs(tt, n, cb):
    per = tt // HALO
    last = n // HALO - 1
    return [pl.BlockSpec((HALO, HALF), lambda i: (jnp.maximum(i * per - 1, 0), cb)),
            pl.BlockSpec((tt, HALF), lambda i: (i, cb)),
            pl.BlockSpec((HALO, HALF), lambda i: (jnp.minimum((i + 1) * per, last), cb))]


def conv_fwd(p, dw_w, dw_b, ln_g, ln_b, mix, *, glu_block, name):
    n = p.shape[0]
    ch = dw_w.shape[1]
    nh = ch // HALF
    tt = _rows_tile(n, 512)
    last = n // tt - 1

    def body(*refs):
        a_refs = [refs[3 * h:3 * h + 3] for h in range(nh)]
        b_refs = [refs[3 * (nh + h):3 * (nh + h) + 3] for h in range(nh)]
        pos = 6 * nh
        zb_refs = refs[pos:pos + nh]
        pos += nh
        w_ref, bias_ref, g_ref, be_ref, _, ycv_ref, mix_ref, ext_ref, z_ref = refs[pos:pos + 9]
        i = pl.program_id(0)
        for h in range(nh):
            cs = slice(h * HALF, (h + 1) * HALF)
            ap, am, an = a_refs[h]
            bp, bm, bn = b_refs[h]
            ext_ref[0:HALO, :] = jnp.where(i > 0, ap[...] * _sigmoid(bp[...]), 0.0)
            ext_ref[HALO:HALO + tt, :] = am[...] * _sigmoid(bm[...])
            ext_ref[HALO + tt:2 * HALO + tt, :] = jnp.where(i < last, an[...] * _sigmoid(bn[...]), 0.0)
            for r0, lanes, w_lanes in _pieces(tt, h):
                acc = jnp.broadcast_to(bias_ref[:, w_lanes], (SUB, SUB))
                for r in range(8):
                    z = _tap_group(w_ref, w_lanes, ext_ref, r0, lanes, r, lambda j: j - 1)
                    acc = acc + _shifted(z, z_ref, r)
                ycv_ref[pl.ds(r0, SUB), w_lanes] = acc
        yc = ycv_ref[...]
        mu = jnp.mean(yc, axis=-1, keepdims=True)
        var = jnp.mean(jnp.square(yc - mu), axis=-1, keepdims=True)
        ln = (yc - mu) * lax.rsqrt(var + EPS) * g_ref[...] + be_ref[...]
        out = _silu(ln)
        for h in range(nh):
            cs = slice(h * HALF, (h + 1) * HALF)
            mix_ref[:, cs] = (out[:, cs] * _silu(zb_refs[h][...])).astype(BF16)

    in_specs = []
    for h in range(2 * nh):
        in_specs += _halo_specs(tt, n, glu_block + h)
    in_specs += [pl.BlockSpec((tt, HALF), functools.partial(lambda i, cb: (i, cb), cb=glu_block + 2 * nh + h))
                 for h in range(nh)]
    vec = pl.BlockSpec((1, ch), lambda i: (0, 0))
    in_specs += [pl.BlockSpec((CONV_WIDTH, ch), lambda i: (0, 0)), vec, vec, vec, ANY]
    ops = [p] * (6 * nh + nh) + [dw_w, _row(dw_b), _row(ln_g), _row(ln_b), mix]
    big = pl.BlockSpec((tt, ch), lambda i: (i, 0))
    mix_block = (mix.shape[1] - ch) // ch
    return pl.pallas_call(
        body, name=name, grid=(n // tt,), in_specs=in_specs,
        out_specs=[big, pl.BlockSpec((tt, ch), lambda i: (i, mix_block))],
        out_shape=[jax.ShapeDtypeStruct((n, ch), F32), jax.ShapeDtypeStruct(mix.shape, BF16)],
        input_output_aliases={len(ops) - 1: 1},
        scratch_shapes=[pltpu.VMEM((tt + 2 * HALO, HALF), F32), pltpu.VMEM((SUB + 8, SUB), F32)],
        compiler_params=_params("parallel"),
    )(*ops)


def conv_bwd_rows(dmix, ycv, p, ln_g, ln_b, *, mix_block, zb_block, name):
    n, ch = ycv.shape
    nh = ch // HALF
    tt = _rows_tile(n, 512)

    def body(*refs):
        dm_ref, ycv_ref = refs[:2]
        zb_refs = refs[2:2 + nh]
        g_ref, be_ref, dy_ref, dzb_ref, acc_ref = refs[2 + nh:]
        i = pl.program_id(0)

        @pl.when(i == 0)
        def _():
            acc_ref[...] = jnp.zeros_like(acc_ref)

        yc = ycv_ref[...]
        mu = jnp.mean(yc, axis=-1, keepdims=True)
        var = jnp.mean(jnp.square(yc - mu), axis=-1, keepdims=True)
        rstd = lax.rsqrt(var + EPS)
        xhat = (yc - mu) * rstd
        ln = xhat * g_ref[...] + be_ref[...]
        out = _silu(ln)
        dm = dm_ref[...]
        zb = jnp.concatenate([r[...] for r in zb_refs], axis=-1)
        dzb_ref[...] = (dm * out * _dsilu(zb)).astype(BF16)
        dln = dm * _silu(zb) * _dsilu(ln)
        acc_ref[0:1, :] += jnp.sum(dln * xhat, axis=0, keepdims=True)
        acc_ref[1:2, :] += jnp.sum(dln, axis=0, keepdims=True)
        dxhat = dln * g_ref[...]
        dy_ref[...] = rstd * (dxhat - jnp.mean(dxhat, axis=-1, keepdims=True)
                              - xhat * jnp.mean(dxhat * xhat, axis=-1, keepdims=True))

    big = pl.BlockSpec((tt, ch), lambda i: (i, 0))
    vec = pl.BlockSpec((1, ch), lambda i: (0, 0))
    in_specs = [pl.BlockSpec((tt, ch), lambda i: (i, mix_block)), big]
    in_specs += [pl.BlockSpec((tt, HALF), functools.partial(lambda i, cb: (i, cb), cb=zb_block + h)) for h in range(nh)]
    in_specs += [vec, vec]
    return pl.pallas_call(
        body, name=name, grid=(n // tt,), in_specs=in_specs,
        out_specs=[big, big, pl.BlockSpec((8, ch), lambda i: (0, 0))],
        out_shape=[jax.ShapeDtypeStruct((n, ch), F32), jax.ShapeDtypeStruct((n, ch), BF16),
                   jax.ShapeDtypeStruct((8, ch), F32)],
        compiler_params=_params("arbitrary"),
    )(dmix, ycv, *([p] * nh), _row(ln_g), _row(ln_b))


def conv_bwd_taps(dycv, p, dw_w, *, glu_block, name):
    n, ch = dycv.shape
    nh = ch // HALF
    tt = _rows_tile(n, 512)
    last = n // tt - 1

    def body(*refs):
        d_refs = [refs[3 * h:3 * h + 3] for h in range(nh)]
        a_refs = [refs[3 * (nh + h):3 * (nh + h) + 3] for h in range(nh)]
        b_refs = [refs[3 * (2 * nh + h):3 * (2 * nh + h) + 3] for h in range(nh)]
        w_ref, dglu_ref, dw_ref, db_ref, yext_ref, dext_ref, z_ref, dy_ref, dwp_ref, dbp_ref = refs[9 * nh:]
        i = pl.program_id(0)

        @pl.when(i == 0)
        def _():
            dwp_ref[...] = jnp.zeros_like(dwp_ref)
            dbp_ref[...] = jnp.zeros_like(dbp_ref)

        for h in range(nh):
            cs = slice(h * HALF, (h + 1) * HALF)
            ap, am, an = a_refs[h]
            bp, bm, bn = b_refs[h]
            dp, dm, dn = d_refs[h]
            av = am[...]
            sb = _sigmoid(bm[...])
            yext_ref[0:HALO, :] = jnp.where(i > 0, ap[...] * _sigmoid(bp[...]), 0.0)
            yext_ref[HALO:HALO + tt, :] = av * sb
            yext_ref[HALO + tt:2 * HALO + tt, :] = jnp.where(i < last, an[...] * _sigmoid(bn[...]), 0.0)
            dmain = dm[...]
            dext_ref[0:HALO, :] = jnp.where(i > 0, dp[...], 0.0)
            dext_ref[HALO:HALO + tt, :] = dmain
            dext_ref[HALO + tt:2 * HALO + tt, :] = jnp.where(i < last, dn[...], 0.0)
            for r0, lanes, w_lanes in _pieces(tt, h):
                dsub = dext_ref[pl.ds(HALO + r0, SUB), lanes]
                dy = jnp.zeros((SUB, SUB), F32)
                for r in range(8):
                    z = _tap_group(w_ref, w_lanes, dext_ref, r0, lanes, r, lambda j: CONV_WIDTH - j)
                    dy = dy + _shifted(z, z_ref, r)
                    y_r = yext_ref[pl.ds(r0 + r, SUB + 24), lanes]
                    for a in range(4):
                        k = 8 * a + r - 1
                        if k >= 0:
                            dwp_ref[8 * k:8 * k + 8, w_lanes] += _fold8(dsub * y_r[8 * a:8 * a + SUB])
                dbp_ref[:, w_lanes] += _fold8(dsub)
                dy_ref[pl.ds(r0, SUB), lanes] = dy
            dy = dy_ref[...]
            dglu_ref[:, cs] = (dy * sb).astype(BF16)
            dglu_ref[:, ch + h * HALF:ch + (h + 1) * HALF] = (dy * av * sb * (1.0 - sb)).astype(BF16)

        @pl.when(i == last)
        def _():
            dw_ref[...] = jnp.sum(dwp_ref[...].reshape(32, 8, ch), axis=1)
            db_ref[...] = jnp.broadcast_to(jnp.sum(dbp_ref[...], axis=0, keepdims=True), db_ref.shape)

    in_specs = []
    for h in range(nh):
        in_specs += _halo_specs(tt, n, h)
    for h in range(2 * nh):
        in_specs += _halo_specs(tt, n, glu_block + h)
    in_specs += [pl.BlockSpec((CONV_WIDTH, ch), lambda i: (0, 0))]
    ops = [dycv] * (3 * nh) + [p] * (6 * nh) + [dw_w]
    return pl.pallas_call(
        body, name=name, grid=(n // tt,), in_specs=in_specs,
        out_specs=[pl.BlockSpec((tt, 2 * ch), lambda i: (i, 0)), pl.BlockSpec((32, ch), lambda i: (0, 0)),
                   pl.BlockSpec((8, ch), lambda i: (0, 0))],
        out_shape=[jax.ShapeDtypeStruct((n, 2 * ch), BF16), jax.ShapeDtypeStruct((32, ch), F32),
                   jax.ShapeDtypeStruct((8, ch), F32)],
        scratch_shapes=[pltpu.VMEM((tt + 2 * HALO, HALF), F32), pltpu.VMEM((tt + 2 * HALO, HALF), F32),
                        pltpu.VMEM((SUB + 8, SUB), F32), pltpu.VMEM((tt, HALF), F32),
                        pltpu.VMEM((8 * 32, ch), F32), pltpu.VMEM((8, ch), F32)],
        compiler_params=_params("arbitrary"),
    )(*ops)


SGU_STEP_CHUNKS = 2


def _sgu_common(p_ref, g_ref, be_ref, ws_ref, bs_ref, w):
    gw = w // SGU_GROUPS
    u_pre = p_ref[:, 0:w]
    v_pre = p_ref[:, w:2 * w]
    zc = p_ref[:, 2 * w:3 * w]
    u = _gelu(u_pre)
    v = _gelu(v_pre)
    mu = jnp.mean(v, axis=-1, keepdims=True)
    var = jnp.mean(jnp.square(v - mu), axis=-1, keepdims=True)
    rstd = lax.rsqrt(var + EPS)
    vhat = (v - mu) * rstd
    vn = (vhat * g_ref[...] + be_ref[...]).astype(BF16)
    mixed = jnp.concatenate(
        [jnp.dot(ws_ref[gi].astype(BF16), vn[:, gi * gw:(gi + 1) * gw], preferred_element_type=F32)
         + bs_ref[:, gi:gi + 1] for gi in range(SGU_GROUPS)], axis=-1)
    return u_pre, v_pre, zc, u, rstd, vhat, vn, mixed


def sgu_fwd(p, ln_g, ln_b, ws, bs_t, *, name):
    n, w3 = p.shape
    w = w3 // 3

    def body(p_ref, g_ref, be_ref, ws_ref, bs_ref, m_ref):
        for c in range(SGU_STEP_CHUNKS):
            rows_c = pl.ds(c * CHUNK, CHUNK)
            _, _, zc, u, _, _, _, mixed = _sgu_common(p_ref.at[rows_c], g_ref, be_ref, ws_ref, bs_ref, w)
            m_ref[rows_c, :] = (u * mixed * _silu(zc)).astype(BF16)

    rows = SGU_STEP_CHUNKS * CHUNK
    vec = pl.BlockSpec((1, w), lambda i: (0, 0))
    return pl.pallas_call(
        body, name=name, grid=(n // rows,),
        in_specs=[pl.BlockSpec((rows, w3), lambda i: (i, 0)), vec, vec,
                  pl.BlockSpec((SGU_GROUPS, CHUNK, CHUNK), lambda i: (0, 0, 0)),
                  pl.BlockSpec((CHUNK, SGU_GROUPS), lambda i: (0, 0))],
        out_specs=pl.BlockSpec((rows, w), lambda i: (i, 0)),
        out_shape=jax.ShapeDtypeStruct((n, w), BF16),
        compiler_params=_params("parallel"),
    )(p, _row(ln_g), _row(ln_b), ws, bs_t)


def sgu_bwd(p, dm, ln_g, ln_b, ws, ws_t, bs_t, *, name):
    n, w3 = p.shape
    w = w3 // 3
    gw = w // SGU_GROUPS

    def body(p_ref, dm_ref, g_ref, be_ref, ws_ref, wst_ref, bs_ref, dp_ref, dws_ref, dbs_ref, acc_ref):
        i = pl.program_id(0)

        @pl.when(i == 0)
        def _():
            dws_ref[...] = jnp.zeros_like(dws_ref)
            dbs_ref[...] = jnp.zeros_like(dbs_ref)
            acc_ref[...] = jnp.zeros_like(acc_ref)

        for c in range(SGU_STEP_CHUNKS):
            chunk_bwd(p_ref.at[pl.ds(c * CHUNK, CHUNK)], dm_ref.at[pl.ds(c * CHUNK, CHUNK)],
                      dp_ref.at[pl.ds(c * CHUNK, CHUNK)], g_ref, be_ref, ws_ref, wst_ref, bs_ref,
                      dws_ref, dbs_ref, acc_ref)

    def chunk_bwd(p_ref, dm_ref, dp_ref, g_ref, be_ref, ws_ref, wst_ref, bs_ref, dws_ref, dbs_ref, acc_ref):
        u_pre, v_pre, zc, u, rstd, vhat, vn, mixed = _sgu_common(p_ref, g_ref, be_ref, ws_ref, bs_ref, w)
        dmv = dm_ref[...]
        um = u * mixed
        dp_ref[:, 2 * w:3 * w] = (dmv * um * _dsilu(zc)).astype(BF16)
        dum = dmv * _silu(zc)
        dp_ref[:, 0:w] = (dum * mixed * _dgelu(u_pre)).astype(BF16)
        dmixed = dum * u
        dmixed_b = dmixed.astype(BF16)
        dvn_parts = []
        for gi in range(SGU_GROUPS):
            cs = slice(gi * gw, (gi + 1) * gw)
            dws_ref[gi] += lax.dot_general(dmixed_b[:, cs], vn[:, cs], _DIMS["nt"], preferred_element_type=F32)
            dbs_ref[:, gi:gi + 1] += jnp.sum(dmixed[:, cs], axis=-1, keepdims=True)
            dvn_parts.append(jnp.dot(wst_ref[gi].astype(BF16), dmixed_b[:, cs], preferred_element_type=F32))
        dvn = jnp.concatenate(dvn_parts, axis=-1)
        acc_ref[0:1, :] += jnp.sum(dvn * vhat, axis=0, keepdims=True)
        acc_ref[1:2, :] += jnp.sum(dvn, axis=0, keepdims=True)
        dvhat = dvn * g_ref[...]
        dv = rstd * (dvhat - jnp.mean(dvhat, axis=-1, keepdims=True)
                     - vhat * jnp.mean(dvhat * vhat, axis=-1, keepdims=True))
        dp_ref[:, w:2 * w] = (dv * _dgelu(v_pre)).astype(BF16)

    rows = SGU_STEP_CHUNKS * CHUNK
    vec = pl.BlockSpec((1, w), lambda i: (0, 0))
    wspec = pl.BlockSpec((SGU_GROUPS, CHUNK, CHUNK), lambda i: (0, 0, 0))
    bspec = pl.BlockSpec((CHUNK, SGU_GROUPS), lambda i: (0, 0))
    return pl.pallas_call(
        body, name=name, grid=(n // rows,),
        in_specs=[pl.BlockSpec((rows, w3), lambda i: (i, 0)), pl.BlockSpec((rows, w), lambda i: (i, 0)),
                  vec, vec, wspec, wspec, bspec],
        out_specs=[pl.BlockSpec((rows, w3), lambda i: (i, 0)), wspec, bspec,
                   pl.BlockSpec((8, w), lambda i: (0, 0))],
        out_shape=[jax.ShapeDtypeStruct((n, w3), BF16), jax.ShapeDtypeStruct((SGU_GROUPS, CHUNK, CHUNK), F32),
                   jax.ShapeDtypeStruct((CHUNK, SGU_GROUPS), F32), jax.ShapeDtypeStruct((8, w), F32)],
        compiler_params=_params("arbitrary"),
    )(p, dm, _row(ln_g), _row(ln_b), ws, ws_t, bs_t)


def _adam_math(w, g, m, v):
    m_new = ADAM_B1 * m + (1.0 - ADAM_B1) * g
    v_new = ADAM_B2 * v + (1.0 - ADAM_B2) * (g * g)
    m_hat = m_new / (1.0 - ADAM_B1 ** ADAM_STEP)
    v_hat = v_new / (1.0 - ADAM_B2 ** ADAM_STEP)
    delta = -ADAM_LR * (m_hat / (jnp.sqrt(v_hat) + ADAM_EPS) + ADAM_WD * w)
    return delta, m_new, v_new


def adamw(w, g, m, v, *, name, slots=False, rows=512):
    r, c = w.shape
    tr = min(r, rows)
    assert r % tr == 0, (name, r, tr)

    def body(w_ref, g_ref, m_ref, v_ref, go_ref, d_ref, mo_ref, vo_ref):
        if slots:
            g = g_ref[0].astype(F32)
            for k in range(1, N_DEV):
                g = g + g_ref[k].astype(F32)
        else:
            g = g_ref[...].astype(F32)
        delta, m_new, v_new = _adam_math(w_ref[...], g, m_ref[...], v_ref[...])
        go_ref[...] = g
        d_ref[...] = delta
        mo_ref[...] = m_new
        vo_ref[...] = v_new

    spec = pl.BlockSpec((tr, c), lambda i: (i, 0))
    gspec = pl.BlockSpec((N_DEV, tr, c), lambda i: (0, i, 0)) if slots else spec
    shape = jax.ShapeDtypeStruct((r, c), F32)
    return pl.pallas_call(
        body, name=name, grid=(r // tr,),
        in_specs=[spec, gspec, spec, spec], out_specs=[spec] * 4, out_shape=[shape] * 4,
        compiler_params=_params("parallel"),
    )(w, g, m, v)


def slot_sum(slots, *, name):
    _, r, c = slots.shape

    def body(s_ref, o_ref):
        acc = s_ref[0]
        for k in range(1, N_DEV):
            acc = acc + s_ref[k]
        o_ref[...] = acc

    return pl.pallas_call(
        body, name=name, out_shape=jax.ShapeDtypeStruct((r, c), F32),
        in_specs=[VMEM_SPEC], out_specs=VMEM_SPEC,
        compiler_params=pltpu.CompilerParams(vmem_limit_bytes=VMEM_LIMIT),
    )(slots)


def _after(token, val):
    return val + token[0, 0]


def _zero_of(v):
    bits = lax.bitcast_convert_type(v, jnp.uint16 if v.dtype == BF16 else jnp.uint32)
    return jnp.where((bits | 1) == 0, 1.0, 0.0).astype(F32)


def _pad_rows(a, rows):
    return jnp.pad(a, ((0, rows - a.shape[0]), (0, 0)))


def kernel(x, c, ctx, c_ctx, ada_w, ada_b, norm_g, ev_w_in, ev_q_norm, ev_k_norm, ev_dw_w, ev_dw_b, ev_ln_g, ev_ln_b, ev_w_out, od_w_in, od_ln_g, od_ln_b, od_ws, od_bs, od_w_out, final_g, loss_target, m_c_ctx, m_ada_w, m_ada_b, m_norm_g, m_ev_w_in, m_ev_q_norm, m_ev_k_norm, m_ev_dw_w, m_ev_dw_b, m_ev_ln_g, m_ev_ln_b, m_ev_w_out, m_od_w_in, m_od_ln_g, m_od_ln_b, m_od_ws, m_od_bs, m_od_w_out, m_final_g, v_c_ctx, v_ada_w, v_ada_b, v_norm_g, v_ev_w_in, v_ev_q_norm, v_ev_k_norm, v_ev_dw_w, v_ev_dw_b, v_ev_ln_g, v_ev_ln_b, v_ev_w_out, v_od_w_in, v_od_ln_g, v_od_ln_b, v_od_ws, v_od_bs, v_od_w_out, v_final_g):
    n, d = x.shape[1], x.shape[2]
    lc = ctx.shape[1]
    ev_in = ev_w_in.shape[2] * N_DEV
    od_in = od_w_in.shape[2] * N_DEV
    conv_ch = ev_dw_w.shape[2] * N_DEV
    ada_cols = ada_w.shape[2]
    me = 4 * lax.axis_index("x") + 2 * lax.axis_index("y") + lax.axis_index("c")
    xs, tgt, ctxs = x[0], loss_target[0], ctx[0]
    za_block = (2 * KV_W + ATTN_W) // HEAD_DIM
    glu_block = (2 * KV_W + 2 * ATTN_W) // HALF
    zb_block = glu_block + 2 * conv_ch // HALF

    small = jnp.concatenate([
        jax.nn.silu(c).reshape(1, d),
        od_ln_g.reshape(1, -1), od_ln_b.reshape(1, -1)], axis=1)
    small = _pad_rows(small, 8)
    dw_rows = _pad_rows(ev_dw_w[0], 32)
    small_g, dw_g = all_gather([small, dw_rows], name="gather_small")
    sc_all = small_g[:, 0, :d]
    shard = d // N_DEV
    od_ln_g_full = small_g[:, 0, d:d + shard].reshape(d)
    od_ln_b_full = small_g[:, 0, d + shard:d + 2 * shard].reshape(d)
    dw_w_full = jnp.moveaxis(dw_g, 0, 1).reshape(32, conv_ch)[:CONV_WIDTH]
    scc = jax.nn.silu(c_ctx)
    sc16 = _pad_rows(jnp.concatenate([sc_all, scc.reshape(1, d)], axis=0), 16)

    ada_bf = ada_w.astype(BF16)
    mod_loc = [matmul(sc16, ada_bf[l], mode="nn", tm=16, tn=ada_cols, tk=d, out_dtype=F32, name=f"ada_mod{l}")
               for l in range(2)]
    (mod_g,) = all_gather([jnp.stack(mod_loc)], name="gather_mod")
    mod_all = jnp.moveaxis(mod_g, 0, 2).reshape(2, 16, N_DEV * ada_cols) + ada_b[:, None, :]
    mod_me = lax.dynamic_index_in_dim(mod_all, me, axis=1, keepdims=False)
    shift = [mod_me[l, :d] for l in range(2)]
    scale = [mod_me[l, d:2 * d] for l in range(2)]
    gate = [mod_me[l, 2 * d:] for l in range(2)]
    shift_c, scale_c = mod_all[0, 8, :d], mod_all[0, 8, d:2 * d]

    h_w_in0, near_token = staged_gather_start(1, (ev_w_in[0] + _zero_of(mod_g[0, 0, 0, 0])).astype(BF16), None,
                                              name="gather_w_in0_chips_start")

    cexp, sexp = rope_tables(n)
    h0 = norm_mod_fwd(xs, norm_g[0], _after(near_token, shift[0]), scale[0], name="norm_mod_fwd0")
    hc = norm_mod_fwd(ctxs, norm_g[0], shift_c, scale_c, name="norm_mod_fwd_ctx")

    wi0_shard, wi0_land = staged_gather_wait(1, h_w_in0, h0, name="gather_w_in0_chips_wait")
    h_w_in0, far_token = staged_gather_start(2, wi0_shard, wi0_land, name="gather_w_in0_cores_start")
    wi0_shard, wi0_land = staged_gather_wait(2, h_w_in0, far_token, name="gather_w_in0_cores_wait")
    wi0_g = lax.dynamic_update_slice_in_dim(wi0_land, wi0_shard[None], me, axis=0)
    wi0 = jnp.moveaxis(wi0_g, 0, 1).reshape(d, ev_in)
    later = [w[0].astype(BF16) for w in (ev_w_out, od_w_in, od_w_out)]
    (h_wo0, h_wi1, h_wo1), w_token = exchange_start(later, scatter=False, name="gather_rest_start", after=wi0_g)

    def landed(handle, after, name):
        own, land = exchange_wait(handle, after, scatter=False, name=name)
        return lax.dynamic_update_slice_in_dim(land, own[None], me, axis=0)

    p0 = matmul(h0, wi0, mode="nn", tm=1024, tn=ev_in // 4, tk=d, out_dtype=F32, name="proj_in0", after=w_token)
    pc = matmul(hc, wi0, mode="nn", tm=lc, tn=2 * KV_W, tk=d, out_dtype=F32, n_out=2 * KV_W, name="proj_in_ctx")
    q_r, k_all, v_ext = qkv_prep_fwd(p0, ev_q_norm[0], ev_k_norm[0], cexp, sexp, latent=True, name="qkv_prep",
                                     rows_all=n + lc)
    k_all, v_ext = qkv_prep_fwd(pc, None, ev_k_norm[0], None, None, latent=False, name="kv_prep_ctx",
                                rows_all=n + lc, k_all=k_all, v_ext=v_ext)
    o_attn, mix_a, lse = flash_fwd(q_r, k_all, v_ext, p0, za_block=za_block, mix_width=ATTN_W + conv_ch,
                                   name="flash_fwd")
    ycv, mix0 = conv_fwd(p0, dw_w_full, ev_dw_b[0], ev_ln_g[0], ev_ln_b[0], mix_a, glu_block=glu_block,
                         name="conv_fwd")
    wo0 = landed(h_wo0, ycv, "gather_w_out0_wait").reshape(-1, d)
    x1, r0, h1 = matmul(mix0, wo0, mode="nn", tm=512, tn=d, tk=mix0.shape[1], out_dtype=F32, name="proj_out0",
                        res=xs, gate=_row(gate[0]), norm=(norm_g[1], shift[1], scale[1]))

    wi1 =jnp.moveaxis(landed(h_wi1, h1, "gather_w_in1_wait"), 0, 1).reshape(d, od_in)
    p1 = matmul(h1, wi1, mode="nn", tm=1024, tn=od_in // 4, tk=d, out_dtype=F32, name="proj_in1")
    ws_bf = od_ws[0]
    bs_t = od_bs[0].T
    m1 = sgu_fwd(p1, od_ln_g_full, od_ln_b_full, ws_bf, bs_t, name="sgu_fwd")
    wo1 = landed(h_wo1, m1, "gather_w_out1_wait").reshape(-1, d)
    dx2, dr1, acc_final, loss_tile = proj_out_loss(m1, wo1, x1, gate[1], tgt, final_g, name="proj_out1_loss")

    dm1 = matmul(dr1, wo1, mode="nt", tm=1024, tn=1024, tk=d, out_dtype=F32, name="d_mix1")
    dwo1 = matmul(m1, dr1, mode="tn", tm=1024, tn=1024, tk=2048, out_dtype=BF16, name="d_wout1")
    dp1, dws, dbs_t, acc_sgu = sgu_bwd(p1, dm1, od_ln_g_full, od_ln_b_full, ws_bf, jnp.swapaxes(ws_bf, 1, 2), bs_t,
                                       name="sgu_bwd")
    dwi1_s = matmul(h1, dp1, mode="tn", tm=2048, tn=od_in // N_DEV, tk=2048, out_dtype=BF16, name="d_win1",
                    split_out=True)
    dwo1_s = dwo1.reshape(N_DEV, -1, d)
    (h_gi1, h_go1), g1_token = exchange_start([dwi1_s, dwo1_s], scatter=True, name="grads1_start")
    dh1 = matmul(dp1, wi1, mode="nt", tm=1024, tn=512, tk=od_in, out_dtype=F32, name="d_h1")
    dx1, dr0, acc_norm1 = norm_mod_bwd(x1, dh1, dx2, norm_g[1], _after(g1_token, scale[1]), name="norm_mod_bwd1",
                                       branch=(r0, gate[0]))

    dmix0 = matmul(dr0, wo0, mode="nt", tm=1024, tn=1024, tk=d, out_dtype=F32, name="d_mix0")
    dwo0 = matmul(mix0, dr0, mode="tn", tm=1024, tn=1024, tk=2048, out_dtype=BF16, name="d_wout0")
    do_attn, do_s, dza, delta = attn_gate_bwd(dmix0, o_attn, p0, za_block=za_block, name="attn_gate_bwd")
    dycv, dzb, acc_ln = conv_bwd_rows(dmix0, ycv, p0, ev_ln_g[0], ev_ln_b[0], mix_block=ATTN_W // conv_ch,
                                      zb_block=zb_block, name="conv_bwd_rows")
    dglu, ddw_w, acc_dwb = conv_bwd_taps(dycv, p0, dw_w_full, glu_block=glu_block, name="conv_bwd_taps")
    dq_r, dk_all, dv_all = flash_bwd(q_r, do_attn, do_s, lse, delta, k_all, v_ext, name="flash_bwd")
    dkvq, acc_qk = qkv_prep_bwd(p0, dq_r, dk_all, dv_all, ev_q_norm[0], ev_k_norm[0], cexp, sexp,
                                latent=True, name="qkv_prep_bwd")
    dpc, acc_kc = qkv_prep_bwd(pc, None, dk_all, dv_all, None, ev_k_norm[0], None, None,
                               latent=False, name="kv_prep_ctx_bwd")
    dp0 = jnp.concatenate([dkvq, dza, dglu, dzb], axis=1)
    dwi0 = matmul(h0, dp0, mode="tn", tm=1024, tn=ev_in // 4, tk=2048, out_dtype=F32, name="d_win0")
    dwi0 = matmul(hc, dpc, mode="tn", tm=512, tn=2 * KV_W, tk=lc, out_dtype=F32, name="d_win0_ctx", add=dwi0)
    dwi0_s = jnp.moveaxis(dwi0.astype(BF16).reshape(d, N_DEV, ev_in // N_DEV), 1, 0)
    dwo0_s = dwo0.reshape(N_DEV, -1, d)
    (h_gi0, h_go0), g0_token = exchange_start([dwi0_s, dwo0_s], scatter=True, name="grads0_start")
    dh0 = matmul(dp0, wi0, mode="nt", tm=1024, tn=512, tk=ev_in, out_dtype=F32, name="d_h0", after=g0_token)
    dhc = matmul(dpc, wi0, mode="nt", tm=lc, tn=512, tk=2 * KV_W, out_dtype=F32, name="d_h_ctx")
    grad_x, acc_norm0 = norm_mod_bwd(xs, dh0, dx1, norm_g[0], _after(g0_token, scale[0]), name="norm_mod_bwd0")
    _, acc_normc = norm_mod_bwd(ctxs, dhc, None, norm_g[0], scale_c, name="norm_mod_bwd_ctx")

    zeros_d = jnp.zeros((d,), F32)
    dmod0 = jnp.stack([acc_norm0[0], acc_norm0[1], acc_norm1[GATE_ROW]])
    dmod1 = jnp.stack([acc_norm1[0], acc_norm1[1], acc_final[GATE_ROW]])
    dmodc = jnp.stack([acc_normc[0], acc_normc[1]])
    half_pad = jnp.zeros((d - 2 * conv_ch,), F32) if d > 2 * conv_ch else jnp.zeros((0,), F32)
    row_a = jnp.concatenate([acc_dwb[0], acc_ln[0], half_pad])
    row_b = jnp.concatenate([acc_ln[1], acc_qk[0], acc_qk[1] + acc_kc[1],
                             jnp.zeros((d - conv_ch - 2 * HEAD_DIM,), F32)])
    row_c = jnp.concatenate([dbs_t.T.reshape(-1), jnp.zeros((d - SGU_GROUPS * CHUNK,), F32)])
    row_loss = jnp.concatenate([loss_tile[0, :1], jnp.zeros((d - 1,), F32)])
    pack = jnp.concatenate([
        dmod0, dmod1, dmodc,
        (acc_norm0[2] + acc_normc[2])[None], acc_norm1[2][None],
        acc_final[0][None],
        acc_sgu[0][None], acc_sgu[1][None],
        row_a[None], row_b[None], row_c[None], row_loss[None],
        ddw_w.reshape(-1, d),
        dws.reshape(-1, d),
    ], axis=0)
    n_rows = pack.shape[0]
    pack = _pad_rows(pack, -(-n_rows // 8) * 8)
    (h_pack,), pack_token = exchange_start([pack], scatter=False, name="small_grads_start")

    def summands(handle, after, name):
        mine, land = exchange_wait(handle, after, scatter=True, name=name)
        own = lax.dynamic_index_in_dim(mine, me, axis=0, keepdims=True)
        return lax.dynamic_update_slice_in_dim(land, own, me, axis=0)

    out = {}

    def upd(key, w, g, m, v, slots=False, rows=512):
        shp = w.shape
        w2 = w.reshape(-1, shp[-1])
        g2 = g.reshape((N_DEV, -1, shp[-1])) if slots else g.reshape(-1, shp[-1])
        res = adamw(w2, g2, m.reshape(w2.shape), v.reshape(w2.shape), name="adamw_" + key, slots=slots, rows=rows)
        out[key] = tuple(r.reshape(shp) for r in res)

    gi1 = summands(h_gi1, pack_token, "grads_w_in1_wait")
    upd("od_w_in", od_w_in, gi1, m_od_w_in, v_od_w_in, slots=True, rows=256)
    go1 = summands(h_go1, out["od_w_in"][1], "grads_w_out1_wait")
    upd("od_w_out", od_w_out, go1, m_od_w_out, v_od_w_out, slots=True, rows=256)
    go0 = summands(h_go0, out["od_w_out"][1], "grads_w_out0_wait")
    upd("ev_w_out", ev_w_out, go0, m_ev_w_out, v_ev_w_out, slots=True, rows=256)
    gi0 = summands(h_gi0, out["ev_w_out"][1], "grads_w_in0_wait")
    upd("ev_w_in", ev_w_in, gi0, m_ev_w_in, v_ev_w_in, slots=True, rows=256)

    pack_own, pack_land = exchange_wait(h_pack, out["ev_w_in"][1], scatter=False, name="small_grads_wait")
    pack_g = lax.dynamic_update_slice_in_dim(pack_land, pack_own[None], me, axis=0)
    gsum = slot_sum(pack_g, name="sum_small_grads")
    loss = gsum[16, 0]
    dw_rows_n = 32 * conv_ch // d
    g_dw_w = gsum[17:17 + dw_rows_n].reshape(32, conv_ch)[:CONV_WIDTH]
    g_od_ws = gsum[17 + dw_rows_n:17 + dw_rows_n + SGU_GROUPS * CHUNK * CHUNK // d].reshape(od_ws.shape)

    dmodc_sum = jnp.concatenate([gsum[6], gsum[7], zeros_d])
    col0 = me * ada_cols
    dm_cols = []
    for l in range(2):
        rows = pack_g[:, 3 * l:3 * l + 3, :].reshape(N_DEV, 3 * d)
        extra = dmodc_sum[None] if l == 0 else jnp.zeros((1, 3 * d), F32)
        full = _pad_rows(jnp.concatenate([rows, extra], axis=0), 16)
        dm_cols.append(lax.dynamic_slice_in_dim(full, col0, ada_cols, axis=1))
    g_ada_w = jnp.stack([matmul(sc16, dm_cols[l], mode="tn", tm=512, tn=ada_cols, tk=16, out_dtype=F32,
                                name=f"d_ada_w{l}") for l in range(2)])
    dsc = matmul(dm_cols[0], ada_bf[0], mode="nt", tm=16, tn=512, tk=ada_cols, out_dtype=F32, name="d_scc")
    (_, dscc_sum) = all_gather([dsc[8:16]], name="gather_dscc", sum_out=True)
    sg = jax.nn.sigmoid(c_ctx)
    g_c_ctx = dscc_sum[0] * (sg * (1.0 + c_ctx * (1.0 - sg)))
    g_ada_b = jnp.stack([gsum[0:3].reshape(-1) + dmodc_sum, gsum[3:6].reshape(-1)])

    upd("ada_w", ada_w, g_ada_w, m_ada_w, v_ada_w)

    def my_shard(full, size):
        return lax.dynamic_slice_in_dim(full, me * size, size, axis=full.ndim - 1)

    small_items = [
        ("c_ctx", c_ctx, g_c_ctx, m_c_ctx, v_c_ctx),
        ("ada_b", ada_b, g_ada_b, m_ada_b, v_ada_b),
        ("norm_g", norm_g, gsum[8:10], m_norm_g, v_norm_g),
        ("ev_q_norm", ev_q_norm, gsum[14, conv_ch:conv_ch + HEAD_DIM], m_ev_q_norm, v_ev_q_norm),
        ("ev_k_norm", ev_k_norm, gsum[14, conv_ch + HEAD_DIM:conv_ch + 2 * HEAD_DIM], m_ev_k_norm, v_ev_k_norm),
        ("ev_dw_w", ev_dw_w, my_shard(g_dw_w, conv_ch // N_DEV), m_ev_dw_w, v_ev_dw_w),
        ("ev_dw_b", ev_dw_b, gsum[13, :conv_ch], m_ev_dw_b, v_ev_dw_b),
        ("ev_ln_g", ev_ln_g, gsum[13, conv_ch:2 * conv_ch], m_ev_ln_g, v_ev_ln_g),
        ("ev_ln_b", ev_ln_b, gsum[14, :conv_ch], m_ev_ln_b, v_ev_ln_b),
        ("od_ln_g", od_ln_g, my_shard(gsum[11], shard), m_od_ln_g, v_od_ln_g),
        ("od_ln_b", od_ln_b, my_shard(gsum[12], shard), m_od_ln_b, v_od_ln_b),
        ("od_ws", od_ws, g_od_ws, m_od_ws, v_od_ws),
        ("od_bs", od_bs, gsum[15, :SGU_GROUPS * CHUNK], m_od_bs, v_od_bs),
        ("final_g", final_g, gsum[10], m_final_g, v_final_g),
    ]
    sizes = [it[1].size for it in small_items]
    total = sum(sizes)
    lanes = 1024
    prow = -(-total // lanes)
    prow = -(-prow // 8) * 8

    def pack_small(idx):
        flat = jnp.concatenate([it[idx].reshape(-1).astype(F32) for it in small_items])
        return jnp.pad(flat, (0, prow * lanes - total)).reshape(prow, lanes)

    sres = adamw(pack_small(1), pack_small(2), pack_small(3), pack_small(4), name="adamw_small", rows=prow)
    off = 0
    for it, size in zip(small_items, sizes):
        out[it[0]] = tuple(r.reshape(-1)[off:off + size].reshape(it[1].shape) for r in sres)
        off += size

    names = ['c_ctx', 'ada_w', 'ada_b', 'norm_g', 'ev_w_in', 'ev_q_norm', 'ev_k_norm', 'ev_dw_w', 'ev_dw_b',
             'ev_ln_g', 'ev_ln_b', 'ev_w_out', 'od_w_in', 'od_ln_g', 'od_ln_b', 'od_ws', 'od_bs', 'od_w_out',
             'final_g']
    return (loss, grad_x[None], *[out[k][0] for k in names], *[out[k][1] for k in names],
            *[out[k][2] for k in names], *[out[k][3] for k in names])
```
